```python
import jax, jax.numpy as jnp
from jax import lax
import numpy as np

D_MODEL = 1024
BATCH = 8
SEQ = 8192
DEPTH = 2

N_MIXERS = 2
HEAD_DIM = 64
N_Q_HEADS = D_MODEL // HEAD_DIM
N_KV_HEADS = N_Q_HEADS // 4
GROUP = N_Q_HEADS // N_KV_HEADS
Q_DIM = N_Q_HEADS * HEAD_DIM
KV_DIM = N_KV_HEADS * HEAD_DIM
QKV_DIM = Q_DIM + 2 * KV_DIM
WINDOW = 128
BLOCK = 128
ROT_DIM = HEAD_DIM // 4
ROPE_THETA = 500000.0
NEG_INF = -1e30
HGRN_DK = 128
HGRN_HEADS = D_MODEL // HGRN_DK
HGRN_DV = D_MODEL // HGRN_HEADS
CHUNK = 64
D_FF = 4 * D_MODEL
NORM_EPS = 1e-5
N_ATTN_LAYERS = (DEPTH + N_MIXERS - 1) // N_MIXERS
N_HGRN_LAYERS = DEPTH // N_MIXERS

kernel_name = "hybrid_swa_sink_hgrn2_sqrelu"


def rmsnorm(x, gain):
    xf = x.astype(jnp.float32)
    y = xf * lax.rsqrt(jnp.mean(jnp.square(xf), axis=-1, keepdims=True) + NORM_EPS)
    return (y * gain.astype(jnp.float32)).astype(x.dtype)


def rotary_tables(positions):
    inv_freq = ROPE_THETA ** (-jnp.arange(0, ROT_DIM, 2, dtype=jnp.float32) / ROT_DIM)
    ang = positions.astype(jnp.float32)[..., None] * inv_freq
    return jnp.cos(ang), jnp.sin(ang)


def apply_partial_rotary(x, cos, sin):
    half = ROT_DIM // 2
    cos = cos.astype(x.dtype)
    sin = sin.astype(x.dtype)
    x1 = x[..., :half]
    x2 = x[..., half:ROT_DIM]
    return jnp.concatenate([x1 * cos - x2 * sin, x2 * cos + x1 * sin, x[..., ROT_DIM:]], axis=-1)


def sliding_window_attention(h, positions, w_qkv, b_qkv, sinks, w_o):
    bsz, seq, _ = h.shape
    nb = seq // BLOCK
    qkv = h @ w_qkv + b_qkv
    q = qkv[..., :Q_DIM].reshape(bsz, seq, N_KV_HEADS, GROUP, HEAD_DIM)
    k = qkv[..., Q_DIM:Q_DIM + KV_DIM].reshape(bsz, seq, N_KV_HEADS, HEAD_DIM)
    v = qkv[..., Q_DIM + KV_DIM:].reshape(bsz, seq, N_KV_HEADS, HEAD_DIM)
    cos, sin = rotary_tables(positions)
    q = apply_partial_rotary(q, cos[:, :, None, None, :], sin[:, :, None, None, :])
    k = apply_partial_rotary(k, cos[:, :, None, :], sin[:, :, None, :])
    qb = q.reshape(bsz, nb, BLOCK, N_KV_HEADS, GROUP, HEAD_DIM)

    def band(t):
        tp = jnp.pad(t, ((0, 0), (BLOCK, 0), (0, 0), (0, 0)))
        tp = tp.reshape(bsz, nb + 1, BLOCK, N_KV_HEADS, HEAD_DIM)
        return jnp.concatenate([tp[:, :-1], tp[:, 1:]], axis=2)

    kb = band(k)
    vb = band(v)
    scores = jnp.einsum('bnqkgd,bnskd->bnkgqs', qb, kb).astype(jnp.float32) * (HEAD_DIM ** -0.5)
    qi = jnp.arange(BLOCK)[:, None]
    kj = jnp.arange(2 * BLOCK)[None, :]
    delta = qi + BLOCK - kj
    key_pos = jnp.arange(nb)[:, None, None] * BLOCK + kj[None] - BLOCK
    mask = (delta >= 0) & (delta < WINDOW) & (key_pos >= 0)
    scores = jnp.where(mask[None, :, None, None], scores, NEG_INF)
    sink = sinks.astype(jnp.float32).reshape(1, 1, N_KV_HEADS, GROUP, 1, 1)
    m = jnp.maximum(jnp.max(scores, axis=-1, keepdims=True), sink)
    e = jnp.exp(scores - m)
    probs = e / (jnp.sum(e, axis=-1, keepdims=True) + jnp.exp(sink - m))
    out = jnp.einsum('bnkgqs,bnskd->bnqkgd', probs.astype(vb.dtype), vb)
    return out.reshape(bsz, seq, Q_DIM) @ w_o


def hgrn2_recurrence(h, lower_bound, w_in, g_norm, w_o):
    bsz, seq, _ = h.shape
    nc = seq // CHUNK
    q, f, i, g = jnp.split(h @ w_in, 4, axis=-1)
    q = jax.nn.silu(q.astype(jnp.float32))
    lb = lower_bound.astype(jnp.float32)
    forget = lb + (1.0 - lb) * jax.nn.sigmoid(f.astype(jnp.float32))
    k = 1.0 - forget
    log_f = jnp.log(forget)

    def to_chunks(t, d):
        return t.reshape(bsz, nc, CHUNK, HGRN_HEADS, d).transpose(1, 0, 3, 2, 4)

    xs = (to_chunks(q, HGRN_DK), to_chunks(k, HGRN_DK),
          to_chunks(i.astype(jnp.float32), HGRN_DV), to_chunks(log_f, HGRN_DK))
    causal = jnp.tril(jnp.ones((CHUNK, CHUNK), dtype=bool))

    def chunk_step(state, inp):
        qc, kc, vc, lc = inp
        b = jnp.cumsum(lc, axis=2)
        o_inter = jnp.einsum('bhtd,bhde->bhte', qc * jnp.exp(b), state)
        diff = b[:, :, :, None, :] - b[:, :, None, :, :]
        decay = jnp.exp(jnp.where(causal[None, None, :, :, None], diff, -jnp.inf))
        scores = jnp.einsum('bhtd,bhtsd,bhsd->bhts', qc, decay, kc)
        o_intra = jnp.einsum('bhts,bhse->bhte', scores, vc)
        b_last = b[:, :, -1]
        new_state = jnp.exp(b_last)[..., None] * state + jnp.einsum(
            'bhsd,bhse->bhde', kc * jnp.exp(b_last[:, :, None, :] - b), vc)
        return new_state, o_inter + o_intra

    state0 = jnp.zeros((bsz, HGRN_HEADS, HGRN_DK, HGRN_DV), jnp.float32)
    _, o = lax.scan(chunk_step, state0, xs)
    o = o.transpose(1, 0, 3, 2, 4).reshape(bsz, seq, HGRN_HEADS * HGRN_DV)
    o = rmsnorm(o, g_norm) * jax.nn.silu(g.astype(jnp.float32))
    return o.astype(h.dtype) @ w_o


def sqrelu_mlp(h, w_up, w_down):
    return jnp.square(jax.nn.relu(h @ w_up)) @ w_down


def _fwd_setup_inputs(seed: int = 0) -> dict:
    key = jax.random.key(seed)
    ks = jax.random.split(key, 16)
    f32 = jnp.float32

    def nrm(k, shape, scale):
        return jax.random.normal(k, shape, f32) * scale

    x = jax.random.normal(ks[0], (BATCH, SEQ, D_MODEL), f32)
    positions = jnp.broadcast_to(jnp.arange(SEQ, dtype=jnp.int32)[None, :], (BATCH, SEQ))
    return {
        "x": x,
        "positions": positions,
        "mix_norm": 1.0 + nrm(ks[1], (DEPTH, D_MODEL), 0.02),
        "mlp_norm": 1.0 + nrm(ks[2], (DEPTH, D_MODEL), 0.02),
        "final_norm": 1.0 + nrm(ks[3], (D_MODEL,), 0.02),
        "attn_w_qkv": nrm(ks[4], (N_ATTN_LAYERS, D_MODEL, QKV_DIM), D_MODEL ** -0.5),
        "attn_b_qkv": nrm(ks[5], (N_ATTN_LAYERS, QKV_DIM), 0.02),
        "attn_sinks": nrm(ks[6], (N_ATTN_LAYERS, N_Q_HEADS), 0.5),
        "attn_w_o": nrm(ks[7], (N_ATTN_LAYERS, Q_DIM, D_MODEL), Q_DIM ** -0.5),
        "hgrn_w_in": nrm(ks[8], (N_HGRN_LAYERS, D_MODEL, 4 * D_MODEL), D_MODEL ** -0.5),
        "hgrn_g_norm": 1.0 + nrm(ks[9], (N_HGRN_LAYERS, HGRN_HEADS * HGRN_DV), 0.02),
        "hgrn_w_o": nrm(ks[10], (N_HGRN_LAYERS, HGRN_HEADS * HGRN_DV, D_MODEL), D_MODEL ** -0.5),
        "hgrn_lower_bounds": nrm(ks[11], (DEPTH, HGRN_HEADS * HGRN_DK), 0.1),
        "mlp_w_up": nrm(ks[12], (DEPTH, D_MODEL, D_FF), D_MODEL ** -0.5),
        "mlp_w_down": nrm(ks[13], (DEPTH, D_FF, D_MODEL), D_FF ** -0.5),
    }


def _fwd_reference(x, positions, mix_norm, mlp_norm, final_norm, attn_w_qkv, attn_b_qkv, attn_sinks,
              attn_w_o, hgrn_w_in, hgrn_g_norm, hgrn_w_o, hgrn_lower_bounds, mlp_w_up, mlp_w_down):
    lbs = jnp.cumsum(jax.nn.softmax(hgrn_lower_bounds.astype(jnp.float32), axis=0), axis=0)
    lbs = lbs - lbs[0:1]
    for layer in range(DEPTH):
        j = layer // N_MIXERS
        h = rmsnorm(x, mix_norm[layer])
        if layer % N_MIXERS == 0:
            y = sliding_window_attention(h, positions, attn_w_qkv[j], attn_b_qkv[j],
                                         attn_sinks[j], attn_w_o[j])
        else:
            y = hgrn2_recurrence(h, lbs[layer], hgrn_w_in[j], hgrn_g_norm[j], hgrn_w_o[j])
        x = x + y.astype(x.dtype)
        h = rmsnorm(x, mlp_norm[layer])
        x = x + sqrelu_mlp(h, mlp_w_up[layer], mlp_w_down[layer]).astype(x.dtype)
    return rmsnorm(x, final_norm)


import jax as _jax
import jax.numpy as _jnp

TWIN_FORMAT = 'train_step'
FWD_PARAMS = ['x', 'positions', 'mix_norm', 'mlp_norm', 'final_norm', 'attn_w_qkv', 'attn_b_qkv', 'attn_sinks', 'attn_w_o', 'hgrn_w_in', 'hgrn_g_norm', 'hgrn_w_o', 'hgrn_lower_bounds', 'mlp_w_up', 'mlp_w_down']
TWIN_WEIGHTS = ['mix_norm', 'mlp_norm', 'final_norm', 'attn_w_qkv', 'attn_b_qkv', 'attn_sinks', 'attn_w_o', 'hgrn_w_in', 'hgrn_g_norm', 'hgrn_w_o', 'hgrn_lower_bounds', 'mlp_w_up', 'mlp_w_down']
TWIN_DIFF_INPUT = 'x'
TWIN_INPUTS = ['x', 'positions', 'mix_norm', 'mlp_norm', 'final_norm', 'attn_w_qkv', 'attn_b_qkv', 'attn_sinks', 'attn_w_o', 'hgrn_w_in', 'hgrn_g_norm', 'hgrn_w_o', 'hgrn_lower_bounds', 'mlp_w_up', 'mlp_w_down', 'loss_target', 'm_mix_norm', 'm_mlp_norm', 'm_final_norm', 'm_attn_w_qkv', 'm_attn_b_qkv', 'm_attn_sinks', 'm_attn_w_o', 'm_hgrn_w_in', 'm_hgrn_g_norm', 'm_hgrn_w_o', 'm_hgrn_lower_bounds', 'm_mlp_w_up', 'm_mlp_w_down', 'v_mix_norm', 'v_mlp_norm', 'v_final_norm', 'v_attn_w_qkv', 'v_attn_b_qkv', 'v_attn_sinks', 'v_attn_w_o', 'v_hgrn_w_in', 'v_hgrn_g_norm', 'v_hgrn_w_o', 'v_hgrn_lower_bounds', 'v_mlp_w_up', 'v_mlp_w_down']
TWIN_OUTPUTS = ['loss', 'grad_x', 'grad_mix_norm', 'grad_mlp_norm', 'grad_final_norm', 'grad_attn_w_qkv', 'grad_attn_b_qkv', 'grad_attn_sinks', 'grad_attn_w_o', 'grad_hgrn_w_in', 'grad_hgrn_g_norm', 'grad_hgrn_w_o', 'grad_hgrn_lower_bounds', 'grad_mlp_w_up', 'grad_mlp_w_down', 'delta_mix_norm', 'delta_mlp_norm', 'delta_final_norm', 'delta_attn_w_qkv', 'delta_attn_b_qkv', 'delta_attn_sinks', 'delta_attn_w_o', 'delta_hgrn_w_in', 'delta_hgrn_g_norm', 'delta_hgrn_w_o', 'delta_hgrn_lower_bounds', 'delta_mlp_w_up', 'delta_mlp_w_down', 'new_m_mix_norm', 'new_m_mlp_norm', 'new_m_final_norm', 'new_m_attn_w_qkv', 'new_m_attn_b_qkv', 'new_m_attn_sinks', 'new_m_attn_w_o', 'new_m_hgrn_w_in', 'new_m_hgrn_g_norm', 'new_m_hgrn_w_o', 'new_m_hgrn_lower_bounds', 'new_m_mlp_w_up', 'new_m_mlp_w_down', 'new_v_mix_norm', 'new_v_mlp_norm', 'new_v_final_norm', 'new_v_attn_w_qkv', 'new_v_attn_b_qkv', 'new_v_attn_sinks', 'new_v_attn_w_o', 'new_v_hgrn_w_in', 'new_v_hgrn_g_norm', 'new_v_hgrn_w_o', 'new_v_hgrn_lower_bounds', 'new_v_mlp_w_up', 'new_v_mlp_w_down']
TWIN_LEAF_KINDS = {'loss': 'loss', 'grad_x': 'grad_x', 'grad_mix_norm': 'grad_w', 'grad_mlp_norm': 'grad_w', 'grad_final_norm': 'grad_w', 'grad_attn_w_qkv': 'grad_w', 'grad_attn_b_qkv': 'grad_w', 'grad_attn_sinks': 'grad_w', 'grad_attn_w_o': 'grad_w', 'grad_hgrn_w_in': 'grad_w', 'grad_hgrn_g_norm': 'grad_w', 'grad_hgrn_w_o': 'grad_w', 'grad_hgrn_lower_bounds': 'grad_w', 'grad_mlp_w_up': 'grad_w', 'grad_mlp_w_down': 'grad_w', 'delta_mix_norm': 'delta_w', 'delta_mlp_norm': 'delta_w', 'delta_final_norm': 'delta_w', 'delta_attn_w_qkv': 'delta_w', 'delta_attn_b_qkv': 'delta_w', 'delta_attn_sinks': 'delta_w', 'delta_attn_w_o': 'delta_w', 'delta_hgrn_w_in': 'delta_w', 'delta_hgrn_g_norm': 'delta_w', 'delta_hgrn_w_o': 'delta_w', 'delta_hgrn_lower_bounds': 'delta_w', 'delta_mlp_w_up': 'delta_w', 'delta_mlp_w_down': 'delta_w', 'new_m_mix_norm': 'new_m', 'new_m_mlp_norm': 'new_m', 'new_m_final_norm': 'new_m', 'new_m_attn_w_qkv': 'new_m', 'new_m_attn_b_qkv': 'new_m', 'new_m_attn_sinks': 'new_m', 'new_m_attn_w_o': 'new_m', 'new_m_hgrn_w_in': 'new_m', 'new_m_hgrn_g_norm': 'new_m', 'new_m_hgrn_w_o': 'new_m', 'new_m_hgrn_lower_bounds': 'new_m', 'new_m_mlp_w_up': 'new_m', 'new_m_mlp_w_down': 'new_m', 'new_v_mix_norm': 'new_v', 'new_v_mlp_norm': 'new_v', 'new_v_final_norm': 'new_v', 'new_v_attn_w_qkv': 'new_v', 'new_v_attn_b_qkv': 'new_v', 'new_v_attn_sinks': 'new_v', 'new_v_attn_w_o': 'new_v', 'new_v_hgrn_w_in': 'new_v', 'new_v_hgrn_g_norm': 'new_v', 'new_v_hgrn_w_o': 'new_v', 'new_v_hgrn_lower_bounds': 'new_v', 'new_v_mlp_w_up': 'new_v', 'new_v_mlp_w_down': 'new_v'}


def _forward(args):
    return _fwd_reference(*[args[k] for k in FWD_PARAMS])


def _output_shape():
    def fwd():
        inp = _fwd_setup_inputs(0)
        return _fwd_reference(*[inp[k] for k in FWD_PARAMS])
    out = _jax.eval_shape(fwd)
    return out.shape, out.dtype

N_MICROBATCH = 1
ADAM_LR = 0.001
ADAM_B1 = 0.9
ADAM_B2 = 0.999
ADAM_EPS = 1e-08
ADAM_WD = 0.01
ADAM_STEP = 10
PER_EXAMPLE_BATCH_AXIS = {'x': 0, 'positions': 0, 'loss_target': 0}
SHARED_INPUTS = []
_WEIGHT_DTYPES = {'mix_norm': _jnp.float32, 'mlp_norm': _jnp.float32, 'final_norm': _jnp.float32, 'attn_w_qkv': _jnp.float32, 'attn_b_qkv': _jnp.float32, 'attn_sinks': _jnp.float32, 'attn_w_o': _jnp.float32, 'hgrn_w_in': _jnp.float32, 'hgrn_g_norm': _jnp.float32, 'hgrn_w_o': _jnp.float32, 'hgrn_lower_bounds': _jnp.float32, 'mlp_w_up': _jnp.float32, 'mlp_w_down': _jnp.float32}
MOMENT_SCALE = {'mix_norm': 1.169088e-01, 'mlp_norm': 2.232795e-01, 'final_norm': 6.500432e+01, 'attn_w_qkv': 7.839965e-02, 'attn_b_qkv': 5.039838e-01, 'attn_sinks': 5.377911e-02, 'attn_w_o': 6.180464e-02, 'hgrn_w_in': 7.497423e-02, 'hgrn_g_norm': 1.121976e-01, 'hgrn_w_o': 1.024892e-01, 'hgrn_lower_bounds': 8.471196e-03, 'mlp_w_up': 1.086981e-01, 'mlp_w_down': 2.332787e-01}


def _to_microbatches(a, axis):
    t = _jnp.moveaxis(a, axis, 0)
    t = t.reshape((N_MICROBATCH, t.shape[0] // N_MICROBATCH) + t.shape[1:])
    return _jnp.moveaxis(t, 1, axis + 1)


def setup_inputs(seed: int = 0) -> dict:
    inp = _fwd_setup_inputs(seed)
    key = _jax.random.fold_in(_jax.random.key(seed), 7919)
    shape, _ = _output_shape()
    out = dict(inp)
    out["loss_target"] = _jax.random.normal(_jax.random.fold_in(key, 0), shape, _jnp.float32)
    for i, name in enumerate(TWIN_WEIGHTS):
        w = inp[name].astype(_jnp.float32)
        if MOMENT_SCALE is None:
            s = _jnp.sqrt(_jnp.mean(_jnp.square(w)) + 1e-30)
        else:
            s = MOMENT_SCALE[name]
        km, kv = _jax.random.split(_jax.random.fold_in(key, i + 1))
        out[name] = w
        out["m_" + name] = s * _jax.random.normal(km, w.shape, _jnp.float32)
        out["v_" + name] = (s * s) * _jax.random.uniform(kv, w.shape, _jnp.float32, 0.5, 1.5)
    if N_MICROBATCH > 1:
        for name, axis in PER_EXAMPLE_BATCH_AXIS.items():
            out[name] = _to_microbatches(out[name], axis)
    return {'x': out['x'], 'positions': out['positions'], 'mix_norm': out['mix_norm'], 'mlp_norm': out['mlp_norm'], 'final_norm': out['final_norm'], 'attn_w_qkv': out['attn_w_qkv'], 'attn_b_qkv': out['attn_b_qkv'], 'attn_sinks': out['attn_sinks'], 'attn_w_o': out['attn_w_o'], 'hgrn_w_in': out['hgrn_w_in'], 'hgrn_g_norm': out['hgrn_g_norm'], 'hgrn_w_o': out['hgrn_w_o'], 'hgrn_lower_bounds': out['hgrn_lower_bounds'], 'mlp_w_up': out['mlp_w_up'], 'mlp_w_down': out['mlp_w_down'], 'loss_target': out['loss_target'], 'm_mix_norm': out['m_mix_norm'], 'm_mlp_norm': out['m_mlp_norm'], 'm_final_norm': out['m_final_norm'], 'm_attn_w_qkv': out['m_attn_w_qkv'], 'm_attn_b_qkv': out['m_attn_b_qkv'], 'm_attn_sinks': out['m_attn_sinks'], 'm_attn_w_o': out['m_attn_w_o'], 'm_hgrn_w_in': out['m_hgrn_w_in'], 'm_hgrn_g_norm': out['m_hgrn_g_norm'], 'm_hgrn_w_o': out['m_hgrn_w_o'], 'm_hgrn_lower_bounds': out['m_hgrn_lower_bounds'], 'm_mlp_w_up': out['m_mlp_w_up'], 'm_mlp_w_down': out['m_mlp_w_down'], 'v_mix_norm': out['v_mix_norm'], 'v_mlp_norm': out['v_mlp_norm'], 'v_final_norm': out['v_final_norm'], 'v_attn_w_qkv': out['v_attn_w_qkv'], 'v_attn_b_qkv': out['v_attn_b_qkv'], 'v_attn_sinks': out['v_attn_sinks'], 'v_attn_w_o': out['v_attn_w_o'], 'v_hgrn_w_in': out['v_hgrn_w_in'], 'v_hgrn_g_norm': out['v_hgrn_g_norm'], 'v_hgrn_w_o': out['v_hgrn_w_o'], 'v_hgrn_lower_bounds': out['v_hgrn_lower_bounds'], 'v_mlp_w_up': out['v_mlp_w_up'], 'v_mlp_w_down': out['v_mlp_w_down']}


def _loss(weights, diff, rest, loss_target):
    with _jax.named_scope("forward"):
        args = {**rest, TWIN_DIFF_INPUT: diff, **{k: w.astype(_WEIGHT_DTYPES[k]) for k, w in weights.items()}}
        y = _forward(args)
    with _jax.named_scope("loss_head"):
        err = _jnp.square(y.astype(_jnp.float32) - loss_target)
        return 0.5 * _jnp.sum(_jnp.mean(err, axis=-1)) if err.ndim else 0.5 * err


def _adamw(w, g, m, v):
    m = ADAM_B1 * m + (1.0 - ADAM_B1) * g
    v = ADAM_B2 * v + (1.0 - ADAM_B2) * _jnp.square(g)
    m_hat = m / (1.0 - ADAM_B1 ** ADAM_STEP)
    v_hat = v / (1.0 - ADAM_B2 ** ADAM_STEP)
    delta = -ADAM_LR * (m_hat / (_jnp.sqrt(v_hat) + ADAM_EPS) + ADAM_WD * w)
    return delta, m, v


def reference(x, positions, mix_norm, mlp_norm, final_norm, attn_w_qkv, attn_b_qkv, attn_sinks, attn_w_o, hgrn_w_in, hgrn_g_norm, hgrn_w_o, hgrn_lower_bounds, mlp_w_up, mlp_w_down, loss_target, m_mix_norm, m_mlp_norm, m_final_norm, m_attn_w_qkv, m_attn_b_qkv, m_attn_sinks, m_attn_w_o, m_hgrn_w_in, m_hgrn_g_norm, m_hgrn_w_o, m_hgrn_lower_bounds, m_mlp_w_up, m_mlp_w_down, v_mix_norm, v_mlp_norm, v_final_norm, v_attn_w_qkv, v_attn_b_qkv, v_attn_sinks, v_attn_w_o, v_hgrn_w_in, v_hgrn_g_norm, v_hgrn_w_o, v_hgrn_lower_bounds, v_mlp_w_up, v_mlp_w_down):
    given = dict(x=x, positions=positions, mix_norm=mix_norm, mlp_norm=mlp_norm, final_norm=final_norm, attn_w_qkv=attn_w_qkv, attn_b_qkv=attn_b_qkv, attn_sinks=attn_sinks, attn_w_o=attn_w_o, hgrn_w_in=hgrn_w_in, hgrn_g_norm=hgrn_g_norm, hgrn_w_o=hgrn_w_o, hgrn_lower_bounds=hgrn_lower_bounds, mlp_w_up=mlp_w_up, mlp_w_down=mlp_w_down, loss_target=loss_target, m_mix_norm=m_mix_norm, m_mlp_norm=m_mlp_norm, m_final_norm=m_final_norm, m_attn_w_qkv=m_attn_w_qkv, m_attn_b_qkv=m_attn_b_qkv, m_attn_sinks=m_attn_sinks, m_attn_w_o=m_attn_w_o, m_hgrn_w_in=m_hgrn_w_in, m_hgrn_g_norm=m_hgrn_g_norm, m_hgrn_w_o=m_hgrn_w_o, m_hgrn_lower_bounds=m_hgrn_lower_bounds, m_mlp_w_up=m_mlp_w_up, m_mlp_w_down=m_mlp_w_down, v_mix_norm=v_mix_norm, v_mlp_norm=v_mlp_norm, v_final_norm=v_final_norm, v_attn_w_qkv=v_attn_w_qkv, v_attn_b_qkv=v_attn_b_qkv, v_attn_sinks=v_attn_sinks, v_attn_w_o=v_attn_w_o, v_hgrn_w_in=v_hgrn_w_in, v_hgrn_g_norm=v_hgrn_g_norm, v_hgrn_w_o=v_hgrn_w_o, v_hgrn_lower_bounds=v_hgrn_lower_bounds, v_mlp_w_up=v_mlp_w_up, v_mlp_w_down=v_mlp_w_down)
    weights = {n: given[n] for n in TWIN_WEIGHTS}
    shared = {n: given[n] for n in SHARED_INPUTS}
    per_example = {n: given[n] for n in ['x', 'positions']}
    grad_fn = _jax.value_and_grad(_loss, argnums=(0, 1))

    def one_microbatch(ex, loss_target):
        ex = dict(ex)
        diff = ex.pop(TWIN_DIFF_INPUT)
        return grad_fn(weights, diff, {**shared, **ex}, loss_target)

    if N_MICROBATCH == 1:
        loss, (grad_w, grad_x) = one_microbatch(per_example, given["loss_target"])
    else:
        def body(carry, xs):
            loss_sum, grad_sum = carry
            l_k, (gw_k, gx_k) = one_microbatch(xs[0], xs[1])
            with _jax.named_scope("update"):
                return (loss_sum + l_k, _jax.tree.map(_jnp.add, grad_sum, gw_k)), gx_k

        init = (_jnp.zeros((), _jnp.float32), _jax.tree.map(_jnp.zeros_like, weights))
        (loss, grad_w), grad_x = _jax.lax.scan(body, init, (per_example, given["loss_target"]))
    with _jax.named_scope("update"):
        delta_w, new_m, new_v = {}, {}, {}
        for n in TWIN_WEIGHTS:
            delta_w[n], new_m[n], new_v[n] = _adamw(weights[n], grad_w[n], given["m_" + n], given["v_" + n])
    return (loss, grad_x, *[grad_w[n] for n in TWIN_WEIGHTS], *[delta_w[n] for n in TWIN_WEIGHTS],
            *[new_m[n] for n in TWIN_WEIGHTS], *[new_v[n] for n in TWIN_WEIGHTS])
```

```python
import functools

import jax
import jax.numpy as jnp
from jax import lax
from jax.experimental import pallas as pl
from jax.experimental.pallas import tpu as pltpu

F32 = jnp.float32
BF16 = jnp.bfloat16

D_MODEL = 1024
HEAD_DIM = 64
N_Q_HEADS = 16
Q_DIM = 1024
KV_DIM = 256
QKV_DIM = 1536
ATT_BLOCK = 128
ROT_HALF = 8
ROPE_THETA = 500000.0
NEG_INF = -1e30
HGRN_HEADS = 8
HGRN_DK = 128
CHUNK = 64
D_FF = 4096
NORM_EPS = 1e-5
N_DEV = 8

ADAM_LR = 0.001
ADAM_B1 = 0.9
ADAM_B2 = 0.999
ADAM_EPS = 1e-08
ADAM_WD = 0.01
ADAM_STEP = 10

LANES = 128
VMEM_LIMIT = 56 * 1024 * 1024

PACK_ROWS = (("attn_w_qkv", 192), ("attn_w_o", 128), ("hgrn_w_in", 512), ("hgrn_w_o", 128),
             ("mlp_w_up", 1024), ("mlp_w_down", 1024))
PACK_TOTAL = 3072
SMALL_ROWS = 16


def _dot(a, b):
    return jnp.dot(a, b, preferred_element_type=F32)


def _dot_nt(a, b):
    return lax.dot_general(a, b, (((1,), (1,)), ((), ())), preferred_element_type=F32)


def _dot_tn(a, b):
    return lax.dot_general(a, b, (((0,), (0,)), ((), ())), preferred_element_type=F32)


def _params(**kw):
    return pltpu.CompilerParams(vmem_limit_bytes=VMEM_LIMIT, **kw)


def _full_spec(a):
    nd = a.ndim
    return pl.BlockSpec(a.shape, lambda *_: (0,) * nd)


def _row_call(name, body, n_rows, tm, row_ins, full_ins, row_outs, acc_outs=()):
    grid = (n_rows // tm,)
    in_specs = [pl.BlockSpec((tm, w), functools.partial(lambda i, cb: (i, cb), cb=cb)) for _, w, cb in row_ins]
    in_specs += [_full_spec(a) for a in full_ins]
    out_shape = [jax.ShapeDtypeStruct((n_rows, w), dt) for w, dt in row_outs]
    out_specs = [pl.BlockSpec((tm, w), lambda i: (i, 0)) for w, _ in row_outs]
    for shp, dt in acc_outs:
        out_shape.append(jax.ShapeDtypeStruct(shp, dt))
        out_specs.append(pl.BlockSpec(shp, functools.partial(lambda i, nd: (0,) * nd, nd=len(shp))))
    return pl.pallas_call(
        body, name=name, grid=grid, in_specs=in_specs, out_specs=out_specs, out_shape=out_shape,
        compiler_params=_params(dimension_semantics=("arbitrary",)),
    )(*[a for a, _, _ in row_ins], *full_ins)


def _rms(x, gain):
    r = lax.rsqrt(jnp.mean(x * x, axis=-1, keepdims=True) + NORM_EPS)
    xhat = x * r
    return xhat * gain, xhat, r


def _rms_bwd(dy, xhat, r, gain):
    dxhat = dy * gain
    dx = r * (dxhat - xhat * jnp.mean(dxhat * xhat, axis=-1, keepdims=True))
    return dx, dy * xhat


def _norm_mm(name, x, gain, w, bias=None, tm=256):
    T = x.shape[0]
    tm = min(tm, T)
    n = w.shape[1]
    nc = 512
    assert n % nc == 0

    def body(*refs):
        if bias is None:
            x_ref, g_ref, w_ref, y_ref, h_ref = refs
        else:
            x_ref, g_ref, w_ref, b_ref, y_ref, h_ref = refs
        h, _, _ = _rms(x_ref[...], g_ref[...])
        hb = h.astype(BF16)
        h_ref[...] = hb
        for c in range(n // nc):
            sl = slice(c * nc, (c + 1) * nc)
            y = _dot(hb, w_ref[:, sl])
            if bias is not None:
                y = y + b_ref[:, sl]
            y_ref[:, sl] = y

    full = [gain, w] + ([bias] if bias is not None else [])
    return _row_call(name, body, T, tm, [(x, D_MODEL, 0)], full, [(n, F32), (D_MODEL, BF16)])


def _mm_res(name, a, w, res, tm=512):
    T = a.shape[0]
    tm = min(tm, T)

    def body(a_ref, r_ref, w_ref, o_ref):
        o_ref[...] = r_ref[...] + _dot(a_ref[...], w_ref[...])

    return _row_call(name, body, T, tm, [(a, a.shape[1], 0), (res, D_MODEL, 0)], [w], [(D_MODEL, F32)])[0]


def _mlp_down(name, u, w, res, tm=256):
    T = u.shape[0]
    tm = min(tm, T)
    kc = 1024

    def body(u_ref, r_ref, w_ref, o_ref):
        acc = r_ref[...]
        for c in range(D_FF // kc):
            sl = slice(c * kc, (c + 1) * kc)
            a = jnp.maximum(u_ref[:, sl], 0.0)
            acc = acc + _dot((a * a).astype(BF16), w_ref[sl, :])
        o_ref[...] = acc

    return _row_call(name, body, T, tm, [(u, D_FF, 0), (res, D_MODEL, 0)], [w], [(D_MODEL, F32)])[0]


def _hgrn_out(name, o_raw, z, gn, w, res, tm=256):
    T = o_raw.shape[0]
    tm = min(tm, T)

    def body(o_ref, g_ref, r_ref, gn_ref, w_ref, x_ref, a_ref):
        y, _, _ = _rms(o_ref[...], gn_ref[...])
        g = g_ref[...]
        a = (y * (g * jax.nn.sigmoid(g))).astype(BF16)
        a_ref[...] = a
        x_ref[...] = r_ref[...] + _dot(a, w_ref[...])

    return _row_call(name, body, T, tm, [(o_raw, D_MODEL, 0), (z, D_MODEL, 3), (res, D_MODEL, 0)], [gn, w],
                     [(D_MODEL, F32), (D_MODEL, BF16)])


def _loss_head(name, x, target, gain, tm=512):
    T = x.shape[0]
    tm = min(tm, T)

    def body(x_ref, t_ref, g_ref, dx_ref, loss_ref, dg_ref):
        @pl.when(pl.program_id(0) == 0)
        def _():
            loss_ref[...] = jnp.zeros_like(loss_ref)
            dg_ref[...] = jnp.zeros_like(dg_ref)

        gain_v = g_ref[...]
        y, xhat, r = _rms(x_ref[...], gain_v)
        diff = y - t_ref[...]
        row = jnp.sum(diff * diff, axis=-1, keepdims=True) * (1.0 / D_MODEL)
        loss_ref[...] += jnp.broadcast_to(0.5 * jnp.sum(row, axis=0, keepdims=True), loss_ref.shape)
        dy = diff * (1.0 / D_MODEL)
        dx, dgr = _rms_bwd(dy, xhat, r, gain_v)
        dx_ref[...] = dx
        dg_ref[...] += jnp.sum(dgr, axis=0, keepdims=True)

    return _row_call(name, body, T, tm, [(x, D_MODEL, 0), (target, D_MODEL, 0)], [gain], [(D_MODEL, F32)],
                     [((1, LANES), F32), ((1, D_MODEL), F32)])


def _mm_nt_rmsbwd(name, dy, w, x, gain, dres, tm=256, with_colsum=False):
    T = x.shape[0]
    tm = min(tm, T)
    n = dy.shape[1]

    def body(*refs):
        if with_colsum:
            dy_ref, x_ref, dr_ref, w_ref, g_ref, dx_ref, dg_ref, cs_ref = refs
        else:
            dy_ref, x_ref, dr_ref, w_ref, g_ref, dx_ref, dg_ref = refs

        @pl.when(pl.program_id(0) == 0)
        def _():
            dg_ref[...] = jnp.zeros_like(dg_ref)
            if with_colsum:
                cs_ref[...] = jnp.zeros_like(cs_ref)

        dyv = dy_ref[...]
        dh = _dot_nt(dyv.astype(BF16), w_ref[...])
        gain_v = g_ref[...]
        _, xhat, r = _rms(x_ref[...], gain_v)
        dx, dgr = _rms_bwd(dh, xhat, r, gain_v)
        dx_ref[...] = dr_ref[...] + dx
        dg_ref[...] += jnp.sum(dgr, axis=0, keepdims=True)
        if with_colsum:
            cs_ref[...] += jnp.sum(dyv.astype(F32), axis=0, keepdims=True)

    acc = [((1, D_MODEL), F32)] + ([((1, n), F32)] if with_colsum else [])
    return _row_call(name, body, T, tm, [(dy, n, 0), (x, D_MODEL, 0), (dres, D_MODEL, 0)], [w, gain],
                     [(D_MODEL, F32)], acc)


def _mm_nt(name, dy, w, out_dtype, tm=512):
    T = dy.shape[0]
    tm = min(tm, T)
    k = w.shape[0]

    def body(dy_ref, w_ref, o_ref):
        o_ref[...] = _dot_nt(dy_ref[...].astype(BF16), w_ref[...]).astype(out_dtype)

    return _row_call(name, body, T, tm, [(dy, dy.shape[1], 0)], [w], [(k, out_dtype)])[0]


def _mlp_bwd_act(name, dy, u, w_down, tm=256):
    T = u.shape[0]
    tm = min(tm, T)
    kc = 1024

    def body(dy_ref, u_ref, w_ref, du_ref, a_ref):
        dyb = dy_ref[...].astype(BF16)
        for c in range(D_FF // kc):
            sl = slice(c * kc, (c + 1) * kc)
            a = jnp.maximum(u_ref[:, sl], 0.0)
            da = _dot_nt(dyb, w_ref[sl, :])
            du_ref[:, sl] = (da * (2.0 * a)).astype(BF16)
            a_ref[:, sl] = (a * a).astype(BF16)

    return _row_call(name, body, T, tm, [(dy, D_MODEL, 0), (u, D_FF, 0)], [w_down], [(D_FF, BF16), (D_FF, BF16)])


def _hgrn_out_bwd(name, dx, o_raw, z, w, gn, tm=256):
    T = dx.shape[0]
    tm = min(tm, T)

    def body(dx_ref, o_ref, g_ref, w_ref, gn_ref, do_ref, dg_ref, dgn_ref):
        @pl.when(pl.program_id(0) == 0)
        def _():
            dgn_ref[...] = jnp.zeros_like(dgn_ref)

        da = _dot_nt(dx_ref[...].astype(BF16), w_ref[...])
        gn_v = gn_ref[...]
        y, xhat, r = _rms(o_ref[...], gn_v)
        g = g_ref[...]
        sg = jax.nn.sigmoid(g)
        dg_ref[...] = (da * y * (sg * (1.0 + g * (1.0 - sg)))).astype(BF16)
        dyn = da * (g * sg)
        do, dgr = _rms_bwd(dyn, xhat, r, gn_v)
        do_ref[...] = do
        dgn_ref[...] += jnp.sum(dgr, axis=0, keepdims=True)

    return _row_call(name, body, T, tm, [(dx, D_MODEL, 0), (o_raw, D_MODEL, 0), (z, D_MODEL, 3)], [w, gn],
                     [(D_MODEL, F32), (D_MODEL, BF16)], [((1, D_MODEL), F32)])


def _mm_tn(name, a, b, bm=1024, bn=512, tk=2048):
    T, M = a.shape
    N = b.shape[1]
    bm, bn, tk = min(bm, M), min(bn, N), min(tk, T)

    def body(a_ref, b_ref, o_ref):
        @pl.when(pl.program_id(2) == 0)
        def _():
            o_ref[...] = jnp.zeros_like(o_ref)

        o_ref[...] += _dot_tn(a_ref[...].astype(BF16), b_ref[...].astype(BF16))

    return pl.pallas_call(
        body, name=name, grid=(M // bm, N // bn, T // tk),
        in_specs=[pl.BlockSpec((tk, bm), lambda i, j, k: (k, i)), pl.BlockSpec((tk, bn), lambda i, j, k: (k, j))],
        out_specs=pl.BlockSpec((bm, bn), lambda i, j, k: (i, j)),
        out_shape=jax.ShapeDtypeStruct((M, N), F32),
        compiler_params=_params(dimension_semantics=("parallel", "parallel", "arbitrary")),
    )(a, b)


def _rot_fwd(x, tab):
    c, sa, sb = tab[:, :LANES], tab[:, LANES:2 * LANES], tab[:, 2 * LANES:]
    outs = []
    for j in range(x.shape[1] // LANES):
        xs = x[:, j * LANES:(j + 1) * LANES]
        outs.append(xs * c + pltpu.roll(xs, ROT_HALF, 1) * sa + pltpu.roll(xs, LANES - ROT_HALF, 1) * sb)
    return outs


def _rot_bwd(dys, tab):
    c, sa, sb = tab[:, :LANES], tab[:, LANES:2 * LANES], tab[:, 2 * LANES:]
    return [dy * c + pltpu.roll(dy * sa, LANES - ROT_HALF, 1) + pltpu.roll(dy * sb, ROT_HALF, 1) for dy in dys]


def _attn_masks(n):
    qi = lax.broadcasted_iota(jnp.int32, (ATT_BLOCK, 2 * ATT_BLOCK), 0)
    kj = lax.broadcasted_iota(jnp.int32, (ATT_BLOCK, 2 * ATT_BLOCK), 1)
    delta = qi + ATT_BLOCK - kj
    first_key = jnp.where(n > 0, 0, ATT_BLOCK)
    valid = (delta >= 0) & (delta < ATT_BLOCK) & (kj >= first_key)
    lane = lax.broadcasted_iota(jnp.int32, (1, LANES), 1)
    return valid, lane < HEAD_DIM


def _attn_probs(qm, k_use, valid, sink):
    s = _dot_nt(qm, k_use) * (HEAD_DIM ** -0.5)
    s = jnp.where(valid, s, NEG_INF)
    m = jnp.maximum(jnp.max(s, axis=-1, keepdims=True), sink)
    e = jnp.exp(s - m)
    es = jnp.exp(sink - m)
    inv = 1.0 / (jnp.sum(e, axis=-1, keepdims=True) + es)
    return e * inv, es * inv


def _attn_specs(nb):
    prev = lambda n: jnp.maximum(jnp.minimum(n, nb - 1) - 1, 0)
    cur = lambda n: jnp.minimum(n, nb - 1)
    return [
        pl.BlockSpec((ATT_BLOCK, Q_DIM), lambda n: (cur(n), 0)),
        pl.BlockSpec((ATT_BLOCK, KV_DIM), lambda n: (prev(n), 4)),
        pl.BlockSpec((ATT_BLOCK, KV_DIM), lambda n: (cur(n), 4)),
        pl.BlockSpec((ATT_BLOCK, KV_DIM), lambda n: (prev(n), 5)),
        pl.BlockSpec((ATT_BLOCK, KV_DIM), lambda n: (cur(n), 5)),
        pl.BlockSpec((ATT_BLOCK, 3 * LANES), lambda n: (prev(n), 0)),
        pl.BlockSpec((ATT_BLOCK, 3 * LANES), lambda n: (cur(n), 0)),
        pl.BlockSpec(memory_space=pltpu.SMEM),
    ]


def _kv_band(kp_ref, kc_ref, vp_ref, vc_ref, tp_ref, tc_ref):
    kp = _rot_fwd(kp_ref[...], tp_ref[...])
    kc = _rot_fwd(kc_ref[...], tc_ref[...])
    ks, vs = [], []
    for j in range(KV_DIM // LANES):
        kb = jnp.concatenate([kp[j], kc[j]], axis=0)
        vb = jnp.concatenate([vp_ref[:, j * LANES:(j + 1) * LANES], vc_ref[:, j * LANES:(j + 1) * LANES]], axis=0)
        ks.append((kb.astype(BF16), pltpu.roll(kb, HEAD_DIM, 1).astype(BF16)))
        vs.append((vb.astype(BF16), pltpu.roll(vb, HEAD_DIM, 1).astype(BF16)))
    return ks, vs


def _attn_fwd(qkv, rot, sinks):
    T = qkv.shape[0]
    nb = T // ATT_BLOCK

    def body(q_ref, kp_ref, kc_ref, vp_ref, vc_ref, tp_ref, tc_ref, sink_ref, o_ref):
        n = pl.program_id(0)
        valid, low = _attn_masks(n)
        ks, vs = _kv_band(kp_ref, kc_ref, vp_ref, vc_ref, tp_ref, tc_ref)
        qs = _rot_fwd(q_ref[...], tc_ref[...])
        for p in range(Q_DIM // LANES):
            kpair, khalf = p // 4, (p // 2) % 2
            outs = []
            for hf in range(2):
                qm = jnp.where(low if hf == 0 else ~low, qs[p], 0.0).astype(BF16)
                sw = 0 if khalf == hf else 1
                pr, _ = _attn_probs(qm, ks[kpair][sw], valid, sink_ref[0, 2 * p + hf])
                outs.append(_dot(pr.astype(BF16), vs[kpair][sw]))
            o_ref[:, p * LANES:(p + 1) * LANES] = jnp.where(low, outs[0], outs[1]).astype(BF16)

    return pl.pallas_call(
        body, name="attn_fwd", grid=(nb,), in_specs=_attn_specs(nb),
        out_specs=pl.BlockSpec((ATT_BLOCK, Q_DIM), lambda n: (n, 0)),
        out_shape=jax.ShapeDtypeStruct((T, Q_DIM), BF16),
        compiler_params=_params(dimension_semantics=("arbitrary",)),
    )(qkv, qkv, qkv, qkv, qkv, rot, rot, sinks)


def _attn_bwd(qkv, rot, sinks, dout):
    T = qkv.shape[0]
    nb = T // ATT_BLOCK
    npair = KV_DIM // LANES

    def body(q_ref, kp_ref, kc_ref, vp_ref, vc_ref, tp_ref, tc_ref, sink_ref, do_ref, dqkv_ref, dsink_ref,
             dq_c, dk_c, dv_c):
        n = pl.program_id(0)

        @pl.when(n == 0)
        def _():
            dq_c[...] = jnp.zeros_like(dq_c)
            dk_c[...] = jnp.zeros_like(dk_c)
            dv_c[...] = jnp.zeros_like(dv_c)
            dsink_ref[...] = jnp.zeros_like(dsink_ref)

        def flush(dk_prev, dv_prev, tab_ref):
            dqkv_ref[:, :Q_DIM] = dq_c[...]
            dk = _rot_bwd([dk_c[:, j * LANES:(j + 1) * LANES] + dk_prev[j] for j in range(npair)], tab_ref[...])
            for j in range(npair):
                dqkv_ref[:, Q_DIM + j * LANES:Q_DIM + (j + 1) * LANES] = dk[j]
                dqkv_ref[:, Q_DIM + KV_DIM + j * LANES:Q_DIM + KV_DIM + (j + 1) * LANES] = (
                    dv_c[:, j * LANES:(j + 1) * LANES] + dv_prev[j])

        @pl.when(n < nb)
        def _():
            valid, low = _attn_masks(n)
            lane = lax.broadcasted_iota(jnp.int32, (1, LANES), 1)
            ks, vs = _kv_band(kp_ref, kc_ref, vp_ref, vc_ref, tp_ref, tc_ref)
            qs = _rot_fwd(q_ref[...], tc_ref[...])
            dk_acc = [jnp.zeros((2 * ATT_BLOCK, LANES), F32) for _ in range(npair)]
            dv_acc = [jnp.zeros((2 * ATT_BLOCK, LANES), F32) for _ in range(npair)]
            dsink = jnp.zeros((1, LANES), F32)
            dqs = []
            for p in range(Q_DIM // LANES):
                kpair, khalf = p // 4, (p // 2) % 2
                do_pair = do_ref[:, p * LANES:(p + 1) * LANES]
                dq_h = []
                for hf in range(2):
                    sel = low if hf == 0 else ~low
                    qm = jnp.where(sel, qs[p], 0.0).astype(BF16)
                    dom = jnp.where(sel, do_pair, 0.0).astype(BF16)
                    sw = 0 if khalf == hf else 1
                    k_use, v_use = ks[kpair][sw], vs[kpair][sw]
                    pr, ps = _attn_probs(qm, k_use, valid, sink_ref[0, 2 * p + hf])
                    dp = _dot_nt(dom, v_use)
                    dd = jnp.sum(pr * dp, axis=-1, keepdims=True)
                    ds = (pr * (dp - dd) * (HEAD_DIM ** -0.5)).astype(BF16)
                    dq_h.append(_dot(ds, k_use))
                    dk_u = _dot_tn(ds, qm)
                    dv_u = _dot_tn(pr.astype(BF16), dom)
                    if sw:
                        dk_u = pltpu.roll(dk_u, HEAD_DIM, 1)
                        dv_u = pltpu.roll(dv_u, HEAD_DIM, 1)
                    dk_acc[kpair] = dk_acc[kpair] + dk_u
                    dv_acc[kpair] = dv_acc[kpair] + dv_u
                    dsink = dsink + jnp.where(lane == 2 * p + hf, -jnp.sum(ps * dd, axis=0, keepdims=True), 0.0)
                dqs.append(jnp.where(low, dq_h[0], dq_h[1]))
            flush([a[:ATT_BLOCK] for a in dk_acc], [a[:ATT_BLOCK] for a in dv_acc], tp_ref)
            dq = _rot_bwd(dqs, tc_ref[...])
            for p in range(Q_DIM // LANES):
                dq_c[:, p * LANES:(p + 1) * LANES] = dq[p]
            for j in range(npair):
                dk_c[:, j * LANES:(j + 1) * LANES] = dk_acc[j][ATT_BLOCK:]
                dv_c[:, j * LANES:(j + 1) * LANES] = dv_acc[j][ATT_BLOCK:]
            dsink_ref[...] += dsink

        @pl.when(n == nb)
        def _():
            zero = [jnp.zeros((ATT_BLOCK, LANES), F32) for _ in range(npair)]
            flush(zero, zero, tc_ref)

    do_spec = pl.BlockSpec((ATT_BLOCK, Q_DIM), lambda n: (jnp.minimum(n, nb - 1), 0))
    return pl.pallas_call(
        body, name="attn_bwd", grid=(nb + 1,), in_specs=_attn_specs(nb) + [do_spec],
        out_specs=[pl.BlockSpec((ATT_BLOCK, QKV_DIM), lambda n: (jnp.maximum(n - 1, 0), 0)),
                   pl.BlockSpec((1, LANES), lambda n: (0, 0))],
        out_shape=[jax.ShapeDtypeStruct((T, QKV_DIM), F32), jax.ShapeDtypeStruct((1, LANES), F32)],
        scratch_shapes=[pltpu.VMEM((ATT_BLOCK, Q_DIM), F32), pltpu.VMEM((ATT_BLOCK, KV_DIM), F32),
                        pltpu.VMEM((ATT_BLOCK, KV_DIM), F32)],
        compiler_params=_params(dimension_semantics=("arbitrary",)),
    )(qkv, qkv, qkv, qkv, qkv, rot, rot, sinks, dout)


LEVELS = (32, 16, 8)
DIAG = 8


def _lower_bound(lb_ref):
    l0, l1 = lb_ref[0:1, :], lb_ref[1:2, :]
    mx = jnp.maximum(l0, l1)
    e0, e1 = jnp.exp(l0 - mx), jnp.exp(l1 - mx)
    return e1 / (e0 + e1)


def _cumsum_rows(x, row):
    for sh in (1, 2, 4, 8, 16, 32):
        x = x + jnp.where(row >= sh, pltpu.roll(x, sh, 0), 0.0)
    return x


def _rev_cumsum_rows(x, row):
    for sh in (1, 2, 4, 8, 16, 32):
        x = x + jnp.where(row < CHUNK - sh, pltpu.roll(x, CHUNK - sh, 0), 0.0)
    return x


def _level_masks():
    t = lax.broadcasted_iota(jnp.int32, (CHUNK, CHUNK), 0)
    s = lax.broadcasted_iota(jnp.int32, (CHUNK, CHUNK), 1)
    return [((t & h) != 0) & ((s & h) == 0) & ((t ^ s) < 2 * h) for h in LEVELS]


def _level_scale(b, b_scr, h):
    parts = [jnp.broadcast_to(b_scr[pl.ds(j * 2 * h + h - 1, 1), :], (2 * h, HGRN_DK)) for j in range(CHUNK // (2 * h))]
    mid = parts[0] if len(parts) == 1 else jnp.concatenate(parts, axis=0)
    return jnp.exp(-jnp.abs(b - mid))


def _hgrn_gates(zq, zf, lb):
    sq = jax.nn.sigmoid(zq)
    q = zq * sq
    sg = jax.nn.sigmoid(zf)
    forget = lb + (1.0 - lb) * sg
    return q, sq, sg, forget, 1.0 - forget, jnp.log(forget)


def _hgrn_specs(T, rb, rev):
    nr = T // rb
    ri = (lambda r: nr - 1 - r) if rev else (lambda r: r)
    return nr, ri, [
        pl.BlockSpec((rb, HGRN_DK), lambda h, r: (ri(r), h)),
        pl.BlockSpec((rb, HGRN_DK), lambda h, r: (ri(r), HGRN_HEADS + h)),
        pl.BlockSpec((rb, HGRN_DK), lambda h, r: (ri(r), 2 * HGRN_HEADS + h)),
        pl.BlockSpec((2, HGRN_DK), lambda h, r: (0, h)),
    ]


def _hgrn_fwd(z, lb_raw, rb=512):
    T = z.shape[0]
    rb = min(rb, T)
    ncb = rb // CHUNK
    nr, ri, in_specs = _hgrn_specs(T, rb, False)

    def body(zq_ref, zf_ref, zi_ref, lb_ref, o_ref, st_ref, state, b_scr):
        @pl.when(pl.program_id(1) == 0)
        def _():
            state[...] = jnp.zeros_like(state)

        lb = _lower_bound(lb_ref)
        row = lax.broadcasted_iota(jnp.int32, (CHUNK, HGRN_DK), 0)
        masks = _level_masks()

        def chunk(c, carry):
            rows = pl.ds(pl.multiple_of(c * CHUNK, CHUNK), CHUNK)
            q, _, _, _, k, lf = _hgrn_gates(zq_ref[rows, :], zf_ref[rows, :], lb)
            v = zi_ref[rows, :]
            b = _cumsum_rows(lf, row)
            b_scr[...] = b
            st = state[...]
            st_ref[c, 0] = st
            o = _dot_nt((q * jnp.exp(b)).astype(BF16), st.astype(BF16))
            sc = jnp.zeros((CHUNK, CHUNK), F32)
            for h, mask in zip(LEVELS, masks):
                e = _level_scale(b, b_scr, h)
                sc = sc + jnp.where(mask, _dot_nt((q * e).astype(BF16), (k * e).astype(BF16)), 0.0)
            o = o + _dot(sc.astype(BF16), v.astype(BF16))
            for d in range(DIAG):
                if d == 0:
                    w = q * k
                    vr = v
                else:
                    ok = (row & (DIAG - 1)) >= d
                    w = jnp.where(ok, q * pltpu.roll(k, d, 0) * jnp.exp(jnp.where(ok, b - pltpu.roll(b, d, 0), 0.0)), 0.0)
                    vr = pltpu.roll(v, d, 0)
                o = o + jnp.sum(w, axis=-1, keepdims=True) * vr
            o_ref[rows, :] = o
            b_last = b_scr[pl.ds(CHUNK - 1, 1), :]
            kd = k * jnp.exp(b_last - b)
            state[...] = st * jnp.exp(b_last) + _dot_tn(v.astype(BF16), kd.astype(BF16))
            return carry

        lax.fori_loop(0, ncb, chunk, 0)

    return pl.pallas_call(
        body, name="hgrn_fwd", grid=(HGRN_HEADS, nr), in_specs=in_specs,
        out_specs=[pl.BlockSpec((rb, HGRN_DK), lambda h, r: (r, h)),
                   pl.BlockSpec((ncb, 1, HGRN_DK, HGRN_DK), lambda h, r: (r, h, 0, 0))],
        out_shape=[jax.ShapeDtypeStruct((T, D_MODEL), F32),
                   jax.ShapeDtypeStruct((T // CHUNK, HGRN_HEADS, HGRN_DK, HGRN_DK), F32)],
        scratch_shapes=[pltpu.VMEM((HGRN_DK, HGRN_DK), F32), pltpu.VMEM((CHUNK, HGRN_DK), F32)],
        compiler_params=_params(dimension_semantics=("arbitrary", "arbitrary")),
    )(z, z, z, lb_raw)


def _hgrn_bwd(z, lb_raw, states, do, rb=512):
    T = z.shape[0]
    rb = min(rb, T)
    ncb = rb // CHUNK
    nr, ri, in_specs = _hgrn_specs(T, rb, True)
    in_specs += [pl.BlockSpec((ncb, 1, HGRN_DK, HGRN_DK), lambda h, r: (ri(r), h, 0, 0)),
                 pl.BlockSpec((rb, HGRN_DK), lambda h, r: (ri(r), h))]

    def body(zq_ref, zf_ref, zi_ref, lb_ref, st_ref, do_ref, dq_ref, df_ref, di_ref, dlb_ref, dstate, b_scr):
        @pl.when(pl.program_id(1) == 0)
        def _():
            dstate[...] = jnp.zeros_like(dstate)
            dlb_ref[...] = jnp.zeros_like(dlb_ref)

        lb = _lower_bound(lb_ref)
        row = lax.broadcasted_iota(jnp.int32, (CHUNK, HGRN_DK), 0)
        masks = _level_masks()

        def chunk(ci, dlb):
            c = ncb - 1 - ci
            rows = pl.ds(pl.multiple_of(c * CHUNK, CHUNK), CHUNK)
            zq = zq_ref[rows, :]
            q, sq, sg, forget, k, lf = _hgrn_gates(zq, zf_ref[rows, :], lb)
            v = zi_ref[rows, :]
            dov = do_ref[rows, :]
            b = _cumsum_rows(lf, row)
            b_scr[...] = b
            st = st_ref[c, 0]
            dst = dstate[...]
            b_last = b_scr[pl.ds(CHUNK - 1, 1), :]
            eb = jnp.exp(b)
            ebb = jnp.exp(b_last - b)
            e_last = jnp.exp(b_last)
            dob, vb, stb, dstb = dov.astype(BF16), v.astype(BF16), st.astype(BF16), dst.astype(BF16)
            dq = eb * _dot(dob, stb)
            dv = _dot_nt((k * ebb).astype(BF16), dstb)
            dk = ebb * _dot(vb, dstb)
            extra = e_last * jnp.sum(dst * st, axis=0, keepdims=True) + jnp.sum(k * dk, axis=0, keepdims=True)
            da = _dot_nt(dob, vb)
            sc = jnp.zeros((CHUNK, CHUNK), F32)
            for h, mask in zip(LEVELS, masks):
                e = _level_scale(b, b_scr, h)
                qs, ks = (q * e).astype(BF16), (k * e).astype(BF16)
                dam = jnp.where(mask, da, 0.0).astype(BF16)
                dq = dq + e * _dot(dam, ks)
                dk = dk + e * _dot_tn(dam, qs)
                sc = sc + jnp.where(mask, _dot_nt(qs, ks), 0.0)
            dv = dv + _dot_tn(sc.astype(BF16), dob)
            for d in range(DIAG):
                if d == 0:
                    dad = jnp.sum(dov * v, axis=-1, keepdims=True)
                    dq = dq + dad * k
                    dk = dk + dad * q
                    dv = dv + jnp.sum(q * k, axis=-1, keepdims=True) * dov
                else:
                    ok = (row & (DIAG - 1)) >= d
                    w = jnp.where(ok, jnp.exp(jnp.where(ok, b - pltpu.roll(b, d, 0), 0.0)), 0.0)
                    kr = pltpu.roll(k, d, 0)
                    dad = jnp.sum(dov * pltpu.roll(v, d, 0), axis=-1, keepdims=True)
                    ad = jnp.sum(q * kr * w, axis=-1, keepdims=True)
                    dq = dq + dad * kr * w
                    dk = dk + pltpu.roll(dad * q * w, CHUNK - d, 0)
                    dv = dv + pltpu.roll(ad * dov, CHUNK - d, 0)
            dlf = _rev_cumsum_rows(q * dq - k * dk, row) + extra
            dstate[...] = dst * e_last + _dot_tn(dob, (q * eb).astype(BF16))
            dforget = dlf / forget - dk
            dq_ref[rows, :] = (dq * (sq * (1.0 + zq * (1.0 - sq)))).astype(BF16)
            df_ref[rows, :] = (dforget * (1.0 - lb) * sg * (1.0 - sg)).astype(BF16)
            di_ref[rows, :] = dv.astype(BF16)
            return dlb + jnp.sum(dforget * (1.0 - sg), axis=0, keepdims=True)

        dlb_ref[...] += lax.fori_loop(0, ncb, chunk, jnp.zeros((1, HGRN_DK), F32))

    blk = pl.BlockSpec((rb, HGRN_DK), lambda h, r: (ri(r), h))
    return pl.pallas_call(
        body, name="hgrn_bwd", grid=(HGRN_HEADS, nr), in_specs=in_specs,
        out_specs=[blk, blk, blk, pl.BlockSpec((1, HGRN_DK), lambda h, r: (0, h))],
        out_shape=[jax.ShapeDtypeStruct((T, D_MODEL), BF16)] * 3 + [jax.ShapeDtypeStruct((1, D_MODEL), F32)],
        scratch_shapes=[pltpu.VMEM((HGRN_DK, HGRN_DK), F32), pltpu.VMEM((CHUNK, HGRN_DK), F32)],
        compiler_params=_params(dimension_semantics=("arbitrary", "arbitrary")),
    )(z, z, z, lb_raw, states, do)


MESH = pl.DeviceIdType.MESH
ANY = pl.BlockSpec(memory_space=pl.ANY)


def _place():
    return lax.axis_index("x"), lax.axis_index("y"), lax.axis_index("c")


def _all_gather(x_shard):
    def body(x_ref, out_ref, send_sems, recv_sems, local_sem):
        x, y, c = _place()
        me, sibling = (x, y, c), (x, y, 1 - c)
        chips = [(1 - x, y), (x, 1 - y), (1 - x, 1 - y)]

        def rows(px, py, pc):
            return out_ref.at[4 * px + 2 * py + pc]

        def copy(k, block, to, src=None):
            return pltpu.make_async_remote_copy(
                src_ref=rows(*block) if src is None else src, dst_ref=rows(*block),
                send_sem=send_sems.at[k], recv_sem=recv_sems.at[k], device_id=to, device_id_type=MESH)

        mine = pltpu.make_async_copy(x_ref, rows(*me), local_sem)
        mine.start()
        first = [copy(0, me, sibling, src=x_ref)]
        first += [copy(1 + j, me, (*chip, c), src=x_ref) for j, chip in enumerate(chips)]
        for cp in first:
            cp.start()
        passed = [copy(4 + j, (*chip, c), sibling) for j, chip in enumerate(chips)]
        for j, chip in enumerate(chips):
            copy(1 + j, (*chip, c), me).wait_recv()
            passed[j].start()
        copy(0, sibling, me).wait_recv()
        for j, chip in enumerate(chips):
            copy(4 + j, (*chip, 1 - c), me).wait_recv()
        for cp in first + passed:
            cp.wait_send()
        mine.wait()

    return pl.pallas_call(
        body, name="weights_all_gather",
        out_shape=jax.ShapeDtypeStruct((N_DEV,) + x_shard.shape, x_shard.dtype),
        in_specs=[ANY], out_specs=ANY,
        scratch_shapes=[pltpu.SemaphoreType.DMA((7,)), pltpu.SemaphoreType.DMA((7,)), pltpu.SemaphoreType.DMA],
    )(x_shard)


def _peers(x, y, c):
    out = []
    for k in range(1, N_DEV):
        px = 1 - x if k & 4 else x
        py = 1 - y if k & 2 else y
        pc = 1 - c if k & 1 else c
        out.append((k, (px, py, pc), 4 * px + 2 * py + pc))
    return out


def _exchange(g):
    def body(g_ref, recv_ref, send_sems, recv_sems, local_sem):
        x, y, c = _place()
        me = 4 * x + 2 * y + c
        local = pltpu.make_async_copy(g_ref.at[me], recv_ref.at[me], local_sem)
        local.start()
        copies = []
        for k, peer, pidx in _peers(x, y, c):
            cp = pltpu.make_async_remote_copy(
                src_ref=g_ref.at[pidx], dst_ref=recv_ref.at[me], send_sem=send_sems.at[k - 1],
                recv_sem=recv_sems.at[k - 1], device_id=peer, device_id_type=MESH)
            cp.start()
            copies.append(cp)
        for cp in copies:
            cp.wait()
        local.wait()

    return pl.pallas_call(
        body, name="grad_exchange", out_shape=jax.ShapeDtypeStruct(g.shape, g.dtype),
        in_specs=[ANY], out_specs=ANY,
        scratch_shapes=[pltpu.SemaphoreType.DMA((7,)), pltpu.SemaphoreType.DMA((7,)), pltpu.SemaphoreType.DMA],
    )(g)


def _adamw(w, g, m, v):
    m = ADAM_B1 * m + (1.0 - ADAM_B1) * g
    v = ADAM_B2 * v + (1.0 - ADAM_B2) * (g * g)
    m_hat = m / (1.0 - ADAM_B1 ** ADAM_STEP)
    v_hat = v / (1.0 - ADAM_B2 ** ADAM_STEP)
    delta = -ADAM_LR * (m_hat / (jnp.sqrt(v_hat) + ADAM_EPS) + ADAM_WD * w)
    return delta, m, v


def _adamw_sum(recv, w, m, v, tm=256):
    R, C = w.shape

    def body(r_ref, w_ref, m_ref, v_ref, g_ref, d_ref, nm_ref, nv_ref):
        g = r_ref[0].astype(F32)
        for s in range(1, N_DEV):
            g = g + r_ref[s].astype(F32)
        g_ref[...] = g
        d_ref[...], nm_ref[...], nv_ref[...] = _adamw(w_ref[...], g, m_ref[...], v_ref[...])

    blk = pl.BlockSpec((tm, C), lambda i: (i, 0))
    return pl.pallas_call(
        body, name="adamw_sharded", grid=(R // tm,),
        in_specs=[pl.BlockSpec((N_DEV, tm, C), lambda i: (0, i, 0)), blk, blk, blk],
        out_specs=[blk] * 4, out_shape=[jax.ShapeDtypeStruct((R, C), F32)] * 4,
        compiler_params=_params(dimension_semantics=("arbitrary",)),
    )(recv, w, m, v)


def _small_sync(part, w, m, v):
    def body(p_ref, w_ref, m_ref, v_ref, g_ref, d_ref, nm_ref, nv_ref, gath, send_sems, recv_sems):
        x, y, c = _place()
        me = 4 * x + 2 * y + c
        gath[me] = p_ref[...]
        copies = []
        for k, peer, _ in _peers(x, y, c):
            cp = pltpu.make_async_remote_copy(
                src_ref=p_ref, dst_ref=gath.at[me], send_sem=send_sems.at[k - 1], recv_sem=recv_sems.at[k - 1],
                device_id=peer, device_id_type=MESH)
            cp.start()
            copies.append(cp)
        for cp in copies:
            cp.wait()
        g = gath[0]
        for s in range(1, N_DEV):
            g = g + gath[s]
        wv = w_ref[...]
        l0, l1 = w_ref[8:9, :], w_ref[9:10, :]
        mx = jnp.maximum(l0, l1)
        e0, e1 = jnp.exp(l0 - mx), jnp.exp(l1 - mx)
        g9 = g[9:10, :] * (e0 / (e0 + e1)) * (e1 / (e0 + e1))
        row = lax.broadcasted_iota(jnp.int32, g.shape, 0)
        g = jnp.where(row == 9, g9, jnp.where(row == 8, -g9, g))
        g_ref[...] = g
        d_ref[...], nm_ref[...], nv_ref[...] = _adamw(wv, g, m_ref[...], v_ref[...])

    vm = pl.BlockSpec(memory_space=pltpu.VMEM)
    return pl.pallas_call(
        body, name="small_params_sync", in_specs=[vm] * 4, out_specs=[vm] * 4,
        out_shape=[jax.ShapeDtypeStruct(part.shape, F32)] * 4,
        scratch_shapes=[pltpu.VMEM((N_DEV,) + part.shape, F32), pltpu.SemaphoreType.DMA((7,)),
                        pltpu.SemaphoreType.DMA((7,))],
    )(part, w, m, v)


def _pack_shards(d):
    parts = [d[name].reshape(rows, D_MODEL) for name, rows in PACK_ROWS]
    used = sum(rows for _, rows in PACK_ROWS)
    parts.append(jnp.zeros((PACK_TOTAL - used, D_MODEL), parts[0].dtype))
    return jnp.concatenate(parts, axis=0)


def _unpack_shards(p, like):
    out, r0 = {}, 0
    for name, rows in PACK_ROWS:
        out[name] = p[r0:r0 + rows].reshape(like[name].shape)
        r0 += rows
    return out


def _unpack_gathered(wg):
    offs, r0 = {}, 0
    for name, rows in PACK_ROWS:
        offs[name] = wg[:, r0:r0 + rows]
        r0 += rows

    def cols(a, k, n):
        return a.reshape(N_DEV, k, n).transpose(1, 0, 2).reshape(k, N_DEV * n)

    up = offs["mlp_w_up"].reshape(N_DEV, 2, D_MODEL, D_FF // N_DEV).transpose(1, 2, 0, 3).reshape(2, D_MODEL, D_FF)
    down = offs["mlp_w_down"].reshape(N_DEV, 2, D_FF // N_DEV, D_MODEL).transpose(1, 0, 2, 3).reshape(2, D_FF, D_MODEL)
    return dict(
        attn_w_qkv=cols(offs["attn_w_qkv"], D_MODEL, QKV_DIM // N_DEV),
        attn_w_o=offs["attn_w_o"].reshape(Q_DIM, D_MODEL),
        hgrn_w_in=cols(offs["hgrn_w_in"], D_MODEL, 4 * D_MODEL // N_DEV),
        hgrn_w_o=offs["hgrn_w_o"].reshape(D_MODEL, D_MODEL),
        mlp_w_up=up, mlp_w_down=down)


def _pack_grads(gw):
    def cols(a, n):
        k = a.shape[0]
        return a.reshape(k, N_DEV, n).transpose(1, 0, 2).reshape(N_DEV, k * n // D_MODEL, D_MODEL)

    up = jnp.stack([cols(g, D_FF // N_DEV) for g in gw["mlp_w_up"]], axis=1).reshape(N_DEV, 1024, D_MODEL)
    down = jnp.stack([g.reshape(N_DEV, D_FF // N_DEV, D_MODEL) for g in gw["mlp_w_down"]], axis=1)
    parts = [cols(gw["attn_w_qkv"], QKV_DIM // N_DEV), gw["attn_w_o"].reshape(N_DEV, 128, D_MODEL),
             cols(gw["hgrn_w_in"], 4 * D_MODEL // N_DEV), gw["hgrn_w_o"].reshape(N_DEV, 128, D_MODEL),
             up, down.reshape(N_DEV, 1024, D_MODEL)]
    used = sum(rows for _, rows in PACK_ROWS)
    parts.append(jnp.zeros((N_DEV, PACK_TOTAL - used, D_MODEL), F32))
    return jnp.concatenate(parts, axis=1).astype(BF16)


def _pad_row(a, width=D_MODEL):
    a = a.reshape(1, -1)
    return jnp.pad(a, ((0, 0), (0, width - a.shape[1])))


def _pack_small(d, gn_full):
    rows = [d["mix_norm"], d["mlp_norm"], d["final_norm"].reshape(1, D_MODEL),
            _pad_row(d["attn_b_qkv"], 2 * D_MODEL).reshape(2, D_MODEL), _pad_row(d["attn_sinks"]),
            d["hgrn_lower_bounds"], gn_full.reshape(1, D_MODEL)]
    p = jnp.concatenate(rows, axis=0)
    return jnp.pad(p, ((0, SMALL_ROWS - p.shape[0]), (0, 0)))


def _unpack_small(p, me):
    return dict(
        mix_norm=p[0:2], mlp_norm=p[2:4], final_norm=p[4],
        attn_b_qkv=p[5:7].reshape(1, 2 * D_MODEL)[:, :QKV_DIM], attn_sinks=p[7:8, :N_Q_HEADS],
        hgrn_lower_bounds=p[8:10], hgrn_g_norm=lax.dynamic_slice(p[10:11], (0, me * 128), (1, 128)))


WEIGHT_NAMES = ['mix_norm', 'mlp_norm', 'final_norm', 'attn_w_qkv', 'attn_b_qkv', 'attn_sinks', 'attn_w_o', 'hgrn_w_in',
                'hgrn_g_norm', 'hgrn_w_o', 'hgrn_lower_bounds', 'mlp_w_up', 'mlp_w_down']
SMALL_NAMES = ('mix_norm', 'mlp_norm', 'final_norm', 'attn_b_qkv', 'attn_sinks', 'hgrn_lower_bounds', 'hgrn_g_norm')


def _rotary_tables(positions):
    inv_freq = ROPE_THETA ** (-jnp.arange(0, 2 * ROT_HALF, 2, dtype=F32) / (2 * ROT_HALF))
    ang = positions.astype(F32).reshape(-1, 1) * inv_freq
    cos, sin = jnp.cos(ang), jnp.sin(ang)
    r = jnp.arange(LANES) % HEAD_DIM
    idx = r % ROT_HALF
    c = jnp.where(r < 2 * ROT_HALF, cos[:, idx], 1.0)
    sa = jnp.where((r >= ROT_HALF) & (r < 2 * ROT_HALF), sin[:, idx], 0.0)
    sb = jnp.where(r < ROT_HALF, -sin[:, idx], 0.0)
    return jnp.concatenate([c, sa, sb], axis=1)


def kernel(x, positions, mix_norm, mlp_norm, final_norm, attn_w_qkv, attn_b_qkv, attn_sinks, attn_w_o, hgrn_w_in, hgrn_g_norm, hgrn_w_o, hgrn_lower_bounds, mlp_w_up, mlp_w_down, loss_target, m_mix_norm, m_mlp_norm, m_final_norm, m_attn_w_qkv, m_attn_b_qkv, m_attn_sinks, m_attn_w_o, m_hgrn_w_in, m_hgrn_g_norm, m_hgrn_w_o, m_hgrn_lower_bounds, m_mlp_w_up, m_mlp_w_down, v_mix_norm, v_mlp_norm, v_final_norm, v_attn_w_qkv, v_attn_b_qkv, v_attn_sinks, v_attn_w_o, v_hgrn_w_in, v_hgrn_g_norm, v_hgrn_w_o, v_hgrn_lower_bounds, v_mlp_w_up, v_mlp_w_down):
    w = dict(mix_norm=mix_norm, mlp_norm=mlp_norm, final_norm=final_norm, attn_w_qkv=attn_w_qkv, attn_b_qkv=attn_b_qkv,
             attn_sinks=attn_sinks, attn_w_o=attn_w_o, hgrn_w_in=hgrn_w_in, hgrn_g_norm=hgrn_g_norm, hgrn_w_o=hgrn_w_o,
             hgrn_lower_bounds=hgrn_lower_bounds, mlp_w_up=mlp_w_up, mlp_w_down=mlp_w_down)
    m = dict(mix_norm=m_mix_norm, mlp_norm=m_mlp_norm, final_norm=m_final_norm, attn_w_qkv=m_attn_w_qkv,
             attn_b_qkv=m_attn_b_qkv, attn_sinks=m_attn_sinks, attn_w_o=m_attn_w_o, hgrn_w_in=m_hgrn_w_in,
             hgrn_g_norm=m_hgrn_g_norm, hgrn_w_o=m_hgrn_w_o, hgrn_lower_bounds=m_hgrn_lower_bounds, mlp_w_up=m_mlp_w_up,
             mlp_w_down=m_mlp_w_down)
    v = dict(mix_norm=v_mix_norm, mlp_norm=v_mlp_norm, final_norm=v_final_norm, attn_w_qkv=v_attn_w_qkv,
             attn_b_qkv=v_attn_b_qkv, attn_sinks=v_attn_sinks, attn_w_o=v_attn_w_o, hgrn_w_in=v_hgrn_w_in,
             hgrn_g_norm=v_hgrn_g_norm, hgrn_w_o=v_hgrn_w_o, hgrn_lower_bounds=v_hgrn_lower_bounds, mlp_w_up=v_mlp_w_up,
             mlp_w_down=v_mlp_w_down)
    me = 4 * lax.axis_index("x") + 2 * lax.axis_index("y") + lax.axis_index("c")

    w_pack = _pack_shards(w)
    gn = hgrn_g_norm.reshape(1, 128)
    gn_a = gn.astype(BF16)
    gn_b = (gn - gn_a.astype(F32)).astype(BF16)
    gn_c = (gn - gn_a.astype(F32) - gn_b.astype(F32)).astype(BF16)
    gn_rows = jnp.pad(jnp.concatenate([gn_a, gn_b, gn_c], axis=1), ((0, 15), (0, D_MODEL - 3 * 128)))
    sent = jnp.concatenate([w_pack.astype(BF16), gn_rows], axis=0)
    got = _all_gather(sent)
    full = _unpack_gathered(got[:, :PACK_TOTAL])
    gn_terms = got[:, PACK_TOTAL, :3 * 128].astype(F32).reshape(N_DEV, 3, 128)
    gn_full = ((gn_terms[:, 0] + gn_terms[:, 1]) + gn_terms[:, 2]).reshape(1, D_MODEL)

    x0 = x[0]
    tgt = loss_target[0]
    rot = _rotary_tables(positions)
    row = lambda a: a.reshape(1, -1)

    qkv, h0 = _norm_mm("qkv_proj", x0, row(mix_norm[0]), full["attn_w_qkv"], attn_b_qkv)
    att = _attn_fwd(qkv, rot, attn_sinks)
    x1 = _mm_res("attn_out_proj", att, full["attn_w_o"], x0)
    u0, h1 = _norm_mm("mlp0_up", x1, row(mlp_norm[0]), full["mlp_w_up"][0])
    x2 = _mlp_down("mlp0_down", u0, full["mlp_w_down"][0], x1)
    z, h2 = _norm_mm("hgrn_in_proj", x2, row(mix_norm[1]), full["hgrn_w_in"])
    o_raw, states = _hgrn_fwd(z, hgrn_lower_bounds)
    x3, o2 = _hgrn_out("hgrn_out_proj", o_raw, z, gn_full, full["hgrn_w_o"], x2)
    u1, h3 = _norm_mm("mlp1_up", x3, row(mlp_norm[1]), full["mlp_w_up"][1])
    x4 = _mlp_down("mlp1_down", u1, full["mlp_w_down"][1], x3)
    dx4, loss_part, g_final = _loss_head("loss_head", x4, tgt, row(final_norm))

    gw = {}
    du1, a1 = _mlp_bwd_act("mlp1_bwd_act", dx4, u1, full["mlp_w_down"][1])
    dx3, g_mlp1 = _mm_nt_rmsbwd("mlp1_bwd_in", du1, full["mlp_w_up"][1], x3, row(mlp_norm[1]), dx4)
    gw_down1 = _mm_tn("mlp1_dw_down", a1, dx4)
    gw_up1 = _mm_tn("mlp1_dw_up", h3, du1)

    do_raw, dg, g_gn = _hgrn_out_bwd("hgrn_out_bwd", dx3, o_raw, z, full["hgrn_w_o"], gn_full)
    gw["hgrn_w_o"] = _mm_tn("hgrn_dw_o", o2, dx3)
    dzq, dzf, dzi, g_lb = _hgrn_bwd(z, hgrn_lower_bounds, states, do_raw)
    dz = jnp.concatenate([dzq, dzf, dzi, dg], axis=1)
    dx2, g_mix1 = _mm_nt_rmsbwd("hgrn_in_bwd", dz, full["hgrn_w_in"], x2, row(mix_norm[1]), dx3)
    gw["hgrn_w_in"] = _mm_tn("hgrn_dw_in", h2, dz)

    du0, a0 = _mlp_bwd_act("mlp0_bwd_act", dx2, u0, full["mlp_w_down"][0])
    dx1, g_mlp0 = _mm_nt_rmsbwd("mlp0_bwd_in", du0, full["mlp_w_up"][0], x1, row(mlp_norm[0]), dx2)
    gw_down0 = _mm_tn("mlp0_dw_down", a0, dx2)
    gw_up0 = _mm_tn("mlp0_dw_up", h1, du0)
    gw["mlp_w_up"] = (gw_up0, gw_up1)
    gw["mlp_w_down"] = (gw_down0, gw_down1)

    datt = _mm_nt("attn_out_bwd", dx1, full["attn_w_o"], BF16)
    gw["attn_w_o"] = _mm_tn("attn_dw_o", att, dx1)
    dqkv, g_sink = _attn_bwd(qkv, rot, attn_sinks, datt)
    dx0, g_mix0, g_bqkv = _mm_nt_rmsbwd("qkv_bwd", dqkv, full["attn_w_qkv"], x0, row(mix_norm[0]), dx1, with_colsum=True)
    gw["attn_w_qkv"] = _mm_tn("attn_dw_qkv", h0, dqkv)

    recv = _exchange(_pack_grads(gw))
    g_p, d_p, nm_p, nv_p = _adamw_sum(recv, w_pack, _pack_shards(m), _pack_shards(v))
    big = [_unpack_shards(p, w) for p in (g_p, d_p, nm_p, nv_p)]

    zero_row = jnp.zeros((1, D_MODEL), F32)
    part = _pack_small(dict(
        mix_norm=jnp.concatenate([g_mix0, g_mix1], axis=0), mlp_norm=jnp.concatenate([g_mlp0, g_mlp1], axis=0),
        final_norm=g_final, attn_b_qkv=g_bqkv, attn_sinks=g_sink[:, :N_Q_HEADS],
        hgrn_lower_bounds=jnp.concatenate([zero_row, g_lb], axis=0)), g_gn)

    def spread(a):
        return lax.dynamic_update_slice(zero_row, a.reshape(1, 128), (0, me * 128))

    small_in = [_pack_small({n: d[n] for n in SMALL_NAMES if n != "hgrn_g_norm"}, spread(d["hgrn_g_norm"]))
                for d in (w, m, v)]
    small = [_unpack_small(p, me) for p in _small_sync(part, *small_in)]

    loss = lax.psum(loss_part[0, 0], ("x", "y", "c"))
    outs = [loss, dx0.reshape(x.shape)]
    for grp_big, grp_small in zip(big, small):
        for name in WEIGHT_NAMES:
            val = grp_small[name] if name in SMALL_NAMES else grp_big[name]
            outs.append(val.reshape(w[name].shape))
    return tuple(outs)
```

```python
import functools

import jax
import jax.numpy as jnp
from jax import lax
from jax.experimental import pallas as pl
from jax.experimental.pallas import tpu as pltpu

F32 = jnp.float32
BF16 = jnp.bfloat16

D_MODEL = 1024
HEAD_DIM = 64
N_Q_HEADS = 16
Q_DIM = 1024
KV_DIM = 256
QKV_DIM = 1536
ATT_BLOCK = 128
ROT_HALF = 8
ROPE_THETA = 500000.0
NEG_INF = -1e30
HGRN_HEADS = 8
HGRN_DK = 128
CHUNK = 64
D_FF = 4096
NORM_EPS = 1e-5
N_DEV = 8

ADAM_LR = 0.001
ADAM_B1 = 0.9
ADAM_B2 = 0.999
ADAM_EPS = 1e-08
ADAM_WD = 0.01
ADAM_STEP = 10

LANES = 128
VMEM_LIMIT = 56 * 1024 * 1024

GATHER_FIRST = (("attn_w_qkv", None, 192), ("attn_w_o", None, 128))
GATHER_REST = (("mlp_w_up", 0, 512), ("mlp_w_down", 0, 512), ("hgrn_w_in", None, 512), ("hgrn_w_o", None, 128),
               ("mlp_w_up", 1, 512), ("mlp_w_down", 1, 512))
GRAD_GROUPS = ((("mlp_w_down", 1, 512), ("mlp_w_up", 1, 512), ("hgrn_w_o", None, 128)),
               (("hgrn_w_in", None, 512), ("mlp_w_down", 0, 512), ("mlp_w_up", 0, 512), ("attn_w_o", None, 128)),
               (("attn_w_qkv", None, 192),))
COL_SHARDED = ("attn_w_qkv", "hgrn_w_in", "mlp_w_up")
SMALL_ROWS = 16


def _dot(a, b):
    return jnp.dot(a, b, preferred_element_type=F32)


def _dot_nt(a, b):
    return lax.dot_general(a, b, (((1,), (1,)), ((), ())), preferred_element_type=F32)


def _dot_tn(a, b):
    return lax.dot_general(a, b, (((0,), (0,)), ((), ())), preferred_element_type=F32)


def _params(**kw):
    return pltpu.CompilerParams(vmem_limit_bytes=VMEM_LIMIT, **kw)


def _full_spec(a):
    nd = a.ndim
    return pl.BlockSpec(a.shape, lambda *_: (0,) * nd)


def _row_call(name, body, n_rows, tm, row_ins, full_ins, row_outs, acc_outs=(), carry=(None, None)):
    steps = n_rows // tm
    in_specs = [pl.BlockSpec((tm, w), functools.partial(lambda i, cb: (i, cb), cb=cb)) for _, w, cb in row_ins]
    in_specs += [_full_spec(a) for a in full_ins]
    out_shape = [jax.ShapeDtypeStruct((n_rows, w), dt) for w, dt in row_outs]
    out_specs = [pl.BlockSpec((tm, w), lambda i: (i, 0)) for w, _ in row_outs]
    for shp, dt in acc_outs:
        out_shape.append(jax.ShapeDtypeStruct(shp, dt))
        out_specs.append(pl.BlockSpec(shp, functools.partial(lambda i, nd: (0,) * nd, nd=len(shp))))
    n_in, n_out = len(in_specs), len(out_specs)
    in_specs, out_specs, out_shape, scratch, extra = _carried_specs(*carry, in_specs, out_specs, out_shape, [])

    def wrapped(*refs):
        i = pl.program_id(0)
        own, finish = _carried(carry[0], refs, n_in, n_out, i == 0, i == steps - 1)
        body(*own)
        finish()

    return pl.pallas_call(
        wrapped, name=name, grid=(steps,), in_specs=in_specs, out_specs=out_specs, out_shape=out_shape,
        scratch_shapes=scratch, compiler_params=_params(dimension_semantics=("arbitrary",)),
    )(*[a for a, _, _ in row_ins], *full_ins, *extra)


def _rms(x, gain):
    r = lax.rsqrt(jnp.mean(x * x, axis=-1, keepdims=True) + NORM_EPS)
    xhat = x * r
    return xhat * gain, xhat, r


def _rms_bwd(dy, xhat, r, gain):
    dxhat = dy * gain
    dx = r * (dxhat - xhat * jnp.mean(dxhat * xhat, axis=-1, keepdims=True))
    return dx, dy * xhat


def _norm_mm(name, x, gain, w, bias=None, tm=256):
    T = x.shape[0]
    tm = min(tm, T)
    n = w.shape[1]
    nc = 512
    assert n % nc == 0

    def body(*refs):
        if bias is None:
            x_ref, g_ref, w_ref, y_ref, h_ref = refs
        else:
            x_ref, g_ref, w_ref, b_ref, y_ref, h_ref = refs
        h, _, _ = _rms(x_ref[...], g_ref[...])
        hb = h.astype(BF16)
        h_ref[...] = hb
        for c in range(n // nc):
            sl = slice(c * nc, (c + 1) * nc)
            y = _dot(hb, w_ref[:, sl])
            if bias is not None:
                y = y + b_ref[:, sl]
            y_ref[:, sl] = y

    full = [gain, w] + ([bias] if bias is not None else [])
    return _row_call(name, body, T, tm, [(x, D_MODEL, 0)], full, [(n, F32), (D_MODEL, BF16)])


def _mm_res(name, a, w, res, tm=512):
    T = a.shape[0]
    tm = min(tm, T)

    def body(a_ref, r_ref, w_ref, o_ref):
        o_ref[...] = r_ref[...] + _dot(a_ref[...], w_ref[...])

    return _row_call(name, body, T, tm, [(a, a.shape[1], 0), (res, D_MODEL, 0)], [w], [(D_MODEL, F32)])[0]


def _mlp_down(name, u, w, res, tm=256):
    T = u.shape[0]
    tm = min(tm, T)
    kc = 1024

    def body(u_ref, r_ref, w_ref, o_ref):
        acc = r_ref[...]
        for c in range(D_FF // kc):
            sl = slice(c * kc, (c + 1) * kc)
            a = jnp.maximum(u_ref[:, sl], 0.0)
            acc = acc + _dot((a * a).astype(BF16), w_ref[sl, :])
        o_ref[...] = acc

    return _row_call(name, body, T, tm, [(u, D_FF, 0), (res, D_MODEL, 0)], [w], [(D_MODEL, F32)])[0]


def _hgrn_out(name, o_raw, z, gn, w, res, tm=256):
    T = o_raw.shape[0]
    tm = min(tm, T)

    def body(o_ref, g_ref, r_ref, gn_ref, w_ref, x_ref, a_ref):
        y, _, _ = _rms(o_ref[...], gn_ref[...])
        g = g_ref[...]
        a = (y * (g * jax.nn.sigmoid(g))).astype(BF16)
        a_ref[...] = a
        x_ref[...] = r_ref[...] + _dot(a, w_ref[...])

    return _row_call(name, body, T, tm, [(o_raw, D_MODEL, 0), (z, D_MODEL, 3), (res, D_MODEL, 0)], [gn, w],
                     [(D_MODEL, F32), (D_MODEL, BF16)])


def _loss_head(name, x, target, gain, tm=512):
    T = x.shape[0]
    tm = min(tm, T)

    def body(x_ref, t_ref, g_ref, dx_ref, loss_ref, dg_ref):
        @pl.when(pl.program_id(0) == 0)
        def _():
            loss_ref[...] = jnp.zeros_like(loss_ref)
            dg_ref[...] = jnp.zeros_like(dg_ref)

        gain_v = g_ref[...]
        y, xhat, r = _rms(x_ref[...], gain_v)
        diff = y - t_ref[...]
        row = jnp.sum(diff * diff, axis=-1, keepdims=True) * (1.0 / D_MODEL)
        loss_ref[...] += jnp.broadcast_to(0.5 * jnp.sum(row, axis=0, keepdims=True), loss_ref.shape)
        dy = diff * (1.0 / D_MODEL)
        dx, dgr = _rms_bwd(dy, xhat, r, gain_v)
        dx_ref[...] = dx
        dg_ref[...] += jnp.sum(dgr, axis=0, keepdims=True)

    return _row_call(name, body, T, tm, [(x, D_MODEL, 0), (target, D_MODEL, 0)], [gain], [(D_MODEL, F32)],
                     [((1, LANES), F32), ((1, D_MODEL), F32)])


def _mm_nt_rmsbwd(name, dy, w, x, gain, dres, tm=256, with_colsum=False, carry=(None, None)):
    T = x.shape[0]
    tm = min(tm, T)
    n = dy.shape[1]

    def body(*refs):
        if with_colsum:
            dy_ref, x_ref, dr_ref, w_ref, g_ref, dx_ref, dg_ref, cs_ref = refs
        else:
            dy_ref, x_ref, dr_ref, w_ref, g_ref, dx_ref, dg_ref = refs

        @pl.when(pl.program_id(0) == 0)
        def _():
            dg_ref[...] = jnp.zeros_like(dg_ref)
            if with_colsum:
                cs_ref[...] = jnp.zeros_like(cs_ref)

        dyv = dy_ref[...]
        dh = _dot_nt(dyv.astype(BF16), w_ref[...])
        gain_v = g_ref[...]
        _, xhat, r = _rms(x_ref[...], gain_v)
        dx, dgr = _rms_bwd(dh, xhat, r, gain_v)
        dx_ref[...] = dr_ref[...] + dx
        dg_ref[...] += jnp.sum(dgr, axis=0, keepdims=True)
        if with_colsum:
            cs_ref[...] += jnp.sum(dyv.astype(F32), axis=0, keepdims=True)

    acc = [((1, D_MODEL), F32)] + ([((1, n), F32)] if with_colsum else [])
    return _row_call(name, body, T, tm, [(dy, n, 0), (x, D_MODEL, 0), (dres, D_MODEL, 0)], [w, gain],
                     [(D_MODEL, F32)], acc, carry=carry)


def _mm_nt(name, dy, w, out_dtype, tm=512):
    T = dy.shape[0]
    tm = min(tm, T)
    k = w.shape[0]

    def body(dy_ref, w_ref, o_ref):
        o_ref[...] = _dot_nt(dy_ref[...].astype(BF16), w_ref[...]).astype(out_dtype)

    return _row_call(name, body, T, tm, [(dy, dy.shape[1], 0)], [w], [(k, out_dtype)])[0]


def _mlp_bwd_act(name, dy, u, w_down, tm=256):
    T = u.shape[0]
    tm = min(tm, T)
    kc = 1024

    def body(dy_ref, u_ref, w_ref, du_ref, a_ref):
        dyb = dy_ref[...].astype(BF16)
        for c in range(D_FF // kc):
            sl = slice(c * kc, (c + 1) * kc)
            a = jnp.maximum(u_ref[:, sl], 0.0)
            da = _dot_nt(dyb, w_ref[sl, :])
            du_ref[:, sl] = (da * (2.0 * a)).astype(BF16)
            a_ref[:, sl] = (a * a).astype(BF16)

    return _row_call(name, body, T, tm, [(dy, D_MODEL, 0), (u, D_FF, 0)], [w_down], [(D_FF, BF16), (D_FF, BF16)])


def _hgrn_out_bwd(name, dx, o_raw, z, w, gn, tm=256):
    T = dx.shape[0]
    tm = min(tm, T)

    def body(dx_ref, o_ref, g_ref, w_ref, gn_ref, do_ref, dg_ref, dgn_ref):
        @pl.when(pl.program_id(0) == 0)
        def _():
            dgn_ref[...] = jnp.zeros_like(dgn_ref)

        da = _dot_nt(dx_ref[...].astype(BF16), w_ref[...])
        gn_v = gn_ref[...]
        y, xhat, r = _rms(o_ref[...], gn_v)
        g = g_ref[...]
        sg = jax.nn.sigmoid(g)
        dg_ref[...] = (da * y * (sg * (1.0 + g * (1.0 - sg)))).astype(BF16)
        dyn = da * (g * sg)
        do, dgr = _rms_bwd(dyn, xhat, r, gn_v)
        do_ref[...] = do
        dgn_ref[...] += jnp.sum(dgr, axis=0, keepdims=True)

    return _row_call(name, body, T, tm, [(dx, D_MODEL, 0), (o_raw, D_MODEL, 0), (z, D_MODEL, 3)], [w, gn],
                     [(D_MODEL, F32), (D_MODEL, BF16)], [((1, D_MODEL), F32)])


def _mm_tn(name, a, b, bm=1024, bn=512, tk=2048):
    T, M = a.shape
    N = b.shape[1]
    bm, bn, tk = min(bm, M), min(bn, N), min(tk, T)

    def body(a_ref, b_ref, o_ref):
        @pl.when(pl.program_id(2) == 0)
        def _():
            o_ref[...] = jnp.zeros_like(o_ref)

        o_ref[...] += _dot_tn(a_ref[...].astype(BF16), b_ref[...].astype(BF16))

    return pl.pallas_call(
        body, name=name, grid=(M // bm, N // bn, T // tk),
        in_specs=[pl.BlockSpec((tk, bm), lambda i, j, k: (k, i)), pl.BlockSpec((tk, bn), lambda i, j, k: (k, j))],
        out_specs=pl.BlockSpec((bm, bn), lambda i, j, k: (i, j)),
        out_shape=jax.ShapeDtypeStruct((M, N), F32),
        compiler_params=_params(dimension_semantics=("parallel", "parallel", "arbitrary")),
    )(a, b)


def _rot_fwd(x, tab):
    c, sa, sb = tab[:, :LANES], tab[:, LANES:2 * LANES], tab[:, 2 * LANES:]
    outs = []
    for j in range(x.shape[1] // LANES):
        xs = x[:, j * LANES:(j + 1) * LANES]
        outs.append(xs * c + pltpu.roll(xs, ROT_HALF, 1) * sa + pltpu.roll(xs, LANES - ROT_HALF, 1) * sb)
    return outs


def _rot_bwd(dys, tab):
    c, sa, sb = tab[:, :LANES], tab[:, LANES:2 * LANES], tab[:, 2 * LANES:]
    return [dy * c + pltpu.roll(dy * sa, LANES - ROT_HALF, 1) + pltpu.roll(dy * sb, ROT_HALF, 1) for dy in dys]


def _attn_masks(n):
    qi = lax.broadcasted_iota(jnp.int32, (ATT_BLOCK, 2 * ATT_BLOCK), 0)
    kj = lax.broadcasted_iota(jnp.int32, (ATT_BLOCK, 2 * ATT_BLOCK), 1)
    delta = qi + ATT_BLOCK - kj
    first_key = jnp.where(n > 0, 0, ATT_BLOCK)
    valid = (delta >= 0) & (delta < ATT_BLOCK) & (kj >= first_key)
    lane = lax.broadcasted_iota(jnp.int32, (1, LANES), 1)
    return valid, lane < HEAD_DIM


def _attn_probs(qm, k_use, valid, sink):
    s = _dot_nt(qm, k_use) * (HEAD_DIM ** -0.5)
    s = jnp.where(valid, s, NEG_INF)
    m = jnp.maximum(jnp.max(s, axis=-1, keepdims=True), sink)
    e = jnp.exp(s - m)
    es = jnp.exp(sink - m)
    inv = 1.0 / (jnp.sum(e, axis=-1, keepdims=True) + es)
    return e * inv, es * inv


def _attn_specs(nb):
    prev = lambda n: jnp.maximum(jnp.minimum(n, nb - 1) - 1, 0)
    cur = lambda n: jnp.minimum(n, nb - 1)
    return [
        pl.BlockSpec((ATT_BLOCK, Q_DIM), lambda n: (cur(n), 0)),
        pl.BlockSpec((ATT_BLOCK, KV_DIM), lambda n: (prev(n), 4)),
        pl.BlockSpec((ATT_BLOCK, KV_DIM), lambda n: (cur(n), 4)),
        pl.BlockSpec((ATT_BLOCK, KV_DIM), lambda n: (prev(n), 5)),
        pl.BlockSpec((ATT_BLOCK, KV_DIM), lambda n: (cur(n), 5)),
        pl.BlockSpec((ATT_BLOCK, 3 * LANES), lambda n: (prev(n), 0)),
        pl.BlockSpec((ATT_BLOCK, 3 * LANES), lambda n: (cur(n), 0)),
        pl.BlockSpec(memory_space=pltpu.SMEM),
    ]


def _kv_band(kp_ref, kc_ref, vp_ref, vc_ref, tp_ref, tc_ref):
    kp = _rot_fwd(kp_ref[...], tp_ref[...])
    kc = _rot_fwd(kc_ref[...], tc_ref[...])
    ks, vs = [], []
    for j in range(KV_DIM // LANES):
        kb = jnp.concatenate([kp[j], kc[j]], axis=0)
        vb = jnp.concatenate([vp_ref[:, j * LANES:(j + 1) * LANES], vc_ref[:, j * LANES:(j + 1) * LANES]], axis=0)
        ks.append((kb.astype(BF16), pltpu.roll(kb, HEAD_DIM, 1).astype(BF16)))
        vs.append((vb.astype(BF16), pltpu.roll(vb, HEAD_DIM, 1).astype(BF16)))
    return ks, vs


def _attn_fwd(qkv, rot, sinks, carry=(None, None)):
    T = qkv.shape[0]
    nb = T // ATT_BLOCK

    def body(*refs):
        n = pl.program_id(0)
        own, finish = _carried(carry[0], refs, 8, 1, n == 0, n == nb - 1)
        q_ref, kp_ref, kc_ref, vp_ref, vc_ref, tp_ref, tc_ref, sink_ref, o_ref = own
        valid, low = _attn_masks(n)
        ks, vs = _kv_band(kp_ref, kc_ref, vp_ref, vc_ref, tp_ref, tc_ref)
        qs = _rot_fwd(q_ref[...], tc_ref[...])
        for p in range(Q_DIM // LANES):
            kpair, khalf = p // 4, (p // 2) % 2
            outs = []
            for hf in range(2):
                qm = jnp.where(low if hf == 0 else ~low, qs[p], 0.0).astype(BF16)
                sw = 0 if khalf == hf else 1
                pr, _ = _attn_probs(qm, ks[kpair][sw], valid, sink_ref[0, 2 * p + hf])
                outs.append(_dot(pr.astype(BF16), vs[kpair][sw]))
            o_ref[:, p * LANES:(p + 1) * LANES] = jnp.where(low, outs[0], outs[1]).astype(BF16)
        finish()

    in_specs, out_specs, out_shape, scratch, extra = _carried_specs(
        *carry, _attn_specs(nb), [pl.BlockSpec((ATT_BLOCK, Q_DIM), lambda n: (n, 0))],
        [jax.ShapeDtypeStruct((T, Q_DIM), BF16)], [])
    return pl.pallas_call(
        body, name="attn_fwd", grid=(nb,), in_specs=in_specs, out_specs=out_specs, out_shape=out_shape,
        scratch_shapes=scratch, compiler_params=_params(dimension_semantics=("arbitrary",)),
    )(qkv, qkv, qkv, qkv, qkv, rot, rot, sinks, *extra)


def _attn_bwd(qkv, rot, sinks, dout, carry=(None, None)):
    T = qkv.shape[0]
    nb = T // ATT_BLOCK
    npair = KV_DIM // LANES

    def body(*refs):
        n = pl.program_id(0)
        own, finish = _carried(carry[0], refs, 9, 2, n == 0, n == nb)
        (q_ref, kp_ref, kc_ref, vp_ref, vc_ref, tp_ref, tc_ref, sink_ref, do_ref, dqkv_ref, dsink_ref,
         dq_c, dk_c, dv_c) = own

        @pl.when(n == 0)
        def _():
            dq_c[...] = jnp.zeros_like(dq_c)
            dk_c[...] = jnp.zeros_like(dk_c)
            dv_c[...] = jnp.zeros_like(dv_c)
            dsink_ref[...] = jnp.zeros_like(dsink_ref)

        def flush(dk_prev, dv_prev, tab_ref):
            dqkv_ref[:, :Q_DIM] = dq_c[...]
            dk = _rot_bwd([dk_c[:, j * LANES:(j + 1) * LANES] + dk_prev[j] for j in range(npair)], tab_ref[...])
            for j in range(npair):
                dqkv_ref[:, Q_DIM + j * LANES:Q_DIM + (j + 1) * LANES] = dk[j]
                dqkv_ref[:, Q_DIM + KV_DIM + j * LANES:Q_DIM + KV_DIM + (j + 1) * LANES] = (
                    dv_c[:, j * LANES:(j + 1) * LANES] + dv_prev[j])

        @pl.when(n < nb)
        def _():
            valid, low = _attn_masks(n)
            lane = lax.broadcasted_iota(jnp.int32, (1, LANES), 1)
            ks, vs = _kv_band(kp_ref, kc_ref, vp_ref, vc_ref, tp_ref, tc_ref)
            qs = _rot_fwd(q_ref[...], tc_ref[...])
            dk_acc = [jnp.zeros((2 * ATT_BLOCK, LANES), F32) for _ in range(npair)]
            dv_acc = [jnp.zeros((2 * ATT_BLOCK, LANES), F32) for _ in range(npair)]
            dsink = jnp.zeros((1, LANES), F32)
            dqs = []
            for p in range(Q_DIM // LANES):
                kpair, khalf = p // 4, (p // 2) % 2
                do_pair = do_ref[:, p * LANES:(p + 1) * LANES]
                dq_h = []
                for hf in range(2):
                    sel = low if hf == 0 else ~low
                    qm = jnp.where(sel, qs[p], 0.0).astype(BF16)
                    dom = jnp.where(sel, do_pair, 0.0).astype(BF16)
                    sw = 0 if khalf == hf else 1
                    k_use, v_use = ks[kpair][sw], vs[kpair][sw]
                    pr, ps = _attn_probs(qm, k_use, valid, sink_ref[0, 2 * p + hf])
                    dp = _dot_nt(dom, v_use)
                    dd = jnp.sum(pr * dp, axis=-1, keepdims=True)
                    ds = (pr * (dp - dd) * (HEAD_DIM ** -0.5)).astype(BF16)
                    dq_h.append(_dot(ds, k_use))
                    dk_u = _dot_tn(ds, qm)
                    dv_u = _dot_tn(pr.astype(BF16), dom)
                    if sw:
                        dk_u = pltpu.roll(dk_u, HEAD_DIM, 1)
                        dv_u = pltpu.roll(dv_u, HEAD_DIM, 1)
                    dk_acc[kpair] = dk_acc[kpair] + dk_u
                    dv_acc[kpair] = dv_acc[kpair] + dv_u
                    dsink = dsink + jnp.where(lane == 2 * p + hf, -jnp.sum(ps * dd, axis=0, keepdims=True), 0.0)
                dqs.append(jnp.where(low, dq_h[0], dq_h[1]))
            flush([a[:ATT_BLOCK] for a in dk_acc], [a[:ATT_BLOCK] for a in dv_acc], tp_ref)
            dq = _rot_bwd(dqs, tc_ref[...])
            for p in range(Q_DIM // LANES):
                dq_c[:, p * LANES:(p + 1) * LANES] = dq[p]
            for j in range(npair):
                dk_c[:, j * LANES:(j + 1) * LANES] = dk_acc[j][ATT_BLOCK:]
                dv_c[:, j * LANES:(j + 1) * LANES] = dv_acc[j][ATT_BLOCK:]
            dsink_ref[...] += dsink

        @pl.when(n == nb)
        def _():
            zero = [jnp.zeros((ATT_BLOCK, LANES), F32) for _ in range(npair)]
            flush(zero, zero, tc_ref)

        finish()

    do_spec = pl.BlockSpec((ATT_BLOCK, Q_DIM), lambda n: (jnp.minimum(n, nb - 1), 0))
    in_specs, out_specs, out_shape, scratch, extra = _carried_specs(
        *carry, _attn_specs(nb) + [do_spec],
        [pl.BlockSpec((ATT_BLOCK, QKV_DIM), lambda n: (jnp.maximum(n - 1, 0), 0)),
         pl.BlockSpec((1, LANES), lambda n: (0, 0))],
        [jax.ShapeDtypeStruct((T, QKV_DIM), F32), jax.ShapeDtypeStruct((1, LANES), F32)],
        [pltpu.VMEM((ATT_BLOCK, Q_DIM), F32), pltpu.VMEM((ATT_BLOCK, KV_DIM), F32),
         pltpu.VMEM((ATT_BLOCK, KV_DIM), F32)])
    return pl.pallas_call(
        body, name="attn_bwd", grid=(nb + 1,), in_specs=in_specs, out_specs=out_specs, out_shape=out_shape,
        scratch_shapes=scratch, compiler_params=_params(dimension_semantics=("arbitrary",)),
    )(qkv, qkv, qkv, qkv, qkv, rot, rot, sinks, dout, *extra)


LEVELS = (32, 16, 8)
DIAG = 8


def _lower_bound(lb_ref):
    l0, l1 = lb_ref[0:1, :], lb_ref[1:2, :]
    mx = jnp.maximum(l0, l1)
    e0, e1 = jnp.exp(l0 - mx), jnp.exp(l1 - mx)
    return e1 / (e0 + e1)


def _cumsum_rows(x, row):
    for sh in (1, 2, 4, 8, 16, 32):
        x = x + jnp.where(row >= sh, pltpu.roll(x, sh, 0), 0.0)
    return x


def _rev_cumsum_rows(x, row):
    for sh in (1, 2, 4, 8, 16, 32):
        x = x + jnp.where(row < CHUNK - sh, pltpu.roll(x, CHUNK - sh, 0), 0.0)
    return x


def _level_masks():
    t = lax.broadcasted_iota(jnp.int32, (CHUNK, CHUNK), 0)
    s = lax.broadcasted_iota(jnp.int32, (CHUNK, CHUNK), 1)
    return [((t & h) != 0) & ((s & h) == 0) & ((t ^ s) < 2 * h) for h in LEVELS]


def _level_scale(b, b_scr, h):
    parts = [jnp.broadcast_to(b_scr[pl.ds(j * 2 * h + h - 1, 1), :], (2 * h, HGRN_DK)) for j in range(CHUNK // (2 * h))]
    mid = parts[0] if len(parts) == 1 else jnp.concatenate(parts, axis=0)
    return jnp.exp(-jnp.abs(b - mid))


def _hgrn_gates(zq, zf, lb):
    sq = jax.nn.sigmoid(zq)
    q = zq * sq
    sg = jax.nn.sigmoid(zf)
    forget = lb + (1.0 - lb) * sg
    return q, sq, sg, forget, 1.0 - forget, jnp.log(forget)


def _hgrn_specs(T, rb, rev):
    nr = T // rb
    ri = (lambda r: nr - 1 - r) if rev else (lambda r: r)
    return nr, ri, [
        pl.BlockSpec((rb, HGRN_DK), lambda h, r: (ri(r), h)),
        pl.BlockSpec((rb, HGRN_DK), lambda h, r: (ri(r), HGRN_HEADS + h)),
        pl.BlockSpec((rb, HGRN_DK), lambda h, r: (ri(r), 2 * HGRN_HEADS + h)),
        pl.BlockSpec((2, HGRN_DK), lambda h, r: (0, h)),
    ]


def _hgrn_fwd(z, lb_raw, rb=512):
    T = z.shape[0]
    rb = min(rb, T)
    ncb = rb // CHUNK
    nr, ri, in_specs = _hgrn_specs(T, rb, False)

    def body(zq_ref, zf_ref, zi_ref, lb_ref, o_ref, st_ref, state, b_scr):
        @pl.when(pl.program_id(1) == 0)
        def _():
            state[...] = jnp.zeros_like(state)

        lb = _lower_bound(lb_ref)
        row = lax.broadcasted_iota(jnp.int32, (CHUNK, HGRN_DK), 0)
        masks = _level_masks()

        def chunk(c, carry):
            rows = pl.ds(pl.multiple_of(c * CHUNK, CHUNK), CHUNK)
            q, _, _, _, k, lf = _hgrn_gates(zq_ref[rows, :], zf_ref[rows, :], lb)
            v = zi_ref[rows, :]
            b = _cumsum_rows(lf, row)
            b_scr[...] = b
            st = state[...]
            st_ref[c, 0] = st
            o = _dot_nt((q * jnp.exp(b)).astype(BF16), st.astype(BF16))
            sc = jnp.zeros((CHUNK, CHUNK), F32)
            for h, mask in zip(LEVELS, masks):
                e = _level_scale(b, b_scr, h)
                sc = sc + jnp.where(mask, _dot_nt((q * e).astype(BF16), (k * e).astype(BF16)), 0.0)
            o = o + _dot(sc.astype(BF16), v.astype(BF16))
            for d in range(DIAG):
                if d == 0:
                    w = q * k
                    vr = v
                else:
                    ok = (row & (DIAG - 1)) >= d
                    w = jnp.where(ok, q * pltpu.roll(k, d, 0) * jnp.exp(jnp.where(ok, b - pltpu.roll(b, d, 0), 0.0)), 0.0)
                    vr = pltpu.roll(v, d, 0)
                o = o + jnp.sum(w, axis=-1, keepdims=True) * vr
            o_ref[rows, :] = o
            b_last = b_scr[pl.ds(CHUNK - 1, 1), :]
            kd = k * jnp.exp(b_last - b)
            state[...] = st * jnp.exp(b_last) + _dot_tn(v.astype(BF16), kd.astype(BF16))
            return carry

        lax.fori_loop(0, ncb, chunk, 0)

    return pl.pallas_call(
        body, name="hgrn_fwd", grid=(HGRN_HEADS, nr), in_specs=in_specs,
        out_specs=[pl.BlockSpec((rb, HGRN_DK), lambda h, r: (r, h)),
                   pl.BlockSpec((ncb, 1, HGRN_DK, HGRN_DK), lambda h, r: (r, h, 0, 0))],
        out_shape=[jax.ShapeDtypeStruct((T, D_MODEL), F32),
                   jax.ShapeDtypeStruct((T // CHUNK, HGRN_HEADS, HGRN_DK, HGRN_DK), F32)],
        scratch_shapes=[pltpu.VMEM((HGRN_DK, HGRN_DK), F32), pltpu.VMEM((CHUNK, HGRN_DK), F32)],
        compiler_params=_params(dimension_semantics=("arbitrary", "arbitrary")),
    )(z, z, z, lb_raw)


def _hgrn_bwd(z, lb_raw, states, do, rb=512, carry=(None, None)):
    T = z.shape[0]
    rb = min(rb, T)
    ncb = rb // CHUNK
    nr, ri, in_specs = _hgrn_specs(T, rb, True)
    in_specs += [pl.BlockSpec((ncb, 1, HGRN_DK, HGRN_DK), lambda h, r: (ri(r), h, 0, 0)),
                 pl.BlockSpec((rb, HGRN_DK), lambda h, r: (ri(r), h))]

    def body(*refs):
        hh, rr = pl.program_id(0), pl.program_id(1)
        own, finish = _carried(carry[0], refs, 6, 4, (hh == 0) & (rr == 0), (hh == HGRN_HEADS - 1) & (rr == nr - 1))
        zq_ref, zf_ref, zi_ref, lb_ref, st_ref, do_ref, dq_ref, df_ref, di_ref, dlb_ref, dstate, b_scr = own

        @pl.when(rr == 0)
        def _():
            dstate[...] = jnp.zeros_like(dstate)
            dlb_ref[...] = jnp.zeros_like(dlb_ref)

        lb = _lower_bound(lb_ref)
        row = lax.broadcasted_iota(jnp.int32, (CHUNK, HGRN_DK), 0)
        masks = _level_masks()

        def chunk(ci, dlb):
            c = ncb - 1 - ci
            rows = pl.ds(pl.multiple_of(c * CHUNK, CHUNK), CHUNK)
            zq = zq_ref[rows, :]
            q, sq, sg, forget, k, lf = _hgrn_gates(zq, zf_ref[rows, :], lb)
            v = zi_ref[rows, :]
            dov = do_ref[rows, :]
            b = _cumsum_rows(lf, row)
            b_scr[...] = b
            st = st_ref[c, 0]
            dst = dstate[...]
            b_last = b_scr[pl.ds(CHUNK - 1, 1), :]
            eb = jnp.exp(b)
            ebb = jnp.exp(b_last - b)
            e_last = jnp.exp(b_last)
            dob, vb, stb, dstb = dov.astype(BF16), v.astype(BF16), st.astype(BF16), dst.astype(BF16)
            dq = eb * _dot(dob, stb)
            dv = _dot_nt((k * ebb).astype(BF16), dstb)
            dk = ebb * _dot(vb, dstb)
            extra = e_last * jnp.sum(dst * st, axis=0, keepdims=True) + jnp.sum(k * dk, axis=0, keepdims=True)
            da = _dot_nt(dob, vb)
            sc = jnp.zeros((CHUNK, CHUNK), F32)
            for h, mask in zip(LEVELS, masks):
                e = _level_scale(b, b_scr, h)
                qs, ks = (q * e).astype(BF16), (k * e).astype(BF16)
                dam = jnp.where(mask, da, 0.0).astype(BF16)
                dq = dq + e * _dot(dam, ks)
                dk = dk + e * _dot_tn(dam, qs)
                sc = sc + jnp.where(mask, _dot_nt(qs, ks), 0.0)
            dv = dv + _dot_tn(sc.astype(BF16), dob)
            for d in range(DIAG):
                if d == 0:
                    dad = jnp.sum(dov * v, axis=-1, keepdims=True)
                    dq = dq + dad * k
                    dk = dk + dad * q
                    dv = dv + jnp.sum(q * k, axis=-1, keepdims=True) * dov
                else:
                    ok = (row & (DIAG - 1)) >= d
                    w = jnp.where(ok, jnp.exp(jnp.where(ok, b - pltpu.roll(b, d, 0), 0.0)), 0.0)
                    kr = pltpu.roll(k, d, 0)
                    dad = jnp.sum(dov * pltpu.roll(v, d, 0), axis=-1, keepdims=True)
                    ad = jnp.sum(q * kr * w, axis=-1, keepdims=True)
                    dq = dq + dad * kr * w
                    dk = dk + pltpu.roll(dad * q * w, CHUNK - d, 0)
                    dv = dv + pltpu.roll(ad * dov, CHUNK - d, 0)
            dlf = _rev_cumsum_rows(q * dq - k * dk, row) + extra
            dstate[...] = dst * e_last + _dot_tn(dob, (q * eb).astype(BF16))
            dforget = dlf / forget - dk
            dq_ref[rows, :] = (dq * (sq * (1.0 + zq * (1.0 - sq)))).astype(BF16)
            df_ref[rows, :] = (dforget * (1.0 - lb) * sg * (1.0 - sg)).astype(BF16)
            di_ref[rows, :] = dv.astype(BF16)
            return dlb + jnp.sum(dforget * (1.0 - sg), axis=0, keepdims=True)

        dlb_ref[...] += lax.fori_loop(0, ncb, chunk, jnp.zeros((1, HGRN_DK), F32))
        finish()

    blk = pl.BlockSpec((rb, HGRN_DK), lambda h, r: (ri(r), h))
    in_specs, out_specs, out_shape, scratch, extra = _carried_specs(
        *carry, in_specs, [blk, blk, blk, pl.BlockSpec((1, HGRN_DK), lambda h, r: (0, h))],
        [jax.ShapeDtypeStruct((T, D_MODEL), BF16)] * 3 + [jax.ShapeDtypeStruct((1, D_MODEL), F32)],
        [pltpu.VMEM((HGRN_DK, HGRN_DK), F32), pltpu.VMEM((CHUNK, HGRN_DK), F32)])
    return pl.pallas_call(
        body, name="hgrn_bwd", grid=(HGRN_HEADS, nr), in_specs=in_specs, out_specs=out_specs, out_shape=out_shape,
        scratch_shapes=scratch, compiler_params=_params(dimension_semantics=("arbitrary", "arbitrary")),
    )(z, z, z, lb_raw, states, do, *extra)


MESH = pl.DeviceIdType.MESH
ANY = pl.BlockSpec(memory_space=pl.ANY)


def _place():
    return lax.axis_index("x"), lax.axis_index("y"), lax.axis_index("c")


def _sems():
    return [pltpu.SemaphoreType.DMA((7,)), pltpu.SemaphoreType.DMA((7,)), pltpu.SemaphoreType.DMA]


class _Gather:
    def __init__(self, x_ref, out_ref, send_sems, recv_sems, local_sem):
        self.x_ref, self.out_ref, self.send_sems, self.recv_sems, self.local_sem = (
            x_ref, out_ref, send_sems, recv_sems, local_sem)
        x, y, c = _place()
        self.c = c
        self.me, self.sibling = (x, y, c), (x, y, 1 - c)
        self.chips = [(1 - x, y), (x, 1 - y), (1 - x, 1 - y)]

    def rows(self, px, py, pc):
        return self.out_ref.at[4 * px + 2 * py + pc]

    def copy(self, k, block, to, from_input=False):
        return pltpu.make_async_remote_copy(
            src_ref=self.x_ref if from_input else self.rows(*block), dst_ref=self.rows(*block),
            send_sem=self.send_sems.at[k], recv_sem=self.recv_sems.at[k], device_id=to, device_id_type=MESH)

    def first(self):
        out = [self.copy(0, self.me, self.sibling, from_input=True)]
        return out + [self.copy(1 + j, self.me, (*chip, self.c), from_input=True) for j, chip in enumerate(self.chips)]

    def start(self):
        pltpu.make_async_copy(self.x_ref, self.rows(*self.me), self.local_sem).start()
        for cp in self.first():
            cp.start()

    def finish(self):
        passed = [self.copy(4 + j, (*chip, self.c), self.sibling) for j, chip in enumerate(self.chips)]
        for j, chip in enumerate(self.chips):
            self.copy(1 + j, (*chip, self.c), self.me).wait_recv()
            passed[j].start()
        self.copy(0, self.sibling, self.me).wait_recv()
        for j, chip in enumerate(self.chips):
            self.copy(4 + j, (*chip, 1 - self.c), self.me).wait_recv()
        for cp in self.first() + passed:
            cp.wait_send()
        pltpu.make_async_copy(self.x_ref, self.rows(*self.me), self.local_sem).wait()


def _all_gather(name, x_shard):
    def body(*refs):
        g = _Gather(*refs)
        g.start()
        g.finish()

    return pl.pallas_call(
        body, name=name, out_shape=jax.ShapeDtypeStruct((N_DEV,) + x_shard.shape, x_shard.dtype),
        in_specs=[ANY], out_specs=ANY, scratch_shapes=_sems(),
    )(x_shard)


def _peers(x, y, c):
    out = []
    for k in range(1, N_DEV):
        px = 1 - x if k & 4 else x
        py = 1 - y if k & 2 else y
        pc = 1 - c if k & 1 else c
        out.append((k, (px, py, pc), 4 * px + 2 * py + pc))
    return out


class _Exchange:
    def __init__(self, g_ref, recv_ref, send_sems, recv_sems, local_sem):
        x, y, c = _place()
        me = 4 * x + 2 * y + c
        self.local = pltpu.make_async_copy(g_ref.at[me], recv_ref.at[me], local_sem)
        self.copies = [
            pltpu.make_async_remote_copy(
                src_ref=g_ref.at[pidx], dst_ref=recv_ref.at[me], send_sem=send_sems.at[k - 1],
                recv_sem=recv_sems.at[k - 1], device_id=peer, device_id_type=MESH)
            for k, peer, pidx in _peers(x, y, c)]

    def start(self):
        self.local.start()
        for cp in self.copies:
            cp.start()

    def finish(self):
        for cp in self.copies:
            cp.wait()
        self.local.wait()


def _carried(kind, refs, n_in, n_out, first, last):
    if kind is None:
        return refs, lambda: None
    ins, rest = refs[:n_in], refs[n_in + 1:]
    outs, scratch = rest[:n_out], rest[n_out + 1:]
    sems = scratch[len(scratch) - 3:]
    op = kind(refs[n_in], rest[n_out], *sems)

    @pl.when(first)
    def _():
        op.start()

    def finish():
        @pl.when(last)
        def _():
            op.finish()

    return tuple(ins) + tuple(outs) + tuple(scratch[:len(scratch) - 3]), finish


def _carried_specs(kind, arr, in_specs, out_specs, out_shape, scratch):
    if kind is None:
        return in_specs, out_specs, out_shape, scratch, []
    shape = arr.shape if kind is _Exchange else (N_DEV,) + arr.shape
    return (list(in_specs) + [ANY], list(out_specs) + [ANY],
            list(out_shape) + [jax.ShapeDtypeStruct(shape, arr.dtype)], list(scratch) + _sems(), [arr])


def _adamw(w, g, m, v):
    m = ADAM_B1 * m + (1.0 - ADAM_B1) * g
    v = ADAM_B2 * v + (1.0 - ADAM_B2) * (g * g)
    m_hat = m / (1.0 - ADAM_B1 ** ADAM_STEP)
    v_hat = v / (1.0 - ADAM_B2 ** ADAM_STEP)
    delta = -ADAM_LR * (m_hat / (jnp.sqrt(v_hat) + ADAM_EPS) + ADAM_WD * w)
    return delta, m, v


def _adamw_sum(name, recv, w, m, v):
    R, C = w.shape
    tm = 128 if R % 128 == 0 else 64
    assert R % tm == 0

    def body(r_ref, w_ref, m_ref, v_ref, g_ref, d_ref, nm_ref, nv_ref):
        g = r_ref[0].astype(F32)
        for s in range(1, N_DEV):
            g = g + r_ref[s].astype(F32)
        g_ref[...] = g
        d_ref[...], nm_ref[...], nv_ref[...] = _adamw(w_ref[...], g, m_ref[...], v_ref[...])

    blk = pl.BlockSpec((tm, C), lambda i: (i, 0))
    return pl.pallas_call(
        body, name=name, grid=(R // tm,),
        in_specs=[pl.BlockSpec((N_DEV, tm, C), lambda i: (0, i, 0)), blk, blk, blk],
        out_specs=[blk] * 4, out_shape=[jax.ShapeDtypeStruct((R, C), F32)] * 4,
        compiler_params=_params(dimension_semantics=("arbitrary",)),
    )(recv, w, m, v)


def _small_sync(part, w, m, v):
    def body(p_ref, w_ref, m_ref, v_ref, g_ref, d_ref, nm_ref, nv_ref, gath, send_sems, recv_sems):
        x, y, c = _place()
        me = 4 * x + 2 * y + c
        gath[me] = p_ref[...]
        copies = []
        for k, peer, _ in _peers(x, y, c):
            cp = pltpu.make_async_remote_copy(
                src_ref=p_ref, dst_ref=gath.at[me], send_sem=send_sems.at[k - 1], recv_sem=recv_sems.at[k - 1],
                device_id=peer, device_id_type=MESH)
            cp.start()
            copies.append(cp)
        for cp in copies:
            cp.wait()
        g = gath[0]
        for s in range(1, N_DEV):
            g = g + gath[s]
        wv = w_ref[...]
        l0, l1 = w_ref[8:9, :], w_ref[9:10, :]
        mx = jnp.maximum(l0, l1)
        e0, e1 = jnp.exp(l0 - mx), jnp.exp(l1 - mx)
        g9 = g[9:10, :] * (e0 / (e0 + e1)) * (e1 / (e0 + e1))
        row = lax.broadcasted_iota(jnp.int32, g.shape, 0)
        g = jnp.where(row == 9, g9, jnp.where(row == 8, -g9, g))
        g_ref[...] = g
        d_ref[...], nm_ref[...], nv_ref[...] = _adamw(wv, g, m_ref[...], v_ref[...])

    vm = pl.BlockSpec(memory_space=pltpu.VMEM)
    return pl.pallas_call(
        body, name="small_params_sync", in_specs=[vm] * 4, out_specs=[vm] * 4,
        out_shape=[jax.ShapeDtypeStruct(part.shape, F32)] * 4,
        scratch_shapes=[pltpu.VMEM((N_DEV,) + part.shape, F32), pltpu.SemaphoreType.DMA((7,)),
                        pltpu.SemaphoreType.DMA((7,))],
    )(part, w, m, v)


def _piece(d, name, layer):
    return d[name][0 if layer is None else layer]


def _pack_pieces(d, pieces):
    return jnp.concatenate([_piece(d, name, layer).reshape(rows, D_MODEL) for name, layer, rows in pieces], axis=0)


def _unpack_pieces(p, pieces, like, out):
    r0 = 0
    for name, layer, rows in pieces:
        out[name, layer] = p[r0:r0 + rows].reshape(_piece(like, name, layer).shape)
        r0 += rows


def _unpack_gathered(wg, pieces, out):
    r0 = 0
    for name, layer, rows in pieces:
        a = wg[:, r0:r0 + rows]
        r0 += rows
        if name in COL_SHARDED:
            a = a.reshape(N_DEV, D_MODEL, rows).transpose(1, 0, 2).reshape(D_MODEL, N_DEV * rows)
        else:
            a = a.reshape(N_DEV * rows, D_MODEL)
        out[name, layer] = a


def _pack_grads(gw, pieces):
    parts = []
    for name, layer, rows in pieces:
        a = gw[name, layer]
        if name in COL_SHARDED:
            a = a.reshape(D_MODEL, N_DEV, rows).transpose(1, 0, 2)
        parts.append(a.reshape(N_DEV, rows, D_MODEL))
    return jnp.concatenate(parts, axis=1).astype(BF16)


def _pad_row(a, width=D_MODEL):
    a = a.reshape(1, -1)
    return jnp.pad(a, ((0, 0), (0, width - a.shape[1])))


def _pack_small(d, gn_full):
    rows = [d["mix_norm"], d["mlp_norm"], d["final_norm"].reshape(1, D_MODEL),
            _pad_row(d["attn_b_qkv"], 2 * D_MODEL).reshape(2, D_MODEL), _pad_row(d["attn_sinks"]),
            d["hgrn_lower_bounds"], gn_full.reshape(1, D_MODEL)]
    p = jnp.concatenate(rows, axis=0)
    return jnp.pad(p, ((0, SMALL_ROWS - p.shape[0]), (0, 0)))


def _unpack_small(p, me):
    return dict(
        mix_norm=p[0:2], mlp_norm=p[2:4], final_norm=p[4],
        attn_b_qkv=p[5:7].reshape(1, 2 * D_MODEL)[:, :QKV_DIM], attn_sinks=p[7:8, :N_Q_HEADS],
        hgrn_lower_bounds=p[8:10], hgrn_g_norm=lax.dynamic_slice(p[10:11], (0, me * 128), (1, 128)))


WEIGHT_NAMES = ['mix_norm', 'mlp_norm', 'final_norm', 'attn_w_qkv', 'attn_b_qkv', 'attn_sinks', 'attn_w_o', 'hgrn_w_in',
                'hgrn_g_norm', 'hgrn_w_o', 'hgrn_lower_bounds', 'mlp_w_up', 'mlp_w_down']
SMALL_NAMES = ('mix_norm', 'mlp_norm', 'final_norm', 'attn_b_qkv', 'attn_sinks', 'hgrn_lower_bounds', 'hgrn_g_norm')


def _rotary_tables(positions):
    inv_freq = ROPE_THETA ** (-jnp.arange(0, 2 * ROT_HALF, 2, dtype=F32) / (2 * ROT_HALF))
    ang = positions.astype(F32).reshape(-1, 1) * inv_freq
    cos, sin = jnp.cos(ang), jnp.sin(ang)
    r = jnp.arange(LANES) % HEAD_DIM
    idx = r % ROT_HALF
    c = jnp.where(r < 2 * ROT_HALF, cos[:, idx], 1.0)
    sa = jnp.where((r >= ROT_HALF) & (r < 2 * ROT_HALF), sin[:, idx], 0.0)
    sb = jnp.where(r < ROT_HALF, -sin[:, idx], 0.0)
    return jnp.concatenate([c, sa, sb], axis=1)


def kernel(x, positions, mix_norm, mlp_norm, final_norm, attn_w_qkv, attn_b_qkv, attn_sinks, attn_w_o, hgrn_w_in, hgrn_g_norm, hgrn_w_o, hgrn_lower_bounds, mlp_w_up, mlp_w_down, loss_target, m_mix_norm, m_mlp_norm, m_final_norm, m_attn_w_qkv, m_attn_b_qkv, m_attn_sinks, m_attn_w_o, m_hgrn_w_in, m_hgrn_g_norm, m_hgrn_w_o, m_hgrn_lower_bounds, m_mlp_w_up, m_mlp_w_down, v_mix_norm, v_mlp_norm, v_final_norm, v_attn_w_qkv, v_attn_b_qkv, v_attn_sinks, v_attn_w_o, v_hgrn_w_in, v_hgrn_g_norm, v_hgrn_w_o, v_hgrn_lower_bounds, v_mlp_w_up, v_mlp_w_down):
    w = dict(mix_norm=mix_norm, mlp_norm=mlp_norm, final_norm=final_norm, attn_w_qkv=attn_w_qkv, attn_b_qkv=attn_b_qkv,
             attn_sinks=attn_sinks, attn_w_o=attn_w_o, hgrn_w_in=hgrn_w_in, hgrn_g_norm=hgrn_g_norm, hgrn_w_o=hgrn_w_o,
             hgrn_lower_bounds=hgrn_lower_bounds, mlp_w_up=mlp_w_up, mlp_w_down=mlp_w_down)
    m = dict(mix_norm=m_mix_norm, mlp_norm=m_mlp_norm, final_norm=m_final_norm, attn_w_qkv=m_attn_w_qkv,
             attn_b_qkv=m_attn_b_qkv, attn_sinks=m_attn_sinks, attn_w_o=m_attn_w_o, hgrn_w_in=m_hgrn_w_in,
             hgrn_g_norm=m_hgrn_g_norm, hgrn_w_o=m_hgrn_w_o, hgrn_lower_bounds=m_hgrn_lower_bounds, mlp_w_up=m_mlp_w_up,
             mlp_w_down=m_mlp_w_down)
    v = dict(mix_norm=v_mix_norm, mlp_norm=v_mlp_norm, final_norm=v_final_norm, attn_w_qkv=v_attn_w_qkv,
             attn_b_qkv=v_attn_b_qkv, attn_sinks=v_attn_sinks, attn_w_o=v_attn_w_o, hgrn_w_in=v_hgrn_w_in,
             hgrn_g_norm=v_hgrn_g_norm, hgrn_w_o=v_hgrn_w_o, hgrn_lower_bounds=v_hgrn_lower_bounds, mlp_w_up=v_mlp_w_up,
             mlp_w_down=v_mlp_w_down)
    me = 4 * lax.axis_index("x") + 2 * lax.axis_index("y") + lax.axis_index("c")

    gn = hgrn_g_norm.reshape(1, 128)
    gn_a = gn.astype(BF16)
    gn_b = (gn - gn_a.astype(F32)).astype(BF16)
    gn_c = (gn - gn_a.astype(F32) - gn_b.astype(F32)).astype(BF16)
    gn_rows = jnp.pad(jnp.concatenate([gn_a, gn_b, gn_c], axis=1), ((0, 15), (0, D_MODEL - 3 * 128)))
    first_rows = sum(rows for _, _, rows in GATHER_FIRST)
    full = {}
    got = _all_gather("gather_attn_weights", jnp.concatenate([_pack_pieces(w, GATHER_FIRST).astype(BF16), gn_rows], axis=0))
    _unpack_gathered(got[:, :first_rows], GATHER_FIRST, full)
    gn_terms = got[:, first_rows, :3 * 128].astype(F32).reshape(N_DEV, 3, 128)
    gn_full = ((gn_terms[:, 0] + gn_terms[:, 1]) + gn_terms[:, 2]).reshape(1, D_MODEL)

    x0 = x[0]
    tgt = loss_target[0]
    rot = _rotary_tables(positions)
    row = lambda a: a.reshape(1, -1)

    qkv, h0 = _norm_mm("qkv_proj", x0, row(mix_norm[0]), full["attn_w_qkv", None], attn_b_qkv)
    att, got = _attn_fwd(qkv, rot, attn_sinks, carry=(_Gather, _pack_pieces(w, GATHER_REST).astype(BF16)))
    _unpack_gathered(got, GATHER_REST, full)
    x1 = _mm_res("attn_out_proj", att, full["attn_w_o", None], x0)
    u0, h1 = _norm_mm("mlp0_up", x1, row(mlp_norm[0]), full["mlp_w_up", 0])
    x2 = _mlp_down("mlp0_down", u0, full["mlp_w_down", 0], x1)
    z, h2 = _norm_mm("hgrn_in_proj", x2, row(mix_norm[1]), full["hgrn_w_in", None])
    o_raw, states = _hgrn_fwd(z, hgrn_lower_bounds)
    x3, o2 = _hgrn_out("hgrn_out_proj", o_raw, z, gn_full, full["hgrn_w_o", None], x2)
    u1, h3 = _norm_mm("mlp1_up", x3, row(mlp_norm[1]), full["mlp_w_up", 1])
    x4 = _mlp_down("mlp1_down", u1, full["mlp_w_down", 1], x3)
    dx4, loss_part, g_final = _loss_head("loss_head", x4, tgt, row(final_norm))

    gw = {}
    du1, a1 = _mlp_bwd_act("mlp1_bwd_act", dx4, u1, full["mlp_w_down", 1])
    dx3, g_mlp1 = _mm_nt_rmsbwd("mlp1_bwd_in", du1, full["mlp_w_up", 1], x3, row(mlp_norm[1]), dx4)
    gw["mlp_w_down", 1] = _mm_tn("mlp1_dw_down", a1, dx4)
    gw["mlp_w_up", 1] = _mm_tn("mlp1_dw_up", h3, du1)

    do_raw, dg, g_gn = _hgrn_out_bwd("hgrn_out_bwd", dx3, o_raw, z, full["hgrn_w_o", None], gn_full)
    gw["hgrn_w_o", None] = _mm_tn("hgrn_dw_o", o2, dx3)
    dzq, dzf, dzi, g_lb, recv0 = _hgrn_bwd(z, hgrn_lower_bounds, states, do_raw,
                                           carry=(_Exchange, _pack_grads(gw, GRAD_GROUPS[0])))
    dz = jnp.concatenate([dzq, dzf, dzi, dg], axis=1)
    dx2, g_mix1 = _mm_nt_rmsbwd("hgrn_in_bwd", dz, full["hgrn_w_in", None], x2, row(mix_norm[1]), dx3)
    gw["hgrn_w_in", None] = _mm_tn("hgrn_dw_in", h2, dz)

    du0, a0 = _mlp_bwd_act("mlp0_bwd_act", dx2, u0, full["mlp_w_down", 0])
    dx1, g_mlp0 = _mm_nt_rmsbwd("mlp0_bwd_in", du0, full["mlp_w_up", 0], x1, row(mlp_norm[0]), dx2)
    gw["mlp_w_down", 0] = _mm_tn("mlp0_dw_down", a0, dx2)
    gw["mlp_w_up", 0] = _mm_tn("mlp0_dw_up", h1, du0)

    datt = _mm_nt("attn_out_bwd", dx1, full["attn_w_o", None], BF16)
    gw["attn_w_o", None] = _mm_tn("attn_dw_o", att, dx1)
    dqkv, g_sink, recv1 = _attn_bwd(qkv, rot, attn_sinks, datt, carry=(_Exchange, _pack_grads(gw, GRAD_GROUPS[1])))
    gw["attn_w_qkv", None] = _mm_tn("attn_dw_qkv", h0, dqkv)
    dx0, g_mix0, g_bqkv, recv2 = _mm_nt_rmsbwd(
        "qkv_bwd", dqkv, full["attn_w_qkv", None], x0, row(mix_norm[0]), dx1, with_colsum=True,
        carry=(_Exchange, _pack_grads(gw, GRAD_GROUPS[2])))

    big = [{}, {}, {}, {}]
    for i, (pieces, recv) in enumerate(zip(GRAD_GROUPS, (recv0, recv1, recv2))):
        res = _adamw_sum(f"adamw_group{i}", recv, *[_pack_pieces(d, pieces) for d in (w, m, v)])
        for out, p in zip(big, res):
            _unpack_pieces(p, pieces, w, out)

    zero_row = jnp.zeros((1, D_MODEL), F32)
    part = _pack_small(dict(
        mix_norm=jnp.concatenate([g_mix0, g_mix1], axis=0), mlp_norm=jnp.concatenate([g_mlp0, g_mlp1], axis=0),
        final_norm=g_final, attn_b_qkv=g_bqkv, attn_sinks=g_sink[:, :N_Q_HEADS],
        hgrn_lower_bounds=jnp.concatenate([zero_row, g_lb], axis=0)), g_gn)

    def spread(a):
        return lax.dynamic_update_slice(zero_row, a.reshape(1, 128), (0, me * 128))

    small_in = [_pack_small({n: d[n] for n in SMALL_NAMES if n != "hgrn_g_norm"}, spread(d["hgrn_g_norm"]))
                for d in (w, m, v)]
    small = [_unpack_small(p, me) for p in _small_sync(part, *small_in)]

    loss = lax.psum(loss_part[0, 0], ("x", "y", "c"))
    outs = [loss, dx0.reshape(x.shape)]
    for grp_big, grp_small in zip(big, small):
        for name in WEIGHT_NAMES:
            if name in SMALL_NAMES:
                val = grp_small[name]
            elif (name, None) in grp_big:
                val = grp_big[name, None]
            else:
                val = jnp.stack([grp_big[name, 0], grp_big[name, 1]], axis=0)
            outs.append(val.reshape(w[name].shape))
    return tuple(outs)
```

```python
import functools

import jax
import jax.numpy as jnp
from jax import lax
from jax.experimental import pallas as pl
from jax.experimental.pallas import tpu as pltpu

F32 = jnp.float32
BF16 = jnp.bfloat16

D_MODEL = 1024
HEAD_DIM = 64
N_Q_HEADS = 16
Q_DIM = 1024
KV_DIM = 256
QKV_DIM = 1536
ATT_BLOCK = 128
ROT_HALF = 8
ROPE_THETA = 500000.0
NEG_INF = -1e30
HGRN_HEADS = 8
HGRN_DK = 128
CHUNK = 64
D_FF = 4096
NORM_EPS = 1e-5
N_DEV = 8

ADAM_LR = 0.001
ADAM_B1 = 0.9
ADAM_B2 = 0.999
ADAM_EPS = 1e-08
ADAM_WD = 0.01
ADAM_STEP = 10

LANES = 128
VMEM_LIMIT = 56 * 1024 * 1024

GATHER_FIRST = (("attn_w_qkv", None, 192), ("attn_w_o", None, 128))
GATHER_REST = (("mlp_w_up", 0, 512), ("mlp_w_down", 0, 512), ("hgrn_w_in", None, 512), ("hgrn_w_o", None, 128),
               ("mlp_w_up", 1, 512), ("mlp_w_down", 1, 512))
GRAD_GROUPS = ((("mlp_w_down", 1, 512), ("mlp_w_up", 1, 512), ("hgrn_w_o", None, 128)),
               (("hgrn_w_in", None, 512), ("mlp_w_down", 0, 512), ("mlp_w_up", 0, 512), ("attn_w_o", None, 128)),
               (("attn_w_qkv", None, 192),))
COL_SHARDED = ("attn_w_qkv", "hgrn_w_in", "mlp_w_up")
SMALL_ROWS = 16


def _dot(a, b):
    return jnp.dot(a, b, preferred_element_type=F32)


def _dot_nt(a, b):
    return lax.dot_general(a, b, (((1,), (1,)), ((), ())), preferred_element_type=F32)


def _dot_tn(a, b):
    return lax.dot_general(a, b, (((0,), (0,)), ((), ())), preferred_element_type=F32)


def _params(**kw):
    return pltpu.CompilerParams(vmem_limit_bytes=VMEM_LIMIT, **kw)


def _full_spec(a):
    nd = a.ndim
    return pl.BlockSpec(a.shape, lambda *_: (0,) * nd)


def _row_call(name, body, n_rows, tm, row_ins, full_ins, row_outs, acc_outs=(), carry=(None, None)):
    steps = n_rows // tm
    in_specs = [pl.BlockSpec((tm, w), functools.partial(lambda i, cb: (i, cb), cb=cb)) for _, w, cb in row_ins]
    in_specs += [_full_spec(a) for a in full_ins]
    out_shape = [jax.ShapeDtypeStruct((n_rows, w), dt) for w, dt in row_outs]
    out_specs = [pl.BlockSpec((tm, w), lambda i: (i, 0)) for w, _ in row_outs]
    for shp, dt in acc_outs:
        out_shape.append(jax.ShapeDtypeStruct(shp, dt))
        out_specs.append(pl.BlockSpec(shp, functools.partial(lambda i, nd: (0,) * nd, nd=len(shp))))
    n_in, n_out = len(in_specs), len(out_specs)
    in_specs, out_specs, out_shape, scratch, extra = _carried_specs(*carry, in_specs, out_specs, out_shape, [])

    def wrapped(*refs):
        i = pl.program_id(0)
        own, finish = _carried(carry[0], refs, n_in, n_out, i == 0, i == steps - 1)
        body(*own)
        finish()

    return pl.pallas_call(
        wrapped, name=name, grid=(steps,), in_specs=in_specs, out_specs=out_specs, out_shape=out_shape,
        scratch_shapes=scratch, compiler_params=_params(dimension_semantics=("arbitrary",)),
    )(*[a for a, _, _ in row_ins], *full_ins, *extra)


def _rms(x, gain):
    r = lax.rsqrt(jnp.mean(x * x, axis=-1, keepdims=True) + NORM_EPS)
    xhat = x * r
    return xhat * gain, xhat, r


def _rms_bwd(dy, xhat, r, gain):
    dxhat = dy * gain
    dx = r * (dxhat - xhat * jnp.mean(dxhat * xhat, axis=-1, keepdims=True))
    return dx, dy * xhat


def _norm_mm(name, x, gain, w, bias=None, rot=None, tm=256):
    T = x.shape[0]
    tm = min(tm, T)
    n = w.shape[1]
    nc = 512
    assert n % nc == 0

    def body(*refs):
        x_ref, refs = refs[0], refs[1:]
        if rot is not None:
            t_ref, refs = refs[0], refs[1:]
        g_ref, w_ref, refs = refs[0], refs[1], refs[2:]
        if bias is not None:
            b_ref, refs = refs[0], refs[1:]
        y_ref, h_ref = refs
        h, _, _ = _rms(x_ref[...], g_ref[...])
        hb = h.astype(BF16)
        h_ref[...] = hb
        for c in range(n // nc):
            sl = slice(c * nc, (c + 1) * nc)
            y = _dot(hb, w_ref[:, sl])
            if bias is not None:
                y = y + b_ref[:, sl]
            if rot is None:
                y_ref[:, sl] = y
            else:
                n_rot = max(0, min(nc, Q_DIM + KV_DIM - c * nc)) // LANES
                pieces = _rot_fwd(y[:, :n_rot * LANES], t_ref[...]) if n_rot else []
                for j in range(nc // LANES):
                    col = slice(c * nc + j * LANES, c * nc + (j + 1) * LANES)
                    y_ref[:, col] = pieces[j] if j < n_rot else y[:, j * LANES:(j + 1) * LANES]

    rows = [(x, D_MODEL, 0)] + ([(rot, 3 * LANES, 0)] if rot is not None else [])
    full = [gain, w] + ([bias] if bias is not None else [])
    return _row_call(name, body, T, tm, rows, full, [(n, F32), (D_MODEL, BF16)])


def _mm_res(name, a, w, res, tm=512):
    T = a.shape[0]
    tm = min(tm, T)

    def body(a_ref, r_ref, w_ref, o_ref):
        o_ref[...] = r_ref[...] + _dot(a_ref[...], w_ref[...])

    return _row_call(name, body, T, tm, [(a, a.shape[1], 0), (res, D_MODEL, 0)], [w], [(D_MODEL, F32)])[0]


def _mlp_down(name, u, w, res, tm=256):
    T = u.shape[0]
    tm = min(tm, T)
    kc = 1024

    def body(u_ref, r_ref, w_ref, o_ref):
        acc = r_ref[...]
        for c in range(D_FF // kc):
            sl = slice(c * kc, (c + 1) * kc)
            a = jnp.maximum(u_ref[:, sl], 0.0)
            acc = acc + _dot((a * a).astype(BF16), w_ref[sl, :])
        o_ref[...] = acc

    return _row_call(name, body, T, tm, [(u, D_FF, 0), (res, D_MODEL, 0)], [w], [(D_MODEL, F32)])[0]


def _hgrn_out(name, o_raw, z, gn, w, res, tm=256):
    T = o_raw.shape[0]
    tm = min(tm, T)

    def body(o_ref, g_ref, r_ref, gn_ref, w_ref, x_ref, a_ref):
        y, _, _ = _rms(o_ref[...], gn_ref[...])
        g = g_ref[...]
        a = (y * (g * jax.nn.sigmoid(g))).astype(BF16)
        a_ref[...] = a
        x_ref[...] = r_ref[...] + _dot(a, w_ref[...])

    return _row_call(name, body, T, tm, [(o_raw, D_MODEL, 0), (z, D_MODEL, 3), (res, D_MODEL, 0)], [gn, w],
                     [(D_MODEL, F32), (D_MODEL, BF16)])


def _loss_head(name, x, target, gain, tm=512):
    T = x.shape[0]
    tm = min(tm, T)

    def body(x_ref, t_ref, g_ref, dx_ref, loss_ref, dg_ref):
        @pl.when(pl.program_id(0) == 0)
        def _():
            loss_ref[...] = jnp.zeros_like(loss_ref)
            dg_ref[...] = jnp.zeros_like(dg_ref)

        gain_v = g_ref[...]
        y, xhat, r = _rms(x_ref[...], gain_v)
        diff = y - t_ref[...]
        row = jnp.sum(diff * diff, axis=-1, keepdims=True) * (1.0 / D_MODEL)
        loss_ref[...] += jnp.broadcast_to(0.5 * jnp.sum(row, axis=0, keepdims=True), loss_ref.shape)
        dy = diff * (1.0 / D_MODEL)
        dx, dgr = _rms_bwd(dy, xhat, r, gain_v)
        dx_ref[...] = dx
        dg_ref[...] += jnp.sum(dgr, axis=0, keepdims=True)

    return _row_call(name, body, T, tm, [(x, D_MODEL, 0), (target, D_MODEL, 0)], [gain], [(D_MODEL, F32)],
                     [((1, LANES), F32), ((1, D_MODEL), F32)])


def _mm_nt_rmsbwd(name, dy, w, x, gain, dres, tm=256, with_colsum=False, carry=(None, None)):
    T = x.shape[0]
    tm = min(tm, T)
    n = dy.shape[1]

    def body(*refs):
        if with_colsum:
            dy_ref, x_ref, dr_ref, w_ref, g_ref, dx_ref, dg_ref, cs_ref = refs
        else:
            dy_ref, x_ref, dr_ref, w_ref, g_ref, dx_ref, dg_ref = refs

        @pl.when(pl.program_id(0) == 0)
        def _():
            dg_ref[...] = jnp.zeros_like(dg_ref)
            if with_colsum:
                cs_ref[...] = jnp.zeros_like(cs_ref)

        dyv = dy_ref[...]
        dh = _dot_nt(dyv.astype(BF16), w_ref[...])
        gain_v = g_ref[...]
        _, xhat, r = _rms(x_ref[...], gain_v)
        dx, dgr = _rms_bwd(dh, xhat, r, gain_v)
        dx_ref[...] = dr_ref[...] + dx
        dg_ref[...] += jnp.sum(dgr, axis=0, keepdims=True)
        if with_colsum:
            cs_ref[...] += jnp.sum(dyv.astype(F32), axis=0, keepdims=True)

    acc = [((1, D_MODEL), F32)] + ([((1, n), F32)] if with_colsum else [])
    return _row_call(name, body, T, tm, [(dy, n, 0), (x, D_MODEL, 0), (dres, D_MODEL, 0)], [w, gain],
                     [(D_MODEL, F32)], acc, carry=carry)


def _mm_nt(name, dy, w, out_dtype, tm=512):
    T = dy.shape[0]
    tm = min(tm, T)
    k = w.shape[0]

    def body(dy_ref, w_ref, o_ref):
        o_ref[...] = _dot_nt(dy_ref[...].astype(BF16), w_ref[...]).astype(out_dtype)

    return _row_call(name, body, T, tm, [(dy, dy.shape[1], 0)], [w], [(k, out_dtype)])[0]


def _mlp_bwd_act(name, dy, u, w_down, tm=256):
    T = u.shape[0]
    tm = min(tm, T)
    kc = 1024

    def body(dy_ref, u_ref, w_ref, du_ref, a_ref):
        dyb = dy_ref[...].astype(BF16)
        for c in range(D_FF // kc):
            sl = slice(c * kc, (c + 1) * kc)
            a = jnp.maximum(u_ref[:, sl], 0.0)
            da = _dot_nt(dyb, w_ref[sl, :])
            du_ref[:, sl] = (da * (2.0 * a)).astype(BF16)
            a_ref[:, sl] = (a * a).astype(BF16)

    return _row_call(name, body, T, tm, [(dy, D_MODEL, 0), (u, D_FF, 0)], [w_down], [(D_FF, BF16), (D_FF, BF16)])


def _hgrn_out_bwd(name, dx, o_raw, z, w, gn, tm=256):
    T = dx.shape[0]
    tm = min(tm, T)

    def body(dx_ref, o_ref, g_ref, w_ref, gn_ref, do_ref, dg_ref, dgn_ref):
        @pl.when(pl.program_id(0) == 0)
        def _():
            dgn_ref[...] = jnp.zeros_like(dgn_ref)

        da = _dot_nt(dx_ref[...].astype(BF16), w_ref[...])
        gn_v = gn_ref[...]
        y, xhat, r = _rms(o_ref[...], gn_v)
        g = g_ref[...]
        sg = jax.nn.sigmoid(g)
        dg_ref[...] = (da * y * (sg * (1.0 + g * (1.0 - sg)))).astype(BF16)
        dyn = da * (g * sg)
        do, dgr = _rms_bwd(dyn, xhat, r, gn_v)
        do_ref[...] = do
        dgn_ref[...] += jnp.sum(dgr, axis=0, keepdims=True)

    return _row_call(name, body, T, tm, [(dx, D_MODEL, 0), (o_raw, D_MODEL, 0), (z, D_MODEL, 3)], [w, gn],
                     [(D_MODEL, F32), (D_MODEL, BF16)], [((1, D_MODEL), F32)])


def _mm_tn(name, a, b, bm=1024, bn=512, tk=2048):
    T, M = a.shape
    N = b.shape[1]
    bm, bn, tk = min(bm, M), min(bn, N), min(tk, T)

    def body(a_ref, b_ref, o_ref):
        @pl.when(pl.program_id(2) == 0)
        def _():
            o_ref[...] = jnp.zeros_like(o_ref)

        o_ref[...] += _dot_tn(a_ref[...].astype(BF16), b_ref[...].astype(BF16))

    return pl.pallas_call(
        body, name=name, grid=(M // bm, N // bn, T // tk),
        in_specs=[pl.BlockSpec((tk, bm), lambda i, j, k: (k, i)), pl.BlockSpec((tk, bn), lambda i, j, k: (k, j))],
        out_specs=pl.BlockSpec((bm, bn), lambda i, j, k: (i, j)),
        out_shape=jax.ShapeDtypeStruct((M, N), F32),
        compiler_params=_params(dimension_semantics=("parallel", "parallel", "arbitrary")),
    )(a, b)


def _rot_fwd(x, tab):
    c, sa, sb = tab[:, :LANES], tab[:, LANES:2 * LANES], tab[:, 2 * LANES:]
    outs = []
    for j in range(x.shape[1] // LANES):
        xs = x[:, j * LANES:(j + 1) * LANES]
        outs.append(xs * c + pltpu.roll(xs, ROT_HALF, 1) * sa + pltpu.roll(xs, LANES - ROT_HALF, 1) * sb)
    return outs


def _rot_bwd(dys, tab):
    c, sa, sb = tab[:, :LANES], tab[:, LANES:2 * LANES], tab[:, 2 * LANES:]
    return [dy * c + pltpu.roll(dy * sa, LANES - ROT_HALF, 1) + pltpu.roll(dy * sb, ROT_HALF, 1) for dy in dys]


ATT_SCALE = HEAD_DIM ** -0.5
ATT_ROWS = 128


def _attn_masks(n):
    qi = lax.broadcasted_iota(jnp.int32, (ATT_BLOCK, 2 * ATT_BLOCK), 0)
    kj = lax.broadcasted_iota(jnp.int32, (ATT_BLOCK, 2 * ATT_BLOCK), 1)
    delta = qi + ATT_BLOCK - kj
    first_key = jnp.where(n > 0, 0, ATT_BLOCK)
    valid = (delta >= 0) & (delta < ATT_BLOCK) & (kj >= first_key)
    lane = lax.broadcasted_iota(jnp.int32, (1, LANES), 1)
    return valid, lane < HEAD_DIM


def _attn_probs(qm, k_use, valid, sink):
    s = jnp.where(valid, _dot_nt(qm, k_use), NEG_INF)
    m = jnp.maximum(jnp.max(s, axis=-1, keepdims=True), sink)
    e = jnp.exp(s - m)
    es = jnp.exp(sink - m)
    inv = 1.0 / (jnp.sum(e, axis=-1, keepdims=True) + es)
    return e * inv, es * inv


def _attn_specs(nb, tables):
    prev = lambda n: jnp.maximum(jnp.minimum(n, nb - 1) - 1, 0)
    cur = lambda n: jnp.minimum(n, nb - 1)
    specs = [
        pl.BlockSpec((ATT_BLOCK, Q_DIM), lambda n: (cur(n), 0)),
        pl.BlockSpec((ATT_BLOCK, KV_DIM), lambda n: (prev(n), 4)),
        pl.BlockSpec((ATT_BLOCK, KV_DIM), lambda n: (cur(n), 4)),
        pl.BlockSpec((ATT_BLOCK, KV_DIM), lambda n: (prev(n), 5)),
        pl.BlockSpec((ATT_BLOCK, KV_DIM), lambda n: (cur(n), 5)),
    ]
    if tables:
        specs += [pl.BlockSpec((ATT_BLOCK, 3 * LANES), lambda n: (prev(n), 0)),
                  pl.BlockSpec((ATT_BLOCK, 3 * LANES), lambda n: (cur(n), 0))]
    return specs + [pl.BlockSpec(memory_space=pltpu.SMEM)]


def _kv_band(kp_ref, kc_ref, vp_ref, vc_ref):
    ks, vs = [], []
    for j in range(KV_DIM // LANES):
        sl = slice(j * LANES, (j + 1) * LANES)
        kb = jnp.concatenate([kp_ref[:, sl], kc_ref[:, sl]], axis=0)
        vb = jnp.concatenate([vp_ref[:, sl], vc_ref[:, sl]], axis=0)
        ks.append((kb.astype(BF16), pltpu.roll(kb, HEAD_DIM, 1).astype(BF16)))
        vs.append((vb.astype(BF16), pltpu.roll(vb, HEAD_DIM, 1).astype(BF16)))
    return ks, vs


def _attn_fwd(qkv, sinks, carry=(None, None)):
    T = qkv.shape[0]
    nb = T // ATT_BLOCK

    def body(*refs):
        n = pl.program_id(0)
        own, finish = _carried(carry[0], refs, 6, 1, n == 0, n == nb - 1)
        q_ref, kp_ref, kc_ref, vp_ref, vc_ref, sink_ref, o_ref = own
        valid, low = _attn_masks(n)
        ks, vs = _kv_band(kp_ref, kc_ref, vp_ref, vc_ref)
        for p in range(Q_DIM // LANES):
            kpair, khalf = p // 4, (p // 2) % 2
            q_pair = q_ref[:, p * LANES:(p + 1) * LANES] * ATT_SCALE
            for r0 in range(0, ATT_BLOCK, ATT_ROWS):
                rows = slice(r0, r0 + ATT_ROWS)
                outs = []
                for hf in range(2):
                    qm = jnp.where(low if hf == 0 else ~low, q_pair[rows], 0.0).astype(BF16)
                    sw = 0 if khalf == hf else 1
                    pr, _ = _attn_probs(qm, ks[kpair][sw], valid[rows], sink_ref[0, 2 * p + hf])
                    outs.append(_dot(pr.astype(BF16), vs[kpair][sw]))
                o_ref[rows, p * LANES:(p + 1) * LANES] = jnp.where(low, outs[0], outs[1]).astype(BF16)
        finish()

    in_specs, out_specs, out_shape, scratch, extra = _carried_specs(
        *carry, _attn_specs(nb, False), [pl.BlockSpec((ATT_BLOCK, Q_DIM), lambda n: (n, 0))],
        [jax.ShapeDtypeStruct((T, Q_DIM), BF16)], [])
    return pl.pallas_call(
        body, name="attn_fwd", grid=(nb,), in_specs=in_specs, out_specs=out_specs, out_shape=out_shape,
        scratch_shapes=scratch, compiler_params=_params(dimension_semantics=("arbitrary",)),
    )(qkv, qkv, qkv, qkv, qkv, sinks, *extra)


def _attn_bwd(qkv, rot, sinks, dout, carry=(None, None)):
    T = qkv.shape[0]
    nb = T // ATT_BLOCK
    npair = KV_DIM // LANES

    def body(*refs):
        n = pl.program_id(0)
        own, finish = _carried(carry[0], refs, 9, 2, n == 0, n == nb)
        (q_ref, kp_ref, kc_ref, vp_ref, vc_ref, tp_ref, tc_ref, sink_ref, do_ref, dqkv_ref, dsink_ref,
         dq_c, dk_c, dv_c) = own

        @pl.when(n == 0)
        def _():
            dq_c[...] = jnp.zeros_like(dq_c)
            dk_c[...] = jnp.zeros_like(dk_c)
            dv_c[...] = jnp.zeros_like(dv_c)
            dsink_ref[...] = jnp.zeros_like(dsink_ref)

        def flush(dk_prev, dv_prev, tab_ref):
            dqkv_ref[:, :Q_DIM] = dq_c[...]
            dk = _rot_bwd([dk_c[:, j * LANES:(j + 1) * LANES] + dk_prev[j] for j in range(npair)], tab_ref[...])
            for j in range(npair):
                dqkv_ref[:, Q_DIM + j * LANES:Q_DIM + (j + 1) * LANES] = dk[j]
                dqkv_ref[:, Q_DIM + KV_DIM + j * LANES:Q_DIM + KV_DIM + (j + 1) * LANES] = (
                    dv_c[:, j * LANES:(j + 1) * LANES] + dv_prev[j])

        @pl.when(n < nb)
        def _():
            valid, low = _attn_masks(n)
            lane = lax.broadcasted_iota(jnp.int32, (1, LANES), 1)
            ks, vs = _kv_band(kp_ref, kc_ref, vp_ref, vc_ref)
            dk_acc = [[jnp.zeros((2 * ATT_BLOCK, LANES), F32) for _ in range(2)] for _ in range(npair)]
            dv_acc = [[jnp.zeros((2 * ATT_BLOCK, LANES), F32) for _ in range(2)] for _ in range(npair)]
            dsink = jnp.zeros((1, LANES), F32)
            dqs = []
            for p in range(Q_DIM // LANES):
                kpair, khalf = p // 4, (p // 2) % 2
                q_pair = q_ref[:, p * LANES:(p + 1) * LANES] * ATT_SCALE
                do_pair = do_ref[:, p * LANES:(p + 1) * LANES]
                dq_rows = []
                for r0 in range(0, ATT_BLOCK, ATT_ROWS):
                    rows = slice(r0, r0 + ATT_ROWS)
                    dq_h = []
                    for hf in range(2):
                        sel = low if hf == 0 else ~low
                        qm = jnp.where(sel, q_pair[rows], 0.0).astype(BF16)
                        dom = jnp.where(sel, do_pair[rows], 0.0).astype(BF16)
                        sw = 0 if khalf == hf else 1
                        k_use, v_use = ks[kpair][sw], vs[kpair][sw]
                        pr, ps = _attn_probs(qm, k_use, valid[rows], sink_ref[0, 2 * p + hf])
                        dp = _dot_nt(dom, v_use)
                        dd = jnp.sum(pr * dp, axis=-1, keepdims=True)
                        ds = (pr * (dp - dd)).astype(BF16)
                        dq_h.append(_dot(ds, k_use))
                        dk_acc[kpair][sw] = dk_acc[kpair][sw] + _dot_tn(ds, qm)
                        dv_acc[kpair][sw] = dv_acc[kpair][sw] + _dot_tn(pr.astype(BF16), dom)
                        dsink = dsink + jnp.where(lane == 2 * p + hf, -jnp.sum(ps * dd, axis=0, keepdims=True), 0.0)
                    dq_rows.append(jnp.where(low, dq_h[0], dq_h[1]) * ATT_SCALE)
                dqs.append(jnp.concatenate(dq_rows, axis=0))
            dk_acc = [a[0] + pltpu.roll(a[1], HEAD_DIM, 1) for a in dk_acc]
            dv_acc = [a[0] + pltpu.roll(a[1], HEAD_DIM, 1) for a in dv_acc]
            flush([a[:ATT_BLOCK] for a in dk_acc], [a[:ATT_BLOCK] for a in dv_acc], tp_ref)
            dq = _rot_bwd(dqs, tc_ref[...])
            for p in range(Q_DIM // LANES):
                dq_c[:, p * LANES:(p + 1) * LANES] = dq[p]
            for j in range(npair):
                dk_c[:, j * LANES:(j + 1) * LANES] = dk_acc[j][ATT_BLOCK:]
                dv_c[:, j * LANES:(j + 1) * LANES] = dv_acc[j][ATT_BLOCK:]
            dsink_ref[...] += dsink

        @pl.when(n == nb)
        def _():
            zero = [jnp.zeros((ATT_BLOCK, LANES), F32) for _ in range(npair)]
            flush(zero, zero, tc_ref)

        finish()

    do_spec = pl.BlockSpec((ATT_BLOCK, Q_DIM), lambda n: (jnp.minimum(n, nb - 1), 0))
    in_specs, out_specs, out_shape, scratch, extra = _carried_specs(
        *carry, _attn_specs(nb, True) + [do_spec],
        [pl.BlockSpec((ATT_BLOCK, QKV_DIM), lambda n: (jnp.maximum(n - 1, 0), 0)),
         pl.BlockSpec((1, LANES), lambda n: (0, 0))],
        [jax.ShapeDtypeStruct((T, QKV_DIM), F32), jax.ShapeDtypeStruct((1, LANES), F32)],
        [pltpu.VMEM((ATT_BLOCK, Q_DIM), F32), pltpu.VMEM((ATT_BLOCK, KV_DIM), F32),
         pltpu.VMEM((ATT_BLOCK, KV_DIM), F32)])
    return pl.pallas_call(
        body, name="attn_bwd", grid=(nb + 1,), in_specs=in_specs, out_specs=out_specs, out_shape=out_shape,
        scratch_shapes=scratch, compiler_params=_params(dimension_semantics=("arbitrary",)),
    )(qkv, qkv, qkv, qkv, qkv, rot, rot, sinks, dout, *extra)


LEVELS = (32, 16, 8)
DIAG = 8
UNROLL = 2


def _lower_bound(lb_ref):
    l0, l1 = lb_ref[0:1, :], lb_ref[1:2, :]
    mx = jnp.maximum(l0, l1)
    e0, e1 = jnp.exp(l0 - mx), jnp.exp(l1 - mx)
    return e1 / (e0 + e1)


def _cumsum_rows(x, row):
    for sh in (1, 2, 4, 8, 16, 32):
        x = x + jnp.where(row >= sh, pltpu.roll(x, sh, 0), 0.0)
    return x


def _rev_cumsum_rows(x, row):
    for sh in (1, 2, 4, 8, 16, 32):
        x = x + jnp.where(row < CHUNK - sh, pltpu.roll(x, CHUNK - sh, 0), 0.0)
    return x


def _level_masks():
    t = lax.broadcasted_iota(jnp.int32, (CHUNK, CHUNK), 0)
    s = lax.broadcasted_iota(jnp.int32, (CHUNK, CHUNK), 1)
    return [((t & h) != 0) & ((s & h) == 0) & ((t ^ s) < 2 * h) for h in LEVELS]


def _level_scale(b, h):
    parts = [jnp.broadcast_to(b[j * 2 * h + h - 1:j * 2 * h + h, :], (2 * h, HGRN_DK)) for j in range(CHUNK // (2 * h))]
    mid = parts[0] if len(parts) == 1 else jnp.concatenate(parts, axis=0)
    return jnp.exp(-jnp.abs(b - mid))


def _hgrn_gates(zq, zf, lb):
    sq = jax.nn.sigmoid(zq)
    q = zq * sq
    sg = jax.nn.sigmoid(zf)
    forget = lb + (1.0 - lb) * sg
    return q, sq, sg, forget, 1.0 - forget, jnp.log(forget)


def _hgrn_specs(T, rb, rev):
    nr = T // rb
    ri = (lambda r: nr - 1 - r) if rev else (lambda r: r)
    return nr, ri, [
        pl.BlockSpec((rb, HGRN_DK), lambda h, r: (ri(r), h)),
        pl.BlockSpec((rb, HGRN_DK), lambda h, r: (ri(r), HGRN_HEADS + h)),
        pl.BlockSpec((rb, HGRN_DK), lambda h, r: (ri(r), 2 * HGRN_HEADS + h)),
        pl.BlockSpec((2, HGRN_DK), lambda h, r: (0, h)),
    ]


def _hgrn_fwd(z, lb_raw, rb=512):
    T = z.shape[0]
    rb = min(rb, T)
    ncb = rb // CHUNK
    nr, ri, in_specs = _hgrn_specs(T, rb, False)

    def body(zq_ref, zf_ref, zi_ref, lb_ref, o_ref, st_ref, state):
        @pl.when(pl.program_id(1) == 0)
        def _():
            state[...] = jnp.zeros_like(state)

        lb = _lower_bound(lb_ref)
        row = lax.broadcasted_iota(jnp.int32, (CHUNK, HGRN_DK), 0)
        masks = _level_masks()

        def chunk(c, st):
            rows = pl.ds(pl.multiple_of(c * CHUNK, CHUNK), CHUNK)
            q, _, _, _, k, lf = _hgrn_gates(zq_ref[rows, :], zf_ref[rows, :], lb)
            v = zi_ref[rows, :]
            vb = v.astype(BF16)
            b = _cumsum_rows(lf, row)
            sc = jnp.zeros((CHUNK, CHUNK), F32)
            for h, mask in zip(LEVELS, masks):
                e = _level_scale(b, h)
                sc = sc + jnp.where(mask, _dot_nt((q * e).astype(BF16), (k * e).astype(BF16)), 0.0)
            o = _dot(sc.astype(BF16), vb)
            for d in range(DIAG):
                if d == 0:
                    w = q * k
                    vr = v
                else:
                    w = jnp.where((row & (DIAG - 1)) >= d,
                                  q * pltpu.roll(k, d, 0) * jnp.exp(b - pltpu.roll(b, d, 0)), 0.0)
                    vr = pltpu.roll(v, d, 0)
                o = o + jnp.sum(w, axis=-1, keepdims=True) * vr
            b_last = b[CHUNK - 1:CHUNK, :]
            kd = (k * jnp.exp(b_last - b)).astype(BF16)
            qd = (q * jnp.exp(b)).astype(BF16)
            st_ref[c, 0] = st
            o_ref[rows, :] = o + _dot_nt(qd, st.astype(BF16))
            return st * jnp.exp(b_last) + _dot_tn(vb, kd)

        def group(i, st):
            for j in range(UNROLL):
                st = chunk(i * UNROLL + j, st)
            return st

        state[...] = lax.fori_loop(0, ncb // UNROLL, group, state[...])

    return pl.pallas_call(
        body, name="hgrn_fwd", grid=(HGRN_HEADS, nr), in_specs=in_specs,
        out_specs=[pl.BlockSpec((rb, HGRN_DK), lambda h, r: (r, h)),
                   pl.BlockSpec((ncb, 1, HGRN_DK, HGRN_DK), lambda h, r: (r, h, 0, 0))],
        out_shape=[jax.ShapeDtypeStruct((T, D_MODEL), F32),
                   jax.ShapeDtypeStruct((T // CHUNK, HGRN_HEADS, HGRN_DK, HGRN_DK), F32)],
        scratch_shapes=[pltpu.VMEM((HGRN_DK, HGRN_DK), F32)],
        compiler_params=_params(dimension_semantics=("arbitrary", "arbitrary")),
    )(z, z, z, lb_raw)


def _hgrn_bwd(z, lb_raw, states, do, rb=512, carry=(None, None)):
    T = z.shape[0]
    rb = min(rb, T)
    ncb = rb // CHUNK
    nr, ri, in_specs = _hgrn_specs(T, rb, True)
    in_specs += [pl.BlockSpec((ncb, 1, HGRN_DK, HGRN_DK), lambda h, r: (ri(r), h, 0, 0)),
                 pl.BlockSpec((rb, HGRN_DK), lambda h, r: (ri(r), h))]

    def body(*refs):
        hh, rr = pl.program_id(0), pl.program_id(1)
        own, finish = _carried(carry[0], refs, 6, 4, (hh == 0) & (rr == 0), (hh == HGRN_HEADS - 1) & (rr == nr - 1))
        zq_ref, zf_ref, zi_ref, lb_ref, st_ref, do_ref, dq_ref, df_ref, di_ref, dlb_ref, dstate = own

        @pl.when(rr == 0)
        def _():
            dstate[...] = jnp.zeros_like(dstate)
            dlb_ref[...] = jnp.zeros_like(dlb_ref)

        lb = _lower_bound(lb_ref)
        row = lax.broadcasted_iota(jnp.int32, (CHUNK, HGRN_DK), 0)
        masks = _level_masks()

        def chunk(ci, dlb):
            c = ncb - 1 - ci
            rows = pl.ds(pl.multiple_of(c * CHUNK, CHUNK), CHUNK)
            zq = zq_ref[rows, :]
            q, sq, sg, forget, k, lf = _hgrn_gates(zq, zf_ref[rows, :], lb)
            v = zi_ref[rows, :]
            dov = do_ref[rows, :]
            b = _cumsum_rows(lf, row)
            st = st_ref[c, 0]
            dst = dstate[...]
            b_last = b[CHUNK - 1:CHUNK, :]
            eb = jnp.exp(b)
            ebb = jnp.exp(b_last - b)
            e_last = jnp.exp(b_last)
            dob, vb, stb, dstb = dov.astype(BF16), v.astype(BF16), st.astype(BF16), dst.astype(BF16)
            dq = eb * _dot(dob, stb)
            dv = _dot_nt((k * ebb).astype(BF16), dstb)
            dk = ebb * _dot(vb, dstb)
            extra = e_last * jnp.sum(dst * st, axis=0, keepdims=True) + jnp.sum(k * dk, axis=0, keepdims=True)
            da = _dot_nt(dob, vb)
            sc = jnp.zeros((CHUNK, CHUNK), F32)
            for h, mask in zip(LEVELS, masks):
                e = _level_scale(b, h)
                qs, ks = (q * e).astype(BF16), (k * e).astype(BF16)
                dam = jnp.where(mask, da, 0.0).astype(BF16)
                dq = dq + e * _dot(dam, ks)
                dk = dk + e * _dot_tn(dam, qs)
                sc = sc + jnp.where(mask, _dot_nt(qs, ks), 0.0)
            dv = dv + _dot_tn(sc.astype(BF16), dob)
            for d in range(DIAG):
                if d == 0:
                    dad = jnp.sum(dov * v, axis=-1, keepdims=True)
                    dq = dq + dad * k
                    dk = dk + dad * q
                    dv = dv + jnp.sum(q * k, axis=-1, keepdims=True) * dov
                else:
                    w = jnp.where((row & (DIAG - 1)) >= d, jnp.exp(b - pltpu.roll(b, d, 0)), 0.0)
                    kr = pltpu.roll(k, d, 0)
                    dad = jnp.sum(dov * pltpu.roll(v, d, 0), axis=-1, keepdims=True)
                    ad = jnp.sum(q * kr * w, axis=-1, keepdims=True)
                    dq = dq + dad * kr * w
                    dk = dk + pltpu.roll(dad * q * w, CHUNK - d, 0)
                    dv = dv + pltpu.roll(ad * dov, CHUNK - d, 0)
            dlf = _rev_cumsum_rows(q * dq - k * dk, row) + extra
            dstate[...] = dst * e_last + _dot_tn(dob, (q * eb).astype(BF16))
            dforget = dlf / forget - dk
            dq_ref[rows, :] = (dq * (sq * (1.0 + zq * (1.0 - sq)))).astype(BF16)
            df_ref[rows, :] = (dforget * (1.0 - lb) * sg * (1.0 - sg)).astype(BF16)
            di_ref[rows, :] = dv.astype(BF16)
            return dlb + jnp.sum(dforget * (1.0 - sg), axis=0, keepdims=True)

        def group(i, dlb):
            for j in range(UNROLL):
                dlb = chunk(i * UNROLL + j, dlb)
            return dlb

        dlb_ref[...] += lax.fori_loop(0, ncb // UNROLL, group, jnp.zeros((1, HGRN_DK), F32))
        finish()

    blk = pl.BlockSpec((rb, HGRN_DK), lambda h, r: (ri(r), h))
    in_specs, out_specs, out_shape, scratch, extra = _carried_specs(
        *carry, in_specs, [blk, blk, blk, pl.BlockSpec((1, HGRN_DK), lambda h, r: (0, h))],
        [jax.ShapeDtypeStruct((T, D_MODEL), BF16)] * 3 + [jax.ShapeDtypeStruct((1, D_MODEL), F32)],
        [pltpu.VMEM((HGRN_DK, HGRN_DK), F32)])
    return pl.pallas_call(
        body, name="hgrn_bwd", grid=(HGRN_HEADS, nr), in_specs=in_specs, out_specs=out_specs, out_shape=out_shape,
        scratch_shapes=scratch, compiler_params=_params(dimension_semantics=("arbitrary", "arbitrary")),
    )(z, z, z, lb_raw, states, do, *extra)


MESH = pl.DeviceIdType.MESH
ANY = pl.BlockSpec(memory_space=pl.ANY)


def _place():
    return lax.axis_index("x"), lax.axis_index("y"), lax.axis_index("c")


def _sems():
    return [pltpu.SemaphoreType.DMA((7,)), pltpu.SemaphoreType.DMA((7,)), pltpu.SemaphoreType.DMA]


class _Gather:
    def __init__(self, x_ref, out_ref, send_sems, recv_sems, local_sem):
        self.x_ref, self.out_ref, self.send_sems, self.recv_sems, self.local_sem = (
            x_ref, out_ref, send_sems, recv_sems, local_sem)
        x, y, c = _place()
        self.c = c
        self.me, self.sibling = (x, y, c), (x, y, 1 - c)
        self.chips = [(1 - x, y), (x, 1 - y), (1 - x, 1 - y)]

    def rows(self, px, py, pc):
        return self.out_ref.at[4 * px + 2 * py + pc]

    def copy(self, k, block, to, from_input=False):
        return pltpu.make_async_remote_copy(
            src_ref=self.x_ref if from_input else self.rows(*block), dst_ref=self.rows(*block),
            send_sem=self.send_sems.at[k], recv_sem=self.recv_sems.at[k], device_id=to, device_id_type=MESH)

    def first(self):
        out = [self.copy(0, self.me, self.sibling, from_input=True)]
        return out + [self.copy(1 + j, self.me, (*chip, self.c), from_input=True) for j, chip in enumerate(self.chips)]

    def start(self):
        pltpu.make_async_copy(self.x_ref, self.rows(*self.me), self.local_sem).start()
        for cp in self.first():
            cp.start()

    def finish(self):
        passed = [self.copy(4 + j, (*chip, self.c), self.sibling) for j, chip in enumerate(self.chips)]
        for j, chip in enumerate(self.chips):
            self.copy(1 + j, (*chip, self.c), self.me).wait_recv()
            passed[j].start()
        self.copy(0, self.sibling, self.me).wait_recv()
        for j, chip in enumerate(self.chips):
            self.copy(4 + j, (*chip, 1 - self.c), self.me).wait_recv()
        for cp in self.first() + passed:
            cp.wait_send()
        pltpu.make_async_copy(self.x_ref, self.rows(*self.me), self.local_sem).wait()


def _all_gather(name, x_shard):
    def body(*refs):
        g = _Gather(*refs)
        g.start()
        g.finish()

    return pl.pallas_call(
        body, name=name, out_shape=jax.ShapeDtypeStruct((N_DEV,) + x_shard.shape, x_shard.dtype),
        in_specs=[ANY], out_specs=ANY, scratch_shapes=_sems(),
    )(x_shard)


def _peers(x, y, c):
    out = []
    for k in range(1, N_DEV):
        px = 1 - x if k & 4 else x
        py = 1 - y if k & 2 else y
        pc = 1 - c if k & 1 else c
        out.append((k, (px, py, pc), 4 * px + 2 * py + pc))
    return out


class _Exchange:
    def __init__(self, g_ref, recv_ref, send_sems, recv_sems, local_sem):
        x, y, c = _place()
        me = 4 * x + 2 * y + c
        self.local = pltpu.make_async_copy(g_ref.at[me], recv_ref.at[me], local_sem)
        self.copies = [
            pltpu.make_async_remote_copy(
                src_ref=g_ref.at[pidx], dst_ref=recv_ref.at[me], send_sem=send_sems.at[k - 1],
                recv_sem=recv_sems.at[k - 1], device_id=peer, device_id_type=MESH)
            for k, peer, pidx in _peers(x, y, c)]

    def start(self):
        self.local.start()
        for cp in self.copies:
            cp.start()

    def finish(self):
        for cp in self.copies:
            cp.wait()
        self.local.wait()


def _carried(kind, refs, n_in, n_out, first, last):
    if kind is None:
        return refs, lambda: None
    ins, rest = refs[:n_in], refs[n_in + 1:]
    outs, scratch = rest[:n_out], rest[n_out + 1:]
    sems = scratch[len(scratch) - 3:]
    op = kind(refs[n_in], rest[n_out], *sems)

    @pl.when(first)
    def _():
        op.start()

    def finish():
        @pl.when(last)
        def _():
            op.finish()

    return tuple(ins) + tuple(outs) + tuple(scratch[:len(scratch) - 3]), finish


def _carried_specs(kind, arr, in_specs, out_specs, out_shape, scratch):
    if kind is None:
        return in_specs, out_specs, out_shape, scratch, []
    shape = arr.shape if kind is _Exchange else (N_DEV,) + arr.shape
    return (list(in_specs) + [ANY], list(out_specs) + [ANY],
            list(out_shape) + [jax.ShapeDtypeStruct(shape, arr.dtype)], list(scratch) + _sems(), [arr])


def _adamw(w, g, m, v):
    m = ADAM_B1 * m + (1.0 - ADAM_B1) * g
    v = ADAM_B2 * v + (1.0 - ADAM_B2) * (g * g)
    m_hat = m / (1.0 - ADAM_B1 ** ADAM_STEP)
    v_hat = v / (1.0 - ADAM_B2 ** ADAM_STEP)
    delta = -ADAM_LR * (m_hat / (jnp.sqrt(v_hat) + ADAM_EPS) + ADAM_WD * w)
    return delta, m, v


def _adamw_sum(name, recv, w, m, v):
    R, C = w.shape
    tm = 128 if R % 128 == 0 else 64
    assert R % tm == 0

    def body(r_ref, w_ref, m_ref, v_ref, g_ref, d_ref, nm_ref, nv_ref):
        g = r_ref[0].astype(F32)
        for s in range(1, N_DEV):
            g = g + r_ref[s].astype(F32)
        g_ref[...] = g
        d_ref[...], nm_ref[...], nv_ref[...] = _adamw(w_ref[...], g, m_ref[...], v_ref[...])

    blk = pl.BlockSpec((tm, C), lambda i: (i, 0))
    return pl.pallas_call(
        body, name=name, grid=(R // tm,),
        in_specs=[pl.BlockSpec((N_DEV, tm, C), lambda i: (0, i, 0)), blk, blk, blk],
        out_specs=[blk] * 4, out_shape=[jax.ShapeDtypeStruct((R, C), F32)] * 4,
        compiler_params=_params(dimension_semantics=("arbitrary",)),
    )(recv, w, m, v)


def _small_sync(part, w, m, v):
    def body(p_ref, w_ref, m_ref, v_ref, g_ref, d_ref, nm_ref, nv_ref, gath, send_sems, recv_sems):
        x, y, c = _place()
        me = 4 * x + 2 * y + c
        gath[me] = p_ref[...]
        copies = []
        for k, peer, _ in _peers(x, y, c):
            cp = pltpu.make_async_remote_copy(
                src_ref=p_ref, dst_ref=gath.at[me], send_sem=send_sems.at[k - 1], recv_sem=recv_sems.at[k - 1],
                device_id=peer, device_id_type=MESH)
            cp.start()
            copies.append(cp)
        for cp in copies:
            cp.wait()
        g = gath[0]
        for s in range(1, N_DEV):
            g = g + gath[s]
        wv = w_ref[...]
        l0, l1 = w_ref[8:9, :], w_ref[9:10, :]
        mx = jnp.maximum(l0, l1)
        e0, e1 = jnp.exp(l0 - mx), jnp.exp(l1 - mx)
        g9 = g[9:10, :] * (e0 / (e0 + e1)) * (e1 / (e0 + e1))
        row = lax.broadcasted_iota(jnp.int32, g.shape, 0)
        g = jnp.where(row == 9, g9, jnp.where(row == 8, -g9, g))
        g_ref[...] = g
        d_ref[...], nm_ref[...], nv_ref[...] = _adamw(wv, g, m_ref[...], v_ref[...])

    vm = pl.BlockSpec(memory_space=pltpu.VMEM)
    return pl.pallas_call(
        body, name="small_params_sync", in_specs=[vm] * 4, out_specs=[vm] * 4,
        out_shape=[jax.ShapeDtypeStruct(part.shape, F32)] * 4,
        scratch_shapes=[pltpu.VMEM((N_DEV,) + part.shape, F32), pltpu.SemaphoreType.DMA((7,)),
                        pltpu.SemaphoreType.DMA((7,))],
    )(part, w, m, v)


def _piece(d, name, layer):
    return d[name][0 if layer is None else layer]


def _pack_pieces(d, pieces):
    return jnp.concatenate([_piece(d, name, layer).reshape(rows, D_MODEL) for name, layer, rows in pieces], axis=0)


def _unpack_pieces(p, pieces, like, out):
    r0 = 0
    for name, layer, rows in pieces:
        out[name, layer] = p[r0:r0 + rows].reshape(_piece(like, name, layer).shape)
        r0 += rows


def _unpack_gathered(wg, pieces, out):
    r0 = 0
    for name, layer, rows in pieces:
        a = wg[:, r0:r0 + rows]
        r0 += rows
        if name in COL_SHARDED:
            a = a.reshape(N_DEV, D_MODEL, rows).transpose(1, 0, 2).reshape(D_MODEL, N_DEV * rows)
        else:
            a = a.reshape(N_DEV * rows, D_MODEL)
        out[name, layer] = a


def _pack_grads(gw, pieces):
    parts = []
    for name, layer, rows in pieces:
        a = gw[name, layer]
        if name in COL_SHARDED:
            a = a.reshape(D_MODEL, N_DEV, rows).transpose(1, 0, 2)
        parts.append(a.reshape(N_DEV, rows, D_MODEL))
    return jnp.concatenate(parts, axis=1).astype(BF16)


def _pad_row(a, width=D_MODEL):
    a = a.reshape(1, -1)
    return jnp.pad(a, ((0, 0), (0, width - a.shape[1])))


def _pack_small(d, gn_full):
    rows = [d["mix_norm"], d["mlp_norm"], d["final_norm"].reshape(1, D_MODEL),
            _pad_row(d["attn_b_qkv"], 2 * D_MODEL).reshape(2, D_MODEL), _pad_row(d["attn_sinks"]),
            d["hgrn_lower_bounds"], gn_full.reshape(1, D_MODEL)]
    p = jnp.concatenate(rows, axis=0)
    return jnp.pad(p, ((0, SMALL_ROWS - p.shape[0]), (0, 0)))


def _unpack_small(p, me):
    return dict(
        mix_norm=p[0:2], mlp_norm=p[2:4], final_norm=p[4],
        attn_b_qkv=p[5:7].reshape(1, 2 * D_MODEL)[:, :QKV_DIM], attn_sinks=p[7:8, :N_Q_HEADS],
        hgrn_lower_bounds=p[8:10], hgrn_g_norm=lax.dynamic_slice(p[10:11], (0, me * 128), (1, 128)))


WEIGHT_NAMES = ['mix_norm', 'mlp_norm', 'final_norm', 'attn_w_qkv', 'attn_b_qkv', 'attn_sinks', 'attn_w_o', 'hgrn_w_in',
                'hgrn_g_norm', 'hgrn_w_o', 'hgrn_lower_bounds', 'mlp_w_up', 'mlp_w_down']
SMALL_NAMES = ('mix_norm', 'mlp_norm', 'final_norm', 'attn_b_qkv', 'attn_sinks', 'hgrn_lower_bounds', 'hgrn_g_norm')


def _rotary_tables(positions):
    inv_freq = ROPE_THETA ** (-jnp.arange(0, 2 * ROT_HALF, 2, dtype=F32) / (2 * ROT_HALF))
    ang = positions.astype(F32).reshape(-1, 1) * inv_freq
    cos, sin = jnp.cos(ang), jnp.sin(ang)
    r = jnp.arange(LANES) % HEAD_DIM
    idx = r % ROT_HALF
    c = jnp.where(r < 2 * ROT_HALF, cos[:, idx], 1.0)
    sa = jnp.where((r >= ROT_HALF) & (r < 2 * ROT_HALF), sin[:, idx], 0.0)
    sb = jnp.where(r < ROT_HALF, -sin[:, idx], 0.0)
    return jnp.concatenate([c, sa, sb], axis=1)


def kernel(x, positions, mix_norm, mlp_norm, final_norm, attn_w_qkv, attn_b_qkv, attn_sinks, attn_w_o, hgrn_w_in, hgrn_g_norm, hgrn_w_o, hgrn_lower_bounds, mlp_w_up, mlp_w_down, loss_target, m_mix_norm, m_mlp_norm, m_final_norm, m_attn_w_qkv, m_attn_b_qkv, m_attn_sinks, m_attn_w_o, m_hgrn_w_in, m_hgrn_g_norm, m_hgrn_w_o, m_hgrn_lower_bounds, m_mlp_w_up, m_mlp_w_down, v_mix_norm, v_mlp_norm, v_final_norm, v_attn_w_qkv, v_attn_b_qkv, v_attn_sinks, v_attn_w_o, v_hgrn_w_in, v_hgrn_g_norm, v_hgrn_w_o, v_hgrn_lower_bounds, v_mlp_w_up, v_mlp_w_down):
    w = dict(mix_norm=mix_norm, mlp_norm=mlp_norm, final_norm=final_norm, attn_w_qkv=attn_w_qkv, attn_b_qkv=attn_b_qkv,
             attn_sinks=attn_sinks, attn_w_o=attn_w_o, hgrn_w_in=hgrn_w_in, hgrn_g_norm=hgrn_g_norm, hgrn_w_o=hgrn_w_o,
             hgrn_lower_bounds=hgrn_lower_bounds, mlp_w_up=mlp_w_up, mlp_w_down=mlp_w_down)
    m = dict(mix_norm=m_mix_norm, mlp_norm=m_mlp_norm, final_norm=m_final_norm, attn_w_qkv=m_attn_w_qkv,
             attn_b_qkv=m_attn_b_qkv, attn_sinks=m_attn_sinks, attn_w_o=m_attn_w_o, hgrn_w_in=m_hgrn_w_in,
             hgrn_g_norm=m_hgrn_g_norm, hgrn_w_o=m_hgrn_w_o, hgrn_lower_bounds=m_hgrn_lower_bounds, mlp_w_up=m_mlp_w_up,
             mlp_w_down=m_mlp_w_down)
    v = dict(mix_norm=v_mix_norm, mlp_norm=v_mlp_norm, final_norm=v_final_norm, attn_w_qkv=v_attn_w_qkv,
             attn_b_qkv=v_attn_b_qkv, attn_sinks=v_attn_sinks, attn_w_o=v_attn_w_o, hgrn_w_in=v_hgrn_w_in,
             hgrn_g_norm=v_hgrn_g_norm, hgrn_w_o=v_hgrn_w_o, hgrn_lower_bounds=v_hgrn_lower_bounds, mlp_w_up=v_mlp_w_up,
             mlp_w_down=v_mlp_w_down)
    me = 4 * lax.axis_index("x") + 2 * lax.axis_index("y") + lax.axis_index("c")

    gn = hgrn_g_norm.reshape(1, 128)
    gn_a = gn.astype(BF16)
    gn_b = (gn - gn_a.astype(F32)).astype(BF16)
    gn_c = (gn - gn_a.astype(F32) - gn_b.astype(F32)).astype(BF16)
    gn_rows = jnp.pad(jnp.concatenate([gn_a, gn_b, gn_c], axis=1), ((0, 15), (0, D_MODEL - 3 * 128)))
    first_rows = sum(rows for _, _, rows in GATHER_FIRST)
    full = {}
    got = _all_gather("gather_attn_weights", jnp.concatenate([_pack_pieces(w, GATHER_FIRST).astype(BF16), gn_rows], axis=0))
    _unpack_gathered(got[:, :first_rows], GATHER_FIRST, full)
    gn_terms = got[:, first_rows, :3 * 128].astype(F32).reshape(N_DEV, 3, 128)
    gn_full = ((gn_terms[:, 0] + gn_terms[:, 1]) + gn_terms[:, 2]).reshape(1, D_MODEL)

    x0 = x[0]
    tgt = loss_target[0]
    rot = _rotary_tables(positions)
    row = lambda a: a.reshape(1, -1)

    qkv, h0 = _norm_mm("qkv_proj", x0, row(mix_norm[0]), full["attn_w_qkv", None], attn_b_qkv, rot=rot)
    att, got = _attn_fwd(qkv, attn_sinks, carry=(_Gather, _pack_pieces(w, GATHER_REST).astype(BF16)))
    _unpack_gathered(got, GATHER_REST, full)
    x1 = _mm_res("attn_out_proj", att, full["attn_w_o", None], x0)
    u0, h1 = _norm_mm("mlp0_up", x1, row(mlp_norm[0]), full["mlp_w_up", 0])
    x2 = _mlp_down("mlp0_down", u0, full["mlp_w_down", 0], x1)
    z, h2 = _norm_mm("hgrn_in_proj", x2, row(mix_norm[1]), full["hgrn_w_in", None])
    o_raw, states = _hgrn_fwd(z, hgrn_lower_bounds)
    x3, o2 = _hgrn_out("hgrn_out_proj", o_raw, z, gn_full, full["hgrn_w_o", None], x2)
    u1, h3 = _norm_mm("mlp1_up", x3, row(mlp_norm[1]), full["mlp_w_up", 1])
    x4 = _mlp_down("mlp1_down", u1, full["mlp_w_down", 1], x3)
    dx4, loss_part, g_final = _loss_head("loss_head", x4, tgt, row(final_norm))

    gw = {}
    du1, a1 = _mlp_bwd_act("mlp1_bwd_act", dx4, u1, full["mlp_w_down", 1])
    dx3, g_mlp1 = _mm_nt_rmsbwd("mlp1_bwd_in", du1, full["mlp_w_up", 1], x3, row(mlp_norm[1]), dx4)
    gw["mlp_w_down", 1] = _mm_tn("mlp1_dw_down", a1, dx4)
    gw["mlp_w_up", 1] = _mm_tn("mlp1_dw_up", h3, du1)

    do_raw, dg, g_gn = _hgrn_out_bwd("hgrn_out_bwd", dx3, o_raw, z, full["hgrn_w_o", None], gn_full)
    gw["hgrn_w_o", None] = _mm_tn("hgrn_dw_o", o2, dx3)
    dzq, dzf, dzi, g_lb, recv0 = _hgrn_bwd(z, hgrn_lower_bounds, states, do_raw,
                                           carry=(_Exchange, _pack_grads(gw, GRAD_GROUPS[0])))
    dz = jnp.concatenate([dzq, dzf, dzi, dg], axis=1)
    dx2, g_mix1 = _mm_nt_rmsbwd("hgrn_in_bwd", dz, full["hgrn_w_in", None], x2, row(mix_norm[1]), dx3)
    gw["hgrn_w_in", None] = _mm_tn("hgrn_dw_in", h2, dz)

    du0, a0 = _mlp_bwd_act("mlp0_bwd_act", dx2, u0, full["mlp_w_down", 0])
    dx1, g_mlp0 = _mm_nt_rmsbwd("mlp0_bwd_in", du0, full["mlp_w_up", 0], x1, row(mlp_norm[0]), dx2)
    gw["mlp_w_down", 0] = _mm_tn("mlp0_dw_down", a0, dx2)
    gw["mlp_w_up", 0] = _mm_tn("mlp0_dw_up", h1, du0)

    datt = _mm_nt("attn_out_bwd", dx1, full["attn_w_o", None], BF16)
    gw["attn_w_o", None] = _mm_tn("attn_dw_o", att, dx1)
    dqkv, g_sink, recv1 = _attn_bwd(qkv, rot, attn_sinks, datt, carry=(_Exchange, _pack_grads(gw, GRAD_GROUPS[1])))
    gw["attn_w_qkv", None] = _mm_tn("attn_dw_qkv", h0, dqkv)
    dx0, g_mix0, g_bqkv, recv2 = _mm_nt_rmsbwd(
        "qkv_bwd", dqkv, full["attn_w_qkv", None], x0, row(mix_norm[0]), dx1, with_colsum=True,
        carry=(_Exchange, _pack_grads(gw, GRAD_GROUPS[2])))

    big = [{}, {}, {}, {}]
    for i, (pieces, recv) in enumerate(zip(GRAD_GROUPS, (recv0, recv1, recv2))):
        res = _adamw_sum(f"adamw_group{i}", recv, *[_pack_pieces(d, pieces) for d in (w, m, v)])
        for out, p in zip(big, res):
            _unpack_pieces(p, pieces, w, out)

    zero_row = jnp.zeros((1, D_MODEL), F32)
    part = _pack_small(dict(
        mix_norm=jnp.concatenate([g_mix0, g_mix1], axis=0), mlp_norm=jnp.concatenate([g_mlp0, g_mlp1], axis=0),
        final_norm=g_final, attn_b_qkv=g_bqkv, attn_sinks=g_sink[:, :N_Q_HEADS],
        hgrn_lower_bounds=jnp.concatenate([zero_row, g_lb], axis=0)), g_gn)

    def spread(a):
        return lax.dynamic_update_slice(zero_row, a.reshape(1, 128), (0, me * 128))

    small_in = [_pack_small({n: d[n] for n in SMALL_NAMES if n != "hgrn_g_norm"}, spread(d["hgrn_g_norm"]))
                for d in (w, m, v)]
    small = [_unpack_small(p, me) for p in _small_sync(part, *small_in)]

    loss = lax.psum(loss_part[0, 0], ("x", "y", "c"))
    outs = [loss, dx0.reshape(x.shape)]
    for grp_big, grp_small in zip(big, small):
        for name in WEIGHT_NAMES:
            if name in SMALL_NAMES:
                val = grp_small[name]
            elif (name, None) in grp_big:
                val = grp_big[name, None]
            else:
                val = jnp.stack([grp_big[name, 0], grp_big[name, 1]], axis=0)
            outs.append(val.reshape(w[name].shape))
    return tuple(outs)
```

```python
import functools

import jax
import jax.numpy as jnp
from jax import lax
from jax.experimental import pallas as pl
from jax.experimental.pallas import tpu as pltpu

F32 = jnp.float32
BF16 = jnp.bfloat16

D_MODEL = 1024
HEAD_DIM = 64
N_Q_HEADS = 16
Q_DIM = 1024
KV_DIM = 256
QKV_DIM = 1536
ATT_BLOCK = 128
ROT_HALF = 8
ROPE_THETA = 500000.0
NEG_INF = -1e30
HGRN_HEADS = 8
HGRN_DK = 128
CHUNK = 64
D_FF = 4096
NORM_EPS = 1e-5
N_DEV = 8

ADAM_LR = 0.001
ADAM_B1 = 0.9
ADAM_B2 = 0.999
ADAM_EPS = 1e-08
ADAM_WD = 0.01
ADAM_STEP = 10

LANES = 128
VMEM_LIMIT = 56 * 1024 * 1024

GATHER_FIRST = (("attn_w_qkv", 0), ("attn_w_o", 0))
GATHER_ATTN = (("mlp_w_up", 0), ("mlp_w_down", 0), ("hgrn_w_in", 0), ("hgrn_w_o", 0))
GATHER_HGRN = (("mlp_w_up", 1), ("mlp_w_down", 1))
GRAD_GROUPS = ((("mlp_w_down", 1), ("mlp_w_up", 1), ("hgrn_w_o", 0)),
               (("hgrn_w_in", 0), ("mlp_w_down", 0), ("mlp_w_up", 0), ("attn_w_o", 0)),
               (("attn_w_qkv", 0),))
COL_SHARDED = ("attn_w_qkv", "hgrn_w_in", "mlp_w_up")
BIG_NAMES = ("attn_w_qkv", "attn_w_o", "hgrn_w_in", "hgrn_w_o", "mlp_w_up", "mlp_w_down")
SMALL_ROWS = 16


def _dot(a, b):
    return jnp.dot(a, b, preferred_element_type=F32)


def _dot_nt(a, b):
    return lax.dot_general(a, b, (((1,), (1,)), ((), ())), preferred_element_type=F32)


def _dot_tn(a, b):
    return lax.dot_general(a, b, (((0,), (0,)), ((), ())), preferred_element_type=F32)


def _params(**kw):
    return pltpu.CompilerParams(vmem_limit_bytes=VMEM_LIMIT, **kw)


def _full_spec(a):
    nd = a.ndim
    return pl.BlockSpec(a.shape, lambda *_: (0,) * nd)


def _row_call(name, body, n_rows, tm, row_ins, full_ins, row_outs, acc_outs=(), carry=(None, None)):
    steps = n_rows // tm
    in_specs = [pl.BlockSpec((tm, w), functools.partial(lambda i, cb: (i, cb), cb=cb)) for _, w, cb in row_ins]
    in_specs += [_full_spec(a) for a in full_ins]
    out_shape = [jax.ShapeDtypeStruct((n_rows, w), dt) for w, dt in row_outs]
    out_specs = [pl.BlockSpec((tm, w), lambda i: (i, 0)) for w, _ in row_outs]
    for shp, dt in acc_outs:
        out_shape.append(jax.ShapeDtypeStruct(shp, dt))
        out_specs.append(pl.BlockSpec(shp, functools.partial(lambda i, nd: (0,) * nd, nd=len(shp))))
    n_in, n_out = len(in_specs), len(out_specs)
    in_specs, out_specs, out_shape, scratch, extra = _carried_specs(carry, in_specs, out_specs, out_shape, [])

    def wrapped(*refs):
        i = pl.program_id(0)
        own, finish = _carried(carry, refs, n_in, n_out, i == 0, i == steps - 1)
        body(*own)
        finish()

    return pl.pallas_call(
        wrapped, name=name, grid=(steps,), in_specs=in_specs, out_specs=out_specs, out_shape=out_shape,
        scratch_shapes=scratch, compiler_params=_params(dimension_semantics=("arbitrary",)),
    )(*[a for a, _, _ in row_ins], *full_ins, *extra)


def _rms(x, gain):
    r = lax.rsqrt(jnp.mean(x * x, axis=-1, keepdims=True) + NORM_EPS)
    xhat = x * r
    return xhat * gain, xhat, r


def _rms_bwd(dy, xhat, r, gain):
    dxhat = dy * gain
    dx = r * (dxhat - xhat * jnp.mean(dxhat * xhat, axis=-1, keepdims=True))
    return dx, dy * xhat


def _norm_mm(name, x, gain, w, bias=None, rot=None, tm=256):
    T = x.shape[0]
    tm = min(tm, T)
    nc = 512
    blocked = w.ndim == 3
    n = N_DEV * w.shape[2] if blocked else w.shape[1]
    assert n % nc == 0 and (not blocked or w.shape[2] == nc)

    def body(*refs):
        x_ref, refs = refs[0], refs[1:]
        if rot is not None:
            t_ref, refs = refs[0], refs[1:]
        g_ref, w_ref, refs = refs[0], refs[1], refs[2:]
        if bias is not None:
            b_ref, refs = refs[0], refs[1:]
        y_ref, h_ref = refs
        h, _, _ = _rms(x_ref[...], g_ref[...])
        hb = h.astype(BF16)
        h_ref[...] = hb
        for c in range(n // nc):
            sl = slice(c * nc, (c + 1) * nc)
            y = _dot(hb, w_ref[c] if blocked else w_ref[:, sl])
            if bias is not None:
                y = y + b_ref[:, sl]
            if rot is None:
                y_ref[:, sl] = y
            else:
                n_rot = max(0, min(nc, Q_DIM + KV_DIM - c * nc)) // LANES
                pieces = _rot_fwd(y[:, :n_rot * LANES], t_ref[...]) if n_rot else []
                for j in range(nc // LANES):
                    col = slice(c * nc + j * LANES, c * nc + (j + 1) * LANES)
                    y_ref[:, col] = pieces[j] if j < n_rot else y[:, j * LANES:(j + 1) * LANES]

    rows = [(x, D_MODEL, 0)] + ([(rot, 3 * LANES, 0)] if rot is not None else [])
    full = [gain, w] + ([bias] if bias is not None else [])
    return _row_call(name, body, T, tm, rows, full, [(n, F32), (D_MODEL, BF16)])


def _mm_res(name, a, w, res, tm=512):
    T = a.shape[0]
    tm = min(tm, T)

    def body(a_ref, r_ref, w_ref, o_ref):
        o_ref[...] = r_ref[...] + _dot(a_ref[...], w_ref[...])

    return _row_call(name, body, T, tm, [(a, a.shape[1], 0), (res, D_MODEL, 0)], [w], [(D_MODEL, F32)])[0]


def _mlp_down(name, u, w, res, tm=256):
    T = u.shape[0]
    tm = min(tm, T)
    kc = 1024

    def body(u_ref, r_ref, w_ref, o_ref):
        acc = r_ref[...]
        for c in range(D_FF // kc):
            sl = slice(c * kc, (c + 1) * kc)
            a = jnp.maximum(u_ref[:, sl], 0.0)
            acc = acc + _dot((a * a).astype(BF16), w_ref[sl, :])
        o_ref[...] = acc

    return _row_call(name, body, T, tm, [(u, D_FF, 0), (res, D_MODEL, 0)], [w], [(D_MODEL, F32)])[0]


def _hgrn_out(name, o_raw, z, gn, w, res, tm=256):
    T = o_raw.shape[0]
    tm = min(tm, T)

    def body(o_ref, g_ref, r_ref, gn_ref, w_ref, x_ref, a_ref):
        y, _, _ = _rms(o_ref[...], gn_ref[...])
        g = g_ref[...]
        a = (y * (g * jax.nn.sigmoid(g))).astype(BF16)
        a_ref[...] = a
        x_ref[...] = r_ref[...] + _dot(a, w_ref[...])

    return _row_call(name, body, T, tm, [(o_raw, D_MODEL, 0), (z, D_MODEL, 3), (res, D_MODEL, 0)], [gn, w],
                     [(D_MODEL, F32), (D_MODEL, BF16)])


def _loss_head(name, x, target, gain, tm=512):
    T = x.shape[0]
    tm = min(tm, T)

    def body(x_ref, t_ref, g_ref, dx_ref, loss_ref, dg_ref):
        @pl.when(pl.program_id(0) == 0)
        def _():
            loss_ref[...] = jnp.zeros_like(loss_ref)
            dg_ref[...] = jnp.zeros_like(dg_ref)

        gain_v = g_ref[...]
        y, xhat, r = _rms(x_ref[...], gain_v)
        diff = y - t_ref[...]
        row = jnp.sum(diff * diff, axis=-1, keepdims=True) * (1.0 / D_MODEL)
        loss_ref[...] += jnp.broadcast_to(0.5 * jnp.sum(row, axis=0, keepdims=True), loss_ref.shape)
        dy = diff * (1.0 / D_MODEL)
        dx, dgr = _rms_bwd(dy, xhat, r, gain_v)
        dx_ref[...] = dx
        dg_ref[...] += jnp.sum(dgr, axis=0, keepdims=True)

    return _row_call(name, body, T, tm, [(x, D_MODEL, 0), (target, D_MODEL, 0)], [gain], [(D_MODEL, F32)],
                     [((1, LANES), F32), ((1, D_MODEL), F32)])


def _mm_nt_rmsbwd(name, dy, w, x, gain, dres, tm=256, with_colsum=False, carry=(None, None)):
    T = x.shape[0]
    tm = min(tm, T)
    n = dy.shape[1]

    def body(*refs):
        if with_colsum:
            dy_ref, x_ref, dr_ref, w_ref, g_ref, dx_ref, dg_ref, cs_ref = refs
        else:
            dy_ref, x_ref, dr_ref, w_ref, g_ref, dx_ref, dg_ref = refs

        @pl.when(pl.program_id(0) == 0)
        def _():
            dg_ref[...] = jnp.zeros_like(dg_ref)
            if with_colsum:
                cs_ref[...] = jnp.zeros_like(cs_ref)

        dyv = dy_ref[...]
        if w.ndim == 3:
            nb = w.shape[2]
            dh = _dot_nt(dy_ref[:, :nb].astype(BF16), w_ref[0])
            for p in range(1, N_DEV):
                dh = dh + _dot_nt(dy_ref[:, p * nb:(p + 1) * nb].astype(BF16), w_ref[p])
        else:
            dh = _dot_nt(dyv.astype(BF16), w_ref[...])
        gain_v = g_ref[...]
        _, xhat, r = _rms(x_ref[...], gain_v)
        dx, dgr = _rms_bwd(dh, xhat, r, gain_v)
        dx_ref[...] = dr_ref[...] + dx
        dg_ref[...] += jnp.sum(dgr, axis=0, keepdims=True)
        if with_colsum:
            cs_ref[...] += jnp.sum(dyv.astype(F32), axis=0, keepdims=True)

    acc = [((1, D_MODEL), F32)] + ([((1, n), F32)] if with_colsum else [])
    return _row_call(name, body, T, tm, [(dy, n, 0), (x, D_MODEL, 0), (dres, D_MODEL, 0)], [w, gain],
                     [(D_MODEL, F32)], acc, carry=carry)


def _mm_nt(name, dy, w, out_dtype, tm=512):
    T = dy.shape[0]
    tm = min(tm, T)
    k = w.shape[0]

    def body(dy_ref, w_ref, o_ref):
        o_ref[...] = _dot_nt(dy_ref[...].astype(BF16), w_ref[...]).astype(out_dtype)

    return _row_call(name, body, T, tm, [(dy, dy.shape[1], 0)], [w], [(k, out_dtype)])[0]


def _mlp_bwd_act(name, dy, u, w_down, tm=256):
    T = u.shape[0]
    tm = min(tm, T)
    kc = 1024

    def body(dy_ref, u_ref, w_ref, du_ref, a_ref):
        dyb = dy_ref[...].astype(BF16)
        for c in range(D_FF // kc):
            sl = slice(c * kc, (c + 1) * kc)
            a = jnp.maximum(u_ref[:, sl], 0.0)
            da = _dot_nt(dyb, w_ref[sl, :])
            du_ref[:, sl] = (da * (2.0 * a)).astype(BF16)
            a_ref[:, sl] = (a * a).astype(BF16)

    return _row_call(name, body, T, tm, [(dy, D_MODEL, 0), (u, D_FF, 0)], [w_down], [(D_FF, BF16), (D_FF, BF16)])


def _hgrn_out_bwd(name, dx, o_raw, z, w, gn, tm=256):
    T = dx.shape[0]
    tm = min(tm, T)

    def body(dx_ref, o_ref, g_ref, w_ref, gn_ref, do_ref, dg_ref, dgn_ref):
        @pl.when(pl.program_id(0) == 0)
        def _():
            dgn_ref[...] = jnp.zeros_like(dgn_ref)

        da = _dot_nt(dx_ref[...].astype(BF16), w_ref[...])
        gn_v = gn_ref[...]
        y, xhat, r = _rms(o_ref[...], gn_v)
        g = g_ref[...]
        sg = jax.nn.sigmoid(g)
        dg_ref[...] = (da * y * (sg * (1.0 + g * (1.0 - sg)))).astype(BF16)
        dyn = da * (g * sg)
        do, dgr = _rms_bwd(dyn, xhat, r, gn_v)
        do_ref[...] = do
        dgn_ref[...] += jnp.sum(dgr, axis=0, keepdims=True)

    return _row_call(name, body, T, tm, [(dx, D_MODEL, 0), (o_raw, D_MODEL, 0), (z, D_MODEL, 3)], [w, gn],
                     [(D_MODEL, F32), (D_MODEL, BF16)], [((1, D_MODEL), F32)])


def _mm_tn(name, a, b, shard=None, bm=1024, bn=512, tk=2048):
    T, M = a.shape
    N = b.shape[1]
    bm, bn, tk = min(bm, M), min(bn, N), min(tk, T)
    nk = T // tk
    if shard is None:
        out_shape, out_block = jax.ShapeDtypeStruct((M, N), F32), (bm, bn)
        out_map = lambda i, j, k: (i, j)
    elif shard == "cols":
        assert bn == N // N_DEV
        out_shape, out_block = jax.ShapeDtypeStruct((N_DEV, M, bn), BF16), (1, bm, bn)
        out_map = lambda i, j, k: (j, i, 0)
    else:
        rows = M // N_DEV
        assert bm % rows == 0
        out_shape, out_block = jax.ShapeDtypeStruct((N_DEV, rows, N), BF16), (bm // rows, rows, bn)
        out_map = lambda i, j, k: (i, 0, j)

    def body(a_ref, b_ref, o_ref, acc):
        k = pl.program_id(2)

        @pl.when(k == 0)
        def _():
            acc[...] = jnp.zeros_like(acc)

        acc[...] += _dot_tn(a_ref[...].astype(BF16), b_ref[...].astype(BF16))

        @pl.when(k == nk - 1)
        def _():
            o_ref[...] = acc[...].reshape(out_block).astype(o_ref.dtype)

    return pl.pallas_call(
        body, name=name, grid=(M // bm, N // bn, nk),
        in_specs=[pl.BlockSpec((tk, bm), lambda i, j, k: (k, i)), pl.BlockSpec((tk, bn), lambda i, j, k: (k, j))],
        out_specs=pl.BlockSpec(out_block, out_map), out_shape=out_shape,
        scratch_shapes=[pltpu.VMEM((bm, bn), F32)],
        compiler_params=_params(dimension_semantics=("parallel", "parallel", "arbitrary")),
    )(a, b)


def _rot_fwd(x, tab):
    c, sa, sb = tab[:, :LANES], tab[:, LANES:2 * LANES], tab[:, 2 * LANES:]
    outs = []
    for j in range(x.shape[1] // LANES):
        xs = x[:, j * LANES:(j + 1) * LANES]
        outs.append(xs * c + pltpu.roll(xs, ROT_HALF, 1) * sa + pltpu.roll(xs, LANES - ROT_HALF, 1) * sb)
    return outs


def _rot_bwd(dys, tab):
    c, sa, sb = tab[:, :LANES], tab[:, LANES:2 * LANES], tab[:, 2 * LANES:]
    return [dy * c + pltpu.roll(dy * sa, LANES - ROT_HALF, 1) + pltpu.roll(dy * sb, ROT_HALF, 1) for dy in dys]


ATT_SCALE = HEAD_DIM ** -0.5
ATT_ROWS = 128


def _attn_masks(n):
    qi = lax.broadcasted_iota(jnp.int32, (ATT_BLOCK, 2 * ATT_BLOCK), 0)
    kj = lax.broadcasted_iota(jnp.int32, (ATT_BLOCK, 2 * ATT_BLOCK), 1)
    delta = qi + ATT_BLOCK - kj
    first_key = jnp.where(n > 0, 0, ATT_BLOCK)
    valid = (delta >= 0) & (delta < ATT_BLOCK) & (kj >= first_key)
    lane = lax.broadcasted_iota(jnp.int32, (1, LANES), 1)
    return valid, lane < HEAD_DIM


def _attn_probs(qm, k_use, valid, sink):
    s = jnp.where(valid, _dot_nt(qm, k_use), NEG_INF)
    m = jnp.maximum(jnp.max(s, axis=-1, keepdims=True), sink)
    e = jnp.exp(s - m)
    es = jnp.exp(sink - m)
    inv = 1.0 / (jnp.sum(e, axis=-1, keepdims=True) + es)
    return e * inv, es * inv


def _attn_specs(nb, tables):
    prev = lambda n: jnp.maximum(jnp.minimum(n, nb - 1) - 1, 0)
    cur = lambda n: jnp.minimum(n, nb - 1)
    specs = [
        pl.BlockSpec((ATT_BLOCK, Q_DIM), lambda n: (cur(n), 0)),
        pl.BlockSpec((ATT_BLOCK, KV_DIM), lambda n: (prev(n), 4)),
        pl.BlockSpec((ATT_BLOCK, KV_DIM), lambda n: (cur(n), 4)),
        pl.BlockSpec((ATT_BLOCK, KV_DIM), lambda n: (prev(n), 5)),
        pl.BlockSpec((ATT_BLOCK, KV_DIM), lambda n: (cur(n), 5)),
    ]
    if tables:
        specs += [pl.BlockSpec((ATT_BLOCK, 3 * LANES), lambda n: (prev(n), 0)),
                  pl.BlockSpec((ATT_BLOCK, 3 * LANES), lambda n: (cur(n), 0))]
    return specs + [pl.BlockSpec(memory_space=pltpu.SMEM)]


def _kv_band(kp_ref, kc_ref, vp_ref, vc_ref):
    ks, vs = [], []
    for j in range(KV_DIM // LANES):
        sl = slice(j * LANES, (j + 1) * LANES)
        kb = jnp.concatenate([kp_ref[:, sl], kc_ref[:, sl]], axis=0)
        vb = jnp.concatenate([vp_ref[:, sl], vc_ref[:, sl]], axis=0)
        ks.append((kb.astype(BF16), pltpu.roll(kb, HEAD_DIM, 1).astype(BF16)))
        vs.append((vb.astype(BF16), pltpu.roll(vb, HEAD_DIM, 1).astype(BF16)))
    return ks, vs


def _attn_fwd(qkv, sinks, carry=(None, None)):
    T = qkv.shape[0]
    nb = T // ATT_BLOCK

    def body(*refs):
        n = pl.program_id(0)
        own, finish = _carried(carry, refs, 6, 1, n == 0, n == nb - 1)
        q_ref, kp_ref, kc_ref, vp_ref, vc_ref, sink_ref, o_ref = own
        valid, low = _attn_masks(n)
        ks, vs = _kv_band(kp_ref, kc_ref, vp_ref, vc_ref)
        for p in range(Q_DIM // LANES):
            kpair, khalf = p // 4, (p // 2) % 2
            q_pair = q_ref[:, p * LANES:(p + 1) * LANES] * ATT_SCALE
            for r0 in range(0, ATT_BLOCK, ATT_ROWS):
                rows = slice(r0, r0 + ATT_ROWS)
                outs = []
                for hf in range(2):
                    qm = jnp.where(low if hf == 0 else ~low, q_pair[rows], 0.0).astype(BF16)
                    sw = 0 if khalf == hf else 1
                    pr, _ = _attn_probs(qm, ks[kpair][sw], valid[rows], sink_ref[0, 2 * p + hf])
                    outs.append(_dot(pr.astype(BF16), vs[kpair][sw]))
                o_ref[rows, p * LANES:(p + 1) * LANES] = jnp.where(low, outs[0], outs[1]).astype(BF16)
        finish()

    in_specs, out_specs, out_shape, scratch, extra = _carried_specs(
        carry, _attn_specs(nb, False), [pl.BlockSpec((ATT_BLOCK, Q_DIM), lambda n: (n, 0))],
        [jax.ShapeDtypeStruct((T, Q_DIM), BF16)], [])
    return pl.pallas_call(
        body, name="attn_fwd", grid=(nb,), in_specs=in_specs, out_specs=out_specs, out_shape=out_shape,
        scratch_shapes=scratch, compiler_params=_params(dimension_semantics=("arbitrary",)),
    )(qkv, qkv, qkv, qkv, qkv, sinks, *extra)


def _attn_bwd(qkv, rot, sinks, dout, carry=(None, None)):
    T = qkv.shape[0]
    nb = T // ATT_BLOCK
    npair = KV_DIM // LANES

    def body(*refs):
        n = pl.program_id(0)
        own, finish = _carried(carry, refs, 9, 2, n == 0, n == nb)
        (q_ref, kp_ref, kc_ref, vp_ref, vc_ref, tp_ref, tc_ref, sink_ref, do_ref, dqkv_ref, dsink_ref,
         dq_c, dk_c, dv_c) = own

        @pl.when(n == 0)
        def _():
            dq_c[...] = jnp.zeros_like(dq_c)
            dk_c[...] = jnp.zeros_like(dk_c)
            dv_c[...] = jnp.zeros_like(dv_c)
            dsink_ref[...] = jnp.zeros_like(dsink_ref)

        def flush(dk_prev, dv_prev, tab_ref):
            dqkv_ref[:, :Q_DIM] = dq_c[...]
            dk = _rot_bwd([dk_c[:, j * LANES:(j + 1) * LANES] + dk_prev[j] for j in range(npair)], tab_ref[...])
            for j in range(npair):
                dqkv_ref[:, Q_DIM + j * LANES:Q_DIM + (j + 1) * LANES] = dk[j]
                dqkv_ref[:, Q_DIM + KV_DIM + j * LANES:Q_DIM + KV_DIM + (j + 1) * LANES] = (
                    dv_c[:, j * LANES:(j + 1) * LANES] + dv_prev[j])

        @pl.when(n < nb)
        def _():
            valid, low = _attn_masks(n)
            lane = lax.broadcasted_iota(jnp.int32, (1, LANES), 1)
            ks, vs = _kv_band(kp_ref, kc_ref, vp_ref, vc_ref)
            dk_acc = [[jnp.zeros((2 * ATT_BLOCK, LANES), F32) for _ in range(2)] for _ in range(npair)]
            dv_acc = [[jnp.zeros((2 * ATT_BLOCK, LANES), F32) for _ in range(2)] for _ in range(npair)]
            dsink = jnp.zeros((1, LANES), F32)
            dqs = []
            for p in range(Q_DIM // LANES):
                kpair, khalf = p // 4, (p // 2) % 2
                q_pair = q_ref[:, p * LANES:(p + 1) * LANES] * ATT_SCALE
                do_pair = do_ref[:, p * LANES:(p + 1) * LANES]
                dq_rows = []
                for r0 in range(0, ATT_BLOCK, ATT_ROWS):
                    rows = slice(r0, r0 + ATT_ROWS)
                    dq_h = []
                    for hf in range(2):
                        sel = low if hf == 0 else ~low
                        qm = jnp.where(sel, q_pair[rows], 0.0).astype(BF16)
                        dom = jnp.where(sel, do_pair[rows], 0.0).astype(BF16)
                        sw = 0 if khalf == hf else 1
                        k_use, v_use = ks[kpair][sw], vs[kpair][sw]
                        pr, ps = _attn_probs(qm, k_use, valid[rows], sink_ref[0, 2 * p + hf])
                        dp = _dot_nt(dom, v_use)
                        dd = jnp.sum(pr * dp, axis=-1, keepdims=True)
                        ds = (pr * (dp - dd)).astype(BF16)
                        dq_h.append(_dot(ds, k_use))
                        dk_acc[kpair][sw] = dk_acc[kpair][sw] + _dot_tn(ds, qm)
                        dv_acc[kpair][sw] = dv_acc[kpair][sw] + _dot_tn(pr.astype(BF16), dom)
                        dsink = dsink + jnp.where(lane == 2 * p + hf, -jnp.sum(ps * dd, axis=0, keepdims=True), 0.0)
                    dq_rows.append(jnp.where(low, dq_h[0], dq_h[1]) * ATT_SCALE)
                dqs.append(jnp.concatenate(dq_rows, axis=0))
            dk_acc = [a[0] + pltpu.roll(a[1], HEAD_DIM, 1) for a in dk_acc]
            dv_acc = [a[0] + pltpu.roll(a[1], HEAD_DIM, 1) for a in dv_acc]
            flush([a[:ATT_BLOCK] for a in dk_acc], [a[:ATT_BLOCK] for a in dv_acc], tp_ref)
            dq = _rot_bwd(dqs, tc_ref[...])
            for p in range(Q_DIM // LANES):
                dq_c[:, p * LANES:(p + 1) * LANES] = dq[p]
            for j in range(npair):
                dk_c[:, j * LANES:(j + 1) * LANES] = dk_acc[j][ATT_BLOCK:]
                dv_c[:, j * LANES:(j + 1) * LANES] = dv_acc[j][ATT_BLOCK:]
            dsink_ref[...] += dsink

        @pl.when(n == nb)
        def _():
            zero = [jnp.zeros((ATT_BLOCK, LANES), F32) for _ in range(npair)]
            flush(zero, zero, tc_ref)

        finish()

    do_spec = pl.BlockSpec((ATT_BLOCK, Q_DIM), lambda n: (jnp.minimum(n, nb - 1), 0))
    in_specs, out_specs, out_shape, scratch, extra = _carried_specs(
        carry, _attn_specs(nb, True) + [do_spec],
        [pl.BlockSpec((ATT_BLOCK, QKV_DIM), lambda n: (jnp.maximum(n - 1, 0), 0)),
         pl.BlockSpec((1, LANES), lambda n: (0, 0))],
        [jax.ShapeDtypeStruct((T, QKV_DIM), F32), jax.ShapeDtypeStruct((1, LANES), F32)],
        [pltpu.VMEM((ATT_BLOCK, Q_DIM), F32), pltpu.VMEM((ATT_BLOCK, KV_DIM), F32),
         pltpu.VMEM((ATT_BLOCK, KV_DIM), F32)])
    return pl.pallas_call(
        body, name="attn_bwd", grid=(nb + 1,), in_specs=in_specs, out_specs=out_specs, out_shape=out_shape,
        scratch_shapes=scratch, compiler_params=_params(dimension_semantics=("arbitrary",)),
    )(qkv, qkv, qkv, qkv, qkv, rot, rot, sinks, dout, *extra)


LEVELS = (32, 16, 8)
DIAG = 8
UNROLL = 2


def _lower_bound(lb_ref):
    l0, l1 = lb_ref[0:1, :], lb_ref[1:2, :]
    mx = jnp.maximum(l0, l1)
    e0, e1 = jnp.exp(l0 - mx), jnp.exp(l1 - mx)
    return e1 / (e0 + e1)


def _cumsum_rows(x, row):
    for sh in (1, 2, 4, 8, 16, 32):
        x = x + jnp.where(row >= sh, pltpu.roll(x, sh, 0), 0.0)
    return x


def _rev_cumsum_rows(x, row):
    for sh in (1, 2, 4, 8, 16, 32):
        x = x + jnp.where(row < CHUNK - sh, pltpu.roll(x, CHUNK - sh, 0), 0.0)
    return x


def _level_masks():
    t = lax.broadcasted_iota(jnp.int32, (CHUNK, CHUNK), 0)
    s = lax.broadcasted_iota(jnp.int32, (CHUNK, CHUNK), 1)
    return [((t & h) != 0) & ((s & h) == 0) & ((t ^ s) < 2 * h) for h in LEVELS]


def _level_scale(b, h):
    parts = [jnp.broadcast_to(b[j * 2 * h + h - 1:j * 2 * h + h, :], (2 * h, HGRN_DK)) for j in range(CHUNK // (2 * h))]
    mid = parts[0] if len(parts) == 1 else jnp.concatenate(parts, axis=0)
    return jnp.exp(-jnp.abs(b - mid))


def _hgrn_gates(zq, zf, lb):
    sq = jax.nn.sigmoid(zq)
    q = zq * sq
    sg = jax.nn.sigmoid(zf)
    forget = lb + (1.0 - lb) * sg
    return q, sq, sg, forget, 1.0 - forget, jnp.log(forget)


def _hgrn_specs(T, rb, rev):
    nr = T // rb
    ri = (lambda r: nr - 1 - r) if rev else (lambda r: r)
    return nr, ri, [
        pl.BlockSpec((rb, HGRN_DK), lambda h, r: (ri(r), h)),
        pl.BlockSpec((rb, HGRN_DK), lambda h, r: (ri(r), HGRN_HEADS + h)),
        pl.BlockSpec((rb, HGRN_DK), lambda h, r: (ri(r), 2 * HGRN_HEADS + h)),
        pl.BlockSpec((2, HGRN_DK), lambda h, r: (0, h)),
    ]


def _hgrn_fwd(z, lb_raw, rb=512, carry=(None, None)):
    T = z.shape[0]
    rb = min(rb, T)
    ncb = rb // CHUNK
    nr, ri, in_specs = _hgrn_specs(T, rb, False)

    def body(*refs):
        hh, rr = pl.program_id(0), pl.program_id(1)
        own, finish = _carried(carry, refs, 4, 2, (hh == 0) & (rr == 0), (hh == HGRN_HEADS - 1) & (rr == nr - 1))
        zq_ref, zf_ref, zi_ref, lb_ref, o_ref, st_ref, state = own

        @pl.when(rr == 0)
        def _():
            state[...] = jnp.zeros_like(state)

        lb = _lower_bound(lb_ref)
        row = lax.broadcasted_iota(jnp.int32, (CHUNK, HGRN_DK), 0)
        masks = _level_masks()

        def chunk(c, st):
            rows = pl.ds(pl.multiple_of(c * CHUNK, CHUNK), CHUNK)
            q, _, _, _, k, lf = _hgrn_gates(zq_ref[rows, :], zf_ref[rows, :], lb)
            v = zi_ref[rows, :]
            vb = v.astype(BF16)
            b = _cumsum_rows(lf, row)
            sc = jnp.zeros((CHUNK, CHUNK), F32)
            for h, mask in zip(LEVELS, masks):
                e = _level_scale(b, h)
                sc = sc + jnp.where(mask, _dot_nt((q * e).astype(BF16), (k * e).astype(BF16)), 0.0)
            o = _dot(sc.astype(BF16), vb)
            for d in range(DIAG):
                if d == 0:
                    w = q * k
                    vr = v
                else:
                    w = jnp.where((row & (DIAG - 1)) >= d,
                                  q * pltpu.roll(k, d, 0) * jnp.exp(b - pltpu.roll(b, d, 0)), 0.0)
                    vr = pltpu.roll(v, d, 0)
                o = o + jnp.sum(w, axis=-1, keepdims=True) * vr
            b_last = b[CHUNK - 1:CHUNK, :]
            kd = (k * jnp.exp(b_last - b)).astype(BF16)
            qd = (q * jnp.exp(b)).astype(BF16)
            st_ref[c, 0] = st
            o_ref[rows, :] = o + _dot_nt(qd, st.astype(BF16))
            return st * jnp.exp(b_last) + _dot_tn(vb, kd)

        def group(i, st):
            for j in range(UNROLL):
                st = chunk(i * UNROLL + j, st)
            return st

        state[...] = lax.fori_loop(0, ncb // UNROLL, group, state[...])
        finish()

    in_specs, out_specs, out_shape, scratch, extra = _carried_specs(
        carry, in_specs,
        [pl.BlockSpec((rb, HGRN_DK), lambda h, r: (r, h)),
         pl.BlockSpec((ncb, 1, HGRN_DK, HGRN_DK), lambda h, r: (r, h, 0, 0))],
        [jax.ShapeDtypeStruct((T, D_MODEL), F32),
         jax.ShapeDtypeStruct((T // CHUNK, HGRN_HEADS, HGRN_DK, HGRN_DK), F32)],
        [pltpu.VMEM((HGRN_DK, HGRN_DK), F32)])
    return pl.pallas_call(
        body, name="hgrn_fwd", grid=(HGRN_HEADS, nr), in_specs=in_specs, out_specs=out_specs, out_shape=out_shape,
        scratch_shapes=scratch, compiler_params=_params(dimension_semantics=("arbitrary", "arbitrary")),
    )(z, z, z, lb_raw, *extra)


def _hgrn_bwd(z, lb_raw, states, do, rb=512, carry=(None, None)):
    T = z.shape[0]
    rb = min(rb, T)
    ncb = rb // CHUNK
    nr, ri, in_specs = _hgrn_specs(T, rb, True)
    in_specs += [pl.BlockSpec((ncb, 1, HGRN_DK, HGRN_DK), lambda h, r: (ri(r), h, 0, 0)),
                 pl.BlockSpec((rb, HGRN_DK), lambda h, r: (ri(r), h))]

    def body(*refs):
        hh, rr = pl.program_id(0), pl.program_id(1)
        own, finish = _carried(carry, refs, 6, 4, (hh == 0) & (rr == 0), (hh == HGRN_HEADS - 1) & (rr == nr - 1))
        zq_ref, zf_ref, zi_ref, lb_ref, st_ref, do_ref, dq_ref, df_ref, di_ref, dlb_ref, dstate = own

        @pl.when(rr == 0)
        def _():
            dstate[...] = jnp.zeros_like(dstate)
            dlb_ref[...] = jnp.zeros_like(dlb_ref)

        lb = _lower_bound(lb_ref)
        row = lax.broadcasted_iota(jnp.int32, (CHUNK, HGRN_DK), 0)
        masks = _level_masks()

        def chunk(ci, dlb):
            c = ncb - 1 - ci
            rows = pl.ds(pl.multiple_of(c * CHUNK, CHUNK), CHUNK)
            zq = zq_ref[rows, :]
            q, sq, sg, forget, k, lf = _hgrn_gates(zq, zf_ref[rows, :], lb)
            v = zi_ref[rows, :]
            dov = do_ref[rows, :]
            b = _cumsum_rows(lf, row)
            st = st_ref[c, 0]
            dst = dstate[...]
            b_last = b[CHUNK - 1:CHUNK, :]
            eb = jnp.exp(b)
            ebb = jnp.exp(b_last - b)
            e_last = jnp.exp(b_last)
            dob, vb, stb, dstb = dov.astype(BF16), v.astype(BF16), st.astype(BF16), dst.astype(BF16)
            dq = eb * _dot(dob, stb)
            dv = _dot_nt((k * ebb).astype(BF16), dstb)
            dk = ebb * _dot(vb, dstb)
            extra = e_last * jnp.sum(dst * st, axis=0, keepdims=True) + jnp.sum(k * dk, axis=0, keepdims=True)
            da = _dot_nt(dob, vb)
            sc = jnp.zeros((CHUNK, CHUNK), F32)
            for h, mask in zip(LEVELS, masks):
                e = _level_scale(b, h)
                qs, ks = (q * e).astype(BF16), (k * e).astype(BF16)
                dam = jnp.where(mask, da, 0.0).astype(BF16)
                dq = dq + e * _dot(dam, ks)
                dk = dk + e * _dot_tn(dam, qs)
                sc = sc + jnp.where(mask, _dot_nt(qs, ks), 0.0)
            dv = dv + _dot_tn(sc.astype(BF16), dob)
            for d in range(DIAG):
                if d == 0:
                    dad = jnp.sum(dov * v, axis=-1, keepdims=True)
                    dq = dq + dad * k
                    dk = dk + dad * q
                    dv = dv + jnp.sum(q * k, axis=-1, keepdims=True) * dov
                else:
                    w = jnp.where((row & (DIAG - 1)) >= d, jnp.exp(b - pltpu.roll(b, d, 0)), 0.0)
                    kr = pltpu.roll(k, d, 0)
                    dad = jnp.sum(dov * pltpu.roll(v, d, 0), axis=-1, keepdims=True)
                    ad = jnp.sum(q * kr * w, axis=-1, keepdims=True)
                    dq = dq + dad * kr * w
                    dk = dk + pltpu.roll(dad * q * w, CHUNK - d, 0)
                    dv = dv + pltpu.roll(ad * dov, CHUNK - d, 0)
            dlf = _rev_cumsum_rows(q * dq - k * dk, row) + extra
            dstate[...] = dst * e_last + _dot_tn(dob, (q * eb).astype(BF16))
            dforget = dlf / forget - dk
            dq_ref[rows, :] = (dq * (sq * (1.0 + zq * (1.0 - sq)))).astype(BF16)
            df_ref[rows, :] = (dforget * (1.0 - lb) * sg * (1.0 - sg)).astype(BF16)
            di_ref[rows, :] = dv.astype(BF16)
            return dlb + jnp.sum(dforget * (1.0 - sg), axis=0, keepdims=True)

        def group(i, dlb):
            for j in range(UNROLL):
                dlb = chunk(i * UNROLL + j, dlb)
            return dlb

        dlb_ref[...] += lax.fori_loop(0, ncb // UNROLL, group, jnp.zeros((1, HGRN_DK), F32))
        finish()

    blk = pl.BlockSpec((rb, HGRN_DK), lambda h, r: (ri(r), h))
    in_specs, out_specs, out_shape, scratch, extra = _carried_specs(
        carry, in_specs, [blk, blk, blk, pl.BlockSpec((1, HGRN_DK), lambda h, r: (0, h))],
        [jax.ShapeDtypeStruct((T, D_MODEL), BF16)] * 3 + [jax.ShapeDtypeStruct((1, D_MODEL), F32)],
        [pltpu.VMEM((HGRN_DK, HGRN_DK), F32)])
    return pl.pallas_call(
        body, name="hgrn_bwd", grid=(HGRN_HEADS, nr), in_specs=in_specs, out_specs=out_specs, out_shape=out_shape,
        scratch_shapes=scratch, compiler_params=_params(dimension_semantics=("arbitrary", "arbitrary")),
    )(z, z, z, lb_raw, states, do, *extra)


MESH = pl.DeviceIdType.MESH
ANY = pl.BlockSpec(memory_space=pl.ANY)


def _place():
    return lax.axis_index("x"), lax.axis_index("y"), lax.axis_index("c")


def _sems(n):
    return [pltpu.SemaphoreType.DMA((7 * n,)), pltpu.SemaphoreType.DMA((7 * n,)), pltpu.SemaphoreType.DMA((n,))]


class _Gather:
    def __init__(self, x_ref, out_ref, send_sems, recv_sems, local_sems, idx):
        self.x_ref, self.out_ref, self.send_sems, self.recv_sems, self.local_sem, self.base = (
            x_ref, out_ref, send_sems, recv_sems, local_sems.at[idx], 7 * idx)
        x, y, c = _place()
        self.c = c
        self.me, self.sibling = (x, y, c), (x, y, 1 - c)
        self.chips = [(1 - x, y), (x, 1 - y), (1 - x, 1 - y)]

    def rows(self, px, py, pc):
        return self.out_ref.at[4 * px + 2 * py + pc]

    def copy(self, k, block, to, from_input=False):
        return pltpu.make_async_remote_copy(
            src_ref=self.x_ref if from_input else self.rows(*block), dst_ref=self.rows(*block),
            send_sem=self.send_sems.at[self.base + k], recv_sem=self.recv_sems.at[self.base + k], device_id=to,
            device_id_type=MESH)

    def first(self):
        out = [self.copy(0, self.me, self.sibling, from_input=True)]
        return out + [self.copy(1 + j, self.me, (*chip, self.c), from_input=True) for j, chip in enumerate(self.chips)]

    def start(self):
        pltpu.make_async_copy(self.x_ref, self.rows(*self.me), self.local_sem).start()
        for cp in self.first():
            cp.start()

    def finish(self):
        passed = [self.copy(4 + j, (*chip, self.c), self.sibling) for j, chip in enumerate(self.chips)]
        for j, chip in enumerate(self.chips):
            self.copy(1 + j, (*chip, self.c), self.me).wait_recv()
            passed[j].start()
        self.copy(0, self.sibling, self.me).wait_recv()
        for j, chip in enumerate(self.chips):
            self.copy(4 + j, (*chip, 1 - self.c), self.me).wait_recv()
        for cp in self.first() + passed:
            cp.wait_send()
        pltpu.make_async_copy(self.x_ref, self.rows(*self.me), self.local_sem).wait()


class _Many:
    def __init__(self, kind, in_refs, out_refs, send_sems, recv_sems, local_sems):
        self.ops = [kind(x, o, send_sems, recv_sems, local_sems, i) for i, (x, o) in enumerate(zip(in_refs, out_refs))]

    def start(self):
        for op in self.ops:
            op.start()

    def finish(self):
        for op in self.ops:
            op.finish()


def _result_shapes(kind, arrs):
    return [jax.ShapeDtypeStruct(a.shape if kind is _Exchange else (N_DEV,) + a.shape, a.dtype) for a in arrs]


def _all_gather(name, shards):
    n = len(shards)

    def body(*refs):
        g = _Many(_Gather, refs[:n], refs[n:2 * n], *refs[2 * n:])
        g.start()
        g.finish()

    return pl.pallas_call(
        body, name=name, out_shape=_result_shapes(_Gather, shards), in_specs=[ANY] * n, out_specs=[ANY] * n,
        scratch_shapes=_sems(n),
    )(*shards)


def _peers(x, y, c):
    out = []
    for k in range(1, N_DEV):
        px = 1 - x if k & 4 else x
        py = 1 - y if k & 2 else y
        pc = 1 - c if k & 1 else c
        out.append((k, (px, py, pc), 4 * px + 2 * py + pc))
    return out


class _Exchange:
    def __init__(self, g_ref, recv_ref, send_sems, recv_sems, local_sems, idx):
        x, y, c = _place()
        me = 4 * x + 2 * y + c
        self.local = pltpu.make_async_copy(g_ref.at[me], recv_ref.at[me], local_sems.at[idx])
        self.copies = [
            pltpu.make_async_remote_copy(
                src_ref=g_ref.at[pidx], dst_ref=recv_ref.at[me], send_sem=send_sems.at[7 * idx + k - 1],
                recv_sem=recv_sems.at[7 * idx + k - 1], device_id=peer, device_id_type=MESH)
            for k, peer, pidx in _peers(x, y, c)]

    def start(self):
        self.local.start()
        for cp in self.copies:
            cp.start()

    def finish(self):
        for cp in self.copies:
            cp.wait()
        self.local.wait()


def _carried(carry, refs, n_in, n_out, first, last):
    kind, arrs = carry
    if kind is None:
        return refs, lambda: None
    n = len(arrs)
    ins, rest = refs[:n_in], refs[n_in + n:]
    outs, scratch = rest[:n_out], rest[n_out + n:]
    op = _Many(kind, refs[n_in:n_in + n], rest[n_out:n_out + n], *scratch[len(scratch) - 3:])

    @pl.when(first)
    def _():
        op.start()

    def finish():
        @pl.when(last)
        def _():
            op.finish()

    return tuple(ins) + tuple(outs) + tuple(scratch[:len(scratch) - 3]), finish


def _carried_specs(carry, in_specs, out_specs, out_shape, scratch):
    kind, arrs = carry
    if kind is None:
        return in_specs, out_specs, out_shape, scratch, []
    n = len(arrs)
    return (list(in_specs) + [ANY] * n, list(out_specs) + [ANY] * n,
            list(out_shape) + _result_shapes(kind, arrs), list(scratch) + _sems(n), list(arrs))


def _adamw(w, g, m, v):
    m = ADAM_B1 * m + (1.0 - ADAM_B1) * g
    v = ADAM_B2 * v + (1.0 - ADAM_B2) * (g * g)
    m_hat = m / (1.0 - ADAM_B1 ** ADAM_STEP)
    v_hat = v / (1.0 - ADAM_B2 ** ADAM_STEP)
    delta = -ADAM_LR * (m_hat / (jnp.sqrt(v_hat) + ADAM_EPS) + ADAM_WD * w)
    return delta, m, v


def _adamw_sum(name, recvs, w, m, v):
    L, R, C = w.shape
    tm = 128 if R % 128 == 0 else 64
    assert R % tm == 0 and len(recvs) == L

    def body(*refs):
        r_refs, (w_ref, m_ref, v_ref, g_ref, d_ref, nm_ref, nv_ref) = refs[:L], refs[L:]
        for l in range(L):
            g = r_refs[l][0].astype(F32)
            for s in range(1, N_DEV):
                g = g + r_refs[l][s].astype(F32)
            g_ref[l] = g
            d_ref[l], nm_ref[l], nv_ref[l] = _adamw(w_ref[l], g, m_ref[l], v_ref[l])

    blk = pl.BlockSpec((L, tm, C), lambda i: (0, i, 0))
    return pl.pallas_call(
        body, name=name, grid=(R // tm,),
        in_specs=[pl.BlockSpec((N_DEV, tm, C), lambda i: (0, i, 0))] * L + [blk, blk, blk],
        out_specs=[blk] * 4, out_shape=[jax.ShapeDtypeStruct((L, R, C), F32)] * 4,
        compiler_params=_params(dimension_semantics=("arbitrary",)),
    )(*recvs, w, m, v)


def _small_sync(part, w, m, v):
    def body(p_ref, w_ref, m_ref, v_ref, g_ref, d_ref, nm_ref, nv_ref, gath, send_sems, recv_sems):
        x, y, c = _place()
        me = 4 * x + 2 * y + c
        gath[me] = p_ref[...]
        copies = []
        for k, peer, _ in _peers(x, y, c):
            cp = pltpu.make_async_remote_copy(
                src_ref=p_ref, dst_ref=gath.at[me], send_sem=send_sems.at[k - 1], recv_sem=recv_sems.at[k - 1],
                device_id=peer, device_id_type=MESH)
            cp.start()
            copies.append(cp)
        for cp in copies:
            cp.wait()
        g = gath[0]
        for s in range(1, N_DEV):
            g = g + gath[s]
        wv = w_ref[...]
        l0, l1 = w_ref[8:9, :], w_ref[9:10, :]
        mx = jnp.maximum(l0, l1)
        e0, e1 = jnp.exp(l0 - mx), jnp.exp(l1 - mx)
        g9 = g[9:10, :] * (e0 / (e0 + e1)) * (e1 / (e0 + e1))
        row = lax.broadcasted_iota(jnp.int32, g.shape, 0)
        g = jnp.where(row == 9, g9, jnp.where(row == 8, -g9, g))
        g_ref[...] = g
        d_ref[...], nm_ref[...], nv_ref[...] = _adamw(wv, g, m_ref[...], v_ref[...])

    vm = pl.BlockSpec(memory_space=pltpu.VMEM)
    return pl.pallas_call(
        body, name="small_params_sync", in_specs=[vm] * 4, out_specs=[vm] * 4,
        out_shape=[jax.ShapeDtypeStruct(part.shape, F32)] * 4,
        scratch_shapes=[pltpu.VMEM((N_DEV,) + part.shape, F32), pltpu.SemaphoreType.DMA((7,)),
                        pltpu.SemaphoreType.DMA((7,))],
    )(part, w, m, v)


def _shards_bf16(d, pieces):
    return [d[name][layer].astype(BF16) for name, layer in pieces]


def _gathered(arrs, pieces, out):
    for a, (name, layer) in zip(arrs, pieces):
        out[name, layer] = a if name in COL_SHARDED else a.reshape(N_DEV * a.shape[1], a.shape[2])


def _pad_row(a, width=D_MODEL):
    a = a.reshape(1, -1)
    return jnp.pad(a, ((0, 0), (0, width - a.shape[1])))


def _pack_small(d, gn_full):
    rows = [d["mix_norm"], d["mlp_norm"], d["final_norm"].reshape(1, D_MODEL),
            _pad_row(d["attn_b_qkv"], 2 * D_MODEL).reshape(2, D_MODEL), _pad_row(d["attn_sinks"]),
            d["hgrn_lower_bounds"], gn_full.reshape(1, D_MODEL)]
    p = jnp.concatenate(rows, axis=0)
    return jnp.pad(p, ((0, SMALL_ROWS - p.shape[0]), (0, 0)))


def _unpack_small(p, me):
    return dict(
        mix_norm=p[0:2], mlp_norm=p[2:4], final_norm=p[4],
        attn_b_qkv=p[5:7].reshape(1, 2 * D_MODEL)[:, :QKV_DIM], attn_sinks=p[7:8, :N_Q_HEADS],
        hgrn_lower_bounds=p[8:10], hgrn_g_norm=lax.dynamic_slice(p[10:11], (0, me * 128), (1, 128)))


WEIGHT_NAMES = ['mix_norm', 'mlp_norm', 'final_norm', 'attn_w_qkv', 'attn_b_qkv', 'attn_sinks', 'attn_w_o', 'hgrn_w_in',
                'hgrn_g_norm', 'hgrn_w_o', 'hgrn_lower_bounds', 'mlp_w_up', 'mlp_w_down']
SMALL_NAMES = ('mix_norm', 'mlp_norm', 'final_norm', 'attn_b_qkv', 'attn_sinks', 'hgrn_lower_bounds', 'hgrn_g_norm')


def _rotary_tables(positions):
    inv_freq = ROPE_THETA ** (-jnp.arange(0, 2 * ROT_HALF, 2, dtype=F32) / (2 * ROT_HALF))
    ang = positions.astype(F32).reshape(-1, 1) * inv_freq
    cos, sin = jnp.cos(ang), jnp.sin(ang)
    r = jnp.arange(LANES) % HEAD_DIM
    idx = r % ROT_HALF
    c = jnp.where(r < 2 * ROT_HALF, cos[:, idx], 1.0)
    sa = jnp.where((r >= ROT_HALF) & (r < 2 * ROT_HALF), sin[:, idx], 0.0)
    sb = jnp.where(r < ROT_HALF, -sin[:, idx], 0.0)
    return jnp.concatenate([c, sa, sb], axis=1)


def kernel(x, positions, mix_norm, mlp_norm, final_norm, attn_w_qkv, attn_b_qkv, attn_sinks, attn_w_o, hgrn_w_in, hgrn_g_norm, hgrn_w_o, hgrn_lower_bounds, mlp_w_up, mlp_w_down, loss_target, m_mix_norm, m_mlp_norm, m_final_norm, m_attn_w_qkv, m_attn_b_qkv, m_attn_sinks, m_attn_w_o, m_hgrn_w_in, m_hgrn_g_norm, m_hgrn_w_o, m_hgrn_lower_bounds, m_mlp_w_up, m_mlp_w_down, v_mix_norm, v_mlp_norm, v_final_norm, v_attn_w_qkv, v_attn_b_qkv, v_attn_sinks, v_attn_w_o, v_hgrn_w_in, v_hgrn_g_norm, v_hgrn_w_o, v_hgrn_lower_bounds, v_mlp_w_up, v_mlp_w_down):
    w = dict(mix_norm=mix_norm, mlp_norm=mlp_norm, final_norm=final_norm, attn_w_qkv=attn_w_qkv, attn_b_qkv=attn_b_qkv,
             attn_sinks=attn_sinks, attn_w_o=attn_w_o, hgrn_w_in=hgrn_w_in, hgrn_g_norm=hgrn_g_norm, hgrn_w_o=hgrn_w_o,
             hgrn_lower_bounds=hgrn_lower_bounds, mlp_w_up=mlp_w_up, mlp_w_down=mlp_w_down)
    m = dict(mix_norm=m_mix_norm, mlp_norm=m_mlp_norm, final_norm=m_final_norm, attn_w_qkv=m_attn_w_qkv,
             attn_b_qkv=m_attn_b_qkv, attn_sinks=m_attn_sinks, attn_w_o=m_attn_w_o, hgrn_w_in=m_hgrn_w_in,
             hgrn_g_norm=m_hgrn_g_norm, hgrn_w_o=m_hgrn_w_o, hgrn_lower_bounds=m_hgrn_lower_bounds, mlp_w_up=m_mlp_w_up,
             mlp_w_down=m_mlp_w_down)
    v = dict(mix_norm=v_mix_norm, mlp_norm=v_mlp_norm, final_norm=v_final_norm, attn_w_qkv=v_attn_w_qkv,
             attn_b_qkv=v_attn_b_qkv, attn_sinks=v_attn_sinks, attn_w_o=v_attn_w_o, hgrn_w_in=v_hgrn_w_in,
             hgrn_g_norm=v_hgrn_g_norm, hgrn_w_o=v_hgrn_w_o, hgrn_lower_bounds=v_hgrn_lower_bounds, mlp_w_up=v_mlp_w_up,
             mlp_w_down=v_mlp_w_down)
    me = 4 * lax.axis_index("x") + 2 * lax.axis_index("y") + lax.axis_index("c")

    gn = hgrn_g_norm.reshape(1, 128)
    gn_a = gn.astype(BF16)
    gn_b = (gn - gn_a.astype(F32)).astype(BF16)
    gn_c = (gn - gn_a.astype(F32) - gn_b.astype(F32)).astype(BF16)
    gn_rows = jnp.pad(jnp.concatenate([gn_a, gn_b, gn_c], axis=1), ((0, 15), (0, D_MODEL - 3 * 128)))
    full = {}
    got = _all_gather("gather_attn_weights", _shards_bf16(w, GATHER_FIRST) + [gn_rows])
    _gathered(got[:2], GATHER_FIRST, full)
    w_qkv = full["attn_w_qkv", 0].transpose(1, 0, 2).reshape(D_MODEL, QKV_DIM)
    gn_terms = got[2][:, 0, :3 * 128].astype(F32).reshape(N_DEV, 3, 128)
    gn_full = ((gn_terms[:, 0] + gn_terms[:, 1]) + gn_terms[:, 2]).reshape(1, D_MODEL)

    x0 = x[0]
    tgt = loss_target[0]
    rot = _rotary_tables(positions)
    row = lambda a: a.reshape(1, -1)

    qkv, h0 = _norm_mm("qkv_proj", x0, row(mix_norm[0]), w_qkv, attn_b_qkv, rot=rot)
    att, *got = _attn_fwd(qkv, attn_sinks, carry=(_Gather, _shards_bf16(w, GATHER_ATTN)))
    _gathered(got, GATHER_ATTN, full)
    x1 = _mm_res("attn_out_proj", att, full["attn_w_o", 0], x0)
    u0, h1 = _norm_mm("mlp0_up", x1, row(mlp_norm[0]), full["mlp_w_up", 0])
    x2 = _mlp_down("mlp0_down", u0, full["mlp_w_down", 0], x1)
    z, h2 = _norm_mm("hgrn_in_proj", x2, row(mix_norm[1]), full["hgrn_w_in", 0])
    o_raw, states, *got = _hgrn_fwd(z, hgrn_lower_bounds, carry=(_Gather, _shards_bf16(w, GATHER_HGRN)))
    _gathered(got, GATHER_HGRN, full)
    x3, o2 = _hgrn_out("hgrn_out_proj", o_raw, z, gn_full, full["hgrn_w_o", 0], x2)
    u1, h3 = _norm_mm("mlp1_up", x3, row(mlp_norm[1]), full["mlp_w_up", 1])
    x4 = _mlp_down("mlp1_down", u1, full["mlp_w_down", 1], x3)
    dx4, loss_part, g_final = _loss_head("loss_head", x4, tgt, row(final_norm))

    gw = {}
    du1, a1 = _mlp_bwd_act("mlp1_bwd_act", dx4, u1, full["mlp_w_down", 1])
    dx3, g_mlp1 = _mm_nt_rmsbwd("mlp1_bwd_in", du1, full["mlp_w_up", 1], x3, row(mlp_norm[1]), dx4)
    gw["mlp_w_down", 1] = _mm_tn("mlp1_dw_down", a1, dx4, "rows")
    gw["mlp_w_up", 1] = _mm_tn("mlp1_dw_up", h3, du1, "cols")

    do_raw, dg, g_gn = _hgrn_out_bwd("hgrn_out_bwd", dx3, o_raw, z, full["hgrn_w_o", 0], gn_full)
    gw["hgrn_w_o", 0] = _mm_tn("hgrn_dw_o", o2, dx3, "rows")
    recvs = {}
    dzq, dzf, dzi, g_lb, *recv = _hgrn_bwd(z, hgrn_lower_bounds, states, do_raw,
                                           carry=(_Exchange, [gw[p] for p in GRAD_GROUPS[0]]))
    recvs.update(zip(GRAD_GROUPS[0], recv))
    dz = jnp.concatenate([dzq, dzf, dzi, dg], axis=1)
    dx2, g_mix1 = _mm_nt_rmsbwd("hgrn_in_bwd", dz, full["hgrn_w_in", 0], x2, row(mix_norm[1]), dx3)
    gw["hgrn_w_in", 0] = _mm_tn("hgrn_dw_in", h2, dz, "cols")

    du0, a0 = _mlp_bwd_act("mlp0_bwd_act", dx2, u0, full["mlp_w_down", 0])
    dx1, g_mlp0 = _mm_nt_rmsbwd("mlp0_bwd_in", du0, full["mlp_w_up", 0], x1, row(mlp_norm[0]), dx2)
    gw["mlp_w_down", 0] = _mm_tn("mlp0_dw_down", a0, dx2, "rows")
    gw["mlp_w_up", 0] = _mm_tn("mlp0_dw_up", h1, du0, "cols")

    datt = _mm_nt("attn_out_bwd", dx1, full["attn_w_o", 0], BF16)
    gw["attn_w_o", 0] = _mm_tn("attn_dw_o", att, dx1, "rows")
    dqkv, g_sink, *recv = _attn_bwd(qkv, rot, attn_sinks, datt, carry=(_Exchange, [gw[p] for p in GRAD_GROUPS[1]]))
    recvs.update(zip(GRAD_GROUPS[1], recv))
    g_qkv = _mm_tn("attn_dw_qkv", h0, dqkv)
    g_qkv = g_qkv.reshape(D_MODEL, N_DEV, QKV_DIM // N_DEV).transpose(1, 0, 2).astype(BF16)
    dx0, g_mix0, g_bqkv, recvs["attn_w_qkv", 0] = _mm_nt_rmsbwd(
        "qkv_bwd", dqkv, w_qkv, x0, row(mix_norm[0]), dx1, with_colsum=True, carry=(_Exchange, [g_qkv]))

    big = {name: _adamw_sum("adamw_" + name, [recvs[name, l] for l in range(w[name].shape[0])], w[name], m[name], v[name])
           for name in BIG_NAMES}

    zero_row = jnp.zeros((1, D_MODEL), F32)
    part = _pack_small(dict(
        mix_norm=jnp.concatenate([g_mix0, g_mix1], axis=0), mlp_norm=jnp.concatenate([g_mlp0, g_mlp1], axis=0),
        final_norm=g_final, attn_b_qkv=g_bqkv, attn_sinks=g_sink[:, :N_Q_HEADS],
        hgrn_lower_bounds=jnp.concatenate([zero_row, g_lb], axis=0)), g_gn)

    def spread(a):
        return lax.dynamic_update_slice(zero_row, a.reshape(1, 128), (0, me * 128))

    small_in = [_pack_small({n: d[n] for n in SMALL_NAMES if n != "hgrn_g_norm"}, spread(d["hgrn_g_norm"]))
                for d in (w, m, v)]
    small = [_unpack_small(p, me) for p in _small_sync(part, *small_in)]

    loss = lax.psum(loss_part[0, 0], ("x", "y", "c"))
    outs = [loss, dx0.reshape(x.shape)]
    for kind, grp_small in enumerate(small):
        for name in WEIGHT_NAMES:
            val = grp_small[name] if name in SMALL_NAMES else big[name][kind]
            outs.append(val.reshape(w[name].shape))
    return tuple(outs)
```

```python
import functools

import jax
import jax.numpy as jnp
from jax import lax
from jax.experimental import pallas as pl
from jax.experimental.pallas import tpu as pltpu

F32 = jnp.float32
BF16 = jnp.bfloat16

D_MODEL = 1024
HEAD_DIM = 64
N_Q_HEADS = 16
Q_DIM = 1024
KV_DIM = 256
QKV_DIM = 1536
ATT_BLOCK = 128
ROT_HALF = 8
ROPE_THETA = 500000.0
NEG_INF = -1e30
HGRN_HEADS = 8
HGRN_DK = 128
CHUNK = 64
D_FF = 4096
NORM_EPS = 1e-5
N_DEV = 8

ADAM_LR = 0.001
ADAM_B1 = 0.9
ADAM_B2 = 0.999
ADAM_EPS = 1e-08
ADAM_WD = 0.01
ADAM_STEP = 10

LANES = 128
VMEM_LIMIT = 56 * 1024 * 1024

GATHER_FIRST = (("attn_w_qkv", 0), ("attn_w_o", 0))
GATHER_ATTN = (("mlp_w_up", 0), ("mlp_w_down", 0), ("hgrn_w_in", 0), ("hgrn_w_o", 0))
GATHER_HGRN = (("mlp_w_up", 1), ("mlp_w_down", 1))
GRAD_GROUPS = ((("mlp_w_down", 1), ("mlp_w_up", 1), ("hgrn_w_o", 0)),
               (("hgrn_w_in", 0), ("mlp_w_down", 0), ("mlp_w_up", 0), ("attn_w_o", 0)),
               (("attn_w_qkv", 0),))
COL_SHARDED = ("attn_w_qkv", "hgrn_w_in", "mlp_w_up")
BIG_NAMES = ("attn_w_qkv", "attn_w_o", "hgrn_w_in", "hgrn_w_o", "mlp_w_up", "mlp_w_down")
SMALL_ROWS = 16


def _dot(a, b):
    return jnp.dot(a, b, preferred_element_type=F32)


def _dot_nt(a, b):
    return lax.dot_general(a, b, (((1,), (1,)), ((), ())), preferred_element_type=F32)


def _dot_tn(a, b):
    return lax.dot_general(a, b, (((0,), (0,)), ((), ())), preferred_element_type=F32)


def _params(**kw):
    return pltpu.CompilerParams(vmem_limit_bytes=VMEM_LIMIT, **kw)


def _full_spec(a):
    nd = a.ndim
    return pl.BlockSpec(a.shape, lambda *_: (0,) * nd)


def _row_call(name, body, n_rows, tm, row_ins, full_ins, row_outs, acc_outs=(), carry=(None, None)):
    steps = n_rows // tm
    in_specs = [pl.BlockSpec((tm, w), functools.partial(lambda i, cb: (i, cb), cb=cb)) for _, w, cb in row_ins]
    in_specs += [_full_spec(a) for a in full_ins]
    out_shape = [jax.ShapeDtypeStruct((n_rows, w), dt) for w, dt in row_outs]
    out_specs = [pl.BlockSpec((tm, w), lambda i: (i, 0)) for w, _ in row_outs]
    for shp, dt in acc_outs:
        out_shape.append(jax.ShapeDtypeStruct(shp, dt))
        out_specs.append(pl.BlockSpec(shp, functools.partial(lambda i, nd: (0,) * nd, nd=len(shp))))
    n_in, n_out = len(in_specs), len(out_specs)
    in_specs, out_specs, out_shape, scratch, extra = _carried_specs(carry, in_specs, out_specs, out_shape, [])

    def wrapped(*refs):
        i = pl.program_id(0)
        own, finish = _carried(carry, refs, n_in, n_out, i == 0, i == steps - 1)
        body(*own)
        finish()

    return pl.pallas_call(
        wrapped, name=name, grid=(steps,), in_specs=in_specs, out_specs=out_specs, out_shape=out_shape,
        scratch_shapes=scratch, compiler_params=_params(dimension_semantics=("arbitrary",)),
    )(*[a for a, _, _ in row_ins], *full_ins, *extra)


def _rms(x, gain):
    r = lax.rsqrt(jnp.mean(x * x, axis=-1, keepdims=True) + NORM_EPS)
    xhat = x * r
    return xhat * gain, xhat, r


def _rms_bwd(dy, xhat, r, gain):
    dxhat = dy * gain
    dx = r * (dxhat - xhat * jnp.mean(dxhat * xhat, axis=-1, keepdims=True))
    return dx, dy * xhat


def _norm_mm(name, x, gain, w, bias=None, rot=None, tm=256):
    T = x.shape[0]
    tm = min(tm, T)
    nc = 512
    blocked = w.ndim == 3
    n = N_DEV * w.shape[2] if blocked else w.shape[1]
    assert n % nc == 0 and (not blocked or w.shape[2] == nc)

    def body(*refs):
        x_ref, refs = refs[0], refs[1:]
        if rot is not None:
            t_ref, refs = refs[0], refs[1:]
        g_ref, w_ref, refs = refs[0], refs[1], refs[2:]
        if bias is not None:
            b_ref, refs = refs[0], refs[1:]
        y_ref, h_ref = refs
        h, _, _ = _rms(x_ref[...], g_ref[...])
        hb = h.astype(BF16)
        h_ref[...] = hb
        for c in range(n // nc):
            sl = slice(c * nc, (c + 1) * nc)
            y = _dot(hb, w_ref[c] if blocked else w_ref[:, sl])
            if bias is not None:
                y = y + b_ref[:, sl]
            if rot is None:
                y_ref[:, sl] = y
            else:
                n_rot = max(0, min(nc, Q_DIM + KV_DIM - c * nc)) // LANES
                pieces = _rot_fwd(y[:, :n_rot * LANES], t_ref[...]) if n_rot else []
                for j in range(nc // LANES):
                    col = slice(c * nc + j * LANES, c * nc + (j + 1) * LANES)
                    y_ref[:, col] = pieces[j] if j < n_rot else y[:, j * LANES:(j + 1) * LANES]

    rows = [(x, D_MODEL, 0)] + ([(rot, 3 * LANES, 0)] if rot is not None else [])
    full = [gain, w] + ([bias] if bias is not None else [])
    return _row_call(name, body, T, tm, rows, full, [(n, F32), (D_MODEL, BF16)])


def _mm_res(name, a, w, res, tm=512):
    T = a.shape[0]
    tm = min(tm, T)

    def body(a_ref, r_ref, w_ref, o_ref):
        o_ref[...] = r_ref[...] + _dot(a_ref[...], w_ref[...])

    return _row_call(name, body, T, tm, [(a, a.shape[1], 0), (res, D_MODEL, 0)], [w], [(D_MODEL, F32)])[0]


def _mlp_down(name, u, w, res, tm=256):
    T = u.shape[0]
    tm = min(tm, T)
    kc = 1024

    def body(u_ref, r_ref, w_ref, o_ref):
        acc = r_ref[...]
        for c in range(D_FF // kc):
            sl = slice(c * kc, (c + 1) * kc)
            a = jnp.maximum(u_ref[:, sl], 0.0)
            acc = acc + _dot((a * a).astype(BF16), w_ref[sl, :])
        o_ref[...] = acc

    return _row_call(name, body, T, tm, [(u, D_FF, 0), (res, D_MODEL, 0)], [w], [(D_MODEL, F32)])[0]


def _hgrn_out(name, o_raw, z, gn, w, res, tm=256):
    T = o_raw.shape[0]
    tm = min(tm, T)

    def body(o_ref, g_ref, r_ref, gn_ref, w_ref, x_ref, a_ref):
        y, _, _ = _rms(o_ref[...], gn_ref[...])
        g = g_ref[...]
        a = (y * (g * jax.nn.sigmoid(g))).astype(BF16)
        a_ref[...] = a
        x_ref[...] = r_ref[...] + _dot(a, w_ref[...])

    return _row_call(name, body, T, tm, [(o_raw, D_MODEL, 0), (z, D_MODEL, 3), (res, D_MODEL, 0)], [gn, w],
                     [(D_MODEL, F32), (D_MODEL, BF16)])


def _loss_head(name, x, target, gain, tm=512):
    T = x.shape[0]
    tm = min(tm, T)

    def body(x_ref, t_ref, g_ref, dx_ref, loss_ref, dg_ref):
        @pl.when(pl.program_id(0) == 0)
        def _():
            loss_ref[...] = jnp.zeros_like(loss_ref)
            dg_ref[...] = jnp.zeros_like(dg_ref)

        gain_v = g_ref[...]
        y, xhat, r = _rms(x_ref[...], gain_v)
        diff = y - t_ref[...]
        row = jnp.sum(diff * diff, axis=-1, keepdims=True) * (1.0 / D_MODEL)
        loss_ref[...] += jnp.broadcast_to(0.5 * jnp.sum(row, axis=0, keepdims=True), loss_ref.shape)
        dy = diff * (1.0 / D_MODEL)
        dx, dgr = _rms_bwd(dy, xhat, r, gain_v)
        dx_ref[...] = dx
        dg_ref[...] += jnp.sum(dgr, axis=0, keepdims=True)

    return _row_call(name, body, T, tm, [(x, D_MODEL, 0), (target, D_MODEL, 0)], [gain], [(D_MODEL, F32)],
                     [((1, LANES), F32), ((1, D_MODEL), F32)])


def _mm_nt_rmsbwd(name, dy, w, x, gain, dres, tm=256, with_colsum=False, carry=(None, None)):
    T = x.shape[0]
    tm = min(tm, T)
    dys = list(dy) if isinstance(dy, (list, tuple)) else [dy]
    width = dys[0].shape[1]
    n = width * len(dys)
    assert not with_colsum or len(dys) == 1

    def body(*refs):
        dy_refs, refs = refs[:len(dys)], refs[len(dys):]
        if with_colsum:
            x_ref, dr_ref, w_ref, g_ref, dx_ref, dg_ref, cs_ref = refs
        else:
            x_ref, dr_ref, w_ref, g_ref, dx_ref, dg_ref = refs

        @pl.when(pl.program_id(0) == 0)
        def _():
            dg_ref[...] = jnp.zeros_like(dg_ref)
            if with_colsum:
                cs_ref[...] = jnp.zeros_like(cs_ref)

        if w.ndim == 3:
            nb = w.shape[2]
            dh = None
            for p in range(N_DEV):
                piece, off = divmod(p * nb, width)
                part = _dot_nt(dy_refs[piece][:, off:off + nb].astype(BF16), w_ref[p])
                dh = part if dh is None else dh + part
        else:
            dh = _dot_nt(dy_refs[0][...].astype(BF16), w_ref[...])
        gain_v = g_ref[...]
        _, xhat, r = _rms(x_ref[...], gain_v)
        dx, dgr = _rms_bwd(dh, xhat, r, gain_v)
        dx_ref[...] = dr_ref[...] + dx
        dg_ref[...] += jnp.sum(dgr, axis=0, keepdims=True)
        if with_colsum:
            cs_ref[...] += jnp.sum(dy_refs[0][...].astype(F32), axis=0, keepdims=True)

    acc = [((1, D_MODEL), F32)] + ([((1, n), F32)] if with_colsum else [])
    rows = [(d, width, 0) for d in dys] + [(x, D_MODEL, 0), (dres, D_MODEL, 0)]
    return _row_call(name, body, T, tm, rows, [w, gain], [(D_MODEL, F32)], acc, carry=carry)


def _mm_nt(name, dy, w, out_dtype, tm=512):
    T = dy.shape[0]
    tm = min(tm, T)
    k = w.shape[0]

    def body(dy_ref, w_ref, o_ref):
        o_ref[...] = _dot_nt(dy_ref[...].astype(BF16), w_ref[...]).astype(out_dtype)

    return _row_call(name, body, T, tm, [(dy, dy.shape[1], 0)], [w], [(k, out_dtype)])[0]


def _mlp_bwd_act(name, dy, u, w_down, tm=256):
    T = u.shape[0]
    tm = min(tm, T)
    kc = 1024

    def body(dy_ref, u_ref, w_ref, du_ref, a_ref):
        dyb = dy_ref[...].astype(BF16)
        for c in range(D_FF // kc):
            sl = slice(c * kc, (c + 1) * kc)
            a = jnp.maximum(u_ref[:, sl], 0.0)
            da = _dot_nt(dyb, w_ref[sl, :])
            du_ref[:, sl] = (da * (2.0 * a)).astype(BF16)
            a_ref[:, sl] = (a * a).astype(BF16)

    return _row_call(name, body, T, tm, [(dy, D_MODEL, 0), (u, D_FF, 0)], [w_down], [(D_FF, BF16), (D_FF, BF16)])


def _hgrn_out_bwd(name, dx, o_raw, z, w, gn, tm=256):
    T = dx.shape[0]
    tm = min(tm, T)

    def body(dx_ref, o_ref, g_ref, w_ref, gn_ref, do_ref, dg_ref, dgn_ref):
        @pl.when(pl.program_id(0) == 0)
        def _():
            dgn_ref[...] = jnp.zeros_like(dgn_ref)

        da = _dot_nt(dx_ref[...].astype(BF16), w_ref[...])
        gn_v = gn_ref[...]
        y, xhat, r = _rms(o_ref[...], gn_v)
        g = g_ref[...]
        sg = jax.nn.sigmoid(g)
        dg_ref[...] = (da * y * (sg * (1.0 + g * (1.0 - sg)))).astype(BF16)
        dyn = da * (g * sg)
        do, dgr = _rms_bwd(dyn, xhat, r, gn_v)
        do_ref[...] = do
        dgn_ref[...] += jnp.sum(dgr, axis=0, keepdims=True)

    return _row_call(name, body, T, tm, [(dx, D_MODEL, 0), (o_raw, D_MODEL, 0), (z, D_MODEL, 3)], [w, gn],
                     [(D_MODEL, F32), (D_MODEL, BF16)], [((1, D_MODEL), F32)])


def _mm_tn(name, a, b, shard=None, bm=1024, bn=512, tk=2048):
    T, M = a.shape
    N = b.shape[1]
    bm, bn, tk = min(bm, M), min(bn, N), min(tk, T)
    nk = T // tk
    if shard is None:
        out_shape, out_block = jax.ShapeDtypeStruct((M, N), F32), (bm, bn)
        out_map = lambda i, j, k: (i, j)
    elif shard == "cols":
        assert N % bn == 0
        out_shape, out_block = jax.ShapeDtypeStruct((N // bn, M, bn), BF16), (1, bm, bn)
        out_map = lambda i, j, k: (j, i, 0)
    else:
        rows = M // N_DEV
        assert bm % rows == 0
        out_shape, out_block = jax.ShapeDtypeStruct((N_DEV, rows, N), BF16), (bm // rows, rows, bn)
        out_map = lambda i, j, k: (i, 0, j)

    def body(a_ref, b_ref, o_ref, acc):
        k = pl.program_id(2)

        @pl.when(k == 0)
        def _():
            acc[...] = jnp.zeros_like(acc)

        acc[...] += _dot_tn(a_ref[...].astype(BF16), b_ref[...].astype(BF16))

        @pl.when(k == nk - 1)
        def _():
            o_ref[...] = acc[...].reshape(out_block).astype(o_ref.dtype)

    return pl.pallas_call(
        body, name=name, grid=(M // bm, N // bn, nk),
        in_specs=[pl.BlockSpec((tk, bm), lambda i, j, k: (k, i)), pl.BlockSpec((tk, bn), lambda i, j, k: (k, j))],
        out_specs=pl.BlockSpec(out_block, out_map), out_shape=out_shape,
        scratch_shapes=[pltpu.VMEM((bm, bn), F32)],
        compiler_params=_params(dimension_semantics=("parallel", "parallel", "arbitrary")),
    )(a, b)


def _rot_fwd(x, tab):
    c, sa, sb = tab[:, :LANES], tab[:, LANES:2 * LANES], tab[:, 2 * LANES:]
    outs = []
    for j in range(x.shape[1] // LANES):
        xs = x[:, j * LANES:(j + 1) * LANES]
        outs.append(xs * c + pltpu.roll(xs, ROT_HALF, 1) * sa + pltpu.roll(xs, LANES - ROT_HALF, 1) * sb)
    return outs


def _rot_bwd(dys, tab):
    c, sa, sb = tab[:, :LANES], tab[:, LANES:2 * LANES], tab[:, 2 * LANES:]
    return [dy * c + pltpu.roll(dy * sa, LANES - ROT_HALF, 1) + pltpu.roll(dy * sb, ROT_HALF, 1) for dy in dys]


ATT_SCALE = HEAD_DIM ** -0.5
ATT_ROWS = 128


def _attn_masks(n):
    qi = lax.broadcasted_iota(jnp.int32, (ATT_BLOCK, 2 * ATT_BLOCK), 0)
    kj = lax.broadcasted_iota(jnp.int32, (ATT_BLOCK, 2 * ATT_BLOCK), 1)
    delta = qi + ATT_BLOCK - kj
    first_key = jnp.where(n > 0, 0, ATT_BLOCK)
    valid = (delta >= 0) & (delta < ATT_BLOCK) & (kj >= first_key)
    lane = lax.broadcasted_iota(jnp.int32, (1, LANES), 1)
    return valid, lane < HEAD_DIM


def _attn_probs(qm, k_use, valid, sink):
    s = jnp.where(valid, _dot_nt(qm, k_use), NEG_INF)
    m = jnp.maximum(jnp.max(s, axis=-1, keepdims=True), sink)
    e = jnp.exp(s - m)
    es = jnp.exp(sink - m)
    inv = 1.0 / (jnp.sum(e, axis=-1, keepdims=True) + es)
    return e * inv, es * inv


def _attn_specs(nb, tables):
    prev = lambda n: jnp.maximum(jnp.minimum(n, nb - 1) - 1, 0)
    cur = lambda n: jnp.minimum(n, nb - 1)
    specs = [
        pl.BlockSpec((ATT_BLOCK, Q_DIM), lambda n: (cur(n), 0)),
        pl.BlockSpec((ATT_BLOCK, KV_DIM), lambda n: (prev(n), 4)),
        pl.BlockSpec((ATT_BLOCK, KV_DIM), lambda n: (cur(n), 4)),
        pl.BlockSpec((ATT_BLOCK, KV_DIM), lambda n: (prev(n), 5)),
        pl.BlockSpec((ATT_BLOCK, KV_DIM), lambda n: (cur(n), 5)),
    ]
    if tables:
        specs += [pl.BlockSpec((ATT_BLOCK, 3 * LANES), lambda n: (prev(n), 0)),
                  pl.BlockSpec((ATT_BLOCK, 3 * LANES), lambda n: (cur(n), 0))]
    return specs + [pl.BlockSpec(memory_space=pltpu.SMEM)]


def _kv_band(kp_ref, kc_ref, vp_ref, vc_ref):
    ks, vs = [], []
    for j in range(KV_DIM // LANES):
        sl = slice(j * LANES, (j + 1) * LANES)
        kb = jnp.concatenate([kp_ref[:, sl], kc_ref[:, sl]], axis=0)
        vb = jnp.concatenate([vp_ref[:, sl], vc_ref[:, sl]], axis=0)
        ks.append((kb.astype(BF16), pltpu.roll(kb, HEAD_DIM, 1).astype(BF16)))
        vs.append((vb.astype(BF16), pltpu.roll(vb, HEAD_DIM, 1).astype(BF16)))
    return ks, vs


def _attn_fwd(qkv, sinks, carry=(None, None)):
    T = qkv.shape[0]
    nb = T // ATT_BLOCK

    def body(*refs):
        n = pl.program_id(0)
        own, finish = _carried(carry, refs, 6, 1, n == 0, n == nb - 1)
        q_ref, kp_ref, kc_ref, vp_ref, vc_ref, sink_ref, o_ref = own
        valid, low = _attn_masks(n)
        ks, vs = _kv_band(kp_ref, kc_ref, vp_ref, vc_ref)
        for p in range(Q_DIM // LANES):
            kpair, khalf = p // 4, (p // 2) % 2
            q_pair = q_ref[:, p * LANES:(p + 1) * LANES] * ATT_SCALE
            for r0 in range(0, ATT_BLOCK, ATT_ROWS):
                rows = slice(r0, r0 + ATT_ROWS)
                outs = []
                for hf in range(2):
                    qm = jnp.where(low if hf == 0 else ~low, q_pair[rows], 0.0).astype(BF16)
                    sw = 0 if khalf == hf else 1
                    pr, _ = _attn_probs(qm, ks[kpair][sw], valid[rows], sink_ref[0, 2 * p + hf])
                    outs.append(_dot(pr.astype(BF16), vs[kpair][sw]))
                o_ref[rows, p * LANES:(p + 1) * LANES] = jnp.where(low, outs[0], outs[1]).astype(BF16)
        finish()

    in_specs, out_specs, out_shape, scratch, extra = _carried_specs(
        carry, _attn_specs(nb, False), [pl.BlockSpec((ATT_BLOCK, Q_DIM), lambda n: (n, 0))],
        [jax.ShapeDtypeStruct((T, Q_DIM), BF16)], [])
    return pl.pallas_call(
        body, name="attn_fwd", grid=(nb,), in_specs=in_specs, out_specs=out_specs, out_shape=out_shape,
        scratch_shapes=scratch, compiler_params=_params(dimension_semantics=("arbitrary",)),
    )(qkv, qkv, qkv, qkv, qkv, sinks, *extra)


def _attn_bwd(qkv, rot, sinks, dout, carry=(None, None)):
    T = qkv.shape[0]
    nb = T // ATT_BLOCK
    npair = KV_DIM // LANES

    def body(*refs):
        n = pl.program_id(0)
        own, finish = _carried(carry, refs, 9, 2, n == 0, n == nb)
        (q_ref, kp_ref, kc_ref, vp_ref, vc_ref, tp_ref, tc_ref, sink_ref, do_ref, dqkv_ref, dsink_ref,
         dq_c, dk_c, dv_c) = own

        @pl.when(n == 0)
        def _():
            dq_c[...] = jnp.zeros_like(dq_c)
            dk_c[...] = jnp.zeros_like(dk_c)
            dv_c[...] = jnp.zeros_like(dv_c)
            dsink_ref[...] = jnp.zeros_like(dsink_ref)

        def flush(dk_prev, dv_prev, tab_ref):
            dqkv_ref[:, :Q_DIM] = dq_c[...]
            dk = _rot_bwd([dk_c[:, j * LANES:(j + 1) * LANES] + dk_prev[j] for j in range(npair)], tab_ref[...])
            for j in range(npair):
                dqkv_ref[:, Q_DIM + j * LANES:Q_DIM + (j + 1) * LANES] = dk[j]
                dqkv_ref[:, Q_DIM + KV_DIM + j * LANES:Q_DIM + KV_DIM + (j + 1) * LANES] = (
                    dv_c[:, j * LANES:(j + 1) * LANES] + dv_prev[j])

        @pl.when(n < nb)
        def _():
            valid, low = _attn_masks(n)
            lane = lax.broadcasted_iota(jnp.int32, (1, LANES), 1)
            ks, vs = _kv_band(kp_ref, kc_ref, vp_ref, vc_ref)
            dk_acc = [[jnp.zeros((2 * ATT_BLOCK, LANES), F32) for _ in range(2)] for _ in range(npair)]
            dv_acc = [[jnp.zeros((2 * ATT_BLOCK, LANES), F32) for _ in range(2)] for _ in range(npair)]
            dsink = jnp.zeros((1, LANES), F32)
            dqs = []
            for p in range(Q_DIM // LANES):
                kpair, khalf = p // 4, (p // 2) % 2
                q_pair = q_ref[:, p * LANES:(p + 1) * LANES] * ATT_SCALE
                do_pair = do_ref[:, p * LANES:(p + 1) * LANES]
                dq_rows = []
                for r0 in range(0, ATT_BLOCK, ATT_ROWS):
                    rows = slice(r0, r0 + ATT_ROWS)
                    dq_h = []
                    for hf in range(2):
                        sel = low if hf == 0 else ~low
                        qm = jnp.where(sel, q_pair[rows], 0.0).astype(BF16)
                        dom = jnp.where(sel, do_pair[rows], 0.0).astype(BF16)
                        sw = 0 if khalf == hf else 1
                        k_use, v_use = ks[kpair][sw], vs[kpair][sw]
                        pr, ps = _attn_probs(qm, k_use, valid[rows], sink_ref[0, 2 * p + hf])
                        dp = _dot_nt(dom, v_use)
                        dd = jnp.sum(pr * dp, axis=-1, keepdims=True)
                        ds = (pr * (dp - dd)).astype(BF16)
                        dq_h.append(_dot(ds, k_use))
                        dk_acc[kpair][sw] = dk_acc[kpair][sw] + _dot_tn(ds, qm)
                        dv_acc[kpair][sw] = dv_acc[kpair][sw] + _dot_tn(pr.astype(BF16), dom)
                        dsink = dsink + jnp.where(lane == 2 * p + hf, -jnp.sum(ps * dd, axis=0, keepdims=True), 0.0)
                    dq_rows.append(jnp.where(low, dq_h[0], dq_h[1]) * ATT_SCALE)
                dqs.append(jnp.concatenate(dq_rows, axis=0))
            dk_acc = [a[0] + pltpu.roll(a[1], HEAD_DIM, 1) for a in dk_acc]
            dv_acc = [a[0] + pltpu.roll(a[1], HEAD_DIM, 1) for a in dv_acc]
            flush([a[:ATT_BLOCK] for a in dk_acc], [a[:ATT_BLOCK] for a in dv_acc], tp_ref)
            dq = _rot_bwd(dqs, tc_ref[...])
            for p in range(Q_DIM // LANES):
                dq_c[:, p * LANES:(p + 1) * LANES] = dq[p]
            for j in range(npair):
                dk_c[:, j * LANES:(j + 1) * LANES] = dk_acc[j][ATT_BLOCK:]
                dv_c[:, j * LANES:(j + 1) * LANES] = dv_acc[j][ATT_BLOCK:]
            dsink_ref[...] += dsink

        @pl.when(n == nb)
        def _():
            zero = [jnp.zeros((ATT_BLOCK, LANES), F32) for _ in range(npair)]
            flush(zero, zero, tc_ref)

        finish()

    do_spec = pl.BlockSpec((ATT_BLOCK, Q_DIM), lambda n: (jnp.minimum(n, nb - 1), 0))
    in_specs, out_specs, out_shape, scratch, extra = _carried_specs(
        carry, _attn_specs(nb, True) + [do_spec],
        [pl.BlockSpec((ATT_BLOCK, QKV_DIM), lambda n: (jnp.maximum(n - 1, 0), 0)),
         pl.BlockSpec((1, LANES), lambda n: (0, 0))],
        [jax.ShapeDtypeStruct((T, QKV_DIM), F32), jax.ShapeDtypeStruct((1, LANES), F32)],
        [pltpu.VMEM((ATT_BLOCK, Q_DIM), F32), pltpu.VMEM((ATT_BLOCK, KV_DIM), F32),
         pltpu.VMEM((ATT_BLOCK, KV_DIM), F32)])
    return pl.pallas_call(
        body, name="attn_bwd", grid=(nb + 1,), in_specs=in_specs, out_specs=out_specs, out_shape=out_shape,
        scratch_shapes=scratch, compiler_params=_params(dimension_semantics=("arbitrary",)),
    )(qkv, qkv, qkv, qkv, qkv, rot, rot, sinks, dout, *extra)


LEVELS = (32, 16, 8)
DIAG = 8
UNROLL = 4
UNROLL_BWD = 2


def _lower_bound(lb_ref):
    l0, l1 = lb_ref[0:1, :], lb_ref[1:2, :]
    mx = jnp.maximum(l0, l1)
    e0, e1 = jnp.exp(l0 - mx), jnp.exp(l1 - mx)
    return e1 / (e0 + e1)


GROUPS = CHUNK // DIAG


def _group_roll(x, k):
    return pltpu.roll(x.reshape(GROUPS, DIAG, HGRN_DK), k % DIAG, 1).reshape(CHUNK, HGRN_DK)


def _scan_rows(x, row, reverse):
    r8 = row & (DIAG - 1)
    for sh in (1, 2, 4):
        ok = (r8 < DIAG - sh) if reverse else (r8 >= sh)
        x = x + jnp.where(ok, _group_roll(x, -sh if reverse else sh), 0.0)
    g = x.reshape(GROUPS, DIAG, HGRN_DK)
    edge = 0 if reverse else DIAG - 1
    tot = jnp.broadcast_to(g[:, edge:edge + 1, :], g.shape)

    def shifted(a, n):
        z = jnp.zeros((n, DIAG, HGRN_DK), F32)
        return jnp.concatenate([a[n:], z] if reverse else [z, a[:GROUPS - n]], axis=0)

    acc = shifted(tot, 1)
    for sh in (1, 2, 4):
        acc = acc + shifted(acc, sh)
    return (g + acc).reshape(CHUNK, HGRN_DK)


def _level_masks():
    t = lax.broadcasted_iota(jnp.int32, (CHUNK, CHUNK), 0)
    s = lax.broadcasted_iota(jnp.int32, (CHUNK, CHUNK), 1)
    return [((t & h) != 0) & ((s & h) == 0) & ((t ^ s) < 2 * h) for h in LEVELS]


def _level_scale(b, h):
    parts = [jnp.broadcast_to(b[j * 2 * h + h - 1:j * 2 * h + h, :], (2 * h, HGRN_DK)) for j in range(CHUNK // (2 * h))]
    mid = parts[0] if len(parts) == 1 else jnp.concatenate(parts, axis=0)
    return jnp.exp(-jnp.abs(b - mid))


def _hgrn_gates(zq, zf, lb):
    sq = jax.nn.sigmoid(zq)
    q = zq * sq
    sg = jax.nn.sigmoid(zf)
    forget = lb + (1.0 - lb) * sg
    return q, sq, sg, forget, 1.0 - forget, jnp.log(forget)


def _hgrn_specs(T, rb, rev):
    nr = T // rb
    ri = (lambda r: nr - 1 - r) if rev else (lambda r: r)
    return nr, ri, [
        pl.BlockSpec((rb, HGRN_DK), lambda h, r: (ri(r), h)),
        pl.BlockSpec((rb, HGRN_DK), lambda h, r: (ri(r), HGRN_HEADS + h)),
        pl.BlockSpec((rb, HGRN_DK), lambda h, r: (ri(r), 2 * HGRN_HEADS + h)),
        pl.BlockSpec((2, HGRN_DK), lambda h, r: (0, h)),
    ]


def _hgrn_fwd(z, lb_raw, rb=1024, carry=(None, None)):
    T = z.shape[0]
    rb = min(rb, T)
    ncb = rb // CHUNK
    nr, ri, in_specs = _hgrn_specs(T, rb, False)

    def body(*refs):
        hh, rr = pl.program_id(0), pl.program_id(1)
        own, finish = _carried(carry, refs, 4, 2, (hh == 0) & (rr == 0), (hh == HGRN_HEADS - 1) & (rr == nr - 1))
        zq_ref, zf_ref, zi_ref, lb_ref, o_ref, st_ref, state = own

        @pl.when(rr == 0)
        def _():
            state[...] = jnp.zeros_like(state)

        lb = _lower_bound(lb_ref)
        row = lax.broadcasted_iota(jnp.int32, (CHUNK, HGRN_DK), 0)
        masks = _level_masks()
        r8 = row & (DIAG - 1)

        def chunk(c, st):
            rows = pl.ds(pl.multiple_of(c * CHUNK, CHUNK), CHUNK)
            q, _, _, _, k, lf = _hgrn_gates(zq_ref[rows, :], zf_ref[rows, :], lb)
            v = zi_ref[rows, :]
            vb = v.astype(BF16)
            b = _scan_rows(lf, row, False)
            sc = jnp.zeros((CHUNK, CHUNK), F32)
            for h, mask in zip(LEVELS, masks):
                e = _level_scale(b, h)
                sc = sc + jnp.where(mask, _dot_nt((q * e).astype(BF16), (k * e).astype(BF16)), 0.0)
            o = _dot(sc.astype(BF16), vb) + jnp.sum(q * k, axis=-1, keepdims=True) * v
            for d in range(1, DIAG):
                w = jnp.where(r8 >= d, q * _group_roll(k, d) * jnp.exp(b - _group_roll(b, d)), 0.0)
                o = o + jnp.sum(w, axis=-1, keepdims=True) * _group_roll(v, d)
            b_last = b[CHUNK - 1:CHUNK, :]
            kd = (k * jnp.exp(b_last - b)).astype(BF16)
            qd = (q * jnp.exp(b)).astype(BF16)
            st_ref[c, 0] = st
            o_ref[rows, :] = o + _dot_nt(qd, st.astype(BF16))
            return st * jnp.exp(b_last) + _dot_tn(vb, kd)

        def group(i, st):
            for j in range(UNROLL):
                st = chunk(i * UNROLL + j, st)
            return st

        state[...] = lax.fori_loop(0, ncb // UNROLL, group, state[...])
        finish()

    in_specs, out_specs, out_shape, scratch, extra = _carried_specs(
        carry, in_specs,
        [pl.BlockSpec((rb, HGRN_DK), lambda h, r: (r, h)),
         pl.BlockSpec((ncb, 1, HGRN_DK, HGRN_DK), lambda h, r: (r, h, 0, 0))],
        [jax.ShapeDtypeStruct((T, D_MODEL), F32),
         jax.ShapeDtypeStruct((T // CHUNK, HGRN_HEADS, HGRN_DK, HGRN_DK), F32)],
        [pltpu.VMEM((HGRN_DK, HGRN_DK), F32)])
    return pl.pallas_call(
        body, name="hgrn_fwd", grid=(HGRN_HEADS, nr), in_specs=in_specs, out_specs=out_specs, out_shape=out_shape,
        scratch_shapes=scratch, compiler_params=_params(dimension_semantics=("arbitrary", "arbitrary")),
    )(z, z, z, lb_raw, *extra)


def _hgrn_bwd(z, lb_raw, states, do, rb=1024, carry=(None, None)):
    T = z.shape[0]
    rb = min(rb, T)
    ncb = rb // CHUNK
    nr, ri, in_specs = _hgrn_specs(T, rb, True)
    in_specs += [pl.BlockSpec((ncb, 1, HGRN_DK, HGRN_DK), lambda h, r: (ri(r), h, 0, 0)),
                 pl.BlockSpec((rb, HGRN_DK), lambda h, r: (ri(r), h))]

    def body(*refs):
        hh, rr = pl.program_id(0), pl.program_id(1)
        own, finish = _carried(carry, refs, 6, 4, (hh == 0) & (rr == 0), (hh == HGRN_HEADS - 1) & (rr == nr - 1))
        zq_ref, zf_ref, zi_ref, lb_ref, st_ref, do_ref, dq_ref, df_ref, di_ref, dlb_ref, dstate = own

        @pl.when(rr == 0)
        def _():
            dstate[...] = jnp.zeros_like(dstate)
            dlb_ref[...] = jnp.zeros_like(dlb_ref)

        lb = _lower_bound(lb_ref)
        row = lax.broadcasted_iota(jnp.int32, (CHUNK, HGRN_DK), 0)
        masks = _level_masks()
        r8 = row & (DIAG - 1)

        def chunk(ci, dlb):
            c = ncb - 1 - ci
            rows = pl.ds(pl.multiple_of(c * CHUNK, CHUNK), CHUNK)
            zq = zq_ref[rows, :]
            q, sq, sg, forget, k, lf = _hgrn_gates(zq, zf_ref[rows, :], lb)
            v = zi_ref[rows, :]
            dov = do_ref[rows, :]
            b = _scan_rows(lf, row, False)
            st = st_ref[c, 0]
            dst = dstate[...]
            b_last = b[CHUNK - 1:CHUNK, :]
            eb = jnp.exp(b)
            ebb = jnp.exp(b_last - b)
            e_last = jnp.exp(b_last)
            dob, vb, stb, dstb = dov.astype(BF16), v.astype(BF16), st.astype(BF16), dst.astype(BF16)
            dq = eb * _dot(dob, stb)
            dv = _dot_nt((k * ebb).astype(BF16), dstb)
            dk = ebb * _dot(vb, dstb)
            extra = e_last * jnp.sum(dst * st, axis=0, keepdims=True) + jnp.sum(k * dk, axis=0, keepdims=True)
            da = _dot_nt(dob, vb)
            sc = jnp.zeros((CHUNK, CHUNK), F32)
            for h, mask in zip(LEVELS, masks):
                e = _level_scale(b, h)
                qs, ks = (q * e).astype(BF16), (k * e).astype(BF16)
                dam = jnp.where(mask, da, 0.0).astype(BF16)
                dq = dq + e * _dot(dam, ks)
                dk = dk + e * _dot_tn(dam, qs)
                sc = sc + jnp.where(mask, _dot_nt(qs, ks), 0.0)
            dv = dv + _dot_tn(sc.astype(BF16), dob)
            dad = jnp.sum(dov * v, axis=-1, keepdims=True)
            dq = dq + dad * k
            dk = dk + dad * q
            dv = dv + jnp.sum(q * k, axis=-1, keepdims=True) * dov
            for d in range(1, DIAG):
                w = jnp.where(r8 >= d, jnp.exp(b - _group_roll(b, d)), 0.0)
                kr = _group_roll(k, d)
                dad = jnp.sum(dov * _group_roll(v, d), axis=-1, keepdims=True)
                ad = jnp.sum(q * kr * w, axis=-1, keepdims=True)
                dq = dq + dad * kr * w
                dk = dk + _group_roll(dad * q * w, -d)
                dv = dv + _group_roll(ad * dov, -d)
            dlf = _scan_rows(q * dq - k * dk, row, True) + extra
            dstate[...] = dst * e_last + _dot_tn(dob, (q * eb).astype(BF16))
            dforget = dlf / forget - dk
            dq_ref[rows, :] = (dq * (sq * (1.0 + zq * (1.0 - sq)))).astype(BF16)
            df_ref[rows, :] = (dforget * (1.0 - lb) * sg * (1.0 - sg)).astype(BF16)
            di_ref[rows, :] = dv.astype(BF16)
            return dlb + jnp.sum(dforget * (1.0 - sg), axis=0, keepdims=True)

        def group(i, dlb):
            for j in range(UNROLL_BWD):
                dlb = chunk(i * UNROLL_BWD + j, dlb)
            return dlb

        dlb_ref[...] += lax.fori_loop(0, ncb // UNROLL_BWD, group, jnp.zeros((1, HGRN_DK), F32))
        finish()

    blk = pl.BlockSpec((rb, HGRN_DK), lambda h, r: (ri(r), h))
    in_specs, out_specs, out_shape, scratch, extra = _carried_specs(
        carry, in_specs, [blk, blk, blk, pl.BlockSpec((1, HGRN_DK), lambda h, r: (0, h))],
        [jax.ShapeDtypeStruct((T, D_MODEL), BF16)] * 3 + [jax.ShapeDtypeStruct((1, D_MODEL), F32)],
        [pltpu.VMEM((HGRN_DK, HGRN_DK), F32)])
    return pl.pallas_call(
        body, name="hgrn_bwd", grid=(HGRN_HEADS, nr), in_specs=in_specs, out_specs=out_specs, out_shape=out_shape,
        scratch_shapes=scratch, compiler_params=_params(dimension_semantics=("arbitrary", "arbitrary")),
    )(z, z, z, lb_raw, states, do, *extra)


MESH = pl.DeviceIdType.MESH
ANY = pl.BlockSpec(memory_space=pl.ANY)


def _place():
    return lax.axis_index("x"), lax.axis_index("y"), lax.axis_index("c")


def _sems(n):
    return [pltpu.SemaphoreType.DMA((7 * n,)), pltpu.SemaphoreType.DMA((7 * n,)), pltpu.SemaphoreType.DMA((n,))]


class _Gather:
    def __init__(self, x_ref, out_ref, send_sems, recv_sems, local_sems, idx):
        self.x_ref, self.out_ref, self.send_sems, self.recv_sems, self.local_sem, self.base = (
            x_ref, out_ref, send_sems, recv_sems, local_sems.at[idx], 7 * idx)
        x, y, c = _place()
        self.c = c
        self.me, self.sibling = (x, y, c), (x, y, 1 - c)
        self.chips = [(1 - x, y), (x, 1 - y), (1 - x, 1 - y)]

    def rows(self, px, py, pc):
        return self.out_ref.at[4 * px + 2 * py + pc]

    def copy(self, k, block, to, from_input=False):
        return pltpu.make_async_remote_copy(
            src_ref=self.x_ref if from_input else self.rows(*block), dst_ref=self.rows(*block),
            send_sem=self.send_sems.at[self.base + k], recv_sem=self.recv_sems.at[self.base + k], device_id=to,
            device_id_type=MESH)

    def first(self):
        out = [self.copy(0, self.me, self.sibling, from_input=True)]
        return out + [self.copy(1 + j, self.me, (*chip, self.c), from_input=True) for j, chip in enumerate(self.chips)]

    def start(self):
        pltpu.make_async_copy(self.x_ref, self.rows(*self.me), self.local_sem).start()
        for cp in self.first():
            cp.start()

    def finish(self):
        passed = [self.copy(4 + j, (*chip, self.c), self.sibling) for j, chip in enumerate(self.chips)]
        for j, chip in enumerate(self.chips):
            self.copy(1 + j, (*chip, self.c), self.me).wait_recv()
            passed[j].start()
        self.copy(0, self.sibling, self.me).wait_recv()
        for j, chip in enumerate(self.chips):
            self.copy(4 + j, (*chip, 1 - self.c), self.me).wait_recv()
        for cp in self.first() + passed:
            cp.wait_send()
        pltpu.make_async_copy(self.x_ref, self.rows(*self.me), self.local_sem).wait()


class _Many:
    def __init__(self, kind, in_refs, out_refs, send_sems, recv_sems, local_sems):
        self.ops = [kind(x, o, send_sems, recv_sems, local_sems, i) for i, (x, o) in enumerate(zip(in_refs, out_refs))]

    def start(self):
        for op in self.ops:
            op.start()

    def finish(self):
        for op in self.ops:
            op.finish()


def _result_shapes(kind, arrs):
    return [jax.ShapeDtypeStruct(a.shape if kind is _Exchange else (N_DEV,) + a.shape, a.dtype) for a in arrs]


def _all_gather(name, shards):
    n = len(shards)

    def body(*refs):
        g = _Many(_Gather, refs[:n], refs[n:2 * n], *refs[2 * n:])
        g.start()
        g.finish()

    return pl.pallas_call(
        body, name=name, out_shape=_result_shapes(_Gather, shards), in_specs=[ANY] * n, out_specs=[ANY] * n,
        scratch_shapes=_sems(n),
    )(*shards)


def _peers(x, y, c):
    out = []
    for k in range(1, N_DEV):
        px = 1 - x if k & 4 else x
        py = 1 - y if k & 2 else y
        pc = 1 - c if k & 1 else c
        out.append((k, (px, py, pc), 4 * px + 2 * py + pc))
    return out


class _Exchange:
    def __init__(self, g_ref, recv_ref, send_sems, recv_sems, local_sems, idx):
        x, y, c = _place()
        me = 4 * x + 2 * y + c
        self.local = pltpu.make_async_copy(g_ref.at[me], recv_ref.at[me], local_sems.at[idx])
        self.copies = [
            pltpu.make_async_remote_copy(
                src_ref=g_ref.at[pidx], dst_ref=recv_ref.at[me], send_sem=send_sems.at[7 * idx + k - 1],
                recv_sem=recv_sems.at[7 * idx + k - 1], device_id=peer, device_id_type=MESH)
            for k, peer, pidx in _peers(x, y, c)]

    def start(self):
        self.local.start()
        for cp in self.copies:
            cp.start()

    def finish(self):
        for cp in self.copies:
            cp.wait()
        self.local.wait()


def _carried(carry, refs, n_in, n_out, first, last):
    kind, arrs = carry
    if kind is None:
        return refs, lambda: None
    n = len(arrs)
    ins, rest = refs[:n_in], refs[n_in + n:]
    outs, scratch = rest[:n_out], rest[n_out + n:]
    op = _Many(kind, refs[n_in:n_in + n], rest[n_out:n_out + n], *scratch[len(scratch) - 3:])

    @pl.when(first)
    def _():
        op.start()

    def finish():
        @pl.when(last)
        def _():
            op.finish()

    return tuple(ins) + tuple(outs) + tuple(scratch[:len(scratch) - 3]), finish


def _carried_specs(carry, in_specs, out_specs, out_shape, scratch):
    kind, arrs = carry
    if kind is None:
        return in_specs, out_specs, out_shape, scratch, []
    n = len(arrs)
    return (list(in_specs) + [ANY] * n, list(out_specs) + [ANY] * n,
            list(out_shape) + _result_shapes(kind, arrs), list(scratch) + _sems(n), list(arrs))


def _adamw(w, g, m, v):
    m = ADAM_B1 * m + (1.0 - ADAM_B1) * g
    v = ADAM_B2 * v + (1.0 - ADAM_B2) * (g * g)
    m_hat = m / (1.0 - ADAM_B1 ** ADAM_STEP)
    v_hat = v / (1.0 - ADAM_B2 ** ADAM_STEP)
    delta = -ADAM_LR * (m_hat / (jnp.sqrt(v_hat) + ADAM_EPS) + ADAM_WD * w)
    return delta, m, v


def _adamw_sum(name, recvs, w, m, v):
    L, R, C = w.shape
    tm = 128 if R % 128 == 0 else 64
    assert R % tm == 0 and len(recvs) == L

    def body(*refs):
        r_refs, (w_ref, m_ref, v_ref, g_ref, d_ref, nm_ref, nv_ref) = refs[:L], refs[L:]
        for l in range(L):
            g = r_refs[l][0].astype(F32)
            for s in range(1, N_DEV):
                g = g + r_refs[l][s].astype(F32)
            g_ref[l] = g
            d_ref[l], nm_ref[l], nv_ref[l] = _adamw(w_ref[l], g, m_ref[l], v_ref[l])

    blk = pl.BlockSpec((L, tm, C), lambda i: (0, i, 0))
    return pl.pallas_call(
        body, name=name, grid=(R // tm,),
        in_specs=[pl.BlockSpec((N_DEV, tm, C), lambda i: (0, i, 0))] * L + [blk, blk, blk],
        out_specs=[blk] * 4, out_shape=[jax.ShapeDtypeStruct((L, R, C), F32)] * 4,
        compiler_params=_params(dimension_semantics=("arbitrary",)),
    )(*recvs, w, m, v)


def _small_sync(part, w, m, v):
    def body(p_ref, w_ref, m_ref, v_ref, g_ref, d_ref, nm_ref, nv_ref, gath, send_sems, recv_sems):
        x, y, c = _place()
        me = 4 * x + 2 * y + c
        gath[me] = p_ref[...]
        copies = []
        for k, peer, _ in _peers(x, y, c):
            cp = pltpu.make_async_remote_copy(
                src_ref=p_ref, dst_ref=gath.at[me], send_sem=send_sems.at[k - 1], recv_sem=recv_sems.at[k - 1],
                device_id=peer, device_id_type=MESH)
            cp.start()
            copies.append(cp)
        for cp in copies:
            cp.wait()
        g = gath[0]
        for s in range(1, N_DEV):
            g = g + gath[s]
        wv = w_ref[...]
        l0, l1 = w_ref[8:9, :], w_ref[9:10, :]
        mx = jnp.maximum(l0, l1)
        e0, e1 = jnp.exp(l0 - mx), jnp.exp(l1 - mx)
        g9 = g[9:10, :] * (e0 / (e0 + e1)) * (e1 / (e0 + e1))
        row = lax.broadcasted_iota(jnp.int32, g.shape, 0)
        g = jnp.where(row == 9, g9, jnp.where(row == 8, -g9, g))
        g_ref[...] = g
        d_ref[...], nm_ref[...], nv_ref[...] = _adamw(wv, g, m_ref[...], v_ref[...])

    vm = pl.BlockSpec(memory_space=pltpu.VMEM)
    return pl.pallas_call(
        body, name="small_params_sync", in_specs=[vm] * 4, out_specs=[vm] * 4,
        out_shape=[jax.ShapeDtypeStruct(part.shape, F32)] * 4,
        scratch_shapes=[pltpu.VMEM((N_DEV,) + part.shape, F32), pltpu.SemaphoreType.DMA((7,)),
                        pltpu.SemaphoreType.DMA((7,))],
    )(part, w, m, v)


def _shards_bf16(d, pieces):
    return [d[name][layer].astype(BF16) for name, layer in pieces]


def _gathered(arrs, pieces, out):
    for a, (name, layer) in zip(arrs, pieces):
        out[name, layer] = a if name in COL_SHARDED else a.reshape(N_DEV * a.shape[1], a.shape[2])


def _pad_row(a, width=D_MODEL):
    a = a.reshape(1, -1)
    return jnp.pad(a, ((0, 0), (0, width - a.shape[1])))


def _pack_small(d, gn_full):
    rows = [d["mix_norm"], d["mlp_norm"], d["final_norm"].reshape(1, D_MODEL),
            _pad_row(d["attn_b_qkv"], 2 * D_MODEL).reshape(2, D_MODEL), _pad_row(d["attn_sinks"]),
            d["hgrn_lower_bounds"], gn_full.reshape(1, D_MODEL)]
    p = jnp.concatenate(rows, axis=0)
    return jnp.pad(p, ((0, SMALL_ROWS - p.shape[0]), (0, 0)))


def _unpack_small(p, me):
    return dict(
        mix_norm=p[0:2], mlp_norm=p[2:4], final_norm=p[4],
        attn_b_qkv=p[5:7].reshape(1, 2 * D_MODEL)[:, :QKV_DIM], attn_sinks=p[7:8, :N_Q_HEADS],
        hgrn_lower_bounds=p[8:10], hgrn_g_norm=lax.dynamic_slice(p[10:11], (0, me * 128), (1, 128)))


WEIGHT_NAMES = ['mix_norm', 'mlp_norm', 'final_norm', 'attn_w_qkv', 'attn_b_qkv', 'attn_sinks', 'attn_w_o', 'hgrn_w_in',
                'hgrn_g_norm', 'hgrn_w_o', 'hgrn_lower_bounds', 'mlp_w_up', 'mlp_w_down']
SMALL_NAMES = ('mix_norm', 'mlp_norm', 'final_norm', 'attn_b_qkv', 'attn_sinks', 'hgrn_lower_bounds', 'hgrn_g_norm')


def _rotary_tables(positions):
    inv_freq = ROPE_THETA ** (-jnp.arange(0, 2 * ROT_HALF, 2, dtype=F32) / (2 * ROT_HALF))
    ang = positions.astype(F32).reshape(-1, 1) * inv_freq
    cos, sin = jnp.cos(ang), jnp.sin(ang)
    r = jnp.arange(LANES) % HEAD_DIM
    idx = r % ROT_HALF
    c = jnp.where(r < 2 * ROT_HALF, cos[:, idx], 1.0)
    sa = jnp.where((r >= ROT_HALF) & (r < 2 * ROT_HALF), sin[:, idx], 0.0)
    sb = jnp.where(r < ROT_HALF, -sin[:, idx], 0.0)
    return jnp.concatenate([c, sa, sb], axis=1)


def kernel(x, positions, mix_norm, mlp_norm, final_norm, attn_w_qkv, attn_b_qkv, attn_sinks, attn_w_o, hgrn_w_in, hgrn_g_norm, hgrn_w_o, hgrn_lower_bounds, mlp_w_up, mlp_w_down, loss_target, m_mix_norm, m_mlp_norm, m_final_norm, m_attn_w_qkv, m_attn_b_qkv, m_attn_sinks, m_attn_w_o, m_hgrn_w_in, m_hgrn_g_norm, m_hgrn_w_o, m_hgrn_lower_bounds, m_mlp_w_up, m_mlp_w_down, v_mix_norm, v_mlp_norm, v_final_norm, v_attn_w_qkv, v_attn_b_qkv, v_attn_sinks, v_attn_w_o, v_hgrn_w_in, v_hgrn_g_norm, v_hgrn_w_o, v_hgrn_lower_bounds, v_mlp_w_up, v_mlp_w_down):
    w = dict(mix_norm=mix_norm, mlp_norm=mlp_norm, final_norm=final_norm, attn_w_qkv=attn_w_qkv, attn_b_qkv=attn_b_qkv,
             attn_sinks=attn_sinks, attn_w_o=attn_w_o, hgrn_w_in=hgrn_w_in, hgrn_g_norm=hgrn_g_norm, hgrn_w_o=hgrn_w_o,
             hgrn_lower_bounds=hgrn_lower_bounds, mlp_w_up=mlp_w_up, mlp_w_down=mlp_w_down)
    m = dict(mix_norm=m_mix_norm, mlp_norm=m_mlp_norm, final_norm=m_final_norm, attn_w_qkv=m_attn_w_qkv,
             attn_b_qkv=m_attn_b_qkv, attn_sinks=m_attn_sinks, attn_w_o=m_attn_w_o, hgrn_w_in=m_hgrn_w_in,
             hgrn_g_norm=m_hgrn_g_norm, hgrn_w_o=m_hgrn_w_o, hgrn_lower_bounds=m_hgrn_lower_bounds, mlp_w_up=m_mlp_w_up,
             mlp_w_down=m_mlp_w_down)
    v = dict(mix_norm=v_mix_norm, mlp_norm=v_mlp_norm, final_norm=v_final_norm, attn_w_qkv=v_attn_w_qkv,
             attn_b_qkv=v_attn_b_qkv, attn_sinks=v_attn_sinks, attn_w_o=v_attn_w_o, hgrn_w_in=v_hgrn_w_in,
             hgrn_g_norm=v_hgrn_g_norm, hgrn_w_o=v_hgrn_w_o, hgrn_lower_bounds=v_hgrn_lower_bounds, mlp_w_up=v_mlp_w_up,
             mlp_w_down=v_mlp_w_down)
    me = 4 * lax.axis_index("x") + 2 * lax.axis_index("y") + lax.axis_index("c")

    gn = hgrn_g_norm.reshape(1, 128)
    gn_a = gn.astype(BF16)
    gn_b = (gn - gn_a.astype(F32)).astype(BF16)
    gn_c = (gn - gn_a.astype(F32) - gn_b.astype(F32)).astype(BF16)
    gn_rows = jnp.pad(jnp.concatenate([gn_a, gn_b, gn_c], axis=1), ((0, 15), (0, D_MODEL - 3 * 128)))
    full = {}
    got = _all_gather("gather_attn_weights", _shards_bf16(w, GATHER_FIRST) + [gn_rows])
    _gathered(got[:2], GATHER_FIRST, full)
    w_qkv = full["attn_w_qkv", 0].transpose(1, 0, 2).reshape(D_MODEL, QKV_DIM)
    gn_terms = got[2][:, 0, :3 * 128].astype(F32).reshape(N_DEV, 3, 128)
    gn_full = ((gn_terms[:, 0] + gn_terms[:, 1]) + gn_terms[:, 2]).reshape(1, D_MODEL)

    x0 = x[0]
    tgt = loss_target[0]
    rot = _rotary_tables(positions)
    row = lambda a: a.reshape(1, -1)

    qkv, h0 = _norm_mm("qkv_proj", x0, row(mix_norm[0]), w_qkv, attn_b_qkv, rot=rot)
    att, *got = _attn_fwd(qkv, attn_sinks, carry=(_Gather, _shards_bf16(w, GATHER_ATTN)))
    _gathered(got, GATHER_ATTN, full)
    x1 = _mm_res("attn_out_proj", att, full["attn_w_o", 0], x0)
    u0, h1 = _norm_mm("mlp0_up", x1, row(mlp_norm[0]), full["mlp_w_up", 0])
    x2 = _mlp_down("mlp0_down", u0, full["mlp_w_down", 0], x1)
    z, h2 = _norm_mm("hgrn_in_proj", x2, row(mix_norm[1]), full["hgrn_w_in", 0])
    o_raw, states, *got = _hgrn_fwd(z, hgrn_lower_bounds, carry=(_Gather, _shards_bf16(w, GATHER_HGRN)))
    _gathered(got, GATHER_HGRN, full)
    x3, o2 = _hgrn_out("hgrn_out_proj", o_raw, z, gn_full, full["hgrn_w_o", 0], x2)
    u1, h3 = _norm_mm("mlp1_up", x3, row(mlp_norm[1]), full["mlp_w_up", 1])
    x4 = _mlp_down("mlp1_down", u1, full["mlp_w_down", 1], x3)
    dx4, loss_part, g_final = _loss_head("loss_head", x4, tgt, row(final_norm))

    gw = {}
    du1, a1 = _mlp_bwd_act("mlp1_bwd_act", dx4, u1, full["mlp_w_down", 1])
    dx3, g_mlp1 = _mm_nt_rmsbwd("mlp1_bwd_in", du1, full["mlp_w_up", 1], x3, row(mlp_norm[1]), dx4)
    gw["mlp_w_down", 1] = _mm_tn("mlp1_dw_down", a1, dx4, "rows")
    gw["mlp_w_up", 1] = _mm_tn("mlp1_dw_up", h3, du1, "cols")

    do_raw, dg, g_gn = _hgrn_out_bwd("hgrn_out_bwd", dx3, o_raw, z, full["hgrn_w_o", 0], gn_full)
    gw["hgrn_w_o", 0] = _mm_tn("hgrn_dw_o", o2, dx3, "rows")
    recvs = {}
    dzq, dzf, dzi, g_lb, *recv = _hgrn_bwd(z, hgrn_lower_bounds, states, do_raw,
                                           carry=(_Exchange, [gw[p] for p in GRAD_GROUPS[0]]))
    recvs.update(zip(GRAD_GROUPS[0], recv))
    dz = [dzq, dzf, dzi, dg]
    dx2, g_mix1 = _mm_nt_rmsbwd("hgrn_in_bwd", dz, full["hgrn_w_in", 0], x2, row(mix_norm[1]), dx3)
    gw["hgrn_w_in", 0] = jnp.concatenate(
        [_mm_tn(f"hgrn_dw_in{j}", h2, d, "cols") for j, d in enumerate(dz)], axis=0)

    du0, a0 = _mlp_bwd_act("mlp0_bwd_act", dx2, u0, full["mlp_w_down", 0])
    dx1, g_mlp0 = _mm_nt_rmsbwd("mlp0_bwd_in", du0, full["mlp_w_up", 0], x1, row(mlp_norm[0]), dx2)
    gw["mlp_w_down", 0] = _mm_tn("mlp0_dw_down", a0, dx2, "rows")
    gw["mlp_w_up", 0] = _mm_tn("mlp0_dw_up", h1, du0, "cols")

    datt = _mm_nt("attn_out_bwd", dx1, full["attn_w_o", 0], BF16)
    gw["attn_w_o", 0] = _mm_tn("attn_dw_o", att, dx1, "rows")
    dqkv, g_sink, *recv = _attn_bwd(qkv, rot, attn_sinks, datt, carry=(_Exchange, [gw[p] for p in GRAD_GROUPS[1]]))
    recvs.update(zip(GRAD_GROUPS[1], recv))
    g_qkv = _mm_tn("attn_dw_qkv", h0, dqkv)
    g_qkv = g_qkv.reshape(D_MODEL, N_DEV, QKV_DIM // N_DEV).transpose(1, 0, 2).astype(BF16)
    dx0, g_mix0, g_bqkv, recvs["attn_w_qkv", 0] = _mm_nt_rmsbwd(
        "qkv_bwd", dqkv, w_qkv, x0, row(mix_norm[0]), dx1, with_colsum=True, carry=(_Exchange, [g_qkv]))

    big = {name: _adamw_sum("adamw_" + name, [recvs[name, l] for l in range(w[name].shape[0])], w[name], m[name], v[name])
           for name in BIG_NAMES}

    zero_row = jnp.zeros((1, D_MODEL), F32)
    part = _pack_small(dict(
        mix_norm=jnp.concatenate([g_mix0, g_mix1], axis=0), mlp_norm=jnp.concatenate([g_mlp0, g_mlp1], axis=0),
        final_norm=g_final, attn_b_qkv=g_bqkv, attn_sinks=g_sink[:, :N_Q_HEADS],
        hgrn_lower_bounds=jnp.concatenate([zero_row, g_lb], axis=0)), g_gn)

    def spread(a):
        return lax.dynamic_update_slice(zero_row, a.reshape(1, 128), (0, me * 128))

    small_in = [_pack_small({n: d[n] for n in SMALL_NAMES if n != "hgrn_g_norm"}, spread(d["hgrn_g_norm"]))
                for d in (w, m, v)]
    small = [_unpack_small(p, me) for p in _small_sync(part, *small_in)]

    loss = lax.psum(loss_part[0, 0], ("x", "y", "c"))
    outs = [loss, dx0.reshape(x.shape)]
    for kind, grp_small in enumerate(small):
        for name in WEIGHT_NAMES:
            val = grp_small[name] if name in SMALL_NAMES else big[name][kind]
            outs.append(val.reshape(w[name].shape))
    return tuple(outs)
```

```python
import functools

import jax
import jax.numpy as jnp
from jax import lax
from jax.experimental import pallas as pl
from jax.experimental.pallas import tpu as pltpu

F32 = jnp.float32
BF16 = jnp.bfloat16

D_MODEL = 1024
HEAD_DIM = 64
N_Q_HEADS = 16
Q_DIM = 1024
KV_DIM = 256
QKV_DIM = 1536
ATT_BLOCK = 128
ROT_HALF = 8
ROPE_THETA = 500000.0
NEG_INF = -1e30
HGRN_HEADS = 8
HGRN_DK = 128
CHUNK = 64
D_FF = 4096
NORM_EPS = 1e-5
N_DEV = 8

ADAM_LR = 0.001
ADAM_B1 = 0.9
ADAM_B2 = 0.999
ADAM_EPS = 1e-08
ADAM_WD = 0.01
ADAM_STEP = 10

LANES = 128
VMEM_LIMIT = 56 * 1024 * 1024

GATHER_FIRST = (("attn_w_qkv", 0),)
GATHER_ATTN = (("attn_w_o", 0), ("mlp_w_up", 0), ("mlp_w_down", 0), ("hgrn_w_in", 0), ("hgrn_w_o", 0))
GATHER_HGRN = (("mlp_w_up", 1), ("mlp_w_down", 1))
GRAD_GROUPS = ((("mlp_w_down", 1), ("mlp_w_up", 1), ("hgrn_w_o", 0)),
               (("hgrn_w_in", 0), ("mlp_w_down", 0), ("mlp_w_up", 0), ("attn_w_o", 0)),
               (("attn_w_qkv", 0),))
COL_SHARDED = ("attn_w_qkv", "hgrn_w_in", "mlp_w_up")
BIG_NAMES = ("attn_w_qkv", "attn_w_o", "hgrn_w_in", "hgrn_w_o", "mlp_w_up", "mlp_w_down")
SMALL_ROWS = 16


def _dot(a, b):
    return jnp.dot(a, b, preferred_element_type=F32)


def _dot_nt(a, b):
    return lax.dot_general(a, b, (((1,), (1,)), ((), ())), preferred_element_type=F32)


def _dot_tn(a, b):
    return lax.dot_general(a, b, (((0,), (0,)), ((), ())), preferred_element_type=F32)


def _params(**kw):
    return pltpu.CompilerParams(vmem_limit_bytes=VMEM_LIMIT, **kw)


def _full_spec(a):
    nd = a.ndim
    return pl.BlockSpec(a.shape, lambda *_: (0,) * nd)


def _row_call(name, body, n_rows, tm, row_ins, full_ins, row_outs, acc_outs=(), carry=(None, None)):
    steps = n_rows // tm
    in_specs = [pl.BlockSpec((tm, w), functools.partial(lambda i, cb: (i, cb), cb=cb)) for _, w, cb in row_ins]
    in_specs += [_full_spec(a) for a in full_ins]
    out_shape = [jax.ShapeDtypeStruct((n_rows, w), dt) for w, dt in row_outs]
    out_specs = [pl.BlockSpec((tm, w), lambda i: (i, 0)) for w, _ in row_outs]
    for shp, dt in acc_outs:
        out_shape.append(jax.ShapeDtypeStruct(shp, dt))
        out_specs.append(pl.BlockSpec(shp, functools.partial(lambda i, nd: (0,) * nd, nd=len(shp))))
    n_in, n_out = len(in_specs), len(out_specs)
    in_specs, out_specs, out_shape, scratch, extra = _carried_specs(carry, in_specs, out_specs, out_shape, [])

    def wrapped(*refs):
        i = pl.program_id(0)
        own, finish = _carried(carry, refs, n_in, n_out, i == 0, i == steps - 1)
        body(*own)
        finish()

    return pl.pallas_call(
        wrapped, name=name, grid=(steps,), in_specs=in_specs, out_specs=out_specs, out_shape=out_shape,
        scratch_shapes=scratch, compiler_params=_params(dimension_semantics=("arbitrary",)),
    )(*[a for a, _, _ in row_ins], *full_ins, *extra)


def _rms(x, gain):
    r = lax.rsqrt(jnp.mean(x * x, axis=-1, keepdims=True) + NORM_EPS)
    xhat = x * r
    return xhat * gain, xhat, r


def _rms_bwd(dy, xhat, r, gain):
    dxhat = dy * gain
    dx = r * (dxhat - xhat * jnp.mean(dxhat * xhat, axis=-1, keepdims=True))
    return dx, dy * xhat


def _norm_mm(name, x, gain, w, bias=None, rot=None, tm=256):
    T = x.shape[0]
    tm = min(tm, T)
    nc = 512
    blocked = w.ndim == 3
    n = N_DEV * w.shape[2] if blocked else w.shape[1]
    assert n % nc == 0 and (not blocked or w.shape[2] == nc)

    def body(*refs):
        x_ref, refs = refs[0], refs[1:]
        if rot is not None:
            t_ref, refs = refs[0], refs[1:]
        g_ref, w_ref, refs = refs[0], refs[1], refs[2:]
        if bias is not None:
            b_ref, refs = refs[0], refs[1:]
        y_ref, h_ref = refs
        h, _, _ = _rms(x_ref[...], g_ref[...])
        hb = h.astype(BF16)
        h_ref[...] = hb
        for c in range(n // nc):
            sl = slice(c * nc, (c + 1) * nc)
            y = _dot(hb, w_ref[c] if blocked else w_ref[:, sl])
            if bias is not None:
                y = y + b_ref[:, sl]
            if rot is None:
                y_ref[:, sl] = y
            else:
                n_rot = max(0, min(nc, Q_DIM + KV_DIM - c * nc)) // LANES
                pieces = _rot_fwd(y[:, :n_rot * LANES], t_ref[...]) if n_rot else []
                for j in range(nc // LANES):
                    col = slice(c * nc + j * LANES, c * nc + (j + 1) * LANES)
                    y_ref[:, col] = pieces[j] if j < n_rot else y[:, j * LANES:(j + 1) * LANES]

    rows = [(x, D_MODEL, 0)] + ([(rot, 3 * LANES, 0)] if rot is not None else [])
    full = [gain, w] + ([bias] if bias is not None else [])
    return _row_call(name, body, T, tm, rows, full, [(n, F32), (D_MODEL, BF16)])


def _mm_res(name, a, w, res, tm=512):
    T = a.shape[0]
    tm = min(tm, T)

    def body(a_ref, r_ref, w_ref, o_ref):
        o_ref[...] = r_ref[...] + _dot(a_ref[...], w_ref[...])

    return _row_call(name, body, T, tm, [(a, a.shape[1], 0), (res, D_MODEL, 0)], [w], [(D_MODEL, F32)])[0]


def _mlp_down(name, u, w, res, tm=256):
    T = u.shape[0]
    tm = min(tm, T)
    kc = 1024

    def body(u_ref, r_ref, w_ref, o_ref):
        acc = r_ref[...]
        for c in range(D_FF // kc):
            sl = slice(c * kc, (c + 1) * kc)
            a = jnp.maximum(u_ref[:, sl], 0.0)
            acc = acc + _dot((a * a).astype(BF16), w_ref[sl, :])
        o_ref[...] = acc

    return _row_call(name, body, T, tm, [(u, D_FF, 0), (res, D_MODEL, 0)], [w], [(D_MODEL, F32)])[0]


def _hgrn_out(name, o_raw, z, gn, w, res, tm=256):
    T = o_raw.shape[0]
    tm = min(tm, T)

    def body(o_ref, g_ref, r_ref, gn_ref, w_ref, x_ref, a_ref):
        y, _, _ = _rms(o_ref[...], gn_ref[...])
        g = g_ref[...]
        a = (y * (g * jax.nn.sigmoid(g))).astype(BF16)
        a_ref[...] = a
        x_ref[...] = r_ref[...] + _dot(a, w_ref[...])

    return _row_call(name, body, T, tm, [(o_raw, D_MODEL, 0), (z, D_MODEL, 3), (res, D_MODEL, 0)], [gn, w],
                     [(D_MODEL, F32), (D_MODEL, BF16)])


def _loss_head(name, x, target, gain, tm=512):
    T = x.shape[0]
    tm = min(tm, T)

    def body(x_ref, t_ref, g_ref, dx_ref, loss_ref, dg_ref):
        @pl.when(pl.program_id(0) == 0)
        def _():
            loss_ref[...] = jnp.zeros_like(loss_ref)
            dg_ref[...] = jnp.zeros_like(dg_ref)

        gain_v = g_ref[...]
        y, xhat, r = _rms(x_ref[...], gain_v)
        diff = y - t_ref[...]
        row = jnp.sum(diff * diff, axis=-1, keepdims=True) * (1.0 / D_MODEL)
        loss_ref[...] += jnp.broadcast_to(0.5 * jnp.sum(row, axis=0, keepdims=True), loss_ref.shape)
        dy = diff * (1.0 / D_MODEL)
        dx, dgr = _rms_bwd(dy, xhat, r, gain_v)
        dx_ref[...] = dx
        dg_ref[...] += jnp.sum(dgr, axis=0, keepdims=True)

    return _row_call(name, body, T, tm, [(x, D_MODEL, 0), (target, D_MODEL, 0)], [gain], [(D_MODEL, F32)],
                     [((1, LANES), F32), ((1, D_MODEL), F32)])


def _mm_nt_rmsbwd(name, dy, w, x, gain, dres, tm=256, with_colsum=False, carry=(None, None)):
    T = x.shape[0]
    tm = min(tm, T)
    dys = list(dy) if isinstance(dy, (list, tuple)) else [dy]
    width = dys[0].shape[1]
    n = width * len(dys)
    assert not with_colsum or len(dys) == 1

    def body(*refs):
        dy_refs, refs = refs[:len(dys)], refs[len(dys):]
        if with_colsum:
            x_ref, dr_ref, w_ref, g_ref, dx_ref, dg_ref, cs_ref = refs
        else:
            x_ref, dr_ref, w_ref, g_ref, dx_ref, dg_ref = refs

        @pl.when(pl.program_id(0) == 0)
        def _():
            dg_ref[...] = jnp.zeros_like(dg_ref)
            if with_colsum:
                cs_ref[...] = jnp.zeros_like(cs_ref)

        if w.ndim == 3:
            nb = w.shape[2]
            dh = None
            for p in range(N_DEV):
                piece, off = divmod(p * nb, width)
                part = _dot_nt(dy_refs[piece][:, off:off + nb].astype(BF16), w_ref[p])
                dh = part if dh is None else dh + part
        else:
            dh = _dot_nt(dy_refs[0][...].astype(BF16), w_ref[...])
        gain_v = g_ref[...]
        _, xhat, r = _rms(x_ref[...], gain_v)
        dx, dgr = _rms_bwd(dh, xhat, r, gain_v)
        dx_ref[...] = dr_ref[...] + dx
        dg_ref[...] += jnp.sum(dgr, axis=0, keepdims=True)
        if with_colsum:
            cs_ref[...] += jnp.sum(dy_refs[0][...].astype(F32), axis=0, keepdims=True)

    acc = [((1, D_MODEL), F32)] + ([((1, n), F32)] if with_colsum else [])
    rows = [(d, width, 0) for d in dys] + [(x, D_MODEL, 0), (dres, D_MODEL, 0)]
    return _row_call(name, body, T, tm, rows, [w, gain], [(D_MODEL, F32)], acc, carry=carry)


def _mm_nt(name, dy, w, out_dtype, tm=512):
    T = dy.shape[0]
    tm = min(tm, T)
    k = w.shape[0]

    def body(dy_ref, w_ref, o_ref):
        o_ref[...] = _dot_nt(dy_ref[...].astype(BF16), w_ref[...]).astype(out_dtype)

    return _row_call(name, body, T, tm, [(dy, dy.shape[1], 0)], [w], [(k, out_dtype)])[0]


def _mlp_bwd_act(name, dy, u, w_down, tm=256):
    T = u.shape[0]
    tm = min(tm, T)
    kc = 1024

    def body(dy_ref, u_ref, w_ref, du_ref, a_ref):
        dyb = dy_ref[...].astype(BF16)
        for c in range(D_FF // kc):
            sl = slice(c * kc, (c + 1) * kc)
            a = jnp.maximum(u_ref[:, sl], 0.0)
            da = _dot_nt(dyb, w_ref[sl, :])
            du_ref[:, sl] = (da * (2.0 * a)).astype(BF16)
            a_ref[:, sl] = (a * a).astype(BF16)

    return _row_call(name, body, T, tm, [(dy, D_MODEL, 0), (u, D_FF, 0)], [w_down], [(D_FF, BF16), (D_FF, BF16)])


def _hgrn_out_bwd(name, dx, o_raw, z, w, gn, tm=256):
    T = dx.shape[0]
    tm = min(tm, T)

    def body(dx_ref, o_ref, g_ref, w_ref, gn_ref, do_ref, dg_ref, dgn_ref):
        @pl.when(pl.program_id(0) == 0)
        def _():
            dgn_ref[...] = jnp.zeros_like(dgn_ref)

        da = _dot_nt(dx_ref[...].astype(BF16), w_ref[...])
        gn_v = gn_ref[...]
        y, xhat, r = _rms(o_ref[...], gn_v)
        g = g_ref[...]
        sg = jax.nn.sigmoid(g)
        dg_ref[...] = (da * y * (sg * (1.0 + g * (1.0 - sg)))).astype(BF16)
        dyn = da * (g * sg)
        do, dgr = _rms_bwd(dyn, xhat, r, gn_v)
        do_ref[...] = do
        dgn_ref[...] += jnp.sum(dgr, axis=0, keepdims=True)

    return _row_call(name, body, T, tm, [(dx, D_MODEL, 0), (o_raw, D_MODEL, 0), (z, D_MODEL, 3)], [w, gn],
                     [(D_MODEL, F32), (D_MODEL, BF16)], [((1, D_MODEL), F32)])


def _mm_tn(name, a, b, shard=None, bm=1024, bn=512, tk=2048):
    T, M = a.shape
    N = b.shape[1]
    bm, bn, tk = min(bm, M), min(bn, N), min(tk, T)
    nk = T // tk
    if shard is None:
        out_shape, out_block = jax.ShapeDtypeStruct((M, N), F32), (bm, bn)
        out_map = lambda i, j, k: (i, j)
    elif shard == "cols":
        assert N % bn == 0
        out_shape, out_block = jax.ShapeDtypeStruct((N // bn, M, bn), BF16), (1, bm, bn)
        out_map = lambda i, j, k: (j, i, 0)
    else:
        rows = M // N_DEV
        assert bm % rows == 0
        out_shape, out_block = jax.ShapeDtypeStruct((N_DEV, rows, N), BF16), (bm // rows, rows, bn)
        out_map = lambda i, j, k: (i, 0, j)

    def body(a_ref, b_ref, o_ref, acc):
        k = pl.program_id(2)

        @pl.when(k == 0)
        def _():
            acc[...] = jnp.zeros_like(acc)

        acc[...] += _dot_tn(a_ref[...].astype(BF16), b_ref[...].astype(BF16))

        @pl.when(k == nk - 1)
        def _():
            o_ref[...] = acc[...].reshape(out_block).astype(o_ref.dtype)

    return pl.pallas_call(
        body, name=name, grid=(M // bm, N // bn, nk),
        in_specs=[pl.BlockSpec((tk, bm), lambda i, j, k: (k, i)), pl.BlockSpec((tk, bn), lambda i, j, k: (k, j))],
        out_specs=pl.BlockSpec(out_block, out_map), out_shape=out_shape,
        scratch_shapes=[pltpu.VMEM((bm, bn), F32)],
        compiler_params=_params(dimension_semantics=("parallel", "parallel", "arbitrary")),
    )(a, b)


def _rot_fwd(x, tab):
    c, sa, sb = tab[:, :LANES], tab[:, LANES:2 * LANES], tab[:, 2 * LANES:]
    outs = []
    for j in range(x.shape[1] // LANES):
        xs = x[:, j * LANES:(j + 1) * LANES]
        outs.append(xs * c + pltpu.roll(xs, ROT_HALF, 1) * sa + pltpu.roll(xs, LANES - ROT_HALF, 1) * sb)
    return outs


def _rot_bwd(dys, tab):
    c, sa, sb = tab[:, :LANES], tab[:, LANES:2 * LANES], tab[:, 2 * LANES:]
    return [dy * c + pltpu.roll(dy * sa, LANES - ROT_HALF, 1) + pltpu.roll(dy * sb, ROT_HALF, 1) for dy in dys]


ATT_SCALE = HEAD_DIM ** -0.5
ATT_ROWS = 128


def _attn_masks(n):
    qi = lax.broadcasted_iota(jnp.int32, (ATT_BLOCK, 2 * ATT_BLOCK), 0)
    kj = lax.broadcasted_iota(jnp.int32, (ATT_BLOCK, 2 * ATT_BLOCK), 1)
    delta = qi + ATT_BLOCK - kj
    first_key = jnp.where(n > 0, 0, ATT_BLOCK)
    valid = (delta >= 0) & (delta < ATT_BLOCK) & (kj >= first_key)
    lane = lax.broadcasted_iota(jnp.int32, (1, LANES), 1)
    return valid, lane < HEAD_DIM


def _attn_probs(qm, k_use, valid, sink):
    s = jnp.where(valid, _dot_nt(qm, k_use), NEG_INF)
    m = jnp.maximum(jnp.max(s, axis=-1, keepdims=True), sink)
    e = jnp.exp(s - m)
    es = jnp.exp(sink - m)
    inv = 1.0 / (jnp.sum(e, axis=-1, keepdims=True) + es)
    return e * inv, es * inv


def _attn_specs(nb, tables):
    prev = lambda n: jnp.maximum(jnp.minimum(n, nb - 1) - 1, 0)
    cur = lambda n: jnp.minimum(n, nb - 1)
    specs = [
        pl.BlockSpec((ATT_BLOCK, Q_DIM), lambda n: (cur(n), 0)),
        pl.BlockSpec((ATT_BLOCK, KV_DIM), lambda n: (prev(n), 4)),
        pl.BlockSpec((ATT_BLOCK, KV_DIM), lambda n: (cur(n), 4)),
        pl.BlockSpec((ATT_BLOCK, KV_DIM), lambda n: (prev(n), 5)),
        pl.BlockSpec((ATT_BLOCK, KV_DIM), lambda n: (cur(n), 5)),
    ]
    if tables:
        specs += [pl.BlockSpec((ATT_BLOCK, 3 * LANES), lambda n: (prev(n), 0)),
                  pl.BlockSpec((ATT_BLOCK, 3 * LANES), lambda n: (cur(n), 0))]
    return specs + [pl.BlockSpec(memory_space=pltpu.SMEM)]


def _kv_band(kp_ref, kc_ref, vp_ref, vc_ref):
    ks, vs = [], []
    for j in range(KV_DIM // LANES):
        sl = slice(j * LANES, (j + 1) * LANES)
        kb = jnp.concatenate([kp_ref[:, sl], kc_ref[:, sl]], axis=0)
        vb = jnp.concatenate([vp_ref[:, sl], vc_ref[:, sl]], axis=0)
        ks.append((kb.astype(BF16), pltpu.roll(kb, HEAD_DIM, 1).astype(BF16)))
        vs.append((vb.astype(BF16), pltpu.roll(vb, HEAD_DIM, 1).astype(BF16)))
    return ks, vs


def _attn_fwd(qkv, sinks, carry=(None, None)):
    T = qkv.shape[0]
    nb = T // ATT_BLOCK

    def body(*refs):
        n = pl.program_id(0)
        own, finish = _carried(carry, refs, 6, 1, n == 0, n == nb - 1)
        q_ref, kp_ref, kc_ref, vp_ref, vc_ref, sink_ref, o_ref = own
        valid, low = _attn_masks(n)
        ks, vs = _kv_band(kp_ref, kc_ref, vp_ref, vc_ref)
        for p in range(Q_DIM // LANES):
            kpair, khalf = p // 4, (p // 2) % 2
            q_pair = q_ref[:, p * LANES:(p + 1) * LANES] * ATT_SCALE
            for r0 in range(0, ATT_BLOCK, ATT_ROWS):
                rows = slice(r0, r0 + ATT_ROWS)
                outs = []
                for hf in range(2):
                    qm = jnp.where(low if hf == 0 else ~low, q_pair[rows], 0.0).astype(BF16)
                    sw = 0 if khalf == hf else 1
                    pr, _ = _attn_probs(qm, ks[kpair][sw], valid[rows], sink_ref[0, 2 * p + hf])
                    outs.append(_dot(pr.astype(BF16), vs[kpair][sw]))
                o_ref[rows, p * LANES:(p + 1) * LANES] = jnp.where(low, outs[0], outs[1]).astype(BF16)
        finish()

    in_specs, out_specs, out_shape, scratch, extra = _carried_specs(
        carry, _attn_specs(nb, False), [pl.BlockSpec((ATT_BLOCK, Q_DIM), lambda n: (n, 0))],
        [jax.ShapeDtypeStruct((T, Q_DIM), BF16)], [])
    return pl.pallas_call(
        body, name="attn_fwd", grid=(nb,), in_specs=in_specs, out_specs=out_specs, out_shape=out_shape,
        scratch_shapes=scratch, compiler_params=_params(dimension_semantics=("arbitrary",)),
    )(qkv, qkv, qkv, qkv, qkv, sinks, *extra)


def _attn_bwd(qkv, rot, sinks, dout, carry=(None, None)):
    T = qkv.shape[0]
    nb = T // ATT_BLOCK
    npair = KV_DIM // LANES

    def body(*refs):
        n = pl.program_id(0)
        own, finish = _carried(carry, refs, 9, 2, n == 0, n == nb)
        (q_ref, kp_ref, kc_ref, vp_ref, vc_ref, tp_ref, tc_ref, sink_ref, do_ref, dqkv_ref, dsink_ref,
         dq_c, dk_c, dv_c) = own

        @pl.when(n == 0)
        def _():
            dq_c[...] = jnp.zeros_like(dq_c)
            dk_c[...] = jnp.zeros_like(dk_c)
            dv_c[...] = jnp.zeros_like(dv_c)
            dsink_ref[...] = jnp.zeros_like(dsink_ref)

        def flush(dk_prev, dv_prev, tab_ref):
            dqkv_ref[:, :Q_DIM] = dq_c[...]
            dk = _rot_bwd([dk_c[:, j * LANES:(j + 1) * LANES] + dk_prev[j] for j in range(npair)], tab_ref[...])
            for j in range(npair):
                dqkv_ref[:, Q_DIM + j * LANES:Q_DIM + (j + 1) * LANES] = dk[j]
                dqkv_ref[:, Q_DIM + KV_DIM + j * LANES:Q_DIM + KV_DIM + (j + 1) * LANES] = (
                    dv_c[:, j * LANES:(j + 1) * LANES] + dv_prev[j])

        @pl.when(n < nb)
        def _():
            _, low = _attn_masks(n)
            kj = lax.broadcasted_iota(jnp.int32, (2 * ATT_BLOCK, ATT_BLOCK), 0)
            qi = lax.broadcasted_iota(jnp.int32, (2 * ATT_BLOCK, ATT_BLOCK), 1)
            delta = qi + ATT_BLOCK - kj
            valid = (delta >= 0) & (delta < ATT_BLOCK) & (kj >= jnp.where(n > 0, 0, ATT_BLOCK))
            upper = lax.broadcasted_iota(jnp.int32, (LANES, 1), 0) < HEAD_DIM
            lane = lax.broadcasted_iota(jnp.int32, (1, LANES), 1)
            ks, vs = _kv_band(kp_ref, kc_ref, vp_ref, vc_ref)
            kts = [[a.astype(F32).T.astype(BF16) for a in pair] for pair in ks]
            dk_acc = [[jnp.zeros((2 * ATT_BLOCK, LANES), F32) for _ in range(2)] for _ in range(npair)]
            dv_acc = [[jnp.zeros((2 * ATT_BLOCK, LANES), F32) for _ in range(2)] for _ in range(npair)]
            dsink = jnp.zeros((1, LANES), F32)
            dqs = []
            for p in range(Q_DIM // LANES):
                kpair, khalf = p // 4, (p // 2) % 2
                q_pair = q_ref[:, p * LANES:(p + 1) * LANES] * ATT_SCALE
                do_pair = do_ref[:, p * LANES:(p + 1) * LANES]
                dq_t = []
                for hf in range(2):
                    sel = low if hf == 0 else ~low
                    qm = jnp.where(sel, q_pair, 0.0).astype(BF16)
                    dom = jnp.where(sel, do_pair, 0.0).astype(BF16)
                    sw = 0 if khalf == hf else 1
                    sink = sink_ref[0, 2 * p + hf]
                    s = jnp.where(valid, _dot_nt(ks[kpair][sw], qm), NEG_INF)
                    m = jnp.maximum(jnp.max(s, axis=0, keepdims=True), sink)
                    e = jnp.exp(s - m)
                    es = jnp.exp(sink - m)
                    inv = 1.0 / (jnp.sum(e, axis=0, keepdims=True) + es)
                    pr = e * inv
                    dp = _dot_nt(vs[kpair][sw], dom)
                    dd = jnp.sum(pr * dp, axis=0, keepdims=True)
                    ds = (pr * (dp - dd)).astype(BF16)
                    dq_t.append(_dot(kts[kpair][sw], ds))
                    dk_acc[kpair][sw] = dk_acc[kpair][sw] + _dot(ds, qm)
                    dv_acc[kpair][sw] = dv_acc[kpair][sw] + _dot(pr.astype(BF16), dom)
                    dsink = dsink + jnp.where(lane == 2 * p + hf, -jnp.sum(es * inv * dd, axis=1, keepdims=True), 0.0)
                dqs.append(jnp.where(upper, dq_t[0], dq_t[1]).T * ATT_SCALE)
            dk_acc = [a[0] + pltpu.roll(a[1], HEAD_DIM, 1) for a in dk_acc]
            dv_acc = [a[0] + pltpu.roll(a[1], HEAD_DIM, 1) for a in dv_acc]
            flush([a[:ATT_BLOCK] for a in dk_acc], [a[:ATT_BLOCK] for a in dv_acc], tp_ref)
            dq = _rot_bwd(dqs, tc_ref[...])
            for p in range(Q_DIM // LANES):
                dq_c[:, p * LANES:(p + 1) * LANES] = dq[p]
            for j in range(npair):
                dk_c[:, j * LANES:(j + 1) * LANES] = dk_acc[j][ATT_BLOCK:]
                dv_c[:, j * LANES:(j + 1) * LANES] = dv_acc[j][ATT_BLOCK:]
            dsink_ref[...] += dsink

        @pl.when(n == nb)
        def _():
            zero = [jnp.zeros((ATT_BLOCK, LANES), F32) for _ in range(npair)]
            flush(zero, zero, tc_ref)

        finish()

    do_spec = pl.BlockSpec((ATT_BLOCK, Q_DIM), lambda n: (jnp.minimum(n, nb - 1), 0))
    in_specs, out_specs, out_shape, scratch, extra = _carried_specs(
        carry, _attn_specs(nb, True) + [do_spec],
        [pl.BlockSpec((ATT_BLOCK, QKV_DIM), lambda n: (jnp.maximum(n - 1, 0), 0)),
         pl.BlockSpec((1, LANES), lambda n: (0, 0))],
        [jax.ShapeDtypeStruct((T, QKV_DIM), F32), jax.ShapeDtypeStruct((1, LANES), F32)],
        [pltpu.VMEM((ATT_BLOCK, Q_DIM), F32), pltpu.VMEM((ATT_BLOCK, KV_DIM), F32),
         pltpu.VMEM((ATT_BLOCK, KV_DIM), F32)])
    return pl.pallas_call(
        body, name="attn_bwd", grid=(nb + 1,), in_specs=in_specs, out_specs=out_specs, out_shape=out_shape,
        scratch_shapes=scratch, compiler_params=_params(dimension_semantics=("arbitrary",)),
    )(qkv, qkv, qkv, qkv, qkv, rot, rot, sinks, dout, *extra)


LEVELS = (32, 16, 8)
DIAG = 8
UNROLL = 4
UNROLL_BWD = 2


def _lower_bound(lb_ref):
    l0, l1 = lb_ref[0:1, :], lb_ref[1:2, :]
    mx = jnp.maximum(l0, l1)
    e0, e1 = jnp.exp(l0 - mx), jnp.exp(l1 - mx)
    return e1 / (e0 + e1)


GROUPS = CHUNK // DIAG


def _group_roll(x, k):
    return pltpu.roll(x.reshape(GROUPS, DIAG, HGRN_DK), k % DIAG, 1).reshape(CHUNK, HGRN_DK)


def _scan_rows(x, row, reverse):
    r8 = row & (DIAG - 1)
    for sh in (1, 2, 4):
        ok = (r8 < DIAG - sh) if reverse else (r8 >= sh)
        x = x + jnp.where(ok, _group_roll(x, -sh if reverse else sh), 0.0)
    g = x.reshape(GROUPS, DIAG, HGRN_DK)
    edge = 0 if reverse else DIAG - 1
    tot = jnp.broadcast_to(g[:, edge:edge + 1, :], g.shape)

    def shifted(a, n):
        z = jnp.zeros((n, DIAG, HGRN_DK), F32)
        return jnp.concatenate([a[n:], z] if reverse else [z, a[:GROUPS - n]], axis=0)

    acc = shifted(tot, 1)
    for sh in (1, 2, 4):
        acc = acc + shifted(acc, sh)
    return (g + acc).reshape(CHUNK, HGRN_DK)


def _level_masks():
    t = lax.broadcasted_iota(jnp.int32, (CHUNK, CHUNK), 0)
    s = lax.broadcasted_iota(jnp.int32, (CHUNK, CHUNK), 1)
    return [((t & h) != 0) & ((s & h) == 0) & ((t ^ s) < 2 * h) for h in LEVELS]


def _level_scale(b, h):
    parts = [jnp.broadcast_to(b[j * 2 * h + h - 1:j * 2 * h + h, :], (2 * h, HGRN_DK)) for j in range(CHUNK // (2 * h))]
    mid = parts[0] if len(parts) == 1 else jnp.concatenate(parts, axis=0)
    return jnp.exp(-jnp.abs(b - mid))


def _hgrn_gates(zq, zf, lb):
    sq = jax.nn.sigmoid(zq)
    q = zq * sq
    sg = jax.nn.sigmoid(zf)
    forget = lb + (1.0 - lb) * sg
    return q, sq, sg, forget, 1.0 - forget, jnp.log(forget)


def _hgrn_specs(T, rb, rev):
    nr = T // rb
    ri = (lambda r: nr - 1 - r) if rev else (lambda r: r)
    return nr, ri, [
        pl.BlockSpec((rb, HGRN_DK), lambda h, r: (ri(r), h)),
        pl.BlockSpec((rb, HGRN_DK), lambda h, r: (ri(r), HGRN_HEADS + h)),
        pl.BlockSpec((rb, HGRN_DK), lambda h, r: (ri(r), 2 * HGRN_HEADS + h)),
        pl.BlockSpec((2, HGRN_DK), lambda h, r: (0, h)),
    ]


def _hgrn_fwd(z, lb_raw, rb=1024, carry=(None, None)):
    T = z.shape[0]
    rb = min(rb, T)
    ncb = rb // CHUNK
    nr, ri, in_specs = _hgrn_specs(T, rb, False)

    def body(*refs):
        hh, rr = pl.program_id(0), pl.program_id(1)
        own, finish = _carried(carry, refs, 4, 2, (hh == 0) & (rr == 0), (hh == HGRN_HEADS - 1) & (rr == nr - 1))
        zq_ref, zf_ref, zi_ref, lb_ref, o_ref, st_ref, state = own

        @pl.when(rr == 0)
        def _():
            state[...] = jnp.zeros_like(state)

        lb = _lower_bound(lb_ref)
        row = lax.broadcasted_iota(jnp.int32, (CHUNK, HGRN_DK), 0)
        masks = _level_masks()
        r8 = row & (DIAG - 1)

        def chunk(c, st):
            rows = pl.ds(pl.multiple_of(c * CHUNK, CHUNK), CHUNK)
            q, _, _, _, k, lf = _hgrn_gates(zq_ref[rows, :], zf_ref[rows, :], lb)
            v = zi_ref[rows, :]
            vb = v.astype(BF16)
            b = _scan_rows(lf, row, False)
            sc = jnp.zeros((CHUNK, CHUNK), F32)
            for h, mask in zip(LEVELS, masks):
                e = _level_scale(b, h)
                sc = sc + jnp.where(mask, _dot_nt((q * e).astype(BF16), (k * e).astype(BF16)), 0.0)
            o = _dot(sc.astype(BF16), vb) + jnp.sum(q * k, axis=-1, keepdims=True) * v
            for d in range(1, DIAG):
                w = jnp.where(r8 >= d, q * _group_roll(k, d) * jnp.exp(b - _group_roll(b, d)), 0.0)
                o = o + jnp.sum(w, axis=-1, keepdims=True) * _group_roll(v, d)
            b_last = b[CHUNK - 1:CHUNK, :]
            kd = (k * jnp.exp(b_last - b)).astype(BF16)
            qd = (q * jnp.exp(b)).astype(BF16)
            st_ref[c, 0] = st
            o_ref[rows, :] = o + _dot_nt(qd, st.astype(BF16))
            return st * jnp.exp(b_last) + _dot_tn(vb, kd)

        def group(i, st):
            for j in range(UNROLL):
                st = chunk(i * UNROLL + j, st)
            return st

        state[...] = lax.fori_loop(0, ncb // UNROLL, group, state[...])
        finish()

    in_specs, out_specs, out_shape, scratch, extra = _carried_specs(
        carry, in_specs,
        [pl.BlockSpec((rb, HGRN_DK), lambda h, r: (r, h)),
         pl.BlockSpec((ncb, 1, HGRN_DK, HGRN_DK), lambda h, r: (r, h, 0, 0))],
        [jax.ShapeDtypeStruct((T, D_MODEL), F32),
         jax.ShapeDtypeStruct((T // CHUNK, HGRN_HEADS, HGRN_DK, HGRN_DK), F32)],
        [pltpu.VMEM((HGRN_DK, HGRN_DK), F32)])
    return pl.pallas_call(
        body, name="hgrn_fwd", grid=(HGRN_HEADS, nr), in_specs=in_specs, out_specs=out_specs, out_shape=out_shape,
        scratch_shapes=scratch, compiler_params=_params(dimension_semantics=("arbitrary", "arbitrary")),
    )(z, z, z, lb_raw, *extra)


def _hgrn_bwd(z, lb_raw, states, do, rb=1024, carry=(None, None)):
    T = z.shape[0]
    rb = min(rb, T)
    ncb = rb // CHUNK
    nr, ri, in_specs = _hgrn_specs(T, rb, True)
    in_specs += [pl.BlockSpec((ncb, 1, HGRN_DK, HGRN_DK), lambda h, r: (ri(r), h, 0, 0)),
                 pl.BlockSpec((rb, HGRN_DK), lambda h, r: (ri(r), h))]

    def body(*refs):
        hh, rr = pl.program_id(0), pl.program_id(1)
        own, finish = _carried(carry, refs, 6, 4, (hh == 0) & (rr == 0), (hh == HGRN_HEADS - 1) & (rr == nr - 1))
        zq_ref, zf_ref, zi_ref, lb_ref, st_ref, do_ref, dq_ref, df_ref, di_ref, dlb_ref, dstate = own

        @pl.when(rr == 0)
        def _():
            dstate[...] = jnp.zeros_like(dstate)
            dlb_ref[...] = jnp.zeros_like(dlb_ref)

        lb = _lower_bound(lb_ref)
        row = lax.broadcasted_iota(jnp.int32, (CHUNK, HGRN_DK), 0)
        masks = _level_masks()
        r8 = row & (DIAG - 1)

        def chunk(ci, dlb):
            c = ncb - 1 - ci
            rows = pl.ds(pl.multiple_of(c * CHUNK, CHUNK), CHUNK)
            zq = zq_ref[rows, :]
            q, sq, sg, forget, k, lf = _hgrn_gates(zq, zf_ref[rows, :], lb)
            v = zi_ref[rows, :]
            dov = do_ref[rows, :]
            b = _scan_rows(lf, row, False)
            st = st_ref[c, 0]
            dst = dstate[...]
            b_last = b[CHUNK - 1:CHUNK, :]
            eb = jnp.exp(b)
            ebb = jnp.exp(b_last - b)
            e_last = jnp.exp(b_last)
            dob, vb, stb, dstb = dov.astype(BF16), v.astype(BF16), st.astype(BF16), dst.astype(BF16)
            dq = eb * _dot(dob, stb)
            dv = _dot_nt((k * ebb).astype(BF16), dstb)
            dk = ebb * _dot(vb, dstb)
            extra = e_last * jnp.sum(dst * st, axis=0, keepdims=True) + jnp.sum(k * dk, axis=0, keepdims=True)
            da = _dot_nt(dob, vb)
            sc = jnp.zeros((CHUNK, CHUNK), F32)
            for h, mask in zip(LEVELS, masks):
                e = _level_scale(b, h)
                qs, ks = (q * e).astype(BF16), (k * e).astype(BF16)
                dam = jnp.where(mask, da, 0.0).astype(BF16)
                dq = dq + e * _dot(dam, ks)
                dk = dk + e * _dot_tn(dam, qs)
                sc = sc + jnp.where(mask, _dot_nt(qs, ks), 0.0)
            dv = dv + _dot_tn(sc.astype(BF16), dob)
            dad = jnp.sum(dov * v, axis=-1, keepdims=True)
            dq = dq + dad * k
            dk = dk + dad * q
            dv = dv + jnp.sum(q * k, axis=-1, keepdims=True) * dov
            for d in range(1, DIAG):
                w = jnp.where(r8 >= d, jnp.exp(b - _group_roll(b, d)), 0.0)
                kr = _group_roll(k, d)
                dad = jnp.sum(dov * _group_roll(v, d), axis=-1, keepdims=True)
                ad = jnp.sum(q * kr * w, axis=-1, keepdims=True)
                dq = dq + dad * kr * w
                dk = dk + _group_roll(dad * q * w, -d)
                dv = dv + _group_roll(ad * dov, -d)
            dlf = _scan_rows(q * dq - k * dk, row, True) + extra
            dstate[...] = dst * e_last + _dot_tn(dob, (q * eb).astype(BF16))
            dforget = dlf / forget - dk
            dq_ref[rows, :] = (dq * (sq * (1.0 + zq * (1.0 - sq)))).astype(BF16)
            df_ref[rows, :] = (dforget * (1.0 - lb) * sg * (1.0 - sg)).astype(BF16)
            di_ref[rows, :] = dv.astype(BF16)
            return dlb + jnp.sum(dforget * (1.0 - sg), axis=0, keepdims=True)

        def group(i, dlb):
            for j in range(UNROLL_BWD):
                dlb = chunk(i * UNROLL_BWD + j, dlb)
            return dlb

        dlb_ref[...] += lax.fori_loop(0, ncb // UNROLL_BWD, group, jnp.zeros((1, HGRN_DK), F32))
        finish()

    blk = pl.BlockSpec((rb, HGRN_DK), lambda h, r: (ri(r), h))
    in_specs, out_specs, out_shape, scratch, extra = _carried_specs(
        carry, in_specs, [blk, blk, blk, pl.BlockSpec((1, HGRN_DK), lambda h, r: (0, h))],
        [jax.ShapeDtypeStruct((T, D_MODEL), BF16)] * 3 + [jax.ShapeDtypeStruct((1, D_MODEL), F32)],
        [pltpu.VMEM((HGRN_DK, HGRN_DK), F32)])
    return pl.pallas_call(
        body, name="hgrn_bwd", grid=(HGRN_HEADS, nr), in_specs=in_specs, out_specs=out_specs, out_shape=out_shape,
        scratch_shapes=scratch, compiler_params=_params(dimension_semantics=("arbitrary", "arbitrary")),
    )(z, z, z, lb_raw, states, do, *extra)


MESH = pl.DeviceIdType.MESH
ANY = pl.BlockSpec(memory_space=pl.ANY)


def _place():
    return lax.axis_index("x"), lax.axis_index("y"), lax.axis_index("c")


def _sems(n):
    return [pltpu.SemaphoreType.DMA((7 * n,)), pltpu.SemaphoreType.DMA((7 * n,)), pltpu.SemaphoreType.DMA((n,))]


class _Gather:
    def __init__(self, x_ref, out_ref, send_sems, recv_sems, local_sems, idx):
        self.x_ref, self.out_ref, self.send_sems, self.recv_sems, self.local_sem, self.base = (
            x_ref, out_ref, send_sems, recv_sems, local_sems.at[idx], 7 * idx)
        x, y, c = _place()
        self.c = c
        self.me, self.sibling = (x, y, c), (x, y, 1 - c)
        self.chips = [(1 - x, y), (x, 1 - y), (1 - x, 1 - y)]

    def rows(self, px, py, pc):
        return self.out_ref.at[4 * px + 2 * py + pc]

    def copy(self, k, block, to, from_input=False):
        return pltpu.make_async_remote_copy(
            src_ref=self.x_ref if from_input else self.rows(*block), dst_ref=self.rows(*block),
            send_sem=self.send_sems.at[self.base + k], recv_sem=self.recv_sems.at[self.base + k], device_id=to,
            device_id_type=MESH)

    def first(self):
        out = [self.copy(0, self.me, self.sibling, from_input=True)]
        return out + [self.copy(1 + j, self.me, (*chip, self.c), from_input=True) for j, chip in enumerate(self.chips)]

    def start(self):
        pltpu.make_async_copy(self.x_ref, self.rows(*self.me), self.local_sem).start()
        for cp in self.first():
            cp.start()

    def finish(self):
        passed = [self.copy(4 + j, (*chip, self.c), self.sibling) for j, chip in enumerate(self.chips)]
        for j, chip in enumerate(self.chips):
            self.copy(1 + j, (*chip, self.c), self.me).wait_recv()
            passed[j].start()
        self.copy(0, self.sibling, self.me).wait_recv()
        for j, chip in enumerate(self.chips):
            self.copy(4 + j, (*chip, 1 - self.c), self.me).wait_recv()
        for cp in self.first() + passed:
            cp.wait_send()
        pltpu.make_async_copy(self.x_ref, self.rows(*self.me), self.local_sem).wait()


class _Many:
    def __init__(self, kind, in_refs, out_refs, send_sems, recv_sems, local_sems):
        self.ops = [kind(x, o, send_sems, recv_sems, local_sems, i) for i, (x, o) in enumerate(zip(in_refs, out_refs))]

    def start(self):
        for op in self.ops:
            op.start()

    def finish(self):
        for op in self.ops:
            op.finish()


def _result_shapes(kind, arrs):
    return [jax.ShapeDtypeStruct(a.shape if kind is _Exchange else (N_DEV,) + a.shape, a.dtype) for a in arrs]


def _all_gather(name, shards):
    n = len(shards)

    def body(*refs):
        g = _Many(_Gather, refs[:n], refs[n:2 * n], *refs[2 * n:])
        g.start()
        g.finish()

    return pl.pallas_call(
        body, name=name, out_shape=_result_shapes(_Gather, shards), in_specs=[ANY] * n, out_specs=[ANY] * n,
        scratch_shapes=_sems(n),
    )(*shards)


def _peers(x, y, c):
    out = []
    for k in range(1, N_DEV):
        px = 1 - x if k & 4 else x
        py = 1 - y if k & 2 else y
        pc = 1 - c if k & 1 else c
        out.append((k, (px, py, pc), 4 * px + 2 * py + pc))
    return out


class _Exchange:
    def __init__(self, g_ref, recv_ref, send_sems, recv_sems, local_sems, idx):
        x, y, c = _place()
        me = 4 * x + 2 * y + c
        self.local = pltpu.make_async_copy(g_ref.at[me], recv_ref.at[me], local_sems.at[idx])
        self.copies = [
            pltpu.make_async_remote_copy(
                src_ref=g_ref.at[pidx], dst_ref=recv_ref.at[me], send_sem=send_sems.at[7 * idx + k - 1],
                recv_sem=recv_sems.at[7 * idx + k - 1], device_id=peer, device_id_type=MESH)
            for k, peer, pidx in _peers(x, y, c)]

    def start(self):
        self.local.start()
        for cp in self.copies:
            cp.start()

    def finish(self):
        for cp in self.copies:
            cp.wait()
        self.local.wait()


def _carried(carry, refs, n_in, n_out, first, last):
    kind, arrs = carry
    if kind is None:
        return refs, lambda: None
    n = len(arrs)
    ins, rest = refs[:n_in], refs[n_in + n:]
    outs, scratch = rest[:n_out], rest[n_out + n:]
    op = _Many(kind, refs[n_in:n_in + n], rest[n_out:n_out + n], *scratch[len(scratch) - 3:])

    @pl.when(first)
    def _():
        op.start()

    def finish():
        @pl.when(last)
        def _():
            op.finish()

    return tuple(ins) + tuple(outs) + tuple(scratch[:len(scratch) - 3]), finish


def _carried_specs(carry, in_specs, out_specs, out_shape, scratch):
    kind, arrs = carry
    if kind is None:
        return in_specs, out_specs, out_shape, scratch, []
    n = len(arrs)
    return (list(in_specs) + [ANY] * n, list(out_specs) + [ANY] * n,
            list(out_shape) + _result_shapes(kind, arrs), list(scratch) + _sems(n), list(arrs))


def _adamw(w, g, m, v):
    m = ADAM_B1 * m + (1.0 - ADAM_B1) * g
    v = ADAM_B2 * v + (1.0 - ADAM_B2) * (g * g)
    m_hat = m / (1.0 - ADAM_B1 ** ADAM_STEP)
    v_hat = v / (1.0 - ADAM_B2 ** ADAM_STEP)
    delta = -ADAM_LR * (m_hat / (jnp.sqrt(v_hat) + ADAM_EPS) + ADAM_WD * w)
    return delta, m, v


def _adamw_sum(name, recvs, w, m, v):
    L, R, C = w.shape
    tm = 128 if R % 128 == 0 else 64
    assert R % tm == 0 and len(recvs) == L

    def body(*refs):
        r_refs, (w_ref, m_ref, v_ref, g_ref, d_ref, nm_ref, nv_ref) = refs[:L], refs[L:]
        for l in range(L):
            g = r_refs[l][0].astype(F32)
            for s in range(1, N_DEV):
                g = g + r_refs[l][s].astype(F32)
            g_ref[l] = g
            d_ref[l], nm_ref[l], nv_ref[l] = _adamw(w_ref[l], g, m_ref[l], v_ref[l])

    blk = pl.BlockSpec((L, tm, C), lambda i: (0, i, 0))
    return pl.pallas_call(
        body, name=name, grid=(R // tm,),
        in_specs=[pl.BlockSpec((N_DEV, tm, C), lambda i: (0, i, 0))] * L + [blk, blk, blk],
        out_specs=[blk] * 4, out_shape=[jax.ShapeDtypeStruct((L, R, C), F32)] * 4,
        compiler_params=_params(dimension_semantics=("arbitrary",)),
    )(*recvs, w, m, v)


def _small_sync(part, w, m, v):
    def body(p_ref, w_ref, m_ref, v_ref, g_ref, d_ref, nm_ref, nv_ref, gath, send_sems, recv_sems):
        x, y, c = _place()
        me = 4 * x + 2 * y + c
        gath[me] = p_ref[...]
        copies = []
        for k, peer, _ in _peers(x, y, c):
            cp = pltpu.make_async_remote_copy(
                src_ref=p_ref, dst_ref=gath.at[me], send_sem=send_sems.at[k - 1], recv_sem=recv_sems.at[k - 1],
                device_id=peer, device_id_type=MESH)
            cp.start()
            copies.append(cp)
        for cp in copies:
            cp.wait()
        g = gath[0]
        for s in range(1, N_DEV):
            g = g + gath[s]
        wv = w_ref[...]
        l0, l1 = w_ref[8:9, :], w_ref[9:10, :]
        mx = jnp.maximum(l0, l1)
        e0, e1 = jnp.exp(l0 - mx), jnp.exp(l1 - mx)
        g9 = g[9:10, :] * (e0 / (e0 + e1)) * (e1 / (e0 + e1))
        row = lax.broadcasted_iota(jnp.int32, g.shape, 0)
        g = jnp.where(row == 9, g9, jnp.where(row == 8, -g9, g))
        g_ref[...] = g
        d_ref[...], nm_ref[...], nv_ref[...] = _adamw(wv, g, m_ref[...], v_ref[...])

    vm = pl.BlockSpec(memory_space=pltpu.VMEM)
    return pl.pallas_call(
        body, name="small_params_sync", in_specs=[vm] * 4, out_specs=[vm] * 4,
        out_shape=[jax.ShapeDtypeStruct(part.shape, F32)] * 4,
        scratch_shapes=[pltpu.VMEM((N_DEV,) + part.shape, F32), pltpu.SemaphoreType.DMA((7,)),
                        pltpu.SemaphoreType.DMA((7,))],
    )(part, w, m, v)


def _shards_bf16(d, pieces):
    return [d[name][layer].astype(BF16) for name, layer in pieces]


def _gathered(arrs, pieces, out):
    for a, (name, layer) in zip(arrs, pieces):
        out[name, layer] = a if name in COL_SHARDED else a.reshape(N_DEV * a.shape[1], a.shape[2])


def _pad_row(a, width=D_MODEL):
    a = a.reshape(1, -1)
    return jnp.pad(a, ((0, 0), (0, width - a.shape[1])))


def _pack_small(d, gn_full):
    rows = [d["mix_norm"], d["mlp_norm"], d["final_norm"].reshape(1, D_MODEL),
            _pad_row(d["attn_b_qkv"], 2 * D_MODEL).reshape(2, D_MODEL), _pad_row(d["attn_sinks"]),
            d["hgrn_lower_bounds"], gn_full.reshape(1, D_MODEL)]
    p = jnp.concatenate(rows, axis=0)
    return jnp.pad(p, ((0, SMALL_ROWS - p.shape[0]), (0, 0)))


def _unpack_small(p, me):
    return dict(
        mix_norm=p[0:2], mlp_norm=p[2:4], final_norm=p[4],
        attn_b_qkv=p[5:7].reshape(1, 2 * D_MODEL)[:, :QKV_DIM], attn_sinks=p[7:8, :N_Q_HEADS],
        hgrn_lower_bounds=p[8:10], hgrn_g_norm=lax.dynamic_slice(p[10:11], (0, me * 128), (1, 128)))


WEIGHT_NAMES = ['mix_norm', 'mlp_norm', 'final_norm', 'attn_w_qkv', 'attn_b_qkv', 'attn_sinks', 'attn_w_o', 'hgrn_w_in',
                'hgrn_g_norm', 'hgrn_w_o', 'hgrn_lower_bounds', 'mlp_w_up', 'mlp_w_down']
SMALL_NAMES = ('mix_norm', 'mlp_norm', 'final_norm', 'attn_b_qkv', 'attn_sinks', 'hgrn_lower_bounds', 'hgrn_g_norm')


def _rotary_tables(positions):
    inv_freq = ROPE_THETA ** (-jnp.arange(0, 2 * ROT_HALF, 2, dtype=F32) / (2 * ROT_HALF))
    ang = positions.astype(F32).reshape(-1, 1) * inv_freq
    cos, sin = jnp.cos(ang), jnp.sin(ang)
    r = jnp.arange(LANES) % HEAD_DIM
    idx = r % ROT_HALF
    c = jnp.where(r < 2 * ROT_HALF, cos[:, idx], 1.0)
    sa = jnp.where((r >= ROT_HALF) & (r < 2 * ROT_HALF), sin[:, idx], 0.0)
    sb = jnp.where(r < ROT_HALF, -sin[:, idx], 0.0)
    return jnp.concatenate([c, sa, sb], axis=1)


def kernel(x, positions, mix_norm, mlp_norm, final_norm, attn_w_qkv, attn_b_qkv, attn_sinks, attn_w_o, hgrn_w_in, hgrn_g_norm, hgrn_w_o, hgrn_lower_bounds, mlp_w_up, mlp_w_down, loss_target, m_mix_norm, m_mlp_norm, m_final_norm, m_attn_w_qkv, m_attn_b_qkv, m_attn_sinks, m_attn_w_o, m_hgrn_w_in, m_hgrn_g_norm, m_hgrn_w_o, m_hgrn_lower_bounds, m_mlp_w_up, m_mlp_w_down, v_mix_norm, v_mlp_norm, v_final_norm, v_attn_w_qkv, v_attn_b_qkv, v_attn_sinks, v_attn_w_o, v_hgrn_w_in, v_hgrn_g_norm, v_hgrn_w_o, v_hgrn_lower_bounds, v_mlp_w_up, v_mlp_w_down):
    w = dict(mix_norm=mix_norm, mlp_norm=mlp_norm, final_norm=final_norm, attn_w_qkv=attn_w_qkv, attn_b_qkv=attn_b_qkv,
             attn_sinks=attn_sinks, attn_w_o=attn_w_o, hgrn_w_in=hgrn_w_in, hgrn_g_norm=hgrn_g_norm, hgrn_w_o=hgrn_w_o,
             hgrn_lower_bounds=hgrn_lower_bounds, mlp_w_up=mlp_w_up, mlp_w_down=mlp_w_down)
    m = dict(mix_norm=m_mix_norm, mlp_norm=m_mlp_norm, final_norm=m_final_norm, attn_w_qkv=m_attn_w_qkv,
             attn_b_qkv=m_attn_b_qkv, attn_sinks=m_attn_sinks, attn_w_o=m_attn_w_o, hgrn_w_in=m_hgrn_w_in,
             hgrn_g_norm=m_hgrn_g_norm, hgrn_w_o=m_hgrn_w_o, hgrn_lower_bounds=m_hgrn_lower_bounds, mlp_w_up=m_mlp_w_up,
             mlp_w_down=m_mlp_w_down)
    v = dict(mix_norm=v_mix_norm, mlp_norm=v_mlp_norm, final_norm=v_final_norm, attn_w_qkv=v_attn_w_qkv,
             attn_b_qkv=v_attn_b_qkv, attn_sinks=v_attn_sinks, attn_w_o=v_attn_w_o, hgrn_w_in=v_hgrn_w_in,
             hgrn_g_norm=v_hgrn_g_norm, hgrn_w_o=v_hgrn_w_o, hgrn_lower_bounds=v_hgrn_lower_bounds, mlp_w_up=v_mlp_w_up,
             mlp_w_down=v_mlp_w_down)
    me = 4 * lax.axis_index("x") + 2 * lax.axis_index("y") + lax.axis_index("c")

    gn = hgrn_g_norm.reshape(1, 128)
    gn_a = gn.astype(BF16)
    gn_b = (gn - gn_a.astype(F32)).astype(BF16)
    gn_c = (gn - gn_a.astype(F32) - gn_b.astype(F32)).astype(BF16)
    gn_rows = jnp.pad(jnp.concatenate([gn_a, gn_b, gn_c], axis=1), ((0, 15), (0, D_MODEL - 3 * 128)))
    full = {}
    got = _all_gather("gather_attn_weights", _shards_bf16(w, GATHER_FIRST) + [gn_rows])
    _gathered(got[:1], GATHER_FIRST, full)
    w_qkv = full["attn_w_qkv", 0].transpose(1, 0, 2).reshape(D_MODEL, QKV_DIM)
    gn_terms = got[1][:, 0, :3 * 128].astype(F32).reshape(N_DEV, 3, 128)
    gn_full = ((gn_terms[:, 0] + gn_terms[:, 1]) + gn_terms[:, 2]).reshape(1, D_MODEL)

    x0 = x[0]
    tgt = loss_target[0]
    rot = _rotary_tables(positions)
    row = lambda a: a.reshape(1, -1)

    qkv, h0 = _norm_mm("qkv_proj", x0, row(mix_norm[0]), w_qkv, attn_b_qkv, rot=rot)
    att, *got = _attn_fwd(qkv, attn_sinks, carry=(_Gather, _shards_bf16(w, GATHER_ATTN)))
    _gathered(got, GATHER_ATTN, full)
    x1 = _mm_res("attn_out_proj", att, full["attn_w_o", 0], x0)
    u0, h1 = _norm_mm("mlp0_up", x1, row(mlp_norm[0]), full["mlp_w_up", 0])
    x2 = _mlp_down("mlp0_down", u0, full["mlp_w_down", 0], x1)
    z, h2 = _norm_mm("hgrn_in_proj", x2, row(mix_norm[1]), full["hgrn_w_in", 0])
    o_raw, states, *got = _hgrn_fwd(z, hgrn_lower_bounds, carry=(_Gather, _shards_bf16(w, GATHER_HGRN)))
    _gathered(got, GATHER_HGRN, full)
    x3, o2 = _hgrn_out("hgrn_out_proj", o_raw, z, gn_full, full["hgrn_w_o", 0], x2)
    u1, h3 = _norm_mm("mlp1_up", x3, row(mlp_norm[1]), full["mlp_w_up", 1])
    x4 = _mlp_down("mlp1_down", u1, full["mlp_w_down", 1], x3)
    dx4, loss_part, g_final = _loss_head("loss_head", x4, tgt, row(final_norm))

    gw = {}
    du1, a1 = _mlp_bwd_act("mlp1_bwd_act", dx4, u1, full["mlp_w_down", 1])
    dx3, g_mlp1 = _mm_nt_rmsbwd("mlp1_bwd_in", du1, full["mlp_w_up", 1], x3, row(mlp_norm[1]), dx4)
    gw["mlp_w_down", 1] = _mm_tn("mlp1_dw_down", a1, dx4, "rows")
    gw["mlp_w_up", 1] = _mm_tn("mlp1_dw_up", h3, du1, "cols")

    do_raw, dg, g_gn = _hgrn_out_bwd("hgrn_out_bwd", dx3, o_raw, z, full["hgrn_w_o", 0], gn_full)
    gw["hgrn_w_o", 0] = _mm_tn("hgrn_dw_o", o2, dx3, "rows")
    recvs = {}
    dzq, dzf, dzi, g_lb, *recv = _hgrn_bwd(z, hgrn_lower_bounds, states, do_raw,
                                           carry=(_Exchange, [gw[p] for p in GRAD_GROUPS[0]]))
    recvs.update(zip(GRAD_GROUPS[0], recv))
    dz = [dzq, dzf, dzi, dg]
    dx2, g_mix1 = _mm_nt_rmsbwd("hgrn_in_bwd", dz, full["hgrn_w_in", 0], x2, row(mix_norm[1]), dx3)
    gw["hgrn_w_in", 0] = jnp.concatenate(
        [_mm_tn(f"hgrn_dw_in{j}", h2, d, "cols") for j, d in enumerate(dz)], axis=0)

    du0, a0 = _mlp_bwd_act("mlp0_bwd_act", dx2, u0, full["mlp_w_down", 0])
    dx1, g_mlp0 = _mm_nt_rmsbwd("mlp0_bwd_in", du0, full["mlp_w_up", 0], x1, row(mlp_norm[0]), dx2)
    gw["mlp_w_down", 0] = _mm_tn("mlp0_dw_down", a0, dx2, "rows")
    gw["mlp_w_up", 0] = _mm_tn("mlp0_dw_up", h1, du0, "cols")

    datt = _mm_nt("attn_out_bwd", dx1, full["attn_w_o", 0], BF16)
    gw["attn_w_o", 0] = _mm_tn("attn_dw_o", att, dx1, "rows")
    dqkv, g_sink, *recv = _attn_bwd(qkv, rot, attn_sinks, datt, carry=(_Exchange, [gw[p] for p in GRAD_GROUPS[1]]))
    recvs.update(zip(GRAD_GROUPS[1], recv))
    g_qkv = _mm_tn("attn_dw_qkv", h0, dqkv)
    g_qkv = g_qkv.reshape(D_MODEL, N_DEV, QKV_DIM // N_DEV).transpose(1, 0, 2).astype(BF16)
    dx0, g_mix0, g_bqkv, recvs["attn_w_qkv", 0] = _mm_nt_rmsbwd(
        "qkv_bwd", dqkv, w_qkv, x0, row(mix_norm[0]), dx1, with_colsum=True, carry=(_Exchange, [g_qkv]))

    big = {name: _adamw_sum("adamw_" + name, [recvs[name, l] for l in range(w[name].shape[0])], w[name], m[name], v[name])
           for name in BIG_NAMES}

    zero_row = jnp.zeros((1, D_MODEL), F32)
    part = _pack_small(dict(
        mix_norm=jnp.concatenate([g_mix0, g_mix1], axis=0), mlp_norm=jnp.concatenate([g_mlp0, g_mlp1], axis=0),
        final_norm=g_final, attn_b_qkv=g_bqkv, attn_sinks=g_sink[:, :N_Q_HEADS],
        hgrn_lower_bounds=jnp.concatenate([zero_row, g_lb], axis=0)), g_gn)

    def spread(a):
        return lax.dynamic_update_slice(zero_row, a.reshape(1, 128), (0, me * 128))

    small_in = [_pack_small({n: d[n] for n in SMALL_NAMES if n != "hgrn_g_norm"}, spread(d["hgrn_g_norm"]))
                for d in (w, m, v)]
    small = [_unpack_small(p, me) for p in _small_sync(part, *small_in)]

    loss = lax.psum(loss_part[0, 0], ("x", "y", "c"))
    outs = [loss, dx0.reshape(x.shape)]
    for kind, grp_small in enumerate(small):
        for name in WEIGHT_NAMES:
            val = grp_small[name] if name in SMALL_NAMES else big[name][kind]
            outs.append(val.reshape(w[name].shape))
    return tuple(outs)
```

```python
import functools

import jax
import jax.numpy as jnp
from jax import lax
from jax.experimental import pallas as pl
from jax.experimental.pallas import tpu as pltpu

F32 = jnp.float32
BF16 = jnp.bfloat16

D_MODEL = 1024
HEAD_DIM = 64
N_Q_HEADS = 16
Q_DIM = 1024
KV_DIM = 256
QKV_DIM = 1536
ATT_BLOCK = 128
ROT_HALF = 8
ROPE_THETA = 500000.0
NEG_INF = -1e30
HGRN_HEADS = 8
HGRN_DK = 128
CHUNK = 64
D_FF = 4096
NORM_EPS = 1e-5
N_DEV = 8

ADAM_LR = 0.001
ADAM_B1 = 0.9
ADAM_B2 = 0.999
ADAM_EPS = 1e-08
ADAM_WD = 0.01
ADAM_STEP = 10

LANES = 128
VMEM_LIMIT = 56 * 1024 * 1024

GATHER_FIRST = (("attn_w_qkv", 0),)
GATHER_ATTN = (("attn_w_o", 0), ("mlp_w_up", 0), ("mlp_w_down", 0))
GATHER_MLP0 = (("hgrn_w_in", 0), ("hgrn_w_o", 0))
GATHER_HGRN = (("mlp_w_up", 1), ("mlp_w_down", 1))
GRAD_GROUPS = ((("mlp_w_down", 1), ("mlp_w_up", 1), ("hgrn_w_o", 0)),
               (("hgrn_w_in", 0), ("mlp_w_down", 0), ("mlp_w_up", 0), ("attn_w_o", 0)),
               (("attn_w_qkv", 0),))
COL_SHARDED = ("attn_w_qkv", "hgrn_w_in", "mlp_w_up")
BIG_NAMES = ("attn_w_qkv", "attn_w_o", "hgrn_w_in", "hgrn_w_o", "mlp_w_up", "mlp_w_down")
SMALL_ROWS = 16


def _dot(a, b):
    return jnp.dot(a, b, preferred_element_type=F32)


def _dot_nt(a, b):
    return lax.dot_general(a, b, (((1,), (1,)), ((), ())), preferred_element_type=F32)


def _dot_tn(a, b):
    return lax.dot_general(a, b, (((0,), (0,)), ((), ())), preferred_element_type=F32)


def _params(**kw):
    return pltpu.CompilerParams(vmem_limit_bytes=VMEM_LIMIT, **kw)


def _full_spec(a):
    nd = a.ndim
    return pl.BlockSpec(a.shape, lambda *_: (0,) * nd)


def _row_call(name, body, n_rows, tm, row_ins, full_ins, row_outs, acc_outs=(), carry=(None, None)):
    steps = n_rows // tm
    in_specs = [pl.BlockSpec((tm, w), functools.partial(lambda i, cb: (i, cb), cb=cb)) for _, w, cb in row_ins]
    in_specs += [_full_spec(a) for a in full_ins]
    out_shape = [jax.ShapeDtypeStruct((n_rows, w), dt) for w, dt in row_outs]
    out_specs = [pl.BlockSpec((tm, w), lambda i: (i, 0)) for w, _ in row_outs]
    for shp, dt in acc_outs:
        out_shape.append(jax.ShapeDtypeStruct(shp, dt))
        out_specs.append(pl.BlockSpec(shp, functools.partial(lambda i, nd: (0,) * nd, nd=len(shp))))
    n_in, n_out = len(in_specs), len(out_specs)
    in_specs, out_specs, out_shape, scratch, extra = _carried_specs(carry, in_specs, out_specs, out_shape, [])

    def wrapped(*refs):
        i = pl.program_id(0)
        own, finish = _carried(carry, refs, n_in, n_out, i == 0, i == steps - 1)
        body(*own)
        finish()

    return pl.pallas_call(
        wrapped, name=name, grid=(steps,), in_specs=in_specs, out_specs=out_specs, out_shape=out_shape,
        scratch_shapes=scratch, compiler_params=_params(dimension_semantics=("arbitrary",)),
    )(*[a for a, _, _ in row_ins], *full_ins, *extra)


def _rms(x, gain):
    r = lax.rsqrt(jnp.mean(x * x, axis=-1, keepdims=True) + NORM_EPS)
    xhat = x * r
    return xhat * gain, xhat, r


def _rms_bwd(dy, xhat, r, gain):
    dxhat = dy * gain
    dx = r * (dxhat - xhat * jnp.mean(dxhat * xhat, axis=-1, keepdims=True))
    return dx, dy * xhat


def _norm_mm(name, x, gain, w, bias=None, rot=None, tm=256, carry=(None, None)):
    T = x.shape[0]
    tm = min(tm, T)
    nc = 512
    blocked = w.ndim == 3
    n = N_DEV * w.shape[2] if blocked else w.shape[1]
    assert n % nc == 0 and (not blocked or w.shape[2] == nc)

    def body(*refs):
        x_ref, refs = refs[0], refs[1:]
        if rot is not None:
            t_ref, refs = refs[0], refs[1:]
        g_ref, w_ref, refs = refs[0], refs[1], refs[2:]
        if bias is not None:
            b_ref, refs = refs[0], refs[1:]
        y_ref, h_ref = refs
        h, _, _ = _rms(x_ref[...], g_ref[...])
        hb = h.astype(BF16)
        h_ref[...] = hb
        for c in range(n // nc):
            sl = slice(c * nc, (c + 1) * nc)
            y = _dot(hb, w_ref[c] if blocked else w_ref[:, sl])
            if bias is not None:
                y = y + b_ref[:, sl]
            if rot is None:
                y_ref[:, sl] = y
            else:
                n_rot = max(0, min(nc, Q_DIM + KV_DIM - c * nc)) // LANES
                pieces = _rot_fwd(y[:, :n_rot * LANES], t_ref[...]) if n_rot else []
                for j in range(nc // LANES):
                    col = slice(c * nc + j * LANES, c * nc + (j + 1) * LANES)
                    y_ref[:, col] = pieces[j] if j < n_rot else y[:, j * LANES:(j + 1) * LANES]

    rows = [(x, D_MODEL, 0)] + ([(rot, 3 * LANES, 0)] if rot is not None else [])
    full = [gain, w] + ([bias] if bias is not None else [])
    return _row_call(name, body, T, tm, rows, full, [(n, F32), (D_MODEL, BF16)], carry=carry)


def _mm_res(name, a, w, res, tm=512):
    T = a.shape[0]
    tm = min(tm, T)

    def body(a_ref, r_ref, w_ref, o_ref):
        o_ref[...] = r_ref[...] + _dot(a_ref[...], w_ref[...])

    return _row_call(name, body, T, tm, [(a, a.shape[1], 0), (res, D_MODEL, 0)], [w], [(D_MODEL, F32)])[0]


def _mlp_down(name, u, w, res, tm=256):
    T = u.shape[0]
    tm = min(tm, T)
    kc = 1024

    def body(u_ref, r_ref, w_ref, o_ref, a_ref):
        acc = r_ref[...]
        for c in range(D_FF // kc):
            sl = slice(c * kc, (c + 1) * kc)
            a = jnp.maximum(u_ref[:, sl], 0.0)
            ab = (a * a).astype(BF16)
            a_ref[:, sl] = ab
            acc = acc + _dot(ab, w_ref[sl, :])
        o_ref[...] = acc

    return _row_call(name, body, T, tm, [(u, D_FF, 0), (res, D_MODEL, 0)], [w], [(D_MODEL, F32), (D_FF, BF16)])


def _hgrn_out(name, o_raw, z, gn, w, res, tm=256):
    T = o_raw.shape[0]
    tm = min(tm, T)

    def body(o_ref, g_ref, r_ref, gn_ref, w_ref, x_ref, a_ref):
        y, _, _ = _rms(o_ref[...], gn_ref[...])
        g = g_ref[...]
        a = (y * (g * jax.nn.sigmoid(g))).astype(BF16)
        a_ref[...] = a
        x_ref[...] = r_ref[...] + _dot(a, w_ref[...])

    return _row_call(name, body, T, tm, [(o_raw, D_MODEL, 0), (z, D_MODEL, 3), (res, D_MODEL, 0)], [gn, w],
                     [(D_MODEL, F32), (D_MODEL, BF16)])


def _loss_head(name, x, target, gain, tm=512):
    T = x.shape[0]
    tm = min(tm, T)

    def body(x_ref, t_ref, g_ref, dx_ref, loss_ref, dg_ref):
        @pl.when(pl.program_id(0) == 0)
        def _():
            loss_ref[...] = jnp.zeros_like(loss_ref)
            dg_ref[...] = jnp.zeros_like(dg_ref)

        gain_v = g_ref[...]
        y, xhat, r = _rms(x_ref[...], gain_v)
        diff = y - t_ref[...]
        row = jnp.sum(diff * diff, axis=-1, keepdims=True) * (1.0 / D_MODEL)
        loss_ref[...] += jnp.broadcast_to(0.5 * jnp.sum(row, axis=0, keepdims=True), loss_ref.shape)
        dy = diff * (1.0 / D_MODEL)
        dx, dgr = _rms_bwd(dy, xhat, r, gain_v)
        dx_ref[...] = dx
        dg_ref[...] += jnp.sum(dgr, axis=0, keepdims=True)

    return _row_call(name, body, T, tm, [(x, D_MODEL, 0), (target, D_MODEL, 0)], [gain], [(D_MODEL, F32)],
                     [((1, LANES), F32), ((1, D_MODEL), F32)])


def _mm_nt_rmsbwd(name, dy, w, x, gain, dres, tm=256, with_colsum=False, carry=(None, None)):
    T = x.shape[0]
    tm = min(tm, T)
    dys = list(dy) if isinstance(dy, (list, tuple)) else [dy]
    width = dys[0].shape[1]
    n = width * len(dys)
    assert not with_colsum or len(dys) == 1

    def body(*refs):
        dy_refs, refs = refs[:len(dys)], refs[len(dys):]
        if with_colsum:
            x_ref, dr_ref, w_ref, g_ref, dx_ref, dg_ref, cs_ref = refs
        else:
            x_ref, dr_ref, w_ref, g_ref, dx_ref, dg_ref = refs

        @pl.when(pl.program_id(0) == 0)
        def _():
            dg_ref[...] = jnp.zeros_like(dg_ref)
            if with_colsum:
                cs_ref[...] = jnp.zeros_like(cs_ref)

        if w.ndim == 3:
            nb = w.shape[2]
            dh = None
            for p in range(N_DEV):
                piece, off = divmod(p * nb, width)
                part = _dot_nt(dy_refs[piece][:, off:off + nb].astype(BF16), w_ref[p])
                dh = part if dh is None else dh + part
        else:
            dh = _dot_nt(dy_refs[0][...].astype(BF16), w_ref[...])
        gain_v = g_ref[...]
        _, xhat, r = _rms(x_ref[...], gain_v)
        dx, dgr = _rms_bwd(dh, xhat, r, gain_v)
        dx_ref[...] = dr_ref[...] + dx
        dg_ref[...] += jnp.sum(dgr, axis=0, keepdims=True)
        if with_colsum:
            cs_ref[...] += jnp.sum(dy_refs[0][...].astype(F32), axis=0, keepdims=True)

    acc = [((1, D_MODEL), F32)] + ([((1, n), F32)] if with_colsum else [])
    rows = [(d, width, 0) for d in dys] + [(x, D_MODEL, 0), (dres, D_MODEL, 0)]
    return _row_call(name, body, T, tm, rows, [w, gain], [(D_MODEL, F32)], acc, carry=carry)


def _mm_nt(name, dy, w, out_dtype, tm=512):
    T = dy.shape[0]
    tm = min(tm, T)
    k = w.shape[0]

    def body(dy_ref, w_ref, o_ref):
        o_ref[...] = _dot_nt(dy_ref[...].astype(BF16), w_ref[...]).astype(out_dtype)

    return _row_call(name, body, T, tm, [(dy, dy.shape[1], 0)], [w], [(k, out_dtype)])[0]


def _mlp_bwd_act(name, dy, u, w_down, tm=256):
    T = u.shape[0]
    tm = min(tm, T)
    kc = 1024

    def body(dy_ref, u_ref, w_ref, du_ref):
        dyb = dy_ref[...].astype(BF16)
        for c in range(D_FF // kc):
            sl = slice(c * kc, (c + 1) * kc)
            da = _dot_nt(dyb, w_ref[sl, :])
            du_ref[:, sl] = (da * (2.0 * jnp.maximum(u_ref[:, sl], 0.0))).astype(BF16)

    return _row_call(name, body, T, tm, [(dy, D_MODEL, 0), (u, D_FF, 0)], [w_down], [(D_FF, BF16)])[0]


def _hgrn_out_bwd(name, dx, o_raw, z, w, gn, tm=256):
    T = dx.shape[0]
    tm = min(tm, T)

    def body(dx_ref, o_ref, g_ref, w_ref, gn_ref, do_ref, dg_ref, dgn_ref):
        @pl.when(pl.program_id(0) == 0)
        def _():
            dgn_ref[...] = jnp.zeros_like(dgn_ref)

        da = _dot_nt(dx_ref[...].astype(BF16), w_ref[...])
        gn_v = gn_ref[...]
        y, xhat, r = _rms(o_ref[...], gn_v)
        g = g_ref[...]
        sg = jax.nn.sigmoid(g)
        dg_ref[...] = (da * y * (sg * (1.0 + g * (1.0 - sg)))).astype(BF16)
        dyn = da * (g * sg)
        do, dgr = _rms_bwd(dyn, xhat, r, gn_v)
        do_ref[...] = do
        dgn_ref[...] += jnp.sum(dgr, axis=0, keepdims=True)

    return _row_call(name, body, T, tm, [(dx, D_MODEL, 0), (o_raw, D_MODEL, 0), (z, D_MODEL, 3)], [w, gn],
                     [(D_MODEL, F32), (D_MODEL, BF16)], [((1, D_MODEL), F32)])


def _mm_tn(name, a, b, shard=None, bm=1024, bn=512, tk=2048):
    T, M = a.shape
    N = b.shape[1]
    bm, bn, tk = min(bm, M), min(bn, N), min(tk, T)
    nk = T // tk
    if shard is None:
        out_shape, out_block = jax.ShapeDtypeStruct((M, N), F32), (bm, bn)
        out_map = lambda i, j, k: (i, j)
    elif shard == "cols":
        assert N % bn == 0
        out_shape, out_block = jax.ShapeDtypeStruct((N // bn, M, bn), BF16), (1, bm, bn)
        out_map = lambda i, j, k: (j, i, 0)
    else:
        rows = M // N_DEV
        assert bm % rows == 0
        out_shape, out_block = jax.ShapeDtypeStruct((N_DEV, rows, N), BF16), (bm // rows, rows, bn)
        out_map = lambda i, j, k: (i, 0, j)

    def body(a_ref, b_ref, o_ref, acc):
        k = pl.program_id(2)

        @pl.when(k == 0)
        def _():
            acc[...] = jnp.zeros_like(acc)

        acc[...] += _dot_tn(a_ref[...].astype(BF16), b_ref[...].astype(BF16))

        @pl.when(k == nk - 1)
        def _():
            o_ref[...] = acc[...].reshape(out_block).astype(o_ref.dtype)

    return pl.pallas_call(
        body, name=name, grid=(M // bm, N // bn, nk),
        in_specs=[pl.BlockSpec((tk, bm), lambda i, j, k: (k, i)), pl.BlockSpec((tk, bn), lambda i, j, k: (k, j))],
        out_specs=pl.BlockSpec(out_block, out_map), out_shape=out_shape,
        scratch_shapes=[pltpu.VMEM((bm, bn), F32)],
        compiler_params=_params(dimension_semantics=("parallel", "parallel", "arbitrary")),
    )(a, b)


def _rot_fwd(x, tab):
    c, sa, sb = tab[:, :LANES], tab[:, LANES:2 * LANES], tab[:, 2 * LANES:]
    outs = []
    for j in range(x.shape[1] // LANES):
        xs = x[:, j * LANES:(j + 1) * LANES]
        outs.append(xs * c + pltpu.roll(xs, ROT_HALF, 1) * sa + pltpu.roll(xs, LANES - ROT_HALF, 1) * sb)
    return outs


def _rot_bwd(dys, tab):
    c, sa, sb = tab[:, :LANES], tab[:, LANES:2 * LANES], tab[:, 2 * LANES:]
    return [dy * c + pltpu.roll(dy * sa, LANES - ROT_HALF, 1) + pltpu.roll(dy * sb, ROT_HALF, 1) for dy in dys]


ATT_SCALE = HEAD_DIM ** -0.5


def _attn_masks(n):
    kj = lax.broadcasted_iota(jnp.int32, (2 * ATT_BLOCK, ATT_BLOCK), 0)
    qi = lax.broadcasted_iota(jnp.int32, (2 * ATT_BLOCK, ATT_BLOCK), 1)
    delta = qi + ATT_BLOCK - kj
    first_key = jnp.where(n > 0, 0, ATT_BLOCK)
    valid = (delta >= 0) & (delta < ATT_BLOCK) & (kj >= first_key)
    low = lax.broadcasted_iota(jnp.int32, (1, LANES), 1) < HEAD_DIM
    upper = lax.broadcasted_iota(jnp.int32, (LANES, 1), 0) < HEAD_DIM
    return valid, low, upper


def _attn_probs(k_use, qm, valid, sink):
    s = jnp.where(valid, _dot_nt(k_use, qm), NEG_INF)
    m = jnp.maximum(jnp.max(s, axis=0, keepdims=True), sink)
    e = jnp.exp(s - m)
    es = jnp.exp(sink - m)
    inv = 1.0 / (jnp.sum(e, axis=0, keepdims=True) + es)
    return e * inv, es * inv


def _attn_specs(nb, tables):
    prev = lambda n: jnp.maximum(jnp.minimum(n, nb - 1) - 1, 0)
    cur = lambda n: jnp.minimum(n, nb - 1)
    specs = [
        pl.BlockSpec((ATT_BLOCK, Q_DIM), lambda n: (cur(n), 0)),
        pl.BlockSpec((ATT_BLOCK, KV_DIM), lambda n: (prev(n), 4)),
        pl.BlockSpec((ATT_BLOCK, KV_DIM), lambda n: (cur(n), 4)),
        pl.BlockSpec((ATT_BLOCK, KV_DIM), lambda n: (prev(n), 5)),
        pl.BlockSpec((ATT_BLOCK, KV_DIM), lambda n: (cur(n), 5)),
    ]
    if tables:
        specs += [pl.BlockSpec((ATT_BLOCK, 3 * LANES), lambda n: (prev(n), 0)),
                  pl.BlockSpec((ATT_BLOCK, 3 * LANES), lambda n: (cur(n), 0))]
    return specs + [pl.BlockSpec(memory_space=pltpu.SMEM)]


def _kv_band(prev_ref, cur_ref):
    out = []
    for j in range(KV_DIM // LANES):
        sl = slice(j * LANES, (j + 1) * LANES)
        band = jnp.concatenate([prev_ref[:, sl], cur_ref[:, sl]], axis=0)
        out.append((band, pltpu.roll(band, HEAD_DIM, 1)))
    return out


def _bf16(bands, transposed=False):
    return [[(a.T if transposed else a).astype(BF16) for a in pair] for pair in bands]


def _attn_fwd(qkv, sinks, carry=(None, None)):
    T = qkv.shape[0]
    nb = T // ATT_BLOCK

    def body(*refs):
        n = pl.program_id(0)
        own, finish = _carried(carry, refs, 6, 1, n == 0, n == nb - 1)
        q_ref, kp_ref, kc_ref, vp_ref, vc_ref, sink_ref, o_ref = own
        valid, low, upper = _attn_masks(n)
        ks = _bf16(_kv_band(kp_ref, kc_ref))
        vts = _bf16(_kv_band(vp_ref, vc_ref), transposed=True)
        for p in range(Q_DIM // LANES):
            kpair, khalf = p // 4, (p // 2) % 2
            q_pair = q_ref[:, p * LANES:(p + 1) * LANES] * ATT_SCALE
            outs = []
            for hf in range(2):
                qm = jnp.where(low if hf == 0 else ~low, q_pair, 0.0).astype(BF16)
                sw = 0 if khalf == hf else 1
                pr, _ = _attn_probs(ks[kpair][sw], qm, valid, sink_ref[0, 2 * p + hf])
                outs.append(_dot(vts[kpair][sw], pr.astype(BF16)))
            o_ref[:, p * LANES:(p + 1) * LANES] = jnp.where(upper, outs[0], outs[1]).T.astype(BF16)
        finish()

    in_specs, out_specs, out_shape, scratch, extra = _carried_specs(
        carry, _attn_specs(nb, False), [pl.BlockSpec((ATT_BLOCK, Q_DIM), lambda n: (n, 0))],
        [jax.ShapeDtypeStruct((T, Q_DIM), BF16)], [])
    return pl.pallas_call(
        body, name="attn_fwd", grid=(nb,), in_specs=in_specs, out_specs=out_specs, out_shape=out_shape,
        scratch_shapes=scratch, compiler_params=_params(dimension_semantics=("arbitrary",)),
    )(qkv, qkv, qkv, qkv, qkv, sinks, *extra)


def _attn_bwd(qkv, rot, sinks, dout, carry=(None, None)):
    T = qkv.shape[0]
    nb = T // ATT_BLOCK
    npair = KV_DIM // LANES

    def body(*refs):
        n = pl.program_id(0)
        own, finish = _carried(carry, refs, 9, 2, n == 0, n == nb)
        (q_ref, kp_ref, kc_ref, vp_ref, vc_ref, tp_ref, tc_ref, sink_ref, do_ref, dqkv_ref, dsink_ref,
         dq_c, dk_c, dv_c) = own

        @pl.when(n == 0)
        def _():
            dq_c[...] = jnp.zeros_like(dq_c)
            dk_c[...] = jnp.zeros_like(dk_c)
            dv_c[...] = jnp.zeros_like(dv_c)
            dsink_ref[...] = jnp.zeros_like(dsink_ref)

        def flush(dk_prev, dv_prev, tab_ref):
            dqkv_ref[:, :Q_DIM] = dq_c[...]
            dk = _rot_bwd([dk_c[:, j * LANES:(j + 1) * LANES] + dk_prev[j] for j in range(npair)], tab_ref[...])
            for j in range(npair):
                dqkv_ref[:, Q_DIM + j * LANES:Q_DIM + (j + 1) * LANES] = dk[j]
                dqkv_ref[:, Q_DIM + KV_DIM + j * LANES:Q_DIM + KV_DIM + (j + 1) * LANES] = (
                    dv_c[:, j * LANES:(j + 1) * LANES] + dv_prev[j])

        @pl.when(n < nb)
        def _():
            valid, low, upper = _attn_masks(n)
            lane = lax.broadcasted_iota(jnp.int32, (1, LANES), 1)
            k_band = _kv_band(kp_ref, kc_ref)
            ks, kts = _bf16(k_band), _bf16(k_band, transposed=True)
            vs = _bf16(_kv_band(vp_ref, vc_ref))
            dk_acc = [[jnp.zeros((2 * ATT_BLOCK, LANES), F32) for _ in range(2)] for _ in range(npair)]
            dv_acc = [[jnp.zeros((2 * ATT_BLOCK, LANES), F32) for _ in range(2)] for _ in range(npair)]
            dsink = jnp.zeros((1, LANES), F32)
            dqs = []
            for p in range(Q_DIM // LANES):
                kpair, khalf = p // 4, (p // 2) % 2
                q_pair = q_ref[:, p * LANES:(p + 1) * LANES] * ATT_SCALE
                do_pair = do_ref[:, p * LANES:(p + 1) * LANES]
                dq_t = []
                for hf in range(2):
                    sel = low if hf == 0 else ~low
                    qm = jnp.where(sel, q_pair, 0.0).astype(BF16)
                    dom = jnp.where(sel, do_pair, 0.0).astype(BF16)
                    sw = 0 if khalf == hf else 1
                    pr, ps = _attn_probs(ks[kpair][sw], qm, valid, sink_ref[0, 2 * p + hf])
                    dp = _dot_nt(vs[kpair][sw], dom)
                    dd = jnp.sum(pr * dp, axis=0, keepdims=True)
                    ds = (pr * (dp - dd)).astype(BF16)
                    dq_t.append(_dot(kts[kpair][sw], ds))
                    dk_acc[kpair][sw] = dk_acc[kpair][sw] + _dot(ds, qm)
                    dv_acc[kpair][sw] = dv_acc[kpair][sw] + _dot(pr.astype(BF16), dom)
                    dsink = dsink + jnp.where(lane == 2 * p + hf, -jnp.sum(ps * dd, axis=1, keepdims=True), 0.0)
                dqs.append(jnp.where(upper, dq_t[0], dq_t[1]).T * ATT_SCALE)
            dk_acc = [a[0] + pltpu.roll(a[1], HEAD_DIM, 1) for a in dk_acc]
            dv_acc = [a[0] + pltpu.roll(a[1], HEAD_DIM, 1) for a in dv_acc]
            flush([a[:ATT_BLOCK] for a in dk_acc], [a[:ATT_BLOCK] for a in dv_acc], tp_ref)
            dq = _rot_bwd(dqs, tc_ref[...])
            for p in range(Q_DIM // LANES):
                dq_c[:, p * LANES:(p + 1) * LANES] = dq[p]
            for j in range(npair):
                dk_c[:, j * LANES:(j + 1) * LANES] = dk_acc[j][ATT_BLOCK:]
                dv_c[:, j * LANES:(j + 1) * LANES] = dv_acc[j][ATT_BLOCK:]
            dsink_ref[...] += dsink

        @pl.when(n == nb)
        def _():
            zero = [jnp.zeros((ATT_BLOCK, LANES), F32) for _ in range(npair)]
            flush(zero, zero, tc_ref)

        finish()

    do_spec = pl.BlockSpec((ATT_BLOCK, Q_DIM), lambda n: (jnp.minimum(n, nb - 1), 0))
    in_specs, out_specs, out_shape, scratch, extra = _carried_specs(
        carry, _attn_specs(nb, True) + [do_spec],
        [pl.BlockSpec((ATT_BLOCK, QKV_DIM), lambda n: (jnp.maximum(n - 1, 0), 0)),
         pl.BlockSpec((1, LANES), lambda n: (0, 0))],
        [jax.ShapeDtypeStruct((T, QKV_DIM), F32), jax.ShapeDtypeStruct((1, LANES), F32)],
        [pltpu.VMEM((ATT_BLOCK, Q_DIM), F32), pltpu.VMEM((ATT_BLOCK, KV_DIM), F32),
         pltpu.VMEM((ATT_BLOCK, KV_DIM), F32)])
    return pl.pallas_call(
        body, name="attn_bwd", grid=(nb + 1,), in_specs=in_specs, out_specs=out_specs, out_shape=out_shape,
        scratch_shapes=scratch, compiler_params=_params(dimension_semantics=("arbitrary",)),
    )(qkv, qkv, qkv, qkv, qkv, rot, rot, sinks, dout, *extra)


LEVELS = (32, 16, 8)
DIAG = 8
SUBLANES = 8
UNROLL = 4
UNROLL_BWD = 2


def _lower_bound(lb_ref):
    l0, l1 = lb_ref[0:1, :], lb_ref[1:2, :]
    mx = jnp.maximum(l0, l1)
    e0, e1 = jnp.exp(l0 - mx), jnp.exp(l1 - mx)
    return e1 / (e0 + e1)


GROUPS = CHUNK // SUBLANES


def _group_roll(x, k):
    return pltpu.roll(x.reshape(GROUPS, SUBLANES, HGRN_DK), k % SUBLANES, 1).reshape(CHUNK, HGRN_DK)


def _scan_rows(x, row, reverse):
    r8 = row & (SUBLANES - 1)
    for sh in (1, 2, 4):
        ok = (r8 < SUBLANES - sh) if reverse else (r8 >= sh)
        x = x + jnp.where(ok, _group_roll(x, -sh if reverse else sh), 0.0)
    g = x.reshape(GROUPS, SUBLANES, HGRN_DK)
    edge = 0 if reverse else SUBLANES - 1
    tot = jnp.broadcast_to(g[:, edge:edge + 1, :], g.shape)

    def shifted(a, n):
        z = jnp.zeros((n, SUBLANES, HGRN_DK), F32)
        return jnp.concatenate([a[n:], z] if reverse else [z, a[:GROUPS - n]], axis=0)

    acc = shifted(tot, 1)
    for sh in (1, 2, 4):
        acc = acc + shifted(acc, sh)
    return (g + acc).reshape(CHUNK, HGRN_DK)


def _level_masks():
    t = lax.broadcasted_iota(jnp.int32, (CHUNK, CHUNK), 0)
    s = lax.broadcasted_iota(jnp.int32, (CHUNK, CHUNK), 1)
    return [((t & h) != 0) & ((s & h) == 0) & ((t ^ s) < 2 * h) for h in LEVELS]


def _level_scales(b):
    out = []
    for h in LEVELS:
        parts = [jnp.broadcast_to(b[j * 2 * h + h - 1:j * 2 * h + h, :], (2 * h, HGRN_DK))
                 for j in range(CHUNK // (2 * h))]
        mid = parts[0] if len(parts) == 1 else jnp.concatenate(parts, axis=0)
        out.append(jnp.exp(-jnp.abs(b - mid)))
    return out


def _hgrn_gates(zq, zf, lb):
    sq = jax.nn.sigmoid(zq)
    q = zq * sq
    sg = jax.nn.sigmoid(zf)
    forget = lb + (1.0 - lb) * sg
    return q, sq, sg, forget, 1.0 - forget, jnp.log(forget)


def _hgrn_specs(T, rb, rev):
    nr = T // rb
    ri = (lambda r: nr - 1 - r) if rev else (lambda r: r)
    return nr, ri, [
        pl.BlockSpec((rb, HGRN_DK), lambda h, r: (ri(r), h)),
        pl.BlockSpec((rb, HGRN_DK), lambda h, r: (ri(r), HGRN_HEADS + h)),
        pl.BlockSpec((rb, HGRN_DK), lambda h, r: (ri(r), 2 * HGRN_HEADS + h)),
        pl.BlockSpec((2, HGRN_DK), lambda h, r: (0, h)),
    ]


def _hgrn_fwd(z, lb_raw, rb=1024, carry=(None, None)):
    T = z.shape[0]
    rb = min(rb, T)
    ncb = rb // CHUNK
    nr, ri, in_specs = _hgrn_specs(T, rb, False)

    def body(*refs):
        hh, rr = pl.program_id(0), pl.program_id(1)
        own, finish = _carried(carry, refs, 4, 2, (hh == 0) & (rr == 0), (hh == HGRN_HEADS - 1) & (rr == nr - 1))
        zq_ref, zf_ref, zi_ref, lb_ref, o_ref, st_ref, state = own

        @pl.when(rr == 0)
        def _():
            state[...] = jnp.zeros_like(state)

        lb = _lower_bound(lb_ref)
        row = lax.broadcasted_iota(jnp.int32, (CHUNK, HGRN_DK), 0)
        masks = _level_masks()
        rd = row & (DIAG - 1)

        def chunk(c, st):
            rows = pl.ds(pl.multiple_of(c * CHUNK, CHUNK), CHUNK)
            q, _, _, _, k, lf = _hgrn_gates(zq_ref[rows, :], zf_ref[rows, :], lb)
            v = zi_ref[rows, :]
            vb = v.astype(BF16)
            b = _scan_rows(lf, row, False)
            sc = jnp.zeros((CHUNK, CHUNK), F32)
            for e, mask in zip(_level_scales(b), masks):
                sc = sc + jnp.where(mask, _dot_nt((q * e).astype(BF16), (k * e).astype(BF16)), 0.0)
            o = _dot(sc.astype(BF16), vb) + jnp.sum(q * k, axis=-1, keepdims=True) * v
            for d in range(1, DIAG):
                w = jnp.where(rd >= d, q * _group_roll(k, d) * jnp.exp(b - _group_roll(b, d)), 0.0)
                o = o + jnp.sum(w, axis=-1, keepdims=True) * _group_roll(v, d)
            b_last = b[CHUNK - 1:CHUNK, :]
            kd = (k * jnp.exp(b_last - b)).astype(BF16)
            qd = (q * jnp.exp(b)).astype(BF16)
            st_ref[c, 0] = st
            o_ref[rows, :] = o + _dot_nt(qd, st.astype(BF16))
            return st * jnp.exp(b_last) + _dot_tn(vb, kd)

        def group(i, st):
            for j in range(UNROLL):
                st = chunk(i * UNROLL + j, st)
            return st

        state[...] = lax.fori_loop(0, ncb // UNROLL, group, state[...])
        finish()

    in_specs, out_specs, out_shape, scratch, extra = _carried_specs(
        carry, in_specs,
        [pl.BlockSpec((rb, HGRN_DK), lambda h, r: (r, h)),
         pl.BlockSpec((ncb, 1, HGRN_DK, HGRN_DK), lambda h, r: (r, h, 0, 0))],
        [jax.ShapeDtypeStruct((T, D_MODEL), F32),
         jax.ShapeDtypeStruct((T // CHUNK, HGRN_HEADS, HGRN_DK, HGRN_DK), F32)],
        [pltpu.VMEM((HGRN_DK, HGRN_DK), F32)])
    return pl.pallas_call(
        body, name="hgrn_fwd", grid=(HGRN_HEADS, nr), in_specs=in_specs, out_specs=out_specs, out_shape=out_shape,
        scratch_shapes=scratch, compiler_params=_params(dimension_semantics=("arbitrary", "arbitrary")),
    )(z, z, z, lb_raw, *extra)


def _hgrn_bwd(z, lb_raw, states, do, rb=1024, carry=(None, None)):
    T = z.shape[0]
    rb = min(rb, T)
    ncb = rb // CHUNK
    nr, ri, in_specs = _hgrn_specs(T, rb, True)
    in_specs += [pl.BlockSpec((ncb, 1, HGRN_DK, HGRN_DK), lambda h, r: (ri(r), h, 0, 0)),
                 pl.BlockSpec((rb, HGRN_DK), lambda h, r: (ri(r), h))]

    def body(*refs):
        hh, rr = pl.program_id(0), pl.program_id(1)
        own, finish = _carried(carry, refs, 6, 4, (hh == 0) & (rr == 0), (hh == HGRN_HEADS - 1) & (rr == nr - 1))
        zq_ref, zf_ref, zi_ref, lb_ref, st_ref, do_ref, dq_ref, df_ref, di_ref, dlb_ref, dstate = own

        @pl.when(rr == 0)
        def _():
            dstate[...] = jnp.zeros_like(dstate)
            dlb_ref[...] = jnp.zeros_like(dlb_ref)

        lb = _lower_bound(lb_ref)
        row = lax.broadcasted_iota(jnp.int32, (CHUNK, HGRN_DK), 0)
        masks = _level_masks()
        rd = row & (DIAG - 1)

        def chunk(ci, dlb):
            c = ncb - 1 - ci
            rows = pl.ds(pl.multiple_of(c * CHUNK, CHUNK), CHUNK)
            zq = zq_ref[rows, :]
            q, sq, sg, forget, k, lf = _hgrn_gates(zq, zf_ref[rows, :], lb)
            v = zi_ref[rows, :]
            dov = do_ref[rows, :]
            b = _scan_rows(lf, row, False)
            st = st_ref[c, 0]
            dst = dstate[...]
            b_last = b[CHUNK - 1:CHUNK, :]
            eb = jnp.exp(b)
            ebb = jnp.exp(b_last - b)
            e_last = jnp.exp(b_last)
            dob, vb, stb, dstb = dov.astype(BF16), v.astype(BF16), st.astype(BF16), dst.astype(BF16)
            dq = eb * _dot(dob, stb)
            dv = _dot_nt((k * ebb).astype(BF16), dstb)
            dk = ebb * _dot(vb, dstb)
            extra = e_last * jnp.sum(dst * st, axis=0, keepdims=True) + jnp.sum(k * dk, axis=0, keepdims=True)
            da = _dot_nt(dob, vb)
            sc = jnp.zeros((CHUNK, CHUNK), F32)
            for e, mask in zip(_level_scales(b), masks):
                qs, ks = (q * e).astype(BF16), (k * e).astype(BF16)
                dam = jnp.where(mask, da, 0.0).astype(BF16)
                dq = dq + e * _dot(dam, ks)
                dk = dk + e * _dot_tn(dam, qs)
                sc = sc + jnp.where(mask, _dot_nt(qs, ks), 0.0)
            dv = dv + _dot_tn(sc.astype(BF16), dob)
            dad = jnp.sum(dov * v, axis=-1, keepdims=True)
            dq = dq + dad * k
            dk = dk + dad * q
            dv = dv + jnp.sum(q * k, axis=-1, keepdims=True) * dov
            for d in range(1, DIAG):
                w = jnp.where(rd >= d, jnp.exp(b - _group_roll(b, d)), 0.0)
                kr = _group_roll(k, d)
                dad = jnp.sum(dov * _group_roll(v, d), axis=-1, keepdims=True)
                ad = jnp.sum(q * kr * w, axis=-1, keepdims=True)
                dq = dq + dad * kr * w
                dk = dk + _group_roll(dad * q * w, -d)
                dv = dv + _group_roll(ad * dov, -d)
            dlf = _scan_rows(q * dq - k * dk, row, True) + extra
            dstate[...] = dst * e_last + _dot_tn(dob, (q * eb).astype(BF16))
            dforget = dlf / forget - dk
            dq_ref[rows, :] = (dq * (sq * (1.0 + zq * (1.0 - sq)))).astype(BF16)
            df_ref[rows, :] = (dforget * (1.0 - lb) * sg * (1.0 - sg)).astype(BF16)
            di_ref[rows, :] = dv.astype(BF16)
            return dlb + jnp.sum(dforget * (1.0 - sg), axis=0, keepdims=True)

        def group(i, dlb):
            for j in range(UNROLL_BWD):
                dlb = chunk(i * UNROLL_BWD + j, dlb)
            return dlb

        dlb_ref[...] += lax.fori_loop(0, ncb // UNROLL_BWD, group, jnp.zeros((1, HGRN_DK), F32))
        finish()

    blk = pl.BlockSpec((rb, HGRN_DK), lambda h, r: (ri(r), h))
    in_specs, out_specs, out_shape, scratch, extra = _carried_specs(
        carry, in_specs, [blk, blk, blk, pl.BlockSpec((1, HGRN_DK), lambda h, r: (0, h))],
        [jax.ShapeDtypeStruct((T, D_MODEL), BF16)] * 3 + [jax.ShapeDtypeStruct((1, D_MODEL), F32)],
        [pltpu.VMEM((HGRN_DK, HGRN_DK), F32)])
    return pl.pallas_call(
        body, name="hgrn_bwd", grid=(HGRN_HEADS, nr), in_specs=in_specs, out_specs=out_specs, out_shape=out_shape,
        scratch_shapes=scratch, compiler_params=_params(dimension_semantics=("arbitrary", "arbitrary")),
    )(z, z, z, lb_raw, states, do, *extra)


MESH = pl.DeviceIdType.MESH
ANY = pl.BlockSpec(memory_space=pl.ANY)


def _place():
    return lax.axis_index("x"), lax.axis_index("y"), lax.axis_index("c")


def _sems(n):
    return [pltpu.SemaphoreType.DMA((7 * n,)), pltpu.SemaphoreType.DMA((7 * n,)), pltpu.SemaphoreType.DMA((n,))]


class _Gather:
    def __init__(self, x_ref, out_ref, send_sems, recv_sems, local_sems, idx):
        self.x_ref, self.out_ref, self.send_sems, self.recv_sems, self.local_sem, self.base = (
            x_ref, out_ref, send_sems, recv_sems, local_sems.at[idx], 7 * idx)
        x, y, c = _place()
        self.c = c
        self.me, self.sibling = (x, y, c), (x, y, 1 - c)
        self.chips = [(1 - x, y), (x, 1 - y), (1 - x, 1 - y)]

    def rows(self, px, py, pc):
        return self.out_ref.at[4 * px + 2 * py + pc]

    def copy(self, k, block, to, from_input=False):
        return pltpu.make_async_remote_copy(
            src_ref=self.x_ref if from_input else self.rows(*block), dst_ref=self.rows(*block),
            send_sem=self.send_sems.at[self.base + k], recv_sem=self.recv_sems.at[self.base + k], device_id=to,
            device_id_type=MESH)

    def first(self):
        out = [self.copy(0, self.me, self.sibling, from_input=True)]
        return out + [self.copy(1 + j, self.me, (*chip, self.c), from_input=True) for j, chip in enumerate(self.chips)]

    def start(self):
        pltpu.make_async_copy(self.x_ref, self.rows(*self.me), self.local_sem).start()
        for cp in self.first():
            cp.start()

    def finish(self):
        passed = [self.copy(4 + j, (*chip, self.c), self.sibling) for j, chip in enumerate(self.chips)]
        for j, chip in enumerate(self.chips):
            self.copy(1 + j, (*chip, self.c), self.me).wait_recv()
            passed[j].start()
        self.copy(0, self.sibling, self.me).wait_recv()
        for j, chip in enumerate(self.chips):
            self.copy(4 + j, (*chip, 1 - self.c), self.me).wait_recv()
        for cp in self.first() + passed:
            cp.wait_send()
        pltpu.make_async_copy(self.x_ref, self.rows(*self.me), self.local_sem).wait()


class _Many:
    def __init__(self, kind, in_refs, out_refs, send_sems, recv_sems, local_sems):
        self.ops = [kind(x, o, send_sems, recv_sems, local_sems, i) for i, (x, o) in enumerate(zip(in_refs, out_refs))]

    def start(self):
        for op in self.ops:
            op.start()

    def finish(self):
        for op in self.ops:
            op.finish()


def _result_shapes(kind, arrs):
    return [jax.ShapeDtypeStruct(a.shape if kind is _Exchange else (N_DEV,) + a.shape, a.dtype) for a in arrs]


def _all_gather(name, shards):
    n = len(shards)

    def body(*refs):
        g = _Many(_Gather, refs[:n], refs[n:2 * n], *refs[2 * n:])
        g.start()
        g.finish()

    return pl.pallas_call(
        body, name=name, out_shape=_result_shapes(_Gather, shards), in_specs=[ANY] * n, out_specs=[ANY] * n,
        scratch_shapes=_sems(n),
    )(*shards)


def _peers(x, y, c):
    out = []
    for k in range(1, N_DEV):
        px = 1 - x if k & 4 else x
        py = 1 - y if k & 2 else y
        pc = 1 - c if k & 1 else c
        out.append((k, (px, py, pc), 4 * px + 2 * py + pc))
    return out


class _Exchange:
    def __init__(self, g_ref, recv_ref, send_sems, recv_sems, local_sems, idx):
        x, y, c = _place()
        me = 4 * x + 2 * y + c
        self.local = pltpu.make_async_copy(g_ref.at[me], recv_ref.at[me], local_sems.at[idx])
        self.copies = [
            pltpu.make_async_remote_copy(
                src_ref=g_ref.at[pidx], dst_ref=recv_ref.at[me], send_sem=send_sems.at[7 * idx + k - 1],
                recv_sem=recv_sems.at[7 * idx + k - 1], device_id=peer, device_id_type=MESH)
            for k, peer, pidx in _peers(x, y, c)]

    def start(self):
        self.local.start()
        for cp in self.copies:
            cp.start()

    def finish(self):
        for cp in self.copies:
            cp.wait()
        self.local.wait()


def _carried(carry, refs, n_in, n_out, first, last):
    kind, arrs = carry
    if kind is None:
        return refs, lambda: None
    n = len(arrs)
    ins, rest = refs[:n_in], refs[n_in + n:]
    outs, scratch = rest[:n_out], rest[n_out + n:]
    op = _Many(kind, refs[n_in:n_in + n], rest[n_out:n_out + n], *scratch[len(scratch) - 3:])

    @pl.when(first)
    def _():
        op.start()

    def finish():
        @pl.when(last)
        def _():
            op.finish()

    return tuple(ins) + tuple(outs) + tuple(scratch[:len(scratch) - 3]), finish


def _carried_specs(carry, in_specs, out_specs, out_shape, scratch):
    kind, arrs = carry
    if kind is None:
        return in_specs, out_specs, out_shape, scratch, []
    n = len(arrs)
    return (list(in_specs) + [ANY] * n, list(out_specs) + [ANY] * n,
            list(out_shape) + _result_shapes(kind, arrs), list(scratch) + _sems(n), list(arrs))


def _adamw(w, g, m, v):
    m = ADAM_B1 * m + (1.0 - ADAM_B1) * g
    v = ADAM_B2 * v + (1.0 - ADAM_B2) * (g * g)
    m_hat = m / (1.0 - ADAM_B1 ** ADAM_STEP)
    v_hat = v / (1.0 - ADAM_B2 ** ADAM_STEP)
    delta = -ADAM_LR * (m_hat / (jnp.sqrt(v_hat) + ADAM_EPS) + ADAM_WD * w)
    return delta, m, v


def _adamw_sum(name, recvs, w, m, v):
    L, R, C = w.shape
    tm = 128 if R % 128 == 0 else 64
    assert R % tm == 0 and len(recvs) == L

    def body(*refs):
        r_refs, (w_ref, m_ref, v_ref, g_ref, d_ref, nm_ref, nv_ref) = refs[:L], refs[L:]
        for l in range(L):
            g = r_refs[l][0].astype(F32)
            for s in range(1, N_DEV):
                g = g + r_refs[l][s].astype(F32)
            g_ref[l] = g
            d_ref[l], nm_ref[l], nv_ref[l] = _adamw(w_ref[l], g, m_ref[l], v_ref[l])

    blk = pl.BlockSpec((L, tm, C), lambda i: (0, i, 0))
    return pl.pallas_call(
        body, name=name, grid=(R // tm,),
        in_specs=[pl.BlockSpec((N_DEV, tm, C), lambda i: (0, i, 0))] * L + [blk, blk, blk],
        out_specs=[blk] * 4, out_shape=[jax.ShapeDtypeStruct((L, R, C), F32)] * 4,
        compiler_params=_params(dimension_semantics=("arbitrary",)),
    )(*recvs, w, m, v)


def _small_sync(part, w, m, v):
    def body(p_ref, w_ref, m_ref, v_ref, g_ref, d_ref, nm_ref, nv_ref, gath, send_sems, recv_sems):
        x, y, c = _place()
        me = 4 * x + 2 * y + c
        gath[me] = p_ref[...]
        copies = []
        for k, peer, _ in _peers(x, y, c):
            cp = pltpu.make_async_remote_copy(
                src_ref=p_ref, dst_ref=gath.at[me], send_sem=send_sems.at[k - 1], recv_sem=recv_sems.at[k - 1],
                device_id=peer, device_id_type=MESH)
            cp.start()
            copies.append(cp)
        for cp in copies:
            cp.wait()
        g = gath[0]
        for s in range(1, N_DEV):
            g = g + gath[s]
        wv = w_ref[...]
        l0, l1 = w_ref[8:9, :], w_ref[9:10, :]
        mx = jnp.maximum(l0, l1)
        e0, e1 = jnp.exp(l0 - mx), jnp.exp(l1 - mx)
        g9 = g[9:10, :] * (e0 / (e0 + e1)) * (e1 / (e0 + e1))
        row = lax.broadcasted_iota(jnp.int32, g.shape, 0)
        g = jnp.where(row == 9, g9, jnp.where(row == 8, -g9, g))
        g_ref[...] = g
        d_ref[...], nm_ref[...], nv_ref[...] = _adamw(wv, g, m_ref[...], v_ref[...])

    vm = pl.BlockSpec(memory_space=pltpu.VMEM)
    return pl.pallas_call(
        body, name="small_params_sync", in_specs=[vm] * 4, out_specs=[vm] * 4,
        out_shape=[jax.ShapeDtypeStruct(part.shape, F32)] * 4,
        scratch_shapes=[pltpu.VMEM((N_DEV,) + part.shape, F32), pltpu.SemaphoreType.DMA((7,)),
                        pltpu.SemaphoreType.DMA((7,))],
    )(part, w, m, v)


def _shards_bf16(d, pieces):
    return [d[name][layer].astype(BF16) for name, layer in pieces]


def _gathered(arrs, pieces, out):
    for a, (name, layer) in zip(arrs, pieces):
        out[name, layer] = a if name in COL_SHARDED else a.reshape(N_DEV * a.shape[1], a.shape[2])


def _pad_row(a, width=D_MODEL):
    a = a.reshape(1, -1)
    return jnp.pad(a, ((0, 0), (0, width - a.shape[1])))


def _pack_small(d, gn_full):
    rows = [d["mix_norm"], d["mlp_norm"], d["final_norm"].reshape(1, D_MODEL),
            _pad_row(d["attn_b_qkv"], 2 * D_MODEL).reshape(2, D_MODEL), _pad_row(d["attn_sinks"]),
            d["hgrn_lower_bounds"], gn_full.reshape(1, D_MODEL)]
    p = jnp.concatenate(rows, axis=0)
    return jnp.pad(p, ((0, SMALL_ROWS - p.shape[0]), (0, 0)))


def _unpack_small(p, me):
    return dict(
        mix_norm=p[0:2], mlp_norm=p[2:4], final_norm=p[4],
        attn_b_qkv=p[5:7].reshape(1, 2 * D_MODEL)[:, :QKV_DIM], attn_sinks=p[7:8, :N_Q_HEADS],
        hgrn_lower_bounds=p[8:10], hgrn_g_norm=lax.dynamic_slice(p[10:11], (0, me * 128), (1, 128)))


WEIGHT_NAMES = ['mix_norm', 'mlp_norm', 'final_norm', 'attn_w_qkv', 'attn_b_qkv', 'attn_sinks', 'attn_w_o', 'hgrn_w_in',
                'hgrn_g_norm', 'hgrn_w_o', 'hgrn_lower_bounds', 'mlp_w_up', 'mlp_w_down']
SMALL_NAMES = ('mix_norm', 'mlp_norm', 'final_norm', 'attn_b_qkv', 'attn_sinks', 'hgrn_lower_bounds', 'hgrn_g_norm')


def _rotary_tables(positions):
    inv_freq = ROPE_THETA ** (-jnp.arange(0, 2 * ROT_HALF, 2, dtype=F32) / (2 * ROT_HALF))
    ang = positions.astype(F32).reshape(-1, 1) * inv_freq
    cos, sin = jnp.cos(ang), jnp.sin(ang)
    r = jnp.arange(LANES) % HEAD_DIM
    idx = r % ROT_HALF
    c = jnp.where(r < 2 * ROT_HALF, cos[:, idx], 1.0)
    sa = jnp.where((r >= ROT_HALF) & (r < 2 * ROT_HALF), sin[:, idx], 0.0)
    sb = jnp.where(r < ROT_HALF, -sin[:, idx], 0.0)
    return jnp.concatenate([c, sa, sb], axis=1)


def kernel(x, positions, mix_norm, mlp_norm, final_norm, attn_w_qkv, attn_b_qkv, attn_sinks, attn_w_o, hgrn_w_in, hgrn_g_norm, hgrn_w_o, hgrn_lower_bounds, mlp_w_up, mlp_w_down, loss_target, m_mix_norm, m_mlp_norm, m_final_norm, m_attn_w_qkv, m_attn_b_qkv, m_attn_sinks, m_attn_w_o, m_hgrn_w_in, m_hgrn_g_norm, m_hgrn_w_o, m_hgrn_lower_bounds, m_mlp_w_up, m_mlp_w_down, v_mix_norm, v_mlp_norm, v_final_norm, v_attn_w_qkv, v_attn_b_qkv, v_attn_sinks, v_attn_w_o, v_hgrn_w_in, v_hgrn_g_norm, v_hgrn_w_o, v_hgrn_lower_bounds, v_mlp_w_up, v_mlp_w_down):
    w = dict(mix_norm=mix_norm, mlp_norm=mlp_norm, final_norm=final_norm, attn_w_qkv=attn_w_qkv, attn_b_qkv=attn_b_qkv,
             attn_sinks=attn_sinks, attn_w_o=attn_w_o, hgrn_w_in=hgrn_w_in, hgrn_g_norm=hgrn_g_norm, hgrn_w_o=hgrn_w_o,
             hgrn_lower_bounds=hgrn_lower_bounds, mlp_w_up=mlp_w_up, mlp_w_down=mlp_w_down)
    m = dict(mix_norm=m_mix_norm, mlp_norm=m_mlp_norm, final_norm=m_final_norm, attn_w_qkv=m_attn_w_qkv,
             attn_b_qkv=m_attn_b_qkv, attn_sinks=m_attn_sinks, attn_w_o=m_attn_w_o, hgrn_w_in=m_hgrn_w_in,
             hgrn_g_norm=m_hgrn_g_norm, hgrn_w_o=m_hgrn_w_o, hgrn_lower_bounds=m_hgrn_lower_bounds, mlp_w_up=m_mlp_w_up,
             mlp_w_down=m_mlp_w_down)
    v = dict(mix_norm=v_mix_norm, mlp_norm=v_mlp_norm, final_norm=v_final_norm, attn_w_qkv=v_attn_w_qkv,
             attn_b_qkv=v_attn_b_qkv, attn_sinks=v_attn_sinks, attn_w_o=v_attn_w_o, hgrn_w_in=v_hgrn_w_in,
             hgrn_g_norm=v_hgrn_g_norm, hgrn_w_o=v_hgrn_w_o, hgrn_lower_bounds=v_hgrn_lower_bounds, mlp_w_up=v_mlp_w_up,
             mlp_w_down=v_mlp_w_down)
    me = 4 * lax.axis_index("x") + 2 * lax.axis_index("y") + lax.axis_index("c")

    gn = hgrn_g_norm.reshape(1, 128)
    gn_a = gn.astype(BF16)
    gn_b = (gn - gn_a.astype(F32)).astype(BF16)
    gn_c = (gn - gn_a.astype(F32) - gn_b.astype(F32)).astype(BF16)
    gn_rows = jnp.pad(jnp.concatenate([gn_a, gn_b, gn_c], axis=1), ((0, 15), (0, D_MODEL - 3 * 128)))
    full = {}
    got = _all_gather("gather_attn_weights", _shards_bf16(w, GATHER_FIRST) + [gn_rows])
    _gathered(got[:1], GATHER_FIRST, full)
    w_qkv = full["attn_w_qkv", 0].transpose(1, 0, 2).reshape(D_MODEL, QKV_DIM)
    gn_terms = got[1][:, 0, :3 * 128].astype(F32).reshape(N_DEV, 3, 128)
    gn_full = ((gn_terms[:, 0] + gn_terms[:, 1]) + gn_terms[:, 2]).reshape(1, D_MODEL)

    x0 = x[0]
    tgt = loss_target[0]
    rot = _rotary_tables(positions)
    row = lambda a: a.reshape(1, -1)

    qkv, h0 = _norm_mm("qkv_proj", x0, row(mix_norm[0]), w_qkv, attn_b_qkv, rot=rot)
    att, *got = _attn_fwd(qkv, attn_sinks, carry=(_Gather, _shards_bf16(w, GATHER_ATTN)))
    _gathered(got, GATHER_ATTN, full)
    x1 = _mm_res("attn_out_proj", att, full["attn_w_o", 0], x0)
    u0, h1, *got = _norm_mm("mlp0_up", x1, row(mlp_norm[0]), full["mlp_w_up", 0],
                            carry=(_Gather, _shards_bf16(w, GATHER_MLP0)))
    _gathered(got, GATHER_MLP0, full)
    x2, a0 = _mlp_down("mlp0_down", u0, full["mlp_w_down", 0], x1)
    z, h2 = _norm_mm("hgrn_in_proj", x2, row(mix_norm[1]), full["hgrn_w_in", 0])
    o_raw, states, *got = _hgrn_fwd(z, hgrn_lower_bounds, carry=(_Gather, _shards_bf16(w, GATHER_HGRN)))
    _gathered(got, GATHER_HGRN, full)
    x3, o2 = _hgrn_out("hgrn_out_proj", o_raw, z, gn_full, full["hgrn_w_o", 0], x2)
    u1, h3 = _norm_mm("mlp1_up", x3, row(mlp_norm[1]), full["mlp_w_up", 1])
    x4, a1 = _mlp_down("mlp1_down", u1, full["mlp_w_down", 1], x3)
    dx4, loss_part, g_final = _loss_head("loss_head", x4, tgt, row(final_norm))

    gw = {}
    du1 = _mlp_bwd_act("mlp1_bwd_act", dx4, u1, full["mlp_w_down", 1])
    dx3, g_mlp1 = _mm_nt_rmsbwd("mlp1_bwd_in", du1, full["mlp_w_up", 1], x3, row(mlp_norm[1]), dx4)
    gw["mlp_w_down", 1] = _mm_tn("mlp1_dw_down", a1, dx4, "rows")
    gw["mlp_w_up", 1] = _mm_tn("mlp1_dw_up", h3, du1, "cols")

    do_raw, dg, g_gn = _hgrn_out_bwd("hgrn_out_bwd", dx3, o_raw, z, full["hgrn_w_o", 0], gn_full)
    gw["hgrn_w_o", 0] = _mm_tn("hgrn_dw_o", o2, dx3, "rows")
    recvs = {}
    dzq, dzf, dzi, g_lb, *recv = _hgrn_bwd(z, hgrn_lower_bounds, states, do_raw,
                                           carry=(_Exchange, [gw[p] for p in GRAD_GROUPS[0]]))
    recvs.update(zip(GRAD_GROUPS[0], recv))
    dz = [dzq, dzf, dzi, dg]
    dx2, g_mix1 = _mm_nt_rmsbwd("hgrn_in_bwd", dz, full["hgrn_w_in", 0], x2, row(mix_norm[1]), dx3)
    gw["hgrn_w_in", 0] = jnp.concatenate(
        [_mm_tn(f"hgrn_dw_in{j}", h2, d, "cols") for j, d in enumerate(dz)], axis=0)

    du0 = _mlp_bwd_act("mlp0_bwd_act", dx2, u0, full["mlp_w_down", 0])
    dx1, g_mlp0 = _mm_nt_rmsbwd("mlp0_bwd_in", du0, full["mlp_w_up", 0], x1, row(mlp_norm[0]), dx2)
    gw["mlp_w_down", 0] = _mm_tn("mlp0_dw_down", a0, dx2, "rows")
    gw["mlp_w_up", 0] = _mm_tn("mlp0_dw_up", h1, du0, "cols")

    datt = _mm_nt("attn_out_bwd", dx1, full["attn_w_o", 0], BF16)
    gw["attn_w_o", 0] = _mm_tn("attn_dw_o", att, dx1, "rows")
    dqkv, g_sink, *recv = _attn_bwd(qkv, rot, attn_sinks, datt, carry=(_Exchange, [gw[p] for p in GRAD_GROUPS[1]]))
    recvs.update(zip(GRAD_GROUPS[1], recv))
    g_qkv = _mm_tn("attn_dw_qkv", h0, dqkv)
    g_qkv = g_qkv.reshape(D_MODEL, N_DEV, QKV_DIM // N_DEV).transpose(1, 0, 2).astype(BF16)
    dx0, g_mix0, g_bqkv, recvs["attn_w_qkv", 0] = _mm_nt_rmsbwd(
        "qkv_bwd", dqkv, w_qkv, x0, row(mix_norm[0]), dx1, with_colsum=True, carry=(_Exchange, [g_qkv]))

    big = {name: _adamw_sum("adamw_" + name, [recvs[name, l] for l in range(w[name].shape[0])], w[name], m[name], v[name])
           for name in BIG_NAMES}

    zero_row = jnp.zeros((1, D_MODEL), F32)
    part = _pack_small(dict(
        mix_norm=jnp.concatenate([g_mix0, g_mix1], axis=0), mlp_norm=jnp.concatenate([g_mlp0, g_mlp1], axis=0),
        final_norm=g_final, attn_b_qkv=g_bqkv, attn_sinks=g_sink[:, :N_Q_HEADS],
        hgrn_lower_bounds=jnp.concatenate([zero_row, g_lb], axis=0)), g_gn)

    def spread(a):
        return lax.dynamic_update_slice(zero_row, a.reshape(1, 128), (0, me * 128))

    small_in = [_pack_small({n: d[n] for n in SMALL_NAMES if n != "hgrn_g_norm"}, spread(d["hgrn_g_norm"]))
                for d in (w, m, v)]
    small = [_unpack_small(p, me) for p in _small_sync(part, *small_in)]

    loss = lax.psum(loss_part[0, 0], ("x", "y", "c"))
    outs = [loss, dx0.reshape(x.shape)]
    for kind, grp_small in enumerate(small):
        for name in WEIGHT_NAMES:
            val = grp_small[name] if name in SMALL_NAMES else big[name][kind]
            outs.append(val.reshape(w[name].shape))
    return tuple(outs)
```

```python
import functools

import jax
import jax.numpy as jnp
from jax import lax
from jax.experimental import pallas as pl
from jax.experimental.pallas import tpu as pltpu

F32 = jnp.float32
BF16 = jnp.bfloat16

D_MODEL = 1024
HEAD_DIM = 64
N_Q_HEADS = 16
Q_DIM = 1024
KV_DIM = 256
QKV_DIM = 1536
ATT_BLOCK = 128
ROT_HALF = 8
ROPE_THETA = 500000.0
NEG_INF = -1e30
HGRN_HEADS = 8
HGRN_DK = 128
CHUNK = 64
D_FF = 4096
NORM_EPS = 1e-5
N_DEV = 8

ADAM_LR = 0.001
ADAM_B1 = 0.9
ADAM_B2 = 0.999
ADAM_EPS = 1e-08
ADAM_WD = 0.01
ADAM_STEP = 10

LANES = 128
VMEM_LIMIT = 56 * 1024 * 1024

GATHER_FIRST = (("attn_w_qkv", 0),)
GATHER_ATTN = (("attn_w_o", 0), ("mlp_w_up", 0), ("mlp_w_down", 0))
GATHER_MLP0 = (("hgrn_w_in", 0), ("hgrn_w_o", 0))
GATHER_HGRN = (("mlp_w_up", 1), ("mlp_w_down", 1))
GRAD_GROUPS = ((("mlp_w_down", 1), ("mlp_w_up", 1), ("hgrn_w_o", 0)),
               (("hgrn_w_in", 0), ("mlp_w_down", 0), ("mlp_w_up", 0), ("attn_w_o", 0)),
               (("attn_w_qkv", 0),))
COL_SHARDED = ("attn_w_qkv", "hgrn_w_in", "mlp_w_up")
BIG_NAMES = ("attn_w_qkv", "attn_w_o", "hgrn_w_in", "hgrn_w_o", "mlp_w_up", "mlp_w_down")
SMALL_ROWS = 16


def _dot(a, b):
    return jnp.dot(a, b, preferred_element_type=F32)


def _dot_nt(a, b):
    return lax.dot_general(a, b, (((1,), (1,)), ((), ())), preferred_element_type=F32)


def _dot_tn(a, b):
    return lax.dot_general(a, b, (((0,), (0,)), ((), ())), preferred_element_type=F32)


def _params(**kw):
    return pltpu.CompilerParams(vmem_limit_bytes=VMEM_LIMIT, **kw)


def _full_spec(a):
    nd = a.ndim
    return pl.BlockSpec(a.shape, lambda *_: (0,) * nd)


def _row_call(name, body, n_rows, tm, row_ins, full_ins, row_outs, acc_outs=(), carry=(None, None)):
    steps = n_rows // tm
    in_specs = [pl.BlockSpec((tm, w), functools.partial(lambda i, cb: (i, cb), cb=cb)) for _, w, cb in row_ins]
    in_specs += [_full_spec(a) for a in full_ins]
    out_shape = [jax.ShapeDtypeStruct((n_rows, w), dt) for w, dt in row_outs]
    out_specs = [pl.BlockSpec((tm, w), lambda i: (i, 0)) for w, _ in row_outs]
    for shp, dt in acc_outs:
        out_shape.append(jax.ShapeDtypeStruct(shp, dt))
        out_specs.append(pl.BlockSpec(shp, functools.partial(lambda i, nd: (0,) * nd, nd=len(shp))))
    n_in, n_out = len(in_specs), len(out_specs)
    in_specs, out_specs, out_shape, scratch, extra = _carried_specs(carry, in_specs, out_specs, out_shape, [])

    def wrapped(*refs):
        i = pl.program_id(0)
        own, finish = _carried(carry, refs, n_in, n_out, i == 0, i == steps - 1)
        body(*own)
        finish()

    return pl.pallas_call(
        wrapped, name=name, grid=(steps,), in_specs=in_specs, out_specs=out_specs, out_shape=out_shape,
        scratch_shapes=scratch, compiler_params=_params(dimension_semantics=("arbitrary",)),
    )(*[a for a, _, _ in row_ins], *full_ins, *extra)


def _rms(x, gain):
    r = lax.rsqrt(jnp.mean(x * x, axis=-1, keepdims=True) + NORM_EPS)
    xhat = x * r
    return xhat * gain, xhat, r


def _rms_bwd(dy, xhat, r, gain):
    dxhat = dy * gain
    dx = r * (dxhat - xhat * jnp.mean(dxhat * xhat, axis=-1, keepdims=True))
    return dx, dy * xhat


def _norm_mm(name, x, gain, w, bias=None, rot=None, tm=256, carry=(None, None)):
    T = x.shape[0]
    tm = min(tm, T)
    nc = 512
    blocked = w.ndim == 3
    n = N_DEV * w.shape[2] if blocked else w.shape[1]
    assert n % nc == 0 and (not blocked or w.shape[2] == nc)

    def body(*refs):
        x_ref, refs = refs[0], refs[1:]
        if rot is not None:
            t_ref, refs = refs[0], refs[1:]
        g_ref, w_ref, refs = refs[0], refs[1], refs[2:]
        if bias is not None:
            b_ref, refs = refs[0], refs[1:]
        y_ref, h_ref = refs
        h, _, _ = _rms(x_ref[...], g_ref[...])
        hb = h.astype(BF16)
        h_ref[...] = hb
        for c in range(n // nc):
            sl = slice(c * nc, (c + 1) * nc)
            y = _dot(hb, w_ref[c] if blocked else w_ref[:, sl])
            if bias is not None:
                y = y + b_ref[:, sl]
            if rot is None:
                y_ref[:, sl] = y
            else:
                n_rot = max(0, min(nc, Q_DIM + KV_DIM - c * nc)) // LANES
                pieces = _rot_fwd(y[:, :n_rot * LANES], t_ref[...]) if n_rot else []
                for j in range(nc // LANES):
                    col = slice(c * nc + j * LANES, c * nc + (j + 1) * LANES)
                    y_ref[:, col] = pieces[j] if j < n_rot else y[:, j * LANES:(j + 1) * LANES]

    rows = [(x, D_MODEL, 0)] + ([(rot, 3 * LANES, 0)] if rot is not None else [])
    full = [gain, w] + ([bias] if bias is not None else [])
    return _row_call(name, body, T, tm, rows, full, [(n, F32), (D_MODEL, BF16)], carry=carry)


def _mm_res(name, a, w, res, tm=512):
    T = a.shape[0]
    tm = min(tm, T)

    def body(a_ref, r_ref, w_ref, o_ref):
        o_ref[...] = r_ref[...] + _dot(a_ref[...], w_ref[...])

    return _row_call(name, body, T, tm, [(a, a.shape[1], 0), (res, D_MODEL, 0)], [w], [(D_MODEL, F32)])[0]


def _mlp_down(name, u, w, res, tm=256):
    T = u.shape[0]
    tm = min(tm, T)
    kc = 1024

    def body(u_ref, r_ref, w_ref, o_ref, a_ref):
        acc = r_ref[...]
        for c in range(D_FF // kc):
            sl = slice(c * kc, (c + 1) * kc)
            a = jnp.maximum(u_ref[:, sl], 0.0)
            ab = (a * a).astype(BF16)
            a_ref[:, sl] = ab
            acc = acc + _dot(ab, w_ref[sl, :])
        o_ref[...] = acc

    return _row_call(name, body, T, tm, [(u, D_FF, 0), (res, D_MODEL, 0)], [w], [(D_MODEL, F32), (D_FF, BF16)])


def _hgrn_out(name, o_raw, z, gn, w, res, tm=256):
    T = o_raw.shape[0]
    tm = min(tm, T)

    def body(o_ref, g_ref, r_ref, gn_ref, w_ref, x_ref, a_ref):
        y, _, _ = _rms(o_ref[...], gn_ref[...])
        g = g_ref[...]
        a = (y * (g * jax.nn.sigmoid(g))).astype(BF16)
        a_ref[...] = a
        x_ref[...] = r_ref[...] + _dot(a, w_ref[...])

    return _row_call(name, body, T, tm, [(o_raw, D_MODEL, 0), (z, D_MODEL, 3), (res, D_MODEL, 0)], [gn, w],
                     [(D_MODEL, F32), (D_MODEL, BF16)])


def _loss_head(name, x, target, gain, tm=512):
    T = x.shape[0]
    tm = min(tm, T)

    def body(x_ref, t_ref, g_ref, dx_ref, loss_ref, dg_ref):
        @pl.when(pl.program_id(0) == 0)
        def _():
            loss_ref[...] = jnp.zeros_like(loss_ref)
            dg_ref[...] = jnp.zeros_like(dg_ref)

        gain_v = g_ref[...]
        y, xhat, r = _rms(x_ref[...], gain_v)
        diff = y - t_ref[...]
        row = jnp.sum(diff * diff, axis=-1, keepdims=True) * (1.0 / D_MODEL)
        loss_ref[...] += jnp.broadcast_to(0.5 * jnp.sum(row, axis=0, keepdims=True), loss_ref.shape)
        dy = diff * (1.0 / D_MODEL)
        dx, dgr = _rms_bwd(dy, xhat, r, gain_v)
        dx_ref[...] = dx
        dg_ref[...] += jnp.sum(dgr, axis=0, keepdims=True)

    return _row_call(name, body, T, tm, [(x, D_MODEL, 0), (target, D_MODEL, 0)], [gain], [(D_MODEL, F32)],
                     [((1, LANES), F32), ((1, D_MODEL), F32)])


def _mm_nt_rmsbwd(name, dy, w, x, gain, dres, tm=256, with_colsum=False, carry=(None, None)):
    T = x.shape[0]
    tm = min(tm, T)
    dys = list(dy) if isinstance(dy, (list, tuple)) else [dy]
    width = dys[0].shape[1]
    n = width * len(dys)
    assert not with_colsum or len(dys) == 1

    def body(*refs):
        dy_refs, refs = refs[:len(dys)], refs[len(dys):]
        if with_colsum:
            x_ref, dr_ref, w_ref, g_ref, dx_ref, dg_ref, cs_ref = refs
        else:
            x_ref, dr_ref, w_ref, g_ref, dx_ref, dg_ref = refs

        @pl.when(pl.program_id(0) == 0)
        def _():
            dg_ref[...] = jnp.zeros_like(dg_ref)
            if with_colsum:
                cs_ref[...] = jnp.zeros_like(cs_ref)

        if w.ndim == 3:
            nb = w.shape[2]
            dh = None
            for p in range(N_DEV):
                piece, off = divmod(p * nb, width)
                part = _dot_nt(dy_refs[piece][:, off:off + nb].astype(BF16), w_ref[p])
                dh = part if dh is None else dh + part
        else:
            dh = _dot_nt(dy_refs[0][...].astype(BF16), w_ref[...])
        gain_v = g_ref[...]
        _, xhat, r = _rms(x_ref[...], gain_v)
        dx, dgr = _rms_bwd(dh, xhat, r, gain_v)
        dx_ref[...] = dr_ref[...] + dx
        dg_ref[...] += jnp.sum(dgr, axis=0, keepdims=True)
        if with_colsum:
            cs_ref[...] += jnp.sum(dy_refs[0][...].astype(F32), axis=0, keepdims=True)

    acc = [((1, D_MODEL), F32)] + ([((1, n), F32)] if with_colsum else [])
    rows = [(d, width, 0) for d in dys] + [(x, D_MODEL, 0), (dres, D_MODEL, 0)]
    return _row_call(name, body, T, tm, rows, [w, gain], [(D_MODEL, F32)], acc, carry=carry)


def _mm_nt(name, dy, w, out_dtype, tm=512):
    T = dy.shape[0]
    tm = min(tm, T)
    k = w.shape[0]

    def body(dy_ref, w_ref, o_ref):
        o_ref[...] = _dot_nt(dy_ref[...].astype(BF16), w_ref[...]).astype(out_dtype)

    return _row_call(name, body, T, tm, [(dy, dy.shape[1], 0)], [w], [(k, out_dtype)])[0]


def _mlp_bwd_act(name, dy, u, w_down, tm=256):
    T = u.shape[0]
    tm = min(tm, T)
    kc = 1024

    def body(dy_ref, u_ref, w_ref, du_ref):
        dyb = dy_ref[...].astype(BF16)
        for c in range(D_FF // kc):
            sl = slice(c * kc, (c + 1) * kc)
            da = _dot_nt(dyb, w_ref[sl, :])
            du_ref[:, sl] = (da * (2.0 * jnp.maximum(u_ref[:, sl], 0.0))).astype(BF16)

    return _row_call(name, body, T, tm, [(dy, D_MODEL, 0), (u, D_FF, 0)], [w_down], [(D_FF, BF16)])[0]


def _hgrn_out_bwd(name, dx, o_raw, z, w, gn, tm=256):
    T = dx.shape[0]
    tm = min(tm, T)

    def body(dx_ref, o_ref, g_ref, w_ref, gn_ref, do_ref, dg_ref, dgn_ref):
        @pl.when(pl.program_id(0) == 0)
        def _():
            dgn_ref[...] = jnp.zeros_like(dgn_ref)

        da = _dot_nt(dx_ref[...].astype(BF16), w_ref[...])
        gn_v = gn_ref[...]
        y, xhat, r = _rms(o_ref[...], gn_v)
        g = g_ref[...]
        sg = jax.nn.sigmoid(g)
        dg_ref[...] = (da * y * (sg * (1.0 + g * (1.0 - sg)))).astype(BF16)
        dyn = da * (g * sg)
        do, dgr = _rms_bwd(dyn, xhat, r, gn_v)
        do_ref[...] = do
        dgn_ref[...] += jnp.sum(dgr, axis=0, keepdims=True)

    return _row_call(name, body, T, tm, [(dx, D_MODEL, 0), (o_raw, D_MODEL, 0), (z, D_MODEL, 3)], [w, gn],
                     [(D_MODEL, F32), (D_MODEL, BF16)], [((1, D_MODEL), F32)])


def _mm_tn(name, a, b, shard=None, bm=1024, bn=512, tk=2048):
    T, M = a.shape
    N = b.shape[1]
    bm, bn, tk = min(bm, M), min(bn, N), min(tk, T)
    nk = T // tk
    if shard is None:
        out_shape, out_block = jax.ShapeDtypeStruct((M, N), F32), (bm, bn)
        out_map = lambda i, j, k: (i, j)
    elif shard == "cols":
        assert N % bn == 0
        out_shape, out_block = jax.ShapeDtypeStruct((N // bn, M, bn), BF16), (1, bm, bn)
        out_map = lambda i, j, k: (j, i, 0)
    else:
        rows = M // N_DEV
        assert bm % rows == 0
        out_shape, out_block = jax.ShapeDtypeStruct((N_DEV, rows, N), BF16), (bm // rows, rows, bn)
        out_map = lambda i, j, k: (i, 0, j)

    def body(a_ref, b_ref, o_ref, acc):
        k = pl.program_id(2)

        @pl.when(k == 0)
        def _():
            acc[...] = jnp.zeros_like(acc)

        acc[...] += _dot_tn(a_ref[...].astype(BF16), b_ref[...].astype(BF16))

        @pl.when(k == nk - 1)
        def _():
            o_ref[...] = acc[...].reshape(out_block).astype(o_ref.dtype)

    return pl.pallas_call(
        body, name=name, grid=(M // bm, N // bn, nk),
        in_specs=[pl.BlockSpec((tk, bm), lambda i, j, k: (k, i)), pl.BlockSpec((tk, bn), lambda i, j, k: (k, j))],
        out_specs=pl.BlockSpec(out_block, out_map), out_shape=out_shape,
        scratch_shapes=[pltpu.VMEM((bm, bn), F32)],
        compiler_params=_params(dimension_semantics=("parallel", "parallel", "arbitrary")),
    )(a, b)


def _rot_fwd(x, tab):
    c, sa, sb = tab[:, :LANES], tab[:, LANES:2 * LANES], tab[:, 2 * LANES:]
    outs = []
    for j in range(x.shape[1] // LANES):
        xs = x[:, j * LANES:(j + 1) * LANES]
        outs.append(xs * c + pltpu.roll(xs, ROT_HALF, 1) * sa + pltpu.roll(xs, LANES - ROT_HALF, 1) * sb)
    return outs


def _rot_bwd(dys, tab):
    c, sa, sb = tab[:, :LANES], tab[:, LANES:2 * LANES], tab[:, 2 * LANES:]
    return [dy * c + pltpu.roll(dy * sa, LANES - ROT_HALF, 1) + pltpu.roll(dy * sb, ROT_HALF, 1) for dy in dys]


ATT_SCALE = HEAD_DIM ** -0.5


def _attn_masks(n):
    kj = lax.broadcasted_iota(jnp.int32, (2 * ATT_BLOCK, ATT_BLOCK), 0)
    qi = lax.broadcasted_iota(jnp.int32, (2 * ATT_BLOCK, ATT_BLOCK), 1)
    delta = qi + ATT_BLOCK - kj
    first_key = jnp.where(n > 0, 0, ATT_BLOCK)
    valid = (delta >= 0) & (delta < ATT_BLOCK) & (kj >= first_key)
    low = lax.broadcasted_iota(jnp.int32, (1, LANES), 1) < HEAD_DIM
    upper = lax.broadcasted_iota(jnp.int32, (LANES, 1), 0) < HEAD_DIM
    return valid, low, upper


def _softmax_sink(s, valid, sink):
    s = jnp.where(valid, s, NEG_INF)
    m = jnp.maximum(jnp.max(s, axis=0, keepdims=True), sink)
    e = jnp.exp(s - m)
    es = jnp.exp(sink - m)
    inv = 1.0 / (jnp.sum(e, axis=0, keepdims=True) + es)
    return e * inv, es * inv


def _attn_specs(nb, tables):
    prev = lambda n: jnp.maximum(jnp.minimum(n, nb - 1) - 1, 0)
    cur = lambda n: jnp.minimum(n, nb - 1)
    specs = [
        pl.BlockSpec((ATT_BLOCK, Q_DIM), lambda n: (cur(n), 0)),
        pl.BlockSpec((ATT_BLOCK, KV_DIM), lambda n: (prev(n), 4)),
        pl.BlockSpec((ATT_BLOCK, KV_DIM), lambda n: (cur(n), 4)),
        pl.BlockSpec((ATT_BLOCK, KV_DIM), lambda n: (prev(n), 5)),
        pl.BlockSpec((ATT_BLOCK, KV_DIM), lambda n: (cur(n), 5)),
    ]
    if tables:
        specs += [pl.BlockSpec((ATT_BLOCK, 3 * LANES), lambda n: (prev(n), 0)),
                  pl.BlockSpec((ATT_BLOCK, 3 * LANES), lambda n: (cur(n), 0))]
    return specs + [pl.BlockSpec(memory_space=pltpu.SMEM)]


def _kv_band(prev_ref, cur_ref):
    out = []
    for j in range(KV_DIM // LANES):
        sl = slice(j * LANES, (j + 1) * LANES)
        band = jnp.concatenate([prev_ref[:, sl], cur_ref[:, sl]], axis=0)
        out.append((band, pltpu.roll(band, HEAD_DIM, 1)))
    return out


def _bf16(bands, transposed=False):
    return [[(a.T if transposed else a).astype(BF16) for a in pair] for pair in bands]


def _attn_fwd(qkv, sinks, carry=(None, None)):
    T = qkv.shape[0]
    nb = T // ATT_BLOCK

    def body(*refs):
        n = pl.program_id(0)
        own, finish = _carried(carry, refs, 6, 1, n == 0, n == nb - 1)
        q_ref, kp_ref, kc_ref, vp_ref, vc_ref, sink_ref, o_ref = own
        valid, low, upper = _attn_masks(n)
        ks = _bf16(_kv_band(kp_ref, kc_ref))
        vts = _bf16(_kv_band(vp_ref, vc_ref), transposed=True)
        heads = []
        for p in range(Q_DIM // LANES):
            kpair, khalf = p // 4, (p // 2) % 2
            q_pair = q_ref[:, p * LANES:(p + 1) * LANES] * ATT_SCALE
            for hf in range(2):
                qm = jnp.where(low if hf == 0 else ~low, q_pair, 0.0).astype(BF16)
                sw = 0 if khalf == hf else 1
                heads.append((2 * p + hf, kpair, sw, _dot_nt(ks[kpair][sw], qm)))
        probs = [_softmax_sink(s, valid, sink_ref[0, h])[0].astype(BF16) for h, _, _, s in heads]
        outs = [_dot(vts[kpair][sw], pr) for (_, kpair, sw, _), pr in zip(heads, probs)]
        for p in range(Q_DIM // LANES):
            o_ref[:, p * LANES:(p + 1) * LANES] = jnp.where(upper, outs[2 * p], outs[2 * p + 1]).T.astype(BF16)
        finish()

    in_specs, out_specs, out_shape, scratch, extra = _carried_specs(
        carry, _attn_specs(nb, False), [pl.BlockSpec((ATT_BLOCK, Q_DIM), lambda n: (n, 0))],
        [jax.ShapeDtypeStruct((T, Q_DIM), BF16)], [])
    return pl.pallas_call(
        body, name="attn_fwd", grid=(nb,), in_specs=in_specs, out_specs=out_specs, out_shape=out_shape,
        scratch_shapes=scratch, compiler_params=_params(dimension_semantics=("arbitrary",)),
    )(qkv, qkv, qkv, qkv, qkv, sinks, *extra)


def _attn_bwd(qkv, rot, sinks, dout, carry=(None, None)):
    T = qkv.shape[0]
    nb = T // ATT_BLOCK
    npair = KV_DIM // LANES

    def body(*refs):
        n = pl.program_id(0)
        own, finish = _carried(carry, refs, 9, 2, n == 0, n == nb)
        (q_ref, kp_ref, kc_ref, vp_ref, vc_ref, tp_ref, tc_ref, sink_ref, do_ref, dqkv_ref, dsink_ref,
         dq_c, dk_c, dv_c) = own

        @pl.when(n == 0)
        def _():
            dq_c[...] = jnp.zeros_like(dq_c)
            dk_c[...] = jnp.zeros_like(dk_c)
            dv_c[...] = jnp.zeros_like(dv_c)
            dsink_ref[...] = jnp.zeros_like(dsink_ref)

        def flush(dk_prev, dv_prev, tab_ref):
            dqkv_ref[:, :Q_DIM] = dq_c[...]
            dk = _rot_bwd([dk_c[:, j * LANES:(j + 1) * LANES] + dk_prev[j] for j in range(npair)], tab_ref[...])
            for j in range(npair):
                dqkv_ref[:, Q_DIM + j * LANES:Q_DIM + (j + 1) * LANES] = dk[j]
                dqkv_ref[:, Q_DIM + KV_DIM + j * LANES:Q_DIM + KV_DIM + (j + 1) * LANES] = (
                    dv_c[:, j * LANES:(j + 1) * LANES] + dv_prev[j])

        @pl.when(n < nb)
        def _():
            valid, low, upper = _attn_masks(n)
            lane = lax.broadcasted_iota(jnp.int32, (1, LANES), 1)
            k_band = _kv_band(kp_ref, kc_ref)
            ks, kts = _bf16(k_band), _bf16(k_band, transposed=True)
            vs = _bf16(_kv_band(vp_ref, vc_ref))
            dk_acc = [[jnp.zeros((2 * ATT_BLOCK, LANES), F32) for _ in range(2)] for _ in range(npair)]
            dv_acc = [[jnp.zeros((2 * ATT_BLOCK, LANES), F32) for _ in range(2)] for _ in range(npair)]
            dsink = jnp.zeros((1, LANES), F32)
            heads = []
            for p in range(Q_DIM // LANES):
                kpair, khalf = p // 4, (p // 2) % 2
                q_pair = q_ref[:, p * LANES:(p + 1) * LANES] * ATT_SCALE
                do_pair = do_ref[:, p * LANES:(p + 1) * LANES]
                for hf in range(2):
                    sel = low if hf == 0 else ~low
                    qm = jnp.where(sel, q_pair, 0.0).astype(BF16)
                    dom = jnp.where(sel, do_pair, 0.0).astype(BF16)
                    sw = 0 if khalf == hf else 1
                    heads.append((2 * p + hf, kpair, sw, qm, dom,
                                  _dot_nt(ks[kpair][sw], qm), _dot_nt(vs[kpair][sw], dom)))
            grads = []
            for h, kpair, sw, qm, dom, s, dp in heads:
                pr, ps = _softmax_sink(s, valid, sink_ref[0, h])
                dd = jnp.sum(pr * dp, axis=0, keepdims=True)
                dsink = dsink + jnp.where(lane == h, -jnp.sum(ps * dd, axis=1, keepdims=True), 0.0)
                grads.append((pr * (dp - dd)).astype(BF16))
                heads[h] = (kpair, sw, qm, dom, pr.astype(BF16))
            dq_t = []
            for (kpair, sw, qm, dom, pr), ds in zip(heads, grads):
                dq_t.append(_dot(kts[kpair][sw], ds))
                dk_acc[kpair][sw] = dk_acc[kpair][sw] + _dot(ds, qm)
                dv_acc[kpair][sw] = dv_acc[kpair][sw] + _dot(pr, dom)
            dqs = [jnp.where(upper, dq_t[2 * p], dq_t[2 * p + 1]).T * ATT_SCALE for p in range(Q_DIM // LANES)]
            dk_acc = [a[0] + pltpu.roll(a[1], HEAD_DIM, 1) for a in dk_acc]
            dv_acc = [a[0] + pltpu.roll(a[1], HEAD_DIM, 1) for a in dv_acc]
            flush([a[:ATT_BLOCK] for a in dk_acc], [a[:ATT_BLOCK] for a in dv_acc], tp_ref)
            dq = _rot_bwd(dqs, tc_ref[...])
            for p in range(Q_DIM // LANES):
                dq_c[:, p * LANES:(p + 1) * LANES] = dq[p]
            for j in range(npair):
                dk_c[:, j * LANES:(j + 1) * LANES] = dk_acc[j][ATT_BLOCK:]
                dv_c[:, j * LANES:(j + 1) * LANES] = dv_acc[j][ATT_BLOCK:]
            dsink_ref[...] += dsink

        @pl.when(n == nb)
        def _():
            zero = [jnp.zeros((ATT_BLOCK, LANES), F32) for _ in range(npair)]
            flush(zero, zero, tc_ref)

        finish()

    do_spec = pl.BlockSpec((ATT_BLOCK, Q_DIM), lambda n: (jnp.minimum(n, nb - 1), 0))
    in_specs, out_specs, out_shape, scratch, extra = _carried_specs(
        carry, _attn_specs(nb, True) + [do_spec],
        [pl.BlockSpec((ATT_BLOCK, QKV_DIM), lambda n: (jnp.maximum(n - 1, 0), 0)),
         pl.BlockSpec((1, LANES), lambda n: (0, 0))],
        [jax.ShapeDtypeStruct((T, QKV_DIM), F32), jax.ShapeDtypeStruct((1, LANES), F32)],
        [pltpu.VMEM((ATT_BLOCK, Q_DIM), F32), pltpu.VMEM((ATT_BLOCK, KV_DIM), F32),
         pltpu.VMEM((ATT_BLOCK, KV_DIM), F32)])
    return pl.pallas_call(
        body, name="attn_bwd", grid=(nb + 1,), in_specs=in_specs, out_specs=out_specs, out_shape=out_shape,
        scratch_shapes=scratch, compiler_params=_params(dimension_semantics=("arbitrary",)),
    )(qkv, qkv, qkv, qkv, qkv, rot, rot, sinks, dout, *extra)


LEVELS = (32, 16, 8)
DIAG = 8
SUBLANES = 8
UNROLL = 4
UNROLL_BWD = 2


def _lower_bound(lb_ref):
    l0, l1 = lb_ref[0:1, :], lb_ref[1:2, :]
    mx = jnp.maximum(l0, l1)
    e0, e1 = jnp.exp(l0 - mx), jnp.exp(l1 - mx)
    return e1 / (e0 + e1)


GROUPS = CHUNK // SUBLANES


def _group_roll(x, k):
    return pltpu.roll(x.reshape(GROUPS, SUBLANES, HGRN_DK), k % SUBLANES, 1).reshape(CHUNK, HGRN_DK)


def _scan_rows(x, row, reverse):
    r8 = row & (SUBLANES - 1)
    for sh in (1, 2, 4):
        ok = (r8 < SUBLANES - sh) if reverse else (r8 >= sh)
        x = x + jnp.where(ok, _group_roll(x, -sh if reverse else sh), 0.0)
    g = x.reshape(GROUPS, SUBLANES, HGRN_DK)
    edge = 0 if reverse else SUBLANES - 1
    tot = jnp.broadcast_to(g[:, edge:edge + 1, :], g.shape)

    def shifted(a, n):
        z = jnp.zeros((n, SUBLANES, HGRN_DK), F32)
        return jnp.concatenate([a[n:], z] if reverse else [z, a[:GROUPS - n]], axis=0)

    acc = shifted(tot, 1)
    for sh in (1, 2, 4):
        acc = acc + shifted(acc, sh)
    return (g + acc).reshape(CHUNK, HGRN_DK)


def _level_masks():
    t = lax.broadcasted_iota(jnp.int32, (CHUNK, CHUNK), 0)
    s = lax.broadcasted_iota(jnp.int32, (CHUNK, CHUNK), 1)
    return [((t & h) != 0) & ((s & h) == 0) & ((t ^ s) < 2 * h) for h in LEVELS]


def _level_scales(b):
    out = []
    for h in LEVELS:
        parts = [jnp.broadcast_to(b[j * 2 * h + h - 1:j * 2 * h + h, :], (2 * h, HGRN_DK))
                 for j in range(CHUNK // (2 * h))]
        mid = parts[0] if len(parts) == 1 else jnp.concatenate(parts, axis=0)
        out.append(jnp.exp(-jnp.abs(b - mid)))
    return out


def _hgrn_gates(zq, zf, lb):
    sq = jax.nn.sigmoid(zq)
    q = zq * sq
    sg = jax.nn.sigmoid(zf)
    forget = lb + (1.0 - lb) * sg
    return q, sq, sg, forget, 1.0 - forget, jnp.log(forget)


def _hgrn_specs(T, rb, rev):
    nr = T // rb
    ri = (lambda r: nr - 1 - r) if rev else (lambda r: r)
    return nr, ri, [
        pl.BlockSpec((rb, HGRN_DK), lambda h, r: (ri(r), h)),
        pl.BlockSpec((rb, HGRN_DK), lambda h, r: (ri(r), HGRN_HEADS + h)),
        pl.BlockSpec((rb, HGRN_DK), lambda h, r: (ri(r), 2 * HGRN_HEADS + h)),
        pl.BlockSpec((2, HGRN_DK), lambda h, r: (0, h)),
    ]


def _hgrn_fwd(z, lb_raw, rb=1024, carry=(None, None)):
    T = z.shape[0]
    rb = min(rb, T)
    ncb = rb // CHUNK
    nr, ri, in_specs = _hgrn_specs(T, rb, False)

    def body(*refs):
        hh, rr = pl.program_id(0), pl.program_id(1)
        own, finish = _carried(carry, refs, 4, 2, (hh == 0) & (rr == 0), (hh == HGRN_HEADS - 1) & (rr == nr - 1))
        zq_ref, zf_ref, zi_ref, lb_ref, o_ref, st_ref, state = own

        @pl.when(rr == 0)
        def _():
            state[...] = jnp.zeros_like(state)

        lb = _lower_bound(lb_ref)
        row = lax.broadcasted_iota(jnp.int32, (CHUNK, HGRN_DK), 0)
        masks = _level_masks()
        rd = row & (DIAG - 1)

        def chunk(c, st):
            rows = pl.ds(pl.multiple_of(c * CHUNK, CHUNK), CHUNK)
            q, _, _, _, k, lf = _hgrn_gates(zq_ref[rows, :], zf_ref[rows, :], lb)
            v = zi_ref[rows, :]
            vb = v.astype(BF16)
            b = _scan_rows(lf, row, False)
            sc = jnp.zeros((CHUNK, CHUNK), F32)
            for e, mask in zip(_level_scales(b), masks):
                sc = sc + jnp.where(mask, _dot_nt((q * e).astype(BF16), (k * e).astype(BF16)), 0.0)
            o = _dot(sc.astype(BF16), vb) + jnp.sum(q * k, axis=-1, keepdims=True) * v
            for d in range(1, DIAG):
                w = jnp.where(rd >= d, q * _group_roll(k, d) * jnp.exp(b - _group_roll(b, d)), 0.0)
                o = o + jnp.sum(w, axis=-1, keepdims=True) * _group_roll(v, d)
            b_last = b[CHUNK - 1:CHUNK, :]
            kd = (k * jnp.exp(b_last - b)).astype(BF16)
            qd = (q * jnp.exp(b)).astype(BF16)
            st_ref[c, 0] = st
            o_ref[rows, :] = o + _dot_nt(qd, st.astype(BF16))
            return st * jnp.exp(b_last) + _dot_tn(vb, kd)

        def group(i, st):
            for j in range(UNROLL):
                st = chunk(i * UNROLL + j, st)
            return st

        state[...] = lax.fori_loop(0, ncb // UNROLL, group, state[...])
        finish()

    in_specs, out_specs, out_shape, scratch, extra = _carried_specs(
        carry, in_specs,
        [pl.BlockSpec((rb, HGRN_DK), lambda h, r: (r, h)),
         pl.BlockSpec((ncb, 1, HGRN_DK, HGRN_DK), lambda h, r: (r, h, 0, 0))],
        [jax.ShapeDtypeStruct((T, D_MODEL), F32),
         jax.ShapeDtypeStruct((T // CHUNK, HGRN_HEADS, HGRN_DK, HGRN_DK), F32)],
        [pltpu.VMEM((HGRN_DK, HGRN_DK), F32)])
    return pl.pallas_call(
        body, name="hgrn_fwd", grid=(HGRN_HEADS, nr), in_specs=in_specs, out_specs=out_specs, out_shape=out_shape,
        scratch_shapes=scratch, compiler_params=_params(dimension_semantics=("arbitrary", "arbitrary")),
    )(z, z, z, lb_raw, *extra)


def _hgrn_bwd(z, lb_raw, states, do, rb=1024, carry=(None, None)):
    T = z.shape[0]
    rb = min(rb, T)
    ncb = rb // CHUNK
    nr, ri, in_specs = _hgrn_specs(T, rb, True)
    in_specs += [pl.BlockSpec((ncb, 1, HGRN_DK, HGRN_DK), lambda h, r: (ri(r), h, 0, 0)),
                 pl.BlockSpec((rb, HGRN_DK), lambda h, r: (ri(r), h))]

    def body(*refs):
        hh, rr = pl.program_id(0), pl.program_id(1)
        own, finish = _carried(carry, refs, 6, 4, (hh == 0) & (rr == 0), (hh == HGRN_HEADS - 1) & (rr == nr - 1))
        zq_ref, zf_ref, zi_ref, lb_ref, st_ref, do_ref, dq_ref, df_ref, di_ref, dlb_ref, dstate = own

        @pl.when(rr == 0)
        def _():
            dstate[...] = jnp.zeros_like(dstate)
            dlb_ref[...] = jnp.zeros_like(dlb_ref)

        lb = _lower_bound(lb_ref)
        row = lax.broadcasted_iota(jnp.int32, (CHUNK, HGRN_DK), 0)
        masks = _level_masks()
        rd = row & (DIAG - 1)

        def chunk(ci, dlb):
            c = ncb - 1 - ci
            rows = pl.ds(pl.multiple_of(c * CHUNK, CHUNK), CHUNK)
            zq = zq_ref[rows, :]
            q, sq, sg, forget, k, lf = _hgrn_gates(zq, zf_ref[rows, :], lb)
            v = zi_ref[rows, :]
            dov = do_ref[rows, :]
            b = _scan_rows(lf, row, False)
            st = st_ref[c, 0]
            dst = dstate[...]
            b_last = b[CHUNK - 1:CHUNK, :]
            eb = jnp.exp(b)
            ebb = jnp.exp(b_last - b)
            e_last = jnp.exp(b_last)
            dob, vb, stb, dstb = dov.astype(BF16), v.astype(BF16), st.astype(BF16), dst.astype(BF16)
            dq = eb * _dot(dob, stb)
            dv = _dot_nt((k * ebb).astype(BF16), dstb)
            dk = ebb * _dot(vb, dstb)
            extra = e_last * jnp.sum(dst * st, axis=0, keepdims=True) + jnp.sum(k * dk, axis=0, keepdims=True)
            da = _dot_nt(dob, vb)
            sc = jnp.zeros((CHUNK, CHUNK), F32)
            for e, mask in zip(_level_scales(b), masks):
                qs, ks = (q * e).astype(BF16), (k * e).astype(BF16)
                dam = jnp.where(mask, da, 0.0).astype(BF16)
                dq = dq + e * _dot(dam, ks)
                dk = dk + e * _dot_tn(dam, qs)
                sc = sc + jnp.where(mask, _dot_nt(qs, ks), 0.0)
            dv = dv + _dot_tn(sc.astype(BF16), dob)
            dad = jnp.sum(dov * v, axis=-1, keepdims=True)
            dq = dq + dad * k
            dk = dk + dad * q
            dv = dv + jnp.sum(q * k, axis=-1, keepdims=True) * dov
            for d in range(1, DIAG):
                w = jnp.where(rd >= d, jnp.exp(b - _group_roll(b, d)), 0.0)
                kr = _group_roll(k, d)
                dad = jnp.sum(dov * _group_roll(v, d), axis=-1, keepdims=True)
                ad = jnp.sum(q * kr * w, axis=-1, keepdims=True)
                dq = dq + dad * kr * w
                dk = dk + _group_roll(dad * q * w, -d)
                dv = dv + _group_roll(ad * dov, -d)
            dlf = _scan_rows(q * dq - k * dk, row, True) + extra
            dstate[...] = dst * e_last + _dot_tn(dob, (q * eb).astype(BF16))
            dforget = dlf / forget - dk
            dq_ref[rows, :] = (dq * (sq * (1.0 + zq * (1.0 - sq)))).astype(BF16)
            df_ref[rows, :] = (dforget * (1.0 - lb) * sg * (1.0 - sg)).astype(BF16)
            di_ref[rows, :] = dv.astype(BF16)
            return dlb + jnp.sum(dforget * (1.0 - sg), axis=0, keepdims=True)

        def group(i, dlb):
            for j in range(UNROLL_BWD):
                dlb = chunk(i * UNROLL_BWD + j, dlb)
            return dlb

        dlb_ref[...] += lax.fori_loop(0, ncb // UNROLL_BWD, group, jnp.zeros((1, HGRN_DK), F32))
        finish()

    blk = pl.BlockSpec((rb, HGRN_DK), lambda h, r: (ri(r), h))
    in_specs, out_specs, out_shape, scratch, extra = _carried_specs(
        carry, in_specs, [blk, blk, blk, pl.BlockSpec((1, HGRN_DK), lambda h, r: (0, h))],
        [jax.ShapeDtypeStruct((T, D_MODEL), BF16)] * 3 + [jax.ShapeDtypeStruct((1, D_MODEL), F32)],
        [pltpu.VMEM((HGRN_DK, HGRN_DK), F32)])
    return pl.pallas_call(
        body, name="hgrn_bwd", grid=(HGRN_HEADS, nr), in_specs=in_specs, out_specs=out_specs, out_shape=out_shape,
        scratch_shapes=scratch, compiler_params=_params(dimension_semantics=("arbitrary", "arbitrary")),
    )(z, z, z, lb_raw, states, do, *extra)


MESH = pl.DeviceIdType.MESH
ANY = pl.BlockSpec(memory_space=pl.ANY)


def _place():
    return lax.axis_index("x"), lax.axis_index("y"), lax.axis_index("c")


def _sems(n):
    return [pltpu.SemaphoreType.DMA((7 * n,)), pltpu.SemaphoreType.DMA((7 * n,)), pltpu.SemaphoreType.DMA((n,))]


class _Gather:
    def __init__(self, x_ref, out_ref, send_sems, recv_sems, local_sems, idx):
        self.x_ref, self.out_ref, self.send_sems, self.recv_sems, self.local_sem, self.base = (
            x_ref, out_ref, send_sems, recv_sems, local_sems.at[idx], 7 * idx)
        x, y, c = _place()
        self.c = c
        self.me, self.sibling = (x, y, c), (x, y, 1 - c)
        self.chips = [(1 - x, y), (x, 1 - y), (1 - x, 1 - y)]

    def rows(self, px, py, pc):
        return self.out_ref.at[4 * px + 2 * py + pc]

    def copy(self, k, block, to, from_input=False):
        return pltpu.make_async_remote_copy(
            src_ref=self.x_ref if from_input else self.rows(*block), dst_ref=self.rows(*block),
            send_sem=self.send_sems.at[self.base + k], recv_sem=self.recv_sems.at[self.base + k], device_id=to,
            device_id_type=MESH)

    def first(self):
        out = [self.copy(0, self.me, self.sibling, from_input=True)]
        return out + [self.copy(1 + j, self.me, (*chip, self.c), from_input=True) for j, chip in enumerate(self.chips)]

    def start(self):
        pltpu.make_async_copy(self.x_ref, self.rows(*self.me), self.local_sem).start()
        for cp in self.first():
            cp.start()

    def finish(self):
        passed = [self.copy(4 + j, (*chip, self.c), self.sibling) for j, chip in enumerate(self.chips)]
        for j, chip in enumerate(self.chips):
            self.copy(1 + j, (*chip, self.c), self.me).wait_recv()
            passed[j].start()
        self.copy(0, self.sibling, self.me).wait_recv()
        for j, chip in enumerate(self.chips):
            self.copy(4 + j, (*chip, 1 - self.c), self.me).wait_recv()
        for cp in self.first() + passed:
            cp.wait_send()
        pltpu.make_async_copy(self.x_ref, self.rows(*self.me), self.local_sem).wait()


class _Many:
    def __init__(self, kind, in_refs, out_refs, send_sems, recv_sems, local_sems):
        self.ops = [kind(x, o, send_sems, recv_sems, local_sems, i) for i, (x, o) in enumerate(zip(in_refs, out_refs))]

    def start(self):
        for op in self.ops:
            op.start()

    def finish(self):
        for op in self.ops:
            op.finish()


def _result_shapes(kind, arrs):
    return [jax.ShapeDtypeStruct(a.shape if kind is _Exchange else (N_DEV,) + a.shape, a.dtype) for a in arrs]


def _all_gather(name, shards):
    n = len(shards)

    def body(*refs):
        g = _Many(_Gather, refs[:n], refs[n:2 * n], *refs[2 * n:])
        g.start()
        g.finish()

    return pl.pallas_call(
        body, name=name, out_shape=_result_shapes(_Gather, shards), in_specs=[ANY] * n, out_specs=[ANY] * n,
        scratch_shapes=_sems(n),
    )(*shards)


def _peers(x, y, c):
    out = []
    for k in range(1, N_DEV):
        px = 1 - x if k & 4 else x
        py = 1 - y if k & 2 else y
        pc = 1 - c if k & 1 else c
        out.append((k, (px, py, pc), 4 * px + 2 * py + pc))
    return out


class _Exchange:
    def __init__(self, g_ref, recv_ref, send_sems, recv_sems, local_sems, idx):
        x, y, c = _place()
        me = 4 * x + 2 * y + c
        self.local = pltpu.make_async_copy(g_ref.at[me], recv_ref.at[me], local_sems.at[idx])
        self.copies = [
            pltpu.make_async_remote_copy(
                src_ref=g_ref.at[pidx], dst_ref=recv_ref.at[me], send_sem=send_sems.at[7 * idx + k - 1],
                recv_sem=recv_sems.at[7 * idx + k - 1], device_id=peer, device_id_type=MESH)
            for k, peer, pidx in _peers(x, y, c)]

    def start(self):
        self.local.start()
        for cp in self.copies:
            cp.start()

    def finish(self):
        for cp in self.copies:
            cp.wait()
        self.local.wait()


def _carried(carry, refs, n_in, n_out, first, last):
    kind, arrs = carry
    if kind is None:
        return refs, lambda: None
    n = len(arrs)
    ins, rest = refs[:n_in], refs[n_in + n:]
    outs, scratch = rest[:n_out], rest[n_out + n:]
    op = _Many(kind, refs[n_in:n_in + n], rest[n_out:n_out + n], *scratch[len(scratch) - 3:])

    @pl.when(first)
    def _():
        op.start()

    def finish():
        @pl.when(last)
        def _():
            op.finish()

    return tuple(ins) + tuple(outs) + tuple(scratch[:len(scratch) - 3]), finish


def _carried_specs(carry, in_specs, out_specs, out_shape, scratch):
    kind, arrs = carry
    if kind is None:
        return in_specs, out_specs, out_shape, scratch, []
    n = len(arrs)
    return (list(in_specs) + [ANY] * n, list(out_specs) + [ANY] * n,
            list(out_shape) + _result_shapes(kind, arrs), list(scratch) + _sems(n), list(arrs))


def _adamw(w, g, m, v):
    m = ADAM_B1 * m + (1.0 - ADAM_B1) * g
    v = ADAM_B2 * v + (1.0 - ADAM_B2) * (g * g)
    m_hat = m / (1.0 - ADAM_B1 ** ADAM_STEP)
    v_hat = v / (1.0 - ADAM_B2 ** ADAM_STEP)
    delta = -ADAM_LR * (m_hat / (jnp.sqrt(v_hat) + ADAM_EPS) + ADAM_WD * w)
    return delta, m, v


def _adamw_sum(name, recvs, w, m, v):
    L, R, C = w.shape
    tm = 128 if R % 128 == 0 else 64
    assert R % tm == 0 and len(recvs) == L

    def body(*refs):
        r_refs, (w_ref, m_ref, v_ref, g_ref, d_ref, nm_ref, nv_ref) = refs[:L], refs[L:]
        for l in range(L):
            g = r_refs[l][0].astype(F32)
            for s in range(1, N_DEV):
                g = g + r_refs[l][s].astype(F32)
            g_ref[l] = g
            d_ref[l], nm_ref[l], nv_ref[l] = _adamw(w_ref[l], g, m_ref[l], v_ref[l])

    blk = pl.BlockSpec((L, tm, C), lambda i: (0, i, 0))
    return pl.pallas_call(
        body, name=name, grid=(R // tm,),
        in_specs=[pl.BlockSpec((N_DEV, tm, C), lambda i: (0, i, 0))] * L + [blk, blk, blk],
        out_specs=[blk] * 4, out_shape=[jax.ShapeDtypeStruct((L, R, C), F32)] * 4,
        compiler_params=_params(dimension_semantics=("arbitrary",)),
    )(*recvs, w, m, v)


def _small_sync(part, w, m, v):
    def body(p_ref, w_ref, m_ref, v_ref, g_ref, d_ref, nm_ref, nv_ref, gath, send_sems, recv_sems):
        x, y, c = _place()
        me = 4 * x + 2 * y + c
        gath[me] = p_ref[...]
        copies = []
        for k, peer, _ in _peers(x, y, c):
            cp = pltpu.make_async_remote_copy(
                src_ref=p_ref, dst_ref=gath.at[me], send_sem=send_sems.at[k - 1], recv_sem=recv_sems.at[k - 1],
                device_id=peer, device_id_type=MESH)
            cp.start()
            copies.append(cp)
        for cp in copies:
            cp.wait()
        g = gath[0]
        for s in range(1, N_DEV):
            g = g + gath[s]
        wv = w_ref[...]
        l0, l1 = w_ref[8:9, :], w_ref[9:10, :]
        mx = jnp.maximum(l0, l1)
        e0, e1 = jnp.exp(l0 - mx), jnp.exp(l1 - mx)
        g9 = g[9:10, :] * (e0 / (e0 + e1)) * (e1 / (e0 + e1))
        row = lax.broadcasted_iota(jnp.int32, g.shape, 0)
        g = jnp.where(row == 9, g9, jnp.where(row == 8, -g9, g))
        g_ref[...] = g
        d_ref[...], nm_ref[...], nv_ref[...] = _adamw(wv, g, m_ref[...], v_ref[...])

    vm = pl.BlockSpec(memory_space=pltpu.VMEM)
    return pl.pallas_call(
        body, name="small_params_sync", in_specs=[vm] * 4, out_specs=[vm] * 4,
        out_shape=[jax.ShapeDtypeStruct(part.shape, F32)] * 4,
        scratch_shapes=[pltpu.VMEM((N_DEV,) + part.shape, F32), pltpu.SemaphoreType.DMA((7,)),
                        pltpu.SemaphoreType.DMA((7,))],
    )(part, w, m, v)


def _shards_bf16(d, pieces):
    return [d[name][layer].astype(BF16) for name, layer in pieces]


def _gathered(arrs, pieces, out):
    for a, (name, layer) in zip(arrs, pieces):
        out[name, layer] = a if name in COL_SHARDED else a.reshape(N_DEV * a.shape[1], a.shape[2])


def _pad_row(a, width=D_MODEL):
    a = a.reshape(1, -1)
    return jnp.pad(a, ((0, 0), (0, width - a.shape[1])))


def _pack_small(d, gn_full):
    rows = [d["mix_norm"], d["mlp_norm"], d["final_norm"].reshape(1, D_MODEL),
            _pad_row(d["attn_b_qkv"], 2 * D_MODEL).reshape(2, D_MODEL), _pad_row(d["attn_sinks"]),
            d["hgrn_lower_bounds"], gn_full.reshape(1, D_MODEL)]
    p = jnp.concatenate(rows, axis=0)
    return jnp.pad(p, ((0, SMALL_ROWS - p.shape[0]), (0, 0)))


def _unpack_small(p, me):
    return dict(
        mix_norm=p[0:2], mlp_norm=p[2:4], final_norm=p[4],
        attn_b_qkv=p[5:7].reshape(1, 2 * D_MODEL)[:, :QKV_DIM], attn_sinks=p[7:8, :N_Q_HEADS],
        hgrn_lower_bounds=p[8:10], hgrn_g_norm=lax.dynamic_slice(p[10:11], (0, me * 128), (1, 128)))


WEIGHT_NAMES = ['mix_norm', 'mlp_norm', 'final_norm', 'attn_w_qkv', 'attn_b_qkv', 'attn_sinks', 'attn_w_o', 'hgrn_w_in',
                'hgrn_g_norm', 'hgrn_w_o', 'hgrn_lower_bounds', 'mlp_w_up', 'mlp_w_down']
SMALL_NAMES = ('mix_norm', 'mlp_norm', 'final_norm', 'attn_b_qkv', 'attn_sinks', 'hgrn_lower_bounds', 'hgrn_g_norm')


def _rotary_tables(positions):
    inv_freq = ROPE_THETA ** (-jnp.arange(0, 2 * ROT_HALF, 2, dtype=F32) / (2 * ROT_HALF))
    ang = positions.astype(F32).reshape(-1, 1) * inv_freq
    cos, sin = jnp.cos(ang), jnp.sin(ang)
    r = jnp.arange(LANES) % HEAD_DIM
    idx = r % ROT_HALF
    c = jnp.where(r < 2 * ROT_HALF, cos[:, idx], 1.0)
    sa = jnp.where((r >= ROT_HALF) & (r < 2 * ROT_HALF), sin[:, idx], 0.0)
    sb = jnp.where(r < ROT_HALF, -sin[:, idx], 0.0)
    return jnp.concatenate([c, sa, sb], axis=1)


def kernel(x, positions, mix_norm, mlp_norm, final_norm, attn_w_qkv, attn_b_qkv, attn_sinks, attn_w_o, hgrn_w_in, hgrn_g_norm, hgrn_w_o, hgrn_lower_bounds, mlp_w_up, mlp_w_down, loss_target, m_mix_norm, m_mlp_norm, m_final_norm, m_attn_w_qkv, m_attn_b_qkv, m_attn_sinks, m_attn_w_o, m_hgrn_w_in, m_hgrn_g_norm, m_hgrn_w_o, m_hgrn_lower_bounds, m_mlp_w_up, m_mlp_w_down, v_mix_norm, v_mlp_norm, v_final_norm, v_attn_w_qkv, v_attn_b_qkv, v_attn_sinks, v_attn_w_o, v_hgrn_w_in, v_hgrn_g_norm, v_hgrn_w_o, v_hgrn_lower_bounds, v_mlp_w_up, v_mlp_w_down):
    w = dict(mix_norm=mix_norm, mlp_norm=mlp_norm, final_norm=final_norm, attn_w_qkv=attn_w_qkv, attn_b_qkv=attn_b_qkv,
             attn_sinks=attn_sinks, attn_w_o=attn_w_o, hgrn_w_in=hgrn_w_in, hgrn_g_norm=hgrn_g_norm, hgrn_w_o=hgrn_w_o,
             hgrn_lower_bounds=hgrn_lower_bounds, mlp_w_up=mlp_w_up, mlp_w_down=mlp_w_down)
    m = dict(mix_norm=m_mix_norm, mlp_norm=m_mlp_norm, final_norm=m_final_norm, attn_w_qkv=m_attn_w_qkv,
             attn_b_qkv=m_attn_b_qkv, attn_sinks=m_attn_sinks, attn_w_o=m_attn_w_o, hgrn_w_in=m_hgrn_w_in,
             hgrn_g_norm=m_hgrn_g_norm, hgrn_w_o=m_hgrn_w_o, hgrn_lower_bounds=m_hgrn_lower_bounds, mlp_w_up=m_mlp_w_up,
             mlp_w_down=m_mlp_w_down)
    v = dict(mix_norm=v_mix_norm, mlp_norm=v_mlp_norm, final_norm=v_final_norm, attn_w_qkv=v_attn_w_qkv,
             attn_b_qkv=v_attn_b_qkv, attn_sinks=v_attn_sinks, attn_w_o=v_attn_w_o, hgrn_w_in=v_hgrn_w_in,
             hgrn_g_norm=v_hgrn_g_norm, hgrn_w_o=v_hgrn_w_o, hgrn_lower_bounds=v_hgrn_lower_bounds, mlp_w_up=v_mlp_w_up,
             mlp_w_down=v_mlp_w_down)
    me = 4 * lax.axis_index("x") + 2 * lax.axis_index("y") + lax.axis_index("c")

    gn = hgrn_g_norm.reshape(1, 128)
    gn_a = gn.astype(BF16)
    gn_b = (gn - gn_a.astype(F32)).astype(BF16)
    gn_c = (gn - gn_a.astype(F32) - gn_b.astype(F32)).astype(BF16)
    gn_rows = jnp.pad(jnp.concatenate([gn_a, gn_b, gn_c], axis=1), ((0, 15), (0, D_MODEL - 3 * 128)))
    full = {}
    got = _all_gather("gather_attn_weights", _shards_bf16(w, GATHER_FIRST) + [gn_rows])
    _gathered(got[:1], GATHER_FIRST, full)
    w_qkv = full["attn_w_qkv", 0].transpose(1, 0, 2).reshape(D_MODEL, QKV_DIM)
    gn_terms = got[1][:, 0, :3 * 128].astype(F32).reshape(N_DEV, 3, 128)
    gn_full = ((gn_terms[:, 0] + gn_terms[:, 1]) + gn_terms[:, 2]).reshape(1, D_MODEL)

    x0 = x[0]
    tgt = loss_target[0]
    rot = _rotary_tables(positions)
    row = lambda a: a.reshape(1, -1)

    qkv, h0 = _norm_mm("qkv_proj", x0, row(mix_norm[0]), w_qkv, attn_b_qkv, rot=rot)
    att, *got = _attn_fwd(qkv, attn_sinks, carry=(_Gather, _shards_bf16(w, GATHER_ATTN)))
    _gathered(got, GATHER_ATTN, full)
    x1 = _mm_res("attn_out_proj", att, full["attn_w_o", 0], x0)
    u0, h1, *got = _norm_mm("mlp0_up", x1, row(mlp_norm[0]), full["mlp_w_up", 0],
                            carry=(_Gather, _shards_bf16(w, GATHER_MLP0)))
    _gathered(got, GATHER_MLP0, full)
    x2, a0 = _mlp_down("mlp0_down", u0, full["mlp_w_down", 0], x1)
    z, h2 = _norm_mm("hgrn_in_proj", x2, row(mix_norm[1]), full["hgrn_w_in", 0])
    o_raw, states, *got = _hgrn_fwd(z, hgrn_lower_bounds, carry=(_Gather, _shards_bf16(w, GATHER_HGRN)))
    _gathered(got, GATHER_HGRN, full)
    x3, o2 = _hgrn_out("hgrn_out_proj", o_raw, z, gn_full, full["hgrn_w_o", 0], x2)
    u1, h3 = _norm_mm("mlp1_up", x3, row(mlp_norm[1]), full["mlp_w_up", 1])
    x4, a1 = _mlp_down("mlp1_down", u1, full["mlp_w_down", 1], x3)
    dx4, loss_part, g_final = _loss_head("loss_head", x4, tgt, row(final_norm))

    gw = {}
    du1 = _mlp_bwd_act("mlp1_bwd_act", dx4, u1, full["mlp_w_down", 1])
    dx3, g_mlp1 = _mm_nt_rmsbwd("mlp1_bwd_in", du1, full["mlp_w_up", 1], x3, row(mlp_norm[1]), dx4)
    gw["mlp_w_down", 1] = _mm_tn("mlp1_dw_down", a1, dx4, "rows")
    gw["mlp_w_up", 1] = _mm_tn("mlp1_dw_up", h3, du1, "cols")

    do_raw, dg, g_gn = _hgrn_out_bwd("hgrn_out_bwd", dx3, o_raw, z, full["hgrn_w_o", 0], gn_full)
    gw["hgrn_w_o", 0] = _mm_tn("hgrn_dw_o", o2, dx3, "rows")
    recvs = {}
    dzq, dzf, dzi, g_lb, *recv = _hgrn_bwd(z, hgrn_lower_bounds, states, do_raw,
                                           carry=(_Exchange, [gw[p] for p in GRAD_GROUPS[0]]))
    recvs.update(zip(GRAD_GROUPS[0], recv))
    dz = [dzq, dzf, dzi, dg]
    dx2, g_mix1 = _mm_nt_rmsbwd("hgrn_in_bwd", dz, full["hgrn_w_in", 0], x2, row(mix_norm[1]), dx3)
    gw["hgrn_w_in", 0] = jnp.concatenate(
        [_mm_tn(f"hgrn_dw_in{j}", h2, d, "cols") for j, d in enumerate(dz)], axis=0)

    du0 = _mlp_bwd_act("mlp0_bwd_act", dx2, u0, full["mlp_w_down", 0])
    dx1, g_mlp0 = _mm_nt_rmsbwd("mlp0_bwd_in", du0, full["mlp_w_up", 0], x1, row(mlp_norm[0]), dx2)
    gw["mlp_w_down", 0] = _mm_tn("mlp0_dw_down", a0, dx2, "rows")
    gw["mlp_w_up", 0] = _mm_tn("mlp0_dw_up", h1, du0, "cols")

    datt = _mm_nt("attn_out_bwd", dx1, full["attn_w_o", 0], BF16)
    gw["attn_w_o", 0] = _mm_tn("attn_dw_o", att, dx1, "rows")
    dqkv, g_sink, *recv = _attn_bwd(qkv, rot, attn_sinks, datt, carry=(_Exchange, [gw[p] for p in GRAD_GROUPS[1]]))
    recvs.update(zip(GRAD_GROUPS[1], recv))
    g_qkv = _mm_tn("attn_dw_qkv", h0, dqkv)
    g_qkv = g_qkv.reshape(D_MODEL, N_DEV, QKV_DIM // N_DEV).transpose(1, 0, 2).astype(BF16)
    dx0, g_mix0, g_bqkv, recvs["attn_w_qkv", 0] = _mm_nt_rmsbwd(
        "qkv_bwd", dqkv, w_qkv, x0, row(mix_norm[0]), dx1, with_colsum=True, carry=(_Exchange, [g_qkv]))

    big = {name: _adamw_sum("adamw_" + name, [recvs[name, l] for l in range(w[name].shape[0])], w[name], m[name], v[name])
           for name in BIG_NAMES}

    zero_row = jnp.zeros((1, D_MODEL), F32)
    part = _pack_small(dict(
        mix_norm=jnp.concatenate([g_mix0, g_mix1], axis=0), mlp_norm=jnp.concatenate([g_mlp0, g_mlp1], axis=0),
        final_norm=g_final, attn_b_qkv=g_bqkv, attn_sinks=g_sink[:, :N_Q_HEADS],
        hgrn_lower_bounds=jnp.concatenate([zero_row, g_lb], axis=0)), g_gn)

    def spread(a):
        return lax.dynamic_update_slice(zero_row, a.reshape(1, 128), (0, me * 128))

    small_in = [_pack_small({n: d[n] for n in SMALL_NAMES if n != "hgrn_g_norm"}, spread(d["hgrn_g_norm"]))
                for d in (w, m, v)]
    small = [_unpack_small(p, me) for p in _small_sync(part, *small_in)]

    loss = lax.psum(loss_part[0, 0], ("x", "y", "c"))
    outs = [loss, dx0.reshape(x.shape)]
    for kind, grp_small in enumerate(small):
        for name in WEIGHT_NAMES:
            val = grp_small[name] if name in SMALL_NAMES else big[name][kind]
            outs.append(val.reshape(w[name].shape))
    return tuple(outs)
```

```python
import functools

import jax
import jax.numpy as jnp
from jax import lax
from jax.experimental import pallas as pl
from jax.experimental.pallas import tpu as pltpu

F32 = jnp.float32
BF16 = jnp.bfloat16

D_MODEL = 1024
HEAD_DIM = 64
N_Q_HEADS = 16
Q_DIM = 1024
KV_DIM = 256
QKV_DIM = 1536
ATT_BLOCK = 128
ROT_HALF = 8
ROPE_THETA = 500000.0
NEG_INF = -1e30
HGRN_HEADS = 8
HGRN_DK = 128
CHUNK = 64
D_FF = 4096
NORM_EPS = 1e-5
N_DEV = 8

ADAM_LR = 0.001
ADAM_B1 = 0.9
ADAM_B2 = 0.999
ADAM_EPS = 1e-08
ADAM_WD = 0.01
ADAM_STEP = 10

LANES = 128
VMEM_LIMIT = 56 * 1024 * 1024

GATHER_FIRST = (("attn_w_qkv", 0),)
GATHER_ATTN = (("attn_w_o", 0), ("mlp_w_up", 0), ("mlp_w_down", 0))
GATHER_MLP0 = (("hgrn_w_in", 0), ("hgrn_w_o", 0))
GATHER_HGRN = (("mlp_w_up", 1), ("mlp_w_down", 1))
GRADS_HGRN = (("mlp_w_down", 1), ("mlp_w_up", 1), ("hgrn_w_o", 0))
GRADS_MLP0 = (("hgrn_w_in", 0),)
GRADS_ATTN = (("mlp_w_down", 0), ("mlp_w_up", 0), ("attn_w_o", 0))
COL_SHARDED = ("attn_w_qkv", "hgrn_w_in", "mlp_w_up")
BIG_NAMES = ("attn_w_qkv", "attn_w_o", "hgrn_w_in", "hgrn_w_o", "mlp_w_up", "mlp_w_down")
SMALL_ROWS = 16


def _dot(a, b):
    return jnp.dot(a, b, preferred_element_type=F32)


def _dot_nt(a, b):
    return lax.dot_general(a, b, (((1,), (1,)), ((), ())), preferred_element_type=F32)


def _dot_tn(a, b):
    return lax.dot_general(a, b, (((0,), (0,)), ((), ())), preferred_element_type=F32)


def _params(**kw):
    return pltpu.CompilerParams(vmem_limit_bytes=VMEM_LIMIT, **kw)


def _full_spec(a):
    nd = a.ndim
    return pl.BlockSpec(a.shape, lambda *_: (0,) * nd)


def _row_call(name, body, n_rows, tm, row_ins, full_ins, row_outs, acc_outs=(), carry=(None, None)):
    steps = n_rows // tm
    in_specs = [pl.BlockSpec((tm, w), functools.partial(lambda i, cb: (i, cb), cb=cb)) for _, w, cb in row_ins]
    in_specs += [_full_spec(a) for a in full_ins]
    out_shape = [jax.ShapeDtypeStruct((n_rows, w), dt) for w, dt in row_outs]
    out_specs = [pl.BlockSpec((tm, w), lambda i: (i, 0)) for w, _ in row_outs]
    for shp, dt in acc_outs:
        out_shape.append(jax.ShapeDtypeStruct(shp, dt))
        out_specs.append(pl.BlockSpec(shp, functools.partial(lambda i, nd: (0,) * nd, nd=len(shp))))
    n_in, n_out = len(in_specs), len(out_specs)
    in_specs, out_specs, out_shape, scratch, extra = _carried_specs(carry, in_specs, out_specs, out_shape, [])

    def wrapped(*refs):
        i = pl.program_id(0)
        own, finish = _carried(carry, refs, n_in, n_out, i == 0, i == steps - 1)
        body(*own)
        finish()

    return pl.pallas_call(
        wrapped, name=name, grid=(steps,), in_specs=in_specs, out_specs=out_specs, out_shape=out_shape,
        scratch_shapes=scratch, compiler_params=_params(dimension_semantics=("arbitrary",)),
    )(*[a for a, _, _ in row_ins], *full_ins, *extra)


def _rms(x, gain):
    r = lax.rsqrt(jnp.mean(x * x, axis=-1, keepdims=True) + NORM_EPS)
    xhat = x * r
    return xhat * gain, xhat, r


def _rms_bwd(dy, xhat, r, gain):
    dxhat = dy * gain
    dx = r * (dxhat - xhat * jnp.mean(dxhat * xhat, axis=-1, keepdims=True))
    return dx, dy * xhat


def _norm_mm(name, x, gain, w, bias=None, rot=None, tm=256, carry=(None, None)):
    T = x.shape[0]
    tm = min(tm, T)
    nc = 512
    blocked = w.ndim == 3
    n = N_DEV * w.shape[2] if blocked else w.shape[1]
    assert n % nc == 0 and (not blocked or w.shape[2] == nc)

    def body(*refs):
        x_ref, refs = refs[0], refs[1:]
        if rot is not None:
            t_ref, refs = refs[0], refs[1:]
        g_ref, w_ref, refs = refs[0], refs[1], refs[2:]
        if bias is not None:
            b_ref, refs = refs[0], refs[1:]
        y_ref, h_ref = refs
        h, _, _ = _rms(x_ref[...], g_ref[...])
        hb = h.astype(BF16)
        h_ref[...] = hb
        for c in range(n // nc):
            sl = slice(c * nc, (c + 1) * nc)
            y = _dot(hb, w_ref[c] if blocked else w_ref[:, sl])
            if bias is not None:
                y = y + b_ref[:, sl]
            if rot is None:
                y_ref[:, sl] = y
            else:
                n_rot = max(0, min(nc, Q_DIM + KV_DIM - c * nc)) // LANES
                pieces = _rot_fwd(y[:, :n_rot * LANES], t_ref[...]) if n_rot else []
                for j in range(nc // LANES):
                    col = slice(c * nc + j * LANES, c * nc + (j + 1) * LANES)
                    y_ref[:, col] = pieces[j] if j < n_rot else y[:, j * LANES:(j + 1) * LANES]

    rows = [(x, D_MODEL, 0)] + ([(rot, 3 * LANES, 0)] if rot is not None else [])
    full = [gain, w] + ([bias] if bias is not None else [])
    return _row_call(name, body, T, tm, rows, full, [(n, F32), (D_MODEL, BF16)], carry=carry)


def _mm_res(name, a, w, res, tm=512):
    T = a.shape[0]
    tm = min(tm, T)

    def body(a_ref, r_ref, w_ref, o_ref):
        o_ref[...] = r_ref[...] + _dot(a_ref[...], w_ref[...])

    return _row_call(name, body, T, tm, [(a, a.shape[1], 0), (res, D_MODEL, 0)], [w], [(D_MODEL, F32)])[0]


def _mlp_down(name, u, w, res, tm=256, loss_head=None):
    T = u.shape[0]
    tm = min(tm, T)
    kc = 1024

    def body(*refs):
        if loss_head is None:
            u_ref, r_ref, w_ref, o_ref, a_ref = refs
        else:
            u_ref, r_ref, t_ref, w_ref, g_ref, o_ref, a_ref, loss_ref, dg_ref = refs

            @pl.when(pl.program_id(0) == 0)
            def _():
                loss_ref[...] = jnp.zeros_like(loss_ref)
                dg_ref[...] = jnp.zeros_like(dg_ref)

        acc = r_ref[...]
        for c in range(D_FF // kc):
            sl = slice(c * kc, (c + 1) * kc)
            a = jnp.maximum(u_ref[:, sl], 0.0)
            ab = (a * a).astype(BF16)
            a_ref[:, sl] = ab
            acc = acc + _dot(ab, w_ref[sl, :])
        if loss_head is None:
            o_ref[...] = acc
        else:
            gain_v = g_ref[...]
            y, xhat, r = _rms(acc, gain_v)
            diff = y - t_ref[...]
            per_row = jnp.sum(diff * diff, axis=-1, keepdims=True) * (1.0 / D_MODEL)
            loss_ref[...] += jnp.broadcast_to(0.5 * jnp.sum(per_row, axis=0, keepdims=True), loss_ref.shape)
            dx, dgr = _rms_bwd(diff * (1.0 / D_MODEL), xhat, r, gain_v)
            o_ref[...] = dx
            dg_ref[...] += jnp.sum(dgr, axis=0, keepdims=True)

    rows, full, acc_outs = [(u, D_FF, 0), (res, D_MODEL, 0)], [w], []
    if loss_head is not None:
        rows, full = rows + [(loss_head[0], D_MODEL, 0)], full + [loss_head[1]]
        acc_outs = [((1, LANES), F32), ((1, D_MODEL), F32)]
    return _row_call(name, body, T, tm, rows, full, [(D_MODEL, F32), (D_FF, BF16)], acc_outs)


def _hgrn_out(name, o_raw, z, gn, w, res, tm=256):
    T = o_raw.shape[0]
    tm = min(tm, T)

    def body(o_ref, g_ref, r_ref, gn_ref, w_ref, x_ref, a_ref):
        y, _, _ = _rms(o_ref[...], gn_ref[...])
        g = g_ref[...]
        a = (y * (g * jax.nn.sigmoid(g))).astype(BF16)
        a_ref[...] = a
        x_ref[...] = r_ref[...] + _dot(a, w_ref[...])

    return _row_call(name, body, T, tm, [(o_raw, D_MODEL, 0), (z, D_MODEL, 3), (res, D_MODEL, 0)], [gn, w],
                     [(D_MODEL, F32), (D_MODEL, BF16)])


def _mm_nt_rmsbwd(name, dy, w, x, gain, dres, tm=256, with_colsum=False, carry=(None, None)):
    T = x.shape[0]
    tm = min(tm, T)
    dys = list(dy) if isinstance(dy, (list, tuple)) else [dy]
    width = dys[0].shape[1]
    n = width * len(dys)
    assert not with_colsum or len(dys) == 1

    def body(*refs):
        dy_refs, refs = refs[:len(dys)], refs[len(dys):]
        if with_colsum:
            x_ref, dr_ref, w_ref, g_ref, dx_ref, dg_ref, cs_ref = refs
        else:
            x_ref, dr_ref, w_ref, g_ref, dx_ref, dg_ref = refs

        @pl.when(pl.program_id(0) == 0)
        def _():
            dg_ref[...] = jnp.zeros_like(dg_ref)
            if with_colsum:
                cs_ref[...] = jnp.zeros_like(cs_ref)

        if w.ndim == 3:
            nb = w.shape[2]
            dh = None
            for p in range(N_DEV):
                piece, off = divmod(p * nb, width)
                part = _dot_nt(dy_refs[piece][:, off:off + nb].astype(BF16), w_ref[p])
                dh = part if dh is None else dh + part
        else:
            dh = _dot_nt(dy_refs[0][...].astype(BF16), w_ref[...])
        gain_v = g_ref[...]
        _, xhat, r = _rms(x_ref[...], gain_v)
        dx, dgr = _rms_bwd(dh, xhat, r, gain_v)
        dx_ref[...] = dr_ref[...] + dx
        dg_ref[...] += jnp.sum(dgr, axis=0, keepdims=True)
        if with_colsum:
            cs_ref[...] += jnp.sum(dy_refs[0][...].astype(F32), axis=0, keepdims=True)

    acc = [((1, D_MODEL), F32)] + ([((1, n), F32)] if with_colsum else [])
    rows = [(d, width, 0) for d in dys] + [(x, D_MODEL, 0), (dres, D_MODEL, 0)]
    return _row_call(name, body, T, tm, rows, [w, gain], [(D_MODEL, F32)], acc, carry=carry)


def _mm_nt(name, dy, w, out_dtype, tm=512):
    T = dy.shape[0]
    tm = min(tm, T)
    k = w.shape[0]

    def body(dy_ref, w_ref, o_ref):
        o_ref[...] = _dot_nt(dy_ref[...].astype(BF16), w_ref[...]).astype(out_dtype)

    return _row_call(name, body, T, tm, [(dy, dy.shape[1], 0)], [w], [(k, out_dtype)])[0]


def _mlp_bwd_act(name, dy, u, w_down, tm=256, carry=(None, None)):
    T = u.shape[0]
    tm = min(tm, T)
    kc = 1024

    def body(dy_ref, u_ref, w_ref, du_ref):
        dyb = dy_ref[...].astype(BF16)
        for c in range(D_FF // kc):
            sl = slice(c * kc, (c + 1) * kc)
            da = _dot_nt(dyb, w_ref[sl, :])
            du_ref[:, sl] = (da * (2.0 * jnp.maximum(u_ref[:, sl], 0.0))).astype(BF16)

    return _row_call(name, body, T, tm, [(dy, D_MODEL, 0), (u, D_FF, 0)], [w_down], [(D_FF, BF16)], carry=carry)


def _hgrn_out_bwd(name, dx, o_raw, z, w, gn, tm=256):
    T = dx.shape[0]
    tm = min(tm, T)

    def body(dx_ref, o_ref, g_ref, w_ref, gn_ref, do_ref, dg_ref, dgn_ref):
        @pl.when(pl.program_id(0) == 0)
        def _():
            dgn_ref[...] = jnp.zeros_like(dgn_ref)

        da = _dot_nt(dx_ref[...].astype(BF16), w_ref[...])
        gn_v = gn_ref[...]
        y, xhat, r = _rms(o_ref[...], gn_v)
        g = g_ref[...]
        sg = jax.nn.sigmoid(g)
        dg_ref[...] = (da * y * (sg * (1.0 + g * (1.0 - sg)))).astype(BF16)
        dyn = da * (g * sg)
        do, dgr = _rms_bwd(dyn, xhat, r, gn_v)
        do_ref[...] = do
        dgn_ref[...] += jnp.sum(dgr, axis=0, keepdims=True)

    return _row_call(name, body, T, tm, [(dx, D_MODEL, 0), (o_raw, D_MODEL, 0), (z, D_MODEL, 3)], [w, gn],
                     [(D_MODEL, F32), (D_MODEL, BF16)], [((1, D_MODEL), F32)])


def _mm_tn(name, a, b, shard=None, bm=1024, bn=512, tk=2048):
    T, M = a.shape
    N = b.shape[1]
    bm, bn, tk = min(bm, M), min(bn, N), min(tk, T)
    nk = T // tk
    if shard is None:
        out_shape, out_block = jax.ShapeDtypeStruct((M, N), F32), (bm, bn)
        out_map = lambda i, j, k: (i, j)
    elif shard == "cols":
        assert N % bn == 0
        out_shape, out_block = jax.ShapeDtypeStruct((N // bn, M, bn), BF16), (1, bm, bn)
        out_map = lambda i, j, k: (j, i, 0)
    else:
        rows = M // N_DEV
        assert bm % rows == 0
        out_shape, out_block = jax.ShapeDtypeStruct((N_DEV, rows, N), BF16), (bm // rows, rows, bn)
        out_map = lambda i, j, k: (i, 0, j)

    def body(a_ref, b_ref, o_ref, acc):
        k = pl.program_id(2)

        @pl.when(k == 0)
        def _():
            acc[...] = jnp.zeros_like(acc)

        acc[...] += _dot_tn(a_ref[...].astype(BF16), b_ref[...].astype(BF16))

        @pl.when(k == nk - 1)
        def _():
            o_ref[...] = acc[...].reshape(out_block).astype(o_ref.dtype)

    return pl.pallas_call(
        body, name=name, grid=(M // bm, N // bn, nk),
        in_specs=[pl.BlockSpec((tk, bm), lambda i, j, k: (k, i)), pl.BlockSpec((tk, bn), lambda i, j, k: (k, j))],
        out_specs=pl.BlockSpec(out_block, out_map), out_shape=out_shape,
        scratch_shapes=[pltpu.VMEM((bm, bn), F32)],
        compiler_params=_params(dimension_semantics=("parallel", "parallel", "arbitrary")),
    )(a, b)


def _rot_fwd(x, tab):
    c, sa, sb = tab[:, :LANES], tab[:, LANES:2 * LANES], tab[:, 2 * LANES:]
    outs = []
    for j in range(x.shape[1] // LANES):
        xs = x[:, j * LANES:(j + 1) * LANES]
        outs.append(xs * c + pltpu.roll(xs, ROT_HALF, 1) * sa + pltpu.roll(xs, LANES - ROT_HALF, 1) * sb)
    return outs


def _rot_bwd(dys, tab):
    c, sa, sb = tab[:, :LANES], tab[:, LANES:2 * LANES], tab[:, 2 * LANES:]
    return [dy * c + pltpu.roll(dy * sa, LANES - ROT_HALF, 1) + pltpu.roll(dy * sb, ROT_HALF, 1) for dy in dys]


ATT_SCALE = HEAD_DIM ** -0.5


def _attn_masks(n):
    kj = lax.broadcasted_iota(jnp.int32, (2 * ATT_BLOCK, ATT_BLOCK), 0)
    qi = lax.broadcasted_iota(jnp.int32, (2 * ATT_BLOCK, ATT_BLOCK), 1)
    delta = qi + ATT_BLOCK - kj
    first_key = jnp.where(n > 0, 0, ATT_BLOCK)
    valid = (delta >= 0) & (delta < ATT_BLOCK) & (kj >= first_key)
    low = lax.broadcasted_iota(jnp.int32, (1, LANES), 1) < HEAD_DIM
    upper = lax.broadcasted_iota(jnp.int32, (LANES, 1), 0) < HEAD_DIM
    return valid, low, upper


def _softmax_sink(s, valid, sink):
    s = jnp.where(valid, s, NEG_INF)
    m = jnp.maximum(jnp.max(s, axis=0, keepdims=True), sink)
    e = jnp.exp(s - m)
    es = jnp.exp(sink - m)
    inv = 1.0 / (jnp.sum(e, axis=0, keepdims=True) + es)
    return e * inv, es * inv


def _attn_specs(nb, tables):
    prev = lambda n: jnp.maximum(jnp.minimum(n, nb - 1) - 1, 0)
    cur = lambda n: jnp.minimum(n, nb - 1)
    specs = [
        pl.BlockSpec((ATT_BLOCK, Q_DIM), lambda n: (cur(n), 0)),
        pl.BlockSpec((ATT_BLOCK, KV_DIM), lambda n: (prev(n), 4)),
        pl.BlockSpec((ATT_BLOCK, KV_DIM), lambda n: (cur(n), 4)),
        pl.BlockSpec((ATT_BLOCK, KV_DIM), lambda n: (prev(n), 5)),
        pl.BlockSpec((ATT_BLOCK, KV_DIM), lambda n: (cur(n), 5)),
    ]
    if tables:
        specs += [pl.BlockSpec((ATT_BLOCK, 3 * LANES), lambda n: (prev(n), 0)),
                  pl.BlockSpec((ATT_BLOCK, 3 * LANES), lambda n: (cur(n), 0))]
    return specs + [pl.BlockSpec(memory_space=pltpu.SMEM)]


def _kv_band(prev_ref, cur_ref):
    out = []
    for j in range(KV_DIM // LANES):
        sl = slice(j * LANES, (j + 1) * LANES)
        band = jnp.concatenate([prev_ref[:, sl], cur_ref[:, sl]], axis=0)
        out.append((band, pltpu.roll(band, HEAD_DIM, 1)))
    return out


def _bf16(bands, transposed=False):
    return [[(a.T if transposed else a).astype(BF16) for a in pair] for pair in bands]


def _attn_fwd(qkv, sinks, carry=(None, None)):
    T = qkv.shape[0]
    nb = T // ATT_BLOCK

    def body(*refs):
        n = pl.program_id(0)
        own, finish = _carried(carry, refs, 6, 1, n == 0, n == nb - 1)
        q_ref, kp_ref, kc_ref, vp_ref, vc_ref, sink_ref, o_ref = own
        valid, low, upper = _attn_masks(n)
        ks = _bf16(_kv_band(kp_ref, kc_ref))
        vts = _bf16(_kv_band(vp_ref, vc_ref), transposed=True)
        heads = []
        for p in range(Q_DIM // LANES):
            kpair, khalf = p // 4, (p // 2) % 2
            q_pair = q_ref[:, p * LANES:(p + 1) * LANES] * ATT_SCALE
            for hf in range(2):
                qm = jnp.where(low if hf == 0 else ~low, q_pair, 0.0).astype(BF16)
                sw = 0 if khalf == hf else 1
                heads.append((2 * p + hf, kpair, sw, _dot_nt(ks[kpair][sw], qm)))
        probs = [_softmax_sink(s, valid, sink_ref[0, h])[0].astype(BF16) for h, _, _, s in heads]
        outs = [_dot(vts[kpair][sw], pr) for (_, kpair, sw, _), pr in zip(heads, probs)]
        for p in range(Q_DIM // LANES):
            o_ref[:, p * LANES:(p + 1) * LANES] = jnp.where(upper, outs[2 * p], outs[2 * p + 1]).T.astype(BF16)
        finish()

    in_specs, out_specs, out_shape, scratch, extra = _carried_specs(
        carry, _attn_specs(nb, False), [pl.BlockSpec((ATT_BLOCK, Q_DIM), lambda n: (n, 0))],
        [jax.ShapeDtypeStruct((T, Q_DIM), BF16)], [])
    return pl.pallas_call(
        body, name="attn_fwd", grid=(nb,), in_specs=in_specs, out_specs=out_specs, out_shape=out_shape,
        scratch_shapes=scratch, compiler_params=_params(dimension_semantics=("arbitrary",)),
    )(qkv, qkv, qkv, qkv, qkv, sinks, *extra)


def _attn_bwd(qkv, rot, sinks, dout, carry=(None, None)):
    T = qkv.shape[0]
    nb = T // ATT_BLOCK
    npair = KV_DIM // LANES

    def body(*refs):
        n = pl.program_id(0)
        own, finish = _carried(carry, refs, 9, 2, n == 0, n == nb)
        (q_ref, kp_ref, kc_ref, vp_ref, vc_ref, tp_ref, tc_ref, sink_ref, do_ref, dqkv_ref, dsink_ref,
         dq_c, dk_c, dv_c) = own

        @pl.when(n == 0)
        def _():
            dq_c[...] = jnp.zeros_like(dq_c)
            dk_c[...] = jnp.zeros_like(dk_c)
            dv_c[...] = jnp.zeros_like(dv_c)
            dsink_ref[...] = jnp.zeros_like(dsink_ref)

        def flush(dk_prev, dv_prev, tab_ref):
            dqkv_ref[:, :Q_DIM] = dq_c[...]
            dk = _rot_bwd([dk_c[:, j * LANES:(j + 1) * LANES] + dk_prev[j] for j in range(npair)], tab_ref[...])
            for j in range(npair):
                dqkv_ref[:, Q_DIM + j * LANES:Q_DIM + (j + 1) * LANES] = dk[j]
                dqkv_ref[:, Q_DIM + KV_DIM + j * LANES:Q_DIM + KV_DIM + (j + 1) * LANES] = (
                    dv_c[:, j * LANES:(j + 1) * LANES] + dv_prev[j])

        @pl.when(n < nb)
        def _():
            valid, low, upper = _attn_masks(n)
            lane = lax.broadcasted_iota(jnp.int32, (1, LANES), 1)
            k_band = _kv_band(kp_ref, kc_ref)
            ks, kts = _bf16(k_band), _bf16(k_band, transposed=True)
            vs = _bf16(_kv_band(vp_ref, vc_ref))
            dk_acc = [[jnp.zeros((2 * ATT_BLOCK, LANES), F32) for _ in range(2)] for _ in range(npair)]
            dv_acc = [[jnp.zeros((2 * ATT_BLOCK, LANES), F32) for _ in range(2)] for _ in range(npair)]
            dsink = jnp.zeros((1, LANES), F32)
            heads = []
            for p in range(Q_DIM // LANES):
                kpair, khalf = p // 4, (p // 2) % 2
                q_pair = q_ref[:, p * LANES:(p + 1) * LANES] * ATT_SCALE
                do_pair = do_ref[:, p * LANES:(p + 1) * LANES]
                for hf in range(2):
                    sel = low if hf == 0 else ~low
                    qm = jnp.where(sel, q_pair, 0.0).astype(BF16)
                    dom = jnp.where(sel, do_pair, 0.0).astype(BF16)
                    sw = 0 if khalf == hf else 1
                    heads.append((2 * p + hf, kpair, sw, qm, dom,
                                  _dot_nt(ks[kpair][sw], qm), _dot_nt(vs[kpair][sw], dom)))
            grads = []
            for h, kpair, sw, qm, dom, s, dp in heads:
                pr, ps = _softmax_sink(s, valid, sink_ref[0, h])
                dd = jnp.sum(pr * dp, axis=0, keepdims=True)
                dsink = dsink + jnp.where(lane == h, -jnp.sum(ps * dd, axis=1, keepdims=True), 0.0)
                grads.append((pr * (dp - dd)).astype(BF16))
                heads[h] = (kpair, sw, qm, dom, pr.astype(BF16))
            dq_t = []
            for (kpair, sw, qm, dom, pr), ds in zip(heads, grads):
                dq_t.append(_dot(kts[kpair][sw], ds))
                dk_acc[kpair][sw] = dk_acc[kpair][sw] + _dot(ds, qm)
                dv_acc[kpair][sw] = dv_acc[kpair][sw] + _dot(pr, dom)
            dqs = [jnp.where(upper, dq_t[2 * p], dq_t[2 * p + 1]).T * ATT_SCALE for p in range(Q_DIM // LANES)]
            dk_acc = [a[0] + pltpu.roll(a[1], HEAD_DIM, 1) for a in dk_acc]
            dv_acc = [a[0] + pltpu.roll(a[1], HEAD_DIM, 1) for a in dv_acc]
            flush([a[:ATT_BLOCK] for a in dk_acc], [a[:ATT_BLOCK] for a in dv_acc], tp_ref)
            dq = _rot_bwd(dqs, tc_ref[...])
            for p in range(Q_DIM // LANES):
                dq_c[:, p * LANES:(p + 1) * LANES] = dq[p]
            for j in range(npair):
                dk_c[:, j * LANES:(j + 1) * LANES] = dk_acc[j][ATT_BLOCK:]
                dv_c[:, j * LANES:(j + 1) * LANES] = dv_acc[j][ATT_BLOCK:]
            dsink_ref[...] += dsink

        @pl.when(n == nb)
        def _():
            zero = [jnp.zeros((ATT_BLOCK, LANES), F32) for _ in range(npair)]
            flush(zero, zero, tc_ref)

        finish()

    do_spec = pl.BlockSpec((ATT_BLOCK, Q_DIM), lambda n: (jnp.minimum(n, nb - 1), 0))
    in_specs, out_specs, out_shape, scratch, extra = _carried_specs(
        carry, _attn_specs(nb, True) + [do_spec],
        [pl.BlockSpec((ATT_BLOCK, QKV_DIM), lambda n: (jnp.maximum(n - 1, 0), 0)),
         pl.BlockSpec((1, LANES), lambda n: (0, 0))],
        [jax.ShapeDtypeStruct((T, QKV_DIM), F32), jax.ShapeDtypeStruct((1, LANES), F32)],
        [pltpu.VMEM((ATT_BLOCK, Q_DIM), F32), pltpu.VMEM((ATT_BLOCK, KV_DIM), F32),
         pltpu.VMEM((ATT_BLOCK, KV_DIM), F32)])
    return pl.pallas_call(
        body, name="attn_bwd", grid=(nb + 1,), in_specs=in_specs, out_specs=out_specs, out_shape=out_shape,
        scratch_shapes=scratch, compiler_params=_params(dimension_semantics=("arbitrary",)),
    )(qkv, qkv, qkv, qkv, qkv, rot, rot, sinks, dout, *extra)


LEVELS = (32, 16, 8)
DIAG = 8
SUBLANES = 8
UNROLL = 4
UNROLL_BWD = 2


def _lower_bound(lb_ref):
    l0, l1 = lb_ref[0:1, :], lb_ref[1:2, :]
    mx = jnp.maximum(l0, l1)
    e0, e1 = jnp.exp(l0 - mx), jnp.exp(l1 - mx)
    return e1 / (e0 + e1)


GROUPS = CHUNK // SUBLANES


def _group_roll(x, k):
    return pltpu.roll(x.reshape(GROUPS, SUBLANES, HGRN_DK), k % SUBLANES, 1).reshape(CHUNK, HGRN_DK)


def _scan_rows(x, row, reverse):
    r8 = row & (SUBLANES - 1)
    for sh in (1, 2, 4):
        ok = (r8 < SUBLANES - sh) if reverse else (r8 >= sh)
        x = x + jnp.where(ok, _group_roll(x, -sh if reverse else sh), 0.0)
    g = x.reshape(GROUPS, SUBLANES, HGRN_DK)
    edge = 0 if reverse else SUBLANES - 1
    tot = jnp.broadcast_to(g[:, edge:edge + 1, :], g.shape)

    def shifted(a, n):
        z = jnp.zeros((n, SUBLANES, HGRN_DK), F32)
        return jnp.concatenate([a[n:], z] if reverse else [z, a[:GROUPS - n]], axis=0)

    acc = shifted(tot, 1)
    for sh in (1, 2, 4):
        acc = acc + shifted(acc, sh)
    return (g + acc).reshape(CHUNK, HGRN_DK)


def _level_masks():
    t = lax.broadcasted_iota(jnp.int32, (CHUNK, CHUNK), 0)
    s = lax.broadcasted_iota(jnp.int32, (CHUNK, CHUNK), 1)
    return [((t & h) != 0) & ((s & h) == 0) & ((t ^ s) < 2 * h) for h in LEVELS]


def _level_scales(b):
    out = []
    for h in LEVELS:
        parts = [jnp.broadcast_to(b[j * 2 * h + h - 1:j * 2 * h + h, :], (2 * h, HGRN_DK))
                 for j in range(CHUNK // (2 * h))]
        mid = parts[0] if len(parts) == 1 else jnp.concatenate(parts, axis=0)
        out.append(jnp.exp(-jnp.abs(b - mid)))
    return out


def _hgrn_gates(zq, zf, lb):
    sq = jax.nn.sigmoid(zq)
    q = zq * sq
    sg = jax.nn.sigmoid(zf)
    forget = lb + (1.0 - lb) * sg
    return q, sq, sg, forget, 1.0 - forget, jnp.log(forget)


def _hgrn_specs(T, rb, rev):
    nr = T // rb
    ri = (lambda r: nr - 1 - r) if rev else (lambda r: r)
    return nr, ri, [
        pl.BlockSpec((rb, HGRN_DK), lambda h, r: (ri(r), h)),
        pl.BlockSpec((rb, HGRN_DK), lambda h, r: (ri(r), HGRN_HEADS + h)),
        pl.BlockSpec((rb, HGRN_DK), lambda h, r: (ri(r), 2 * HGRN_HEADS + h)),
        pl.BlockSpec((2, HGRN_DK), lambda h, r: (0, h)),
    ]


def _hgrn_fwd(z, lb_raw, rb=1024, carry=(None, None)):
    T = z.shape[0]
    rb = min(rb, T)
    ncb = rb // CHUNK
    nr, ri, in_specs = _hgrn_specs(T, rb, False)

    def body(*refs):
        hh, rr = pl.program_id(0), pl.program_id(1)
        own, finish = _carried(carry, refs, 4, 2, (hh == 0) & (rr == 0), (hh == HGRN_HEADS - 1) & (rr == nr - 1))
        zq_ref, zf_ref, zi_ref, lb_ref, o_ref, st_ref, state = own

        @pl.when(rr == 0)
        def _():
            state[...] = jnp.zeros_like(state)

        lb = _lower_bound(lb_ref)
        row = lax.broadcasted_iota(jnp.int32, (CHUNK, HGRN_DK), 0)
        masks = _level_masks()
        rd = row & (DIAG - 1)

        def chunk(c, st):
            rows = pl.ds(pl.multiple_of(c * CHUNK, CHUNK), CHUNK)
            q, _, _, _, k, lf = _hgrn_gates(zq_ref[rows, :], zf_ref[rows, :], lb)
            v = zi_ref[rows, :]
            vb = v.astype(BF16)
            b = _scan_rows(lf, row, False)
            sc = jnp.zeros((CHUNK, CHUNK), F32)
            for e, mask in zip(_level_scales(b), masks):
                sc = sc + jnp.where(mask, _dot_nt((q * e).astype(BF16), (k * e).astype(BF16)), 0.0)
            o = _dot(sc.astype(BF16), vb) + jnp.sum(q * k, axis=-1, keepdims=True) * v
            for d in range(1, DIAG):
                w = jnp.where(rd >= d, q * _group_roll(k, d) * jnp.exp(b - _group_roll(b, d)), 0.0)
                o = o + jnp.sum(w, axis=-1, keepdims=True) * _group_roll(v, d)
            b_last = b[CHUNK - 1:CHUNK, :]
            kd = (k * jnp.exp(b_last - b)).astype(BF16)
            qd = (q * jnp.exp(b)).astype(BF16)
            st_ref[c, 0] = st
            o_ref[rows, :] = o + _dot_nt(qd, st.astype(BF16))
            return st * jnp.exp(b_last) + _dot_tn(vb, kd)

        def group(i, st):
            for j in range(UNROLL):
                st = chunk(i * UNROLL + j, st)
            return st

        state[...] = lax.fori_loop(0, ncb // UNROLL, group, state[...])
        finish()

    in_specs, out_specs, out_shape, scratch, extra = _carried_specs(
        carry, in_specs,
        [pl.BlockSpec((rb, HGRN_DK), lambda h, r: (r, h)),
         pl.BlockSpec((ncb, 1, HGRN_DK, HGRN_DK), lambda h, r: (r, h, 0, 0))],
        [jax.ShapeDtypeStruct((T, D_MODEL), F32),
         jax.ShapeDtypeStruct((T // CHUNK, HGRN_HEADS, HGRN_DK, HGRN_DK), F32)],
        [pltpu.VMEM((HGRN_DK, HGRN_DK), F32)])
    return pl.pallas_call(
        body, name="hgrn_fwd", grid=(HGRN_HEADS, nr), in_specs=in_specs, out_specs=out_specs, out_shape=out_shape,
        scratch_shapes=scratch, compiler_params=_params(dimension_semantics=("arbitrary", "arbitrary")),
    )(z, z, z, lb_raw, *extra)


def _hgrn_bwd(z, lb_raw, states, do, rb=1024, carry=(None, None)):
    T = z.shape[0]
    rb = min(rb, T)
    ncb = rb // CHUNK
    nr, ri, in_specs = _hgrn_specs(T, rb, True)
    in_specs += [pl.BlockSpec((ncb, 1, HGRN_DK, HGRN_DK), lambda h, r: (ri(r), h, 0, 0)),
                 pl.BlockSpec((rb, HGRN_DK), lambda h, r: (ri(r), h))]

    def body(*refs):
        hh, rr = pl.program_id(0), pl.program_id(1)
        own, finish = _carried(carry, refs, 6, 4, (hh == 0) & (rr == 0), (hh == HGRN_HEADS - 1) & (rr == nr - 1))
        zq_ref, zf_ref, zi_ref, lb_ref, st_ref, do_ref, dq_ref, df_ref, di_ref, dlb_ref, dstate = own

        @pl.when(rr == 0)
        def _():
            dstate[...] = jnp.zeros_like(dstate)
            dlb_ref[...] = jnp.zeros_like(dlb_ref)

        lb = _lower_bound(lb_ref)
        row = lax.broadcasted_iota(jnp.int32, (CHUNK, HGRN_DK), 0)
        masks = _level_masks()
        rd = row & (DIAG - 1)

        def chunk(ci, dlb):
            c = ncb - 1 - ci
            rows = pl.ds(pl.multiple_of(c * CHUNK, CHUNK), CHUNK)
            zq = zq_ref[rows, :]
            q, sq, sg, forget, k, lf = _hgrn_gates(zq, zf_ref[rows, :], lb)
            v = zi_ref[rows, :]
            dov = do_ref[rows, :]
            b = _scan_rows(lf, row, False)
            st = st_ref[c, 0]
            dst = dstate[...]
            b_last = b[CHUNK - 1:CHUNK, :]
            eb = jnp.exp(b)
            ebb = jnp.exp(b_last - b)
            e_last = jnp.exp(b_last)
            dob, vb, stb, dstb = dov.astype(BF16), v.astype(BF16), st.astype(BF16), dst.astype(BF16)
            dq = eb * _dot(dob, stb)
            dv = _dot_nt((k * ebb).astype(BF16), dstb)
            dk = ebb * _dot(vb, dstb)
            extra = e_last * jnp.sum(dst * st, axis=0, keepdims=True) + jnp.sum(k * dk, axis=0, keepdims=True)
            da = _dot_nt(dob, vb)
            sc = jnp.zeros((CHUNK, CHUNK), F32)
            for e, mask in zip(_level_scales(b), masks):
                qs, ks = (q * e).astype(BF16), (k * e).astype(BF16)
                dam = jnp.where(mask, da, 0.0).astype(BF16)
                dq = dq + e * _dot(dam, ks)
                dk = dk + e * _dot_tn(dam, qs)
                sc = sc + jnp.where(mask, _dot_nt(qs, ks), 0.0)
            dv = dv + _dot_tn(sc.astype(BF16), dob)
            dad = jnp.sum(dov * v, axis=-1, keepdims=True)
            dq = dq + dad * k
            dk = dk + dad * q
            dv = dv + jnp.sum(q * k, axis=-1, keepdims=True) * dov
            for d in range(1, DIAG):
                w = jnp.where(rd >= d, jnp.exp(b - _group_roll(b, d)), 0.0)
                kr = _group_roll(k, d)
                dad = jnp.sum(dov * _group_roll(v, d), axis=-1, keepdims=True)
                ad = jnp.sum(q * kr * w, axis=-1, keepdims=True)
                dq = dq + dad * kr * w
                dk = dk + _group_roll(dad * q * w, -d)
                dv = dv + _group_roll(ad * dov, -d)
            dlf = _scan_rows(q * dq - k * dk, row, True) + extra
            dstate[...] = dst * e_last + _dot_tn(dob, (q * eb).astype(BF16))
            dforget = dlf / forget - dk
            dq_ref[rows, :] = (dq * (sq * (1.0 + zq * (1.0 - sq)))).astype(BF16)
            df_ref[rows, :] = (dforget * (1.0 - lb) * sg * (1.0 - sg)).astype(BF16)
            di_ref[rows, :] = dv.astype(BF16)
            return dlb + jnp.sum(dforget * (1.0 - sg), axis=0, keepdims=True)

        def group(i, dlb):
            for j in range(UNROLL_BWD):
                dlb = chunk(i * UNROLL_BWD + j, dlb)
            return dlb

        dlb_ref[...] += lax.fori_loop(0, ncb // UNROLL_BWD, group, jnp.zeros((1, HGRN_DK), F32))
        finish()

    blk = pl.BlockSpec((rb, HGRN_DK), lambda h, r: (ri(r), h))
    in_specs, out_specs, out_shape, scratch, extra = _carried_specs(
        carry, in_specs, [blk, blk, blk, pl.BlockSpec((1, HGRN_DK), lambda h, r: (0, h))],
        [jax.ShapeDtypeStruct((T, D_MODEL), BF16)] * 3 + [jax.ShapeDtypeStruct((1, D_MODEL), F32)],
        [pltpu.VMEM((HGRN_DK, HGRN_DK), F32)])
    return pl.pallas_call(
        body, name="hgrn_bwd", grid=(HGRN_HEADS, nr), in_specs=in_specs, out_specs=out_specs, out_shape=out_shape,
        scratch_shapes=scratch, compiler_params=_params(dimension_semantics=("arbitrary", "arbitrary")),
    )(z, z, z, lb_raw, states, do, *extra)


MESH = pl.DeviceIdType.MESH
ANY = pl.BlockSpec(memory_space=pl.ANY)


def _place():
    return lax.axis_index("x"), lax.axis_index("y"), lax.axis_index("c")


def _sems(n):
    return [pltpu.SemaphoreType.DMA((7 * n,)), pltpu.SemaphoreType.DMA((7 * n,)), pltpu.SemaphoreType.DMA((n,))]


class _Gather:
    def __init__(self, x_ref, out_ref, send_sems, recv_sems, local_sems, idx):
        self.x_ref, self.out_ref, self.send_sems, self.recv_sems, self.local_sem, self.base = (
            x_ref, out_ref, send_sems, recv_sems, local_sems.at[idx], 7 * idx)
        x, y, c = _place()
        self.c = c
        self.me, self.sibling = (x, y, c), (x, y, 1 - c)
        self.chips = [(1 - x, y), (x, 1 - y), (1 - x, 1 - y)]

    def rows(self, px, py, pc):
        return self.out_ref.at[4 * px + 2 * py + pc]

    def copy(self, k, block, to, from_input=False):
        return pltpu.make_async_remote_copy(
            src_ref=self.x_ref if from_input else self.rows(*block), dst_ref=self.rows(*block),
            send_sem=self.send_sems.at[self.base + k], recv_sem=self.recv_sems.at[self.base + k], device_id=to,
            device_id_type=MESH)

    def first(self):
        out = [self.copy(0, self.me, self.sibling, from_input=True)]
        return out + [self.copy(1 + j, self.me, (*chip, self.c), from_input=True) for j, chip in enumerate(self.chips)]

    def start(self):
        pltpu.make_async_copy(self.x_ref, self.rows(*self.me), self.local_sem).start()
        for cp in self.first():
            cp.start()

    def finish(self):
        passed = [self.copy(4 + j, (*chip, self.c), self.sibling) for j, chip in enumerate(self.chips)]
        for j, chip in enumerate(self.chips):
            self.copy(1 + j, (*chip, self.c), self.me).wait_recv()
            passed[j].start()
        self.copy(0, self.sibling, self.me).wait_recv()
        for j, chip in enumerate(self.chips):
            self.copy(4 + j, (*chip, 1 - self.c), self.me).wait_recv()
        for cp in self.first() + passed:
            cp.wait_send()
        pltpu.make_async_copy(self.x_ref, self.rows(*self.me), self.local_sem).wait()


class _Many:
    def __init__(self, kind, in_refs, out_refs, send_sems, recv_sems, local_sems):
        self.ops = [kind(x, o, send_sems, recv_sems, local_sems, i) for i, (x, o) in enumerate(zip(in_refs, out_refs))]

    def start(self):
        for op in self.ops:
            op.start()

    def finish(self):
        for op in self.ops:
            op.finish()


def _result_shapes(kind, arrs):
    return [jax.ShapeDtypeStruct(a.shape if kind is _Exchange else (N_DEV,) + a.shape, a.dtype) for a in arrs]


def _all_gather(name, shards):
    n = len(shards)

    def body(*refs):
        g = _Many(_Gather, refs[:n], refs[n:2 * n], *refs[2 * n:])
        g.start()
        g.finish()

    return pl.pallas_call(
        body, name=name, out_shape=_result_shapes(_Gather, shards), in_specs=[ANY] * n, out_specs=[ANY] * n,
        scratch_shapes=_sems(n),
    )(*shards)


def _peers(x, y, c):
    out = []
    for k in range(1, N_DEV):
        px = 1 - x if k & 4 else x
        py = 1 - y if k & 2 else y
        pc = 1 - c if k & 1 else c
        out.append((k, (px, py, pc), 4 * px + 2 * py + pc))
    return out


class _Exchange:
    def __init__(self, g_ref, recv_ref, send_sems, recv_sems, local_sems, idx):
        x, y, c = _place()
        me = 4 * x + 2 * y + c
        self.local = pltpu.make_async_copy(g_ref.at[me], recv_ref.at[me], local_sems.at[idx])
        self.copies = [
            pltpu.make_async_remote_copy(
                src_ref=g_ref.at[pidx], dst_ref=recv_ref.at[me], send_sem=send_sems.at[7 * idx + k - 1],
                recv_sem=recv_sems.at[7 * idx + k - 1], device_id=peer, device_id_type=MESH)
            for k, peer, pidx in _peers(x, y, c)]

    def start(self):
        self.local.start()
        for cp in self.copies:
            cp.start()

    def finish(self):
        for cp in self.copies:
            cp.wait()
        self.local.wait()


def _carried(carry, refs, n_in, n_out, first, last):
    kind, arrs = carry
    if kind is None:
        return refs, lambda: None
    n = len(arrs)
    ins, rest = refs[:n_in], refs[n_in + n:]
    outs, scratch = rest[:n_out], rest[n_out + n:]
    op = _Many(kind, refs[n_in:n_in + n], rest[n_out:n_out + n], *scratch[len(scratch) - 3:])

    @pl.when(first)
    def _():
        op.start()

    def finish():
        @pl.when(last)
        def _():
            op.finish()

    return tuple(ins) + tuple(outs) + tuple(scratch[:len(scratch) - 3]), finish


def _carried_specs(carry, in_specs, out_specs, out_shape, scratch):
    kind, arrs = carry
    if kind is None:
        return in_specs, out_specs, out_shape, scratch, []
    n = len(arrs)
    return (list(in_specs) + [ANY] * n, list(out_specs) + [ANY] * n,
            list(out_shape) + _result_shapes(kind, arrs), list(scratch) + _sems(n), list(arrs))


def _adamw(w, g, m, v):
    m = ADAM_B1 * m + (1.0 - ADAM_B1) * g
    v = ADAM_B2 * v + (1.0 - ADAM_B2) * (g * g)
    m_hat = m / (1.0 - ADAM_B1 ** ADAM_STEP)
    v_hat = v / (1.0 - ADAM_B2 ** ADAM_STEP)
    delta = -ADAM_LR * (m_hat / (jnp.sqrt(v_hat) + ADAM_EPS) + ADAM_WD * w)
    return delta, m, v


def _adamw_sum(name, recvs, w, m, v):
    L, R, C = w.shape
    tm = 128 if R % 128 == 0 else 64
    assert R % tm == 0 and len(recvs) == L

    def body(*refs):
        r_refs, (w_ref, m_ref, v_ref, g_ref, d_ref, nm_ref, nv_ref) = refs[:L], refs[L:]
        for l in range(L):
            g = r_refs[l][0].astype(F32)
            for s in range(1, N_DEV):
                g = g + r_refs[l][s].astype(F32)
            g_ref[l] = g
            d_ref[l], nm_ref[l], nv_ref[l] = _adamw(w_ref[l], g, m_ref[l], v_ref[l])

    blk = pl.BlockSpec((L, tm, C), lambda i: (0, i, 0))
    return pl.pallas_call(
        body, name=name, grid=(R // tm,),
        in_specs=[pl.BlockSpec((N_DEV, tm, C), lambda i: (0, i, 0))] * L + [blk, blk, blk],
        out_specs=[blk] * 4, out_shape=[jax.ShapeDtypeStruct((L, R, C), F32)] * 4,
        compiler_params=_params(dimension_semantics=("arbitrary",)),
    )(*recvs, w, m, v)


def _small_sync(part, w, m, v):
    def body(p_ref, w_ref, m_ref, v_ref, g_ref, d_ref, nm_ref, nv_ref, gath, send_sems, recv_sems):
        x, y, c = _place()
        me = 4 * x + 2 * y + c
        gath[me] = p_ref[...]
        copies = []
        for k, peer, _ in _peers(x, y, c):
            cp = pltpu.make_async_remote_copy(
                src_ref=p_ref, dst_ref=gath.at[me], send_sem=send_sems.at[k - 1], recv_sem=recv_sems.at[k - 1],
                device_id=peer, device_id_type=MESH)
            cp.start()
            copies.append(cp)
        for cp in copies:
            cp.wait()
        g = gath[0]
        for s in range(1, N_DEV):
            g = g + gath[s]
        wv = w_ref[...]
        l0, l1 = w_ref[8:9, :], w_ref[9:10, :]
        mx = jnp.maximum(l0, l1)
        e0, e1 = jnp.exp(l0 - mx), jnp.exp(l1 - mx)
        g9 = g[9:10, :] * (e0 / (e0 + e1)) * (e1 / (e0 + e1))
        row = lax.broadcasted_iota(jnp.int32, g.shape, 0)
        g = jnp.where(row == 9, g9, jnp.where(row == 8, -g9, g))
        g_ref[...] = g
        d_ref[...], nm_ref[...], nv_ref[...] = _adamw(wv, g, m_ref[...], v_ref[...])

    vm = pl.BlockSpec(memory_space=pltpu.VMEM)
    return pl.pallas_call(
        body, name="small_params_sync", in_specs=[vm] * 4, out_specs=[vm] * 4,
        out_shape=[jax.ShapeDtypeStruct(part.shape, F32)] * 4,
        scratch_shapes=[pltpu.VMEM((N_DEV,) + part.shape, F32), pltpu.SemaphoreType.DMA((7,)),
                        pltpu.SemaphoreType.DMA((7,))],
    )(part, w, m, v)


def _shards_bf16(d, pieces):
    return [d[name][layer].astype(BF16) for name, layer in pieces]


def _gathered(arrs, pieces, out):
    for a, (name, layer) in zip(arrs, pieces):
        out[name, layer] = a if name in COL_SHARDED else a.reshape(N_DEV * a.shape[1], a.shape[2])


def _pad_row(a, width=D_MODEL):
    a = a.reshape(1, -1)
    return jnp.pad(a, ((0, 0), (0, width - a.shape[1])))


def _pack_small(d, gn_full):
    rows = [d["mix_norm"], d["mlp_norm"], d["final_norm"].reshape(1, D_MODEL),
            _pad_row(d["attn_b_qkv"], 2 * D_MODEL).reshape(2, D_MODEL), _pad_row(d["attn_sinks"]),
            d["hgrn_lower_bounds"], gn_full.reshape(1, D_MODEL)]
    p = jnp.concatenate(rows, axis=0)
    return jnp.pad(p, ((0, SMALL_ROWS - p.shape[0]), (0, 0)))


def _unpack_small(p, me):
    return dict(
        mix_norm=p[0:2], mlp_norm=p[2:4], final_norm=p[4],
        attn_b_qkv=p[5:7].reshape(1, 2 * D_MODEL)[:, :QKV_DIM], attn_sinks=p[7:8, :N_Q_HEADS],
        hgrn_lower_bounds=p[8:10], hgrn_g_norm=lax.dynamic_slice(p[10:11], (0, me * 128), (1, 128)))


WEIGHT_NAMES = ['mix_norm', 'mlp_norm', 'final_norm', 'attn_w_qkv', 'attn_b_qkv', 'attn_sinks', 'attn_w_o', 'hgrn_w_in',
                'hgrn_g_norm', 'hgrn_w_o', 'hgrn_lower_bounds', 'mlp_w_up', 'mlp_w_down']
SMALL_NAMES = ('mix_norm', 'mlp_norm', 'final_norm', 'attn_b_qkv', 'attn_sinks', 'hgrn_lower_bounds', 'hgrn_g_norm')


def _rotary_tables(positions):
    inv_freq = ROPE_THETA ** (-jnp.arange(0, 2 * ROT_HALF, 2, dtype=F32) / (2 * ROT_HALF))
    ang = positions.astype(F32).reshape(-1, 1) * inv_freq
    cos, sin = jnp.cos(ang), jnp.sin(ang)
    r = jnp.arange(LANES) % HEAD_DIM
    idx = r % ROT_HALF
    c = jnp.where(r < 2 * ROT_HALF, cos[:, idx], 1.0)
    sa = jnp.where((r >= ROT_HALF) & (r < 2 * ROT_HALF), sin[:, idx], 0.0)
    sb = jnp.where(r < ROT_HALF, -sin[:, idx], 0.0)
    return jnp.concatenate([c, sa, sb], axis=1)


def kernel(x, positions, mix_norm, mlp_norm, final_norm, attn_w_qkv, attn_b_qkv, attn_sinks, attn_w_o, hgrn_w_in, hgrn_g_norm, hgrn_w_o, hgrn_lower_bounds, mlp_w_up, mlp_w_down, loss_target, m_mix_norm, m_mlp_norm, m_final_norm, m_attn_w_qkv, m_attn_b_qkv, m_attn_sinks, m_attn_w_o, m_hgrn_w_in, m_hgrn_g_norm, m_hgrn_w_o, m_hgrn_lower_bounds, m_mlp_w_up, m_mlp_w_down, v_mix_norm, v_mlp_norm, v_final_norm, v_attn_w_qkv, v_attn_b_qkv, v_attn_sinks, v_attn_w_o, v_hgrn_w_in, v_hgrn_g_norm, v_hgrn_w_o, v_hgrn_lower_bounds, v_mlp_w_up, v_mlp_w_down):
    w = dict(mix_norm=mix_norm, mlp_norm=mlp_norm, final_norm=final_norm, attn_w_qkv=attn_w_qkv, attn_b_qkv=attn_b_qkv,
             attn_sinks=attn_sinks, attn_w_o=attn_w_o, hgrn_w_in=hgrn_w_in, hgrn_g_norm=hgrn_g_norm, hgrn_w_o=hgrn_w_o,
             hgrn_lower_bounds=hgrn_lower_bounds, mlp_w_up=mlp_w_up, mlp_w_down=mlp_w_down)
    m = dict(mix_norm=m_mix_norm, mlp_norm=m_mlp_norm, final_norm=m_final_norm, attn_w_qkv=m_attn_w_qkv,
             attn_b_qkv=m_attn_b_qkv, attn_sinks=m_attn_sinks, attn_w_o=m_attn_w_o, hgrn_w_in=m_hgrn_w_in,
             hgrn_g_norm=m_hgrn_g_norm, hgrn_w_o=m_hgrn_w_o, hgrn_lower_bounds=m_hgrn_lower_bounds, mlp_w_up=m_mlp_w_up,
             mlp_w_down=m_mlp_w_down)
    v = dict(mix_norm=v_mix_norm, mlp_norm=v_mlp_norm, final_norm=v_final_norm, attn_w_qkv=v_attn_w_qkv,
             attn_b_qkv=v_attn_b_qkv, attn_sinks=v_attn_sinks, attn_w_o=v_attn_w_o, hgrn_w_in=v_hgrn_w_in,
             hgrn_g_norm=v_hgrn_g_norm, hgrn_w_o=v_hgrn_w_o, hgrn_lower_bounds=v_hgrn_lower_bounds, mlp_w_up=v_mlp_w_up,
             mlp_w_down=v_mlp_w_down)
    me = 4 * lax.axis_index("x") + 2 * lax.axis_index("y") + lax.axis_index("c")

    gn = hgrn_g_norm.reshape(1, 128)
    gn_a = gn.astype(BF16)
    gn_b = (gn - gn_a.astype(F32)).astype(BF16)
    gn_c = (gn - gn_a.astype(F32) - gn_b.astype(F32)).astype(BF16)
    gn_rows = jnp.pad(jnp.concatenate([gn_a, gn_b, gn_c], axis=1), ((0, 15), (0, D_MODEL - 3 * 128)))
    full = {}
    got = _all_gather("gather_attn_weights", _shards_bf16(w, GATHER_FIRST) + [gn_rows])
    _gathered(got[:1], GATHER_FIRST, full)
    w_qkv = full["attn_w_qkv", 0].transpose(1, 0, 2).reshape(D_MODEL, QKV_DIM)
    gn_terms = got[1][:, 0, :3 * 128].astype(F32).reshape(N_DEV, 3, 128)
    gn_full = ((gn_terms[:, 0] + gn_terms[:, 1]) + gn_terms[:, 2]).reshape(1, D_MODEL)

    x0 = x[0]
    tgt = loss_target[0]
    rot = _rotary_tables(positions)
    row = lambda a: a.reshape(1, -1)

    qkv, h0 = _norm_mm("qkv_proj", x0, row(mix_norm[0]), w_qkv, attn_b_qkv, rot=rot)
    att, *got = _attn_fwd(qkv, attn_sinks, carry=(_Gather, _shards_bf16(w, GATHER_ATTN)))
    _gathered(got, GATHER_ATTN, full)
    x1 = _mm_res("attn_out_proj", att, full["attn_w_o", 0], x0)
    u0, h1, *got = _norm_mm("mlp0_up", x1, row(mlp_norm[0]), full["mlp_w_up", 0],
                            carry=(_Gather, _shards_bf16(w, GATHER_MLP0)))
    _gathered(got, GATHER_MLP0, full)
    x2, a0 = _mlp_down("mlp0_down", u0, full["mlp_w_down", 0], x1)
    z, h2 = _norm_mm("hgrn_in_proj", x2, row(mix_norm[1]), full["hgrn_w_in", 0])
    o_raw, states, *got = _hgrn_fwd(z, hgrn_lower_bounds, carry=(_Gather, _shards_bf16(w, GATHER_HGRN)))
    _gathered(got, GATHER_HGRN, full)
    x3, o2 = _hgrn_out("hgrn_out_proj", o_raw, z, gn_full, full["hgrn_w_o", 0], x2)
    u1, h3 = _norm_mm("mlp1_up", x3, row(mlp_norm[1]), full["mlp_w_up", 1])
    dx4, a1, loss_part, g_final = _mlp_down("mlp1_down_loss", u1, full["mlp_w_down", 1], x3,
                                            loss_head=(tgt, row(final_norm)))

    gw = {}
    du1, = _mlp_bwd_act("mlp1_bwd_act", dx4, u1, full["mlp_w_down", 1])
    dx3, g_mlp1 = _mm_nt_rmsbwd("mlp1_bwd_in", du1, full["mlp_w_up", 1], x3, row(mlp_norm[1]), dx4)
    gw["mlp_w_down", 1] = _mm_tn("mlp1_dw_down", a1, dx4, "rows")
    gw["mlp_w_up", 1] = _mm_tn("mlp1_dw_up", h3, du1, "cols")

    do_raw, dg, g_gn = _hgrn_out_bwd("hgrn_out_bwd", dx3, o_raw, z, full["hgrn_w_o", 0], gn_full)
    gw["hgrn_w_o", 0] = _mm_tn("hgrn_dw_o", o2, dx3, "rows")
    recvs = {}
    dzq, dzf, dzi, g_lb, *recv = _hgrn_bwd(z, hgrn_lower_bounds, states, do_raw,
                                           carry=(_Exchange, [gw[p] for p in GRADS_HGRN]))
    recvs.update(zip(GRADS_HGRN, recv))
    dz = [dzq, dzf, dzi, dg]
    dx2, g_mix1 = _mm_nt_rmsbwd("hgrn_in_bwd", dz, full["hgrn_w_in", 0], x2, row(mix_norm[1]), dx3)
    gw["hgrn_w_in", 0] = jnp.concatenate(
        [_mm_tn(f"hgrn_dw_in{j}", h2, d, "cols") for j, d in enumerate(dz)], axis=0)

    du0, *recv = _mlp_bwd_act("mlp0_bwd_act", dx2, u0, full["mlp_w_down", 0],
                              carry=(_Exchange, [gw[p] for p in GRADS_MLP0]))
    recvs.update(zip(GRADS_MLP0, recv))
    dx1, g_mlp0 = _mm_nt_rmsbwd("mlp0_bwd_in", du0, full["mlp_w_up", 0], x1, row(mlp_norm[0]), dx2)
    gw["mlp_w_down", 0] = _mm_tn("mlp0_dw_down", a0, dx2, "rows")
    gw["mlp_w_up", 0] = _mm_tn("mlp0_dw_up", h1, du0, "cols")

    datt = _mm_nt("attn_out_bwd", dx1, full["attn_w_o", 0], BF16)
    gw["attn_w_o", 0] = _mm_tn("attn_dw_o", att, dx1, "rows")
    dqkv, g_sink, *recv = _attn_bwd(qkv, rot, attn_sinks, datt, carry=(_Exchange, [gw[p] for p in GRADS_ATTN]))
    recvs.update(zip(GRADS_ATTN, recv))
    g_qkv = _mm_tn("attn_dw_qkv", h0, dqkv)
    g_qkv = g_qkv.reshape(D_MODEL, N_DEV, QKV_DIM // N_DEV).transpose(1, 0, 2).astype(BF16)
    dx0, g_mix0, g_bqkv, recvs["attn_w_qkv", 0] = _mm_nt_rmsbwd(
        "qkv_bwd", dqkv, w_qkv, x0, row(mix_norm[0]), dx1, with_colsum=True, carry=(_Exchange, [g_qkv]))

    big = {name: _adamw_sum("adamw_" + name, [recvs[name, l] for l in range(w[name].shape[0])], w[name], m[name], v[name])
           for name in BIG_NAMES}

    zero_row = jnp.zeros((1, D_MODEL), F32)
    part = _pack_small(dict(
        mix_norm=jnp.concatenate([g_mix0, g_mix1], axis=0), mlp_norm=jnp.concatenate([g_mlp0, g_mlp1], axis=0),
        final_norm=g_final, attn_b_qkv=g_bqkv, attn_sinks=g_sink[:, :N_Q_HEADS],
        hgrn_lower_bounds=jnp.concatenate([zero_row, g_lb], axis=0)), g_gn)

    def spread(a):
        return lax.dynamic_update_slice(zero_row, a.reshape(1, 128), (0, me * 128))

    small_in = [_pack_small({n: d[n] for n in SMALL_NAMES if n != "hgrn_g_norm"}, spread(d["hgrn_g_norm"]))
                for d in (w, m, v)]
    small = [_unpack_small(p, me) for p in _small_sync(part, *small_in)]

    loss = lax.psum(loss_part[0, 0], ("x", "y", "c"))
    outs = [loss, dx0.reshape(x.shape)]
    for kind, grp_small in enumerate(small):
        for name in WEIGHT_NAMES:
            val = grp_small[name] if name in SMALL_NAMES else big[name][kind]
            outs.append(val.reshape(w[name].shape))
    return tuple(outs)
```

```python
import functools

import jax
import jax.numpy as jnp
from jax import lax
from jax.experimental import pallas as pl
from jax.experimental.pallas import tpu as pltpu

F32 = jnp.float32
BF16 = jnp.bfloat16

D_MODEL = 1024
HEAD_DIM = 64
N_Q_HEADS = 16
Q_DIM = 1024
KV_DIM = 256
QKV_DIM = 1536
ATT_BLOCK = 128
ROT_HALF = 8
ROPE_THETA = 500000.0
NEG_INF = -1e30
HGRN_HEADS = 8
HGRN_DK = 128
CHUNK = 64
D_FF = 4096
NORM_EPS = 1e-5
N_DEV = 8

ADAM_LR = 0.001
ADAM_B1 = 0.9
ADAM_B2 = 0.999
ADAM_EPS = 1e-08
ADAM_WD = 0.01
ADAM_STEP = 10

LANES = 128
VMEM_LIMIT = 56 * 1024 * 1024

GATHER_FIRST = (("attn_w_qkv", 0),)
GATHER_ATTN = (("attn_w_o", 0), ("mlp_w_up", 0), ("mlp_w_down", 0))
GATHER_MLP0 = (("hgrn_w_in", 0), ("hgrn_w_o", 0))
GATHER_HGRN = (("mlp_w_up", 1), ("mlp_w_down", 1))
GRADS_HGRN = (("mlp_w_down", 1), ("mlp_w_up", 1), ("hgrn_w_o", 0))
GRADS_MLP0 = (("hgrn_w_in", 0),)
GRADS_ATTN = (("mlp_w_down", 0), ("mlp_w_up", 0), ("attn_w_o", 0))
COL_SHARDED = ("attn_w_qkv", "hgrn_w_in", "mlp_w_up")
BIG_NAMES = ("attn_w_qkv", "attn_w_o", "hgrn_w_in", "hgrn_w_o", "mlp_w_up", "mlp_w_down")
SMALL_ROWS = 16


def _dot(a, b):
    return jnp.dot(a, b, preferred_element_type=F32)


def _dot_nt(a, b):
    return lax.dot_general(a, b, (((1,), (1,)), ((), ())), preferred_element_type=F32)


def _dot_tn(a, b):
    return lax.dot_general(a, b, (((0,), (0,)), ((), ())), preferred_element_type=F32)


def _params(**kw):
    return pltpu.CompilerParams(vmem_limit_bytes=VMEM_LIMIT, **kw)


def _full_spec(a):
    nd = a.ndim
    return pl.BlockSpec(a.shape, lambda *_: (0,) * nd)


def _row_call(name, body, n_rows, tm, row_ins, full_ins, row_outs, acc_outs=(), carry=(None, None)):
    steps = n_rows // tm
    in_specs = [pl.BlockSpec((tm, w), functools.partial(lambda i, cb: (i, cb), cb=cb)) for _, w, cb in row_ins]
    in_specs += [_full_spec(a) for a in full_ins]
    out_shape = [jax.ShapeDtypeStruct((n_rows, w), dt) for w, dt in row_outs]
    out_specs = [pl.BlockSpec((tm, w), lambda i: (i, 0)) for w, _ in row_outs]
    for shp, dt in acc_outs:
        out_shape.append(jax.ShapeDtypeStruct(shp, dt))
        out_specs.append(pl.BlockSpec(shp, functools.partial(lambda i, nd: (0,) * nd, nd=len(shp))))
    n_in, n_out = len(in_specs), len(out_specs)
    in_specs, out_specs, out_shape, scratch, extra = _carried_specs(carry, in_specs, out_specs, out_shape, [])

    def wrapped(*refs):
        i = pl.program_id(0)
        own, finish = _carried(carry, refs, n_in, n_out, i == 0, i == steps - 1)
        body(*own)
        finish()

    return pl.pallas_call(
        wrapped, name=name, grid=(steps,), in_specs=in_specs, out_specs=out_specs, out_shape=out_shape,
        scratch_shapes=scratch, compiler_params=_params(dimension_semantics=("arbitrary",)),
    )(*[a for a, _, _ in row_ins], *full_ins, *extra)


def _rms(x, gain):
    r = lax.rsqrt(jnp.mean(x * x, axis=-1, keepdims=True) + NORM_EPS)
    xhat = x * r
    return xhat * gain, xhat, r


def _rms_bwd(dy, xhat, r, gain):
    dxhat = dy * gain
    dx = r * (dxhat - xhat * jnp.mean(dxhat * xhat, axis=-1, keepdims=True))
    return dx, dy * xhat


def _norm_mm(name, x, gain, w, bias=None, rot=None, tm=512, carry=(None, None)):
    T = x.shape[0]
    tm = min(tm, T)
    nc = 512
    blocked = w.ndim == 3
    n = N_DEV * w.shape[2] if blocked else w.shape[1]
    assert n % nc == 0 and (not blocked or w.shape[2] == nc)

    def body(*refs):
        x_ref, refs = refs[0], refs[1:]
        if rot is not None:
            t_ref, refs = refs[0], refs[1:]
        g_ref, w_ref, refs = refs[0], refs[1], refs[2:]
        if bias is not None:
            b_ref, refs = refs[0], refs[1:]
        y_ref, h_ref = refs
        h, _, _ = _rms(x_ref[...], g_ref[...])
        hb = h.astype(BF16)
        h_ref[...] = hb
        for c in range(n // nc):
            sl = slice(c * nc, (c + 1) * nc)
            y = _dot(hb, w_ref[c] if blocked else w_ref[:, sl])
            if bias is not None:
                y = y + b_ref[:, sl]
            if rot is None:
                y_ref[:, sl] = y
            else:
                n_rot = max(0, min(nc, Q_DIM + KV_DIM - c * nc)) // LANES
                pieces = _rot_fwd(y[:, :n_rot * LANES], t_ref[...]) if n_rot else []
                for j in range(nc // LANES):
                    col = slice(c * nc + j * LANES, c * nc + (j + 1) * LANES)
                    y_ref[:, col] = pieces[j] if j < n_rot else y[:, j * LANES:(j + 1) * LANES]

    rows = [(x, D_MODEL, 0)] + ([(rot, 3 * LANES, 0)] if rot is not None else [])
    full = [gain, w] + ([bias] if bias is not None else [])
    return _row_call(name, body, T, tm, rows, full, [(n, F32), (D_MODEL, BF16)], carry=carry)


def _mm_res(name, a, w, res, tm=512):
    T = a.shape[0]
    tm = min(tm, T)

    def body(a_ref, r_ref, w_ref, o_ref):
        o_ref[...] = r_ref[...] + _dot(a_ref[...], w_ref[...])

    return _row_call(name, body, T, tm, [(a, a.shape[1], 0), (res, D_MODEL, 0)], [w], [(D_MODEL, F32)])[0]


def _mlp_down(name, u, w, res, tm=512, loss_head=None):
    T = u.shape[0]
    tm = min(tm, T)
    kc = 1024
    sub = min(256, tm)

    def body(*refs):
        if loss_head is None:
            u_ref, r_ref, w_ref, o_ref, a_ref = refs
        else:
            u_ref, r_ref, t_ref, w_ref, g_ref, o_ref, a_ref, loss_ref, dg_ref = refs

            @pl.when(pl.program_id(0) == 0)
            def _():
                loss_ref[...] = jnp.zeros_like(loss_ref)
                dg_ref[...] = jnp.zeros_like(dg_ref)

        for r0 in range(0, tm, sub):
            rs = slice(r0, r0 + sub)
            acc = r_ref[rs, :]
            for c in range(D_FF // kc):
                sl = slice(c * kc, (c + 1) * kc)
                a = jnp.maximum(u_ref[rs, sl], 0.0)
                ab = (a * a).astype(BF16)
                a_ref[rs, sl] = ab
                acc = acc + _dot(ab, w_ref[sl, :])
            if loss_head is None:
                o_ref[rs, :] = acc
            else:
                gain_v = g_ref[...]
                y, xhat, r = _rms(acc, gain_v)
                diff = y - t_ref[rs, :]
                per_row = jnp.sum(diff * diff, axis=-1, keepdims=True) * (1.0 / D_MODEL)
                loss_ref[...] += jnp.broadcast_to(0.5 * jnp.sum(per_row, axis=0, keepdims=True), loss_ref.shape)
                dx, dgr = _rms_bwd(diff * (1.0 / D_MODEL), xhat, r, gain_v)
                o_ref[rs, :] = dx
                dg_ref[...] += jnp.sum(dgr, axis=0, keepdims=True)

    rows, full, acc_outs = [(u, D_FF, 0), (res, D_MODEL, 0)], [w], []
    if loss_head is not None:
        rows, full = rows + [(loss_head[0], D_MODEL, 0)], full + [loss_head[1]]
        acc_outs = [((1, LANES), F32), ((1, D_MODEL), F32)]
    return _row_call(name, body, T, tm, rows, full, [(D_MODEL, F32), (D_FF, BF16)], acc_outs)


def _hgrn_out(name, o_raw, z, gn, w, res, tm=256):
    T = o_raw.shape[0]
    tm = min(tm, T)

    def body(o_ref, g_ref, r_ref, gn_ref, w_ref, x_ref, a_ref):
        y, _, _ = _rms(o_ref[...], gn_ref[...])
        g = g_ref[...]
        a = (y * (g * jax.nn.sigmoid(g))).astype(BF16)
        a_ref[...] = a
        x_ref[...] = r_ref[...] + _dot(a, w_ref[...])

    return _row_call(name, body, T, tm, [(o_raw, D_MODEL, 0), (z, D_MODEL, 3), (res, D_MODEL, 0)], [gn, w],
                     [(D_MODEL, F32), (D_MODEL, BF16)])


def _mm_nt_rmsbwd(name, dy, w, x, gain, dres, tm=512, with_colsum=False, carry=(None, None)):
    T = x.shape[0]
    tm = min(tm, T)
    dys = list(dy) if isinstance(dy, (list, tuple)) else [dy]
    width = dys[0].shape[1]
    n = width * len(dys)
    sub = min(256, tm)
    assert not with_colsum or len(dys) == 1

    def body(*refs):
        dy_refs, refs = refs[:len(dys)], refs[len(dys):]
        if with_colsum:
            x_ref, dr_ref, w_ref, g_ref, dx_ref, dg_ref, cs_ref = refs
        else:
            x_ref, dr_ref, w_ref, g_ref, dx_ref, dg_ref = refs

        @pl.when(pl.program_id(0) == 0)
        def _():
            dg_ref[...] = jnp.zeros_like(dg_ref)
            if with_colsum:
                cs_ref[...] = jnp.zeros_like(cs_ref)

        gain_v = g_ref[...]
        for r0 in range(0, tm, sub):
            rs = slice(r0, r0 + sub)
            if w.ndim == 3:
                nb = w.shape[2]
                dh = None
                for p in range(N_DEV):
                    piece, off = divmod(p * nb, width)
                    part = _dot_nt(dy_refs[piece][rs, off:off + nb].astype(BF16), w_ref[p])
                    dh = part if dh is None else dh + part
            else:
                dh = _dot_nt(dy_refs[0][rs, :].astype(BF16), w_ref[...])
            _, xhat, r = _rms(x_ref[rs, :], gain_v)
            dx, dgr = _rms_bwd(dh, xhat, r, gain_v)
            dx_ref[rs, :] = dr_ref[rs, :] + dx
            dg_ref[...] += jnp.sum(dgr, axis=0, keepdims=True)
            if with_colsum:
                cs_ref[...] += jnp.sum(dy_refs[0][rs, :].astype(F32), axis=0, keepdims=True)

    acc = [((1, D_MODEL), F32)] + ([((1, n), F32)] if with_colsum else [])
    rows = [(d, width, 0) for d in dys] + [(x, D_MODEL, 0), (dres, D_MODEL, 0)]
    return _row_call(name, body, T, tm, rows, [w, gain], [(D_MODEL, F32)], acc, carry=carry)


def _mm_nt(name, dy, w, out_dtype, tm=512):
    T = dy.shape[0]
    tm = min(tm, T)
    k = w.shape[0]

    def body(dy_ref, w_ref, o_ref):
        o_ref[...] = _dot_nt(dy_ref[...].astype(BF16), w_ref[...]).astype(out_dtype)

    return _row_call(name, body, T, tm, [(dy, dy.shape[1], 0)], [w], [(k, out_dtype)])[0]


def _mlp_bwd_act(name, dy, u, w_down, tm=512, carry=(None, None)):
    T = u.shape[0]
    tm = min(tm, T)
    kc = 1024

    def body(dy_ref, u_ref, w_ref, du_ref):
        dyb = dy_ref[...].astype(BF16)
        for c in range(D_FF // kc):
            sl = slice(c * kc, (c + 1) * kc)
            da = _dot_nt(dyb, w_ref[sl, :])
            du_ref[:, sl] = (da * (2.0 * jnp.maximum(u_ref[:, sl], 0.0))).astype(BF16)

    return _row_call(name, body, T, tm, [(dy, D_MODEL, 0), (u, D_FF, 0)], [w_down], [(D_FF, BF16)], carry=carry)


def _hgrn_out_bwd(name, dx, o_raw, z, w, gn, tm=256):
    T = dx.shape[0]
    tm = min(tm, T)

    def body(dx_ref, o_ref, g_ref, w_ref, gn_ref, do_ref, dg_ref, dgn_ref):
        @pl.when(pl.program_id(0) == 0)
        def _():
            dgn_ref[...] = jnp.zeros_like(dgn_ref)

        da = _dot_nt(dx_ref[...].astype(BF16), w_ref[...])
        gn_v = gn_ref[...]
        y, xhat, r = _rms(o_ref[...], gn_v)
        g = g_ref[...]
        sg = jax.nn.sigmoid(g)
        dg_ref[...] = (da * y * (sg * (1.0 + g * (1.0 - sg)))).astype(BF16)
        dyn = da * (g * sg)
        do, dgr = _rms_bwd(dyn, xhat, r, gn_v)
        do_ref[...] = do
        dgn_ref[...] += jnp.sum(dgr, axis=0, keepdims=True)

    return _row_call(name, body, T, tm, [(dx, D_MODEL, 0), (o_raw, D_MODEL, 0), (z, D_MODEL, 3)], [w, gn],
                     [(D_MODEL, F32), (D_MODEL, BF16)], [((1, D_MODEL), F32)])


def _mm_tn(name, a, b, shard=None, bm=1024, bn=512, tk=2048):
    T, M = a.shape
    N = b.shape[1]
    bm, bn, tk = min(bm, M), min(bn, N), min(tk, T)
    nk = T // tk
    if shard is None:
        out_shape, out_block = jax.ShapeDtypeStruct((M, N), F32), (bm, bn)
        out_map = lambda i, j, k: (i, j)
    elif shard == "cols":
        assert N % bn == 0
        out_shape, out_block = jax.ShapeDtypeStruct((N // bn, M, bn), BF16), (1, bm, bn)
        out_map = lambda i, j, k: (j, i, 0)
    else:
        rows = M // N_DEV
        assert bm % rows == 0
        out_shape, out_block = jax.ShapeDtypeStruct((N_DEV, rows, N), BF16), (bm // rows, rows, bn)
        out_map = lambda i, j, k: (i, 0, j)

    def body(a_ref, b_ref, o_ref, acc):
        k = pl.program_id(2)

        @pl.when(k == 0)
        def _():
            acc[...] = jnp.zeros_like(acc)

        acc[...] += _dot_tn(a_ref[...].astype(BF16), b_ref[...].astype(BF16))

        @pl.when(k == nk - 1)
        def _():
            o_ref[...] = acc[...].reshape(out_block).astype(o_ref.dtype)

    return pl.pallas_call(
        body, name=name, grid=(M // bm, N // bn, nk),
        in_specs=[pl.BlockSpec((tk, bm), lambda i, j, k: (k, i)), pl.BlockSpec((tk, bn), lambda i, j, k: (k, j))],
        out_specs=pl.BlockSpec(out_block, out_map), out_shape=out_shape,
        scratch_shapes=[pltpu.VMEM((bm, bn), F32)],
        compiler_params=_params(dimension_semantics=("parallel", "parallel", "arbitrary")),
    )(a, b)


def _rot_fwd(x, tab):
    c, sa, sb = tab[:, :LANES], tab[:, LANES:2 * LANES], tab[:, 2 * LANES:]
    outs = []
    for j in range(x.shape[1] // LANES):
        xs = x[:, j * LANES:(j + 1) * LANES]
        outs.append(xs * c + pltpu.roll(xs, ROT_HALF, 1) * sa + pltpu.roll(xs, LANES - ROT_HALF, 1) * sb)
    return outs


def _rot_bwd(dys, tab):
    c, sa, sb = tab[:, :LANES], tab[:, LANES:2 * LANES], tab[:, 2 * LANES:]
    return [dy * c + pltpu.roll(dy * sa, LANES - ROT_HALF, 1) + pltpu.roll(dy * sb, ROT_HALF, 1) for dy in dys]


ATT_SCALE = HEAD_DIM ** -0.5


def _attn_masks(n):
    kj = lax.broadcasted_iota(jnp.int32, (2 * ATT_BLOCK, ATT_BLOCK), 0)
    qi = lax.broadcasted_iota(jnp.int32, (2 * ATT_BLOCK, ATT_BLOCK), 1)
    delta = qi + ATT_BLOCK - kj
    first_key = jnp.where(n > 0, 0, ATT_BLOCK)
    valid = (delta >= 0) & (delta < ATT_BLOCK) & (kj >= first_key)
    low = lax.broadcasted_iota(jnp.int32, (1, LANES), 1) < HEAD_DIM
    upper = lax.broadcasted_iota(jnp.int32, (LANES, 1), 0) < HEAD_DIM
    return valid, low, upper


def _softmax_sink(s, valid, sink):
    s = jnp.where(valid, s, NEG_INF)
    m = jnp.maximum(jnp.max(s, axis=0, keepdims=True), sink)
    e = jnp.exp(s - m)
    es = jnp.exp(sink - m)
    inv = 1.0 / (jnp.sum(e, axis=0, keepdims=True) + es)
    return e * inv, es * inv


def _attn_specs(nb, tables):
    prev = lambda n: jnp.maximum(jnp.minimum(n, nb - 1) - 1, 0)
    cur = lambda n: jnp.minimum(n, nb - 1)
    specs = [
        pl.BlockSpec((ATT_BLOCK, Q_DIM), lambda n: (cur(n), 0)),
        pl.BlockSpec((ATT_BLOCK, KV_DIM), lambda n: (prev(n), 4)),
        pl.BlockSpec((ATT_BLOCK, KV_DIM), lambda n: (cur(n), 4)),
        pl.BlockSpec((ATT_BLOCK, KV_DIM), lambda n: (prev(n), 5)),
        pl.BlockSpec((ATT_BLOCK, KV_DIM), lambda n: (cur(n), 5)),
    ]
    if tables:
        specs += [pl.BlockSpec((ATT_BLOCK, 3 * LANES), lambda n: (prev(n), 0)),
                  pl.BlockSpec((ATT_BLOCK, 3 * LANES), lambda n: (cur(n), 0))]
    return specs + [pl.BlockSpec(memory_space=pltpu.SMEM)]


def _kv_band(prev_ref, cur_ref):
    out = []
    for j in range(KV_DIM // LANES):
        sl = slice(j * LANES, (j + 1) * LANES)
        band = jnp.concatenate([prev_ref[:, sl], cur_ref[:, sl]], axis=0)
        out.append((band, pltpu.roll(band, HEAD_DIM, 1)))
    return out


def _bf16(bands, transposed=False):
    return [[(a.T if transposed else a).astype(BF16) for a in pair] for pair in bands]


def _attn_fwd(qkv, sinks, carry=(None, None)):
    T = qkv.shape[0]
    nb = T // ATT_BLOCK

    def body(*refs):
        n = pl.program_id(0)
        own, finish = _carried(carry, refs, 6, 1, n == 0, n == nb - 1)
        q_ref, kp_ref, kc_ref, vp_ref, vc_ref, sink_ref, o_ref = own
        valid, low, upper = _attn_masks(n)
        ks = _bf16(_kv_band(kp_ref, kc_ref))
        vts = _bf16(_kv_band(vp_ref, vc_ref), transposed=True)
        heads = []
        for p in range(Q_DIM // LANES):
            kpair, khalf = p // 4, (p // 2) % 2
            q_pair = q_ref[:, p * LANES:(p + 1) * LANES] * ATT_SCALE
            for hf in range(2):
                qm = jnp.where(low if hf == 0 else ~low, q_pair, 0.0).astype(BF16)
                sw = 0 if khalf == hf else 1
                heads.append((2 * p + hf, kpair, sw, _dot_nt(ks[kpair][sw], qm)))
        probs = [_softmax_sink(s, valid, sink_ref[0, h])[0].astype(BF16) for h, _, _, s in heads]
        outs = [_dot(vts[kpair][sw], pr) for (_, kpair, sw, _), pr in zip(heads, probs)]
        for p in range(Q_DIM // LANES):
            o_ref[:, p * LANES:(p + 1) * LANES] = jnp.where(upper, outs[2 * p], outs[2 * p + 1]).T.astype(BF16)
        finish()

    in_specs, out_specs, out_shape, scratch, extra = _carried_specs(
        carry, _attn_specs(nb, False), [pl.BlockSpec((ATT_BLOCK, Q_DIM), lambda n: (n, 0))],
        [jax.ShapeDtypeStruct((T, Q_DIM), BF16)], [])
    return pl.pallas_call(
        body, name="attn_fwd", grid=(nb,), in_specs=in_specs, out_specs=out_specs, out_shape=out_shape,
        scratch_shapes=scratch, compiler_params=_params(dimension_semantics=("arbitrary",)),
    )(qkv, qkv, qkv, qkv, qkv, sinks, *extra)


def _attn_bwd(qkv, rot, sinks, dout, carry=(None, None)):
    T = qkv.shape[0]
    nb = T // ATT_BLOCK
    npair = KV_DIM // LANES

    def body(*refs):
        n = pl.program_id(0)
        own, finish = _carried(carry, refs, 9, 2, n == 0, n == nb)
        (q_ref, kp_ref, kc_ref, vp_ref, vc_ref, tp_ref, tc_ref, sink_ref, do_ref, dqkv_ref, dsink_ref,
         dq_c, dk_c, dv_c) = own

        @pl.when(n == 0)
        def _():
            dq_c[...] = jnp.zeros_like(dq_c)
            dk_c[...] = jnp.zeros_like(dk_c)
            dv_c[...] = jnp.zeros_like(dv_c)
            dsink_ref[...] = jnp.zeros_like(dsink_ref)

        def flush(dk_prev, dv_prev, tab_ref):
            dqkv_ref[:, :Q_DIM] = dq_c[...].astype(BF16)
            dk = _rot_bwd([dk_c[:, j * LANES:(j + 1) * LANES] + dk_prev[j] for j in range(npair)], tab_ref[...])
            for j in range(npair):
                dqkv_ref[:, Q_DIM + j * LANES:Q_DIM + (j + 1) * LANES] = dk[j].astype(BF16)
                dqkv_ref[:, Q_DIM + KV_DIM + j * LANES:Q_DIM + KV_DIM + (j + 1) * LANES] = (
                    dv_c[:, j * LANES:(j + 1) * LANES] + dv_prev[j]).astype(BF16)

        @pl.when(n < nb)
        def _():
            valid, low, upper = _attn_masks(n)
            lane = lax.broadcasted_iota(jnp.int32, (1, LANES), 1)
            k_band = _kv_band(kp_ref, kc_ref)
            ks, kts = _bf16(k_band), _bf16(k_band, transposed=True)
            vs = _bf16(_kv_band(vp_ref, vc_ref))
            dk_acc = [[jnp.zeros((2 * ATT_BLOCK, LANES), F32) for _ in range(2)] for _ in range(npair)]
            dv_acc = [[jnp.zeros((2 * ATT_BLOCK, LANES), F32) for _ in range(2)] for _ in range(npair)]
            dsink = jnp.zeros((1, LANES), F32)
            heads = []
            for p in range(Q_DIM // LANES):
                kpair, khalf = p // 4, (p // 2) % 2
                q_pair = q_ref[:, p * LANES:(p + 1) * LANES] * ATT_SCALE
                do_pair = do_ref[:, p * LANES:(p + 1) * LANES]
                for hf in range(2):
                    sel = low if hf == 0 else ~low
                    qm = jnp.where(sel, q_pair, 0.0).astype(BF16)
                    dom = jnp.where(sel, do_pair, 0.0).astype(BF16)
                    sw = 0 if khalf == hf else 1
                    heads.append((2 * p + hf, kpair, sw, qm, dom,
                                  _dot_nt(ks[kpair][sw], qm), _dot_nt(vs[kpair][sw], dom)))
            grads = []
            for h, kpair, sw, qm, dom, s, dp in heads:
                pr, ps = _softmax_sink(s, valid, sink_ref[0, h])
                dd = jnp.sum(pr * dp, axis=0, keepdims=True)
                dsink = dsink + jnp.where(lane == h, -jnp.sum(ps * dd, axis=1, keepdims=True), 0.0)
                grads.append((pr * (dp - dd)).astype(BF16))
                heads[h] = (kpair, sw, qm, dom, pr.astype(BF16))
            dq_t = []
            for (kpair, sw, qm, dom, pr), ds in zip(heads, grads):
                dq_t.append(_dot(kts[kpair][sw], ds))
                dk_acc[kpair][sw] = dk_acc[kpair][sw] + _dot(ds, qm)
                dv_acc[kpair][sw] = dv_acc[kpair][sw] + _dot(pr, dom)
            dqs = [jnp.where(upper, dq_t[2 * p], dq_t[2 * p + 1]).T * ATT_SCALE for p in range(Q_DIM // LANES)]
            dk_acc = [a[0] + pltpu.roll(a[1], HEAD_DIM, 1) for a in dk_acc]
            dv_acc = [a[0] + pltpu.roll(a[1], HEAD_DIM, 1) for a in dv_acc]
            flush([a[:ATT_BLOCK] for a in dk_acc], [a[:ATT_BLOCK] for a in dv_acc], tp_ref)
            dq = _rot_bwd(dqs, tc_ref[...])
            for p in range(Q_DIM // LANES):
                dq_c[:, p * LANES:(p + 1) * LANES] = dq[p]
            for j in range(npair):
                dk_c[:, j * LANES:(j + 1) * LANES] = dk_acc[j][ATT_BLOCK:]
                dv_c[:, j * LANES:(j + 1) * LANES] = dv_acc[j][ATT_BLOCK:]
            dsink_ref[...] += dsink

        @pl.when(n == nb)
        def _():
            zero = [jnp.zeros((ATT_BLOCK, LANES), F32) for _ in range(npair)]
            flush(zero, zero, tc_ref)

        finish()

    do_spec = pl.BlockSpec((ATT_BLOCK, Q_DIM), lambda n: (jnp.minimum(n, nb - 1), 0))
    in_specs, out_specs, out_shape, scratch, extra = _carried_specs(
        carry, _attn_specs(nb, True) + [do_spec],
        [pl.BlockSpec((ATT_BLOCK, QKV_DIM), lambda n: (jnp.maximum(n - 1, 0), 0)),
         pl.BlockSpec((1, LANES), lambda n: (0, 0))],
        [jax.ShapeDtypeStruct((T, QKV_DIM), BF16), jax.ShapeDtypeStruct((1, LANES), F32)],
        [pltpu.VMEM((ATT_BLOCK, Q_DIM), F32), pltpu.VMEM((ATT_BLOCK, KV_DIM), F32),
         pltpu.VMEM((ATT_BLOCK, KV_DIM), F32)])
    return pl.pallas_call(
        body, name="attn_bwd", grid=(nb + 1,), in_specs=in_specs, out_specs=out_specs, out_shape=out_shape,
        scratch_shapes=scratch, compiler_params=_params(dimension_semantics=("arbitrary",)),
    )(qkv, qkv, qkv, qkv, qkv, rot, rot, sinks, dout, *extra)


LEVELS = (32, 16, 8)
DIAG = 8
SUBLANES = 8
UNROLL = 4
UNROLL_BWD = 2


def _lower_bound(lb_ref):
    l0, l1 = lb_ref[0:1, :], lb_ref[1:2, :]
    mx = jnp.maximum(l0, l1)
    e0, e1 = jnp.exp(l0 - mx), jnp.exp(l1 - mx)
    return e1 / (e0 + e1)


GROUPS = CHUNK // SUBLANES


def _group_roll(x, k):
    return pltpu.roll(x.reshape(GROUPS, SUBLANES, HGRN_DK), k % SUBLANES, 1).reshape(CHUNK, HGRN_DK)


def _scan_rows(x, row, reverse):
    r8 = row & (SUBLANES - 1)
    for sh in (1, 2, 4):
        ok = (r8 < SUBLANES - sh) if reverse else (r8 >= sh)
        x = x + jnp.where(ok, _group_roll(x, -sh if reverse else sh), 0.0)
    g = x.reshape(GROUPS, SUBLANES, HGRN_DK)
    edge = 0 if reverse else SUBLANES - 1
    tot = jnp.broadcast_to(g[:, edge:edge + 1, :], g.shape)

    def shifted(a, n):
        z = jnp.zeros((n, SUBLANES, HGRN_DK), F32)
        return jnp.concatenate([a[n:], z] if reverse else [z, a[:GROUPS - n]], axis=0)

    acc = shifted(tot, 1)
    for sh in (1, 2, 4):
        acc = acc + shifted(acc, sh)
    return (g + acc).reshape(CHUNK, HGRN_DK)


def _level_masks():
    t = lax.broadcasted_iota(jnp.int32, (CHUNK, CHUNK), 0)
    s = lax.broadcasted_iota(jnp.int32, (CHUNK, CHUNK), 1)
    return [((t & h) != 0) & ((s & h) == 0) & ((t ^ s) < 2 * h) for h in LEVELS]


def _level_scales(b):
    out = []
    for h in LEVELS:
        parts = [jnp.broadcast_to(b[j * 2 * h + h - 1:j * 2 * h + h, :], (2 * h, HGRN_DK))
                 for j in range(CHUNK // (2 * h))]
        mid = parts[0] if len(parts) == 1 else jnp.concatenate(parts, axis=0)
        out.append(jnp.exp(-jnp.abs(b - mid)))
    return out


def _hgrn_gates(zq, zf, lb):
    sq = jax.nn.sigmoid(zq)
    q = zq * sq
    sg = jax.nn.sigmoid(zf)
    forget = lb + (1.0 - lb) * sg
    return q, sq, sg, forget, 1.0 - forget, jnp.log(forget)


def _hgrn_specs(T, rb, rev):
    nr = T // rb
    ri = (lambda r: nr - 1 - r) if rev else (lambda r: r)
    return nr, ri, [
        pl.BlockSpec((rb, HGRN_DK), lambda h, r: (ri(r), h)),
        pl.BlockSpec((rb, HGRN_DK), lambda h, r: (ri(r), HGRN_HEADS + h)),
        pl.BlockSpec((rb, HGRN_DK), lambda h, r: (ri(r), 2 * HGRN_HEADS + h)),
        pl.BlockSpec((2, HGRN_DK), lambda h, r: (0, h)),
    ]


def _hgrn_fwd(z, lb_raw, rb=1024, carry=(None, None)):
    T = z.shape[0]
    rb = min(rb, T)
    ncb = rb // CHUNK
    nr, ri, in_specs = _hgrn_specs(T, rb, False)

    def body(*refs):
        hh, rr = pl.program_id(0), pl.program_id(1)
        own, finish = _carried(carry, refs, 4, 2, (hh == 0) & (rr == 0), (hh == HGRN_HEADS - 1) & (rr == nr - 1))
        zq_ref, zf_ref, zi_ref, lb_ref, o_ref, st_ref, state = own

        @pl.when(rr == 0)
        def _():
            state[...] = jnp.zeros_like(state)

        lb = _lower_bound(lb_ref)
        row = lax.broadcasted_iota(jnp.int32, (CHUNK, HGRN_DK), 0)
        masks = _level_masks()
        rd = row & (DIAG - 1)

        def chunk(c, st):
            rows = pl.ds(pl.multiple_of(c * CHUNK, CHUNK), CHUNK)
            q, _, _, _, k, lf = _hgrn_gates(zq_ref[rows, :], zf_ref[rows, :], lb)
            v = zi_ref[rows, :]
            vb = v.astype(BF16)
            b = _scan_rows(lf, row, False)
            sc = jnp.zeros((CHUNK, CHUNK), F32)
            for e, mask in zip(_level_scales(b), masks):
                sc = sc + jnp.where(mask, _dot_nt((q * e).astype(BF16), (k * e).astype(BF16)), 0.0)
            o = _dot(sc.astype(BF16), vb) + jnp.sum(q * k, axis=-1, keepdims=True) * v
            for d in range(1, DIAG):
                w = jnp.where(rd >= d, q * _group_roll(k, d) * jnp.exp(b - _group_roll(b, d)), 0.0)
                o = o + jnp.sum(w, axis=-1, keepdims=True) * _group_roll(v, d)
            b_last = b[CHUNK - 1:CHUNK, :]
            kd = (k * jnp.exp(b_last - b)).astype(BF16)
            qd = (q * jnp.exp(b)).astype(BF16)
            st_ref[c, 0] = st
            o_ref[rows, :] = o + _dot_nt(qd, st.astype(BF16))
            return st * jnp.exp(b_last) + _dot_tn(vb, kd)

        def group(i, st):
            for j in range(UNROLL):
                st = chunk(i * UNROLL + j, st)
            return st

        state[...] = lax.fori_loop(0, ncb // UNROLL, group, state[...])
        finish()

    in_specs, out_specs, out_shape, scratch, extra = _carried_specs(
        carry, in_specs,
        [pl.BlockSpec((rb, HGRN_DK), lambda h, r: (r, h)),
         pl.BlockSpec((ncb, 1, HGRN_DK, HGRN_DK), lambda h, r: (r, h, 0, 0))],
        [jax.ShapeDtypeStruct((T, D_MODEL), F32),
         jax.ShapeDtypeStruct((T // CHUNK, HGRN_HEADS, HGRN_DK, HGRN_DK), F32)],
        [pltpu.VMEM((HGRN_DK, HGRN_DK), F32)])
    return pl.pallas_call(
        body, name="hgrn_fwd", grid=(HGRN_HEADS, nr), in_specs=in_specs, out_specs=out_specs, out_shape=out_shape,
        scratch_shapes=scratch, compiler_params=_params(dimension_semantics=("arbitrary", "arbitrary")),
    )(z, z, z, lb_raw, *extra)


def _hgrn_bwd(z, lb_raw, states, do, rb=1024, carry=(None, None)):
    T = z.shape[0]
    rb = min(rb, T)
    ncb = rb // CHUNK
    nr, ri, in_specs = _hgrn_specs(T, rb, True)
    in_specs += [pl.BlockSpec((ncb, 1, HGRN_DK, HGRN_DK), lambda h, r: (ri(r), h, 0, 0)),
                 pl.BlockSpec((rb, HGRN_DK), lambda h, r: (ri(r), h))]

    def body(*refs):
        hh, rr = pl.program_id(0), pl.program_id(1)
        own, finish = _carried(carry, refs, 6, 4, (hh == 0) & (rr == 0), (hh == HGRN_HEADS - 1) & (rr == nr - 1))
        zq_ref, zf_ref, zi_ref, lb_ref, st_ref, do_ref, dq_ref, df_ref, di_ref, dlb_ref, dstate = own

        @pl.when(rr == 0)
        def _():
            dstate[...] = jnp.zeros_like(dstate)
            dlb_ref[...] = jnp.zeros_like(dlb_ref)

        lb = _lower_bound(lb_ref)
        row = lax.broadcasted_iota(jnp.int32, (CHUNK, HGRN_DK), 0)
        masks = _level_masks()
        rd = row & (DIAG - 1)

        def chunk(ci, dlb):
            c = ncb - 1 - ci
            rows = pl.ds(pl.multiple_of(c * CHUNK, CHUNK), CHUNK)
            zq = zq_ref[rows, :]
            q, sq, sg, forget, k, lf = _hgrn_gates(zq, zf_ref[rows, :], lb)
            v = zi_ref[rows, :]
            dov = do_ref[rows, :]
            b = _scan_rows(lf, row, False)
            st = st_ref[c, 0]
            dst = dstate[...]
            b_last = b[CHUNK - 1:CHUNK, :]
            eb = jnp.exp(b)
            ebb = jnp.exp(b_last - b)
            e_last = jnp.exp(b_last)
            dob, vb, stb, dstb = dov.astype(BF16), v.astype(BF16), st.astype(BF16), dst.astype(BF16)
            dq = eb * _dot(dob, stb)
            dv = _dot_nt((k * ebb).astype(BF16), dstb)
            dk = ebb * _dot(vb, dstb)
            extra = e_last * jnp.sum(dst * st, axis=0, keepdims=True) + jnp.sum(k * dk, axis=0, keepdims=True)
            da = _dot_nt(dob, vb)
            sc = jnp.zeros((CHUNK, CHUNK), F32)
            for e, mask in zip(_level_scales(b), masks):
                qs, ks = (q * e).astype(BF16), (k * e).astype(BF16)
                dam = jnp.where(mask, da, 0.0).astype(BF16)
                dq = dq + e * _dot(dam, ks)
                dk = dk + e * _dot_tn(dam, qs)
                sc = sc + jnp.where(mask, _dot_nt(qs, ks), 0.0)
            dv = dv + _dot_tn(sc.astype(BF16), dob)
            dad = jnp.sum(dov * v, axis=-1, keepdims=True)
            dq = dq + dad * k
            dk = dk + dad * q
            dv = dv + jnp.sum(q * k, axis=-1, keepdims=True) * dov
            for d in range(1, DIAG):
                w = jnp.where(rd >= d, jnp.exp(b - _group_roll(b, d)), 0.0)
                kr = _group_roll(k, d)
                dad = jnp.sum(dov * _group_roll(v, d), axis=-1, keepdims=True)
                ad = jnp.sum(q * kr * w, axis=-1, keepdims=True)
                dq = dq + dad * kr * w
                dk = dk + _group_roll(dad * q * w, -d)
                dv = dv + _group_roll(ad * dov, -d)
            dlf = _scan_rows(q * dq - k * dk, row, True) + extra
            dstate[...] = dst * e_last + _dot_tn(dob, (q * eb).astype(BF16))
            dforget = dlf / forget - dk
            dq_ref[rows, :] = (dq * (sq * (1.0 + zq * (1.0 - sq)))).astype(BF16)
            df_ref[rows, :] = (dforget * (1.0 - lb) * sg * (1.0 - sg)).astype(BF16)
            di_ref[rows, :] = dv.astype(BF16)
            return dlb + jnp.sum(dforget * (1.0 - sg), axis=0, keepdims=True)

        def group(i, dlb):
            for j in range(UNROLL_BWD):
                dlb = chunk(i * UNROLL_BWD + j, dlb)
            return dlb

        dlb_ref[...] += lax.fori_loop(0, ncb // UNROLL_BWD, group, jnp.zeros((1, HGRN_DK), F32))
        finish()

    blk = pl.BlockSpec((rb, HGRN_DK), lambda h, r: (ri(r), h))
    in_specs, out_specs, out_shape, scratch, extra = _carried_specs(
        carry, in_specs, [blk, blk, blk, pl.BlockSpec((1, HGRN_DK), lambda h, r: (0, h))],
        [jax.ShapeDtypeStruct((T, D_MODEL), BF16)] * 3 + [jax.ShapeDtypeStruct((1, D_MODEL), F32)],
        [pltpu.VMEM((HGRN_DK, HGRN_DK), F32)])
    return pl.pallas_call(
        body, name="hgrn_bwd", grid=(HGRN_HEADS, nr), in_specs=in_specs, out_specs=out_specs, out_shape=out_shape,
        scratch_shapes=scratch, compiler_params=_params(dimension_semantics=("arbitrary", "arbitrary")),
    )(z, z, z, lb_raw, states, do, *extra)


MESH = pl.DeviceIdType.MESH
ANY = pl.BlockSpec(memory_space=pl.ANY)


def _place():
    return lax.axis_index("x"), lax.axis_index("y"), lax.axis_index("c")


def _sems(n):
    return [pltpu.SemaphoreType.DMA((7 * n,)), pltpu.SemaphoreType.DMA((7 * n,)), pltpu.SemaphoreType.DMA((n,))]


class _Gather:
    def __init__(self, x_ref, out_ref, send_sems, recv_sems, local_sems, idx):
        self.x_ref, self.out_ref, self.send_sems, self.recv_sems, self.local_sem, self.base = (
            x_ref, out_ref, send_sems, recv_sems, local_sems.at[idx], 7 * idx)
        x, y, c = _place()
        self.c = c
        self.me, self.sibling = (x, y, c), (x, y, 1 - c)
        self.chips = [(1 - x, y), (x, 1 - y), (1 - x, 1 - y)]

    def rows(self, px, py, pc):
        return self.out_ref.at[4 * px + 2 * py + pc]

    def copy(self, k, block, to, from_input=False):
        return pltpu.make_async_remote_copy(
            src_ref=self.x_ref if from_input else self.rows(*block), dst_ref=self.rows(*block),
            send_sem=self.send_sems.at[self.base + k], recv_sem=self.recv_sems.at[self.base + k], device_id=to,
            device_id_type=MESH)

    def first(self):
        out = [self.copy(0, self.me, self.sibling, from_input=True)]
        return out + [self.copy(1 + j, self.me, (*chip, self.c), from_input=True) for j, chip in enumerate(self.chips)]

    def start(self):
        pltpu.make_async_copy(self.x_ref, self.rows(*self.me), self.local_sem).start()
        for cp in self.first():
            cp.start()

    def finish(self):
        passed = [self.copy(4 + j, (*chip, self.c), self.sibling) for j, chip in enumerate(self.chips)]
        for j, chip in enumerate(self.chips):
            self.copy(1 + j, (*chip, self.c), self.me).wait_recv()
            passed[j].start()
        self.copy(0, self.sibling, self.me).wait_recv()
        for j, chip in enumerate(self.chips):
            self.copy(4 + j, (*chip, 1 - self.c), self.me).wait_recv()
        for cp in self.first() + passed:
            cp.wait_send()
        pltpu.make_async_copy(self.x_ref, self.rows(*self.me), self.local_sem).wait()


class _Many:
    def __init__(self, kind, in_refs, out_refs, send_sems, recv_sems, local_sems):
        self.ops = [kind(x, o, send_sems, recv_sems, local_sems, i) for i, (x, o) in enumerate(zip(in_refs, out_refs))]

    def start(self):
        for op in self.ops:
            op.start()

    def finish(self):
        for op in self.ops:
            op.finish()


def _result_shapes(kind, arrs):
    return [jax.ShapeDtypeStruct(a.shape if kind is _Exchange else (N_DEV,) + a.shape, a.dtype) for a in arrs]


def _all_gather(name, shards):
    n = len(shards)

    def body(*refs):
        g = _Many(_Gather, refs[:n], refs[n:2 * n], *refs[2 * n:])
        g.start()
        g.finish()

    return pl.pallas_call(
        body, name=name, out_shape=_result_shapes(_Gather, shards), in_specs=[ANY] * n, out_specs=[ANY] * n,
        scratch_shapes=_sems(n),
    )(*shards)


def _peers(x, y, c):
    out = []
    for k in range(1, N_DEV):
        px = 1 - x if k & 4 else x
        py = 1 - y if k & 2 else y
        pc = 1 - c if k & 1 else c
        out.append((k, (px, py, pc), 4 * px + 2 * py + pc))
    return out


class _Exchange:
    def __init__(self, g_ref, recv_ref, send_sems, recv_sems, local_sems, idx):
        x, y, c = _place()
        me = 4 * x + 2 * y + c
        self.local = pltpu.make_async_copy(g_ref.at[me], recv_ref.at[me], local_sems.at[idx])
        self.copies = [
            pltpu.make_async_remote_copy(
                src_ref=g_ref.at[pidx], dst_ref=recv_ref.at[me], send_sem=send_sems.at[7 * idx + k - 1],
                recv_sem=recv_sems.at[7 * idx + k - 1], device_id=peer, device_id_type=MESH)
            for k, peer, pidx in _peers(x, y, c)]

    def start(self):
        self.local.start()
        for cp in self.copies:
            cp.start()

    def finish(self):
        for cp in self.copies:
            cp.wait()
        self.local.wait()


def _carried(carry, refs, n_in, n_out, first, last):
    kind, arrs = carry
    if kind is None:
        return refs, lambda: None
    n = len(arrs)
    ins, rest = refs[:n_in], refs[n_in + n:]
    outs, scratch = rest[:n_out], rest[n_out + n:]
    op = _Many(kind, refs[n_in:n_in + n], rest[n_out:n_out + n], *scratch[len(scratch) - 3:])

    @pl.when(first)
    def _():
        op.start()

    def finish():
        @pl.when(last)
        def _():
            op.finish()

    return tuple(ins) + tuple(outs) + tuple(scratch[:len(scratch) - 3]), finish


def _carried_specs(carry, in_specs, out_specs, out_shape, scratch):
    kind, arrs = carry
    if kind is None:
        return in_specs, out_specs, out_shape, scratch, []
    n = len(arrs)
    return (list(in_specs) + [ANY] * n, list(out_specs) + [ANY] * n,
            list(out_shape) + _result_shapes(kind, arrs), list(scratch) + _sems(n), list(arrs))


def _adamw(w, g, m, v):
    m = ADAM_B1 * m + (1.0 - ADAM_B1) * g
    v = ADAM_B2 * v + (1.0 - ADAM_B2) * (g * g)
    m_hat = m / (1.0 - ADAM_B1 ** ADAM_STEP)
    v_hat = v / (1.0 - ADAM_B2 ** ADAM_STEP)
    delta = -ADAM_LR * (m_hat / (jnp.sqrt(v_hat) + ADAM_EPS) + ADAM_WD * w)
    return delta, m, v


def _adamw_sum(name, recvs, w, m, v):
    L, R, C = w.shape
    tm = 128 if R % 128 == 0 else 64
    assert R % tm == 0 and len(recvs) == L

    def body(*refs):
        r_refs, (w_ref, m_ref, v_ref, g_ref, d_ref, nm_ref, nv_ref) = refs[:L], refs[L:]
        for l in range(L):
            g = r_refs[l][0].astype(F32)
            for s in range(1, N_DEV):
                g = g + r_refs[l][s].astype(F32)
            g_ref[l] = g
            d_ref[l], nm_ref[l], nv_ref[l] = _adamw(w_ref[l], g, m_ref[l], v_ref[l])

    blk = pl.BlockSpec((L, tm, C), lambda i: (0, i, 0))
    return pl.pallas_call(
        body, name=name, grid=(R // tm,),
        in_specs=[pl.BlockSpec((N_DEV, tm, C), lambda i: (0, i, 0))] * L + [blk, blk, blk],
        out_specs=[blk] * 4, out_shape=[jax.ShapeDtypeStruct((L, R, C), F32)] * 4,
        compiler_params=_params(dimension_semantics=("arbitrary",)),
    )(*recvs, w, m, v)


def _small_sync(part, w, m, v):
    def body(p_ref, w_ref, m_ref, v_ref, g_ref, d_ref, nm_ref, nv_ref, gath, send_sems, recv_sems):
        x, y, c = _place()
        me = 4 * x + 2 * y + c
        gath[me] = p_ref[...]
        copies = []
        for k, peer, _ in _peers(x, y, c):
            cp = pltpu.make_async_remote_copy(
                src_ref=p_ref, dst_ref=gath.at[me], send_sem=send_sems.at[k - 1], recv_sem=recv_sems.at[k - 1],
                device_id=peer, device_id_type=MESH)
            cp.start()
            copies.append(cp)
        for cp in copies:
            cp.wait()
        g = gath[0]
        for s in range(1, N_DEV):
            g = g + gath[s]
        wv = w_ref[...]
        l0, l1 = w_ref[8:9, :], w_ref[9:10, :]
        mx = jnp.maximum(l0, l1)
        e0, e1 = jnp.exp(l0 - mx), jnp.exp(l1 - mx)
        g9 = g[9:10, :] * (e0 / (e0 + e1)) * (e1 / (e0 + e1))
        row = lax.broadcasted_iota(jnp.int32, g.shape, 0)
        g = jnp.where(row == 9, g9, jnp.where(row == 8, -g9, g))
        g_ref[...] = g
        d_ref[...], nm_ref[...], nv_ref[...] = _adamw(wv, g, m_ref[...], v_ref[...])

    vm = pl.BlockSpec(memory_space=pltpu.VMEM)
    return pl.pallas_call(
        body, name="small_params_sync", in_specs=[vm] * 4, out_specs=[vm] * 4,
        out_shape=[jax.ShapeDtypeStruct(part.shape, F32)] * 4,
        scratch_shapes=[pltpu.VMEM((N_DEV,) + part.shape, F32), pltpu.SemaphoreType.DMA((7,)),
                        pltpu.SemaphoreType.DMA((7,))],
    )(part, w, m, v)


def _shards_bf16(d, pieces):
    return [d[name][layer].astype(BF16) for name, layer in pieces]


def _gathered(arrs, pieces, out):
    for a, (name, layer) in zip(arrs, pieces):
        out[name, layer] = a if name in COL_SHARDED else a.reshape(N_DEV * a.shape[1], a.shape[2])


def _pad_row(a, width=D_MODEL):
    a = a.reshape(1, -1)
    return jnp.pad(a, ((0, 0), (0, width - a.shape[1])))


def _pack_small(d, gn_full):
    rows = [d["mix_norm"], d["mlp_norm"], d["final_norm"].reshape(1, D_MODEL),
            _pad_row(d["attn_b_qkv"], 2 * D_MODEL).reshape(2, D_MODEL), _pad_row(d["attn_sinks"]),
            d["hgrn_lower_bounds"], gn_full.reshape(1, D_MODEL)]
    p = jnp.concatenate(rows, axis=0)
    return jnp.pad(p, ((0, SMALL_ROWS - p.shape[0]), (0, 0)))


def _unpack_small(p, me):
    return dict(
        mix_norm=p[0:2], mlp_norm=p[2:4], final_norm=p[4],
        attn_b_qkv=p[5:7].reshape(1, 2 * D_MODEL)[:, :QKV_DIM], attn_sinks=p[7:8, :N_Q_HEADS],
        hgrn_lower_bounds=p[8:10], hgrn_g_norm=lax.dynamic_slice(p[10:11], (0, me * 128), (1, 128)))


WEIGHT_NAMES = ['mix_norm', 'mlp_norm', 'final_norm', 'attn_w_qkv', 'attn_b_qkv', 'attn_sinks', 'attn_w_o', 'hgrn_w_in',
                'hgrn_g_norm', 'hgrn_w_o', 'hgrn_lower_bounds', 'mlp_w_up', 'mlp_w_down']
SMALL_NAMES = ('mix_norm', 'mlp_norm', 'final_norm', 'attn_b_qkv', 'attn_sinks', 'hgrn_lower_bounds', 'hgrn_g_norm')


def _rotary_tables(positions):
    inv_freq = ROPE_THETA ** (-jnp.arange(0, 2 * ROT_HALF, 2, dtype=F32) / (2 * ROT_HALF))
    ang = positions.astype(F32).reshape(-1, 1) * inv_freq
    cos, sin = jnp.cos(ang), jnp.sin(ang)
    r = jnp.arange(LANES) % HEAD_DIM
    idx = r % ROT_HALF
    c = jnp.where(r < 2 * ROT_HALF, cos[:, idx], 1.0)
    sa = jnp.where((r >= ROT_HALF) & (r < 2 * ROT_HALF), sin[:, idx], 0.0)
    sb = jnp.where(r < ROT_HALF, -sin[:, idx], 0.0)
    return jnp.concatenate([c, sa, sb], axis=1)


def kernel(x, positions, mix_norm, mlp_norm, final_norm, attn_w_qkv, attn_b_qkv, attn_sinks, attn_w_o, hgrn_w_in, hgrn_g_norm, hgrn_w_o, hgrn_lower_bounds, mlp_w_up, mlp_w_down, loss_target, m_mix_norm, m_mlp_norm, m_final_norm, m_attn_w_qkv, m_attn_b_qkv, m_attn_sinks, m_attn_w_o, m_hgrn_w_in, m_hgrn_g_norm, m_hgrn_w_o, m_hgrn_lower_bounds, m_mlp_w_up, m_mlp_w_down, v_mix_norm, v_mlp_norm, v_final_norm, v_attn_w_qkv, v_attn_b_qkv, v_attn_sinks, v_attn_w_o, v_hgrn_w_in, v_hgrn_g_norm, v_hgrn_w_o, v_hgrn_lower_bounds, v_mlp_w_up, v_mlp_w_down):
    w = dict(mix_norm=mix_norm, mlp_norm=mlp_norm, final_norm=final_norm, attn_w_qkv=attn_w_qkv, attn_b_qkv=attn_b_qkv,
             attn_sinks=attn_sinks, attn_w_o=attn_w_o, hgrn_w_in=hgrn_w_in, hgrn_g_norm=hgrn_g_norm, hgrn_w_o=hgrn_w_o,
             hgrn_lower_bounds=hgrn_lower_bounds, mlp_w_up=mlp_w_up, mlp_w_down=mlp_w_down)
    m = dict(mix_norm=m_mix_norm, mlp_norm=m_mlp_norm, final_norm=m_final_norm, attn_w_qkv=m_attn_w_qkv,
             attn_b_qkv=m_attn_b_qkv, attn_sinks=m_attn_sinks, attn_w_o=m_attn_w_o, hgrn_w_in=m_hgrn_w_in,
             hgrn_g_norm=m_hgrn_g_norm, hgrn_w_o=m_hgrn_w_o, hgrn_lower_bounds=m_hgrn_lower_bounds, mlp_w_up=m_mlp_w_up,
             mlp_w_down=m_mlp_w_down)
    v = dict(mix_norm=v_mix_norm, mlp_norm=v_mlp_norm, final_norm=v_final_norm, attn_w_qkv=v_attn_w_qkv,
             attn_b_qkv=v_attn_b_qkv, attn_sinks=v_attn_sinks, attn_w_o=v_attn_w_o, hgrn_w_in=v_hgrn_w_in,
             hgrn_g_norm=v_hgrn_g_norm, hgrn_w_o=v_hgrn_w_o, hgrn_lower_bounds=v_hgrn_lower_bounds, mlp_w_up=v_mlp_w_up,
             mlp_w_down=v_mlp_w_down)
    me = 4 * lax.axis_index("x") + 2 * lax.axis_index("y") + lax.axis_index("c")

    gn = hgrn_g_norm.reshape(1, 128)
    gn_a = gn.astype(BF16)
    gn_b = (gn - gn_a.astype(F32)).astype(BF16)
    gn_c = (gn - gn_a.astype(F32) - gn_b.astype(F32)).astype(BF16)
    gn_rows = jnp.pad(jnp.concatenate([gn_a, gn_b, gn_c], axis=1), ((0, 15), (0, D_MODEL - 3 * 128)))
    full = {}
    got = _all_gather("gather_attn_weights", _shards_bf16(w, GATHER_FIRST) + [gn_rows])
    _gathered(got[:1], GATHER_FIRST, full)
    w_qkv = full["attn_w_qkv", 0].transpose(1, 0, 2).reshape(D_MODEL, QKV_DIM)
    gn_terms = got[1][:, 0, :3 * 128].astype(F32).reshape(N_DEV, 3, 128)
    gn_full = ((gn_terms[:, 0] + gn_terms[:, 1]) + gn_terms[:, 2]).reshape(1, D_MODEL)

    x0 = x[0]
    tgt = loss_target[0]
    rot = _rotary_tables(positions)
    row = lambda a: a.reshape(1, -1)

    qkv, h0 = _norm_mm("qkv_proj", x0, row(mix_norm[0]), w_qkv, attn_b_qkv, rot=rot)
    att, *got = _attn_fwd(qkv, attn_sinks, carry=(_Gather, _shards_bf16(w, GATHER_ATTN)))
    _gathered(got, GATHER_ATTN, full)
    x1 = _mm_res("attn_out_proj", att, full["attn_w_o", 0], x0)
    u0, h1, *got = _norm_mm("mlp0_up", x1, row(mlp_norm[0]), full["mlp_w_up", 0],
                            carry=(_Gather, _shards_bf16(w, GATHER_MLP0)))
    _gathered(got, GATHER_MLP0, full)
    x2, a0 = _mlp_down("mlp0_down", u0, full["mlp_w_down", 0], x1)
    z, h2 = _norm_mm("hgrn_in_proj", x2, row(mix_norm[1]), full["hgrn_w_in", 0])
    o_raw, states, *got = _hgrn_fwd(z, hgrn_lower_bounds, carry=(_Gather, _shards_bf16(w, GATHER_HGRN)))
    _gathered(got, GATHER_HGRN, full)
    x3, o2 = _hgrn_out("hgrn_out_proj", o_raw, z, gn_full, full["hgrn_w_o", 0], x2)
    u1, h3 = _norm_mm("mlp1_up", x3, row(mlp_norm[1]), full["mlp_w_up", 1])
    dx4, a1, loss_part, g_final = _mlp_down("mlp1_down_loss", u1, full["mlp_w_down", 1], x3,
                                            loss_head=(tgt, row(final_norm)))

    gw = {}
    du1, = _mlp_bwd_act("mlp1_bwd_act", dx4, u1, full["mlp_w_down", 1])
    dx3, g_mlp1 = _mm_nt_rmsbwd("mlp1_bwd_in", du1, full["mlp_w_up", 1], x3, row(mlp_norm[1]), dx4)
    gw["mlp_w_down", 1] = _mm_tn("mlp1_dw_down", a1, dx4, "rows")
    gw["mlp_w_up", 1] = _mm_tn("mlp1_dw_up", h3, du1, "cols")

    do_raw, dg, g_gn = _hgrn_out_bwd("hgrn_out_bwd", dx3, o_raw, z, full["hgrn_w_o", 0], gn_full)
    gw["hgrn_w_o", 0] = _mm_tn("hgrn_dw_o", o2, dx3, "rows")
    recvs = {}
    dzq, dzf, dzi, g_lb, *recv = _hgrn_bwd(z, hgrn_lower_bounds, states, do_raw,
                                           carry=(_Exchange, [gw[p] for p in GRADS_HGRN]))
    recvs.update(zip(GRADS_HGRN, recv))
    dz = [dzq, dzf, dzi, dg]
    dx2, g_mix1 = _mm_nt_rmsbwd("hgrn_in_bwd", dz, full["hgrn_w_in", 0], x2, row(mix_norm[1]), dx3)
    gw["hgrn_w_in", 0] = jnp.concatenate(
        [_mm_tn(f"hgrn_dw_in{j}", h2, d, "cols") for j, d in enumerate(dz)], axis=0)

    du0, *recv = _mlp_bwd_act("mlp0_bwd_act", dx2, u0, full["mlp_w_down", 0],
                              carry=(_Exchange, [gw[p] for p in GRADS_MLP0]))
    recvs.update(zip(GRADS_MLP0, recv))
    dx1, g_mlp0 = _mm_nt_rmsbwd("mlp0_bwd_in", du0, full["mlp_w_up", 0], x1, row(mlp_norm[0]), dx2)
    gw["mlp_w_down", 0] = _mm_tn("mlp0_dw_down", a0, dx2, "rows")
    gw["mlp_w_up", 0] = _mm_tn("mlp0_dw_up", h1, du0, "cols")

    datt = _mm_nt("attn_out_bwd", dx1, full["attn_w_o", 0], BF16)
    gw["attn_w_o", 0] = _mm_tn("attn_dw_o", att, dx1, "rows")
    dqkv, g_sink, *recv = _attn_bwd(qkv, rot, attn_sinks, datt, carry=(_Exchange, [gw[p] for p in GRADS_ATTN]))
    recvs.update(zip(GRADS_ATTN, recv))
    g_qkv = _mm_tn("attn_dw_qkv", h0, dqkv)
    g_qkv = g_qkv.reshape(D_MODEL, N_DEV, QKV_DIM // N_DEV).transpose(1, 0, 2).astype(BF16)
    dx0, g_mix0, g_bqkv, recvs["attn_w_qkv", 0] = _mm_nt_rmsbwd(
        "qkv_bwd", dqkv, w_qkv, x0, row(mix_norm[0]), dx1, with_colsum=True, carry=(_Exchange, [g_qkv]))

    big = {name: _adamw_sum("adamw_" + name, [recvs[name, l] for l in range(w[name].shape[0])], w[name], m[name], v[name])
           for name in BIG_NAMES}

    zero_row = jnp.zeros((1, D_MODEL), F32)
    part = _pack_small(dict(
        mix_norm=jnp.concatenate([g_mix0, g_mix1], axis=0), mlp_norm=jnp.concatenate([g_mlp0, g_mlp1], axis=0),
        final_norm=g_final, attn_b_qkv=g_bqkv, attn_sinks=g_sink[:, :N_Q_HEADS],
        hgrn_lower_bounds=jnp.concatenate([zero_row, g_lb], axis=0)), g_gn)

    def spread(a):
        return lax.dynamic_update_slice(zero_row, a.reshape(1, 128), (0, me * 128))

    small_in = [_pack_small({n: d[n] for n in SMALL_NAMES if n != "hgrn_g_norm"}, spread(d["hgrn_g_norm"]))
                for d in (w, m, v)]
    small = [_unpack_small(p, me) for p in _small_sync(part, *small_in)]

    loss = lax.psum(loss_part[0, 0], ("x", "y", "c"))
    outs = [loss, dx0.reshape(x.shape)]
    for kind, grp_small in enumerate(small):
        for name in WEIGHT_NAMES:
            val = grp_small[name] if name in SMALL_NAMES else big[name][kind]
            outs.append(val.reshape(w[name].shape))
    return tuple(outs)
```

```python
import functools

import jax
import jax.numpy as jnp
from jax import lax
from jax.experimental import pallas as pl
from jax.experimental.pallas import tpu as pltpu

F32 = jnp.float32
BF16 = jnp.bfloat16

D_MODEL = 1024
HEAD_DIM = 64
N_Q_HEADS = 16
Q_DIM = 1024
KV_DIM = 256
QKV_DIM = 1536
ATT_BLOCK = 128
ROT_HALF = 8
ROPE_THETA = 500000.0
NEG_INF = -1e30
HGRN_HEADS = 8
HGRN_DK = 128
CHUNK = 64
D_FF = 4096
NORM_EPS = 1e-5
N_DEV = 8

ADAM_LR = 0.001
ADAM_B1 = 0.9
ADAM_B2 = 0.999
ADAM_EPS = 1e-08
ADAM_WD = 0.01
ADAM_STEP = 10

LANES = 128
VMEM_LIMIT = 56 * 1024 * 1024

GATHER_FIRST = (("attn_w_qkv", 0),)
GATHER_ATTN = (("attn_w_o", 0), ("mlp_w_up", 0), ("mlp_w_down", 0))
GATHER_MLP0 = (("hgrn_w_in", 0), ("hgrn_w_o", 0))
GATHER_HGRN = (("mlp_w_up", 1), ("mlp_w_down", 1))
GRADS_HGRN = (("mlp_w_down", 1), ("mlp_w_up", 1), ("hgrn_w_o", 0))
GRADS_MLP0 = (("hgrn_w_in", 0),)
GRADS_ATTN = (("mlp_w_down", 0), ("mlp_w_up", 0), ("attn_w_o", 0))
COL_SHARDED = ("attn_w_qkv", "hgrn_w_in", "mlp_w_up")
BIG_NAMES = ("attn_w_qkv", "attn_w_o", "hgrn_w_in", "hgrn_w_o", "mlp_w_up", "mlp_w_down")
SMALL_ROWS = 16


def _dot(a, b):
    return jnp.dot(a, b, preferred_element_type=F32)


def _dot_nt(a, b):
    return lax.dot_general(a, b, (((1,), (1,)), ((), ())), preferred_element_type=F32)


def _dot_tn(a, b):
    return lax.dot_general(a, b, (((0,), (0,)), ((), ())), preferred_element_type=F32)


def _params(**kw):
    return pltpu.CompilerParams(vmem_limit_bytes=VMEM_LIMIT, **kw)


def _full_spec(a):
    nd = a.ndim
    return pl.BlockSpec(a.shape, lambda *_: (0,) * nd)


def _row_call(name, body, n_rows, tm, row_ins, full_ins, row_outs, acc_outs=(), carry=(None, None)):
    steps = n_rows // tm
    in_specs = [pl.BlockSpec((tm, w), functools.partial(lambda i, cb: (i, cb), cb=cb)) for _, w, cb in row_ins]
    in_specs += [_full_spec(a) for a in full_ins]
    out_shape = [jax.ShapeDtypeStruct((n_rows, w), dt) for w, dt in row_outs]
    out_specs = [pl.BlockSpec((tm, w), lambda i: (i, 0)) for w, _ in row_outs]
    for shp, dt in acc_outs:
        out_shape.append(jax.ShapeDtypeStruct(shp, dt))
        out_specs.append(pl.BlockSpec(shp, functools.partial(lambda i, nd: (0,) * nd, nd=len(shp))))
    n_in, n_out = len(in_specs), len(out_specs)
    in_specs, out_specs, out_shape, scratch, extra = _carried_specs(carry, in_specs, out_specs, out_shape, [])

    def wrapped(*refs):
        i = pl.program_id(0)
        own, finish = _carried(carry, refs, n_in, n_out, i == 0, i == steps - 1)
        body(*own)
        finish()

    return pl.pallas_call(
        wrapped, name=name, grid=(steps,), in_specs=in_specs, out_specs=out_specs, out_shape=out_shape,
        scratch_shapes=scratch, compiler_params=_params(dimension_semantics=("arbitrary",)),
    )(*[a for a, _, _ in row_ins], *full_ins, *extra)


def _rms(x, gain):
    r = lax.rsqrt(jnp.mean(x * x, axis=-1, keepdims=True) + NORM_EPS)
    xhat = x * r
    return xhat * gain, xhat, r


def _rms_bwd(dy, xhat, r, gain):
    dxhat = dy * gain
    dx = r * (dxhat - xhat * jnp.mean(dxhat * xhat, axis=-1, keepdims=True))
    return dx, dy * xhat


def _norm_mm(name, x, gain, w, bias=None, rot=None, tm=512, carry=(None, None)):
    T = x.shape[0]
    tm = min(tm, T)
    nc = 512
    blocked = w.ndim == 3
    n = N_DEV * w.shape[2] if blocked else w.shape[1]
    assert n % nc == 0 and (not blocked or w.shape[2] == nc)

    def body(*refs):
        x_ref, refs = refs[0], refs[1:]
        if rot is not None:
            t_ref, refs = refs[0], refs[1:]
        g_ref, w_ref, refs = refs[0], refs[1], refs[2:]
        if bias is not None:
            b_ref, refs = refs[0], refs[1:]
        y_ref, h_ref = refs
        h, _, _ = _rms(x_ref[...], g_ref[...])
        hb = h.astype(BF16)
        h_ref[...] = hb
        for c in range(n // nc):
            sl = slice(c * nc, (c + 1) * nc)
            y = _dot(hb, w_ref[c] if blocked else w_ref[:, sl])
            if bias is not None:
                y = y + b_ref[:, sl]
            if rot is None:
                y_ref[:, sl] = y
            else:
                n_rot = max(0, min(nc, Q_DIM + KV_DIM - c * nc)) // LANES
                pieces = _rot_fwd(y[:, :n_rot * LANES], t_ref[...]) if n_rot else []
                for j in range(nc // LANES):
                    col = slice(c * nc + j * LANES, c * nc + (j + 1) * LANES)
                    y_ref[:, col] = pieces[j] if j < n_rot else y[:, j * LANES:(j + 1) * LANES]

    rows = [(x, D_MODEL, 0)] + ([(rot, 3 * LANES, 0)] if rot is not None else [])
    full = [gain, w] + ([bias] if bias is not None else [])
    return _row_call(name, body, T, tm, rows, full, [(n, F32), (D_MODEL, BF16)], carry=carry)


def _mm_res(name, a, w, res, tm=512):
    T = a.shape[0]
    tm = min(tm, T)

    def body(a_ref, r_ref, w_ref, o_ref):
        o_ref[...] = r_ref[...] + _dot(a_ref[...], w_ref[...])

    return _row_call(name, body, T, tm, [(a, a.shape[1], 0), (res, D_MODEL, 0)], [w], [(D_MODEL, F32)])[0]


def _mlp_down(name, u, w, res, tm=512, loss_head=None):
    T = u.shape[0]
    tm = min(tm, T)
    kc = 1024
    sub = min(256, tm)

    def body(*refs):
        if loss_head is None:
            u_ref, r_ref, w_ref, o_ref, a_ref = refs
        else:
            u_ref, r_ref, t_ref, w_ref, g_ref, o_ref, a_ref, loss_ref, dg_ref = refs

            @pl.when(pl.program_id(0) == 0)
            def _():
                loss_ref[...] = jnp.zeros_like(loss_ref)
                dg_ref[...] = jnp.zeros_like(dg_ref)

        for r0 in range(0, tm, sub):
            rs = slice(r0, r0 + sub)
            acc = r_ref[rs, :]
            for c in range(D_FF // kc):
                sl = slice(c * kc, (c + 1) * kc)
                a = jnp.maximum(u_ref[rs, sl], 0.0)
                ab = (a * a).astype(BF16)
                a_ref[rs, sl] = ab
                acc = acc + _dot(ab, w_ref[sl, :])
            if loss_head is None:
                o_ref[rs, :] = acc
            else:
                gain_v = g_ref[...]
                y, xhat, r = _rms(acc, gain_v)
                diff = y - t_ref[rs, :]
                per_row = jnp.sum(diff * diff, axis=-1, keepdims=True) * (1.0 / D_MODEL)
                loss_ref[...] += jnp.broadcast_to(0.5 * jnp.sum(per_row, axis=0, keepdims=True), loss_ref.shape)
                dx, dgr = _rms_bwd(diff * (1.0 / D_MODEL), xhat, r, gain_v)
                o_ref[rs, :] = dx
                dg_ref[...] += jnp.sum(dgr, axis=0, keepdims=True)

    rows, full, acc_outs = [(u, D_FF, 0), (res, D_MODEL, 0)], [w], []
    if loss_head is not None:
        rows, full = rows + [(loss_head[0], D_MODEL, 0)], full + [loss_head[1]]
        acc_outs = [((1, LANES), F32), ((1, D_MODEL), F32)]
    return _row_call(name, body, T, tm, rows, full, [(D_MODEL, F32), (D_FF, BF16)], acc_outs)


def _hgrn_out(name, o_raw, z, gn, w, res, tm=256):
    T = o_raw.shape[0]
    tm = min(tm, T)

    def body(o_ref, g_ref, r_ref, gn_ref, w_ref, x_ref, a_ref):
        y, _, _ = _rms(o_ref[...], gn_ref[...])
        g = g_ref[...]
        a = (y * (g * jax.nn.sigmoid(g))).astype(BF16)
        a_ref[...] = a
        x_ref[...] = r_ref[...] + _dot(a, w_ref[...])

    return _row_call(name, body, T, tm, [(o_raw, D_MODEL, 0), (z, D_MODEL, 3), (res, D_MODEL, 0)], [gn, w],
                     [(D_MODEL, F32), (D_MODEL, BF16)])


def _mm_nt_rmsbwd(name, dy, w, x, gain, dres, tm=512, with_colsum=False, carry=(None, None)):
    T = x.shape[0]
    tm = min(tm, T)
    dys = list(dy) if isinstance(dy, (list, tuple)) else [dy]
    width = dys[0].shape[1]
    n = width * len(dys)
    sub = min(256, tm)
    assert not with_colsum or len(dys) == 1

    def body(*refs):
        dy_refs, refs = refs[:len(dys)], refs[len(dys):]
        if with_colsum:
            x_ref, dr_ref, w_ref, g_ref, dx_ref, dg_ref, cs_ref = refs
        else:
            x_ref, dr_ref, w_ref, g_ref, dx_ref, dg_ref = refs

        @pl.when(pl.program_id(0) == 0)
        def _():
            dg_ref[...] = jnp.zeros_like(dg_ref)
            if with_colsum:
                cs_ref[...] = jnp.zeros_like(cs_ref)

        gain_v = g_ref[...]
        for r0 in range(0, tm, sub):
            rs = slice(r0, r0 + sub)
            if w.ndim == 3:
                nb = w.shape[2]
                dh = None
                for p in range(N_DEV):
                    piece, off = divmod(p * nb, width)
                    part = _dot_nt(dy_refs[piece][rs, off:off + nb].astype(BF16), w_ref[p])
                    dh = part if dh is None else dh + part
            else:
                dh = _dot_nt(dy_refs[0][rs, :].astype(BF16), w_ref[...])
            _, xhat, r = _rms(x_ref[rs, :], gain_v)
            dx, dgr = _rms_bwd(dh, xhat, r, gain_v)
            dx_ref[rs, :] = dr_ref[rs, :] + dx
            dg_ref[...] += jnp.sum(dgr, axis=0, keepdims=True)
            if with_colsum:
                cs_ref[...] += jnp.sum(dy_refs[0][rs, :].astype(F32), axis=0, keepdims=True)

    acc = [((1, D_MODEL), F32)] + ([((1, n), F32)] if with_colsum else [])
    rows = [(d, width, 0) for d in dys] + [(x, D_MODEL, 0), (dres, D_MODEL, 0)]
    return _row_call(name, body, T, tm, rows, [w, gain], [(D_MODEL, F32)], acc, carry=carry)


def _mm_nt(name, dy, w, out_dtype, tm=512):
    T = dy.shape[0]
    tm = min(tm, T)
    k = w.shape[0]

    def body(dy_ref, w_ref, o_ref):
        o_ref[...] = _dot_nt(dy_ref[...].astype(BF16), w_ref[...]).astype(out_dtype)

    return _row_call(name, body, T, tm, [(dy, dy.shape[1], 0)], [w], [(k, out_dtype)])[0]


def _mlp_bwd_act(name, dy, u, w_down, tm=512, carry=(None, None)):
    T = u.shape[0]
    tm = min(tm, T)
    kc = 1024

    def body(dy_ref, u_ref, w_ref, du_ref):
        dyb = dy_ref[...].astype(BF16)
        for c in range(D_FF // kc):
            sl = slice(c * kc, (c + 1) * kc)
            da = _dot_nt(dyb, w_ref[sl, :])
            du_ref[:, sl] = (da * (2.0 * jnp.maximum(u_ref[:, sl], 0.0))).astype(BF16)

    return _row_call(name, body, T, tm, [(dy, D_MODEL, 0), (u, D_FF, 0)], [w_down], [(D_FF, BF16)], carry=carry)


def _hgrn_out_bwd(name, dx, o_raw, z, w, gn, tm=256):
    T = dx.shape[0]
    tm = min(tm, T)

    def body(dx_ref, o_ref, g_ref, w_ref, gn_ref, do_ref, dg_ref, dgn_ref):
        @pl.when(pl.program_id(0) == 0)
        def _():
            dgn_ref[...] = jnp.zeros_like(dgn_ref)

        da = _dot_nt(dx_ref[...].astype(BF16), w_ref[...])
        gn_v = gn_ref[...]
        y, xhat, r = _rms(o_ref[...], gn_v)
        g = g_ref[...]
        sg = jax.nn.sigmoid(g)
        dg_ref[...] = (da * y * (sg * (1.0 + g * (1.0 - sg)))).astype(BF16)
        dyn = da * (g * sg)
        do, dgr = _rms_bwd(dyn, xhat, r, gn_v)
        do_ref[...] = do
        dgn_ref[...] += jnp.sum(dgr, axis=0, keepdims=True)

    return _row_call(name, body, T, tm, [(dx, D_MODEL, 0), (o_raw, D_MODEL, 0), (z, D_MODEL, 3)], [w, gn],
                     [(D_MODEL, F32), (D_MODEL, BF16)], [((1, D_MODEL), F32)])


COL_BLOCK = D_FF // N_DEV


def _mm_tn(name, a, b, shard=None, bm=1024, bn=1024, tk=2048):
    T, M = a.shape
    N = b.shape[1]
    bm, bn, tk = min(bm, M), min(bn, N), min(tk, T)
    nk = T // tk
    if shard is None:
        out_shape, out_block = jax.ShapeDtypeStruct((M, N), F32), (bm, bn)
        out_map = lambda i, j, k: (i, j)
    elif shard == "cols":
        assert bn % COL_BLOCK == 0 and N % bn == 0
        out_shape = jax.ShapeDtypeStruct((N // COL_BLOCK, M, COL_BLOCK), BF16)
        out_block = (bn // COL_BLOCK, bm, COL_BLOCK)
        out_map = lambda i, j, k: (j, i, 0)
    else:
        rows = M // N_DEV
        assert bm % rows == 0
        out_shape, out_block = jax.ShapeDtypeStruct((N_DEV, rows, N), BF16), (bm // rows, rows, bn)
        out_map = lambda i, j, k: (i, 0, j)

    def body(a_ref, b_ref, o_ref, acc):
        k = pl.program_id(2)

        @pl.when(k == 0)
        def _():
            acc[...] = jnp.zeros_like(acc)

        acc[...] += _dot_tn(a_ref[...].astype(BF16), b_ref[...].astype(BF16))

        @pl.when(k == nk - 1)
        def _():
            if shard == "cols":
                for c in range(bn // COL_BLOCK):
                    o_ref[c] = acc[:, c * COL_BLOCK:(c + 1) * COL_BLOCK].astype(BF16)
            else:
                o_ref[...] = acc[...].reshape(out_block).astype(o_ref.dtype)

    return pl.pallas_call(
        body, name=name, grid=(M // bm, N // bn, nk),
        in_specs=[pl.BlockSpec((tk, bm), lambda i, j, k: (k, i)), pl.BlockSpec((tk, bn), lambda i, j, k: (k, j))],
        out_specs=pl.BlockSpec(out_block, out_map), out_shape=out_shape,
        scratch_shapes=[pltpu.VMEM((bm, bn), F32)],
        compiler_params=_params(dimension_semantics=("parallel", "parallel", "arbitrary")),
    )(a, b)


def _rot_fwd(x, tab):
    c, sa, sb = tab[:, :LANES], tab[:, LANES:2 * LANES], tab[:, 2 * LANES:]
    outs = []
    for j in range(x.shape[1] // LANES):
        xs = x[:, j * LANES:(j + 1) * LANES]
        outs.append(xs * c + pltpu.roll(xs, ROT_HALF, 1) * sa + pltpu.roll(xs, LANES - ROT_HALF, 1) * sb)
    return outs


def _rot_bwd(dys, tab):
    c, sa, sb = tab[:, :LANES], tab[:, LANES:2 * LANES], tab[:, 2 * LANES:]
    return [dy * c + pltpu.roll(dy * sa, LANES - ROT_HALF, 1) + pltpu.roll(dy * sb, ROT_HALF, 1) for dy in dys]


ATT_SCALE = HEAD_DIM ** -0.5


def _attn_masks(n):
    kj = lax.broadcasted_iota(jnp.int32, (2 * ATT_BLOCK, ATT_BLOCK), 0)
    qi = lax.broadcasted_iota(jnp.int32, (2 * ATT_BLOCK, ATT_BLOCK), 1)
    delta = qi + ATT_BLOCK - kj
    first_key = jnp.where(n > 0, 0, ATT_BLOCK)
    valid = (delta >= 0) & (delta < ATT_BLOCK) & (kj >= first_key)
    low = lax.broadcasted_iota(jnp.int32, (1, LANES), 1) < HEAD_DIM
    upper = lax.broadcasted_iota(jnp.int32, (LANES, 1), 0) < HEAD_DIM
    return valid, low, upper


def _softmax_sink(s, valid, sink):
    s = jnp.where(valid, s, NEG_INF)
    m = jnp.maximum(jnp.max(s, axis=0, keepdims=True), sink)
    e = jnp.exp(s - m)
    es = jnp.exp(sink - m)
    inv = 1.0 / (jnp.sum(e, axis=0, keepdims=True) + es)
    return e * inv, es * inv


def _attn_specs(nb, tables):
    prev = lambda n: jnp.maximum(jnp.minimum(n, nb - 1) - 1, 0)
    cur = lambda n: jnp.minimum(n, nb - 1)
    specs = [
        pl.BlockSpec((ATT_BLOCK, Q_DIM), lambda n: (cur(n), 0)),
        pl.BlockSpec((ATT_BLOCK, KV_DIM), lambda n: (prev(n), 4)),
        pl.BlockSpec((ATT_BLOCK, KV_DIM), lambda n: (cur(n), 4)),
        pl.BlockSpec((ATT_BLOCK, KV_DIM), lambda n: (prev(n), 5)),
        pl.BlockSpec((ATT_BLOCK, KV_DIM), lambda n: (cur(n), 5)),
    ]
    if tables:
        specs += [pl.BlockSpec((ATT_BLOCK, 3 * LANES), lambda n: (prev(n), 0)),
                  pl.BlockSpec((ATT_BLOCK, 3 * LANES), lambda n: (cur(n), 0))]
    return specs + [pl.BlockSpec(memory_space=pltpu.SMEM)]


def _kv_band(prev_ref, cur_ref):
    out = []
    for j in range(KV_DIM // LANES):
        sl = slice(j * LANES, (j + 1) * LANES)
        band = jnp.concatenate([prev_ref[:, sl], cur_ref[:, sl]], axis=0)
        out.append((band, pltpu.roll(band, HEAD_DIM, 1)))
    return out


def _bf16(bands, transposed=False):
    return [[(a.T if transposed else a).astype(BF16) for a in pair] for pair in bands]


def _attn_fwd(qkv, sinks, carry=(None, None)):
    T = qkv.shape[0]
    nb = T // ATT_BLOCK

    def body(*refs):
        n = pl.program_id(0)
        own, finish = _carried(carry, refs, 6, 1, n == 0, n == nb - 1)
        q_ref, kp_ref, kc_ref, vp_ref, vc_ref, sink_ref, o_ref = own
        valid, low, upper = _attn_masks(n)
        ks = _bf16(_kv_band(kp_ref, kc_ref))
        vts = _bf16(_kv_band(vp_ref, vc_ref), transposed=True)
        heads = []
        for p in range(Q_DIM // LANES):
            kpair, khalf = p // 4, (p // 2) % 2
            q_pair = q_ref[:, p * LANES:(p + 1) * LANES] * ATT_SCALE
            for hf in range(2):
                qm = jnp.where(low if hf == 0 else ~low, q_pair, 0.0).astype(BF16)
                sw = 0 if khalf == hf else 1
                heads.append((2 * p + hf, kpair, sw, _dot_nt(ks[kpair][sw], qm)))
        probs = [_softmax_sink(s, valid, sink_ref[0, h])[0].astype(BF16) for h, _, _, s in heads]
        outs = [_dot(vts[kpair][sw], pr) for (_, kpair, sw, _), pr in zip(heads, probs)]
        for p in range(Q_DIM // LANES):
            o_ref[:, p * LANES:(p + 1) * LANES] = jnp.where(upper, outs[2 * p], outs[2 * p + 1]).T.astype(BF16)
        finish()

    in_specs, out_specs, out_shape, scratch, extra = _carried_specs(
        carry, _attn_specs(nb, False), [pl.BlockSpec((ATT_BLOCK, Q_DIM), lambda n: (n, 0))],
        [jax.ShapeDtypeStruct((T, Q_DIM), BF16)], [])
    return pl.pallas_call(
        body, name="attn_fwd", grid=(nb,), in_specs=in_specs, out_specs=out_specs, out_shape=out_shape,
        scratch_shapes=scratch, compiler_params=_params(dimension_semantics=("arbitrary",)),
    )(qkv, qkv, qkv, qkv, qkv, sinks, *extra)


def _attn_bwd(qkv, rot, sinks, dout, carry=(None, None)):
    T = qkv.shape[0]
    nb = T // ATT_BLOCK
    npair = KV_DIM // LANES

    def body(*refs):
        n = pl.program_id(0)
        own, finish = _carried(carry, refs, 9, 2, n == 0, n == nb)
        (q_ref, kp_ref, kc_ref, vp_ref, vc_ref, tp_ref, tc_ref, sink_ref, do_ref, dqkv_ref, dsink_ref,
         dq_c, dk_c, dv_c) = own

        @pl.when(n == 0)
        def _():
            dq_c[...] = jnp.zeros_like(dq_c)
            dk_c[...] = jnp.zeros_like(dk_c)
            dv_c[...] = jnp.zeros_like(dv_c)
            dsink_ref[...] = jnp.zeros_like(dsink_ref)

        def flush(dk_prev, dv_prev, tab_ref):
            dqkv_ref[:, :Q_DIM] = dq_c[...].astype(BF16)
            dk = _rot_bwd([dk_c[:, j * LANES:(j + 1) * LANES] + dk_prev[j] for j in range(npair)], tab_ref[...])
            for j in range(npair):
                dqkv_ref[:, Q_DIM + j * LANES:Q_DIM + (j + 1) * LANES] = dk[j].astype(BF16)
                dqkv_ref[:, Q_DIM + KV_DIM + j * LANES:Q_DIM + KV_DIM + (j + 1) * LANES] = (
                    dv_c[:, j * LANES:(j + 1) * LANES] + dv_prev[j]).astype(BF16)

        @pl.when(n < nb)
        def _():
            valid, low, upper = _attn_masks(n)
            lane = lax.broadcasted_iota(jnp.int32, (1, LANES), 1)
            k_band = _kv_band(kp_ref, kc_ref)
            ks, kts = _bf16(k_band), _bf16(k_band, transposed=True)
            vs = _bf16(_kv_band(vp_ref, vc_ref))
            dk_acc = [[jnp.zeros((2 * ATT_BLOCK, LANES), F32) for _ in range(2)] for _ in range(npair)]
            dv_acc = [[jnp.zeros((2 * ATT_BLOCK, LANES), F32) for _ in range(2)] for _ in range(npair)]
            dsink = jnp.zeros((1, LANES), F32)
            heads = []
            for p in range(Q_DIM // LANES):
                kpair, khalf = p // 4, (p // 2) % 2
                q_pair = q_ref[:, p * LANES:(p + 1) * LANES] * ATT_SCALE
                do_pair = do_ref[:, p * LANES:(p + 1) * LANES]
                for hf in range(2):
                    sel = low if hf == 0 else ~low
                    qm = jnp.where(sel, q_pair, 0.0).astype(BF16)
                    dom = jnp.where(sel, do_pair, 0.0).astype(BF16)
                    sw = 0 if khalf == hf else 1
                    heads.append((2 * p + hf, kpair, sw, qm, dom,
                                  _dot_nt(ks[kpair][sw], qm), _dot_nt(vs[kpair][sw], dom)))
            grads = []
            for h, kpair, sw, qm, dom, s, dp in heads:
                pr, ps = _softmax_sink(s, valid, sink_ref[0, h])
                dd = jnp.sum(pr * dp, axis=0, keepdims=True)
                dsink = dsink + jnp.where(lane == h, -jnp.sum(ps * dd, axis=1, keepdims=True), 0.0)
                grads.append((pr * (dp - dd)).astype(BF16))
                heads[h] = (kpair, sw, qm, dom, pr.astype(BF16))
            dq_t = []
            for (kpair, sw, qm, dom, pr), ds in zip(heads, grads):
                dq_t.append(_dot(kts[kpair][sw], ds))
                dk_acc[kpair][sw] = dk_acc[kpair][sw] + _dot(ds, qm)
                dv_acc[kpair][sw] = dv_acc[kpair][sw] + _dot(pr, dom)
            dqs = [jnp.where(upper, dq_t[2 * p], dq_t[2 * p + 1]).T * ATT_SCALE for p in range(Q_DIM // LANES)]
            dk_acc = [a[0] + pltpu.roll(a[1], HEAD_DIM, 1) for a in dk_acc]
            dv_acc = [a[0] + pltpu.roll(a[1], HEAD_DIM, 1) for a in dv_acc]
            flush([a[:ATT_BLOCK] for a in dk_acc], [a[:ATT_BLOCK] for a in dv_acc], tp_ref)
            dq = _rot_bwd(dqs, tc_ref[...])
            for p in range(Q_DIM // LANES):
                dq_c[:, p * LANES:(p + 1) * LANES] = dq[p]
            for j in range(npair):
                dk_c[:, j * LANES:(j + 1) * LANES] = dk_acc[j][ATT_BLOCK:]
                dv_c[:, j * LANES:(j + 1) * LANES] = dv_acc[j][ATT_BLOCK:]
            dsink_ref[...] += dsink

        @pl.when(n == nb)
        def _():
            zero = [jnp.zeros((ATT_BLOCK, LANES), F32) for _ in range(npair)]
            flush(zero, zero, tc_ref)

        finish()

    do_spec = pl.BlockSpec((ATT_BLOCK, Q_DIM), lambda n: (jnp.minimum(n, nb - 1), 0))
    in_specs, out_specs, out_shape, scratch, extra = _carried_specs(
        carry, _attn_specs(nb, True) + [do_spec],
        [pl.BlockSpec((ATT_BLOCK, QKV_DIM), lambda n: (jnp.maximum(n - 1, 0), 0)),
         pl.BlockSpec((1, LANES), lambda n: (0, 0))],
        [jax.ShapeDtypeStruct((T, QKV_DIM), BF16), jax.ShapeDtypeStruct((1, LANES), F32)],
        [pltpu.VMEM((ATT_BLOCK, Q_DIM), F32), pltpu.VMEM((ATT_BLOCK, KV_DIM), F32),
         pltpu.VMEM((ATT_BLOCK, KV_DIM), F32)])
    return pl.pallas_call(
        body, name="attn_bwd", grid=(nb + 1,), in_specs=in_specs, out_specs=out_specs, out_shape=out_shape,
        scratch_shapes=scratch, compiler_params=_params(dimension_semantics=("arbitrary",)),
    )(qkv, qkv, qkv, qkv, qkv, rot, rot, sinks, dout, *extra)


LEVELS = (32, 16, 8)
DIAG = 8
SUBLANES = 8
UNROLL = 4
UNROLL_BWD = 2


def _lower_bound(lb_ref):
    l0, l1 = lb_ref[0:1, :], lb_ref[1:2, :]
    mx = jnp.maximum(l0, l1)
    e0, e1 = jnp.exp(l0 - mx), jnp.exp(l1 - mx)
    return e1 / (e0 + e1)


GROUPS = CHUNK // SUBLANES


def _group_roll(x, k):
    return pltpu.roll(x.reshape(GROUPS, SUBLANES, HGRN_DK), k % SUBLANES, 1).reshape(CHUNK, HGRN_DK)


def _scan_rows(x, row, reverse):
    r8 = row & (SUBLANES - 1)
    for sh in (1, 2, 4):
        ok = (r8 < SUBLANES - sh) if reverse else (r8 >= sh)
        x = x + jnp.where(ok, _group_roll(x, -sh if reverse else sh), 0.0)
    g = x.reshape(GROUPS, SUBLANES, HGRN_DK)
    edge = 0 if reverse else SUBLANES - 1
    tot = jnp.broadcast_to(g[:, edge:edge + 1, :], g.shape)

    def shifted(a, n):
        z = jnp.zeros((n, SUBLANES, HGRN_DK), F32)
        return jnp.concatenate([a[n:], z] if reverse else [z, a[:GROUPS - n]], axis=0)

    acc = shifted(tot, 1)
    for sh in (1, 2, 4):
        acc = acc + shifted(acc, sh)
    return (g + acc).reshape(CHUNK, HGRN_DK)


def _level_masks():
    t = lax.broadcasted_iota(jnp.int32, (CHUNK, CHUNK), 0)
    s = lax.broadcasted_iota(jnp.int32, (CHUNK, CHUNK), 1)
    return [((t & h) != 0) & ((s & h) == 0) & ((t ^ s) < 2 * h) for h in LEVELS]


def _level_scales(b):
    out = []
    for h in LEVELS:
        parts = [jnp.broadcast_to(b[j * 2 * h + h - 1:j * 2 * h + h, :], (2 * h, HGRN_DK))
                 for j in range(CHUNK // (2 * h))]
        mid = parts[0] if len(parts) == 1 else jnp.concatenate(parts, axis=0)
        out.append(jnp.exp(-jnp.abs(b - mid)))
    return out


def _hgrn_gates(zq, zf, lb):
    sq = jax.nn.sigmoid(zq)
    q = zq * sq
    sg = jax.nn.sigmoid(zf)
    forget = lb + (1.0 - lb) * sg
    return q, sq, sg, forget, 1.0 - forget, jnp.log(forget)


def _hgrn_specs(T, rb, rev):
    nr = T // rb
    ri = (lambda r: nr - 1 - r) if rev else (lambda r: r)
    return nr, ri, [
        pl.BlockSpec((rb, HGRN_DK), lambda h, r: (ri(r), h)),
        pl.BlockSpec((rb, HGRN_DK), lambda h, r: (ri(r), HGRN_HEADS + h)),
        pl.BlockSpec((rb, HGRN_DK), lambda h, r: (ri(r), 2 * HGRN_HEADS + h)),
        pl.BlockSpec((2, HGRN_DK), lambda h, r: (0, h)),
    ]


def _hgrn_fwd(z, lb_raw, rb=1024, carry=(None, None)):
    T = z.shape[0]
    rb = min(rb, T)
    ncb = rb // CHUNK
    nr, ri, in_specs = _hgrn_specs(T, rb, False)

    def body(*refs):
        hh, rr = pl.program_id(0), pl.program_id(1)
        own, finish = _carried(carry, refs, 4, 2, (hh == 0) & (rr == 0), (hh == HGRN_HEADS - 1) & (rr == nr - 1))
        zq_ref, zf_ref, zi_ref, lb_ref, o_ref, st_ref, state = own

        @pl.when(rr == 0)
        def _():
            state[...] = jnp.zeros_like(state)

        lb = _lower_bound(lb_ref)
        row = lax.broadcasted_iota(jnp.int32, (CHUNK, HGRN_DK), 0)
        masks = _level_masks()
        rd = row & (DIAG - 1)

        def chunk(c, st):
            rows = pl.ds(pl.multiple_of(c * CHUNK, CHUNK), CHUNK)
            q, _, _, _, k, lf = _hgrn_gates(zq_ref[rows, :], zf_ref[rows, :], lb)
            v = zi_ref[rows, :]
            vb = v.astype(BF16)
            b = _scan_rows(lf, row, False)
            sc = jnp.zeros((CHUNK, CHUNK), F32)
            for e, mask in zip(_level_scales(b), masks):
                sc = sc + jnp.where(mask, _dot_nt((q * e).astype(BF16), (k * e).astype(BF16)), 0.0)
            o = _dot(sc.astype(BF16), vb) + jnp.sum(q * k, axis=-1, keepdims=True) * v
            for d in range(1, DIAG):
                w = jnp.where(rd >= d, q * _group_roll(k, d) * jnp.exp(b - _group_roll(b, d)), 0.0)
                o = o + jnp.sum(w, axis=-1, keepdims=True) * _group_roll(v, d)
            b_last = b[CHUNK - 1:CHUNK, :]
            kd = (k * jnp.exp(b_last - b)).astype(BF16)
            qd = (q * jnp.exp(b)).astype(BF16)
            st_ref[c, 0] = st
            o_ref[rows, :] = o + _dot_nt(qd, st.astype(BF16))
            return st * jnp.exp(b_last) + _dot_tn(vb, kd)

        def group(i, st):
            for j in range(UNROLL):
                st = chunk(i * UNROLL + j, st)
            return st

        state[...] = lax.fori_loop(0, ncb // UNROLL, group, state[...])
        finish()

    in_specs, out_specs, out_shape, scratch, extra = _carried_specs(
        carry, in_specs,
        [pl.BlockSpec((rb, HGRN_DK), lambda h, r: (r, h)),
         pl.BlockSpec((ncb, 1, HGRN_DK, HGRN_DK), lambda h, r: (r, h, 0, 0))],
        [jax.ShapeDtypeStruct((T, D_MODEL), F32),
         jax.ShapeDtypeStruct((T // CHUNK, HGRN_HEADS, HGRN_DK, HGRN_DK), F32)],
        [pltpu.VMEM((HGRN_DK, HGRN_DK), F32)])
    return pl.pallas_call(
        body, name="hgrn_fwd", grid=(HGRN_HEADS, nr), in_specs=in_specs, out_specs=out_specs, out_shape=out_shape,
        scratch_shapes=scratch, compiler_params=_params(dimension_semantics=("arbitrary", "arbitrary")),
    )(z, z, z, lb_raw, *extra)


def _hgrn_bwd(z, lb_raw, states, do, rb=1024, carry=(None, None)):
    T = z.shape[0]
    rb = min(rb, T)
    ncb = rb // CHUNK
    nr, ri, in_specs = _hgrn_specs(T, rb, True)
    in_specs += [pl.BlockSpec((ncb, 1, HGRN_DK, HGRN_DK), lambda h, r: (ri(r), h, 0, 0)),
                 pl.BlockSpec((rb, HGRN_DK), lambda h, r: (ri(r), h))]

    def body(*refs):
        hh, rr = pl.program_id(0), pl.program_id(1)
        own, finish = _carried(carry, refs, 6, 4, (hh == 0) & (rr == 0), (hh == HGRN_HEADS - 1) & (rr == nr - 1))
        zq_ref, zf_ref, zi_ref, lb_ref, st_ref, do_ref, dq_ref, df_ref, di_ref, dlb_ref, dstate = own

        @pl.when(rr == 0)
        def _():
            dstate[...] = jnp.zeros_like(dstate)
            dlb_ref[...] = jnp.zeros_like(dlb_ref)

        lb = _lower_bound(lb_ref)
        row = lax.broadcasted_iota(jnp.int32, (CHUNK, HGRN_DK), 0)
        masks = _level_masks()
        rd = row & (DIAG - 1)

        def chunk(ci, dlb):
            c = ncb - 1 - ci
            rows = pl.ds(pl.multiple_of(c * CHUNK, CHUNK), CHUNK)
            zq = zq_ref[rows, :]
            q, sq, sg, forget, k, lf = _hgrn_gates(zq, zf_ref[rows, :], lb)
            v = zi_ref[rows, :]
            dov = do_ref[rows, :]
            b = _scan_rows(lf, row, False)
            st = st_ref[c, 0]
            dst = dstate[...]
            b_last = b[CHUNK - 1:CHUNK, :]
            eb = jnp.exp(b)
            ebb = jnp.exp(b_last - b)
            e_last = jnp.exp(b_last)
            dob, vb, stb, dstb = dov.astype(BF16), v.astype(BF16), st.astype(BF16), dst.astype(BF16)
            dq = eb * _dot(dob, stb)
            dv = _dot_nt((k * ebb).astype(BF16), dstb)
            dk = ebb * _dot(vb, dstb)
            extra = e_last * jnp.sum(dst * st, axis=0, keepdims=True) + jnp.sum(k * dk, axis=0, keepdims=True)
            da = _dot_nt(dob, vb)
            sc = jnp.zeros((CHUNK, CHUNK), F32)
            for e, mask in zip(_level_scales(b), masks):
                qs, ks = (q * e).astype(BF16), (k * e).astype(BF16)
                dam = jnp.where(mask, da, 0.0).astype(BF16)
                dq = dq + e * _dot(dam, ks)
                dk = dk + e * _dot_tn(dam, qs)
                sc = sc + jnp.where(mask, _dot_nt(qs, ks), 0.0)
            dv = dv + _dot_tn(sc.astype(BF16), dob)
            dad = jnp.sum(dov * v, axis=-1, keepdims=True)
            dq = dq + dad * k
            dk = dk + dad * q
            dv = dv + jnp.sum(q * k, axis=-1, keepdims=True) * dov
            for d in range(1, DIAG):
                w = jnp.where(rd >= d, jnp.exp(b - _group_roll(b, d)), 0.0)
                kr = _group_roll(k, d)
                dad = jnp.sum(dov * _group_roll(v, d), axis=-1, keepdims=True)
                ad = jnp.sum(q * kr * w, axis=-1, keepdims=True)
                dq = dq + dad * kr * w
                dk = dk + _group_roll(dad * q * w, -d)
                dv = dv + _group_roll(ad * dov, -d)
            dlf = _scan_rows(q * dq - k * dk, row, True) + extra
            dstate[...] = dst * e_last + _dot_tn(dob, (q * eb).astype(BF16))
            dforget = dlf / forget - dk
            dq_ref[rows, :] = (dq * (sq * (1.0 + zq * (1.0 - sq)))).astype(BF16)
            df_ref[rows, :] = (dforget * (1.0 - lb) * sg * (1.0 - sg)).astype(BF16)
            di_ref[rows, :] = dv.astype(BF16)
            return dlb + jnp.sum(dforget * (1.0 - sg), axis=0, keepdims=True)

        def group(i, dlb):
            for j in range(UNROLL_BWD):
                dlb = chunk(i * UNROLL_BWD + j, dlb)
            return dlb

        dlb_ref[...] += lax.fori_loop(0, ncb // UNROLL_BWD, group, jnp.zeros((1, HGRN_DK), F32))
        finish()

    blk = pl.BlockSpec((rb, HGRN_DK), lambda h, r: (ri(r), h))
    in_specs, out_specs, out_shape, scratch, extra = _carried_specs(
        carry, in_specs, [blk, blk, blk, pl.BlockSpec((1, HGRN_DK), lambda h, r: (0, h))],
        [jax.ShapeDtypeStruct((T, D_MODEL), BF16)] * 3 + [jax.ShapeDtypeStruct((1, D_MODEL), F32)],
        [pltpu.VMEM((HGRN_DK, HGRN_DK), F32)])
    return pl.pallas_call(
        body, name="hgrn_bwd", grid=(HGRN_HEADS, nr), in_specs=in_specs, out_specs=out_specs, out_shape=out_shape,
        scratch_shapes=scratch, compiler_params=_params(dimension_semantics=("arbitrary", "arbitrary")),
    )(z, z, z, lb_raw, states, do, *extra)


MESH = pl.DeviceIdType.MESH
ANY = pl.BlockSpec(memory_space=pl.ANY)


def _place():
    return lax.axis_index("x"), lax.axis_index("y"), lax.axis_index("c")


def _sems(n):
    return [pltpu.SemaphoreType.DMA((7 * n,)), pltpu.SemaphoreType.DMA((7 * n,)), pltpu.SemaphoreType.DMA((n,))]


class _Gather:
    def __init__(self, x_ref, out_ref, send_sems, recv_sems, local_sems, idx):
        self.x_ref, self.out_ref, self.send_sems, self.recv_sems, self.local_sem, self.base = (
            x_ref, out_ref, send_sems, recv_sems, local_sems.at[idx], 7 * idx)
        x, y, c = _place()
        self.c = c
        self.me, self.sibling = (x, y, c), (x, y, 1 - c)
        self.chips = [(1 - x, y), (x, 1 - y), (1 - x, 1 - y)]

    def rows(self, px, py, pc):
        return self.out_ref.at[4 * px + 2 * py + pc]

    def copy(self, k, block, to, from_input=False):
        return pltpu.make_async_remote_copy(
            src_ref=self.x_ref if from_input else self.rows(*block), dst_ref=self.rows(*block),
            send_sem=self.send_sems.at[self.base + k], recv_sem=self.recv_sems.at[self.base + k], device_id=to,
            device_id_type=MESH)

    def first(self):
        out = [self.copy(0, self.me, self.sibling, from_input=True)]
        return out + [self.copy(1 + j, self.me, (*chip, self.c), from_input=True) for j, chip in enumerate(self.chips)]

    def start(self):
        pltpu.make_async_copy(self.x_ref, self.rows(*self.me), self.local_sem).start()
        for cp in self.first():
            cp.start()

    def finish(self):
        passed = [self.copy(4 + j, (*chip, self.c), self.sibling) for j, chip in enumerate(self.chips)]
        for j, chip in enumerate(self.chips):
            self.copy(1 + j, (*chip, self.c), self.me).wait_recv()
            passed[j].start()
        self.copy(0, self.sibling, self.me).wait_recv()
        for j, chip in enumerate(self.chips):
            self.copy(4 + j, (*chip, 1 - self.c), self.me).wait_recv()
        for cp in self.first() + passed:
            cp.wait_send()
        pltpu.make_async_copy(self.x_ref, self.rows(*self.me), self.local_sem).wait()


class _Many:
    def __init__(self, kind, in_refs, out_refs, send_sems, recv_sems, local_sems):
        self.ops = [kind(x, o, send_sems, recv_sems, local_sems, i) for i, (x, o) in enumerate(zip(in_refs, out_refs))]

    def start(self):
        for op in self.ops:
            op.start()

    def finish(self):
        for op in self.ops:
            op.finish()


def _result_shapes(kind, arrs):
    return [jax.ShapeDtypeStruct(a.shape if kind is _Exchange else (N_DEV,) + a.shape, a.dtype) for a in arrs]


def _all_gather(name, shards):
    n = len(shards)

    def body(*refs):
        g = _Many(_Gather, refs[:n], refs[n:2 * n], *refs[2 * n:])
        g.start()
        g.finish()

    return pl.pallas_call(
        body, name=name, out_shape=_result_shapes(_Gather, shards), in_specs=[ANY] * n, out_specs=[ANY] * n,
        scratch_shapes=_sems(n),
    )(*shards)


def _peers(x, y, c):
    out = []
    for k in range(1, N_DEV):
        px = 1 - x if k & 4 else x
        py = 1 - y if k & 2 else y
        pc = 1 - c if k & 1 else c
        out.append((k, (px, py, pc), 4 * px + 2 * py + pc))
    return out


class _Exchange:
    def __init__(self, g_ref, recv_ref, send_sems, recv_sems, local_sems, idx):
        x, y, c = _place()
        me = 4 * x + 2 * y + c
        self.local = pltpu.make_async_copy(g_ref.at[me], recv_ref.at[me], local_sems.at[idx])
        self.copies = [
            pltpu.make_async_remote_copy(
                src_ref=g_ref.at[pidx], dst_ref=recv_ref.at[me], send_sem=send_sems.at[7 * idx + k - 1],
                recv_sem=recv_sems.at[7 * idx + k - 1], device_id=peer, device_id_type=MESH)
            for k, peer, pidx in _peers(x, y, c)]

    def start(self):
        self.local.start()
        for cp in self.copies:
            cp.start()

    def finish(self):
        for cp in self.copies:
            cp.wait()
        self.local.wait()


def _carried(carry, refs, n_in, n_out, first, last):
    kind, arrs = carry
    if kind is None:
        return refs, lambda: None
    n = len(arrs)
    ins, rest = refs[:n_in], refs[n_in + n:]
    outs, scratch = rest[:n_out], rest[n_out + n:]
    op = _Many(kind, refs[n_in:n_in + n], rest[n_out:n_out + n], *scratch[len(scratch) - 3:])

    @pl.when(first)
    def _():
        op.start()

    def finish():
        @pl.when(last)
        def _():
            op.finish()

    return tuple(ins) + tuple(outs) + tuple(scratch[:len(scratch) - 3]), finish


def _carried_specs(carry, in_specs, out_specs, out_shape, scratch):
    kind, arrs = carry
    if kind is None:
        return in_specs, out_specs, out_shape, scratch, []
    n = len(arrs)
    return (list(in_specs) + [ANY] * n, list(out_specs) + [ANY] * n,
            list(out_shape) + _result_shapes(kind, arrs), list(scratch) + _sems(n), list(arrs))


def _adamw(w, g, m, v):
    m = ADAM_B1 * m + (1.0 - ADAM_B1) * g
    v = ADAM_B2 * v + (1.0 - ADAM_B2) * (g * g)
    m_hat = m / (1.0 - ADAM_B1 ** ADAM_STEP)
    v_hat = v / (1.0 - ADAM_B2 ** ADAM_STEP)
    delta = -ADAM_LR * (m_hat / (jnp.sqrt(v_hat) + ADAM_EPS) + ADAM_WD * w)
    return delta, m, v


def _adamw_sum(name, recvs, w, m, v):
    L, R, C = w.shape
    tm = 128 if R % 128 == 0 else 64
    assert R % tm == 0 and len(recvs) == L

    def body(*refs):
        r_refs, (w_ref, m_ref, v_ref, g_ref, d_ref, nm_ref, nv_ref) = refs[:L], refs[L:]
        for l in range(L):
            g = r_refs[l][0].astype(F32)
            for s in range(1, N_DEV):
                g = g + r_refs[l][s].astype(F32)
            g_ref[l] = g
            d_ref[l], nm_ref[l], nv_ref[l] = _adamw(w_ref[l], g, m_ref[l], v_ref[l])

    blk = pl.BlockSpec((L, tm, C), lambda i: (0, i, 0))
    return pl.pallas_call(
        body, name=name, grid=(R // tm,),
        in_specs=[pl.BlockSpec((N_DEV, tm, C), lambda i: (0, i, 0))] * L + [blk, blk, blk],
        out_specs=[blk] * 4, out_shape=[jax.ShapeDtypeStruct((L, R, C), F32)] * 4,
        compiler_params=_params(dimension_semantics=("arbitrary",)),
    )(*recvs, w, m, v)


def _small_sync(part, w, m, v):
    def body(p_ref, w_ref, m_ref, v_ref, g_ref, d_ref, nm_ref, nv_ref, gath, send_sems, recv_sems):
        x, y, c = _place()
        me = 4 * x + 2 * y + c
        gath[me] = p_ref[...]
        copies = []
        for k, peer, _ in _peers(x, y, c):
            cp = pltpu.make_async_remote_copy(
                src_ref=p_ref, dst_ref=gath.at[me], send_sem=send_sems.at[k - 1], recv_sem=recv_sems.at[k - 1],
                device_id=peer, device_id_type=MESH)
            cp.start()
            copies.append(cp)
        for cp in copies:
            cp.wait()
        g = gath[0]
        for s in range(1, N_DEV):
            g = g + gath[s]
        wv = w_ref[...]
        l0, l1 = w_ref[8:9, :], w_ref[9:10, :]
        mx = jnp.maximum(l0, l1)
        e0, e1 = jnp.exp(l0 - mx), jnp.exp(l1 - mx)
        g9 = g[9:10, :] * (e0 / (e0 + e1)) * (e1 / (e0 + e1))
        row = lax.broadcasted_iota(jnp.int32, g.shape, 0)
        g = jnp.where(row == 9, g9, jnp.where(row == 8, -g9, g))
        g_ref[...] = g
        d_ref[...], nm_ref[...], nv_ref[...] = _adamw(wv, g, m_ref[...], v_ref[...])

    vm = pl.BlockSpec(memory_space=pltpu.VMEM)
    return pl.pallas_call(
        body, name="small_params_sync", in_specs=[vm] * 4, out_specs=[vm] * 4,
        out_shape=[jax.ShapeDtypeStruct(part.shape, F32)] * 4,
        scratch_shapes=[pltpu.VMEM((N_DEV,) + part.shape, F32), pltpu.SemaphoreType.DMA((7,)),
                        pltpu.SemaphoreType.DMA((7,))],
    )(part, w, m, v)


def _shards_bf16(d, pieces):
    return [d[name][layer].astype(BF16) for name, layer in pieces]


def _gathered(arrs, pieces, out):
    for a, (name, layer) in zip(arrs, pieces):
        out[name, layer] = a if name in COL_SHARDED else a.reshape(N_DEV * a.shape[1], a.shape[2])


def _pad_row(a, width=D_MODEL):
    a = a.reshape(1, -1)
    return jnp.pad(a, ((0, 0), (0, width - a.shape[1])))


def _pack_small(d, gn_full):
    rows = [d["mix_norm"], d["mlp_norm"], d["final_norm"].reshape(1, D_MODEL),
            _pad_row(d["attn_b_qkv"], 2 * D_MODEL).reshape(2, D_MODEL), _pad_row(d["attn_sinks"]),
            d["hgrn_lower_bounds"], gn_full.reshape(1, D_MODEL)]
    p = jnp.concatenate(rows, axis=0)
    return jnp.pad(p, ((0, SMALL_ROWS - p.shape[0]), (0, 0)))


def _unpack_small(p, me):
    return dict(
        mix_norm=p[0:2], mlp_norm=p[2:4], final_norm=p[4],
        attn_b_qkv=p[5:7].reshape(1, 2 * D_MODEL)[:, :QKV_DIM], attn_sinks=p[7:8, :N_Q_HEADS],
        hgrn_lower_bounds=p[8:10], hgrn_g_norm=lax.dynamic_slice(p[10:11], (0, me * 128), (1, 128)))


WEIGHT_NAMES = ['mix_norm', 'mlp_norm', 'final_norm', 'attn_w_qkv', 'attn_b_qkv', 'attn_sinks', 'attn_w_o', 'hgrn_w_in',
                'hgrn_g_norm', 'hgrn_w_o', 'hgrn_lower_bounds', 'mlp_w_up', 'mlp_w_down']
SMALL_NAMES = ('mix_norm', 'mlp_norm', 'final_norm', 'attn_b_qkv', 'attn_sinks', 'hgrn_lower_bounds', 'hgrn_g_norm')


def _rotary_tables(positions):
    inv_freq = ROPE_THETA ** (-jnp.arange(0, 2 * ROT_HALF, 2, dtype=F32) / (2 * ROT_HALF))
    ang = positions.astype(F32).reshape(-1, 1) * inv_freq
    cos, sin = jnp.cos(ang), jnp.sin(ang)
    r = jnp.arange(LANES) % HEAD_DIM
    idx = r % ROT_HALF
    c = jnp.where(r < 2 * ROT_HALF, cos[:, idx], 1.0)
    sa = jnp.where((r >= ROT_HALF) & (r < 2 * ROT_HALF), sin[:, idx], 0.0)
    sb = jnp.where(r < ROT_HALF, -sin[:, idx], 0.0)
    return jnp.concatenate([c, sa, sb], axis=1)


def kernel(x, positions, mix_norm, mlp_norm, final_norm, attn_w_qkv, attn_b_qkv, attn_sinks, attn_w_o, hgrn_w_in, hgrn_g_norm, hgrn_w_o, hgrn_lower_bounds, mlp_w_up, mlp_w_down, loss_target, m_mix_norm, m_mlp_norm, m_final_norm, m_attn_w_qkv, m_attn_b_qkv, m_attn_sinks, m_attn_w_o, m_hgrn_w_in, m_hgrn_g_norm, m_hgrn_w_o, m_hgrn_lower_bounds, m_mlp_w_up, m_mlp_w_down, v_mix_norm, v_mlp_norm, v_final_norm, v_attn_w_qkv, v_attn_b_qkv, v_attn_sinks, v_attn_w_o, v_hgrn_w_in, v_hgrn_g_norm, v_hgrn_w_o, v_hgrn_lower_bounds, v_mlp_w_up, v_mlp_w_down):
    w = dict(mix_norm=mix_norm, mlp_norm=mlp_norm, final_norm=final_norm, attn_w_qkv=attn_w_qkv, attn_b_qkv=attn_b_qkv,
             attn_sinks=attn_sinks, attn_w_o=attn_w_o, hgrn_w_in=hgrn_w_in, hgrn_g_norm=hgrn_g_norm, hgrn_w_o=hgrn_w_o,
             hgrn_lower_bounds=hgrn_lower_bounds, mlp_w_up=mlp_w_up, mlp_w_down=mlp_w_down)
    m = dict(mix_norm=m_mix_norm, mlp_norm=m_mlp_norm, final_norm=m_final_norm, attn_w_qkv=m_attn_w_qkv,
             attn_b_qkv=m_attn_b_qkv, attn_sinks=m_attn_sinks, attn_w_o=m_attn_w_o, hgrn_w_in=m_hgrn_w_in,
             hgrn_g_norm=m_hgrn_g_norm, hgrn_w_o=m_hgrn_w_o, hgrn_lower_bounds=m_hgrn_lower_bounds, mlp_w_up=m_mlp_w_up,
             mlp_w_down=m_mlp_w_down)
    v = dict(mix_norm=v_mix_norm, mlp_norm=v_mlp_norm, final_norm=v_final_norm, attn_w_qkv=v_attn_w_qkv,
             attn_b_qkv=v_attn_b_qkv, attn_sinks=v_attn_sinks, attn_w_o=v_attn_w_o, hgrn_w_in=v_hgrn_w_in,
             hgrn_g_norm=v_hgrn_g_norm, hgrn_w_o=v_hgrn_w_o, hgrn_lower_bounds=v_hgrn_lower_bounds, mlp_w_up=v_mlp_w_up,
             mlp_w_down=v_mlp_w_down)
    me = 4 * lax.axis_index("x") + 2 * lax.axis_index("y") + lax.axis_index("c")

    gn = hgrn_g_norm.reshape(1, 128)
    gn_a = gn.astype(BF16)
    gn_b = (gn - gn_a.astype(F32)).astype(BF16)
    gn_c = (gn - gn_a.astype(F32) - gn_b.astype(F32)).astype(BF16)
    gn_rows = jnp.pad(jnp.concatenate([gn_a, gn_b, gn_c], axis=1), ((0, 15), (0, D_MODEL - 3 * 128)))
    full = {}
    got = _all_gather("gather_attn_weights", _shards_bf16(w, GATHER_FIRST) + [gn_rows])
    _gathered(got[:1], GATHER_FIRST, full)
    w_qkv = full["attn_w_qkv", 0].transpose(1, 0, 2).reshape(D_MODEL, QKV_DIM)
    gn_terms = got[1][:, 0, :3 * 128].astype(F32).reshape(N_DEV, 3, 128)
    gn_full = ((gn_terms[:, 0] + gn_terms[:, 1]) + gn_terms[:, 2]).reshape(1, D_MODEL)

    x0 = x[0]
    tgt = loss_target[0]
    rot = _rotary_tables(positions)
    row = lambda a: a.reshape(1, -1)

    qkv, h0 = _norm_mm("qkv_proj", x0, row(mix_norm[0]), w_qkv, attn_b_qkv, rot=rot)
    att, *got = _attn_fwd(qkv, attn_sinks, carry=(_Gather, _shards_bf16(w, GATHER_ATTN)))
    _gathered(got, GATHER_ATTN, full)
    x1 = _mm_res("attn_out_proj", att, full["attn_w_o", 0], x0)
    u0, h1, *got = _norm_mm("mlp0_up", x1, row(mlp_norm[0]), full["mlp_w_up", 0],
                            carry=(_Gather, _shards_bf16(w, GATHER_MLP0)))
    _gathered(got, GATHER_MLP0, full)
    x2, a0 = _mlp_down("mlp0_down", u0, full["mlp_w_down", 0], x1)
    z, h2 = _norm_mm("hgrn_in_proj", x2, row(mix_norm[1]), full["hgrn_w_in", 0])
    o_raw, states, *got = _hgrn_fwd(z, hgrn_lower_bounds, carry=(_Gather, _shards_bf16(w, GATHER_HGRN)))
    _gathered(got, GATHER_HGRN, full)
    x3, o2 = _hgrn_out("hgrn_out_proj", o_raw, z, gn_full, full["hgrn_w_o", 0], x2)
    u1, h3 = _norm_mm("mlp1_up", x3, row(mlp_norm[1]), full["mlp_w_up", 1])
    dx4, a1, loss_part, g_final = _mlp_down("mlp1_down_loss", u1, full["mlp_w_down", 1], x3,
                                            loss_head=(tgt, row(final_norm)))

    gw = {}
    du1, = _mlp_bwd_act("mlp1_bwd_act", dx4, u1, full["mlp_w_down", 1])
    dx3, g_mlp1 = _mm_nt_rmsbwd("mlp1_bwd_in", du1, full["mlp_w_up", 1], x3, row(mlp_norm[1]), dx4)
    gw["mlp_w_down", 1] = _mm_tn("mlp1_dw_down", a1, dx4, "rows")
    gw["mlp_w_up", 1] = _mm_tn("mlp1_dw_up", h3, du1, "cols")

    do_raw, dg, g_gn = _hgrn_out_bwd("hgrn_out_bwd", dx3, o_raw, z, full["hgrn_w_o", 0], gn_full)
    gw["hgrn_w_o", 0] = _mm_tn("hgrn_dw_o", o2, dx3, "rows")
    recvs = {}
    dzq, dzf, dzi, g_lb, *recv = _hgrn_bwd(z, hgrn_lower_bounds, states, do_raw,
                                           carry=(_Exchange, [gw[p] for p in GRADS_HGRN]))
    recvs.update(zip(GRADS_HGRN, recv))
    dz = [dzq, dzf, dzi, dg]
    dx2, g_mix1 = _mm_nt_rmsbwd("hgrn_in_bwd", dz, full["hgrn_w_in", 0], x2, row(mix_norm[1]), dx3)
    gw["hgrn_w_in", 0] = jnp.concatenate(
        [_mm_tn(f"hgrn_dw_in{j}", h2, d, "cols") for j, d in enumerate(dz)], axis=0)

    du0, *recv = _mlp_bwd_act("mlp0_bwd_act", dx2, u0, full["mlp_w_down", 0],
                              carry=(_Exchange, [gw[p] for p in GRADS_MLP0]))
    recvs.update(zip(GRADS_MLP0, recv))
    dx1, g_mlp0 = _mm_nt_rmsbwd("mlp0_bwd_in", du0, full["mlp_w_up", 0], x1, row(mlp_norm[0]), dx2)
    gw["mlp_w_down", 0] = _mm_tn("mlp0_dw_down", a0, dx2, "rows")
    gw["mlp_w_up", 0] = _mm_tn("mlp0_dw_up", h1, du0, "cols")

    datt = _mm_nt("attn_out_bwd", dx1, full["attn_w_o", 0], BF16)
    gw["attn_w_o", 0] = _mm_tn("attn_dw_o", att, dx1, "rows")
    dqkv, g_sink, *recv = _attn_bwd(qkv, rot, attn_sinks, datt, carry=(_Exchange, [gw[p] for p in GRADS_ATTN]))
    recvs.update(zip(GRADS_ATTN, recv))
    g_qkv = _mm_tn("attn_dw_qkv", h0, dqkv, bn=512)
    g_qkv = g_qkv.reshape(D_MODEL, N_DEV, QKV_DIM // N_DEV).transpose(1, 0, 2).astype(BF16)
    dx0, g_mix0, g_bqkv, recvs["attn_w_qkv", 0] = _mm_nt_rmsbwd(
        "qkv_bwd", dqkv, w_qkv, x0, row(mix_norm[0]), dx1, with_colsum=True, carry=(_Exchange, [g_qkv]))

    big = {name: _adamw_sum("adamw_" + name, [recvs[name, l] for l in range(w[name].shape[0])], w[name], m[name], v[name])
           for name in BIG_NAMES}

    zero_row = jnp.zeros((1, D_MODEL), F32)
    part = _pack_small(dict(
        mix_norm=jnp.concatenate([g_mix0, g_mix1], axis=0), mlp_norm=jnp.concatenate([g_mlp0, g_mlp1], axis=0),
        final_norm=g_final, attn_b_qkv=g_bqkv, attn_sinks=g_sink[:, :N_Q_HEADS],
        hgrn_lower_bounds=jnp.concatenate([zero_row, g_lb], axis=0)), g_gn)

    def spread(a):
        return lax.dynamic_update_slice(zero_row, a.reshape(1, 128), (0, me * 128))

    small_in = [_pack_small({n: d[n] for n in SMALL_NAMES if n != "hgrn_g_norm"}, spread(d["hgrn_g_norm"]))
                for d in (w, m, v)]
    small = [_unpack_small(p, me) for p in _small_sync(part, *small_in)]

    loss = lax.psum(loss_part[0, 0], ("x", "y", "c"))
    outs = [loss, dx0.reshape(x.shape)]
    for kind, grp_small in enumerate(small):
        for name in WEIGHT_NAMES:
            val = grp_small[name] if name in SMALL_NAMES else big[name][kind]
            outs.append(val.reshape(w[name].shape))
    return tuple(outs)
```

```python
import functools

import jax
import jax.numpy as jnp
from jax import lax
from jax.experimental import pallas as pl
from jax.experimental.pallas import tpu as pltpu

F32 = jnp.float32
BF16 = jnp.bfloat16

D_MODEL = 1024
HEAD_DIM = 64
N_Q_HEADS = 16
Q_DIM = 1024
KV_DIM = 256
QKV_DIM = 1536
ATT_BLOCK = 128
ROT_HALF = 8
ROPE_THETA = 500000.0
NEG_INF = -1e30
HGRN_HEADS = 8
HGRN_DK = 128
CHUNK = 64
D_FF = 4096
NORM_EPS = 1e-5
N_DEV = 8

ADAM_LR = 0.001
ADAM_B1 = 0.9
ADAM_B2 = 0.999
ADAM_EPS = 1e-08
ADAM_WD = 0.01
ADAM_STEP = 10

LANES = 128
VMEM_LIMIT = 56 * 1024 * 1024

GATHER_FIRST = (("attn_w_qkv", 0),)
GATHER_ATTN = (("attn_w_o", 0), ("mlp_w_up", 0), ("mlp_w_down", 0))
GATHER_MLP0 = (("hgrn_w_in", 0), ("hgrn_w_o", 0))
GATHER_HGRN = (("mlp_w_up", 1), ("mlp_w_down", 1))
GRADS_HGRN = (("mlp_w_down", 1), ("mlp_w_up", 1), ("hgrn_w_o", 0))
GRADS_MLP0 = (("hgrn_w_in", 0),)
GRADS_ATTN = (("mlp_w_down", 0), ("mlp_w_up", 0), ("attn_w_o", 0))
COL_SHARDED = ("attn_w_qkv", "hgrn_w_in", "mlp_w_up")
BIG_NAMES = ("attn_w_qkv", "attn_w_o", "hgrn_w_in", "hgrn_w_o", "mlp_w_up", "mlp_w_down")
SMALL_ROWS = 16


def _dot(a, b):
    return jnp.dot(a, b, preferred_element_type=F32)


def _dot_nt(a, b):
    return lax.dot_general(a, b, (((1,), (1,)), ((), ())), preferred_element_type=F32)


def _dot_tn(a, b):
    return lax.dot_general(a, b, (((0,), (0,)), ((), ())), preferred_element_type=F32)


def _params(**kw):
    return pltpu.CompilerParams(vmem_limit_bytes=VMEM_LIMIT, **kw)


def _full_spec(a):
    nd = a.ndim
    return pl.BlockSpec(a.shape, lambda *_: (0,) * nd)


def _row_call(name, body, n_rows, tm, row_ins, full_ins, row_outs, acc_outs=(), carry=(None, None)):
    steps = n_rows // tm
    in_specs = [pl.BlockSpec((tm, w), functools.partial(lambda i, cb: (i, cb), cb=cb)) for _, w, cb in row_ins]
    in_specs += [_full_spec(a) for a in full_ins]
    out_shape = [jax.ShapeDtypeStruct((n_rows, w), dt) for w, dt in row_outs]
    out_specs = [pl.BlockSpec((tm, w), lambda i: (i, 0)) for w, _ in row_outs]
    for shp, dt in acc_outs:
        out_shape.append(jax.ShapeDtypeStruct(shp, dt))
        out_specs.append(pl.BlockSpec(shp, functools.partial(lambda i, nd: (0,) * nd, nd=len(shp))))
    n_in, n_out = len(in_specs), len(out_specs)
    in_specs, out_specs, out_shape, scratch, extra = _carried_specs(carry, in_specs, out_specs, out_shape, [])

    def wrapped(*refs):
        i = pl.program_id(0)
        own, finish = _carried(carry, refs, n_in, n_out, i == 0, i == steps - 1)
        body(*own)
        finish()

    return pl.pallas_call(
        wrapped, name=name, grid=(steps,), in_specs=in_specs, out_specs=out_specs, out_shape=out_shape,
        scratch_shapes=scratch, compiler_params=_params(dimension_semantics=("arbitrary",)),
    )(*[a for a, _, _ in row_ins], *full_ins, *extra)


def _rms(x, gain):
    r = lax.rsqrt(jnp.mean(x * x, axis=-1, keepdims=True) + NORM_EPS)
    xhat = x * r
    return xhat * gain, xhat, r


def _rms_bwd(dy, xhat, r, gain):
    dxhat = dy * gain
    dx = r * (dxhat - xhat * jnp.mean(dxhat * xhat, axis=-1, keepdims=True))
    return dx, dy * xhat


def _norm_mm(name, x, gain, w, bias=None, rot=None, tm=512, carry=(None, None)):
    T = x.shape[0]
    tm = min(tm, T)
    nc = 512
    blocked = w.ndim == 3
    n = N_DEV * w.shape[2] if blocked else w.shape[1]
    assert n % nc == 0 and (not blocked or w.shape[2] == nc)

    def body(*refs):
        x_ref, refs = refs[0], refs[1:]
        if rot is not None:
            t_ref, refs = refs[0], refs[1:]
        g_ref, w_ref, refs = refs[0], refs[1], refs[2:]
        if bias is not None:
            b_ref, refs = refs[0], refs[1:]
        y_ref, h_ref = refs
        h, _, _ = _rms(x_ref[...], g_ref[...])
        hb = h.astype(BF16)
        h_ref[...] = hb
        for c in range(n // nc):
            sl = slice(c * nc, (c + 1) * nc)
            y = _dot(hb, w_ref[c] if blocked else w_ref[:, sl])
            if bias is not None:
                y = y + b_ref[:, sl]
            if rot is None:
                y_ref[:, sl] = y
            else:
                n_rot = max(0, min(nc, Q_DIM + KV_DIM - c * nc)) // LANES
                pieces = _rot_fwd(y[:, :n_rot * LANES], t_ref[...]) if n_rot else []
                for j in range(nc // LANES):
                    col = slice(c * nc + j * LANES, c * nc + (j + 1) * LANES)
                    y_ref[:, col] = pieces[j] if j < n_rot else y[:, j * LANES:(j + 1) * LANES]

    rows = [(x, D_MODEL, 0)] + ([(rot, 3 * LANES, 0)] if rot is not None else [])
    full = [gain, w] + ([bias] if bias is not None else [])
    return _row_call(name, body, T, tm, rows, full, [(n, F32), (D_MODEL, BF16)], carry=carry)


def _mm_res(name, a, w, res, tm=512):
    T = a.shape[0]
    tm = min(tm, T)

    def body(a_ref, r_ref, w_ref, o_ref):
        o_ref[...] = r_ref[...] + _dot(a_ref[...], w_ref[...])

    return _row_call(name, body, T, tm, [(a, a.shape[1], 0), (res, D_MODEL, 0)], [w], [(D_MODEL, F32)])[0]


def _mlp_down(name, u, w, res, tm=512, loss_head=None):
    T = u.shape[0]
    tm = min(tm, T)
    kc = 1024
    sub = min(256, tm)

    def body(*refs):
        if loss_head is None:
            u_ref, r_ref, w_ref, o_ref, a_ref = refs
        else:
            u_ref, r_ref, t_ref, w_ref, g_ref, o_ref, a_ref, loss_ref, dg_ref = refs

            @pl.when(pl.program_id(0) == 0)
            def _():
                loss_ref[...] = jnp.zeros_like(loss_ref)
                dg_ref[...] = jnp.zeros_like(dg_ref)

        for r0 in range(0, tm, sub):
            rs = slice(r0, r0 + sub)
            acc = r_ref[rs, :]
            for c in range(D_FF // kc):
                sl = slice(c * kc, (c + 1) * kc)
                a = jnp.maximum(u_ref[rs, sl], 0.0)
                ab = (a * a).astype(BF16)
                a_ref[rs, sl] = ab
                acc = acc + _dot(ab, w_ref[sl, :])
            if loss_head is None:
                o_ref[rs, :] = acc
            else:
                gain_v = g_ref[...]
                y, xhat, r = _rms(acc, gain_v)
                diff = y - t_ref[rs, :]
                per_row = jnp.sum(diff * diff, axis=-1, keepdims=True) * (1.0 / D_MODEL)
                loss_ref[...] += jnp.broadcast_to(0.5 * jnp.sum(per_row, axis=0, keepdims=True), loss_ref.shape)
                dx, dgr = _rms_bwd(diff * (1.0 / D_MODEL), xhat, r, gain_v)
                o_ref[rs, :] = dx
                dg_ref[...] += jnp.sum(dgr, axis=0, keepdims=True)

    rows, full, acc_outs = [(u, D_FF, 0), (res, D_MODEL, 0)], [w], []
    if loss_head is not None:
        rows, full = rows + [(loss_head[0], D_MODEL, 0)], full + [loss_head[1]]
        acc_outs = [((1, LANES), F32), ((1, D_MODEL), F32)]
    return _row_call(name, body, T, tm, rows, full, [(D_MODEL, F32), (D_FF, BF16)], acc_outs)


def _hgrn_out(name, o_raw, z, gn, w, res, tm=512):
    T = o_raw.shape[0]
    tm = min(tm, T)

    def body(o_ref, g_ref, r_ref, gn_ref, w_ref, x_ref, a_ref):
        y, _, _ = _rms(o_ref[...], gn_ref[...])
        g = g_ref[...]
        a = (y * (g * jax.nn.sigmoid(g))).astype(BF16)
        a_ref[...] = a
        x_ref[...] = r_ref[...] + _dot(a, w_ref[...])

    return _row_call(name, body, T, tm, [(o_raw, D_MODEL, 0), (z, D_MODEL, 3), (res, D_MODEL, 0)], [gn, w],
                     [(D_MODEL, F32), (D_MODEL, BF16)])


def _mm_nt_rmsbwd(name, dy, w, x, gain, dres, tm=512, with_colsum=False, carry=(None, None)):
    T = x.shape[0]
    tm = min(tm, T)
    dys = list(dy) if isinstance(dy, (list, tuple)) else [dy]
    width = dys[0].shape[1]
    n = width * len(dys)
    sub = min(256, tm)
    assert not with_colsum or len(dys) == 1

    def body(*refs):
        dy_refs, refs = refs[:len(dys)], refs[len(dys):]
        if with_colsum:
            x_ref, dr_ref, w_ref, g_ref, dx_ref, dg_ref, cs_ref = refs
        else:
            x_ref, dr_ref, w_ref, g_ref, dx_ref, dg_ref = refs

        @pl.when(pl.program_id(0) == 0)
        def _():
            dg_ref[...] = jnp.zeros_like(dg_ref)
            if with_colsum:
                cs_ref[...] = jnp.zeros_like(cs_ref)

        gain_v = g_ref[...]
        for r0 in range(0, tm, sub):
            rs = slice(r0, r0 + sub)
            if w.ndim == 3:
                nb = w.shape[2]
                dh = None
                for p in range(N_DEV):
                    piece, off = divmod(p * nb, width)
                    part = _dot_nt(dy_refs[piece][rs, off:off + nb].astype(BF16), w_ref[p])
                    dh = part if dh is None else dh + part
            else:
                dh = _dot_nt(dy_refs[0][rs, :].astype(BF16), w_ref[...])
            _, xhat, r = _rms(x_ref[rs, :], gain_v)
            dx, dgr = _rms_bwd(dh, xhat, r, gain_v)
            dx_ref[rs, :] = dr_ref[rs, :] + dx
            dg_ref[...] += jnp.sum(dgr, axis=0, keepdims=True)
            if with_colsum:
                cs_ref[...] += jnp.sum(dy_refs[0][rs, :].astype(F32), axis=0, keepdims=True)

    acc = [((1, D_MODEL), F32)] + ([((1, n), F32)] if with_colsum else [])
    rows = [(d, width, 0) for d in dys] + [(x, D_MODEL, 0), (dres, D_MODEL, 0)]
    return _row_call(name, body, T, tm, rows, [w, gain], [(D_MODEL, F32)], acc, carry=carry)


def _mm_nt(name, dy, w, out_dtype, tm=512):
    T = dy.shape[0]
    tm = min(tm, T)
    k = w.shape[0]

    def body(dy_ref, w_ref, o_ref):
        o_ref[...] = _dot_nt(dy_ref[...].astype(BF16), w_ref[...]).astype(out_dtype)

    return _row_call(name, body, T, tm, [(dy, dy.shape[1], 0)], [w], [(k, out_dtype)])[0]


def _mlp_bwd_act(name, dy, u, w_down, tm=512, carry=(None, None)):
    T = u.shape[0]
    tm = min(tm, T)
    kc = 1024

    def body(dy_ref, u_ref, w_ref, du_ref):
        dyb = dy_ref[...].astype(BF16)
        for c in range(D_FF // kc):
            sl = slice(c * kc, (c + 1) * kc)
            da = _dot_nt(dyb, w_ref[sl, :])
            du_ref[:, sl] = (da * (2.0 * jnp.maximum(u_ref[:, sl], 0.0))).astype(BF16)

    return _row_call(name, body, T, tm, [(dy, D_MODEL, 0), (u, D_FF, 0)], [w_down], [(D_FF, BF16)], carry=carry)


def _hgrn_out_bwd(name, dx, o_raw, z, w, gn, tm=512):
    T = dx.shape[0]
    tm = min(tm, T)

    def body(dx_ref, o_ref, g_ref, w_ref, gn_ref, do_ref, dg_ref, dgn_ref):
        @pl.when(pl.program_id(0) == 0)
        def _():
            dgn_ref[...] = jnp.zeros_like(dgn_ref)

        da = _dot_nt(dx_ref[...].astype(BF16), w_ref[...])
        gn_v = gn_ref[...]
        y, xhat, r = _rms(o_ref[...], gn_v)
        g = g_ref[...]
        sg = jax.nn.sigmoid(g)
        dg_ref[...] = (da * y * (sg * (1.0 + g * (1.0 - sg)))).astype(BF16)
        dyn = da * (g * sg)
        do, dgr = _rms_bwd(dyn, xhat, r, gn_v)
        do_ref[...] = do
        dgn_ref[...] += jnp.sum(dgr, axis=0, keepdims=True)

    return _row_call(name, body, T, tm, [(dx, D_MODEL, 0), (o_raw, D_MODEL, 0), (z, D_MODEL, 3)], [w, gn],
                     [(D_MODEL, F32), (D_MODEL, BF16)], [((1, D_MODEL), F32)])


COL_BLOCK = D_FF // N_DEV


def _mm_tn(name, a, b, shard=None, bm=1024, bn=1024, tk=2048):
    T, M = a.shape
    N = b.shape[1]
    bm, bn, tk = min(bm, M), min(bn, N), min(tk, T)
    nk = T // tk
    if shard is None:
        out_shape, out_block = jax.ShapeDtypeStruct((M, N), F32), (bm, bn)
        out_map = lambda i, j, k: (i, j)
    elif shard == "cols":
        assert bn % COL_BLOCK == 0 and N % bn == 0
        out_shape = jax.ShapeDtypeStruct((N // COL_BLOCK, M, COL_BLOCK), BF16)
        out_block = (bn // COL_BLOCK, bm, COL_BLOCK)
        out_map = lambda i, j, k: (j, i, 0)
    else:
        rows = M // N_DEV
        assert bm % rows == 0
        out_shape, out_block = jax.ShapeDtypeStruct((N_DEV, rows, N), BF16), (bm // rows, rows, bn)
        out_map = lambda i, j, k: (i, 0, j)

    def body(a_ref, b_ref, o_ref, acc):
        k = pl.program_id(2)

        @pl.when(k == 0)
        def _():
            acc[...] = jnp.zeros_like(acc)

        acc[...] += _dot_tn(a_ref[...].astype(BF16), b_ref[...].astype(BF16))

        @pl.when(k == nk - 1)
        def _():
            if shard == "cols":
                for c in range(bn // COL_BLOCK):
                    o_ref[c] = acc[:, c * COL_BLOCK:(c + 1) * COL_BLOCK].astype(BF16)
            else:
                o_ref[...] = acc[...].reshape(out_block).astype(o_ref.dtype)

    return pl.pallas_call(
        body, name=name, grid=(M // bm, N // bn, nk),
        in_specs=[pl.BlockSpec((tk, bm), lambda i, j, k: (k, i)), pl.BlockSpec((tk, bn), lambda i, j, k: (k, j))],
        out_specs=pl.BlockSpec(out_block, out_map), out_shape=out_shape,
        scratch_shapes=[pltpu.VMEM((bm, bn), F32)],
        compiler_params=_params(dimension_semantics=("parallel", "parallel", "arbitrary")),
    )(a, b)


def _rot_fwd(x, tab):
    c, sa, sb = tab[:, :LANES], tab[:, LANES:2 * LANES], tab[:, 2 * LANES:]
    outs = []
    for j in range(x.shape[1] // LANES):
        xs = x[:, j * LANES:(j + 1) * LANES]
        outs.append(xs * c + pltpu.roll(xs, ROT_HALF, 1) * sa + pltpu.roll(xs, LANES - ROT_HALF, 1) * sb)
    return outs


def _rot_bwd(dys, tab):
    c, sa, sb = tab[:, :LANES], tab[:, LANES:2 * LANES], tab[:, 2 * LANES:]
    return [dy * c + pltpu.roll(dy * sa, LANES - ROT_HALF, 1) + pltpu.roll(dy * sb, ROT_HALF, 1) for dy in dys]


ATT_SCALE = HEAD_DIM ** -0.5


def _attn_masks(n):
    kj = lax.broadcasted_iota(jnp.int32, (2 * ATT_BLOCK, ATT_BLOCK), 0)
    qi = lax.broadcasted_iota(jnp.int32, (2 * ATT_BLOCK, ATT_BLOCK), 1)
    delta = qi + ATT_BLOCK - kj
    first_key = jnp.where(n > 0, 0, ATT_BLOCK)
    valid = (delta >= 0) & (delta < ATT_BLOCK) & (kj >= first_key)
    low = lax.broadcasted_iota(jnp.int32, (1, LANES), 1) < HEAD_DIM
    upper = lax.broadcasted_iota(jnp.int32, (LANES, 1), 0) < HEAD_DIM
    return valid, low, upper


def _softmax_sink(s, valid, sink):
    s = jnp.where(valid, s, NEG_INF)
    m = jnp.maximum(jnp.max(s, axis=0, keepdims=True), sink)
    e = jnp.exp(s - m)
    es = jnp.exp(sink - m)
    inv = 1.0 / (jnp.sum(e, axis=0, keepdims=True) + es)
    return e * inv, es * inv


def _attn_specs(nb, tables):
    prev = lambda n: jnp.maximum(jnp.minimum(n, nb - 1) - 1, 0)
    cur = lambda n: jnp.minimum(n, nb - 1)
    specs = [
        pl.BlockSpec((ATT_BLOCK, Q_DIM), lambda n: (cur(n), 0)),
        pl.BlockSpec((ATT_BLOCK, KV_DIM), lambda n: (prev(n), 4)),
        pl.BlockSpec((ATT_BLOCK, KV_DIM), lambda n: (cur(n), 4)),
        pl.BlockSpec((ATT_BLOCK, KV_DIM), lambda n: (prev(n), 5)),
        pl.BlockSpec((ATT_BLOCK, KV_DIM), lambda n: (cur(n), 5)),
    ]
    if tables:
        specs += [pl.BlockSpec((ATT_BLOCK, 3 * LANES), lambda n: (prev(n), 0)),
                  pl.BlockSpec((ATT_BLOCK, 3 * LANES), lambda n: (cur(n), 0))]
    return specs + [pl.BlockSpec(memory_space=pltpu.SMEM)]


def _kv_band(prev_ref, cur_ref):
    out = []
    for j in range(KV_DIM // LANES):
        sl = slice(j * LANES, (j + 1) * LANES)
        band = jnp.concatenate([prev_ref[:, sl], cur_ref[:, sl]], axis=0)
        out.append((band, pltpu.roll(band, HEAD_DIM, 1)))
    return out


def _bf16(bands, transposed=False):
    return [[(a.T if transposed else a).astype(BF16) for a in pair] for pair in bands]


def _attn_fwd(qkv, sinks, carry=(None, None)):
    T = qkv.shape[0]
    nb = T // ATT_BLOCK

    def body(*refs):
        n = pl.program_id(0)
        own, finish = _carried(carry, refs, 6, 1, n == 0, n == nb - 1)
        q_ref, kp_ref, kc_ref, vp_ref, vc_ref, sink_ref, o_ref = own
        valid, low, upper = _attn_masks(n)
        ks = _bf16(_kv_band(kp_ref, kc_ref))
        vts = _bf16(_kv_band(vp_ref, vc_ref), transposed=True)
        heads = []
        for p in range(Q_DIM // LANES):
            kpair, khalf = p // 4, (p // 2) % 2
            q_pair = q_ref[:, p * LANES:(p + 1) * LANES] * ATT_SCALE
            for hf in range(2):
                qm = jnp.where(low if hf == 0 else ~low, q_pair, 0.0).astype(BF16)
                sw = 0 if khalf == hf else 1
                heads.append((2 * p + hf, kpair, sw, _dot_nt(ks[kpair][sw], qm)))
        probs = [_softmax_sink(s, valid, sink_ref[0, h])[0].astype(BF16) for h, _, _, s in heads]
        outs = [_dot(vts[kpair][sw], pr) for (_, kpair, sw, _), pr in zip(heads, probs)]
        for p in range(Q_DIM // LANES):
            o_ref[:, p * LANES:(p + 1) * LANES] = jnp.where(upper, outs[2 * p], outs[2 * p + 1]).T.astype(BF16)
        finish()

    in_specs, out_specs, out_shape, scratch, extra = _carried_specs(
        carry, _attn_specs(nb, False), [pl.BlockSpec((ATT_BLOCK, Q_DIM), lambda n: (n, 0))],
        [jax.ShapeDtypeStruct((T, Q_DIM), BF16)], [])
    return pl.pallas_call(
        body, name="attn_fwd", grid=(nb,), in_specs=in_specs, out_specs=out_specs, out_shape=out_shape,
        scratch_shapes=scratch, compiler_params=_params(dimension_semantics=("arbitrary",)),
    )(qkv, qkv, qkv, qkv, qkv, sinks, *extra)


def _attn_bwd(qkv, rot, sinks, dout, carry=(None, None)):
    T = qkv.shape[0]
    nb = T // ATT_BLOCK
    npair = KV_DIM // LANES

    def body(*refs):
        n = pl.program_id(0)
        own, finish = _carried(carry, refs, 9, 2, n == 0, n == nb)
        (q_ref, kp_ref, kc_ref, vp_ref, vc_ref, tp_ref, tc_ref, sink_ref, do_ref, dqkv_ref, dsink_ref,
         dq_c, dk_c, dv_c) = own

        @pl.when(n == 0)
        def _():
            dq_c[...] = jnp.zeros_like(dq_c)
            dk_c[...] = jnp.zeros_like(dk_c)
            dv_c[...] = jnp.zeros_like(dv_c)
            dsink_ref[...] = jnp.zeros_like(dsink_ref)

        def flush(dk_prev, dv_prev, tab_ref):
            dqkv_ref[:, :Q_DIM] = dq_c[...].astype(BF16)
            dk = _rot_bwd([dk_c[:, j * LANES:(j + 1) * LANES] + dk_prev[j] for j in range(npair)], tab_ref[...])
            for j in range(npair):
                dqkv_ref[:, Q_DIM + j * LANES:Q_DIM + (j + 1) * LANES] = dk[j].astype(BF16)
                dqkv_ref[:, Q_DIM + KV_DIM + j * LANES:Q_DIM + KV_DIM + (j + 1) * LANES] = (
                    dv_c[:, j * LANES:(j + 1) * LANES] + dv_prev[j]).astype(BF16)

        @pl.when(n < nb)
        def _():
            valid, low, upper = _attn_masks(n)
            lane = lax.broadcasted_iota(jnp.int32, (1, LANES), 1)
            k_band = _kv_band(kp_ref, kc_ref)
            ks, kts = _bf16(k_band), _bf16(k_band, transposed=True)
            vs = _bf16(_kv_band(vp_ref, vc_ref))
            dk_acc = [[jnp.zeros((2 * ATT_BLOCK, LANES), F32) for _ in range(2)] for _ in range(npair)]
            dv_acc = [[jnp.zeros((2 * ATT_BLOCK, LANES), F32) for _ in range(2)] for _ in range(npair)]
            dsink = jnp.zeros((1, LANES), F32)
            heads = []
            for p in range(Q_DIM // LANES):
                kpair, khalf = p // 4, (p // 2) % 2
                q_pair = q_ref[:, p * LANES:(p + 1) * LANES] * ATT_SCALE
                do_pair = do_ref[:, p * LANES:(p + 1) * LANES]
                for hf in range(2):
                    sel = low if hf == 0 else ~low
                    qm = jnp.where(sel, q_pair, 0.0).astype(BF16)
                    dom = jnp.where(sel, do_pair, 0.0).astype(BF16)
                    sw = 0 if khalf == hf else 1
                    heads.append((2 * p + hf, kpair, sw, qm, dom,
                                  _dot_nt(ks[kpair][sw], qm), _dot_nt(vs[kpair][sw], dom)))
            grads = []
            for h, kpair, sw, qm, dom, s, dp in heads:
                pr, ps = _softmax_sink(s, valid, sink_ref[0, h])
                dd = jnp.sum(pr * dp, axis=0, keepdims=True)
                dsink = dsink + jnp.where(lane == h, -jnp.sum(ps * dd, axis=1, keepdims=True), 0.0)
                grads.append((pr * (dp - dd)).astype(BF16))
                heads[h] = (kpair, sw, qm, dom, pr.astype(BF16))
            dq_t = []
            for (kpair, sw, qm, dom, pr), ds in zip(heads, grads):
                dq_t.append(_dot(kts[kpair][sw], ds))
                dk_acc[kpair][sw] = dk_acc[kpair][sw] + _dot(ds, qm)
                dv_acc[kpair][sw] = dv_acc[kpair][sw] + _dot(pr, dom)
            dqs = [jnp.where(upper, dq_t[2 * p], dq_t[2 * p + 1]).T * ATT_SCALE for p in range(Q_DIM // LANES)]
            dk_acc = [a[0] + pltpu.roll(a[1], HEAD_DIM, 1) for a in dk_acc]
            dv_acc = [a[0] + pltpu.roll(a[1], HEAD_DIM, 1) for a in dv_acc]
            flush([a[:ATT_BLOCK] for a in dk_acc], [a[:ATT_BLOCK] for a in dv_acc], tp_ref)
            dq = _rot_bwd(dqs, tc_ref[...])
            for p in range(Q_DIM // LANES):
                dq_c[:, p * LANES:(p + 1) * LANES] = dq[p]
            for j in range(npair):
                dk_c[:, j * LANES:(j + 1) * LANES] = dk_acc[j][ATT_BLOCK:]
                dv_c[:, j * LANES:(j + 1) * LANES] = dv_acc[j][ATT_BLOCK:]
            dsink_ref[...] += dsink

        @pl.when(n == nb)
        def _():
            zero = [jnp.zeros((ATT_BLOCK, LANES), F32) for _ in range(npair)]
            flush(zero, zero, tc_ref)

        finish()

    do_spec = pl.BlockSpec((ATT_BLOCK, Q_DIM), lambda n: (jnp.minimum(n, nb - 1), 0))
    in_specs, out_specs, out_shape, scratch, extra = _carried_specs(
        carry, _attn_specs(nb, True) + [do_spec],
        [pl.BlockSpec((ATT_BLOCK, QKV_DIM), lambda n: (jnp.maximum(n - 1, 0), 0)),
         pl.BlockSpec((1, LANES), lambda n: (0, 0))],
        [jax.ShapeDtypeStruct((T, QKV_DIM), BF16), jax.ShapeDtypeStruct((1, LANES), F32)],
        [pltpu.VMEM((ATT_BLOCK, Q_DIM), F32), pltpu.VMEM((ATT_BLOCK, KV_DIM), F32),
         pltpu.VMEM((ATT_BLOCK, KV_DIM), F32)])
    return pl.pallas_call(
        body, name="attn_bwd", grid=(nb + 1,), in_specs=in_specs, out_specs=out_specs, out_shape=out_shape,
        scratch_shapes=scratch, compiler_params=_params(dimension_semantics=("arbitrary",)),
    )(qkv, qkv, qkv, qkv, qkv, rot, rot, sinks, dout, *extra)


LEVELS = (32, 16, 8)
DIAG = 8
SUBLANES = 8
UNROLL = 4
UNROLL_BWD = 2


def _lower_bound(lb_ref):
    l0, l1 = lb_ref[0:1, :], lb_ref[1:2, :]
    mx = jnp.maximum(l0, l1)
    e0, e1 = jnp.exp(l0 - mx), jnp.exp(l1 - mx)
    return e1 / (e0 + e1)


GROUPS = CHUNK // SUBLANES


def _group_roll(x, k):
    return pltpu.roll(x.reshape(GROUPS, SUBLANES, HGRN_DK), k % SUBLANES, 1).reshape(CHUNK, HGRN_DK)


def _scan_rows(x, row, reverse):
    r8 = row & (SUBLANES - 1)
    for sh in (1, 2, 4):
        ok = (r8 < SUBLANES - sh) if reverse else (r8 >= sh)
        x = x + jnp.where(ok, _group_roll(x, -sh if reverse else sh), 0.0)
    g = x.reshape(GROUPS, SUBLANES, HGRN_DK)
    edge = 0 if reverse else SUBLANES - 1
    tot = jnp.broadcast_to(g[:, edge:edge + 1, :], g.shape)

    def shifted(a, n):
        z = jnp.zeros((n, SUBLANES, HGRN_DK), F32)
        return jnp.concatenate([a[n:], z] if reverse else [z, a[:GROUPS - n]], axis=0)

    acc = shifted(tot, 1)
    for sh in (1, 2, 4):
        acc = acc + shifted(acc, sh)
    return (g + acc).reshape(CHUNK, HGRN_DK)


def _level_masks():
    t = lax.broadcasted_iota(jnp.int32, (CHUNK, CHUNK), 0)
    s = lax.broadcasted_iota(jnp.int32, (CHUNK, CHUNK), 1)
    return [((t & h) != 0) & ((s & h) == 0) & ((t ^ s) < 2 * h) for h in LEVELS]


def _level_scales(b):
    out = []
    for h in LEVELS:
        parts = [jnp.broadcast_to(b[j * 2 * h + h - 1:j * 2 * h + h, :], (2 * h, HGRN_DK))
                 for j in range(CHUNK // (2 * h))]
        mid = parts[0] if len(parts) == 1 else jnp.concatenate(parts, axis=0)
        out.append(jnp.exp(-jnp.abs(b - mid)))
    return out


def _hgrn_gates(zq, zf, lb):
    sq = jax.nn.sigmoid(zq)
    q = zq * sq
    sg = jax.nn.sigmoid(zf)
    forget = lb + (1.0 - lb) * sg
    return q, sq, sg, forget, 1.0 - forget, jnp.log(forget)


def _hgrn_specs(T, rb, rev):
    nr = T // rb
    ri = (lambda r: nr - 1 - r) if rev else (lambda r: r)
    return nr, ri, [
        pl.BlockSpec((rb, HGRN_DK), lambda h, r: (ri(r), h)),
        pl.BlockSpec((rb, HGRN_DK), lambda h, r: (ri(r), HGRN_HEADS + h)),
        pl.BlockSpec((rb, HGRN_DK), lambda h, r: (ri(r), 2 * HGRN_HEADS + h)),
        pl.BlockSpec((2, HGRN_DK), lambda h, r: (0, h)),
    ]


def _hgrn_fwd(z, lb_raw, rb=2048, carry=(None, None)):
    T = z.shape[0]
    rb = min(rb, T)
    ncb = rb // CHUNK
    nr, ri, in_specs = _hgrn_specs(T, rb, False)

    def body(*refs):
        hh, rr = pl.program_id(0), pl.program_id(1)
        own, finish = _carried(carry, refs, 4, 2, (hh == 0) & (rr == 0), (hh == HGRN_HEADS - 1) & (rr == nr - 1))
        zq_ref, zf_ref, zi_ref, lb_ref, o_ref, st_ref, state = own

        @pl.when(rr == 0)
        def _():
            state[...] = jnp.zeros_like(state)

        lb = _lower_bound(lb_ref)
        row = lax.broadcasted_iota(jnp.int32, (CHUNK, HGRN_DK), 0)
        masks = _level_masks()
        rd = row & (DIAG - 1)

        def chunk(c, st):
            rows = pl.ds(pl.multiple_of(c * CHUNK, CHUNK), CHUNK)
            q, _, _, _, k, lf = _hgrn_gates(zq_ref[rows, :], zf_ref[rows, :], lb)
            v = zi_ref[rows, :]
            vb = v.astype(BF16)
            b = _scan_rows(lf, row, False)
            sc = jnp.zeros((CHUNK, CHUNK), F32)
            for e, mask in zip(_level_scales(b), masks):
                sc = sc + jnp.where(mask, _dot_nt((q * e).astype(BF16), (k * e).astype(BF16)), 0.0)
            o = _dot(sc.astype(BF16), vb) + jnp.sum(q * k, axis=-1, keepdims=True) * v
            for d in range(1, DIAG):
                w = jnp.where(rd >= d, q * _group_roll(k, d) * jnp.exp(b - _group_roll(b, d)), 0.0)
                o = o + jnp.sum(w, axis=-1, keepdims=True) * _group_roll(v, d)
            b_last = b[CHUNK - 1:CHUNK, :]
            kd = (k * jnp.exp(b_last - b)).astype(BF16)
            qd = (q * jnp.exp(b)).astype(BF16)
            st_ref[c, 0] = st
            o_ref[rows, :] = o + _dot_nt(qd, st.astype(BF16))
            return st * jnp.exp(b_last) + _dot_tn(vb, kd)

        def group(i, st):
            for j in range(UNROLL):
                st = chunk(i * UNROLL + j, st)
            return st

        state[...] = lax.fori_loop(0, ncb // UNROLL, group, state[...])
        finish()

    in_specs, out_specs, out_shape, scratch, extra = _carried_specs(
        carry, in_specs,
        [pl.BlockSpec((rb, HGRN_DK), lambda h, r: (r, h)),
         pl.BlockSpec((ncb, 1, HGRN_DK, HGRN_DK), lambda h, r: (r, h, 0, 0))],
        [jax.ShapeDtypeStruct((T, D_MODEL), F32),
         jax.ShapeDtypeStruct((T // CHUNK, HGRN_HEADS, HGRN_DK, HGRN_DK), F32)],
        [pltpu.VMEM((HGRN_DK, HGRN_DK), F32)])
    return pl.pallas_call(
        body, name="hgrn_fwd", grid=(HGRN_HEADS, nr), in_specs=in_specs, out_specs=out_specs, out_shape=out_shape,
        scratch_shapes=scratch, compiler_params=_params(dimension_semantics=("arbitrary", "arbitrary")),
    )(z, z, z, lb_raw, *extra)


def _hgrn_bwd(z, lb_raw, states, do, rb=2048, carry=(None, None)):
    T = z.shape[0]
    rb = min(rb, T)
    ncb = rb // CHUNK
    nr, ri, in_specs = _hgrn_specs(T, rb, True)
    in_specs += [pl.BlockSpec((ncb, 1, HGRN_DK, HGRN_DK), lambda h, r: (ri(r), h, 0, 0)),
                 pl.BlockSpec((rb, HGRN_DK), lambda h, r: (ri(r), h))]

    def body(*refs):
        hh, rr = pl.program_id(0), pl.program_id(1)
        own, finish = _carried(carry, refs, 6, 4, (hh == 0) & (rr == 0), (hh == HGRN_HEADS - 1) & (rr == nr - 1))
        zq_ref, zf_ref, zi_ref, lb_ref, st_ref, do_ref, dq_ref, df_ref, di_ref, dlb_ref, dstate = own

        @pl.when(rr == 0)
        def _():
            dstate[...] = jnp.zeros_like(dstate)
            dlb_ref[...] = jnp.zeros_like(dlb_ref)

        lb = _lower_bound(lb_ref)
        row = lax.broadcasted_iota(jnp.int32, (CHUNK, HGRN_DK), 0)
        masks = _level_masks()
        rd = row & (DIAG - 1)

        def chunk(ci, dlb):
            c = ncb - 1 - ci
            rows = pl.ds(pl.multiple_of(c * CHUNK, CHUNK), CHUNK)
            zq = zq_ref[rows, :]
            q, sq, sg, forget, k, lf = _hgrn_gates(zq, zf_ref[rows, :], lb)
            v = zi_ref[rows, :]
            dov = do_ref[rows, :]
            b = _scan_rows(lf, row, False)
            st = st_ref[c, 0]
            dst = dstate[...]
            b_last = b[CHUNK - 1:CHUNK, :]
            eb = jnp.exp(b)
            ebb = jnp.exp(b_last - b)
            e_last = jnp.exp(b_last)
            dob, vb, stb, dstb = dov.astype(BF16), v.astype(BF16), st.astype(BF16), dst.astype(BF16)
            dq = eb * _dot(dob, stb)
            dv = _dot_nt((k * ebb).astype(BF16), dstb)
            dk = ebb * _dot(vb, dstb)
            extra = e_last * jnp.sum(dst * st, axis=0, keepdims=True) + jnp.sum(k * dk, axis=0, keepdims=True)
            da = _dot_nt(dob, vb)
            sc = jnp.zeros((CHUNK, CHUNK), F32)
            for e, mask in zip(_level_scales(b), masks):
                qs, ks = (q * e).astype(BF16), (k * e).astype(BF16)
                dam = jnp.where(mask, da, 0.0).astype(BF16)
                dq = dq + e * _dot(dam, ks)
                dk = dk + e * _dot_tn(dam, qs)
                sc = sc + jnp.where(mask, _dot_nt(qs, ks), 0.0)
            dv = dv + _dot_tn(sc.astype(BF16), dob)
            dad = jnp.sum(dov * v, axis=-1, keepdims=True)
            dq = dq + dad * k
            dk = dk + dad * q
            dv = dv + jnp.sum(q * k, axis=-1, keepdims=True) * dov
            for d in range(1, DIAG):
                w = jnp.where(rd >= d, jnp.exp(b - _group_roll(b, d)), 0.0)
                kr = _group_roll(k, d)
                dad = jnp.sum(dov * _group_roll(v, d), axis=-1, keepdims=True)
                ad = jnp.sum(q * kr * w, axis=-1, keepdims=True)
                dq = dq + dad * kr * w
                dk = dk + _group_roll(dad * q * w, -d)
                dv = dv + _group_roll(ad * dov, -d)
            dlf = _scan_rows(q * dq - k * dk, row, True) + extra
            dstate[...] = dst * e_last + _dot_tn(dob, (q * eb).astype(BF16))
            dforget = dlf / forget - dk
            dq_ref[rows, :] = (dq * (sq * (1.0 + zq * (1.0 - sq)))).astype(BF16)
            df_ref[rows, :] = (dforget * (1.0 - lb) * sg * (1.0 - sg)).astype(BF16)
            di_ref[rows, :] = dv.astype(BF16)
            return dlb + jnp.sum(dforget * (1.0 - sg), axis=0, keepdims=True)

        def group(i, dlb):
            for j in range(UNROLL_BWD):
                dlb = chunk(i * UNROLL_BWD + j, dlb)
            return dlb

        dlb_ref[...] += lax.fori_loop(0, ncb // UNROLL_BWD, group, jnp.zeros((1, HGRN_DK), F32))
        finish()

    blk = pl.BlockSpec((rb, HGRN_DK), lambda h, r: (ri(r), h))
    in_specs, out_specs, out_shape, scratch, extra = _carried_specs(
        carry, in_specs, [blk, blk, blk, pl.BlockSpec((1, HGRN_DK), lambda h, r: (0, h))],
        [jax.ShapeDtypeStruct((T, D_MODEL), BF16)] * 3 + [jax.ShapeDtypeStruct((1, D_MODEL), F32)],
        [pltpu.VMEM((HGRN_DK, HGRN_DK), F32)])
    return pl.pallas_call(
        body, name="hgrn_bwd", grid=(HGRN_HEADS, nr), in_specs=in_specs, out_specs=out_specs, out_shape=out_shape,
        scratch_shapes=scratch, compiler_params=_params(dimension_semantics=("arbitrary", "arbitrary")),
    )(z, z, z, lb_raw, states, do, *extra)


MESH = pl.DeviceIdType.MESH
ANY = pl.BlockSpec(memory_space=pl.ANY)


def _place():
    return lax.axis_index("x"), lax.axis_index("y"), lax.axis_index("c")


def _sems(n):
    return [pltpu.SemaphoreType.DMA((7 * n,)), pltpu.SemaphoreType.DMA((7 * n,)), pltpu.SemaphoreType.DMA((n,))]


class _Gather:
    def __init__(self, x_ref, out_ref, send_sems, recv_sems, local_sems, idx):
        self.x_ref, self.out_ref, self.send_sems, self.recv_sems, self.local_sem, self.base = (
            x_ref, out_ref, send_sems, recv_sems, local_sems.at[idx], 7 * idx)
        x, y, c = _place()
        self.c = c
        self.me, self.sibling = (x, y, c), (x, y, 1 - c)
        self.chips = [(1 - x, y), (x, 1 - y), (1 - x, 1 - y)]

    def rows(self, px, py, pc):
        return self.out_ref.at[4 * px + 2 * py + pc]

    def copy(self, k, block, to, from_input=False):
        return pltpu.make_async_remote_copy(
            src_ref=self.x_ref if from_input else self.rows(*block), dst_ref=self.rows(*block),
            send_sem=self.send_sems.at[self.base + k], recv_sem=self.recv_sems.at[self.base + k], device_id=to,
            device_id_type=MESH)

    def first(self):
        out = [self.copy(0, self.me, self.sibling, from_input=True)]
        return out + [self.copy(1 + j, self.me, (*chip, self.c), from_input=True) for j, chip in enumerate(self.chips)]

    def start(self):
        pltpu.make_async_copy(self.x_ref, self.rows(*self.me), self.local_sem).start()
        for cp in self.first():
            cp.start()

    def finish(self):
        passed = [self.copy(4 + j, (*chip, self.c), self.sibling) for j, chip in enumerate(self.chips)]
        for j, chip in enumerate(self.chips):
            self.copy(1 + j, (*chip, self.c), self.me).wait_recv()
            passed[j].start()
        self.copy(0, self.sibling, self.me).wait_recv()
        for j, chip in enumerate(self.chips):
            self.copy(4 + j, (*chip, 1 - self.c), self.me).wait_recv()
        for cp in self.first() + passed:
            cp.wait_send()
        pltpu.make_async_copy(self.x_ref, self.rows(*self.me), self.local_sem).wait()


class _Many:
    def __init__(self, kind, in_refs, out_refs, send_sems, recv_sems, local_sems):
        self.ops = [kind(x, o, send_sems, recv_sems, local_sems, i) for i, (x, o) in enumerate(zip(in_refs, out_refs))]

    def start(self):
        for op in self.ops:
            op.start()

    def finish(self):
        for op in self.ops:
            op.finish()


def _result_shapes(kind, arrs):
    return [jax.ShapeDtypeStruct(a.shape if kind is _Exchange else (N_DEV,) + a.shape, a.dtype) for a in arrs]


def _all_gather(name, shards):
    n = len(shards)

    def body(*refs):
        g = _Many(_Gather, refs[:n], refs[n:2 * n], *refs[2 * n:])
        g.start()
        g.finish()

    return pl.pallas_call(
        body, name=name, out_shape=_result_shapes(_Gather, shards), in_specs=[ANY] * n, out_specs=[ANY] * n,
        scratch_shapes=_sems(n),
    )(*shards)


def _peers(x, y, c):
    out = []
    for k in range(1, N_DEV):
        px = 1 - x if k & 4 else x
        py = 1 - y if k & 2 else y
        pc = 1 - c if k & 1 else c
        out.append((k, (px, py, pc), 4 * px + 2 * py + pc))
    return out


class _Exchange:
    def __init__(self, g_ref, recv_ref, send_sems, recv_sems, local_sems, idx):
        x, y, c = _place()
        me = 4 * x + 2 * y + c
        self.local = pltpu.make_async_copy(g_ref.at[me], recv_ref.at[me], local_sems.at[idx])
        self.copies = [
            pltpu.make_async_remote_copy(
                src_ref=g_ref.at[pidx], dst_ref=recv_ref.at[me], send_sem=send_sems.at[7 * idx + k - 1],
                recv_sem=recv_sems.at[7 * idx + k - 1], device_id=peer, device_id_type=MESH)
            for k, peer, pidx in _peers(x, y, c)]

    def start(self):
        self.local.start()
        for cp in self.copies:
            cp.start()

    def finish(self):
        for cp in self.copies:
            cp.wait()
        self.local.wait()


def _carried(carry, refs, n_in, n_out, first, last):
    kind, arrs = carry
    if kind is None:
        return refs, lambda: None
    n = len(arrs)
    ins, rest = refs[:n_in], refs[n_in + n:]
    outs, scratch = rest[:n_out], rest[n_out + n:]
    op = _Many(kind, refs[n_in:n_in + n], rest[n_out:n_out + n], *scratch[len(scratch) - 3:])

    @pl.when(first)
    def _():
        op.start()

    def finish():
        @pl.when(last)
        def _():
            op.finish()

    return tuple(ins) + tuple(outs) + tuple(scratch[:len(scratch) - 3]), finish


def _carried_specs(carry, in_specs, out_specs, out_shape, scratch):
    kind, arrs = carry
    if kind is None:
        return in_specs, out_specs, out_shape, scratch, []
    n = len(arrs)
    return (list(in_specs) + [ANY] * n, list(out_specs) + [ANY] * n,
            list(out_shape) + _result_shapes(kind, arrs), list(scratch) + _sems(n), list(arrs))


def _adamw(w, g, m, v):
    m = ADAM_B1 * m + (1.0 - ADAM_B1) * g
    v = ADAM_B2 * v + (1.0 - ADAM_B2) * (g * g)
    m_hat = m / (1.0 - ADAM_B1 ** ADAM_STEP)
    v_hat = v / (1.0 - ADAM_B2 ** ADAM_STEP)
    delta = -ADAM_LR * (m_hat / (jnp.sqrt(v_hat) + ADAM_EPS) + ADAM_WD * w)
    return delta, m, v


def _adamw_sum(name, recvs, w, m, v):
    L, R, C = w.shape
    tm = 128 if R % 128 == 0 else 64
    assert R % tm == 0 and len(recvs) == L

    def body(*refs):
        r_refs, (w_ref, m_ref, v_ref, g_ref, d_ref, nm_ref, nv_ref) = refs[:L], refs[L:]
        for l in range(L):
            g = r_refs[l][0].astype(F32)
            for s in range(1, N_DEV):
                g = g + r_refs[l][s].astype(F32)
            g_ref[l] = g
            d_ref[l], nm_ref[l], nv_ref[l] = _adamw(w_ref[l], g, m_ref[l], v_ref[l])

    blk = pl.BlockSpec((L, tm, C), lambda i: (0, i, 0))
    return pl.pallas_call(
        body, name=name, grid=(R // tm,),
        in_specs=[pl.BlockSpec((N_DEV, tm, C), lambda i: (0, i, 0))] * L + [blk, blk, blk],
        out_specs=[blk] * 4, out_shape=[jax.ShapeDtypeStruct((L, R, C), F32)] * 4,
        compiler_params=_params(dimension_semantics=("arbitrary",)),
    )(*recvs, w, m, v)


def _small_sync(part, w, m, v):
    def body(p_ref, w_ref, m_ref, v_ref, g_ref, d_ref, nm_ref, nv_ref, gath, send_sems, recv_sems):
        x, y, c = _place()
        me = 4 * x + 2 * y + c
        gath[me] = p_ref[...]
        copies = []
        for k, peer, _ in _peers(x, y, c):
            cp = pltpu.make_async_remote_copy(
                src_ref=p_ref, dst_ref=gath.at[me], send_sem=send_sems.at[k - 1], recv_sem=recv_sems.at[k - 1],
                device_id=peer, device_id_type=MESH)
            cp.start()
            copies.append(cp)
        for cp in copies:
            cp.wait()
        g = gath[0]
        for s in range(1, N_DEV):
            g = g + gath[s]
        wv = w_ref[...]
        l0, l1 = w_ref[8:9, :], w_ref[9:10, :]
        mx = jnp.maximum(l0, l1)
        e0, e1 = jnp.exp(l0 - mx), jnp.exp(l1 - mx)
        g9 = g[9:10, :] * (e0 / (e0 + e1)) * (e1 / (e0 + e1))
        row = lax.broadcasted_iota(jnp.int32, g.shape, 0)
        g = jnp.where(row == 9, g9, jnp.where(row == 8, -g9, g))
        g_ref[...] = g
        d_ref[...], nm_ref[...], nv_ref[...] = _adamw(wv, g, m_ref[...], v_ref[...])

    vm = pl.BlockSpec(memory_space=pltpu.VMEM)
    return pl.pallas_call(
        body, name="small_params_sync", in_specs=[vm] * 4, out_specs=[vm] * 4,
        out_shape=[jax.ShapeDtypeStruct(part.shape, F32)] * 4,
        scratch_shapes=[pltpu.VMEM((N_DEV,) + part.shape, F32), pltpu.SemaphoreType.DMA((7,)),
                        pltpu.SemaphoreType.DMA((7,))],
    )(part, w, m, v)


def _shards_bf16(d, pieces):
    return [d[name][layer].astype(BF16) for name, layer in pieces]


def _gathered(arrs, pieces, out):
    for a, (name, layer) in zip(arrs, pieces):
        out[name, layer] = a if name in COL_SHARDED else a.reshape(N_DEV * a.shape[1], a.shape[2])


def _pad_row(a, width=D_MODEL):
    a = a.reshape(1, -1)
    return jnp.pad(a, ((0, 0), (0, width - a.shape[1])))


LOSS_ROW = 11


def _pack_small(d, gn_full, loss=None):
    rows = [d["mix_norm"], d["mlp_norm"], d["final_norm"].reshape(1, D_MODEL),
            _pad_row(d["attn_b_qkv"], 2 * D_MODEL).reshape(2, D_MODEL), _pad_row(d["attn_sinks"]),
            d["hgrn_lower_bounds"], gn_full.reshape(1, D_MODEL)]
    if loss is not None:
        rows.append(_pad_row(loss))
    p = jnp.concatenate(rows, axis=0)
    return jnp.pad(p, ((0, SMALL_ROWS - p.shape[0]), (0, 0)))


def _unpack_small(p, me):
    return dict(
        mix_norm=p[0:2], mlp_norm=p[2:4], final_norm=p[4],
        attn_b_qkv=p[5:7].reshape(1, 2 * D_MODEL)[:, :QKV_DIM], attn_sinks=p[7:8, :N_Q_HEADS],
        hgrn_lower_bounds=p[8:10], hgrn_g_norm=lax.dynamic_slice(p[10:11], (0, me * 128), (1, 128)))


WEIGHT_NAMES = ['mix_norm', 'mlp_norm', 'final_norm', 'attn_w_qkv', 'attn_b_qkv', 'attn_sinks', 'attn_w_o', 'hgrn_w_in',
                'hgrn_g_norm', 'hgrn_w_o', 'hgrn_lower_bounds', 'mlp_w_up', 'mlp_w_down']
SMALL_NAMES = ('mix_norm', 'mlp_norm', 'final_norm', 'attn_b_qkv', 'attn_sinks', 'hgrn_lower_bounds', 'hgrn_g_norm')


def _rotary_tables(positions):
    inv_freq = ROPE_THETA ** (-jnp.arange(0, 2 * ROT_HALF, 2, dtype=F32) / (2 * ROT_HALF))
    ang = positions.astype(F32).reshape(-1, 1) * inv_freq
    cos, sin = jnp.cos(ang), jnp.sin(ang)
    r = jnp.arange(LANES) % HEAD_DIM
    idx = r % ROT_HALF
    c = jnp.where(r < 2 * ROT_HALF, cos[:, idx], 1.0)
    sa = jnp.where((r >= ROT_HALF) & (r < 2 * ROT_HALF), sin[:, idx], 0.0)
    sb = jnp.where(r < ROT_HALF, -sin[:, idx], 0.0)
    return jnp.concatenate([c, sa, sb], axis=1)


def kernel(x, positions, mix_norm, mlp_norm, final_norm, attn_w_qkv, attn_b_qkv, attn_sinks, attn_w_o, hgrn_w_in, hgrn_g_norm, hgrn_w_o, hgrn_lower_bounds, mlp_w_up, mlp_w_down, loss_target, m_mix_norm, m_mlp_norm, m_final_norm, m_attn_w_qkv, m_attn_b_qkv, m_attn_sinks, m_attn_w_o, m_hgrn_w_in, m_hgrn_g_norm, m_hgrn_w_o, m_hgrn_lower_bounds, m_mlp_w_up, m_mlp_w_down, v_mix_norm, v_mlp_norm, v_final_norm, v_attn_w_qkv, v_attn_b_qkv, v_attn_sinks, v_attn_w_o, v_hgrn_w_in, v_hgrn_g_norm, v_hgrn_w_o, v_hgrn_lower_bounds, v_mlp_w_up, v_mlp_w_down):
    w = dict(mix_norm=mix_norm, mlp_norm=mlp_norm, final_norm=final_norm, attn_w_qkv=attn_w_qkv, attn_b_qkv=attn_b_qkv,
             attn_sinks=attn_sinks, attn_w_o=attn_w_o, hgrn_w_in=hgrn_w_in, hgrn_g_norm=hgrn_g_norm, hgrn_w_o=hgrn_w_o,
             hgrn_lower_bounds=hgrn_lower_bounds, mlp_w_up=mlp_w_up, mlp_w_down=mlp_w_down)
    m = dict(mix_norm=m_mix_norm, mlp_norm=m_mlp_norm, final_norm=m_final_norm, attn_w_qkv=m_attn_w_qkv,
             attn_b_qkv=m_attn_b_qkv, attn_sinks=m_attn_sinks, attn_w_o=m_attn_w_o, hgrn_w_in=m_hgrn_w_in,
             hgrn_g_norm=m_hgrn_g_norm, hgrn_w_o=m_hgrn_w_o, hgrn_lower_bounds=m_hgrn_lower_bounds, mlp_w_up=m_mlp_w_up,
             mlp_w_down=m_mlp_w_down)
    v = dict(mix_norm=v_mix_norm, mlp_norm=v_mlp_norm, final_norm=v_final_norm, attn_w_qkv=v_attn_w_qkv,
             attn_b_qkv=v_attn_b_qkv, attn_sinks=v_attn_sinks, attn_w_o=v_attn_w_o, hgrn_w_in=v_hgrn_w_in,
             hgrn_g_norm=v_hgrn_g_norm, hgrn_w_o=v_hgrn_w_o, hgrn_lower_bounds=v_hgrn_lower_bounds, mlp_w_up=v_mlp_w_up,
             mlp_w_down=v_mlp_w_down)
    me = 4 * lax.axis_index("x") + 2 * lax.axis_index("y") + lax.axis_index("c")

    gn = hgrn_g_norm.reshape(1, 128)
    gn_a = gn.astype(BF16)
    gn_b = (gn - gn_a.astype(F32)).astype(BF16)
    gn_c = (gn - gn_a.astype(F32) - gn_b.astype(F32)).astype(BF16)
    gn_rows = jnp.pad(jnp.concatenate([gn_a, gn_b, gn_c], axis=1), ((0, 15), (0, D_MODEL - 3 * 128)))
    full = {}
    got = _all_gather("gather_attn_weights", _shards_bf16(w, GATHER_FIRST) + [gn_rows])
    _gathered(got[:1], GATHER_FIRST, full)
    w_qkv = full["attn_w_qkv", 0].transpose(1, 0, 2).reshape(D_MODEL, QKV_DIM)
    gn_terms = got[1][:, 0, :3 * 128].astype(F32).reshape(N_DEV, 3, 128)
    gn_full = ((gn_terms[:, 0] + gn_terms[:, 1]) + gn_terms[:, 2]).reshape(1, D_MODEL)

    x0 = x[0]
    tgt = loss_target[0]
    rot = _rotary_tables(positions)
    row = lambda a: a.reshape(1, -1)

    qkv, h0 = _norm_mm("qkv_proj", x0, row(mix_norm[0]), w_qkv, attn_b_qkv, rot=rot)
    att, *got = _attn_fwd(qkv, attn_sinks, carry=(_Gather, _shards_bf16(w, GATHER_ATTN)))
    _gathered(got, GATHER_ATTN, full)
    x1 = _mm_res("attn_out_proj", att, full["attn_w_o", 0], x0)
    u0, h1, *got = _norm_mm("mlp0_up", x1, row(mlp_norm[0]), full["mlp_w_up", 0],
                            carry=(_Gather, _shards_bf16(w, GATHER_MLP0)))
    _gathered(got, GATHER_MLP0, full)
    x2, a0 = _mlp_down("mlp0_down", u0, full["mlp_w_down", 0], x1)
    z, h2 = _norm_mm("hgrn_in_proj", x2, row(mix_norm[1]), full["hgrn_w_in", 0])
    o_raw, states, *got = _hgrn_fwd(z, hgrn_lower_bounds, carry=(_Gather, _shards_bf16(w, GATHER_HGRN)))
    _gathered(got, GATHER_HGRN, full)
    x3, o2 = _hgrn_out("hgrn_out_proj", o_raw, z, gn_full, full["hgrn_w_o", 0], x2)
    u1, h3 = _norm_mm("mlp1_up", x3, row(mlp_norm[1]), full["mlp_w_up", 1])
    dx4, a1, loss_part, g_final = _mlp_down("mlp1_down_loss", u1, full["mlp_w_down", 1], x3,
                                            loss_head=(tgt, row(final_norm)))

    gw = {}
    du1, = _mlp_bwd_act("mlp1_bwd_act", dx4, u1, full["mlp_w_down", 1])
    dx3, g_mlp1 = _mm_nt_rmsbwd("mlp1_bwd_in", du1, full["mlp_w_up", 1], x3, row(mlp_norm[1]), dx4)
    gw["mlp_w_down", 1] = _mm_tn("mlp1_dw_down", a1, dx4, "rows")
    gw["mlp_w_up", 1] = _mm_tn("mlp1_dw_up", h3, du1, "cols")

    do_raw, dg, g_gn = _hgrn_out_bwd("hgrn_out_bwd", dx3, o_raw, z, full["hgrn_w_o", 0], gn_full)
    gw["hgrn_w_o", 0] = _mm_tn("hgrn_dw_o", o2, dx3, "rows")
    recvs = {}
    dzq, dzf, dzi, g_lb, *recv = _hgrn_bwd(z, hgrn_lower_bounds, states, do_raw,
                                           carry=(_Exchange, [gw[p] for p in GRADS_HGRN]))
    recvs.update(zip(GRADS_HGRN, recv))
    dz = [dzq, dzf, dzi, dg]
    dx2, g_mix1 = _mm_nt_rmsbwd("hgrn_in_bwd", dz, full["hgrn_w_in", 0], x2, row(mix_norm[1]), dx3)
    gw["hgrn_w_in", 0] = jnp.concatenate(
        [_mm_tn(f"hgrn_dw_in{j}", h2, d, "cols") for j, d in enumerate(dz)], axis=0)

    du0, *recv = _mlp_bwd_act("mlp0_bwd_act", dx2, u0, full["mlp_w_down", 0],
                              carry=(_Exchange, [gw[p] for p in GRADS_MLP0]))
    recvs.update(zip(GRADS_MLP0, recv))
    dx1, g_mlp0 = _mm_nt_rmsbwd("mlp0_bwd_in", du0, full["mlp_w_up", 0], x1, row(mlp_norm[0]), dx2)
    gw["mlp_w_down", 0] = _mm_tn("mlp0_dw_down", a0, dx2, "rows")
    gw["mlp_w_up", 0] = _mm_tn("mlp0_dw_up", h1, du0, "cols")

    datt = _mm_nt("attn_out_bwd", dx1, full["attn_w_o", 0], BF16)
    gw["attn_w_o", 0] = _mm_tn("attn_dw_o", att, dx1, "rows")
    dqkv, g_sink, *recv = _attn_bwd(qkv, rot, attn_sinks, datt, carry=(_Exchange, [gw[p] for p in GRADS_ATTN]))
    recvs.update(zip(GRADS_ATTN, recv))
    g_qkv = _mm_tn("attn_dw_qkv", h0, dqkv, bn=512)
    g_qkv = g_qkv.reshape(D_MODEL, N_DEV, QKV_DIM // N_DEV).transpose(1, 0, 2).astype(BF16)
    dx0, g_mix0, g_bqkv, recvs["attn_w_qkv", 0] = _mm_nt_rmsbwd(
        "qkv_bwd", dqkv, w_qkv, x0, row(mix_norm[0]), dx1, with_colsum=True, carry=(_Exchange, [g_qkv]))

    big = {name: _adamw_sum("adamw_" + name, [recvs[name, l] for l in range(w[name].shape[0])], w[name], m[name], v[name])
           for name in BIG_NAMES}

    zero_row = jnp.zeros((1, D_MODEL), F32)
    part = _pack_small(dict(
        mix_norm=jnp.concatenate([g_mix0, g_mix1], axis=0), mlp_norm=jnp.concatenate([g_mlp0, g_mlp1], axis=0),
        final_norm=g_final, attn_b_qkv=g_bqkv, attn_sinks=g_sink[:, :N_Q_HEADS],
        hgrn_lower_bounds=jnp.concatenate([zero_row, g_lb], axis=0)), g_gn, loss=loss_part)

    def spread(a):
        return lax.dynamic_update_slice(zero_row, a.reshape(1, 128), (0, me * 128))

    small_in = [_pack_small({n: d[n] for n in SMALL_NAMES if n != "hgrn_g_norm"}, spread(d["hgrn_g_norm"]))
                for d in (w, m, v)]
    synced = _small_sync(part, *small_in)
    small = [_unpack_small(p, me) for p in synced]

    outs = [synced[0][LOSS_ROW, 0], dx0.reshape(x.shape)]
    for kind, grp_small in enumerate(small):
        for name in WEIGHT_NAMES:
            val = grp_small[name] if name in SMALL_NAMES else big[name][kind]
            outs.append(val.reshape(w[name].shape))
    return tuple(outs)
```

```python
import functools

import jax
import jax.numpy as jnp
from jax import lax
from jax.experimental import pallas as pl
from jax.experimental.pallas import tpu as pltpu

F32 = jnp.float32
BF16 = jnp.bfloat16

D_MODEL = 1024
HEAD_DIM = 64
N_Q_HEADS = 16
Q_DIM = 1024
KV_DIM = 256
QKV_DIM = 1536
ATT_BLOCK = 128
ROT_HALF = 8
ROPE_THETA = 500000.0
NEG_INF = -1e30
HGRN_HEADS = 8
HGRN_DK = 128
CHUNK = 64
D_FF = 4096
NORM_EPS = 1e-5
N_DEV = 8

ADAM_LR = 0.001
ADAM_B1 = 0.9
ADAM_B2 = 0.999
ADAM_EPS = 1e-08
ADAM_WD = 0.01
ADAM_STEP = 10

LANES = 128
VMEM_LIMIT = 56 * 1024 * 1024

GATHER_FIRST = (("attn_w_qkv", 0),)
GATHER_ATTN = (("attn_w_o", 0), ("mlp_w_up", 0), ("mlp_w_down", 0))
GATHER_MLP0 = (("hgrn_w_in", 0), ("hgrn_w_o", 0))
GATHER_HGRN = (("mlp_w_up", 1), ("mlp_w_down", 1))
GRADS_HGRN = (("mlp_w_down", 1), ("mlp_w_up", 1), ("hgrn_w_o", 0))
GRADS_MLP0 = (("hgrn_w_in", 0),)
GRADS_ATTN = (("mlp_w_down", 0), ("mlp_w_up", 0), ("attn_w_o", 0))
COL_SHARDED = ("attn_w_qkv", "hgrn_w_in", "mlp_w_up")
BIG_NAMES = ("attn_w_qkv", "attn_w_o", "hgrn_w_in", "hgrn_w_o", "mlp_w_up", "mlp_w_down")
SMALL_ROWS = 16


def _dot(a, b):
    return jnp.dot(a, b, preferred_element_type=F32)


def _dot_nt(a, b):
    return lax.dot_general(a, b, (((1,), (1,)), ((), ())), preferred_element_type=F32)


def _dot_tn(a, b):
    return lax.dot_general(a, b, (((0,), (0,)), ((), ())), preferred_element_type=F32)


def _params(**kw):
    return pltpu.CompilerParams(vmem_limit_bytes=VMEM_LIMIT, **kw)


def _full_spec(a):
    nd = a.ndim
    return pl.BlockSpec(a.shape, lambda *_: (0,) * nd)


def _row_call(name, body, n_rows, tm, row_ins, full_ins, row_outs, acc_outs=(), carry=(None, None)):
    steps = n_rows // tm
    in_specs = [pl.BlockSpec((tm, w), functools.partial(lambda i, cb: (i, cb), cb=cb)) for _, w, cb in row_ins]
    in_specs += [_full_spec(a) for a in full_ins]
    out_shape = [jax.ShapeDtypeStruct((n_rows, w), dt) for w, dt in row_outs]
    out_specs = [pl.BlockSpec((tm, w), lambda i: (i, 0)) for w, _ in row_outs]
    for shp, dt in acc_outs:
        out_shape.append(jax.ShapeDtypeStruct(shp, dt))
        out_specs.append(pl.BlockSpec(shp, functools.partial(lambda i, nd: (0,) * nd, nd=len(shp))))
    n_in, n_out = len(in_specs), len(out_specs)
    in_specs, out_specs, out_shape, scratch, extra = _carried_specs(carry, in_specs, out_specs, out_shape, [])

    def wrapped(*refs):
        i = pl.program_id(0)
        own, finish = _carried(carry, refs, n_in, n_out, i == 0, i == steps - 1)
        body(*own)
        finish()

    return pl.pallas_call(
        wrapped, name=name, grid=(steps,), in_specs=in_specs, out_specs=out_specs, out_shape=out_shape,
        scratch_shapes=scratch, compiler_params=_params(dimension_semantics=("arbitrary",)),
    )(*[a for a, _, _ in row_ins], *full_ins, *extra)


def _rms(x, gain):
    r = lax.rsqrt(jnp.mean(x * x, axis=-1, keepdims=True) + NORM_EPS)
    xhat = x * r
    return xhat * gain, xhat, r


def _rms_bwd(dy, xhat, r, gain):
    dxhat = dy * gain
    dx = r * (dxhat - xhat * jnp.mean(dxhat * xhat, axis=-1, keepdims=True))
    return dx, dy * xhat


def _norm_mm(name, x, gain, w, bias=None, rot=None, tm=512, carry=(None, None)):
    T = x.shape[0]
    tm = min(tm, T)
    nc = 512
    blocked = w.ndim == 3
    n = N_DEV * w.shape[2] if blocked else w.shape[1]
    assert n % nc == 0 and (not blocked or w.shape[2] == nc)

    def body(*refs):
        x_ref, refs = refs[0], refs[1:]
        if rot is not None:
            t_ref, refs = refs[0], refs[1:]
        g_ref, w_ref, refs = refs[0], refs[1], refs[2:]
        if bias is not None:
            b_ref, refs = refs[0], refs[1:]
        y_ref, h_ref = refs
        h, _, _ = _rms(x_ref[...], g_ref[...])
        hb = h.astype(BF16)
        h_ref[...] = hb
        for c in range(n // nc):
            sl = slice(c * nc, (c + 1) * nc)
            y = _dot(hb, w_ref[c] if blocked else w_ref[:, sl])
            if bias is not None:
                y = y + b_ref[:, sl]
            if rot is None:
                y_ref[:, sl] = y
            else:
                n_rot = max(0, min(nc, Q_DIM + KV_DIM - c * nc)) // LANES
                pieces = _rot_fwd(y[:, :n_rot * LANES], t_ref[...]) if n_rot else []
                for j in range(nc // LANES):
                    col = slice(c * nc + j * LANES, c * nc + (j + 1) * LANES)
                    y_ref[:, col] = pieces[j] if j < n_rot else y[:, j * LANES:(j + 1) * LANES]

    rows = [(x, D_MODEL, 0)] + ([(rot, 3 * LANES, 0)] if rot is not None else [])
    full = [gain, w] + ([bias] if bias is not None else [])
    return _row_call(name, body, T, tm, rows, full, [(n, F32), (D_MODEL, BF16)], carry=carry)


def _mm_res(name, a, w, res, tm=512):
    T = a.shape[0]
    tm = min(tm, T)

    def body(a_ref, r_ref, w_ref, o_ref):
        o_ref[...] = r_ref[...] + _dot(a_ref[...], w_ref[...])

    return _row_call(name, body, T, tm, [(a, a.shape[1], 0), (res, D_MODEL, 0)], [w], [(D_MODEL, F32)])[0]


def _mlp_down(name, u, w, res, tm=512, loss_head=None):
    T = u.shape[0]
    tm = min(tm, T)
    kc = 1024
    sub = min(256, tm)

    def body(*refs):
        if loss_head is None:
            u_ref, r_ref, w_ref, o_ref, a_ref = refs
        else:
            u_ref, r_ref, t_ref, w_ref, g_ref, o_ref, a_ref, loss_ref, dg_ref = refs

            @pl.when(pl.program_id(0) == 0)
            def _():
                loss_ref[...] = jnp.zeros_like(loss_ref)
                dg_ref[...] = jnp.zeros_like(dg_ref)

        for r0 in range(0, tm, sub):
            rs = slice(r0, r0 + sub)
            acc = r_ref[rs, :]
            for c in range(D_FF // kc):
                sl = slice(c * kc, (c + 1) * kc)
                a = jnp.maximum(u_ref[rs, sl], 0.0)
                ab = (a * a).astype(BF16)
                a_ref[rs, sl] = ab
                acc = acc + _dot(ab, w_ref[sl, :])
            if loss_head is None:
                o_ref[rs, :] = acc
            else:
                gain_v = g_ref[...]
                y, xhat, r = _rms(acc, gain_v)
                diff = y - t_ref[rs, :]
                per_row = jnp.sum(diff * diff, axis=-1, keepdims=True) * (1.0 / D_MODEL)
                loss_ref[...] += jnp.broadcast_to(0.5 * jnp.sum(per_row, axis=0, keepdims=True), loss_ref.shape)
                dx, dgr = _rms_bwd(diff * (1.0 / D_MODEL), xhat, r, gain_v)
                o_ref[rs, :] = dx
                dg_ref[...] += jnp.sum(dgr, axis=0, keepdims=True)

    rows, full, acc_outs = [(u, D_FF, 0), (res, D_MODEL, 0)], [w], []
    if loss_head is not None:
        rows, full = rows + [(loss_head[0], D_MODEL, 0)], full + [loss_head[1]]
        acc_outs = [((1, LANES), F32), ((1, D_MODEL), F32)]
    return _row_call(name, body, T, tm, rows, full, [(D_MODEL, F32), (D_FF, BF16)], acc_outs)


def _hgrn_out(name, o_raw, z, gn, w, res, tm=512):
    T = o_raw.shape[0]
    tm = min(tm, T)

    def body(o_ref, g_ref, r_ref, gn_ref, w_ref, x_ref, a_ref):
        y, _, _ = _rms(o_ref[...], gn_ref[...])
        g = g_ref[...]
        a = (y * (g * jax.nn.sigmoid(g))).astype(BF16)
        a_ref[...] = a
        x_ref[...] = r_ref[...] + _dot(a, w_ref[...])

    return _row_call(name, body, T, tm, [(o_raw, D_MODEL, 0), (z, D_MODEL, 3), (res, D_MODEL, 0)], [gn, w],
                     [(D_MODEL, F32), (D_MODEL, BF16)])


def _mm_nt_rmsbwd(name, dy, w, x, gain, dres, tm=512, with_colsum=False, carry=(None, None)):
    T = x.shape[0]
    tm = min(tm, T)
    dys = list(dy) if isinstance(dy, (list, tuple)) else [dy]
    width = dys[0].shape[1]
    n = width * len(dys)
    sub = min(256, tm)
    assert not with_colsum or len(dys) == 1

    def body(*refs):
        dy_refs, refs = refs[:len(dys)], refs[len(dys):]
        if with_colsum:
            x_ref, dr_ref, w_ref, g_ref, dx_ref, dg_ref, cs_ref = refs
        else:
            x_ref, dr_ref, w_ref, g_ref, dx_ref, dg_ref = refs

        @pl.when(pl.program_id(0) == 0)
        def _():
            dg_ref[...] = jnp.zeros_like(dg_ref)
            if with_colsum:
                cs_ref[...] = jnp.zeros_like(cs_ref)

        gain_v = g_ref[...]
        for r0 in range(0, tm, sub):
            rs = slice(r0, r0 + sub)
            if w.ndim == 3:
                nb = w.shape[2]
                dh = None
                for p in range(N_DEV):
                    piece, off = divmod(p * nb, width)
                    part = _dot_nt(dy_refs[piece][rs, off:off + nb].astype(BF16), w_ref[p])
                    dh = part if dh is None else dh + part
            else:
                dh = _dot_nt(dy_refs[0][rs, :].astype(BF16), w_ref[...])
            _, xhat, r = _rms(x_ref[rs, :], gain_v)
            dx, dgr = _rms_bwd(dh, xhat, r, gain_v)
            dx_ref[rs, :] = dr_ref[rs, :] + dx
            dg_ref[...] += jnp.sum(dgr, axis=0, keepdims=True)
            if with_colsum:
                cs_ref[...] += jnp.sum(dy_refs[0][rs, :].astype(F32), axis=0, keepdims=True)

    acc = [((1, D_MODEL), F32)] + ([((1, n), F32)] if with_colsum else [])
    rows = [(d, width, 0) for d in dys] + [(x, D_MODEL, 0), (dres, D_MODEL, 0)]
    return _row_call(name, body, T, tm, rows, [w, gain], [(D_MODEL, F32)], acc, carry=carry)


def _mm_nt(name, dy, w, out_dtype, tm=512):
    T = dy.shape[0]
    tm = min(tm, T)
    k = w.shape[0]

    def body(dy_ref, w_ref, o_ref):
        o_ref[...] = _dot_nt(dy_ref[...].astype(BF16), w_ref[...]).astype(out_dtype)

    return _row_call(name, body, T, tm, [(dy, dy.shape[1], 0)], [w], [(k, out_dtype)])[0]


def _mlp_bwd_act(name, dy, u, w_down, tm=512, carry=(None, None)):
    T = u.shape[0]
    tm = min(tm, T)
    kc = 1024

    def body(dy_ref, u_ref, w_ref, du_ref):
        dyb = dy_ref[...].astype(BF16)
        for c in range(D_FF // kc):
            sl = slice(c * kc, (c + 1) * kc)
            da = _dot_nt(dyb, w_ref[sl, :])
            du_ref[:, sl] = (da * (2.0 * jnp.maximum(u_ref[:, sl], 0.0))).astype(BF16)

    return _row_call(name, body, T, tm, [(dy, D_MODEL, 0), (u, D_FF, 0)], [w_down], [(D_FF, BF16)], carry=carry)


def _hgrn_out_bwd(name, dx, o_raw, z, w, gn, tm=512):
    T = dx.shape[0]
    tm = min(tm, T)

    def body(dx_ref, o_ref, g_ref, w_ref, gn_ref, do_ref, dg_ref, dgn_ref):
        @pl.when(pl.program_id(0) == 0)
        def _():
            dgn_ref[...] = jnp.zeros_like(dgn_ref)

        da = _dot_nt(dx_ref[...].astype(BF16), w_ref[...])
        gn_v = gn_ref[...]
        y, xhat, r = _rms(o_ref[...], gn_v)
        g = g_ref[...]
        sg = jax.nn.sigmoid(g)
        dg_ref[...] = (da * y * (sg * (1.0 + g * (1.0 - sg)))).astype(BF16)
        dyn = da * (g * sg)
        do, dgr = _rms_bwd(dyn, xhat, r, gn_v)
        do_ref[...] = do
        dgn_ref[...] += jnp.sum(dgr, axis=0, keepdims=True)

    return _row_call(name, body, T, tm, [(dx, D_MODEL, 0), (o_raw, D_MODEL, 0), (z, D_MODEL, 3)], [w, gn],
                     [(D_MODEL, F32), (D_MODEL, BF16)], [((1, D_MODEL), F32)])


COL_BLOCK = D_FF // N_DEV


def _mm_tn(name, a, b, shard=None, bm=1024, bn=1024, tk=2048):
    T, M = a.shape
    N = b.shape[1]
    bm, bn, tk = min(bm, M), min(bn, N), min(tk, T)
    nk = T // tk
    if shard is None:
        out_shape, out_block = jax.ShapeDtypeStruct((M, N), F32), (bm, bn)
        out_map = lambda i, j, k: (i, j)
    elif shard == "cols":
        assert bn % COL_BLOCK == 0 and N % bn == 0
        out_shape = jax.ShapeDtypeStruct((N // COL_BLOCK, M, COL_BLOCK), BF16)
        out_block = (bn // COL_BLOCK, bm, COL_BLOCK)
        out_map = lambda i, j, k: (j, i, 0)
    else:
        rows = M // N_DEV
        assert bm % rows == 0
        out_shape, out_block = jax.ShapeDtypeStruct((N_DEV, rows, N), BF16), (bm // rows, rows, bn)
        out_map = lambda i, j, k: (i, 0, j)

    def body(a_ref, b_ref, o_ref, acc):
        k = pl.program_id(2)

        @pl.when(k == 0)
        def _():
            acc[...] = jnp.zeros_like(acc)

        acc[...] += _dot_tn(a_ref[...].astype(BF16), b_ref[...].astype(BF16))

        @pl.when(k == nk - 1)
        def _():
            if shard == "cols":
                for c in range(bn // COL_BLOCK):
                    o_ref[c] = acc[:, c * COL_BLOCK:(c + 1) * COL_BLOCK].astype(BF16)
            else:
                o_ref[...] = acc[...].reshape(out_block).astype(o_ref.dtype)

    return pl.pallas_call(
        body, name=name, grid=(M // bm, N // bn, nk),
        in_specs=[pl.BlockSpec((tk, bm), lambda i, j, k: (k, i)), pl.BlockSpec((tk, bn), lambda i, j, k: (k, j))],
        out_specs=pl.BlockSpec(out_block, out_map), out_shape=out_shape,
        scratch_shapes=[pltpu.VMEM((bm, bn), F32)],
        compiler_params=_params(dimension_semantics=("parallel", "parallel", "arbitrary")),
    )(a, b)


def _rot_fwd(x, tab):
    c, sa, sb = tab[:, :LANES], tab[:, LANES:2 * LANES], tab[:, 2 * LANES:]
    outs = []
    for j in range(x.shape[1] // LANES):
        xs = x[:, j * LANES:(j + 1) * LANES]
        outs.append(xs * c + pltpu.roll(xs, ROT_HALF, 1) * sa + pltpu.roll(xs, LANES - ROT_HALF, 1) * sb)
    return outs


def _rot_bwd(dys, tab):
    c, sa, sb = tab[:, :LANES], tab[:, LANES:2 * LANES], tab[:, 2 * LANES:]
    return [dy * c + pltpu.roll(dy * sa, LANES - ROT_HALF, 1) + pltpu.roll(dy * sb, ROT_HALF, 1) for dy in dys]


ATT_SCALE = HEAD_DIM ** -0.5


def _attn_masks(n):
    kj = lax.broadcasted_iota(jnp.int32, (2 * ATT_BLOCK, ATT_BLOCK), 0)
    qi = lax.broadcasted_iota(jnp.int32, (2 * ATT_BLOCK, ATT_BLOCK), 1)
    delta = qi + ATT_BLOCK - kj
    first_key = jnp.where(n > 0, 0, ATT_BLOCK)
    valid = (delta >= 0) & (delta < ATT_BLOCK) & (kj >= first_key)
    low = lax.broadcasted_iota(jnp.int32, (1, LANES), 1) < HEAD_DIM
    upper = lax.broadcasted_iota(jnp.int32, (LANES, 1), 0) < HEAD_DIM
    return valid, low, upper


def _softmax_sink(s, valid, sink):
    s = jnp.where(valid, s, NEG_INF)
    m = jnp.maximum(jnp.max(s, axis=0, keepdims=True), sink)
    e = jnp.exp(s - m)
    es = jnp.exp(sink - m)
    inv = 1.0 / (jnp.sum(e, axis=0, keepdims=True) + es)
    return e * inv, es * inv


def _attn_specs(nb, tables):
    prev = lambda n: jnp.maximum(jnp.minimum(n, nb - 1) - 1, 0)
    cur = lambda n: jnp.minimum(n, nb - 1)
    specs = [
        pl.BlockSpec((ATT_BLOCK, Q_DIM), lambda n: (cur(n), 0)),
        pl.BlockSpec((ATT_BLOCK, KV_DIM), lambda n: (prev(n), 4)),
        pl.BlockSpec((ATT_BLOCK, KV_DIM), lambda n: (cur(n), 4)),
        pl.BlockSpec((ATT_BLOCK, KV_DIM), lambda n: (prev(n), 5)),
        pl.BlockSpec((ATT_BLOCK, KV_DIM), lambda n: (cur(n), 5)),
    ]
    if tables:
        specs += [pl.BlockSpec((ATT_BLOCK, 3 * LANES), lambda n: (prev(n), 0)),
                  pl.BlockSpec((ATT_BLOCK, 3 * LANES), lambda n: (cur(n), 0))]
    return specs + [pl.BlockSpec(memory_space=pltpu.SMEM)]


def _kv_band(prev_ref, cur_ref):
    out = []
    for j in range(KV_DIM // LANES):
        sl = slice(j * LANES, (j + 1) * LANES)
        band = jnp.concatenate([prev_ref[:, sl], cur_ref[:, sl]], axis=0)
        out.append((band, pltpu.roll(band, HEAD_DIM, 1)))
    return out


def _bf16(bands, transposed=False):
    return [[(a.T if transposed else a).astype(BF16) for a in pair] for pair in bands]


def _attn_fwd(qkv, sinks, carry=(None, None)):
    T = qkv.shape[0]
    nb = T // ATT_BLOCK

    def body(*refs):
        n = pl.program_id(0)
        own, finish = _carried(carry, refs, 6, 1, n == 0, n == nb - 1)
        q_ref, kp_ref, kc_ref, vp_ref, vc_ref, sink_ref, o_ref = own
        valid, low, upper = _attn_masks(n)
        ks = _bf16(_kv_band(kp_ref, kc_ref))
        vts = _bf16(_kv_band(vp_ref, vc_ref), transposed=True)
        heads = []
        for p in range(Q_DIM // LANES):
            kpair, khalf = p // 4, (p // 2) % 2
            q_pair = q_ref[:, p * LANES:(p + 1) * LANES] * ATT_SCALE
            for hf in range(2):
                qm = jnp.where(low if hf == 0 else ~low, q_pair, 0.0).astype(BF16)
                sw = 0 if khalf == hf else 1
                heads.append((2 * p + hf, kpair, sw, _dot_nt(ks[kpair][sw], qm)))
        probs = [_softmax_sink(s, valid, sink_ref[0, h])[0].astype(BF16) for h, _, _, s in heads]
        outs = [_dot(vts[kpair][sw], pr) for (_, kpair, sw, _), pr in zip(heads, probs)]
        for p in range(Q_DIM // LANES):
            o_ref[:, p * LANES:(p + 1) * LANES] = jnp.where(upper, outs[2 * p], outs[2 * p + 1]).T.astype(BF16)
        finish()

    in_specs, out_specs, out_shape, scratch, extra = _carried_specs(
        carry, _attn_specs(nb, False), [pl.BlockSpec((ATT_BLOCK, Q_DIM), lambda n: (n, 0))],
        [jax.ShapeDtypeStruct((T, Q_DIM), BF16)], [])
    return pl.pallas_call(
        body, name="attn_fwd", grid=(nb,), in_specs=in_specs, out_specs=out_specs, out_shape=out_shape,
        scratch_shapes=scratch, compiler_params=_params(dimension_semantics=("arbitrary",)),
    )(qkv, qkv, qkv, qkv, qkv, sinks, *extra)


def _attn_bwd(qkv, rot, sinks, dout, carry=(None, None)):
    T = qkv.shape[0]
    nb = T // ATT_BLOCK
    npair = KV_DIM // LANES

    def body(*refs):
        n = pl.program_id(0)
        own, finish = _carried(carry, refs, 9, 2, n == 0, n == nb)
        (q_ref, kp_ref, kc_ref, vp_ref, vc_ref, tp_ref, tc_ref, sink_ref, do_ref, dqkv_ref, dsink_ref,
         dq_c, dk_c, dv_c) = own

        @pl.when(n == 0)
        def _():
            dq_c[...] = jnp.zeros_like(dq_c)
            dk_c[...] = jnp.zeros_like(dk_c)
            dv_c[...] = jnp.zeros_like(dv_c)
            dsink_ref[...] = jnp.zeros_like(dsink_ref)

        def flush(dk_prev, dv_prev, tab_ref):
            dqkv_ref[:, :Q_DIM] = dq_c[...].astype(BF16)
            dk = _rot_bwd([dk_c[:, j * LANES:(j + 1) * LANES] + dk_prev[j] for j in range(npair)], tab_ref[...])
            for j in range(npair):
                dqkv_ref[:, Q_DIM + j * LANES:Q_DIM + (j + 1) * LANES] = dk[j].astype(BF16)
                dqkv_ref[:, Q_DIM + KV_DIM + j * LANES:Q_DIM + KV_DIM + (j + 1) * LANES] = (
                    dv_c[:, j * LANES:(j + 1) * LANES] + dv_prev[j]).astype(BF16)

        @pl.when(n < nb)
        def _():
            valid, low, upper = _attn_masks(n)
            lane = lax.broadcasted_iota(jnp.int32, (1, LANES), 1)
            k_band = _kv_band(kp_ref, kc_ref)
            ks, kts = _bf16(k_band), _bf16(k_band, transposed=True)
            vs = _bf16(_kv_band(vp_ref, vc_ref))
            dk_acc = [[jnp.zeros((2 * ATT_BLOCK, LANES), F32) for _ in range(2)] for _ in range(npair)]
            dv_acc = [[jnp.zeros((2 * ATT_BLOCK, LANES), F32) for _ in range(2)] for _ in range(npair)]
            dsink = jnp.zeros((1, LANES), F32)
            heads = []
            for p in range(Q_DIM // LANES):
                kpair, khalf = p // 4, (p // 2) % 2
                q_pair = q_ref[:, p * LANES:(p + 1) * LANES] * ATT_SCALE
                do_pair = do_ref[:, p * LANES:(p + 1) * LANES]
                for hf in range(2):
                    sel = low if hf == 0 else ~low
                    qm = jnp.where(sel, q_pair, 0.0).astype(BF16)
                    dom = jnp.where(sel, do_pair, 0.0).astype(BF16)
                    sw = 0 if khalf == hf else 1
                    heads.append((2 * p + hf, kpair, sw, qm, dom,
                                  _dot_nt(ks[kpair][sw], qm), _dot_nt(vs[kpair][sw], dom)))
            grads = []
            for h, kpair, sw, qm, dom, s, dp in heads:
                pr, ps = _softmax_sink(s, valid, sink_ref[0, h])
                dd = jnp.sum(pr * dp, axis=0, keepdims=True)
                dsink = dsink + jnp.where(lane == h, -jnp.sum(ps * dd, axis=1, keepdims=True), 0.0)
                grads.append((pr * (dp - dd)).astype(BF16))
                heads[h] = (kpair, sw, qm, dom, pr.astype(BF16))
            dq_t = []
            for (kpair, sw, qm, dom, pr), ds in zip(heads, grads):
                dq_t.append(_dot(kts[kpair][sw], ds))
                dk_acc[kpair][sw] = dk_acc[kpair][sw] + _dot(ds, qm)
                dv_acc[kpair][sw] = dv_acc[kpair][sw] + _dot(pr, dom)
            dqs = [jnp.where(upper, dq_t[2 * p], dq_t[2 * p + 1]).T * ATT_SCALE for p in range(Q_DIM // LANES)]
            dk_acc = [a[0] + pltpu.roll(a[1], HEAD_DIM, 1) for a in dk_acc]
            dv_acc = [a[0] + pltpu.roll(a[1], HEAD_DIM, 1) for a in dv_acc]
            flush([a[:ATT_BLOCK] for a in dk_acc], [a[:ATT_BLOCK] for a in dv_acc], tp_ref)
            dq = _rot_bwd(dqs, tc_ref[...])
            for p in range(Q_DIM // LANES):
                dq_c[:, p * LANES:(p + 1) * LANES] = dq[p]
            for j in range(npair):
                dk_c[:, j * LANES:(j + 1) * LANES] = dk_acc[j][ATT_BLOCK:]
                dv_c[:, j * LANES:(j + 1) * LANES] = dv_acc[j][ATT_BLOCK:]
            dsink_ref[...] += dsink

        @pl.when(n == nb)
        def _():
            zero = [jnp.zeros((ATT_BLOCK, LANES), F32) for _ in range(npair)]
            flush(zero, zero, tc_ref)

        finish()

    do_spec = pl.BlockSpec((ATT_BLOCK, Q_DIM), lambda n: (jnp.minimum(n, nb - 1), 0))
    in_specs, out_specs, out_shape, scratch, extra = _carried_specs(
        carry, _attn_specs(nb, True) + [do_spec],
        [pl.BlockSpec((ATT_BLOCK, QKV_DIM), lambda n: (jnp.maximum(n - 1, 0), 0)),
         pl.BlockSpec((1, LANES), lambda n: (0, 0))],
        [jax.ShapeDtypeStruct((T, QKV_DIM), BF16), jax.ShapeDtypeStruct((1, LANES), F32)],
        [pltpu.VMEM((ATT_BLOCK, Q_DIM), F32), pltpu.VMEM((ATT_BLOCK, KV_DIM), F32),
         pltpu.VMEM((ATT_BLOCK, KV_DIM), F32)])
    return pl.pallas_call(
        body, name="attn_bwd", grid=(nb + 1,), in_specs=in_specs, out_specs=out_specs, out_shape=out_shape,
        scratch_shapes=scratch, compiler_params=_params(dimension_semantics=("arbitrary",)),
    )(qkv, qkv, qkv, qkv, qkv, rot, rot, sinks, dout, *extra)


LEVELS = (32, 16, 8, 4, 2, 1)
SUBLANES = 8
UNROLL = 8
UNROLL_BWD = 4


def _lower_bound(lb_ref):
    l0, l1 = lb_ref[0:1, :], lb_ref[1:2, :]
    mx = jnp.maximum(l0, l1)
    e0, e1 = jnp.exp(l0 - mx), jnp.exp(l1 - mx)
    return e1 / (e0 + e1)


GROUPS = CHUNK // SUBLANES


def _group_roll(x, k):
    return pltpu.roll(x.reshape(GROUPS, SUBLANES, HGRN_DK), k % SUBLANES, 1).reshape(CHUNK, HGRN_DK)


def _scan_rows(x, row, reverse):
    r8 = row & (SUBLANES - 1)
    for sh in (1, 2, 4):
        ok = (r8 < SUBLANES - sh) if reverse else (r8 >= sh)
        x = x + jnp.where(ok, _group_roll(x, -sh if reverse else sh), 0.0)
    g = x.reshape(GROUPS, SUBLANES, HGRN_DK)
    edge = 0 if reverse else SUBLANES - 1
    tot = jnp.broadcast_to(g[:, edge:edge + 1, :], g.shape)

    def shifted(a, n):
        z = jnp.zeros((n, SUBLANES, HGRN_DK), F32)
        return jnp.concatenate([a[n:], z] if reverse else [z, a[:GROUPS - n]], axis=0)

    acc = shifted(tot, 1)
    for sh in (1, 2, 4):
        acc = acc + shifted(acc, sh)
    return (g + acc).reshape(CHUNK, HGRN_DK)


def _level_masks():
    t = lax.broadcasted_iota(jnp.int32, (CHUNK, CHUNK), 0)
    s = lax.broadcasted_iota(jnp.int32, (CHUNK, CHUNK), 1)
    return [((t & h) != 0) & ((s & h) == 0) & ((t ^ s) < 2 * h) for h in LEVELS]


def _level_scales(b, forget, row):
    out = []
    for h in LEVELS[:3]:
        parts = [jnp.broadcast_to(b[j * 2 * h + h - 1:j * 2 * h + h, :], (2 * h, HGRN_DK))
                 for j in range(CHUNK // (2 * h))]
        mid = parts[0] if len(parts) == 1 else jnp.concatenate(parts, axis=0)
        out.append(jnp.exp(-jnp.abs(b - mid)))
    f = forget
    up1, up2, up3 = _group_roll(f, -1), _group_roll(f, -2), _group_roll(f, -3)
    dn1, dn2, dn3 = _group_roll(f, 1), _group_roll(f, 2), _group_roll(f, 3)
    r8, r4 = row & 7, row & 3
    s2 = up1 * up2
    p2 = dn1 * f
    p3 = dn2 * p2
    below = jnp.where(r8 == 4, f, jnp.where(r8 == 5, p2, jnp.where(r8 == 6, p3, dn3 * p3)))
    above = jnp.where(r8 == 0, s2 * up3, jnp.where(r8 == 1, s2, jnp.where(r8 == 2, up1, 1.0)))
    e4 = jnp.where(r8 >= 4, below, above)
    e2 = jnp.where(r4 == 0, up1, jnp.where(r4 == 1, 1.0, jnp.where(r4 == 2, f, p2)))
    e1 = jnp.where((row & 1) == 1, f, 1.0)
    return out + [e4, e2, e1]


def _hgrn_gates(zq, zf, lb):
    sq = jax.nn.sigmoid(zq)
    q = zq * sq
    sg = jax.nn.sigmoid(zf)
    forget = lb + (1.0 - lb) * sg
    return q, sq, sg, forget, 1.0 - forget, jnp.log(forget)


def _hgrn_specs(T, rb, rev):
    nr = T // rb
    ri = (lambda r: nr - 1 - r) if rev else (lambda r: r)
    return nr, ri, [
        pl.BlockSpec((rb, HGRN_DK), lambda h, r: (ri(r), h)),
        pl.BlockSpec((rb, HGRN_DK), lambda h, r: (ri(r), HGRN_HEADS + h)),
        pl.BlockSpec((rb, HGRN_DK), lambda h, r: (ri(r), 2 * HGRN_HEADS + h)),
        pl.BlockSpec((2, HGRN_DK), lambda h, r: (0, h)),
    ]


def _hgrn_fwd(z, lb_raw, rb=2048, carry=(None, None)):
    T = z.shape[0]
    rb = min(rb, T)
    ncb = rb // CHUNK
    nr, ri, in_specs = _hgrn_specs(T, rb, False)

    def body(*refs):
        hh, rr = pl.program_id(0), pl.program_id(1)
        own, finish = _carried(carry, refs, 4, 2, (hh == 0) & (rr == 0), (hh == HGRN_HEADS - 1) & (rr == nr - 1))
        zq_ref, zf_ref, zi_ref, lb_ref, o_ref, st_ref, state = own

        @pl.when(rr == 0)
        def _():
            state[...] = jnp.zeros_like(state)

        lb = _lower_bound(lb_ref)
        row = lax.broadcasted_iota(jnp.int32, (CHUNK, HGRN_DK), 0)
        masks = _level_masks()

        def operands(c):
            rows = pl.ds(pl.multiple_of(c * CHUNK, CHUNK), CHUNK)
            q, _, _, forget, k, lf = _hgrn_gates(zq_ref[rows, :], zf_ref[rows, :], lb)
            v = zi_ref[rows, :]
            b = _scan_rows(lf, row, False)
            pairs = [((q * e).astype(BF16), (k * e).astype(BF16)) for e in _level_scales(b, forget, row)]
            b_last = b[CHUNK - 1:CHUNK, :]
            return dict(c=c, rows=rows, pairs=pairs, vb=v.astype(BF16), diag=jnp.sum(q * k, axis=-1, keepdims=True) * v,
                        kd=(k * jnp.exp(b_last - b)).astype(BF16), qd=(q * jnp.exp(b)).astype(BF16),
                        decay=jnp.exp(b_last))

        def group(i, st):
            parts = [operands(i * UNROLL + j) for j in range(UNROLL)]
            for p in parts:
                sc = jnp.zeros((CHUNK, CHUNK), F32)
                for (qs, ks), mask in zip(p["pairs"], masks):
                    sc = sc + jnp.where(mask, _dot_nt(qs, ks), 0.0)
                p["sc"] = sc.astype(BF16)
            for p in parts:
                p["o"] = _dot(p["sc"], p["vb"]) + p["diag"]
                p["gain"] = _dot_tn(p["vb"], p["kd"])
            for p in parts:
                st_ref[p["c"], 0] = st
                o_ref[p["rows"], :] = p["o"] + _dot_nt(p["qd"], st.astype(BF16))
                st = st * p["decay"] + p["gain"]
            return st

        state[...] = lax.fori_loop(0, ncb // UNROLL, group, state[...])
        finish()

    in_specs, out_specs, out_shape, scratch, extra = _carried_specs(
        carry, in_specs,
        [pl.BlockSpec((rb, HGRN_DK), lambda h, r: (r, h)),
         pl.BlockSpec((ncb, 1, HGRN_DK, HGRN_DK), lambda h, r: (r, h, 0, 0))],
        [jax.ShapeDtypeStruct((T, D_MODEL), F32),
         jax.ShapeDtypeStruct((T // CHUNK, HGRN_HEADS, HGRN_DK, HGRN_DK), F32)],
        [pltpu.VMEM((HGRN_DK, HGRN_DK), F32)])
    return pl.pallas_call(
        body, name="hgrn_fwd", grid=(HGRN_HEADS, nr), in_specs=in_specs, out_specs=out_specs, out_shape=out_shape,
        scratch_shapes=scratch, compiler_params=_params(dimension_semantics=("arbitrary", "arbitrary")),
    )(z, z, z, lb_raw, *extra)


def _hgrn_bwd(z, lb_raw, states, do, rb=2048, carry=(None, None)):
    T = z.shape[0]
    rb = min(rb, T)
    ncb = rb // CHUNK
    nr, ri, in_specs = _hgrn_specs(T, rb, True)
    in_specs += [pl.BlockSpec((ncb, 1, HGRN_DK, HGRN_DK), lambda h, r: (ri(r), h, 0, 0)),
                 pl.BlockSpec((rb, HGRN_DK), lambda h, r: (ri(r), h))]

    def body(*refs):
        hh, rr = pl.program_id(0), pl.program_id(1)
        own, finish = _carried(carry, refs, 6, 4, (hh == 0) & (rr == 0), (hh == HGRN_HEADS - 1) & (rr == nr - 1))
        zq_ref, zf_ref, zi_ref, lb_ref, st_ref, do_ref, dq_ref, df_ref, di_ref, dlb_ref, dstate = own

        @pl.when(rr == 0)
        def _():
            dstate[...] = jnp.zeros_like(dstate)
            dlb_ref[...] = jnp.zeros_like(dlb_ref)

        lb = _lower_bound(lb_ref)
        row = lax.broadcasted_iota(jnp.int32, (CHUNK, HGRN_DK), 0)
        masks = _level_masks()

        def operands(c):
            rows = pl.ds(pl.multiple_of(c * CHUNK, CHUNK), CHUNK)
            zq = zq_ref[rows, :]
            q, sq, sg, forget, k, lf = _hgrn_gates(zq, zf_ref[rows, :], lb)
            v = zi_ref[rows, :]
            dov = do_ref[rows, :]
            b = _scan_rows(lf, row, False)
            b_last = b[CHUNK - 1:CHUNK, :]
            eb, ebb = jnp.exp(b), jnp.exp(b_last - b)
            es = _level_scales(b, forget, row)
            return dict(rows=rows, zq=zq, q=q, sq=sq, sg=sg, forget=forget, k=k, v=v, dov=dov, eb=eb, ebb=ebb,
                        e_last=jnp.exp(b_last), es=es, st=st_ref[c, 0], dob=dov.astype(BF16), vb=v.astype(BF16),
                        pairs=[((q * e).astype(BF16), (k * e).astype(BF16)) for e in es],
                        qd=(q * eb).astype(BF16), kd=(k * ebb).astype(BF16))

        def group(i, dlb):
            parts = [operands(ncb - 1 - (i * UNROLL_BWD + j)) for j in range(UNROLL_BWD)]
            for p in parts:
                p["da"] = _dot_nt(p["dob"], p["vb"])
                sc = jnp.zeros((CHUNK, CHUNK), F32)
                for (qs, ks), mask in zip(p["pairs"], masks):
                    sc = sc + jnp.where(mask, _dot_nt(qs, ks), 0.0)
                p["sc"] = sc.astype(BF16)
                p["dq_state"] = _dot(p["dob"], p["st"].astype(BF16))
                p["gain"] = _dot_tn(p["dob"], p["qd"])
            dst = dstate[...]
            for p in parts:
                p["dst"] = dst
                dst = dst * p["e_last"] + p["gain"]
            dstate[...] = dst
            for p in parts:
                dstb = p["dst"].astype(BF16)
                dk_state = p["ebb"] * _dot(p["vb"], dstb)
                dq = p["eb"] * p["dq_state"]
                dk = dk_state
                dv = _dot_nt(p["kd"], dstb) + _dot_tn(p["sc"], p["dob"])
                for e, (qs, ks), mask in zip(p["es"], p["pairs"], masks):
                    dam = jnp.where(mask, p["da"], 0.0).astype(BF16)
                    dq = dq + e * _dot(dam, ks)
                    dk = dk + e * _dot_tn(dam, qs)
                dad = jnp.sum(p["dov"] * p["v"], axis=-1, keepdims=True)
                p["dq"] = dq + dad * p["k"]
                p["dk"] = dk + dad * p["q"]
                p["dv"] = dv + jnp.sum(p["q"] * p["k"], axis=-1, keepdims=True) * p["dov"]
                p["extra"] = (p["e_last"] * jnp.sum(p["dst"] * p["st"], axis=0, keepdims=True)
                              + jnp.sum(p["k"] * dk_state, axis=0, keepdims=True))
            for p in parts:
                q, k, sq, sg, zq, rows = p["q"], p["k"], p["sq"], p["sg"], p["zq"], p["rows"]
                dlf = _scan_rows(q * p["dq"] - k * p["dk"], row, True) + p["extra"]
                dforget = dlf / p["forget"] - p["dk"]
                dq_ref[rows, :] = (p["dq"] * (sq * (1.0 + zq * (1.0 - sq)))).astype(BF16)
                df_ref[rows, :] = (dforget * (1.0 - lb) * sg * (1.0 - sg)).astype(BF16)
                di_ref[rows, :] = p["dv"].astype(BF16)
                dlb = dlb + jnp.sum(dforget * (1.0 - sg), axis=0, keepdims=True)
            return dlb

        dlb_ref[...] += lax.fori_loop(0, ncb // UNROLL_BWD, group, jnp.zeros((1, HGRN_DK), F32))
        finish()

    blk = pl.BlockSpec((rb, HGRN_DK), lambda h, r: (ri(r), h))
    in_specs, out_specs, out_shape, scratch, extra = _carried_specs(
        carry, in_specs, [blk, blk, blk, pl.BlockSpec((1, HGRN_DK), lambda h, r: (0, h))],
        [jax.ShapeDtypeStruct((T, D_MODEL), BF16)] * 3 + [jax.ShapeDtypeStruct((1, D_MODEL), F32)],
        [pltpu.VMEM((HGRN_DK, HGRN_DK), F32)])
    return pl.pallas_call(
        body, name="hgrn_bwd", grid=(HGRN_HEADS, nr), in_specs=in_specs, out_specs=out_specs, out_shape=out_shape,
        scratch_shapes=scratch, compiler_params=_params(dimension_semantics=("arbitrary", "arbitrary")),
    )(z, z, z, lb_raw, states, do, *extra)


MESH = pl.DeviceIdType.MESH
ANY = pl.BlockSpec(memory_space=pl.ANY)


def _place():
    return lax.axis_index("x"), lax.axis_index("y"), lax.axis_index("c")


def _sems(n):
    return [pltpu.SemaphoreType.DMA((7 * n,)), pltpu.SemaphoreType.DMA((7 * n,)), pltpu.SemaphoreType.DMA((n,))]


class _Gather:
    def __init__(self, x_ref, out_ref, send_sems, recv_sems, local_sems, idx):
        self.x_ref, self.out_ref, self.send_sems, self.recv_sems, self.local_sem, self.base = (
            x_ref, out_ref, send_sems, recv_sems, local_sems.at[idx], 7 * idx)
        x, y, c = _place()
        self.c = c
        self.me, self.sibling = (x, y, c), (x, y, 1 - c)
        self.chips = [(1 - x, y), (x, 1 - y), (1 - x, 1 - y)]

    def rows(self, px, py, pc):
        return self.out_ref.at[4 * px + 2 * py + pc]

    def copy(self, k, block, to, from_input=False):
        return pltpu.make_async_remote_copy(
            src_ref=self.x_ref if from_input else self.rows(*block), dst_ref=self.rows(*block),
            send_sem=self.send_sems.at[self.base + k], recv_sem=self.recv_sems.at[self.base + k], device_id=to,
            device_id_type=MESH)

    def first(self):
        out = [self.copy(0, self.me, self.sibling, from_input=True)]
        return out + [self.copy(1 + j, self.me, (*chip, self.c), from_input=True) for j, chip in enumerate(self.chips)]

    def start(self):
        pltpu.make_async_copy(self.x_ref, self.rows(*self.me), self.local_sem).start()
        for cp in self.first():
            cp.start()

    def finish(self):
        passed = [self.copy(4 + j, (*chip, self.c), self.sibling) for j, chip in enumerate(self.chips)]
        for j, chip in enumerate(self.chips):
            self.copy(1 + j, (*chip, self.c), self.me).wait_recv()
            passed[j].start()
        self.copy(0, self.sibling, self.me).wait_recv()
        for j, chip in enumerate(self.chips):
            self.copy(4 + j, (*chip, 1 - self.c), self.me).wait_recv()
        for cp in self.first() + passed:
            cp.wait_send()
        pltpu.make_async_copy(self.x_ref, self.rows(*self.me), self.local_sem).wait()


class _Many:
    def __init__(self, kind, in_refs, out_refs, send_sems, recv_sems, local_sems):
        self.ops = [kind(x, o, send_sems, recv_sems, local_sems, i) for i, (x, o) in enumerate(zip(in_refs, out_refs))]

    def start(self):
        for op in self.ops:
            op.start()

    def finish(self):
        for op in self.ops:
            op.finish()


def _result_shapes(kind, arrs):
    return [jax.ShapeDtypeStruct(a.shape if kind is _Exchange else (N_DEV,) + a.shape, a.dtype) for a in arrs]


def _all_gather(name, shards):
    n = len(shards)

    def body(*refs):
        g = _Many(_Gather, refs[:n], refs[n:2 * n], *refs[2 * n:])
        g.start()
        g.finish()

    return pl.pallas_call(
        body, name=name, out_shape=_result_shapes(_Gather, shards), in_specs=[ANY] * n, out_specs=[ANY] * n,
        scratch_shapes=_sems(n),
    )(*shards)


def _peers(x, y, c):
    out = []
    for k in range(1, N_DEV):
        px = 1 - x if k & 4 else x
        py = 1 - y if k & 2 else y
        pc = 1 - c if k & 1 else c
        out.append((k, (px, py, pc), 4 * px + 2 * py + pc))
    return out


class _Exchange:
    def __init__(self, g_ref, recv_ref, send_sems, recv_sems, local_sems, idx):
        x, y, c = _place()
        me = 4 * x + 2 * y + c
        self.local = pltpu.make_async_copy(g_ref.at[me], recv_ref.at[me], local_sems.at[idx])
        self.copies = [
            pltpu.make_async_remote_copy(
                src_ref=g_ref.at[pidx], dst_ref=recv_ref.at[me], send_sem=send_sems.at[7 * idx + k - 1],
                recv_sem=recv_sems.at[7 * idx + k - 1], device_id=peer, device_id_type=MESH)
            for k, peer, pidx in _peers(x, y, c)]

    def start(self):
        self.local.start()
        for cp in self.copies:
            cp.start()

    def finish(self):
        for cp in self.copies:
            cp.wait()
        self.local.wait()


def _carried(carry, refs, n_in, n_out, first, last):
    kind, arrs = carry
    if kind is None:
        return refs, lambda: None
    n = len(arrs)
    ins, rest = refs[:n_in], refs[n_in + n:]
    outs, scratch = rest[:n_out], rest[n_out + n:]
    op = _Many(kind, refs[n_in:n_in + n], rest[n_out:n_out + n], *scratch[len(scratch) - 3:])

    @pl.when(first)
    def _():
        op.start()

    def finish():
        @pl.when(last)
        def _():
            op.finish()

    return tuple(ins) + tuple(outs) + tuple(scratch[:len(scratch) - 3]), finish


def _carried_specs(carry, in_specs, out_specs, out_shape, scratch):
    kind, arrs = carry
    if kind is None:
        return in_specs, out_specs, out_shape, scratch, []
    n = len(arrs)
    return (list(in_specs) + [ANY] * n, list(out_specs) + [ANY] * n,
            list(out_shape) + _result_shapes(kind, arrs), list(scratch) + _sems(n), list(arrs))


def _adamw(w, g, m, v):
    m = ADAM_B1 * m + (1.0 - ADAM_B1) * g
    v = ADAM_B2 * v + (1.0 - ADAM_B2) * (g * g)
    m_hat = m / (1.0 - ADAM_B1 ** ADAM_STEP)
    v_hat = v / (1.0 - ADAM_B2 ** ADAM_STEP)
    delta = -ADAM_LR * (m_hat / (jnp.sqrt(v_hat) + ADAM_EPS) + ADAM_WD * w)
    return delta, m, v


def _adamw_sum(name, recvs, w, m, v):
    L, R, C = w.shape
    tm = 128 if R % 128 == 0 else 64
    assert R % tm == 0 and len(recvs) == L

    def body(*refs):
        r_refs, (w_ref, m_ref, v_ref, g_ref, d_ref, nm_ref, nv_ref) = refs[:L], refs[L:]
        for l in range(L):
            g = r_refs[l][0].astype(F32)
            for s in range(1, N_DEV):
                g = g + r_refs[l][s].astype(F32)
            g_ref[l] = g
            d_ref[l], nm_ref[l], nv_ref[l] = _adamw(w_ref[l], g, m_ref[l], v_ref[l])

    blk = pl.BlockSpec((L, tm, C), lambda i: (0, i, 0))
    return pl.pallas_call(
        body, name=name, grid=(R // tm,),
        in_specs=[pl.BlockSpec((N_DEV, tm, C), lambda i: (0, i, 0))] * L + [blk, blk, blk],
        out_specs=[blk] * 4, out_shape=[jax.ShapeDtypeStruct((L, R, C), F32)] * 4,
        compiler_params=_params(dimension_semantics=("arbitrary",)),
    )(*recvs, w, m, v)


def _small_sync(part, w, m, v):
    def body(p_ref, w_ref, m_ref, v_ref, g_ref, d_ref, nm_ref, nv_ref, gath, send_sems, recv_sems):
        x, y, c = _place()
        me = 4 * x + 2 * y + c
        gath[me] = p_ref[...]
        copies = []
        for k, peer, _ in _peers(x, y, c):
            cp = pltpu.make_async_remote_copy(
                src_ref=p_ref, dst_ref=gath.at[me], send_sem=send_sems.at[k - 1], recv_sem=recv_sems.at[k - 1],
                device_id=peer, device_id_type=MESH)
            cp.start()
            copies.append(cp)
        for cp in copies:
            cp.wait()
        g = gath[0]
        for s in range(1, N_DEV):
            g = g + gath[s]
        wv = w_ref[...]
        l0, l1 = w_ref[8:9, :], w_ref[9:10, :]
        mx = jnp.maximum(l0, l1)
        e0, e1 = jnp.exp(l0 - mx), jnp.exp(l1 - mx)
        g9 = g[9:10, :] * (e0 / (e0 + e1)) * (e1 / (e0 + e1))
        row = lax.broadcasted_iota(jnp.int32, g.shape, 0)
        g = jnp.where(row == 9, g9, jnp.where(row == 8, -g9, g))
        g_ref[...] = g
        d_ref[...], nm_ref[...], nv_ref[...] = _adamw(wv, g, m_ref[...], v_ref[...])

    vm = pl.BlockSpec(memory_space=pltpu.VMEM)
    return pl.pallas_call(
        body, name="small_params_sync", in_specs=[vm] * 4, out_specs=[vm] * 4,
        out_shape=[jax.ShapeDtypeStruct(part.shape, F32)] * 4,
        scratch_shapes=[pltpu.VMEM((N_DEV,) + part.shape, F32), pltpu.SemaphoreType.DMA((7,)),
                        pltpu.SemaphoreType.DMA((7,))],
    )(part, w, m, v)


def _shards_bf16(d, pieces):
    return [d[name][layer].astype(BF16) for name, layer in pieces]


def _gathered(arrs, pieces, out):
    for a, (name, layer) in zip(arrs, pieces):
        out[name, layer] = a if name in COL_SHARDED else a.reshape(N_DEV * a.shape[1], a.shape[2])


def _pad_row(a, width=D_MODEL):
    a = a.reshape(1, -1)
    return jnp.pad(a, ((0, 0), (0, width - a.shape[1])))


LOSS_ROW = 11


def _pack_small(d, gn_full, loss=None):
    rows = [d["mix_norm"], d["mlp_norm"], d["final_norm"].reshape(1, D_MODEL),
            _pad_row(d["attn_b_qkv"], 2 * D_MODEL).reshape(2, D_MODEL), _pad_row(d["attn_sinks"]),
            d["hgrn_lower_bounds"], gn_full.reshape(1, D_MODEL)]
    if loss is not None:
        rows.append(_pad_row(loss))
    p = jnp.concatenate(rows, axis=0)
    return jnp.pad(p, ((0, SMALL_ROWS - p.shape[0]), (0, 0)))


def _unpack_small(p, me):
    return dict(
        mix_norm=p[0:2], mlp_norm=p[2:4], final_norm=p[4],
        attn_b_qkv=p[5:7].reshape(1, 2 * D_MODEL)[:, :QKV_DIM], attn_sinks=p[7:8, :N_Q_HEADS],
        hgrn_lower_bounds=p[8:10], hgrn_g_norm=lax.dynamic_slice(p[10:11], (0, me * 128), (1, 128)))


WEIGHT_NAMES = ['mix_norm', 'mlp_norm', 'final_norm', 'attn_w_qkv', 'attn_b_qkv', 'attn_sinks', 'attn_w_o', 'hgrn_w_in',
                'hgrn_g_norm', 'hgrn_w_o', 'hgrn_lower_bounds', 'mlp_w_up', 'mlp_w_down']
SMALL_NAMES = ('mix_norm', 'mlp_norm', 'final_norm', 'attn_b_qkv', 'attn_sinks', 'hgrn_lower_bounds', 'hgrn_g_norm')


def _rotary_tables(positions):
    inv_freq = ROPE_THETA ** (-jnp.arange(0, 2 * ROT_HALF, 2, dtype=F32) / (2 * ROT_HALF))
    ang = positions.astype(F32).reshape(-1, 1) * inv_freq
    cos, sin = jnp.cos(ang), jnp.sin(ang)
    r = jnp.arange(LANES) % HEAD_DIM
    idx = r % ROT_HALF
    c = jnp.where(r < 2 * ROT_HALF, cos[:, idx], 1.0)
    sa = jnp.where((r >= ROT_HALF) & (r < 2 * ROT_HALF), sin[:, idx], 0.0)
    sb = jnp.where(r < ROT_HALF, -sin[:, idx], 0.0)
    return jnp.concatenate([c, sa, sb], axis=1)


def kernel(x, positions, mix_norm, mlp_norm, final_norm, attn_w_qkv, attn_b_qkv, attn_sinks, attn_w_o, hgrn_w_in, hgrn_g_norm, hgrn_w_o, hgrn_lower_bounds, mlp_w_up, mlp_w_down, loss_target, m_mix_norm, m_mlp_norm, m_final_norm, m_attn_w_qkv, m_attn_b_qkv, m_attn_sinks, m_attn_w_o, m_hgrn_w_in, m_hgrn_g_norm, m_hgrn_w_o, m_hgrn_lower_bounds, m_mlp_w_up, m_mlp_w_down, v_mix_norm, v_mlp_norm, v_final_norm, v_attn_w_qkv, v_attn_b_qkv, v_attn_sinks, v_attn_w_o, v_hgrn_w_in, v_hgrn_g_norm, v_hgrn_w_o, v_hgrn_lower_bounds, v_mlp_w_up, v_mlp_w_down):
    w = dict(mix_norm=mix_norm, mlp_norm=mlp_norm, final_norm=final_norm, attn_w_qkv=attn_w_qkv, attn_b_qkv=attn_b_qkv,
             attn_sinks=attn_sinks, attn_w_o=attn_w_o, hgrn_w_in=hgrn_w_in, hgrn_g_norm=hgrn_g_norm, hgrn_w_o=hgrn_w_o,
             hgrn_lower_bounds=hgrn_lower_bounds, mlp_w_up=mlp_w_up, mlp_w_down=mlp_w_down)
    m = dict(mix_norm=m_mix_norm, mlp_norm=m_mlp_norm, final_norm=m_final_norm, attn_w_qkv=m_attn_w_qkv,
             attn_b_qkv=m_attn_b_qkv, attn_sinks=m_attn_sinks, attn_w_o=m_attn_w_o, hgrn_w_in=m_hgrn_w_in,
             hgrn_g_norm=m_hgrn_g_norm, hgrn_w_o=m_hgrn_w_o, hgrn_lower_bounds=m_hgrn_lower_bounds, mlp_w_up=m_mlp_w_up,
             mlp_w_down=m_mlp_w_down)
    v = dict(mix_norm=v_mix_norm, mlp_norm=v_mlp_norm, final_norm=v_final_norm, attn_w_qkv=v_attn_w_qkv,
             attn_b_qkv=v_attn_b_qkv, attn_sinks=v_attn_sinks, attn_w_o=v_attn_w_o, hgrn_w_in=v_hgrn_w_in,
             hgrn_g_norm=v_hgrn_g_norm, hgrn_w_o=v_hgrn_w_o, hgrn_lower_bounds=v_hgrn_lower_bounds, mlp_w_up=v_mlp_w_up,
             mlp_w_down=v_mlp_w_down)
    me = 4 * lax.axis_index("x") + 2 * lax.axis_index("y") + lax.axis_index("c")

    gn = hgrn_g_norm.reshape(1, 128)
    gn_a = gn.astype(BF16)
    gn_b = (gn - gn_a.astype(F32)).astype(BF16)
    gn_c = (gn - gn_a.astype(F32) - gn_b.astype(F32)).astype(BF16)
    gn_rows = jnp.pad(jnp.concatenate([gn_a, gn_b, gn_c], axis=1), ((0, 15), (0, D_MODEL - 3 * 128)))
    full = {}
    got = _all_gather("gather_attn_weights", _shards_bf16(w, GATHER_FIRST) + [gn_rows])
    _gathered(got[:1], GATHER_FIRST, full)
    w_qkv = full["attn_w_qkv", 0].transpose(1, 0, 2).reshape(D_MODEL, QKV_DIM)
    gn_terms = got[1][:, 0, :3 * 128].astype(F32).reshape(N_DEV, 3, 128)
    gn_full = ((gn_terms[:, 0] + gn_terms[:, 1]) + gn_terms[:, 2]).reshape(1, D_MODEL)

    x0 = x[0]
    tgt = loss_target[0]
    rot = _rotary_tables(positions)
    row = lambda a: a.reshape(1, -1)

    qkv, h0 = _norm_mm("qkv_proj", x0, row(mix_norm[0]), w_qkv, attn_b_qkv, rot=rot)
    att, *got = _attn_fwd(qkv, attn_sinks, carry=(_Gather, _shards_bf16(w, GATHER_ATTN)))
    _gathered(got, GATHER_ATTN, full)
    x1 = _mm_res("attn_out_proj", att, full["attn_w_o", 0], x0)
    u0, h1, *got = _norm_mm("mlp0_up", x1, row(mlp_norm[0]), full["mlp_w_up", 0],
                            carry=(_Gather, _shards_bf16(w, GATHER_MLP0)))
    _gathered(got, GATHER_MLP0, full)
    x2, a0 = _mlp_down("mlp0_down", u0, full["mlp_w_down", 0], x1)
    z, h2 = _norm_mm("hgrn_in_proj", x2, row(mix_norm[1]), full["hgrn_w_in", 0])
    o_raw, states, *got = _hgrn_fwd(z, hgrn_lower_bounds, carry=(_Gather, _shards_bf16(w, GATHER_HGRN)))
    _gathered(got, GATHER_HGRN, full)
    x3, o2 = _hgrn_out("hgrn_out_proj", o_raw, z, gn_full, full["hgrn_w_o", 0], x2)
    u1, h3 = _norm_mm("mlp1_up", x3, row(mlp_norm[1]), full["mlp_w_up", 1])
    dx4, a1, loss_part, g_final = _mlp_down("mlp1_down_loss", u1, full["mlp_w_down", 1], x3,
                                            loss_head=(tgt, row(final_norm)))

    gw = {}
    du1, = _mlp_bwd_act("mlp1_bwd_act", dx4, u1, full["mlp_w_down", 1])
    dx3, g_mlp1 = _mm_nt_rmsbwd("mlp1_bwd_in", du1, full["mlp_w_up", 1], x3, row(mlp_norm[1]), dx4)
    gw["mlp_w_down", 1] = _mm_tn("mlp1_dw_down", a1, dx4, "rows")
    gw["mlp_w_up", 1] = _mm_tn("mlp1_dw_up", h3, du1, "cols")

    do_raw, dg, g_gn = _hgrn_out_bwd("hgrn_out_bwd", dx3, o_raw, z, full["hgrn_w_o", 0], gn_full)
    gw["hgrn_w_o", 0] = _mm_tn("hgrn_dw_o", o2, dx3, "rows")
    recvs = {}
    dzq, dzf, dzi, g_lb, *recv = _hgrn_bwd(z, hgrn_lower_bounds, states, do_raw,
                                           carry=(_Exchange, [gw[p] for p in GRADS_HGRN]))
    recvs.update(zip(GRADS_HGRN, recv))
    dz = [dzq, dzf, dzi, dg]
    dx2, g_mix1 = _mm_nt_rmsbwd("hgrn_in_bwd", dz, full["hgrn_w_in", 0], x2, row(mix_norm[1]), dx3)
    gw["hgrn_w_in", 0] = jnp.concatenate(
        [_mm_tn(f"hgrn_dw_in{j}", h2, d, "cols") for j, d in enumerate(dz)], axis=0)

    du0, *recv = _mlp_bwd_act("mlp0_bwd_act", dx2, u0, full["mlp_w_down", 0],
                              carry=(_Exchange, [gw[p] for p in GRADS_MLP0]))
    recvs.update(zip(GRADS_MLP0, recv))
    dx1, g_mlp0 = _mm_nt_rmsbwd("mlp0_bwd_in", du0, full["mlp_w_up", 0], x1, row(mlp_norm[0]), dx2)
    gw["mlp_w_down", 0] = _mm_tn("mlp0_dw_down", a0, dx2, "rows")
    gw["mlp_w_up", 0] = _mm_tn("mlp0_dw_up", h1, du0, "cols")

    datt = _mm_nt("attn_out_bwd", dx1, full["attn_w_o", 0], BF16)
    gw["attn_w_o", 0] = _mm_tn("attn_dw_o", att, dx1, "rows")
    dqkv, g_sink, *recv = _attn_bwd(qkv, rot, attn_sinks, datt, carry=(_Exchange, [gw[p] for p in GRADS_ATTN]))
    recvs.update(zip(GRADS_ATTN, recv))
    g_qkv = _mm_tn("attn_dw_qkv", h0, dqkv, bn=512)
    g_qkv = g_qkv.reshape(D_MODEL, N_DEV, QKV_DIM // N_DEV).transpose(1, 0, 2).astype(BF16)
    dx0, g_mix0, g_bqkv, recvs["attn_w_qkv", 0] = _mm_nt_rmsbwd(
        "qkv_bwd", dqkv, w_qkv, x0, row(mix_norm[0]), dx1, with_colsum=True, carry=(_Exchange, [g_qkv]))

    big = {name: _adamw_sum("adamw_" + name, [recvs[name, l] for l in range(w[name].shape[0])], w[name], m[name], v[name])
           for name in BIG_NAMES}

    zero_row = jnp.zeros((1, D_MODEL), F32)
    part = _pack_small(dict(
        mix_norm=jnp.concatenate([g_mix0, g_mix1], axis=0), mlp_norm=jnp.concatenate([g_mlp0, g_mlp1], axis=0),
        final_norm=g_final, attn_b_qkv=g_bqkv, attn_sinks=g_sink[:, :N_Q_HEADS],
        hgrn_lower_bounds=jnp.concatenate([zero_row, g_lb], axis=0)), g_gn, loss=loss_part)

    def spread(a):
        return lax.dynamic_update_slice(zero_row, a.reshape(1, 128), (0, me * 128))

    small_in = [_pack_small({n: d[n] for n in SMALL_NAMES if n != "hgrn_g_norm"}, spread(d["hgrn_g_norm"]))
                for d in (w, m, v)]
    synced = _small_sync(part, *small_in)
    small = [_unpack_small(p, me) for p in synced]

    outs = [synced[0][LOSS_ROW, 0], dx0.reshape(x.shape)]
    for kind, grp_small in enumerate(small):
        for name in WEIGHT_NAMES:
            val = grp_small[name] if name in SMALL_NAMES else big[name][kind]
            outs.append(val.reshape(w[name].shape))
    return tuple(outs)
```

```python
import functools

import jax
import jax.numpy as jnp
from jax import lax
from jax.experimental import pallas as pl
from jax.experimental.pallas import tpu as pltpu

F32 = jnp.float32
BF16 = jnp.bfloat16

D_MODEL = 1024
HEAD_DIM = 64
N_Q_HEADS = 16
Q_DIM = 1024
KV_DIM = 256
QKV_DIM = 1536
ATT_BLOCK = 128
ROT_HALF = 8
ROPE_THETA = 500000.0
NEG_INF = -1e30
HGRN_HEADS = 8
HGRN_DK = 128
CHUNK = 64
D_FF = 4096
NORM_EPS = 1e-5
N_DEV = 8

ADAM_LR = 0.001
ADAM_B1 = 0.9
ADAM_B2 = 0.999
ADAM_EPS = 1e-08
ADAM_WD = 0.01
ADAM_STEP = 10

LANES = 128
VMEM_LIMIT = 56 * 1024 * 1024

GATHER_FIRST = (("attn_w_qkv", 0),)
GATHER_ATTN = (("attn_w_o", 0), ("mlp_w_up", 0), ("mlp_w_down", 0))
GATHER_MLP0 = (("hgrn_w_in", 0), ("hgrn_w_o", 0))
GATHER_HGRN = (("mlp_w_up", 1), ("mlp_w_down", 1))
GRADS_HGRN = (("mlp_w_down", 1), ("mlp_w_up", 1), ("hgrn_w_o", 0))
GRADS_MLP0 = (("hgrn_w_in", 0),)
GRADS_ATTN = (("mlp_w_down", 0), ("mlp_w_up", 0), ("attn_w_o", 0))
COL_SHARDED = ("attn_w_qkv", "hgrn_w_in", "mlp_w_up")
BIG_NAMES = ("attn_w_qkv", "attn_w_o", "hgrn_w_in", "hgrn_w_o", "mlp_w_up", "mlp_w_down")
SMALL_ROWS = 16


def _dot(a, b):
    return jnp.dot(a, b, preferred_element_type=F32)


def _dot_nt(a, b):
    return lax.dot_general(a, b, (((1,), (1,)), ((), ())), preferred_element_type=F32)


def _dot_tn(a, b):
    return lax.dot_general(a, b, (((0,), (0,)), ((), ())), preferred_element_type=F32)


def _params(**kw):
    return pltpu.CompilerParams(vmem_limit_bytes=VMEM_LIMIT, **kw)


def _full_spec(a):
    nd = a.ndim
    return pl.BlockSpec(a.shape, lambda *_: (0,) * nd)


def _row_call(name, body, n_rows, tm, row_ins, full_ins, row_outs, acc_outs=(), carry=(None, None)):
    steps = n_rows // tm
    in_specs = [pl.BlockSpec((tm, w), functools.partial(lambda i, cb: (i, cb), cb=cb)) for _, w, cb in row_ins]
    in_specs += [_full_spec(a) for a in full_ins]
    out_shape = [jax.ShapeDtypeStruct((n_rows, w), dt) for w, dt in row_outs]
    out_specs = [pl.BlockSpec((tm, w), lambda i: (i, 0)) for w, _ in row_outs]
    for shp, dt in acc_outs:
        out_shape.append(jax.ShapeDtypeStruct(shp, dt))
        out_specs.append(pl.BlockSpec(shp, functools.partial(lambda i, nd: (0,) * nd, nd=len(shp))))
    n_in, n_out = len(in_specs), len(out_specs)
    in_specs, out_specs, out_shape, scratch, extra = _carried_specs(carry, in_specs, out_specs, out_shape, [])

    def wrapped(*refs):
        i = pl.program_id(0)
        own, finish = _carried(carry, refs, n_in, n_out, i == 0, i == steps - 1)
        body(*own)
        finish()

    return pl.pallas_call(
        wrapped, name=name, grid=(steps,), in_specs=in_specs, out_specs=out_specs, out_shape=out_shape,
        scratch_shapes=scratch, compiler_params=_params(dimension_semantics=("arbitrary",)),
    )(*[a for a, _, _ in row_ins], *full_ins, *extra)


def _rms(x, gain):
    r = lax.rsqrt(jnp.mean(x * x, axis=-1, keepdims=True) + NORM_EPS)
    xhat = x * r
    return xhat * gain, xhat, r


def _rms_bwd(dy, xhat, r, gain):
    dxhat = dy * gain
    dx = r * (dxhat - xhat * jnp.mean(dxhat * xhat, axis=-1, keepdims=True))
    return dx, dy * xhat


def _norm_mm(name, x, gain, w, bias=None, rot=None, tm=512, carry=(None, None)):
    T = x.shape[0]
    tm = min(tm, T)
    nc = 512
    blocked = w.ndim == 3
    n = N_DEV * w.shape[2] if blocked else w.shape[1]
    assert n % nc == 0 and (not blocked or w.shape[2] == nc)

    def body(*refs):
        x_ref, refs = refs[0], refs[1:]
        if rot is not None:
            t_ref, refs = refs[0], refs[1:]
        g_ref, w_ref, refs = refs[0], refs[1], refs[2:]
        if bias is not None:
            b_ref, refs = refs[0], refs[1:]
        y_ref, h_ref = refs
        h, _, _ = _rms(x_ref[...], g_ref[...])
        hb = h.astype(BF16)
        h_ref[...] = hb
        for c in range(n // nc):
            sl = slice(c * nc, (c + 1) * nc)
            y = _dot(hb, w_ref[c] if blocked else w_ref[:, sl])
            if bias is not None:
                y = y + b_ref[:, sl]
            if rot is None:
                y_ref[:, sl] = y
            else:
                n_rot = max(0, min(nc, Q_DIM + KV_DIM - c * nc)) // LANES
                pieces = _rot_fwd(y[:, :n_rot * LANES], t_ref[...]) if n_rot else []
                for j in range(nc // LANES):
                    col = slice(c * nc + j * LANES, c * nc + (j + 1) * LANES)
                    y_ref[:, col] = pieces[j] if j < n_rot else y[:, j * LANES:(j + 1) * LANES]

    rows = [(x, D_MODEL, 0)] + ([(rot, 3 * LANES, 0)] if rot is not None else [])
    full = [gain, w] + ([bias] if bias is not None else [])
    return _row_call(name, body, T, tm, rows, full, [(n, F32), (D_MODEL, BF16)], carry=carry)


def _mm_res(name, a, w, res, tm=512):
    T = a.shape[0]
    tm = min(tm, T)

    def body(a_ref, r_ref, w_ref, o_ref):
        o_ref[...] = r_ref[...] + _dot(a_ref[...], w_ref[...])

    return _row_call(name, body, T, tm, [(a, a.shape[1], 0), (res, D_MODEL, 0)], [w], [(D_MODEL, F32)])[0]


def _mlp_down(name, u, w, res, tm=512, loss_head=None):
    T = u.shape[0]
    tm = min(tm, T)
    kc = 1024
    sub = min(256, tm)

    def body(*refs):
        if loss_head is None:
            u_ref, r_ref, w_ref, o_ref, a_ref = refs
        else:
            u_ref, r_ref, t_ref, w_ref, g_ref, o_ref, a_ref, loss_ref, dg_ref = refs

            @pl.when(pl.program_id(0) == 0)
            def _():
                loss_ref[...] = jnp.zeros_like(loss_ref)
                dg_ref[...] = jnp.zeros_like(dg_ref)

        for r0 in range(0, tm, sub):
            rs = slice(r0, r0 + sub)
            acc = r_ref[rs, :]
            for c in range(D_FF // kc):
                sl = slice(c * kc, (c + 1) * kc)
                a = jnp.maximum(u_ref[rs, sl], 0.0)
                ab = (a * a).astype(BF16)
                a_ref[rs, sl] = ab
                acc = acc + _dot(ab, w_ref[sl, :])
            if loss_head is None:
                o_ref[rs, :] = acc
            else:
                gain_v = g_ref[...]
                y, xhat, r = _rms(acc, gain_v)
                diff = y - t_ref[rs, :]
                per_row = jnp.sum(diff * diff, axis=-1, keepdims=True) * (1.0 / D_MODEL)
                loss_ref[...] += jnp.broadcast_to(0.5 * jnp.sum(per_row, axis=0, keepdims=True), loss_ref.shape)
                dx, dgr = _rms_bwd(diff * (1.0 / D_MODEL), xhat, r, gain_v)
                o_ref[rs, :] = dx
                dg_ref[...] += jnp.sum(dgr, axis=0, keepdims=True)

    rows, full, acc_outs = [(u, D_FF, 0), (res, D_MODEL, 0)], [w], []
    if loss_head is not None:
        rows, full = rows + [(loss_head[0], D_MODEL, 0)], full + [loss_head[1]]
        acc_outs = [((1, LANES), F32), ((1, D_MODEL), F32)]
    return _row_call(name, body, T, tm, rows, full, [(D_MODEL, F32), (D_FF, BF16)], acc_outs)


def _hgrn_out(name, o_raw, z, gn, w, res, tm=512):
    T = o_raw.shape[0]
    tm = min(tm, T)

    def body(o_ref, g_ref, r_ref, gn_ref, w_ref, x_ref, a_ref):
        y, _, _ = _rms(o_ref[...], gn_ref[...])
        g = g_ref[...]
        a = (y * (g * jax.nn.sigmoid(g))).astype(BF16)
        a_ref[...] = a
        x_ref[...] = r_ref[...] + _dot(a, w_ref[...])

    return _row_call(name, body, T, tm, [(o_raw, D_MODEL, 0), (z, D_MODEL, 3), (res, D_MODEL, 0)], [gn, w],
                     [(D_MODEL, F32), (D_MODEL, BF16)])


def _mm_nt_rmsbwd(name, dy, w, x, gain, dres, tm=512, with_colsum=False, carry=(None, None)):
    T = x.shape[0]
    tm = min(tm, T)
    dys = list(dy) if isinstance(dy, (list, tuple)) else [dy]
    width = dys[0].shape[1]
    n = width * len(dys)
    sub = min(256, tm)
    assert not with_colsum or len(dys) == 1

    def body(*refs):
        dy_refs, refs = refs[:len(dys)], refs[len(dys):]
        if with_colsum:
            x_ref, dr_ref, w_ref, g_ref, dx_ref, dg_ref, cs_ref = refs
        else:
            x_ref, dr_ref, w_ref, g_ref, dx_ref, dg_ref = refs

        @pl.when(pl.program_id(0) == 0)
        def _():
            dg_ref[...] = jnp.zeros_like(dg_ref)
            if with_colsum:
                cs_ref[...] = jnp.zeros_like(cs_ref)

        gain_v = g_ref[...]
        for r0 in range(0, tm, sub):
            rs = slice(r0, r0 + sub)
            if w.ndim == 3:
                nb = w.shape[2]
                dh = None
                for p in range(N_DEV):
                    piece, off = divmod(p * nb, width)
                    part = _dot_nt(dy_refs[piece][rs, off:off + nb].astype(BF16), w_ref[p])
                    dh = part if dh is None else dh + part
            else:
                dh = _dot_nt(dy_refs[0][rs, :].astype(BF16), w_ref[...])
            _, xhat, r = _rms(x_ref[rs, :], gain_v)
            dx, dgr = _rms_bwd(dh, xhat, r, gain_v)
            dx_ref[rs, :] = dr_ref[rs, :] + dx
            dg_ref[...] += jnp.sum(dgr, axis=0, keepdims=True)
            if with_colsum:
                cs_ref[...] += jnp.sum(dy_refs[0][rs, :].astype(F32), axis=0, keepdims=True)

    acc = [((1, D_MODEL), F32)] + ([((1, n), F32)] if with_colsum else [])
    rows = [(d, width, 0) for d in dys] + [(x, D_MODEL, 0), (dres, D_MODEL, 0)]
    return _row_call(name, body, T, tm, rows, [w, gain], [(D_MODEL, F32)], acc, carry=carry)


def _mm_nt(name, dy, w, out_dtype, tm=512):
    T = dy.shape[0]
    tm = min(tm, T)
    k = w.shape[0]

    def body(dy_ref, w_ref, o_ref):
        o_ref[...] = _dot_nt(dy_ref[...].astype(BF16), w_ref[...]).astype(out_dtype)

    return _row_call(name, body, T, tm, [(dy, dy.shape[1], 0)], [w], [(k, out_dtype)])[0]


def _mlp_bwd_act(name, dy, u, w_down, tm=512, carry=(None, None)):
    T = u.shape[0]
    tm = min(tm, T)
    kc = 1024

    def body(dy_ref, u_ref, w_ref, du_ref):
        dyb = dy_ref[...].astype(BF16)
        for c in range(D_FF // kc):
            sl = slice(c * kc, (c + 1) * kc)
            da = _dot_nt(dyb, w_ref[sl, :])
            du_ref[:, sl] = (da * (2.0 * jnp.maximum(u_ref[:, sl], 0.0))).astype(BF16)

    return _row_call(name, body, T, tm, [(dy, D_MODEL, 0), (u, D_FF, 0)], [w_down], [(D_FF, BF16)], carry=carry)


def _hgrn_out_bwd(name, dx, o_raw, z, w, gn, tm=512):
    T = dx.shape[0]
    tm = min(tm, T)

    def body(dx_ref, o_ref, g_ref, w_ref, gn_ref, do_ref, dg_ref, dgn_ref):
        @pl.when(pl.program_id(0) == 0)
        def _():
            dgn_ref[...] = jnp.zeros_like(dgn_ref)

        da = _dot_nt(dx_ref[...].astype(BF16), w_ref[...])
        gn_v = gn_ref[...]
        y, xhat, r = _rms(o_ref[...], gn_v)
        g = g_ref[...]
        sg = jax.nn.sigmoid(g)
        dg_ref[...] = (da * y * (sg * (1.0 + g * (1.0 - sg)))).astype(BF16)
        dyn = da * (g * sg)
        do, dgr = _rms_bwd(dyn, xhat, r, gn_v)
        do_ref[...] = do
        dgn_ref[...] += jnp.sum(dgr, axis=0, keepdims=True)

    return _row_call(name, body, T, tm, [(dx, D_MODEL, 0), (o_raw, D_MODEL, 0), (z, D_MODEL, 3)], [w, gn],
                     [(D_MODEL, F32), (D_MODEL, BF16)], [((1, D_MODEL), F32)])


COL_BLOCK = D_FF // N_DEV


def _mm_tn(name, a, b, shard=None, bm=1024, bn=1024, tk=2048):
    T, M = a.shape
    N = b.shape[1]
    bm, bn, tk = min(bm, M), min(bn, N), min(tk, T)
    nk = T // tk
    if shard is None:
        out_shape, out_block = jax.ShapeDtypeStruct((M, N), F32), (bm, bn)
        out_map = lambda i, j, k: (i, j)
    elif shard == "cols":
        assert bn % COL_BLOCK == 0 and N % bn == 0
        out_shape = jax.ShapeDtypeStruct((N // COL_BLOCK, M, COL_BLOCK), BF16)
        out_block = (bn // COL_BLOCK, bm, COL_BLOCK)
        out_map = lambda i, j, k: (j, i, 0)
    else:
        rows = M // N_DEV
        assert bm % rows == 0
        out_shape, out_block = jax.ShapeDtypeStruct((N_DEV, rows, N), BF16), (bm // rows, rows, bn)
        out_map = lambda i, j, k: (i, 0, j)

    def body(a_ref, b_ref, o_ref, acc):
        k = pl.program_id(2)

        @pl.when(k == 0)
        def _():
            acc[...] = jnp.zeros_like(acc)

        acc[...] += _dot_tn(a_ref[...].astype(BF16), b_ref[...].astype(BF16))

        @pl.when(k == nk - 1)
        def _():
            if shard == "cols":
                for c in range(bn // COL_BLOCK):
                    o_ref[c] = acc[:, c * COL_BLOCK:(c + 1) * COL_BLOCK].astype(BF16)
            else:
                o_ref[...] = acc[...].reshape(out_block).astype(o_ref.dtype)

    return pl.pallas_call(
        body, name=name, grid=(M // bm, N // bn, nk),
        in_specs=[pl.BlockSpec((tk, bm), lambda i, j, k: (k, i)), pl.BlockSpec((tk, bn), lambda i, j, k: (k, j))],
        out_specs=pl.BlockSpec(out_block, out_map), out_shape=out_shape,
        scratch_shapes=[pltpu.VMEM((bm, bn), F32)],
        compiler_params=_params(dimension_semantics=("parallel", "parallel", "arbitrary")),
    )(a, b)


def _rot_fwd(x, tab):
    c, sa, sb = tab[:, :LANES], tab[:, LANES:2 * LANES], tab[:, 2 * LANES:]
    outs = []
    for j in range(x.shape[1] // LANES):
        xs = x[:, j * LANES:(j + 1) * LANES]
        outs.append(xs * c + pltpu.roll(xs, ROT_HALF, 1) * sa + pltpu.roll(xs, LANES - ROT_HALF, 1) * sb)
    return outs


def _rot_bwd(dys, tab):
    c, sa, sb = tab[:, :LANES], tab[:, LANES:2 * LANES], tab[:, 2 * LANES:]
    return [dy * c + pltpu.roll(dy * sa, LANES - ROT_HALF, 1) + pltpu.roll(dy * sb, ROT_HALF, 1) for dy in dys]


ATT_SCALE = HEAD_DIM ** -0.5


def _attn_masks(n):
    kj = lax.broadcasted_iota(jnp.int32, (2 * ATT_BLOCK, ATT_BLOCK), 0)
    qi = lax.broadcasted_iota(jnp.int32, (2 * ATT_BLOCK, ATT_BLOCK), 1)
    delta = qi + ATT_BLOCK - kj
    first_key = jnp.where(n > 0, 0, ATT_BLOCK)
    valid = (delta >= 0) & (delta < ATT_BLOCK) & (kj >= first_key)
    low = lax.broadcasted_iota(jnp.int32, (1, LANES), 1) < HEAD_DIM
    upper = lax.broadcasted_iota(jnp.int32, (LANES, 1), 0) < HEAD_DIM
    return valid, low, upper


def _softmax_sink(s, valid, sink):
    s = jnp.where(valid, s, NEG_INF)
    m = jnp.maximum(jnp.max(s, axis=0, keepdims=True), sink)
    e = jnp.exp(s - m)
    es = jnp.exp(sink - m)
    inv = 1.0 / (jnp.sum(e, axis=0, keepdims=True) + es)
    return e * inv, es * inv


def _attn_specs(nb, tables):
    prev = lambda n: jnp.maximum(jnp.minimum(n, nb - 1) - 1, 0)
    cur = lambda n: jnp.minimum(n, nb - 1)
    specs = [
        pl.BlockSpec((ATT_BLOCK, Q_DIM), lambda n: (cur(n), 0)),
        pl.BlockSpec((ATT_BLOCK, KV_DIM), lambda n: (prev(n), 4)),
        pl.BlockSpec((ATT_BLOCK, KV_DIM), lambda n: (cur(n), 4)),
        pl.BlockSpec((ATT_BLOCK, KV_DIM), lambda n: (prev(n), 5)),
        pl.BlockSpec((ATT_BLOCK, KV_DIM), lambda n: (cur(n), 5)),
    ]
    if tables:
        specs += [pl.BlockSpec((ATT_BLOCK, 3 * LANES), lambda n: (prev(n), 0)),
                  pl.BlockSpec((ATT_BLOCK, 3 * LANES), lambda n: (cur(n), 0))]
    return specs + [pl.BlockSpec(memory_space=pltpu.SMEM)]


def _kv_band(prev_ref, cur_ref):
    out = []
    for j in range(KV_DIM // LANES):
        sl = slice(j * LANES, (j + 1) * LANES)
        band = jnp.concatenate([prev_ref[:, sl], cur_ref[:, sl]], axis=0)
        out.append((band, pltpu.roll(band, HEAD_DIM, 1)))
    return out


def _bf16(bands, transposed=False):
    return [[(a.T if transposed else a).astype(BF16) for a in pair] for pair in bands]


def _attn_fwd(qkv, sinks, carry=(None, None)):
    T = qkv.shape[0]
    nb = T // ATT_BLOCK

    def body(*refs):
        n = pl.program_id(0)
        own, finish = _carried(carry, refs, 6, 1, n == 0, n == nb - 1)
        q_ref, kp_ref, kc_ref, vp_ref, vc_ref, sink_ref, o_ref = own
        valid, low, upper = _attn_masks(n)
        ks = _bf16(_kv_band(kp_ref, kc_ref))
        vts = _bf16(_kv_band(vp_ref, vc_ref), transposed=True)
        heads = []
        for p in range(Q_DIM // LANES):
            kpair, khalf = p // 4, (p // 2) % 2
            q_pair = q_ref[:, p * LANES:(p + 1) * LANES] * ATT_SCALE
            for hf in range(2):
                qm = jnp.where(low if hf == 0 else ~low, q_pair, 0.0).astype(BF16)
                sw = 0 if khalf == hf else 1
                heads.append((2 * p + hf, kpair, sw, _dot_nt(ks[kpair][sw], qm)))
        probs = [_softmax_sink(s, valid, sink_ref[0, h])[0].astype(BF16) for h, _, _, s in heads]
        outs = [_dot(vts[kpair][sw], pr) for (_, kpair, sw, _), pr in zip(heads, probs)]
        for p in range(Q_DIM // LANES):
            o_ref[:, p * LANES:(p + 1) * LANES] = jnp.where(upper, outs[2 * p], outs[2 * p + 1]).T.astype(BF16)
        finish()

    in_specs, out_specs, out_shape, scratch, extra = _carried_specs(
        carry, _attn_specs(nb, False), [pl.BlockSpec((ATT_BLOCK, Q_DIM), lambda n: (n, 0))],
        [jax.ShapeDtypeStruct((T, Q_DIM), BF16)], [])
    return pl.pallas_call(
        body, name="attn_fwd", grid=(nb,), in_specs=in_specs, out_specs=out_specs, out_shape=out_shape,
        scratch_shapes=scratch, compiler_params=_params(dimension_semantics=("arbitrary",)),
    )(qkv, qkv, qkv, qkv, qkv, sinks, *extra)


def _attn_bwd(qkv, rot, sinks, dout, carry=(None, None)):
    T = qkv.shape[0]
    nb = T // ATT_BLOCK
    npair = KV_DIM // LANES

    def body(*refs):
        n = pl.program_id(0)
        own, finish = _carried(carry, refs, 9, 2, n == 0, n == nb)
        (q_ref, kp_ref, kc_ref, vp_ref, vc_ref, tp_ref, tc_ref, sink_ref, do_ref, dqkv_ref, dsink_ref,
         dq_c, dk_c, dv_c) = own

        @pl.when(n == 0)
        def _():
            dq_c[...] = jnp.zeros_like(dq_c)
            dk_c[...] = jnp.zeros_like(dk_c)
            dv_c[...] = jnp.zeros_like(dv_c)
            dsink_ref[...] = jnp.zeros_like(dsink_ref)

        def flush(dk_prev, dv_prev, tab_ref):
            dqkv_ref[:, :Q_DIM] = dq_c[...].astype(BF16)
            dk = _rot_bwd([dk_c[:, j * LANES:(j + 1) * LANES] + dk_prev[j] for j in range(npair)], tab_ref[...])
            for j in range(npair):
                dqkv_ref[:, Q_DIM + j * LANES:Q_DIM + (j + 1) * LANES] = dk[j].astype(BF16)
                dqkv_ref[:, Q_DIM + KV_DIM + j * LANES:Q_DIM + KV_DIM + (j + 1) * LANES] = (
                    dv_c[:, j * LANES:(j + 1) * LANES] + dv_prev[j]).astype(BF16)

        @pl.when(n < nb)
        def _():
            valid, low, upper = _attn_masks(n)
            lane = lax.broadcasted_iota(jnp.int32, (1, LANES), 1)
            k_band = _kv_band(kp_ref, kc_ref)
            ks, kts = _bf16(k_band), _bf16(k_band, transposed=True)
            vs = _bf16(_kv_band(vp_ref, vc_ref))
            dk_acc = [[jnp.zeros((2 * ATT_BLOCK, LANES), F32) for _ in range(2)] for _ in range(npair)]
            dv_acc = [[jnp.zeros((2 * ATT_BLOCK, LANES), F32) for _ in range(2)] for _ in range(npair)]
            dsink = jnp.zeros((1, LANES), F32)
            heads = []
            for p in range(Q_DIM // LANES):
                kpair, khalf = p // 4, (p // 2) % 2
                q_pair = q_ref[:, p * LANES:(p + 1) * LANES] * ATT_SCALE
                do_pair = do_ref[:, p * LANES:(p + 1) * LANES]
                for hf in range(2):
                    sel = low if hf == 0 else ~low
                    qm = jnp.where(sel, q_pair, 0.0).astype(BF16)
                    dom = jnp.where(sel, do_pair, 0.0).astype(BF16)
                    sw = 0 if khalf == hf else 1
                    heads.append((2 * p + hf, kpair, sw, qm, dom,
                                  _dot_nt(ks[kpair][sw], qm), _dot_nt(vs[kpair][sw], dom)))
            grads = []
            for h, kpair, sw, qm, dom, s, dp in heads:
                pr, ps = _softmax_sink(s, valid, sink_ref[0, h])
                dd = jnp.sum(pr * dp, axis=0, keepdims=True)
                dsink = dsink + jnp.where(lane == h, -jnp.sum(ps * dd, axis=1, keepdims=True), 0.0)
                grads.append((pr * (dp - dd)).astype(BF16))
                heads[h] = (kpair, sw, qm, dom, pr.astype(BF16))
            dq_t = []
            for (kpair, sw, qm, dom, pr), ds in zip(heads, grads):
                dq_t.append(_dot(kts[kpair][sw], ds))
                dk_acc[kpair][sw] = dk_acc[kpair][sw] + _dot(ds, qm)
                dv_acc[kpair][sw] = dv_acc[kpair][sw] + _dot(pr, dom)
            dqs = [jnp.where(upper, dq_t[2 * p], dq_t[2 * p + 1]).T * ATT_SCALE for p in range(Q_DIM // LANES)]
            dk_acc = [a[0] + pltpu.roll(a[1], HEAD_DIM, 1) for a in dk_acc]
            dv_acc = [a[0] + pltpu.roll(a[1], HEAD_DIM, 1) for a in dv_acc]
            flush([a[:ATT_BLOCK] for a in dk_acc], [a[:ATT_BLOCK] for a in dv_acc], tp_ref)
            dq = _rot_bwd(dqs, tc_ref[...])
            for p in range(Q_DIM // LANES):
                dq_c[:, p * LANES:(p + 1) * LANES] = dq[p]
            for j in range(npair):
                dk_c[:, j * LANES:(j + 1) * LANES] = dk_acc[j][ATT_BLOCK:]
                dv_c[:, j * LANES:(j + 1) * LANES] = dv_acc[j][ATT_BLOCK:]
            dsink_ref[...] += dsink

        @pl.when(n == nb)
        def _():
            zero = [jnp.zeros((ATT_BLOCK, LANES), F32) for _ in range(npair)]
            flush(zero, zero, tc_ref)

        finish()

    do_spec = pl.BlockSpec((ATT_BLOCK, Q_DIM), lambda n: (jnp.minimum(n, nb - 1), 0))
    in_specs, out_specs, out_shape, scratch, extra = _carried_specs(
        carry, _attn_specs(nb, True) + [do_spec],
        [pl.BlockSpec((ATT_BLOCK, QKV_DIM), lambda n: (jnp.maximum(n - 1, 0), 0)),
         pl.BlockSpec((1, LANES), lambda n: (0, 0))],
        [jax.ShapeDtypeStruct((T, QKV_DIM), BF16), jax.ShapeDtypeStruct((1, LANES), F32)],
        [pltpu.VMEM((ATT_BLOCK, Q_DIM), F32), pltpu.VMEM((ATT_BLOCK, KV_DIM), F32),
         pltpu.VMEM((ATT_BLOCK, KV_DIM), F32)])
    return pl.pallas_call(
        body, name="attn_bwd", grid=(nb + 1,), in_specs=in_specs, out_specs=out_specs, out_shape=out_shape,
        scratch_shapes=scratch, compiler_params=_params(dimension_semantics=("arbitrary",)),
    )(qkv, qkv, qkv, qkv, qkv, rot, rot, sinks, dout, *extra)


LEVELS = (32, 16, 8, 4, 2, 1)
SUBLANES = 8
UNROLL = 16
UNROLL_BWD = 8


def _lower_bound(lb_ref):
    l0, l1 = lb_ref[0:1, :], lb_ref[1:2, :]
    mx = jnp.maximum(l0, l1)
    e0, e1 = jnp.exp(l0 - mx), jnp.exp(l1 - mx)
    return e1 / (e0 + e1)


GROUPS = CHUNK // SUBLANES


def _group_roll(x, k):
    return pltpu.roll(x.reshape(GROUPS, SUBLANES, HGRN_DK), k % SUBLANES, 1).reshape(CHUNK, HGRN_DK)


def _scan_rows(x, row, reverse):
    r8 = row & (SUBLANES - 1)
    for sh in (1, 2, 4):
        ok = (r8 < SUBLANES - sh) if reverse else (r8 >= sh)
        x = x + jnp.where(ok, _group_roll(x, -sh if reverse else sh), 0.0)
    g = x.reshape(GROUPS, SUBLANES, HGRN_DK)
    edge = 0 if reverse else SUBLANES - 1
    tot = jnp.broadcast_to(g[:, edge:edge + 1, :], g.shape)

    def shifted(a, n):
        z = jnp.zeros((n, SUBLANES, HGRN_DK), F32)
        return jnp.concatenate([a[n:], z] if reverse else [z, a[:GROUPS - n]], axis=0)

    acc = shifted(tot, 1)
    for sh in (1, 2, 4):
        acc = acc + shifted(acc, sh)
    return (g + acc).reshape(CHUNK, HGRN_DK)


def _level_masks():
    t = lax.broadcasted_iota(jnp.int32, (CHUNK, CHUNK), 0)
    s = lax.broadcasted_iota(jnp.int32, (CHUNK, CHUNK), 1)
    return [((t & h) != 0) & ((s & h) == 0) & ((t ^ s) < 2 * h) for h in LEVELS]


def _level_scales(b, forget, row):
    out = []
    for h in LEVELS[:3]:
        parts = [jnp.broadcast_to(b[j * 2 * h + h - 1:j * 2 * h + h, :], (2 * h, HGRN_DK))
                 for j in range(CHUNK // (2 * h))]
        mid = parts[0] if len(parts) == 1 else jnp.concatenate(parts, axis=0)
        out.append(jnp.exp(-jnp.abs(b - mid)))
    f = forget
    up1, up2, up3 = _group_roll(f, -1), _group_roll(f, -2), _group_roll(f, -3)
    dn1, dn2, dn3 = _group_roll(f, 1), _group_roll(f, 2), _group_roll(f, 3)
    r8, r4 = row & 7, row & 3
    s2 = up1 * up2
    p2 = dn1 * f
    p3 = dn2 * p2
    below = jnp.where(r8 == 4, f, jnp.where(r8 == 5, p2, jnp.where(r8 == 6, p3, dn3 * p3)))
    above = jnp.where(r8 == 0, s2 * up3, jnp.where(r8 == 1, s2, jnp.where(r8 == 2, up1, 1.0)))
    e4 = jnp.where(r8 >= 4, below, above)
    e2 = jnp.where(r4 == 0, up1, jnp.where(r4 == 1, 1.0, jnp.where(r4 == 2, f, p2)))
    e1 = jnp.where((row & 1) == 1, f, 1.0)
    return out + [e4, e2, e1]


def _hgrn_gates(zq, zf, lb):
    sq = jax.nn.sigmoid(zq)
    q = zq * sq
    sg = jax.nn.sigmoid(zf)
    forget = lb + (1.0 - lb) * sg
    return q, sq, sg, forget, 1.0 - forget, jnp.log(forget)


def _hgrn_specs(T, rb, rev):
    nr = T // rb
    ri = (lambda r: nr - 1 - r) if rev else (lambda r: r)
    return nr, ri, [
        pl.BlockSpec((rb, HGRN_DK), lambda h, r: (ri(r), h)),
        pl.BlockSpec((rb, HGRN_DK), lambda h, r: (ri(r), HGRN_HEADS + h)),
        pl.BlockSpec((rb, HGRN_DK), lambda h, r: (ri(r), 2 * HGRN_HEADS + h)),
        pl.BlockSpec((2, HGRN_DK), lambda h, r: (0, h)),
    ]


def _hgrn_fwd(z, lb_raw, rb=2048, carry=(None, None)):
    T = z.shape[0]
    rb = min(rb, T)
    ncb = rb // CHUNK
    unroll = min(UNROLL, ncb)
    assert ncb % unroll == 0
    nr, ri, in_specs = _hgrn_specs(T, rb, False)

    def body(*refs):
        hh, rr = pl.program_id(0), pl.program_id(1)
        own, finish = _carried(carry, refs, 4, 2, (hh == 0) & (rr == 0), (hh == HGRN_HEADS - 1) & (rr == nr - 1))
        zq_ref, zf_ref, zi_ref, lb_ref, o_ref, st_ref, state = own

        @pl.when(rr == 0)
        def _():
            state[...] = jnp.zeros_like(state)

        lb = _lower_bound(lb_ref)
        row = lax.broadcasted_iota(jnp.int32, (CHUNK, HGRN_DK), 0)
        masks = _level_masks()

        def operands(c):
            rows = pl.ds(pl.multiple_of(c * CHUNK, CHUNK), CHUNK)
            q, _, _, forget, k, lf = _hgrn_gates(zq_ref[rows, :], zf_ref[rows, :], lb)
            v = zi_ref[rows, :]
            b = _scan_rows(lf, row, False)
            pairs = [((q * e).astype(BF16), (k * e).astype(BF16)) for e in _level_scales(b, forget, row)]
            b_last = b[CHUNK - 1:CHUNK, :]
            return dict(c=c, rows=rows, pairs=pairs, vb=v.astype(BF16), diag=jnp.sum(q * k, axis=-1, keepdims=True) * v,
                        kd=(k * jnp.exp(b_last - b)).astype(BF16), qd=(q * jnp.exp(b)).astype(BF16),
                        decay=jnp.exp(b_last))

        def group(i, st):
            parts = [operands(i * unroll + j) for j in range(unroll)]
            for p in parts:
                sc = jnp.zeros((CHUNK, CHUNK), F32)
                for (qs, ks), mask in zip(p["pairs"], masks):
                    sc = sc + jnp.where(mask, _dot_nt(qs, ks), 0.0)
                p["sc"] = sc.astype(BF16)
            for p in parts:
                p["o"] = _dot(p["sc"], p["vb"]) + p["diag"]
                p["gain"] = _dot_tn(p["vb"], p["kd"])
            for p in parts:
                st_ref[p["c"], 0] = st
                o_ref[p["rows"], :] = p["o"] + _dot_nt(p["qd"], st.astype(BF16))
                st = st * p["decay"] + p["gain"]
            return st

        state[...] = lax.fori_loop(0, ncb // unroll, group, state[...])
        finish()

    in_specs, out_specs, out_shape, scratch, extra = _carried_specs(
        carry, in_specs,
        [pl.BlockSpec((rb, HGRN_DK), lambda h, r: (r, h)),
         pl.BlockSpec((ncb, 1, HGRN_DK, HGRN_DK), lambda h, r: (r, h, 0, 0))],
        [jax.ShapeDtypeStruct((T, D_MODEL), F32),
         jax.ShapeDtypeStruct((T // CHUNK, HGRN_HEADS, HGRN_DK, HGRN_DK), F32)],
        [pltpu.VMEM((HGRN_DK, HGRN_DK), F32)])
    return pl.pallas_call(
        body, name="hgrn_fwd", grid=(HGRN_HEADS, nr), in_specs=in_specs, out_specs=out_specs, out_shape=out_shape,
        scratch_shapes=scratch, compiler_params=_params(dimension_semantics=("arbitrary", "arbitrary")),
    )(z, z, z, lb_raw, *extra)


def _hgrn_bwd(z, lb_raw, states, do, rb=2048, carry=(None, None)):
    T = z.shape[0]
    rb = min(rb, T)
    ncb = rb // CHUNK
    unroll = min(UNROLL_BWD, ncb)
    assert ncb % unroll == 0
    nr, ri, in_specs = _hgrn_specs(T, rb, True)
    in_specs += [pl.BlockSpec((ncb, 1, HGRN_DK, HGRN_DK), lambda h, r: (ri(r), h, 0, 0)),
                 pl.BlockSpec((rb, HGRN_DK), lambda h, r: (ri(r), h))]

    def body(*refs):
        hh, rr = pl.program_id(0), pl.program_id(1)
        own, finish = _carried(carry, refs, 6, 4, (hh == 0) & (rr == 0), (hh == HGRN_HEADS - 1) & (rr == nr - 1))
        zq_ref, zf_ref, zi_ref, lb_ref, st_ref, do_ref, dq_ref, df_ref, di_ref, dlb_ref, dstate = own

        @pl.when(rr == 0)
        def _():
            dstate[...] = jnp.zeros_like(dstate)
            dlb_ref[...] = jnp.zeros_like(dlb_ref)

        lb = _lower_bound(lb_ref)
        row = lax.broadcasted_iota(jnp.int32, (CHUNK, HGRN_DK), 0)
        masks = _level_masks()

        def operands(c):
            rows = pl.ds(pl.multiple_of(c * CHUNK, CHUNK), CHUNK)
            zq = zq_ref[rows, :]
            q, sq, sg, forget, k, lf = _hgrn_gates(zq, zf_ref[rows, :], lb)
            v = zi_ref[rows, :]
            dov = do_ref[rows, :]
            b = _scan_rows(lf, row, False)
            b_last = b[CHUNK - 1:CHUNK, :]
            eb, ebb = jnp.exp(b), jnp.exp(b_last - b)
            es = _level_scales(b, forget, row)
            return dict(rows=rows, zq=zq, q=q, sq=sq, sg=sg, forget=forget, k=k, v=v, dov=dov, eb=eb, ebb=ebb,
                        e_last=jnp.exp(b_last), es=es, st=st_ref[c, 0], dob=dov.astype(BF16), vb=v.astype(BF16),
                        pairs=[((q * e).astype(BF16), (k * e).astype(BF16)) for e in es],
                        qd=(q * eb).astype(BF16), kd=(k * ebb).astype(BF16))

        def group(i, dlb):
            parts = [operands(ncb - 1 - (i * unroll + j)) for j in range(unroll)]
            for p in parts:
                p["da"] = _dot_nt(p["dob"], p["vb"])
                sc = jnp.zeros((CHUNK, CHUNK), F32)
                for (qs, ks), mask in zip(p["pairs"], masks):
                    sc = sc + jnp.where(mask, _dot_nt(qs, ks), 0.0)
                p["sc"] = sc.astype(BF16)
                p["dq_state"] = _dot(p["dob"], p["st"].astype(BF16))
                p["gain"] = _dot_tn(p["dob"], p["qd"])
            dst = dstate[...]
            for p in parts:
                p["dst"] = dst
                dst = dst * p["e_last"] + p["gain"]
            dstate[...] = dst
            for p in parts:
                dstb = p["dst"].astype(BF16)
                dk_state = p["ebb"] * _dot(p["vb"], dstb)
                dq = p["eb"] * p["dq_state"]
                dk = dk_state
                dv = _dot_nt(p["kd"], dstb) + _dot_tn(p["sc"], p["dob"])
                for e, (qs, ks), mask in zip(p["es"], p["pairs"], masks):
                    dam = jnp.where(mask, p["da"], 0.0).astype(BF16)
                    dq = dq + e * _dot(dam, ks)
                    dk = dk + e * _dot_tn(dam, qs)
                dad = jnp.sum(p["dov"] * p["v"], axis=-1, keepdims=True)
                p["dq"] = dq + dad * p["k"]
                p["dk"] = dk + dad * p["q"]
                p["dv"] = dv + jnp.sum(p["q"] * p["k"], axis=-1, keepdims=True) * p["dov"]
                p["extra"] = (p["e_last"] * jnp.sum(p["dst"] * p["st"], axis=0, keepdims=True)
                              + jnp.sum(p["k"] * dk_state, axis=0, keepdims=True))
            for p in parts:
                q, k, sq, sg, zq, rows = p["q"], p["k"], p["sq"], p["sg"], p["zq"], p["rows"]
                dlf = _scan_rows(q * p["dq"] - k * p["dk"], row, True) + p["extra"]
                dforget = dlf / p["forget"] - p["dk"]
                dq_ref[rows, :] = (p["dq"] * (sq * (1.0 + zq * (1.0 - sq)))).astype(BF16)
                df_ref[rows, :] = (dforget * (1.0 - lb) * sg * (1.0 - sg)).astype(BF16)
                di_ref[rows, :] = p["dv"].astype(BF16)
                dlb = dlb + jnp.sum(dforget * (1.0 - sg), axis=0, keepdims=True)
            return dlb

        dlb_ref[...] += lax.fori_loop(0, ncb // unroll, group, jnp.zeros((1, HGRN_DK), F32))
        finish()

    blk = pl.BlockSpec((rb, HGRN_DK), lambda h, r: (ri(r), h))
    in_specs, out_specs, out_shape, scratch, extra = _carried_specs(
        carry, in_specs, [blk, blk, blk, pl.BlockSpec((1, HGRN_DK), lambda h, r: (0, h))],
        [jax.ShapeDtypeStruct((T, D_MODEL), BF16)] * 3 + [jax.ShapeDtypeStruct((1, D_MODEL), F32)],
        [pltpu.VMEM((HGRN_DK, HGRN_DK), F32)])
    return pl.pallas_call(
        body, name="hgrn_bwd", grid=(HGRN_HEADS, nr), in_specs=in_specs, out_specs=out_specs, out_shape=out_shape,
        scratch_shapes=scratch, compiler_params=_params(dimension_semantics=("arbitrary", "arbitrary")),
    )(z, z, z, lb_raw, states, do, *extra)


MESH = pl.DeviceIdType.MESH
ANY = pl.BlockSpec(memory_space=pl.ANY)


def _place():
    return lax.axis_index("x"), lax.axis_index("y"), lax.axis_index("c")


def _sems(n):
    return [pltpu.SemaphoreType.DMA((7 * n,)), pltpu.SemaphoreType.DMA((7 * n,)), pltpu.SemaphoreType.DMA((n,))]


class _Gather:
    def __init__(self, x_ref, out_ref, send_sems, recv_sems, local_sems, idx):
        self.x_ref, self.out_ref, self.send_sems, self.recv_sems, self.local_sem, self.base = (
            x_ref, out_ref, send_sems, recv_sems, local_sems.at[idx], 7 * idx)
        x, y, c = _place()
        self.c = c
        self.me, self.sibling = (x, y, c), (x, y, 1 - c)
        self.chips = [(1 - x, y), (x, 1 - y), (1 - x, 1 - y)]

    def rows(self, px, py, pc):
        return self.out_ref.at[4 * px + 2 * py + pc]

    def copy(self, k, block, to, from_input=False):
        return pltpu.make_async_remote_copy(
            src_ref=self.x_ref if from_input else self.rows(*block), dst_ref=self.rows(*block),
            send_sem=self.send_sems.at[self.base + k], recv_sem=self.recv_sems.at[self.base + k], device_id=to,
            device_id_type=MESH)

    def first(self):
        out = [self.copy(0, self.me, self.sibling, from_input=True)]
        return out + [self.copy(1 + j, self.me, (*chip, self.c), from_input=True) for j, chip in enumerate(self.chips)]

    def start(self):
        pltpu.make_async_copy(self.x_ref, self.rows(*self.me), self.local_sem).start()
        for cp in self.first():
            cp.start()

    def finish(self):
        passed = [self.copy(4 + j, (*chip, self.c), self.sibling) for j, chip in enumerate(self.chips)]
        for j, chip in enumerate(self.chips):
            self.copy(1 + j, (*chip, self.c), self.me).wait_recv()
            passed[j].start()
        self.copy(0, self.sibling, self.me).wait_recv()
        for j, chip in enumerate(self.chips):
            self.copy(4 + j, (*chip, 1 - self.c), self.me).wait_recv()
        for cp in self.first() + passed:
            cp.wait_send()
        pltpu.make_async_copy(self.x_ref, self.rows(*self.me), self.local_sem).wait()


class _Many:
    def __init__(self, kind, in_refs, out_refs, send_sems, recv_sems, local_sems):
        self.ops = [kind(x, o, send_sems, recv_sems, local_sems, i) for i, (x, o) in enumerate(zip(in_refs, out_refs))]

    def start(self):
        for op in self.ops:
            op.start()

    def finish(self):
        for op in self.ops:
            op.finish()


def _result_shapes(kind, arrs):
    return [jax.ShapeDtypeStruct(a.shape if kind is _Exchange else (N_DEV,) + a.shape, a.dtype) for a in arrs]


def _all_gather(name, shards):
    n = len(shards)

    def body(*refs):
        g = _Many(_Gather, refs[:n], refs[n:2 * n], *refs[2 * n:])
        g.start()
        g.finish()

    return pl.pallas_call(
        body, name=name, out_shape=_result_shapes(_Gather, shards), in_specs=[ANY] * n, out_specs=[ANY] * n,
        scratch_shapes=_sems(n),
    )(*shards)


def _peers(x, y, c):
    out = []
    for k in range(1, N_DEV):
        px = 1 - x if k & 4 else x
        py = 1 - y if k & 2 else y
        pc = 1 - c if k & 1 else c
        out.append((k, (px, py, pc), 4 * px + 2 * py + pc))
    return out


class _Exchange:
    def __init__(self, g_ref, recv_ref, send_sems, recv_sems, local_sems, idx):
        x, y, c = _place()
        me = 4 * x + 2 * y + c
        self.local = pltpu.make_async_copy(g_ref.at[me], recv_ref.at[me], local_sems.at[idx])
        self.copies = [
            pltpu.make_async_remote_copy(
                src_ref=g_ref.at[pidx], dst_ref=recv_ref.at[me], send_sem=send_sems.at[7 * idx + k - 1],
                recv_sem=recv_sems.at[7 * idx + k - 1], device_id=peer, device_id_type=MESH)
            for k, peer, pidx in _peers(x, y, c)]

    def start(self):
        self.local.start()
        for cp in self.copies:
            cp.start()

    def finish(self):
        for cp in self.copies:
            cp.wait()
        self.local.wait()


def _carried(carry, refs, n_in, n_out, first, last):
    kind, arrs = carry
    if kind is None:
        return refs, lambda: None
    n = len(arrs)
    ins, rest = refs[:n_in], refs[n_in + n:]
    outs, scratch = rest[:n_out], rest[n_out + n:]
    op = _Many(kind, refs[n_in:n_in + n], rest[n_out:n_out + n], *scratch[len(scratch) - 3:])

    @pl.when(first)
    def _():
        op.start()

    def finish():
        @pl.when(last)
        def _():
            op.finish()

    return tuple(ins) + tuple(outs) + tuple(scratch[:len(scratch) - 3]), finish


def _carried_specs(carry, in_specs, out_specs, out_shape, scratch):
    kind, arrs = carry
    if kind is None:
        return in_specs, out_specs, out_shape, scratch, []
    n = len(arrs)
    return (list(in_specs) + [ANY] * n, list(out_specs) + [ANY] * n,
            list(out_shape) + _result_shapes(kind, arrs), list(scratch) + _sems(n), list(arrs))


def _adamw(w, g, m, v):
    m = ADAM_B1 * m + (1.0 - ADAM_B1) * g
    v = ADAM_B2 * v + (1.0 - ADAM_B2) * (g * g)
    m_hat = m / (1.0 - ADAM_B1 ** ADAM_STEP)
    v_hat = v / (1.0 - ADAM_B2 ** ADAM_STEP)
    delta = -ADAM_LR * (m_hat / (jnp.sqrt(v_hat) + ADAM_EPS) + ADAM_WD * w)
    return delta, m, v


def _adamw_sum(name, recvs, w, m, v):
    L, R, C = w.shape
    tm = 128 if R % 128 == 0 else 64
    assert R % tm == 0 and len(recvs) == L

    def body(*refs):
        r_refs, (w_ref, m_ref, v_ref, g_ref, d_ref, nm_ref, nv_ref) = refs[:L], refs[L:]
        for l in range(L):
            g = r_refs[l][0].astype(F32)
            for s in range(1, N_DEV):
                g = g + r_refs[l][s].astype(F32)
            g_ref[l] = g
            d_ref[l], nm_ref[l], nv_ref[l] = _adamw(w_ref[l], g, m_ref[l], v_ref[l])

    blk = pl.BlockSpec((L, tm, C), lambda i: (0, i, 0))
    return pl.pallas_call(
        body, name=name, grid=(R // tm,),
        in_specs=[pl.BlockSpec((N_DEV, tm, C), lambda i: (0, i, 0))] * L + [blk, blk, blk],
        out_specs=[blk] * 4, out_shape=[jax.ShapeDtypeStruct((L, R, C), F32)] * 4,
        compiler_params=_params(dimension_semantics=("arbitrary",)),
    )(*recvs, w, m, v)


def _small_sync(part, w, m, v):
    def body(p_ref, w_ref, m_ref, v_ref, g_ref, d_ref, nm_ref, nv_ref, gath, send_sems, recv_sems):
        x, y, c = _place()
        me = 4 * x + 2 * y + c
        gath[me] = p_ref[...]
        copies = []
        for k, peer, _ in _peers(x, y, c):
            cp = pltpu.make_async_remote_copy(
                src_ref=p_ref, dst_ref=gath.at[me], send_sem=send_sems.at[k - 1], recv_sem=recv_sems.at[k - 1],
                device_id=peer, device_id_type=MESH)
            cp.start()
            copies.append(cp)
        for cp in copies:
            cp.wait()
        g = gath[0]
        for s in range(1, N_DEV):
            g = g + gath[s]
        wv = w_ref[...]
        l0, l1 = w_ref[8:9, :], w_ref[9:10, :]
        mx = jnp.maximum(l0, l1)
        e0, e1 = jnp.exp(l0 - mx), jnp.exp(l1 - mx)
        g9 = g[9:10, :] * (e0 / (e0 + e1)) * (e1 / (e0 + e1))
        row = lax.broadcasted_iota(jnp.int32, g.shape, 0)
        g = jnp.where(row == 9, g9, jnp.where(row == 8, -g9, g))
        g_ref[...] = g
        d_ref[...], nm_ref[...], nv_ref[...] = _adamw(wv, g, m_ref[...], v_ref[...])

    vm = pl.BlockSpec(memory_space=pltpu.VMEM)
    return pl.pallas_call(
        body, name="small_params_sync", in_specs=[vm] * 4, out_specs=[vm] * 4,
        out_shape=[jax.ShapeDtypeStruct(part.shape, F32)] * 4,
        scratch_shapes=[pltpu.VMEM((N_DEV,) + part.shape, F32), pltpu.SemaphoreType.DMA((7,)),
                        pltpu.SemaphoreType.DMA((7,))],
    )(part, w, m, v)


def _shards_bf16(d, pieces):
    return [d[name][layer].astype(BF16) for name, layer in pieces]


def _gathered(arrs, pieces, out):
    for a, (name, layer) in zip(arrs, pieces):
        out[name, layer] = a if name in COL_SHARDED else a.reshape(N_DEV * a.shape[1], a.shape[2])


def _pad_row(a, width=D_MODEL):
    a = a.reshape(1, -1)
    return jnp.pad(a, ((0, 0), (0, width - a.shape[1])))


LOSS_ROW = 11


def _pack_small(d, gn_full, loss=None):
    rows = [d["mix_norm"], d["mlp_norm"], d["final_norm"].reshape(1, D_MODEL),
            _pad_row(d["attn_b_qkv"], 2 * D_MODEL).reshape(2, D_MODEL), _pad_row(d["attn_sinks"]),
            d["hgrn_lower_bounds"], gn_full.reshape(1, D_MODEL)]
    if loss is not None:
        rows.append(_pad_row(loss))
    p = jnp.concatenate(rows, axis=0)
    return jnp.pad(p, ((0, SMALL_ROWS - p.shape[0]), (0, 0)))


def _unpack_small(p, me):
    return dict(
        mix_norm=p[0:2], mlp_norm=p[2:4], final_norm=p[4],
        attn_b_qkv=p[5:7].reshape(1, 2 * D_MODEL)[:, :QKV_DIM], attn_sinks=p[7:8, :N_Q_HEADS],
        hgrn_lower_bounds=p[8:10], hgrn_g_norm=lax.dynamic_slice(p[10:11], (0, me * 128), (1, 128)))


WEIGHT_NAMES = ['mix_norm', 'mlp_norm', 'final_norm', 'attn_w_qkv', 'attn_b_qkv', 'attn_sinks', 'attn_w_o', 'hgrn_w_in',
                'hgrn_g_norm', 'hgrn_w_o', 'hgrn_lower_bounds', 'mlp_w_up', 'mlp_w_down']
SMALL_NAMES = ('mix_norm', 'mlp_norm', 'final_norm', 'attn_b_qkv', 'attn_sinks', 'hgrn_lower_bounds', 'hgrn_g_norm')


def _rotary_tables(positions):
    inv_freq = ROPE_THETA ** (-jnp.arange(0, 2 * ROT_HALF, 2, dtype=F32) / (2 * ROT_HALF))
    ang = positions.astype(F32).reshape(-1, 1) * inv_freq
    cos, sin = jnp.cos(ang), jnp.sin(ang)
    r = jnp.arange(LANES) % HEAD_DIM
    idx = r % ROT_HALF
    c = jnp.where(r < 2 * ROT_HALF, cos[:, idx], 1.0)
    sa = jnp.where((r >= ROT_HALF) & (r < 2 * ROT_HALF), sin[:, idx], 0.0)
    sb = jnp.where(r < ROT_HALF, -sin[:, idx], 0.0)
    return jnp.concatenate([c, sa, sb], axis=1)


def kernel(x, positions, mix_norm, mlp_norm, final_norm, attn_w_qkv, attn_b_qkv, attn_sinks, attn_w_o, hgrn_w_in, hgrn_g_norm, hgrn_w_o, hgrn_lower_bounds, mlp_w_up, mlp_w_down, loss_target, m_mix_norm, m_mlp_norm, m_final_norm, m_attn_w_qkv, m_attn_b_qkv, m_attn_sinks, m_attn_w_o, m_hgrn_w_in, m_hgrn_g_norm, m_hgrn_w_o, m_hgrn_lower_bounds, m_mlp_w_up, m_mlp_w_down, v_mix_norm, v_mlp_norm, v_final_norm, v_attn_w_qkv, v_attn_b_qkv, v_attn_sinks, v_attn_w_o, v_hgrn_w_in, v_hgrn_g_norm, v_hgrn_w_o, v_hgrn_lower_bounds, v_mlp_w_up, v_mlp_w_down):
    w = dict(mix_norm=mix_norm, mlp_norm=mlp_norm, final_norm=final_norm, attn_w_qkv=attn_w_qkv, attn_b_qkv=attn_b_qkv,
             attn_sinks=attn_sinks, attn_w_o=attn_w_o, hgrn_w_in=hgrn_w_in, hgrn_g_norm=hgrn_g_norm, hgrn_w_o=hgrn_w_o,
             hgrn_lower_bounds=hgrn_lower_bounds, mlp_w_up=mlp_w_up, mlp_w_down=mlp_w_down)
    m = dict(mix_norm=m_mix_norm, mlp_norm=m_mlp_norm, final_norm=m_final_norm, attn_w_qkv=m_attn_w_qkv,
             attn_b_qkv=m_attn_b_qkv, attn_sinks=m_attn_sinks, attn_w_o=m_attn_w_o, hgrn_w_in=m_hgrn_w_in,
             hgrn_g_norm=m_hgrn_g_norm, hgrn_w_o=m_hgrn_w_o, hgrn_lower_bounds=m_hgrn_lower_bounds, mlp_w_up=m_mlp_w_up,
             mlp_w_down=m_mlp_w_down)
    v = dict(mix_norm=v_mix_norm, mlp_norm=v_mlp_norm, final_norm=v_final_norm, attn_w_qkv=v_attn_w_qkv,
             attn_b_qkv=v_attn_b_qkv, attn_sinks=v_attn_sinks, attn_w_o=v_attn_w_o, hgrn_w_in=v_hgrn_w_in,
             hgrn_g_norm=v_hgrn_g_norm, hgrn_w_o=v_hgrn_w_o, hgrn_lower_bounds=v_hgrn_lower_bounds, mlp_w_up=v_mlp_w_up,
             mlp_w_down=v_mlp_w_down)
    me = 4 * lax.axis_index("x") + 2 * lax.axis_index("y") + lax.axis_index("c")

    gn = hgrn_g_norm.reshape(1, 128)
    gn_a = gn.astype(BF16)
    gn_b = (gn - gn_a.astype(F32)).astype(BF16)
    gn_c = (gn - gn_a.astype(F32) - gn_b.astype(F32)).astype(BF16)
    gn_rows = jnp.pad(jnp.concatenate([gn_a, gn_b, gn_c], axis=1), ((0, 15), (0, D_MODEL - 3 * 128)))
    full = {}
    got = _all_gather("gather_attn_weights", _shards_bf16(w, GATHER_FIRST) + [gn_rows])
    _gathered(got[:1], GATHER_FIRST, full)
    w_qkv = full["attn_w_qkv", 0].transpose(1, 0, 2).reshape(D_MODEL, QKV_DIM)
    gn_terms = got[1][:, 0, :3 * 128].astype(F32).reshape(N_DEV, 3, 128)
    gn_full = ((gn_terms[:, 0] + gn_terms[:, 1]) + gn_terms[:, 2]).reshape(1, D_MODEL)

    x0 = x[0]
    tgt = loss_target[0]
    rot = _rotary_tables(positions)
    row = lambda a: a.reshape(1, -1)

    qkv, h0 = _norm_mm("qkv_proj", x0, row(mix_norm[0]), w_qkv, attn_b_qkv, rot=rot)
    att, *got = _attn_fwd(qkv, attn_sinks, carry=(_Gather, _shards_bf16(w, GATHER_ATTN)))
    _gathered(got, GATHER_ATTN, full)
    x1 = _mm_res("attn_out_proj", att, full["attn_w_o", 0], x0)
    u0, h1, *got = _norm_mm("mlp0_up", x1, row(mlp_norm[0]), full["mlp_w_up", 0],
                            carry=(_Gather, _shards_bf16(w, GATHER_MLP0)))
    _gathered(got, GATHER_MLP0, full)
    x2, a0 = _mlp_down("mlp0_down", u0, full["mlp_w_down", 0], x1)
    z, h2 = _norm_mm("hgrn_in_proj", x2, row(mix_norm[1]), full["hgrn_w_in", 0])
    o_raw, states, *got = _hgrn_fwd(z, hgrn_lower_bounds, carry=(_Gather, _shards_bf16(w, GATHER_HGRN)))
    _gathered(got, GATHER_HGRN, full)
    x3, o2 = _hgrn_out("hgrn_out_proj", o_raw, z, gn_full, full["hgrn_w_o", 0], x2)
    u1, h3 = _norm_mm("mlp1_up", x3, row(mlp_norm[1]), full["mlp_w_up", 1])
    dx4, a1, loss_part, g_final = _mlp_down("mlp1_down_loss", u1, full["mlp_w_down", 1], x3,
                                            loss_head=(tgt, row(final_norm)))

    gw = {}
    du1, = _mlp_bwd_act("mlp1_bwd_act", dx4, u1, full["mlp_w_down", 1])
    dx3, g_mlp1 = _mm_nt_rmsbwd("mlp1_bwd_in", du1, full["mlp_w_up", 1], x3, row(mlp_norm[1]), dx4)
    gw["mlp_w_down", 1] = _mm_tn("mlp1_dw_down", a1, dx4, "rows")
    gw["mlp_w_up", 1] = _mm_tn("mlp1_dw_up", h3, du1, "cols")

    do_raw, dg, g_gn = _hgrn_out_bwd("hgrn_out_bwd", dx3, o_raw, z, full["hgrn_w_o", 0], gn_full)
    gw["hgrn_w_o", 0] = _mm_tn("hgrn_dw_o", o2, dx3, "rows")
    recvs = {}
    dzq, dzf, dzi, g_lb, *recv = _hgrn_bwd(z, hgrn_lower_bounds, states, do_raw,
                                           carry=(_Exchange, [gw[p] for p in GRADS_HGRN]))
    recvs.update(zip(GRADS_HGRN, recv))
    dz = [dzq, dzf, dzi, dg]
    dx2, g_mix1 = _mm_nt_rmsbwd("hgrn_in_bwd", dz, full["hgrn_w_in", 0], x2, row(mix_norm[1]), dx3)
    gw["hgrn_w_in", 0] = jnp.concatenate(
        [_mm_tn(f"hgrn_dw_in{j}", h2, d, "cols") for j, d in enumerate(dz)], axis=0)

    du0, *recv = _mlp_bwd_act("mlp0_bwd_act", dx2, u0, full["mlp_w_down", 0],
                              carry=(_Exchange, [gw[p] for p in GRADS_MLP0]))
    recvs.update(zip(GRADS_MLP0, recv))
    dx1, g_mlp0 = _mm_nt_rmsbwd("mlp0_bwd_in", du0, full["mlp_w_up", 0], x1, row(mlp_norm[0]), dx2)
    gw["mlp_w_down", 0] = _mm_tn("mlp0_dw_down", a0, dx2, "rows")
    gw["mlp_w_up", 0] = _mm_tn("mlp0_dw_up", h1, du0, "cols")

    datt = _mm_nt("attn_out_bwd", dx1, full["attn_w_o", 0], BF16)
    gw["attn_w_o", 0] = _mm_tn("attn_dw_o", att, dx1, "rows")
    dqkv, g_sink, *recv = _attn_bwd(qkv, rot, attn_sinks, datt, carry=(_Exchange, [gw[p] for p in GRADS_ATTN]))
    recvs.update(zip(GRADS_ATTN, recv))
    g_qkv = _mm_tn("attn_dw_qkv", h0, dqkv, bn=512)
    g_qkv = g_qkv.reshape(D_MODEL, N_DEV, QKV_DIM // N_DEV).transpose(1, 0, 2).astype(BF16)
    dx0, g_mix0, g_bqkv, recvs["attn_w_qkv", 0] = _mm_nt_rmsbwd(
        "qkv_bwd", dqkv, w_qkv, x0, row(mix_norm[0]), dx1, with_colsum=True, carry=(_Exchange, [g_qkv]))

    big = {name: _adamw_sum("adamw_" + name, [recvs[name, l] for l in range(w[name].shape[0])], w[name], m[name], v[name])
           for name in BIG_NAMES}

    zero_row = jnp.zeros((1, D_MODEL), F32)
    part = _pack_small(dict(
        mix_norm=jnp.concatenate([g_mix0, g_mix1], axis=0), mlp_norm=jnp.concatenate([g_mlp0, g_mlp1], axis=0),
        final_norm=g_final, attn_b_qkv=g_bqkv, attn_sinks=g_sink[:, :N_Q_HEADS],
        hgrn_lower_bounds=jnp.concatenate([zero_row, g_lb], axis=0)), g_gn, loss=loss_part)

    def spread(a):
        return lax.dynamic_update_slice(zero_row, a.reshape(1, 128), (0, me * 128))

    small_in = [_pack_small({n: d[n] for n in SMALL_NAMES if n != "hgrn_g_norm"}, spread(d["hgrn_g_norm"]))
                for d in (w, m, v)]
    synced = _small_sync(part, *small_in)
    small = [_unpack_small(p, me) for p in synced]

    outs = [synced[0][LOSS_ROW, 0], dx0.reshape(x.shape)]
    for kind, grp_small in enumerate(small):
        for name in WEIGHT_NAMES:
            val = grp_small[name] if name in SMALL_NAMES else big[name][kind]
            outs.append(val.reshape(w[name].shape))
    return tuple(outs)
```

```python
import functools

import jax
import jax.numpy as jnp
from jax import lax
from jax.experimental import pallas as pl
from jax.experimental.pallas import tpu as pltpu

F32 = jnp.float32
BF16 = jnp.bfloat16

D_MODEL = 1024
HEAD_DIM = 64
N_Q_HEADS = 16
Q_DIM = 1024
KV_DIM = 256
QKV_DIM = 1536
ATT_BLOCK = 128
ROT_HALF = 8
ROPE_THETA = 500000.0
NEG_INF = -1e30
HGRN_HEADS = 8
HGRN_DK = 128
CHUNK = 64
D_FF = 4096
NORM_EPS = 1e-5
N_DEV = 8

ADAM_LR = 0.001
ADAM_B1 = 0.9
ADAM_B2 = 0.999
ADAM_EPS = 1e-08
ADAM_WD = 0.01
ADAM_STEP = 10

LANES = 128
VMEM_LIMIT = 56 * 1024 * 1024

GATHER_FIRST = (("attn_w_qkv", 0),)
GATHER_ATTN = (("attn_w_o", 0), ("mlp_w_up", 0), ("mlp_w_down", 0))
GATHER_MLP0 = (("hgrn_w_in", 0), ("hgrn_w_o", 0))
GATHER_HGRN = (("mlp_w_up", 1), ("mlp_w_down", 1))
GRADS_HGRN = (("mlp_w_down", 1), ("mlp_w_up", 1), ("hgrn_w_o", 0))
GRADS_MLP0 = (("hgrn_w_in", 0),)
GRADS_ATTN = (("mlp_w_down", 0), ("mlp_w_up", 0), ("attn_w_o", 0))
COL_SHARDED = ("attn_w_qkv", "hgrn_w_in", "mlp_w_up")
BIG_NAMES = ("attn_w_qkv", "attn_w_o", "hgrn_w_in", "hgrn_w_o", "mlp_w_up", "mlp_w_down")
SMALL_ROWS = 16


def _dot(a, b):
    return jnp.dot(a, b, preferred_element_type=F32)


def _dot_nt(a, b):
    return lax.dot_general(a, b, (((1,), (1,)), ((), ())), preferred_element_type=F32)


def _dot_tn(a, b):
    return lax.dot_general(a, b, (((0,), (0,)), ((), ())), preferred_element_type=F32)


def _params(**kw):
    return pltpu.CompilerParams(vmem_limit_bytes=VMEM_LIMIT, **kw)


def _full_spec(a):
    nd = a.ndim
    return pl.BlockSpec(a.shape, lambda *_: (0,) * nd)


def _row_call(name, body, n_rows, tm, row_ins, full_ins, row_outs, acc_outs=(), carry=(None, None)):
    steps = n_rows // tm
    in_specs = [pl.BlockSpec((tm, w), functools.partial(lambda i, cb: (i, cb), cb=cb)) for _, w, cb in row_ins]
    in_specs += [_full_spec(a) for a in full_ins]
    out_shape = [jax.ShapeDtypeStruct((n_rows, w), dt) for w, dt in row_outs]
    out_specs = [pl.BlockSpec((tm, w), lambda i: (i, 0)) for w, _ in row_outs]
    for shp, dt in acc_outs:
        out_shape.append(jax.ShapeDtypeStruct(shp, dt))
        out_specs.append(pl.BlockSpec(shp, functools.partial(lambda i, nd: (0,) * nd, nd=len(shp))))
    n_in, n_out = len(in_specs), len(out_specs)
    in_specs, out_specs, out_shape, scratch, extra = _carried_specs(carry, in_specs, out_specs, out_shape, [])

    def wrapped(*refs):
        i = pl.program_id(0)
        own, finish = _carried(carry, refs, n_in, n_out, i == 0, i == steps - 1)
        body(*own)
        finish()

    return pl.pallas_call(
        wrapped, name=name, grid=(steps,), in_specs=in_specs, out_specs=out_specs, out_shape=out_shape,
        scratch_shapes=scratch, compiler_params=_params(dimension_semantics=("arbitrary",)),
    )(*[a for a, _, _ in row_ins], *full_ins, *extra)


def _rms(x, gain):
    r = lax.rsqrt(jnp.mean(x * x, axis=-1, keepdims=True) + NORM_EPS)
    xhat = x * r
    return xhat * gain, xhat, r


def _rms_bwd(dy, xhat, r, gain):
    dxhat = dy * gain
    dx = r * (dxhat - xhat * jnp.mean(dxhat * xhat, axis=-1, keepdims=True))
    return dx, dy * xhat


def _norm_mm(name, x, gain, w, bias=None, rot=None, tm=512, carry=(None, None)):
    T = x.shape[0]
    tm = min(tm, T)
    nc = 512
    blocked = w.ndim == 3
    n = N_DEV * w.shape[2] if blocked else w.shape[1]
    assert n % nc == 0 and (not blocked or w.shape[2] == nc)

    def body(*refs):
        x_ref, refs = refs[0], refs[1:]
        if rot is not None:
            t_ref, refs = refs[0], refs[1:]
        g_ref, w_ref, refs = refs[0], refs[1], refs[2:]
        if bias is not None:
            b_ref, refs = refs[0], refs[1:]
        y_ref, h_ref = refs
        h, _, _ = _rms(x_ref[...], g_ref[...])
        hb = h.astype(BF16)
        h_ref[...] = hb
        for c in range(n // nc):
            sl = slice(c * nc, (c + 1) * nc)
            y = _dot(hb, w_ref[c] if blocked else w_ref[:, sl])
            if bias is not None:
                y = y + b_ref[:, sl]
            if rot is None:
                y_ref[:, sl] = y
            else:
                n_rot = max(0, min(nc, Q_DIM + KV_DIM - c * nc)) // LANES
                pieces = _rot_fwd(y[:, :n_rot * LANES], t_ref[...]) if n_rot else []
                for j in range(nc // LANES):
                    col = slice(c * nc + j * LANES, c * nc + (j + 1) * LANES)
                    y_ref[:, col] = pieces[j] if j < n_rot else y[:, j * LANES:(j + 1) * LANES]

    rows = [(x, D_MODEL, 0)] + ([(rot, 3 * LANES, 0)] if rot is not None else [])
    full = [gain, w] + ([bias] if bias is not None else [])
    return _row_call(name, body, T, tm, rows, full, [(n, F32), (D_MODEL, BF16)], carry=carry)


def _mm_res(name, a, w, res, tm=512):
    T = a.shape[0]
    tm = min(tm, T)

    def body(a_ref, r_ref, w_ref, o_ref):
        o_ref[...] = r_ref[...] + _dot(a_ref[...], w_ref[...])

    return _row_call(name, body, T, tm, [(a, a.shape[1], 0), (res, D_MODEL, 0)], [w], [(D_MODEL, F32)])[0]


def _mlp_down(name, u, w, res, tm=512, loss_head=None):
    T = u.shape[0]
    tm = min(tm, T)
    kc = 1024
    sub = min(256, tm)

    def body(*refs):
        if loss_head is None:
            u_ref, r_ref, w_ref, o_ref, a_ref = refs
        else:
            u_ref, r_ref, t_ref, w_ref, g_ref, o_ref, a_ref, loss_ref, dg_ref = refs

            @pl.when(pl.program_id(0) == 0)
            def _():
                loss_ref[...] = jnp.zeros_like(loss_ref)
                dg_ref[...] = jnp.zeros_like(dg_ref)

        for r0 in range(0, tm, sub):
            rs = slice(r0, r0 + sub)
            acc = r_ref[rs, :]
            for c in range(D_FF // kc):
                sl = slice(c * kc, (c + 1) * kc)
                a = jnp.maximum(u_ref[rs, sl], 0.0)
                ab = (a * a).astype(BF16)
                a_ref[rs, sl] = ab
                acc = acc + _dot(ab, w_ref[sl, :])
            if loss_head is None:
                o_ref[rs, :] = acc
            else:
                gain_v = g_ref[...]
                y, xhat, r = _rms(acc, gain_v)
                diff = y - t_ref[rs, :]
                per_row = jnp.sum(diff * diff, axis=-1, keepdims=True) * (1.0 / D_MODEL)
                loss_ref[...] += jnp.broadcast_to(0.5 * jnp.sum(per_row, axis=0, keepdims=True), loss_ref.shape)
                dx, dgr = _rms_bwd(diff * (1.0 / D_MODEL), xhat, r, gain_v)
                o_ref[rs, :] = dx
                dg_ref[...] += jnp.sum(dgr, axis=0, keepdims=True)

    rows, full, acc_outs = [(u, D_FF, 0), (res, D_MODEL, 0)], [w], []
    if loss_head is not None:
        rows, full = rows + [(loss_head[0], D_MODEL, 0)], full + [loss_head[1]]
        acc_outs = [((1, LANES), F32), ((1, D_MODEL), F32)]
    return _row_call(name, body, T, tm, rows, full, [(D_MODEL, F32), (D_FF, BF16)], acc_outs)


def _hgrn_out(name, o_raw, z, gn, w, res, tm=512):
    T = o_raw.shape[0]
    tm = min(tm, T)

    def body(o_ref, g_ref, r_ref, gn_ref, w_ref, x_ref, a_ref):
        y, _, _ = _rms(o_ref[...], gn_ref[...])
        g = g_ref[...]
        a = (y * (g * jax.nn.sigmoid(g))).astype(BF16)
        a_ref[...] = a
        x_ref[...] = r_ref[...] + _dot(a, w_ref[...])

    return _row_call(name, body, T, tm, [(o_raw, D_MODEL, 0), (z, D_MODEL, 3), (res, D_MODEL, 0)], [gn, w],
                     [(D_MODEL, F32), (D_MODEL, BF16)])


def _mm_nt_rmsbwd(name, dy, w, x, gain, dres, tm=512, with_colsum=False, carry=(None, None)):
    T = x.shape[0]
    tm = min(tm, T)
    dys = list(dy) if isinstance(dy, (list, tuple)) else [dy]
    width = dys[0].shape[1]
    n = width * len(dys)
    sub = min(256, tm)
    assert not with_colsum or len(dys) == 1

    def body(*refs):
        dy_refs, refs = refs[:len(dys)], refs[len(dys):]
        if with_colsum:
            x_ref, dr_ref, w_ref, g_ref, dx_ref, dg_ref, cs_ref = refs
        else:
            x_ref, dr_ref, w_ref, g_ref, dx_ref, dg_ref = refs

        @pl.when(pl.program_id(0) == 0)
        def _():
            dg_ref[...] = jnp.zeros_like(dg_ref)
            if with_colsum:
                cs_ref[...] = jnp.zeros_like(cs_ref)

        gain_v = g_ref[...]
        for r0 in range(0, tm, sub):
            rs = slice(r0, r0 + sub)
            if w.ndim == 3:
                nb = w.shape[2]
                dh = None
                for p in range(N_DEV):
                    piece, off = divmod(p * nb, width)
                    part = _dot_nt(dy_refs[piece][rs, off:off + nb].astype(BF16), w_ref[p])
                    dh = part if dh is None else dh + part
            else:
                dh = _dot_nt(dy_refs[0][rs, :].astype(BF16), w_ref[...])
            _, xhat, r = _rms(x_ref[rs, :], gain_v)
            dx, dgr = _rms_bwd(dh, xhat, r, gain_v)
            dx_ref[rs, :] = dr_ref[rs, :] + dx
            dg_ref[...] += jnp.sum(dgr, axis=0, keepdims=True)
            if with_colsum:
                cs_ref[...] += jnp.sum(dy_refs[0][rs, :].astype(F32), axis=0, keepdims=True)

    acc = [((1, D_MODEL), F32)] + ([((1, n), F32)] if with_colsum else [])
    rows = [(d, width, 0) for d in dys] + [(x, D_MODEL, 0), (dres, D_MODEL, 0)]
    return _row_call(name, body, T, tm, rows, [w, gain], [(D_MODEL, F32)], acc, carry=carry)


def _mm_nt(name, dy, w, out_dtype, tm=512):
    T = dy.shape[0]
    tm = min(tm, T)
    k = w.shape[0]

    def body(dy_ref, w_ref, o_ref):
        o_ref[...] = _dot_nt(dy_ref[...].astype(BF16), w_ref[...]).astype(out_dtype)

    return _row_call(name, body, T, tm, [(dy, dy.shape[1], 0)], [w], [(k, out_dtype)])[0]


def _mlp_bwd_act(name, dy, u, w_down, tm=512, carry=(None, None)):
    T = u.shape[0]
    tm = min(tm, T)
    kc = 1024

    def body(dy_ref, u_ref, w_ref, du_ref):
        dyb = dy_ref[...].astype(BF16)
        for c in range(D_FF // kc):
            sl = slice(c * kc, (c + 1) * kc)
            da = _dot_nt(dyb, w_ref[sl, :])
            du_ref[:, sl] = (da * (2.0 * jnp.maximum(u_ref[:, sl], 0.0))).astype(BF16)

    return _row_call(name, body, T, tm, [(dy, D_MODEL, 0), (u, D_FF, 0)], [w_down], [(D_FF, BF16)], carry=carry)


def _hgrn_out_bwd(name, dx, o_raw, z, w, gn, tm=512):
    T = dx.shape[0]
    tm = min(tm, T)

    def body(dx_ref, o_ref, g_ref, w_ref, gn_ref, do_ref, dg_ref, dgn_ref):
        @pl.when(pl.program_id(0) == 0)
        def _():
            dgn_ref[...] = jnp.zeros_like(dgn_ref)

        da = _dot_nt(dx_ref[...].astype(BF16), w_ref[...])
        gn_v = gn_ref[...]
        y, xhat, r = _rms(o_ref[...], gn_v)
        g = g_ref[...]
        sg = jax.nn.sigmoid(g)
        dg_ref[...] = (da * y * (sg * (1.0 + g * (1.0 - sg)))).astype(BF16)
        dyn = da * (g * sg)
        do, dgr = _rms_bwd(dyn, xhat, r, gn_v)
        do_ref[...] = do
        dgn_ref[...] += jnp.sum(dgr, axis=0, keepdims=True)

    return _row_call(name, body, T, tm, [(dx, D_MODEL, 0), (o_raw, D_MODEL, 0), (z, D_MODEL, 3)], [w, gn],
                     [(D_MODEL, F32), (D_MODEL, BF16)], [((1, D_MODEL), F32)])


COL_BLOCK = D_FF // N_DEV


def _mm_tn(name, a, b, shard=None, bm=1024, bn=1024, tk=2048):
    T, M = a.shape
    N = b.shape[1]
    bm, bn, tk = min(bm, M), min(bn, N), min(tk, T)
    nk = T // tk
    if shard is None:
        out_shape, out_block = jax.ShapeDtypeStruct((M, N), F32), (bm, bn)
        out_map = lambda i, j, k: (i, j)
    elif shard == "cols":
        assert bn % COL_BLOCK == 0 and N % bn == 0
        out_shape = jax.ShapeDtypeStruct((N // COL_BLOCK, M, COL_BLOCK), BF16)
        out_block = (bn // COL_BLOCK, bm, COL_BLOCK)
        out_map = lambda i, j, k: (j, i, 0)
    else:
        rows = M // N_DEV
        assert bm % rows == 0
        out_shape, out_block = jax.ShapeDtypeStruct((N_DEV, rows, N), BF16), (bm // rows, rows, bn)
        out_map = lambda i, j, k: (i, 0, j)

    def body(a_ref, b_ref, o_ref, acc):
        k = pl.program_id(2)

        @pl.when(k == 0)
        def _():
            acc[...] = jnp.zeros_like(acc)

        acc[...] += _dot_tn(a_ref[...].astype(BF16), b_ref[...].astype(BF16))

        @pl.when(k == nk - 1)
        def _():
            if shard == "cols":
                for c in range(bn // COL_BLOCK):
                    o_ref[c] = acc[:, c * COL_BLOCK:(c + 1) * COL_BLOCK].astype(BF16)
            else:
                o_ref[...] = acc[...].reshape(out_block).astype(o_ref.dtype)

    return pl.pallas_call(
        body, name=name, grid=(M // bm, N // bn, nk),
        in_specs=[pl.BlockSpec((tk, bm), lambda i, j, k: (k, i)), pl.BlockSpec((tk, bn), lambda i, j, k: (k, j))],
        out_specs=pl.BlockSpec(out_block, out_map), out_shape=out_shape,
        scratch_shapes=[pltpu.VMEM((bm, bn), F32)],
        compiler_params=_params(dimension_semantics=("parallel", "parallel", "arbitrary")),
    )(a, b)


def _rot_fwd(x, tab):
    c, sa, sb = tab[:, :LANES], tab[:, LANES:2 * LANES], tab[:, 2 * LANES:]
    outs = []
    for j in range(x.shape[1] // LANES):
        xs = x[:, j * LANES:(j + 1) * LANES]
        outs.append(xs * c + pltpu.roll(xs, ROT_HALF, 1) * sa + pltpu.roll(xs, LANES - ROT_HALF, 1) * sb)
    return outs


def _rot_bwd(dys, tab):
    c, sa, sb = tab[:, :LANES], tab[:, LANES:2 * LANES], tab[:, 2 * LANES:]
    return [dy * c + pltpu.roll(dy * sa, LANES - ROT_HALF, 1) + pltpu.roll(dy * sb, ROT_HALF, 1) for dy in dys]


ATT_SCALE = HEAD_DIM ** -0.5
HEAD_LAG = 2


def _attn_masks(n):
    kj = lax.broadcasted_iota(jnp.int32, (2 * ATT_BLOCK, ATT_BLOCK), 0)
    qi = lax.broadcasted_iota(jnp.int32, (2 * ATT_BLOCK, ATT_BLOCK), 1)
    delta = qi + ATT_BLOCK - kj
    first_key = jnp.where(n > 0, 0, ATT_BLOCK)
    valid = (delta >= 0) & (delta < ATT_BLOCK) & (kj >= first_key)
    low = lax.broadcasted_iota(jnp.int32, (1, LANES), 1) < HEAD_DIM
    upper = lax.broadcasted_iota(jnp.int32, (LANES, 1), 0) < HEAD_DIM
    return valid, low, upper


def _softmax_sink(s, valid, sink):
    s = jnp.where(valid, s, NEG_INF)
    m = jnp.maximum(jnp.max(s, axis=0, keepdims=True), sink)
    e = jnp.exp(s - m)
    es = jnp.exp(sink - m)
    inv = 1.0 / (jnp.sum(e, axis=0, keepdims=True) + es)
    return e * inv, es * inv


def _attn_specs(nb, tables):
    prev = lambda n: jnp.maximum(jnp.minimum(n, nb - 1) - 1, 0)
    cur = lambda n: jnp.minimum(n, nb - 1)
    specs = [
        pl.BlockSpec((ATT_BLOCK, Q_DIM), lambda n: (cur(n), 0)),
        pl.BlockSpec((ATT_BLOCK, KV_DIM), lambda n: (prev(n), 4)),
        pl.BlockSpec((ATT_BLOCK, KV_DIM), lambda n: (cur(n), 4)),
        pl.BlockSpec((ATT_BLOCK, KV_DIM), lambda n: (prev(n), 5)),
        pl.BlockSpec((ATT_BLOCK, KV_DIM), lambda n: (cur(n), 5)),
    ]
    if tables:
        specs += [pl.BlockSpec((ATT_BLOCK, 3 * LANES), lambda n: (prev(n), 0)),
                  pl.BlockSpec((ATT_BLOCK, 3 * LANES), lambda n: (cur(n), 0))]
    return specs + [pl.BlockSpec(memory_space=pltpu.SMEM)]


def _kv_band(prev_ref, cur_ref):
    out = []
    for j in range(KV_DIM // LANES):
        sl = slice(j * LANES, (j + 1) * LANES)
        band = jnp.concatenate([prev_ref[:, sl], cur_ref[:, sl]], axis=0)
        out.append((band, pltpu.roll(band, HEAD_DIM, 1)))
    return out


def _bf16(bands, transposed=False):
    return [[(a.T if transposed else a).astype(BF16) for a in pair] for pair in bands]


def _attn_fwd(qkv, sinks, carry=(None, None)):
    T = qkv.shape[0]
    nb = T // ATT_BLOCK

    def body(*refs):
        n = pl.program_id(0)
        own, finish = _carried(carry, refs, 6, 1, n == 0, n == nb - 1)
        q_ref, kp_ref, kc_ref, vp_ref, vc_ref, sink_ref, o_ref = own
        valid, low, upper = _attn_masks(n)
        ks = _bf16(_kv_band(kp_ref, kc_ref))
        vts = _bf16(_kv_band(vp_ref, vc_ref), transposed=True)
        heads, outs = {}, {}

        def first(h):
            p, hf = h // 2, h % 2
            kpair, khalf = p // 4, (p // 2) % 2
            qm = jnp.where(low if hf == 0 else ~low, q_ref[:, p * LANES:(p + 1) * LANES] * ATT_SCALE, 0.0)
            sw = 0 if khalf == hf else 1
            heads[h] = (kpair, sw, _dot_nt(ks[kpair][sw], qm.astype(BF16)))

        def second(h):
            kpair, sw, s = heads[h]
            heads[h] = (kpair, sw, _softmax_sink(s, valid, sink_ref[0, h])[0].astype(BF16))

        def third(h):
            kpair, sw, pr = heads.pop(h)
            outs[h] = _dot(vts[kpair][sw], pr)
            if h % 2:
                o_ref[:, (h // 2) * LANES:(h // 2 + 1) * LANES] = jnp.where(upper, outs.pop(h - 1), outs.pop(h)).T.astype(BF16)

        for i in range(N_Q_HEADS + 2 * HEAD_LAG):
            if i < N_Q_HEADS:
                first(i)
            if 0 <= i - HEAD_LAG < N_Q_HEADS:
                second(i - HEAD_LAG)
            if 0 <= i - 2 * HEAD_LAG < N_Q_HEADS:
                third(i - 2 * HEAD_LAG)
        finish()

    in_specs, out_specs, out_shape, scratch, extra = _carried_specs(
        carry, _attn_specs(nb, False), [pl.BlockSpec((ATT_BLOCK, Q_DIM), lambda n: (n, 0))],
        [jax.ShapeDtypeStruct((T, Q_DIM), BF16)], [])
    return pl.pallas_call(
        body, name="attn_fwd", grid=(nb,), in_specs=in_specs, out_specs=out_specs, out_shape=out_shape,
        scratch_shapes=scratch, compiler_params=_params(dimension_semantics=("arbitrary",)),
    )(qkv, qkv, qkv, qkv, qkv, sinks, *extra)


def _attn_bwd(qkv, rot, sinks, dout, carry=(None, None)):
    T = qkv.shape[0]
    nb = T // ATT_BLOCK
    npair = KV_DIM // LANES

    def body(*refs):
        n = pl.program_id(0)
        own, finish = _carried(carry, refs, 9, 2, n == 0, n == nb)
        (q_ref, kp_ref, kc_ref, vp_ref, vc_ref, tp_ref, tc_ref, sink_ref, do_ref, dqkv_ref, dsink_ref,
         dq_c, dk_c, dv_c) = own

        @pl.when(n == 0)
        def _():
            dq_c[...] = jnp.zeros_like(dq_c)
            dk_c[...] = jnp.zeros_like(dk_c)
            dv_c[...] = jnp.zeros_like(dv_c)
            dsink_ref[...] = jnp.zeros_like(dsink_ref)

        def flush(dk_prev, dv_prev, tab_ref):
            dqkv_ref[:, :Q_DIM] = dq_c[...].astype(BF16)
            dk = _rot_bwd([dk_c[:, j * LANES:(j + 1) * LANES] + dk_prev[j] for j in range(npair)], tab_ref[...])
            for j in range(npair):
                dqkv_ref[:, Q_DIM + j * LANES:Q_DIM + (j + 1) * LANES] = dk[j].astype(BF16)
                dqkv_ref[:, Q_DIM + KV_DIM + j * LANES:Q_DIM + KV_DIM + (j + 1) * LANES] = (
                    dv_c[:, j * LANES:(j + 1) * LANES] + dv_prev[j]).astype(BF16)

        @pl.when(n < nb)
        def _():
            valid, low, upper = _attn_masks(n)
            lane = lax.broadcasted_iota(jnp.int32, (1, LANES), 1)
            k_band = _kv_band(kp_ref, kc_ref)
            ks, kts = _bf16(k_band), _bf16(k_band, transposed=True)
            vs = _bf16(_kv_band(vp_ref, vc_ref))
            dk_acc = [[jnp.zeros((2 * ATT_BLOCK, LANES), F32) for _ in range(2)] for _ in range(npair)]
            dv_acc = [[jnp.zeros((2 * ATT_BLOCK, LANES), F32) for _ in range(2)] for _ in range(npair)]
            dsink = jnp.zeros((1, LANES), F32)
            heads, dq_t, dsinks = {}, {}, []

            def first(h):
                p, hf = h // 2, h % 2
                kpair, khalf = p // 4, (p // 2) % 2
                sel = low if hf == 0 else ~low
                qm = jnp.where(sel, q_ref[:, p * LANES:(p + 1) * LANES] * ATT_SCALE, 0.0).astype(BF16)
                dom = jnp.where(sel, do_ref[:, p * LANES:(p + 1) * LANES], 0.0).astype(BF16)
                sw = 0 if khalf == hf else 1
                heads[h] = dict(kpair=kpair, sw=sw, qm=qm, dom=dom, s=_dot_nt(ks[kpair][sw], qm),
                                dp=_dot_nt(vs[kpair][sw], dom))

            def second(h):
                d = heads[h]
                pr, ps = _softmax_sink(d.pop("s"), valid, sink_ref[0, h])
                dp = d.pop("dp")
                dd = jnp.sum(pr * dp, axis=0, keepdims=True)
                dsinks.append(jnp.where(lane == h, -jnp.sum(ps * dd, axis=1, keepdims=True), 0.0))
                d["ds"] = (pr * (dp - dd)).astype(BF16)
                d["pr"] = pr.astype(BF16)

            def third(h):
                d = heads.pop(h)
                kpair, sw = d["kpair"], d["sw"]
                dq_t[h] = _dot(kts[kpair][sw], d["ds"])
                dk_acc[kpair][sw] = dk_acc[kpair][sw] + _dot(d["ds"], d["qm"])
                dv_acc[kpair][sw] = dv_acc[kpair][sw] + _dot(d["pr"], d["dom"])

            for i in range(N_Q_HEADS + 2 * HEAD_LAG):
                if i < N_Q_HEADS:
                    first(i)
                if 0 <= i - HEAD_LAG < N_Q_HEADS:
                    second(i - HEAD_LAG)
                if 0 <= i - 2 * HEAD_LAG < N_Q_HEADS:
                    third(i - 2 * HEAD_LAG)
            dsink = sum(dsinks, dsink)
            dqs = [jnp.where(upper, dq_t[2 * p], dq_t[2 * p + 1]).T * ATT_SCALE for p in range(Q_DIM // LANES)]
            dk_acc = [a[0] + pltpu.roll(a[1], HEAD_DIM, 1) for a in dk_acc]
            dv_acc = [a[0] + pltpu.roll(a[1], HEAD_DIM, 1) for a in dv_acc]
            flush([a[:ATT_BLOCK] for a in dk_acc], [a[:ATT_BLOCK] for a in dv_acc], tp_ref)
            dq = _rot_bwd(dqs, tc_ref[...])
            for p in range(Q_DIM // LANES):
                dq_c[:, p * LANES:(p + 1) * LANES] = dq[p]
            for j in range(npair):
                dk_c[:, j * LANES:(j + 1) * LANES] = dk_acc[j][ATT_BLOCK:]
                dv_c[:, j * LANES:(j + 1) * LANES] = dv_acc[j][ATT_BLOCK:]
            dsink_ref[...] += dsink

        @pl.when(n == nb)
        def _():
            zero = [jnp.zeros((ATT_BLOCK, LANES), F32) for _ in range(npair)]
            flush(zero, zero, tc_ref)

        finish()

    do_spec = pl.BlockSpec((ATT_BLOCK, Q_DIM), lambda n: (jnp.minimum(n, nb - 1), 0))
    in_specs, out_specs, out_shape, scratch, extra = _carried_specs(
        carry, _attn_specs(nb, True) + [do_spec],
        [pl.BlockSpec((ATT_BLOCK, QKV_DIM), lambda n: (jnp.maximum(n - 1, 0), 0)),
         pl.BlockSpec((1, LANES), lambda n: (0, 0))],
        [jax.ShapeDtypeStruct((T, QKV_DIM), BF16), jax.ShapeDtypeStruct((1, LANES), F32)],
        [pltpu.VMEM((ATT_BLOCK, Q_DIM), F32), pltpu.VMEM((ATT_BLOCK, KV_DIM), F32),
         pltpu.VMEM((ATT_BLOCK, KV_DIM), F32)])
    return pl.pallas_call(
        body, name="attn_bwd", grid=(nb + 1,), in_specs=in_specs, out_specs=out_specs, out_shape=out_shape,
        scratch_shapes=scratch, compiler_params=_params(dimension_semantics=("arbitrary",)),
    )(qkv, qkv, qkv, qkv, qkv, rot, rot, sinks, dout, *extra)


LEVELS = (32, 16, 8, 4, 2, 1)
SUBLANES = 8
UNROLL = 16
UNROLL_BWD = 8


def _lower_bound(lb_ref):
    l0, l1 = lb_ref[0:1, :], lb_ref[1:2, :]
    mx = jnp.maximum(l0, l1)
    e0, e1 = jnp.exp(l0 - mx), jnp.exp(l1 - mx)
    return e1 / (e0 + e1)


GROUPS = CHUNK // SUBLANES


def _group_roll(x, k):
    return pltpu.roll(x.reshape(GROUPS, SUBLANES, HGRN_DK), k % SUBLANES, 1).reshape(CHUNK, HGRN_DK)


def _scan_rows(x, row, reverse):
    r8 = row & (SUBLANES - 1)
    for sh in (1, 2, 4):
        ok = (r8 < SUBLANES - sh) if reverse else (r8 >= sh)
        x = x + jnp.where(ok, _group_roll(x, -sh if reverse else sh), 0.0)
    g = x.reshape(GROUPS, SUBLANES, HGRN_DK)
    edge = 0 if reverse else SUBLANES - 1
    tot = jnp.broadcast_to(g[:, edge:edge + 1, :], g.shape)

    def shifted(a, n):
        z = jnp.zeros((n, SUBLANES, HGRN_DK), F32)
        return jnp.concatenate([a[n:], z] if reverse else [z, a[:GROUPS - n]], axis=0)

    acc = shifted(tot, 1)
    for sh in (1, 2, 4):
        acc = acc + shifted(acc, sh)
    return (g + acc).reshape(CHUNK, HGRN_DK)


def _level_masks():
    t = lax.broadcasted_iota(jnp.int32, (CHUNK, CHUNK), 0)
    s = lax.broadcasted_iota(jnp.int32, (CHUNK, CHUNK), 1)
    return [((t & h) != 0) & ((s & h) == 0) & ((t ^ s) < 2 * h) for h in LEVELS]


def _level_scales(b, forget, row):
    out = []
    for h in LEVELS[:3]:
        parts = [jnp.broadcast_to(b[j * 2 * h + h - 1:j * 2 * h + h, :], (2 * h, HGRN_DK))
                 for j in range(CHUNK // (2 * h))]
        mid = parts[0] if len(parts) == 1 else jnp.concatenate(parts, axis=0)
        out.append(jnp.exp(-jnp.abs(b - mid)))
    f = forget
    up1, up2, up3 = _group_roll(f, -1), _group_roll(f, -2), _group_roll(f, -3)
    dn1, dn2, dn3 = _group_roll(f, 1), _group_roll(f, 2), _group_roll(f, 3)
    r8, r4 = row & 7, row & 3
    s2 = up1 * up2
    p2 = dn1 * f
    p3 = dn2 * p2
    below = jnp.where(r8 == 4, f, jnp.where(r8 == 5, p2, jnp.where(r8 == 6, p3, dn3 * p3)))
    above = jnp.where(r8 == 0, s2 * up3, jnp.where(r8 == 1, s2, jnp.where(r8 == 2, up1, 1.0)))
    e4 = jnp.where(r8 >= 4, below, above)
    e2 = jnp.where(r4 == 0, up1, jnp.where(r4 == 1, 1.0, jnp.where(r4 == 2, f, p2)))
    e1 = jnp.where((row & 1) == 1, f, 1.0)
    return out + [e4, e2, e1]


def _hgrn_gates(zq, zf, lb):
    sq = jax.nn.sigmoid(zq)
    q = zq * sq
    sg = jax.nn.sigmoid(zf)
    forget = lb + (1.0 - lb) * sg
    return q, sq, sg, forget, 1.0 - forget, jnp.log(forget)


def _hgrn_specs(T, rb, rev):
    nr = T // rb
    ri = (lambda r: nr - 1 - r) if rev else (lambda r: r)
    return nr, ri, [
        pl.BlockSpec((rb, HGRN_DK), lambda h, r: (ri(r), h)),
        pl.BlockSpec((rb, HGRN_DK), lambda h, r: (ri(r), HGRN_HEADS + h)),
        pl.BlockSpec((rb, HGRN_DK), lambda h, r: (ri(r), 2 * HGRN_HEADS + h)),
        pl.BlockSpec((2, HGRN_DK), lambda h, r: (0, h)),
    ]


def _hgrn_fwd(z, lb_raw, rb=2048, carry=(None, None)):
    T = z.shape[0]
    rb = min(rb, T)
    ncb = rb // CHUNK
    unroll = min(UNROLL, ncb)
    assert ncb % unroll == 0
    nr, ri, in_specs = _hgrn_specs(T, rb, False)

    def body(*refs):
        hh, rr = pl.program_id(0), pl.program_id(1)
        own, finish = _carried(carry, refs, 4, 2, (hh == 0) & (rr == 0), (hh == HGRN_HEADS - 1) & (rr == nr - 1))
        zq_ref, zf_ref, zi_ref, lb_ref, o_ref, st_ref, state = own

        @pl.when(rr == 0)
        def _():
            state[...] = jnp.zeros_like(state)

        lb = _lower_bound(lb_ref)
        row = lax.broadcasted_iota(jnp.int32, (CHUNK, HGRN_DK), 0)
        masks = _level_masks()

        def operands(c):
            rows = pl.ds(pl.multiple_of(c * CHUNK, CHUNK), CHUNK)
            q, _, _, forget, k, lf = _hgrn_gates(zq_ref[rows, :], zf_ref[rows, :], lb)
            v = zi_ref[rows, :]
            b = _scan_rows(lf, row, False)
            pairs = [((q * e).astype(BF16), (k * e).astype(BF16)) for e in _level_scales(b, forget, row)]
            b_last = b[CHUNK - 1:CHUNK, :]
            return dict(c=c, rows=rows, pairs=pairs, vb=v.astype(BF16), diag=jnp.sum(q * k, axis=-1, keepdims=True) * v,
                        kd=(k * jnp.exp(b_last - b)).astype(BF16), qd=(q * jnp.exp(b)).astype(BF16),
                        decay=jnp.exp(b_last))

        def group(i, st):
            parts = [operands(i * unroll + j) for j in range(unroll)]
            for p in parts:
                sc = jnp.zeros((CHUNK, CHUNK), F32)
                for (qs, ks), mask in zip(p["pairs"], masks):
                    sc = sc + jnp.where(mask, _dot_nt(qs, ks), 0.0)
                p["sc"] = sc.astype(BF16)
            for p in parts:
                p["o"] = _dot(p["sc"], p["vb"]) + p["diag"]
                p["gain"] = _dot_tn(p["vb"], p["kd"])
            for p in parts:
                st_ref[p["c"], 0] = st
                o_ref[p["rows"], :] = p["o"] + _dot_nt(p["qd"], st.astype(BF16))
                st = st * p["decay"] + p["gain"]
            return st

        state[...] = lax.fori_loop(0, ncb // unroll, group, state[...])
        finish()

    in_specs, out_specs, out_shape, scratch, extra = _carried_specs(
        carry, in_specs,
        [pl.BlockSpec((rb, HGRN_DK), lambda h, r: (r, h)),
         pl.BlockSpec((ncb, 1, HGRN_DK, HGRN_DK), lambda h, r: (r, h, 0, 0))],
        [jax.ShapeDtypeStruct((T, D_MODEL), F32),
         jax.ShapeDtypeStruct((T // CHUNK, HGRN_HEADS, HGRN_DK, HGRN_DK), F32)],
        [pltpu.VMEM((HGRN_DK, HGRN_DK), F32)])
    return pl.pallas_call(
        body, name="hgrn_fwd", grid=(HGRN_HEADS, nr), in_specs=in_specs, out_specs=out_specs, out_shape=out_shape,
        scratch_shapes=scratch, compiler_params=_params(dimension_semantics=("arbitrary", "arbitrary")),
    )(z, z, z, lb_raw, *extra)


def _hgrn_bwd(z, lb_raw, states, do, rb=2048, carry=(None, None)):
    T = z.shape[0]
    rb = min(rb, T)
    ncb = rb // CHUNK
    unroll = min(UNROLL_BWD, ncb)
    assert ncb % unroll == 0
    nr, ri, in_specs = _hgrn_specs(T, rb, True)
    in_specs += [pl.BlockSpec((ncb, 1, HGRN_DK, HGRN_DK), lambda h, r: (ri(r), h, 0, 0)),
                 pl.BlockSpec((rb, HGRN_DK), lambda h, r: (ri(r), h))]

    def body(*refs):
        hh, rr = pl.program_id(0), pl.program_id(1)
        own, finish = _carried(carry, refs, 6, 4, (hh == 0) & (rr == 0), (hh == HGRN_HEADS - 1) & (rr == nr - 1))
        zq_ref, zf_ref, zi_ref, lb_ref, st_ref, do_ref, dq_ref, df_ref, di_ref, dlb_ref, dstate = own

        @pl.when(rr == 0)
        def _():
            dstate[...] = jnp.zeros_like(dstate)
            dlb_ref[...] = jnp.zeros_like(dlb_ref)

        lb = _lower_bound(lb_ref)
        row = lax.broadcasted_iota(jnp.int32, (CHUNK, HGRN_DK), 0)
        masks = _level_masks()

        def operands(c):
            rows = pl.ds(pl.multiple_of(c * CHUNK, CHUNK), CHUNK)
            zq = zq_ref[rows, :]
            q, sq, sg, forget, k, lf = _hgrn_gates(zq, zf_ref[rows, :], lb)
            v = zi_ref[rows, :]
            dov = do_ref[rows, :]
            b = _scan_rows(lf, row, False)
            b_last = b[CHUNK - 1:CHUNK, :]
            eb, ebb = jnp.exp(b), jnp.exp(b_last - b)
            es = _level_scales(b, forget, row)
            return dict(rows=rows, zq=zq, q=q, sq=sq, sg=sg, forget=forget, k=k, v=v, dov=dov, eb=eb, ebb=ebb,
                        e_last=jnp.exp(b_last), es=es, st=st_ref[c, 0], dob=dov.astype(BF16), vb=v.astype(BF16),
                        pairs=[((q * e).astype(BF16), (k * e).astype(BF16)) for e in es],
                        qd=(q * eb).astype(BF16), kd=(k * ebb).astype(BF16))

        def group(i, dlb):
            parts = [operands(ncb - 1 - (i * unroll + j)) for j in range(unroll)]
            for p in parts:
                p["da"] = _dot_nt(p["dob"], p["vb"])
                sc = jnp.zeros((CHUNK, CHUNK), F32)
                for (qs, ks), mask in zip(p["pairs"], masks):
                    sc = sc + jnp.where(mask, _dot_nt(qs, ks), 0.0)
                p["sc"] = sc.astype(BF16)
                p["dq_state"] = _dot(p["dob"], p["st"].astype(BF16))
                p["gain"] = _dot_tn(p["dob"], p["qd"])
            dst = dstate[...]
            for p in parts:
                p["dst"] = dst
                dst = dst * p["e_last"] + p["gain"]
            dstate[...] = dst
            for p in parts:
                dstb = p["dst"].astype(BF16)
                dk_state = p["ebb"] * _dot(p["vb"], dstb)
                dq = p["eb"] * p["dq_state"]
                dk = dk_state
                dv = _dot_nt(p["kd"], dstb) + _dot_tn(p["sc"], p["dob"])
                for e, (qs, ks), mask in zip(p["es"], p["pairs"], masks):
                    dam = jnp.where(mask, p["da"], 0.0).astype(BF16)
                    dq = dq + e * _dot(dam, ks)
                    dk = dk + e * _dot_tn(dam, qs)
                dad = jnp.sum(p["dov"] * p["v"], axis=-1, keepdims=True)
                p["dq"] = dq + dad * p["k"]
                p["dk"] = dk + dad * p["q"]
                p["dv"] = dv + jnp.sum(p["q"] * p["k"], axis=-1, keepdims=True) * p["dov"]
                p["extra"] = (p["e_last"] * jnp.sum(p["dst"] * p["st"], axis=0, keepdims=True)
                              + jnp.sum(p["k"] * dk_state, axis=0, keepdims=True))
            for p in parts:
                q, k, sq, sg, zq, rows = p["q"], p["k"], p["sq"], p["sg"], p["zq"], p["rows"]
                dlf = _scan_rows(q * p["dq"] - k * p["dk"], row, True) + p["extra"]
                dforget = dlf / p["forget"] - p["dk"]
                dq_ref[rows, :] = (p["dq"] * (sq * (1.0 + zq * (1.0 - sq)))).astype(BF16)
                df_ref[rows, :] = (dforget * (1.0 - lb) * sg * (1.0 - sg)).astype(BF16)
                di_ref[rows, :] = p["dv"].astype(BF16)
                dlb = dlb + jnp.sum(dforget * (1.0 - sg), axis=0, keepdims=True)
            return dlb

        dlb_ref[...] += lax.fori_loop(0, ncb // unroll, group, jnp.zeros((1, HGRN_DK), F32))
        finish()

    blk = pl.BlockSpec((rb, HGRN_DK), lambda h, r: (ri(r), h))
    in_specs, out_specs, out_shape, scratch, extra = _carried_specs(
        carry, in_specs, [blk, blk, blk, pl.BlockSpec((1, HGRN_DK), lambda h, r: (0, h))],
        [jax.ShapeDtypeStruct((T, D_MODEL), BF16)] * 3 + [jax.ShapeDtypeStruct((1, D_MODEL), F32)],
        [pltpu.VMEM((HGRN_DK, HGRN_DK), F32)])
    return pl.pallas_call(
        body, name="hgrn_bwd", grid=(HGRN_HEADS, nr), in_specs=in_specs, out_specs=out_specs, out_shape=out_shape,
        scratch_shapes=scratch, compiler_params=_params(dimension_semantics=("arbitrary", "arbitrary")),
    )(z, z, z, lb_raw, states, do, *extra)


MESH = pl.DeviceIdType.MESH
ANY = pl.BlockSpec(memory_space=pl.ANY)


def _place():
    return lax.axis_index("x"), lax.axis_index("y"), lax.axis_index("c")


def _sems(n):
    return [pltpu.SemaphoreType.DMA((7 * n,)), pltpu.SemaphoreType.DMA((7 * n,)), pltpu.SemaphoreType.DMA((n,))]


class _Gather:
    def __init__(self, x_ref, out_ref, send_sems, recv_sems, local_sems, idx):
        self.x_ref, self.out_ref, self.send_sems, self.recv_sems, self.local_sem, self.base = (
            x_ref, out_ref, send_sems, recv_sems, local_sems.at[idx], 7 * idx)
        x, y, c = _place()
        self.c = c
        self.me, self.sibling = (x, y, c), (x, y, 1 - c)
        self.chips = [(1 - x, y), (x, 1 - y), (1 - x, 1 - y)]

    def rows(self, px, py, pc):
        return self.out_ref.at[4 * px + 2 * py + pc]

    def copy(self, k, block, to, from_input=False):
        return pltpu.make_async_remote_copy(
            src_ref=self.x_ref if from_input else self.rows(*block), dst_ref=self.rows(*block),
            send_sem=self.send_sems.at[self.base + k], recv_sem=self.recv_sems.at[self.base + k], device_id=to,
            device_id_type=MESH)

    def first(self):
        out = [self.copy(0, self.me, self.sibling, from_input=True)]
        return out + [self.copy(1 + j, self.me, (*chip, self.c), from_input=True) for j, chip in enumerate(self.chips)]

    def start(self):
        pltpu.make_async_copy(self.x_ref, self.rows(*self.me), self.local_sem).start()
        for cp in self.first():
            cp.start()

    def finish(self):
        passed = [self.copy(4 + j, (*chip, self.c), self.sibling) for j, chip in enumerate(self.chips)]
        for j, chip in enumerate(self.chips):
            self.copy(1 + j, (*chip, self.c), self.me).wait_recv()
            passed[j].start()
        self.copy(0, self.sibling, self.me).wait_recv()
        for j, chip in enumerate(self.chips):
            self.copy(4 + j, (*chip, 1 - self.c), self.me).wait_recv()
        for cp in self.first() + passed:
            cp.wait_send()
        pltpu.make_async_copy(self.x_ref, self.rows(*self.me), self.local_sem).wait()


class _Many:
    def __init__(self, kind, in_refs, out_refs, send_sems, recv_sems, local_sems):
        self.ops = [kind(x, o, send_sems, recv_sems, local_sems, i) for i, (x, o) in enumerate(zip(in_refs, out_refs))]

    def start(self):
        for op in self.ops:
            op.start()

    def finish(self):
        for op in self.ops:
            op.finish()


def _result_shapes(kind, arrs):
    return [jax.ShapeDtypeStruct(a.shape if kind is _Exchange else (N_DEV,) + a.shape, a.dtype) for a in arrs]


def _all_gather(name, shards):
    n = len(shards)

    def body(*refs):
        g = _Many(_Gather, refs[:n], refs[n:2 * n], *refs[2 * n:])
        g.start()
        g.finish()

    return pl.pallas_call(
        body, name=name, out_shape=_result_shapes(_Gather, shards), in_specs=[ANY] * n, out_specs=[ANY] * n,
        scratch_shapes=_sems(n),
    )(*shards)


def _peers(x, y, c):
    out = []
    for k in range(1, N_DEV):
        px = 1 - x if k & 4 else x
        py = 1 - y if k & 2 else y
        pc = 1 - c if k & 1 else c
        out.append((k, (px, py, pc), 4 * px + 2 * py + pc))
    return out


class _Exchange:
    def __init__(self, g_ref, recv_ref, send_sems, recv_sems, local_sems, idx):
        x, y, c = _place()
        me = 4 * x + 2 * y + c
        self.local = pltpu.make_async_copy(g_ref.at[me], recv_ref.at[me], local_sems.at[idx])
        self.copies = [
            pltpu.make_async_remote_copy(
                src_ref=g_ref.at[pidx], dst_ref=recv_ref.at[me], send_sem=send_sems.at[7 * idx + k - 1],
                recv_sem=recv_sems.at[7 * idx + k - 1], device_id=peer, device_id_type=MESH)
            for k, peer, pidx in _peers(x, y, c)]

    def start(self):
        self.local.start()
        for cp in self.copies:
            cp.start()

    def finish(self):
        for cp in self.copies:
            cp.wait()
        self.local.wait()


def _carried(carry, refs, n_in, n_out, first, last):
    kind, arrs = carry
    if kind is None:
        return refs, lambda: None
    n = len(arrs)
    ins, rest = refs[:n_in], refs[n_in + n:]
    outs, scratch = rest[:n_out], rest[n_out + n:]
    op = _Many(kind, refs[n_in:n_in + n], rest[n_out:n_out + n], *scratch[len(scratch) - 3:])

    @pl.when(first)
    def _():
        op.start()

    def finish():
        @pl.when(last)
        def _():
            op.finish()

    return tuple(ins) + tuple(outs) + tuple(scratch[:len(scratch) - 3]), finish


def _carried_specs(carry, in_specs, out_specs, out_shape, scratch):
    kind, arrs = carry
    if kind is None:
        return in_specs, out_specs, out_shape, scratch, []
    n = len(arrs)
    return (list(in_specs) + [ANY] * n, list(out_specs) + [ANY] * n,
            list(out_shape) + _result_shapes(kind, arrs), list(scratch) + _sems(n), list(arrs))


def _adamw(w, g, m, v):
    m = ADAM_B1 * m + (1.0 - ADAM_B1) * g
    v = ADAM_B2 * v + (1.0 - ADAM_B2) * (g * g)
    m_hat = m / (1.0 - ADAM_B1 ** ADAM_STEP)
    v_hat = v / (1.0 - ADAM_B2 ** ADAM_STEP)
    delta = -ADAM_LR * (m_hat / (jnp.sqrt(v_hat) + ADAM_EPS) + ADAM_WD * w)
    return delta, m, v


def _adamw_sum(name, recvs, w, m, v):
    L, R, C = w.shape
    tm = 128 if R % 128 == 0 else 64
    assert R % tm == 0 and len(recvs) == L

    def body(*refs):
        r_refs, (w_ref, m_ref, v_ref, g_ref, d_ref, nm_ref, nv_ref) = refs[:L], refs[L:]
        for l in range(L):
            g = r_refs[l][0].astype(F32)
            for s in range(1, N_DEV):
                g = g + r_refs[l][s].astype(F32)
            g_ref[l] = g
            d_ref[l], nm_ref[l], nv_ref[l] = _adamw(w_ref[l], g, m_ref[l], v_ref[l])

    blk = pl.BlockSpec((L, tm, C), lambda i: (0, i, 0))
    return pl.pallas_call(
        body, name=name, grid=(R // tm,),
        in_specs=[pl.BlockSpec((N_DEV, tm, C), lambda i: (0, i, 0))] * L + [blk, blk, blk],
        out_specs=[blk] * 4, out_shape=[jax.ShapeDtypeStruct((L, R, C), F32)] * 4,
        compiler_params=_params(dimension_semantics=("arbitrary",)),
    )(*recvs, w, m, v)


def _small_sync(part, w, m, v):
    def body(p_ref, w_ref, m_ref, v_ref, g_ref, d_ref, nm_ref, nv_ref, gath, send_sems, recv_sems):
        x, y, c = _place()
        me = 4 * x + 2 * y + c
        gath[me] = p_ref[...]
        copies = []
        for k, peer, _ in _peers(x, y, c):
            cp = pltpu.make_async_remote_copy(
                src_ref=p_ref, dst_ref=gath.at[me], send_sem=send_sems.at[k - 1], recv_sem=recv_sems.at[k - 1],
                device_id=peer, device_id_type=MESH)
            cp.start()
            copies.append(cp)
        for cp in copies:
            cp.wait()
        g = gath[0]
        for s in range(1, N_DEV):
            g = g + gath[s]
        wv = w_ref[...]
        l0, l1 = w_ref[8:9, :], w_ref[9:10, :]
        mx = jnp.maximum(l0, l1)
        e0, e1 = jnp.exp(l0 - mx), jnp.exp(l1 - mx)
        g9 = g[9:10, :] * (e0 / (e0 + e1)) * (e1 / (e0 + e1))
        row = lax.broadcasted_iota(jnp.int32, g.shape, 0)
        g = jnp.where(row == 9, g9, jnp.where(row == 8, -g9, g))
        g_ref[...] = g
        d_ref[...], nm_ref[...], nv_ref[...] = _adamw(wv, g, m_ref[...], v_ref[...])

    vm = pl.BlockSpec(memory_space=pltpu.VMEM)
    return pl.pallas_call(
        body, name="small_params_sync", in_specs=[vm] * 4, out_specs=[vm] * 4,
        out_shape=[jax.ShapeDtypeStruct(part.shape, F32)] * 4,
        scratch_shapes=[pltpu.VMEM((N_DEV,) + part.shape, F32), pltpu.SemaphoreType.DMA((7,)),
                        pltpu.SemaphoreType.DMA((7,))],
    )(part, w, m, v)


def _shards_bf16(d, pieces):
    return [d[name][layer].astype(BF16) for name, layer in pieces]


def _gathered(arrs, pieces, out):
    for a, (name, layer) in zip(arrs, pieces):
        out[name, layer] = a if name in COL_SHARDED else a.reshape(N_DEV * a.shape[1], a.shape[2])


def _pad_row(a, width=D_MODEL):
    a = a.reshape(1, -1)
    return jnp.pad(a, ((0, 0), (0, width - a.shape[1])))


LOSS_ROW = 11


def _pack_small(d, gn_full, loss=None):
    rows = [d["mix_norm"], d["mlp_norm"], d["final_norm"].reshape(1, D_MODEL),
            _pad_row(d["attn_b_qkv"], 2 * D_MODEL).reshape(2, D_MODEL), _pad_row(d["attn_sinks"]),
            d["hgrn_lower_bounds"], gn_full.reshape(1, D_MODEL)]
    if loss is not None:
        rows.append(_pad_row(loss))
    p = jnp.concatenate(rows, axis=0)
    return jnp.pad(p, ((0, SMALL_ROWS - p.shape[0]), (0, 0)))


def _unpack_small(p, me):
    return dict(
        mix_norm=p[0:2], mlp_norm=p[2:4], final_norm=p[4],
        attn_b_qkv=p[5:7].reshape(1, 2 * D_MODEL)[:, :QKV_DIM], attn_sinks=p[7:8, :N_Q_HEADS],
        hgrn_lower_bounds=p[8:10], hgrn_g_norm=lax.dynamic_slice(p[10:11], (0, me * 128), (1, 128)))


WEIGHT_NAMES = ['mix_norm', 'mlp_norm', 'final_norm', 'attn_w_qkv', 'attn_b_qkv', 'attn_sinks', 'attn_w_o', 'hgrn_w_in',
                'hgrn_g_norm', 'hgrn_w_o', 'hgrn_lower_bounds', 'mlp_w_up', 'mlp_w_down']
SMALL_NAMES = ('mix_norm', 'mlp_norm', 'final_norm', 'attn_b_qkv', 'attn_sinks', 'hgrn_lower_bounds', 'hgrn_g_norm')


def _rotary_tables(positions):
    inv_freq = ROPE_THETA ** (-jnp.arange(0, 2 * ROT_HALF, 2, dtype=F32) / (2 * ROT_HALF))
    ang = positions.astype(F32).reshape(-1, 1) * inv_freq
    cos, sin = jnp.cos(ang), jnp.sin(ang)
    r = jnp.arange(LANES) % HEAD_DIM
    idx = r % ROT_HALF
    c = jnp.where(r < 2 * ROT_HALF, cos[:, idx], 1.0)
    sa = jnp.where((r >= ROT_HALF) & (r < 2 * ROT_HALF), sin[:, idx], 0.0)
    sb = jnp.where(r < ROT_HALF, -sin[:, idx], 0.0)
    return jnp.concatenate([c, sa, sb], axis=1)


def kernel(x, positions, mix_norm, mlp_norm, final_norm, attn_w_qkv, attn_b_qkv, attn_sinks, attn_w_o, hgrn_w_in, hgrn_g_norm, hgrn_w_o, hgrn_lower_bounds, mlp_w_up, mlp_w_down, loss_target, m_mix_norm, m_mlp_norm, m_final_norm, m_attn_w_qkv, m_attn_b_qkv, m_attn_sinks, m_attn_w_o, m_hgrn_w_in, m_hgrn_g_norm, m_hgrn_w_o, m_hgrn_lower_bounds, m_mlp_w_up, m_mlp_w_down, v_mix_norm, v_mlp_norm, v_final_norm, v_attn_w_qkv, v_attn_b_qkv, v_attn_sinks, v_attn_w_o, v_hgrn_w_in, v_hgrn_g_norm, v_hgrn_w_o, v_hgrn_lower_bounds, v_mlp_w_up, v_mlp_w_down):
    w = dict(mix_norm=mix_norm, mlp_norm=mlp_norm, final_norm=final_norm, attn_w_qkv=attn_w_qkv, attn_b_qkv=attn_b_qkv,
             attn_sinks=attn_sinks, attn_w_o=attn_w_o, hgrn_w_in=hgrn_w_in, hgrn_g_norm=hgrn_g_norm, hgrn_w_o=hgrn_w_o,
             hgrn_lower_bounds=hgrn_lower_bounds, mlp_w_up=mlp_w_up, mlp_w_down=mlp_w_down)
    m = dict(mix_norm=m_mix_norm, mlp_norm=m_mlp_norm, final_norm=m_final_norm, attn_w_qkv=m_attn_w_qkv,
             attn_b_qkv=m_attn_b_qkv, attn_sinks=m_attn_sinks, attn_w_o=m_attn_w_o, hgrn_w_in=m_hgrn_w_in,
             hgrn_g_norm=m_hgrn_g_norm, hgrn_w_o=m_hgrn_w_o, hgrn_lower_bounds=m_hgrn_lower_bounds, mlp_w_up=m_mlp_w_up,
             mlp_w_down=m_mlp_w_down)
    v = dict(mix_norm=v_mix_norm, mlp_norm=v_mlp_norm, final_norm=v_final_norm, attn_w_qkv=v_attn_w_qkv,
             attn_b_qkv=v_attn_b_qkv, attn_sinks=v_attn_sinks, attn_w_o=v_attn_w_o, hgrn_w_in=v_hgrn_w_in,
             hgrn_g_norm=v_hgrn_g_norm, hgrn_w_o=v_hgrn_w_o, hgrn_lower_bounds=v_hgrn_lower_bounds, mlp_w_up=v_mlp_w_up,
             mlp_w_down=v_mlp_w_down)
    me = 4 * lax.axis_index("x") + 2 * lax.axis_index("y") + lax.axis_index("c")

    gn = hgrn_g_norm.reshape(1, 128)
    gn_a = gn.astype(BF16)
    gn_b = (gn - gn_a.astype(F32)).astype(BF16)
    gn_c = (gn - gn_a.astype(F32) - gn_b.astype(F32)).astype(BF16)
    gn_rows = jnp.pad(jnp.concatenate([gn_a, gn_b, gn_c], axis=1), ((0, 15), (0, D_MODEL - 3 * 128)))
    full = {}
    got = _all_gather("gather_attn_weights", _shards_bf16(w, GATHER_FIRST) + [gn_rows])
    _gathered(got[:1], GATHER_FIRST, full)
    w_qkv = full["attn_w_qkv", 0].transpose(1, 0, 2).reshape(D_MODEL, QKV_DIM)
    gn_terms = got[1][:, 0, :3 * 128].astype(F32).reshape(N_DEV, 3, 128)
    gn_full = ((gn_terms[:, 0] + gn_terms[:, 1]) + gn_terms[:, 2]).reshape(1, D_MODEL)

    x0 = x[0]
    tgt = loss_target[0]
    rot = _rotary_tables(positions)
    row = lambda a: a.reshape(1, -1)

    qkv, h0 = _norm_mm("qkv_proj", x0, row(mix_norm[0]), w_qkv, attn_b_qkv, rot=rot)
    att, *got = _attn_fwd(qkv, attn_sinks, carry=(_Gather, _shards_bf16(w, GATHER_ATTN)))
    _gathered(got, GATHER_ATTN, full)
    x1 = _mm_res("attn_out_proj", att, full["attn_w_o", 0], x0)
    u0, h1, *got = _norm_mm("mlp0_up", x1, row(mlp_norm[0]), full["mlp_w_up", 0],
                            carry=(_Gather, _shards_bf16(w, GATHER_MLP0)))
    _gathered(got, GATHER_MLP0, full)
    x2, a0 = _mlp_down("mlp0_down", u0, full["mlp_w_down", 0], x1)
    z, h2 = _norm_mm("hgrn_in_proj", x2, row(mix_norm[1]), full["hgrn_w_in", 0])
    o_raw, states, *got = _hgrn_fwd(z, hgrn_lower_bounds, carry=(_Gather, _shards_bf16(w, GATHER_HGRN)))
    _gathered(got, GATHER_HGRN, full)
    x3, o2 = _hgrn_out("hgrn_out_proj", o_raw, z, gn_full, full["hgrn_w_o", 0], x2)
    u1, h3 = _norm_mm("mlp1_up", x3, row(mlp_norm[1]), full["mlp_w_up", 1])
    dx4, a1, loss_part, g_final = _mlp_down("mlp1_down_loss", u1, full["mlp_w_down", 1], x3,
                                            loss_head=(tgt, row(final_norm)))

    gw = {}
    du1, = _mlp_bwd_act("mlp1_bwd_act", dx4, u1, full["mlp_w_down", 1])
    dx3, g_mlp1 = _mm_nt_rmsbwd("mlp1_bwd_in", du1, full["mlp_w_up", 1], x3, row(mlp_norm[1]), dx4)
    gw["mlp_w_down", 1] = _mm_tn("mlp1_dw_down", a1, dx4, "rows")
    gw["mlp_w_up", 1] = _mm_tn("mlp1_dw_up", h3, du1, "cols")

    do_raw, dg, g_gn = _hgrn_out_bwd("hgrn_out_bwd", dx3, o_raw, z, full["hgrn_w_o", 0], gn_full)
    gw["hgrn_w_o", 0] = _mm_tn("hgrn_dw_o", o2, dx3, "rows")
    recvs = {}
    dzq, dzf, dzi, g_lb, *recv = _hgrn_bwd(z, hgrn_lower_bounds, states, do_raw,
                                           carry=(_Exchange, [gw[p] for p in GRADS_HGRN]))
    recvs.update(zip(GRADS_HGRN, recv))
    dz = [dzq, dzf, dzi, dg]
    dx2, g_mix1 = _mm_nt_rmsbwd("hgrn_in_bwd", dz, full["hgrn_w_in", 0], x2, row(mix_norm[1]), dx3)
    gw["hgrn_w_in", 0] = jnp.concatenate(
        [_mm_tn(f"hgrn_dw_in{j}", h2, d, "cols") for j, d in enumerate(dz)], axis=0)

    du0, *recv = _mlp_bwd_act("mlp0_bwd_act", dx2, u0, full["mlp_w_down", 0],
                              carry=(_Exchange, [gw[p] for p in GRADS_MLP0]))
    recvs.update(zip(GRADS_MLP0, recv))
    dx1, g_mlp0 = _mm_nt_rmsbwd("mlp0_bwd_in", du0, full["mlp_w_up", 0], x1, row(mlp_norm[0]), dx2)
    gw["mlp_w_down", 0] = _mm_tn("mlp0_dw_down", a0, dx2, "rows")
    gw["mlp_w_up", 0] = _mm_tn("mlp0_dw_up", h1, du0, "cols")

    datt = _mm_nt("attn_out_bwd", dx1, full["attn_w_o", 0], BF16)
    gw["attn_w_o", 0] = _mm_tn("attn_dw_o", att, dx1, "rows")
    dqkv, g_sink, *recv = _attn_bwd(qkv, rot, attn_sinks, datt, carry=(_Exchange, [gw[p] for p in GRADS_ATTN]))
    recvs.update(zip(GRADS_ATTN, recv))
    g_qkv = _mm_tn("attn_dw_qkv", h0, dqkv, bn=512)
    g_qkv = g_qkv.reshape(D_MODEL, N_DEV, QKV_DIM // N_DEV).transpose(1, 0, 2).astype(BF16)
    dx0, g_mix0, g_bqkv, recvs["attn_w_qkv", 0] = _mm_nt_rmsbwd(
        "qkv_bwd", dqkv, w_qkv, x0, row(mix_norm[0]), dx1, with_colsum=True, carry=(_Exchange, [g_qkv]))

    big = {name: _adamw_sum("adamw_" + name, [recvs[name, l] for l in range(w[name].shape[0])], w[name], m[name], v[name])
           for name in BIG_NAMES}

    zero_row = jnp.zeros((1, D_MODEL), F32)
    part = _pack_small(dict(
        mix_norm=jnp.concatenate([g_mix0, g_mix1], axis=0), mlp_norm=jnp.concatenate([g_mlp0, g_mlp1], axis=0),
        final_norm=g_final, attn_b_qkv=g_bqkv, attn_sinks=g_sink[:, :N_Q_HEADS],
        hgrn_lower_bounds=jnp.concatenate([zero_row, g_lb], axis=0)), g_gn, loss=loss_part)

    def spread(a):
        return lax.dynamic_update_slice(zero_row, a.reshape(1, 128), (0, me * 128))

    small_in = [_pack_small({n: d[n] for n in SMALL_NAMES if n != "hgrn_g_norm"}, spread(d["hgrn_g_norm"]))
                for d in (w, m, v)]
    synced = _small_sync(part, *small_in)
    small = [_unpack_small(p, me) for p in synced]

    outs = [synced[0][LOSS_ROW, 0], dx0.reshape(x.shape)]
    for kind, grp_small in enumerate(small):
        for name in WEIGHT_NAMES:
            val = grp_small[name] if name in SMALL_NAMES else big[name][kind]
            outs.append(val.reshape(w[name].shape))
    return tuple(outs)
```

```python
import functools

import jax
import jax.numpy as jnp
from jax import lax
from jax.experimental import pallas as pl
from jax.experimental.pallas import tpu as pltpu

F32 = jnp.float32
BF16 = jnp.bfloat16

D_MODEL = 1024
HEAD_DIM = 64
N_Q_HEADS = 16
Q_DIM = 1024
KV_DIM = 256
QKV_DIM = 1536
ATT_BLOCK = 128
ROT_HALF = 8
ROPE_THETA = 500000.0
NEG_INF = -1e30
HGRN_HEADS = 8
HGRN_DK = 128
CHUNK = 64
D_FF = 4096
NORM_EPS = 1e-5
N_DEV = 8

ADAM_LR = 0.001
ADAM_B1 = 0.9
ADAM_B2 = 0.999
ADAM_EPS = 1e-08
ADAM_WD = 0.01
ADAM_STEP = 10

LANES = 128
VMEM_LIMIT = 56 * 1024 * 1024

GATHER_FIRST = (("attn_w_qkv", 0),)
GATHER_ATTN = (("attn_w_o", 0), ("mlp_w_up", 0), ("mlp_w_down", 0))
GATHER_MLP0 = (("hgrn_w_in", 0), ("hgrn_w_o", 0))
GATHER_HGRN = (("mlp_w_up", 1), ("mlp_w_down", 1))
GRADS_HGRN = (("mlp_w_down", 1), ("mlp_w_up", 1), ("hgrn_w_o", 0))
GRADS_ATTN = (("mlp_w_up", 0), ("attn_w_o", 0))
COL_SHARDED = ("attn_w_qkv", "hgrn_w_in", "mlp_w_up")
BIG_NAMES = ("attn_w_qkv", "attn_w_o", "hgrn_w_in", "hgrn_w_o", "mlp_w_up", "mlp_w_down")
SMALL_ROWS = 16


def _dot(a, b):
    return jnp.dot(a, b, preferred_element_type=F32)


def _dot_nt(a, b):
    return lax.dot_general(a, b, (((1,), (1,)), ((), ())), preferred_element_type=F32)


def _dot_tn(a, b):
    return lax.dot_general(a, b, (((0,), (0,)), ((), ())), preferred_element_type=F32)


def _params(**kw):
    return pltpu.CompilerParams(vmem_limit_bytes=VMEM_LIMIT, **kw)


def _full_spec(a):
    nd = a.ndim
    return pl.BlockSpec(a.shape, lambda *_: (0,) * nd)


def _row_call(name, body, n_rows, tm, row_ins, full_ins, row_outs, acc_outs=(), carry=(None, None)):
    steps = n_rows // tm
    in_specs = [pl.BlockSpec((tm, w), functools.partial(lambda i, cb: (i, cb), cb=cb)) for _, w, cb in row_ins]
    in_specs += [_full_spec(a) for a in full_ins]
    out_shape = [jax.ShapeDtypeStruct((n_rows, w), dt) for w, dt in row_outs]
    out_specs = [pl.BlockSpec((tm, w), lambda i: (i, 0)) for w, _ in row_outs]
    for shp, dt in acc_outs:
        out_shape.append(jax.ShapeDtypeStruct(shp, dt))
        out_specs.append(pl.BlockSpec(shp, functools.partial(lambda i, nd: (0,) * nd, nd=len(shp))))
    n_in, n_out = len(in_specs), len(out_specs)
    in_specs, out_specs, out_shape, scratch, extra = _carried_specs(carry, in_specs, out_specs, out_shape, [])

    def wrapped(*refs):
        i = pl.program_id(0)
        own, finish = _carried(carry, refs, n_in, n_out, i == 0, i == steps - 1)
        body(*own)
        finish()

    return pl.pallas_call(
        wrapped, name=name, grid=(steps,), in_specs=in_specs, out_specs=out_specs, out_shape=out_shape,
        scratch_shapes=scratch, compiler_params=_params(dimension_semantics=("arbitrary",)),
    )(*[a for a, _, _ in row_ins], *full_ins, *extra)


def _rms(x, gain):
    r = lax.rsqrt(jnp.mean(x * x, axis=-1, keepdims=True) + NORM_EPS)
    xhat = x * r
    return xhat * gain, xhat, r


def _rms_bwd(dy, xhat, r, gain):
    dxhat = dy * gain
    dx = r * (dxhat - xhat * jnp.mean(dxhat * xhat, axis=-1, keepdims=True))
    return dx, dy * xhat


def _norm_mm(name, x, gain, w, bias=None, rot=None, tm=512, carry=(None, None)):
    T = x.shape[0]
    tm = min(tm, T)
    nc = 512
    blocked = w.ndim == 3
    n = N_DEV * w.shape[2] if blocked else w.shape[1]
    assert n % nc == 0 and (not blocked or w.shape[2] == nc)

    def body(*refs):
        x_ref, refs = refs[0], refs[1:]
        if rot is not None:
            t_ref, refs = refs[0], refs[1:]
        g_ref, w_ref, refs = refs[0], refs[1], refs[2:]
        if bias is not None:
            b_ref, refs = refs[0], refs[1:]
        y_ref, h_ref = refs
        h, _, _ = _rms(x_ref[...], g_ref[...])
        hb = h.astype(BF16)
        h_ref[...] = hb
        for c in range(n // nc):
            sl = slice(c * nc, (c + 1) * nc)
            y = _dot(hb, w_ref[c] if blocked else w_ref[:, sl])
            if bias is not None:
                y = y + b_ref[:, sl]
            if rot is None:
                y_ref[:, sl] = y
            else:
                n_rot = max(0, min(nc, Q_DIM + KV_DIM - c * nc)) // LANES
                pieces = _rot_fwd(y[:, :n_rot * LANES], t_ref[...]) if n_rot else []
                for j in range(nc // LANES):
                    col = slice(c * nc + j * LANES, c * nc + (j + 1) * LANES)
                    y_ref[:, col] = pieces[j] if j < n_rot else y[:, j * LANES:(j + 1) * LANES]

    rows = [(x, D_MODEL, 0)] + ([(rot, 3 * LANES, 0)] if rot is not None else [])
    full = [gain, w] + ([bias] if bias is not None else [])
    return _row_call(name, body, T, tm, rows, full, [(n, F32), (D_MODEL, BF16)], carry=carry)


def _mm_res(name, a, w, res, tm=512):
    T = a.shape[0]
    tm = min(tm, T)

    def body(a_ref, r_ref, w_ref, o_ref):
        o_ref[...] = r_ref[...] + _dot(a_ref[...], w_ref[...])

    return _row_call(name, body, T, tm, [(a, a.shape[1], 0), (res, D_MODEL, 0)], [w], [(D_MODEL, F32)])[0]


def _mlp_down(name, u, w, res, tm=512, loss_head=None):
    T = u.shape[0]
    tm = min(tm, T)
    kc = 1024
    sub = min(256, tm)

    def body(*refs):
        if loss_head is None:
            u_ref, r_ref, w_ref, o_ref, a_ref = refs
        else:
            u_ref, r_ref, t_ref, w_ref, g_ref, o_ref, a_ref, loss_ref, dg_ref = refs

            @pl.when(pl.program_id(0) == 0)
            def _():
                loss_ref[...] = jnp.zeros_like(loss_ref)
                dg_ref[...] = jnp.zeros_like(dg_ref)

        for r0 in range(0, tm, sub):
            rs = slice(r0, r0 + sub)
            acc = r_ref[rs, :]
            for c in range(D_FF // kc):
                sl = slice(c * kc, (c + 1) * kc)
                a = jnp.maximum(u_ref[rs, sl], 0.0)
                ab = (a * a).astype(BF16)
                a_ref[rs, sl] = ab
                acc = acc + _dot(ab, w_ref[sl, :])
            if loss_head is None:
                o_ref[rs, :] = acc
            else:
                gain_v = g_ref[...]
                y, xhat, r = _rms(acc, gain_v)
                diff = y - t_ref[rs, :]
                per_row = jnp.sum(diff * diff, axis=-1, keepdims=True) * (1.0 / D_MODEL)
                loss_ref[...] += jnp.broadcast_to(0.5 * jnp.sum(per_row, axis=0, keepdims=True), loss_ref.shape)
                dx, dgr = _rms_bwd(diff * (1.0 / D_MODEL), xhat, r, gain_v)
                o_ref[rs, :] = dx
                dg_ref[...] += jnp.sum(dgr, axis=0, keepdims=True)

    rows, full, acc_outs = [(u, D_FF, 0), (res, D_MODEL, 0)], [w], []
    if loss_head is not None:
        rows, full = rows + [(loss_head[0], D_MODEL, 0)], full + [loss_head[1]]
        acc_outs = [((1, LANES), F32), ((1, D_MODEL), F32)]
    return _row_call(name, body, T, tm, rows, full, [(D_MODEL, F32), (D_FF, BF16)], acc_outs)


def _hgrn_out(name, o_raw, z, gn, w, res, tm=512):
    T = o_raw.shape[0]
    tm = min(tm, T)

    def body(o_ref, g_ref, r_ref, gn_ref, w_ref, x_ref, a_ref):
        y, _, _ = _rms(o_ref[...], gn_ref[...])
        g = g_ref[...]
        a = (y * (g * jax.nn.sigmoid(g))).astype(BF16)
        a_ref[...] = a
        x_ref[...] = r_ref[...] + _dot(a, w_ref[...])

    return _row_call(name, body, T, tm, [(o_raw, D_MODEL, 0), (z, D_MODEL, 3), (res, D_MODEL, 0)], [gn, w],
                     [(D_MODEL, F32), (D_MODEL, BF16)])


def _mm_nt_rmsbwd(name, dy, w, x, gain, dres, tm=512, with_colsum=False, carry=(None, None)):
    T = x.shape[0]
    tm = min(tm, T)
    dys = list(dy) if isinstance(dy, (list, tuple)) else [dy]
    width = dys[0].shape[1]
    n = width * len(dys)
    sub = min(256, tm)
    assert not with_colsum or len(dys) == 1

    def body(*refs):
        dy_refs, refs = refs[:len(dys)], refs[len(dys):]
        if with_colsum:
            x_ref, dr_ref, w_ref, g_ref, dx_ref, dg_ref, cs_ref = refs
        else:
            x_ref, dr_ref, w_ref, g_ref, dx_ref, dg_ref = refs

        @pl.when(pl.program_id(0) == 0)
        def _():
            dg_ref[...] = jnp.zeros_like(dg_ref)
            if with_colsum:
                cs_ref[...] = jnp.zeros_like(cs_ref)

        gain_v = g_ref[...]
        for r0 in range(0, tm, sub):
            rs = slice(r0, r0 + sub)
            if w.ndim == 3:
                nb = w.shape[2]
                dh = None
                for p in range(N_DEV):
                    piece, off = divmod(p * nb, width)
                    part = _dot_nt(dy_refs[piece][rs, off:off + nb].astype(BF16), w_ref[p])
                    dh = part if dh is None else dh + part
            else:
                dh = _dot_nt(dy_refs[0][rs, :].astype(BF16), w_ref[...])
            _, xhat, r = _rms(x_ref[rs, :], gain_v)
            dx, dgr = _rms_bwd(dh, xhat, r, gain_v)
            dx_ref[rs, :] = dr_ref[rs, :] + dx
            dg_ref[...] += jnp.sum(dgr, axis=0, keepdims=True)
            if with_colsum:
                cs_ref[...] += jnp.sum(dy_refs[0][rs, :].astype(F32), axis=0, keepdims=True)

    acc = [((1, D_MODEL), F32)] + ([((1, n), F32)] if with_colsum else [])
    rows = [(d, width, 0) for d in dys] + [(x, D_MODEL, 0), (dres, D_MODEL, 0)]
    return _row_call(name, body, T, tm, rows, [w, gain], [(D_MODEL, F32)], acc, carry=carry)


def _mm_nt(name, dy, w, out_dtype, tm=512):
    T = dy.shape[0]
    tm = min(tm, T)
    k = w.shape[0]

    def body(dy_ref, w_ref, o_ref):
        o_ref[...] = _dot_nt(dy_ref[...].astype(BF16), w_ref[...]).astype(out_dtype)

    return _row_call(name, body, T, tm, [(dy, dy.shape[1], 0)], [w], [(k, out_dtype)])[0]


def _mlp_bwd_act(name, dy, u, w_down, tm=512, carry=(None, None)):
    T = u.shape[0]
    tm = min(tm, T)
    kc = 1024

    def body(dy_ref, u_ref, w_ref, du_ref):
        dyb = dy_ref[...].astype(BF16)
        for c in range(D_FF // kc):
            sl = slice(c * kc, (c + 1) * kc)
            da = _dot_nt(dyb, w_ref[sl, :])
            du_ref[:, sl] = (da * (2.0 * jnp.maximum(u_ref[:, sl], 0.0))).astype(BF16)

    return _row_call(name, body, T, tm, [(dy, D_MODEL, 0), (u, D_FF, 0)], [w_down], [(D_FF, BF16)], carry=carry)


def _hgrn_out_bwd(name, dx, o_raw, z, w, gn, tm=512):
    T = dx.shape[0]
    tm = min(tm, T)

    def body(dx_ref, o_ref, g_ref, w_ref, gn_ref, do_ref, dg_ref, dgn_ref):
        @pl.when(pl.program_id(0) == 0)
        def _():
            dgn_ref[...] = jnp.zeros_like(dgn_ref)

        da = _dot_nt(dx_ref[...].astype(BF16), w_ref[...])
        gn_v = gn_ref[...]
        y, xhat, r = _rms(o_ref[...], gn_v)
        g = g_ref[...]
        sg = jax.nn.sigmoid(g)
        dg_ref[...] = (da * y * (sg * (1.0 + g * (1.0 - sg)))).astype(BF16)
        dyn = da * (g * sg)
        do, dgr = _rms_bwd(dyn, xhat, r, gn_v)
        do_ref[...] = do
        dgn_ref[...] += jnp.sum(dgr, axis=0, keepdims=True)

    return _row_call(name, body, T, tm, [(dx, D_MODEL, 0), (o_raw, D_MODEL, 0), (z, D_MODEL, 3)], [w, gn],
                     [(D_MODEL, F32), (D_MODEL, BF16)], [((1, D_MODEL), F32)])


COL_BLOCK = D_FF // N_DEV


def _mm_tn(name, a, b, shard=None, bm=1024, bn=1024, tk=2048, carry=(None, None)):
    T, M = a.shape
    N = b.shape[1]
    bm, bn, tk = min(bm, M), min(bn, N), min(tk, T)
    nk = T // tk
    if shard is None:
        out_shape, out_block = jax.ShapeDtypeStruct((M, N), F32), (bm, bn)
        out_map = lambda i, j, k: (i, j)
    elif shard == "cols":
        assert bn % COL_BLOCK == 0 and N % bn == 0
        out_shape = jax.ShapeDtypeStruct((N // COL_BLOCK, M, COL_BLOCK), BF16)
        out_block = (bn // COL_BLOCK, bm, COL_BLOCK)
        out_map = lambda i, j, k: (j, i, 0)
    else:
        rows = M // N_DEV
        assert bm % rows == 0
        out_shape, out_block = jax.ShapeDtypeStruct((N_DEV, rows, N), BF16), (bm // rows, rows, bn)
        out_map = lambda i, j, k: (i, 0, j)

    grid = (M // bm, N // bn, nk)

    def body(*refs):
        i, j, k = pl.program_id(0), pl.program_id(1), pl.program_id(2)
        own, finish = _carried(carry, refs, 2, 1, (i == 0) & (j == 0) & (k == 0),
                               (i == grid[0] - 1) & (j == grid[1] - 1) & (k == nk - 1))
        a_ref, b_ref, o_ref, acc = own

        @pl.when(k == 0)
        def _():
            acc[...] = jnp.zeros_like(acc)

        acc[...] += _dot_tn(a_ref[...].astype(BF16), b_ref[...].astype(BF16))

        @pl.when(k == nk - 1)
        def _():
            if shard == "cols":
                for c in range(bn // COL_BLOCK):
                    o_ref[c] = acc[:, c * COL_BLOCK:(c + 1) * COL_BLOCK].astype(BF16)
            else:
                o_ref[...] = acc[...].reshape(out_block).astype(o_ref.dtype)

        finish()

    in_specs, out_specs, out_shapes, scratch, extra = _carried_specs(
        carry, [pl.BlockSpec((tk, bm), lambda i, j, k: (k, i)), pl.BlockSpec((tk, bn), lambda i, j, k: (k, j))],
        [pl.BlockSpec(out_block, out_map)], [out_shape], [pltpu.VMEM((bm, bn), F32)])
    res = pl.pallas_call(
        body, name=name, grid=grid, in_specs=in_specs, out_specs=out_specs, out_shape=out_shapes,
        scratch_shapes=scratch, compiler_params=_params(dimension_semantics=("arbitrary", "arbitrary", "arbitrary")),
    )(a, b, *extra)
    return res[0] if carry[0] is None else res


def _rot_fwd(x, tab):
    c, sa, sb = tab[:, :LANES], tab[:, LANES:2 * LANES], tab[:, 2 * LANES:]
    outs = []
    for j in range(x.shape[1] // LANES):
        xs = x[:, j * LANES:(j + 1) * LANES]
        outs.append(xs * c + pltpu.roll(xs, ROT_HALF, 1) * sa + pltpu.roll(xs, LANES - ROT_HALF, 1) * sb)
    return outs


def _rot_bwd(dys, tab):
    c, sa, sb = tab[:, :LANES], tab[:, LANES:2 * LANES], tab[:, 2 * LANES:]
    return [dy * c + pltpu.roll(dy * sa, LANES - ROT_HALF, 1) + pltpu.roll(dy * sb, ROT_HALF, 1) for dy in dys]


ATT_SCALE = HEAD_DIM ** -0.5
HEAD_LAG = 2


def _attn_masks(n):
    kj = lax.broadcasted_iota(jnp.int32, (2 * ATT_BLOCK, ATT_BLOCK), 0)
    qi = lax.broadcasted_iota(jnp.int32, (2 * ATT_BLOCK, ATT_BLOCK), 1)
    delta = qi + ATT_BLOCK - kj
    first_key = jnp.where(n > 0, 0, ATT_BLOCK)
    valid = (delta >= 0) & (delta < ATT_BLOCK) & (kj >= first_key)
    low = lax.broadcasted_iota(jnp.int32, (1, LANES), 1) < HEAD_DIM
    upper = lax.broadcasted_iota(jnp.int32, (LANES, 1), 0) < HEAD_DIM
    return valid, low, upper


def _softmax_sink(s, valid, sink):
    s = jnp.where(valid, s, NEG_INF)
    m = jnp.maximum(jnp.max(s, axis=0, keepdims=True), sink)
    e = jnp.exp(s - m)
    es = jnp.exp(sink - m)
    inv = 1.0 / (jnp.sum(e, axis=0, keepdims=True) + es)
    return e * inv, es * inv


def _attn_specs(nb, tables):
    prev = lambda n: jnp.maximum(jnp.minimum(n, nb - 1) - 1, 0)
    cur = lambda n: jnp.minimum(n, nb - 1)
    specs = [
        pl.BlockSpec((ATT_BLOCK, Q_DIM), lambda n: (cur(n), 0)),
        pl.BlockSpec((ATT_BLOCK, KV_DIM), lambda n: (prev(n), 4)),
        pl.BlockSpec((ATT_BLOCK, KV_DIM), lambda n: (cur(n), 4)),
        pl.BlockSpec((ATT_BLOCK, KV_DIM), lambda n: (prev(n), 5)),
        pl.BlockSpec((ATT_BLOCK, KV_DIM), lambda n: (cur(n), 5)),
    ]
    if tables:
        specs += [pl.BlockSpec((ATT_BLOCK, 3 * LANES), lambda n: (prev(n), 0)),
                  pl.BlockSpec((ATT_BLOCK, 3 * LANES), lambda n: (cur(n), 0))]
    return specs + [pl.BlockSpec(memory_space=pltpu.SMEM)]


def _kv_band(prev_ref, cur_ref):
    out = []
    for j in range(KV_DIM // LANES):
        sl = slice(j * LANES, (j + 1) * LANES)
        band = jnp.concatenate([prev_ref[:, sl], cur_ref[:, sl]], axis=0)
        out.append((band, pltpu.roll(band, HEAD_DIM, 1)))
    return out


def _bf16(bands, transposed=False):
    return [[(a.T if transposed else a).astype(BF16) for a in pair] for pair in bands]


def _attn_fwd(qkv, sinks, carry=(None, None)):
    T = qkv.shape[0]
    nb = T // ATT_BLOCK

    def body(*refs):
        n = pl.program_id(0)
        own, finish = _carried(carry, refs, 6, 1, n == 0, n == nb - 1)
        q_ref, kp_ref, kc_ref, vp_ref, vc_ref, sink_ref, o_ref = own
        valid, low, upper = _attn_masks(n)
        ks = _bf16(_kv_band(kp_ref, kc_ref))
        vts = _bf16(_kv_band(vp_ref, vc_ref), transposed=True)
        heads, outs = {}, {}

        def first(h):
            p, hf = h // 2, h % 2
            kpair, khalf = p // 4, (p // 2) % 2
            qm = jnp.where(low if hf == 0 else ~low, q_ref[:, p * LANES:(p + 1) * LANES] * ATT_SCALE, 0.0)
            sw = 0 if khalf == hf else 1
            heads[h] = (kpair, sw, _dot_nt(ks[kpair][sw], qm.astype(BF16)))

        def second(h):
            kpair, sw, s = heads[h]
            heads[h] = (kpair, sw, _softmax_sink(s, valid, sink_ref[0, h])[0].astype(BF16))

        def third(h):
            kpair, sw, pr = heads.pop(h)
            outs[h] = _dot(vts[kpair][sw], pr)
            if h % 2:
                o_ref[:, (h // 2) * LANES:(h // 2 + 1) * LANES] = jnp.where(upper, outs.pop(h - 1), outs.pop(h)).T.astype(BF16)

        for i in range(N_Q_HEADS + 2 * HEAD_LAG):
            if i < N_Q_HEADS:
                first(i)
            if 0 <= i - HEAD_LAG < N_Q_HEADS:
                second(i - HEAD_LAG)
            if 0 <= i - 2 * HEAD_LAG < N_Q_HEADS:
                third(i - 2 * HEAD_LAG)
        finish()

    in_specs, out_specs, out_shape, scratch, extra = _carried_specs(
        carry, _attn_specs(nb, False), [pl.BlockSpec((ATT_BLOCK, Q_DIM), lambda n: (n, 0))],
        [jax.ShapeDtypeStruct((T, Q_DIM), BF16)], [])
    return pl.pallas_call(
        body, name="attn_fwd", grid=(nb,), in_specs=in_specs, out_specs=out_specs, out_shape=out_shape,
        scratch_shapes=scratch, compiler_params=_params(dimension_semantics=("arbitrary",)),
    )(qkv, qkv, qkv, qkv, qkv, sinks, *extra)


def _attn_bwd(qkv, rot, sinks, dout, carry=(None, None)):
    T = qkv.shape[0]
    nb = T // ATT_BLOCK
    npair = KV_DIM // LANES

    def body(*refs):
        n = pl.program_id(0)
        own, finish = _carried(carry, refs, 9, 2, n == 0, n == nb)
        (q_ref, kp_ref, kc_ref, vp_ref, vc_ref, tp_ref, tc_ref, sink_ref, do_ref, dqkv_ref, dsink_ref,
         dq_c, dk_c, dv_c) = own

        @pl.when(n == 0)
        def _():
            dq_c[...] = jnp.zeros_like(dq_c)
            dk_c[...] = jnp.zeros_like(dk_c)
            dv_c[...] = jnp.zeros_like(dv_c)
            dsink_ref[...] = jnp.zeros_like(dsink_ref)

        def flush(dk_prev, dv_prev, tab_ref):
            dqkv_ref[:, :Q_DIM] = dq_c[...].astype(BF16)
            dk = _rot_bwd([dk_c[:, j * LANES:(j + 1) * LANES] + dk_prev[j] for j in range(npair)], tab_ref[...])
            for j in range(npair):
                dqkv_ref[:, Q_DIM + j * LANES:Q_DIM + (j + 1) * LANES] = dk[j].astype(BF16)
                dqkv_ref[:, Q_DIM + KV_DIM + j * LANES:Q_DIM + KV_DIM + (j + 1) * LANES] = (
                    dv_c[:, j * LANES:(j + 1) * LANES] + dv_prev[j]).astype(BF16)

        @pl.when(n < nb)
        def _():
            valid, low, upper = _attn_masks(n)
            lane = lax.broadcasted_iota(jnp.int32, (1, LANES), 1)
            k_band = _kv_band(kp_ref, kc_ref)
            ks, kts = _bf16(k_band), _bf16(k_band, transposed=True)
            vs = _bf16(_kv_band(vp_ref, vc_ref))
            dk_acc = [[jnp.zeros((2 * ATT_BLOCK, LANES), F32) for _ in range(2)] for _ in range(npair)]
            dv_acc = [[jnp.zeros((2 * ATT_BLOCK, LANES), F32) for _ in range(2)] for _ in range(npair)]
            dsink = jnp.zeros((1, LANES), F32)
            heads, dq_t, dsinks = {}, {}, []

            def first(h):
                p, hf = h // 2, h % 2
                kpair, khalf = p // 4, (p // 2) % 2
                sel = low if hf == 0 else ~low
                qm = jnp.where(sel, q_ref[:, p * LANES:(p + 1) * LANES] * ATT_SCALE, 0.0).astype(BF16)
                dom = jnp.where(sel, do_ref[:, p * LANES:(p + 1) * LANES], 0.0).astype(BF16)
                sw = 0 if khalf == hf else 1
                heads[h] = dict(kpair=kpair, sw=sw, qm=qm, dom=dom, s=_dot_nt(ks[kpair][sw], qm),
                                dp=_dot_nt(vs[kpair][sw], dom))

            def second(h):
                d = heads[h]
                pr, ps = _softmax_sink(d.pop("s"), valid, sink_ref[0, h])
                dp = d.pop("dp")
                dd = jnp.sum(pr * dp, axis=0, keepdims=True)
                dsinks.append(jnp.where(lane == h, -jnp.sum(ps * dd, axis=1, keepdims=True), 0.0))
                d["ds"] = (pr * (dp - dd)).astype(BF16)
                d["pr"] = pr.astype(BF16)

            def third(h):
                d = heads.pop(h)
                kpair, sw = d["kpair"], d["sw"]
                dq_t[h] = _dot(kts[kpair][sw], d["ds"])
                dk_acc[kpair][sw] = dk_acc[kpair][sw] + _dot(d["ds"], d["qm"])
                dv_acc[kpair][sw] = dv_acc[kpair][sw] + _dot(d["pr"], d["dom"])

            for i in range(N_Q_HEADS + 2 * HEAD_LAG):
                if i < N_Q_HEADS:
                    first(i)
                if 0 <= i - HEAD_LAG < N_Q_HEADS:
                    second(i - HEAD_LAG)
                if 0 <= i - 2 * HEAD_LAG < N_Q_HEADS:
                    third(i - 2 * HEAD_LAG)
            dsink = sum(dsinks, dsink)
            dqs = [jnp.where(upper, dq_t[2 * p], dq_t[2 * p + 1]).T * ATT_SCALE for p in range(Q_DIM // LANES)]
            dk_acc = [a[0] + pltpu.roll(a[1], HEAD_DIM, 1) for a in dk_acc]
            dv_acc = [a[0] + pltpu.roll(a[1], HEAD_DIM, 1) for a in dv_acc]
            flush([a[:ATT_BLOCK] for a in dk_acc], [a[:ATT_BLOCK] for a in dv_acc], tp_ref)
            dq = _rot_bwd(dqs, tc_ref[...])
            for p in range(Q_DIM // LANES):
                dq_c[:, p * LANES:(p + 1) * LANES] = dq[p]
            for j in range(npair):
                dk_c[:, j * LANES:(j + 1) * LANES] = dk_acc[j][ATT_BLOCK:]
                dv_c[:, j * LANES:(j + 1) * LANES] = dv_acc[j][ATT_BLOCK:]
            dsink_ref[...] += dsink

        @pl.when(n == nb)
        def _():
            zero = [jnp.zeros((ATT_BLOCK, LANES), F32) for _ in range(npair)]
            flush(zero, zero, tc_ref)

        finish()

    do_spec = pl.BlockSpec((ATT_BLOCK, Q_DIM), lambda n: (jnp.minimum(n, nb - 1), 0))
    in_specs, out_specs, out_shape, scratch, extra = _carried_specs(
        carry, _attn_specs(nb, True) + [do_spec],
        [pl.BlockSpec((ATT_BLOCK, QKV_DIM), lambda n: (jnp.maximum(n - 1, 0), 0)),
         pl.BlockSpec((1, LANES), lambda n: (0, 0))],
        [jax.ShapeDtypeStruct((T, QKV_DIM), BF16), jax.ShapeDtypeStruct((1, LANES), F32)],
        [pltpu.VMEM((ATT_BLOCK, Q_DIM), F32), pltpu.VMEM((ATT_BLOCK, KV_DIM), F32),
         pltpu.VMEM((ATT_BLOCK, KV_DIM), F32)])
    return pl.pallas_call(
        body, name="attn_bwd", grid=(nb + 1,), in_specs=in_specs, out_specs=out_specs, out_shape=out_shape,
        scratch_shapes=scratch, compiler_params=_params(dimension_semantics=("arbitrary",)),
    )(qkv, qkv, qkv, qkv, qkv, rot, rot, sinks, dout, *extra)


LEVELS = (32, 16, 8, 4, 2, 1)
SUBLANES = 8
UNROLL = 16
UNROLL_BWD = 8


def _lower_bound(lb_ref):
    l0, l1 = lb_ref[0:1, :], lb_ref[1:2, :]
    mx = jnp.maximum(l0, l1)
    e0, e1 = jnp.exp(l0 - mx), jnp.exp(l1 - mx)
    return e1 / (e0 + e1)


GROUPS = CHUNK // SUBLANES


def _group_roll(x, k):
    return pltpu.roll(x.reshape(GROUPS, SUBLANES, HGRN_DK), k % SUBLANES, 1).reshape(CHUNK, HGRN_DK)


def _scan_rows(x, row, reverse):
    r8 = row & (SUBLANES - 1)
    for sh in (1, 2, 4):
        ok = (r8 < SUBLANES - sh) if reverse else (r8 >= sh)
        x = x + jnp.where(ok, _group_roll(x, -sh if reverse else sh), 0.0)
    g = x.reshape(GROUPS, SUBLANES, HGRN_DK)
    edge = 0 if reverse else SUBLANES - 1
    tot = jnp.broadcast_to(g[:, edge:edge + 1, :], g.shape)

    def shifted(a, n):
        z = jnp.zeros((n, SUBLANES, HGRN_DK), F32)
        return jnp.concatenate([a[n:], z] if reverse else [z, a[:GROUPS - n]], axis=0)

    acc = shifted(tot, 1)
    for sh in (1, 2, 4):
        acc = acc + shifted(acc, sh)
    return (g + acc).reshape(CHUNK, HGRN_DK)


def _level_masks():
    t = lax.broadcasted_iota(jnp.int32, (CHUNK, CHUNK), 0)
    s = lax.broadcasted_iota(jnp.int32, (CHUNK, CHUNK), 1)
    return [((t & h) != 0) & ((s & h) == 0) & ((t ^ s) < 2 * h) for h in LEVELS]


def _level_scales(b, forget, row):
    out = []
    for h in LEVELS[:3]:
        parts = [jnp.broadcast_to(b[j * 2 * h + h - 1:j * 2 * h + h, :], (2 * h, HGRN_DK))
                 for j in range(CHUNK // (2 * h))]
        mid = parts[0] if len(parts) == 1 else jnp.concatenate(parts, axis=0)
        out.append(jnp.exp(-jnp.abs(b - mid)))
    f = forget
    up1, up2, up3 = _group_roll(f, -1), _group_roll(f, -2), _group_roll(f, -3)
    dn1, dn2, dn3 = _group_roll(f, 1), _group_roll(f, 2), _group_roll(f, 3)
    r8, r4 = row & 7, row & 3
    s2 = up1 * up2
    p2 = dn1 * f
    p3 = dn2 * p2
    below = jnp.where(r8 == 4, f, jnp.where(r8 == 5, p2, jnp.where(r8 == 6, p3, dn3 * p3)))
    above = jnp.where(r8 == 0, s2 * up3, jnp.where(r8 == 1, s2, jnp.where(r8 == 2, up1, 1.0)))
    e4 = jnp.where(r8 >= 4, below, above)
    e2 = jnp.where(r4 == 0, up1, jnp.where(r4 == 1, 1.0, jnp.where(r4 == 2, f, p2)))
    e1 = jnp.where((row & 1) == 1, f, 1.0)
    return out + [e4, e2, e1]


def _hgrn_gates(zq, zf, lb):
    sq = jax.nn.sigmoid(zq)
    q = zq * sq
    sg = jax.nn.sigmoid(zf)
    forget = lb + (1.0 - lb) * sg
    return q, sq, sg, forget, 1.0 - forget, jnp.log(forget)


def _hgrn_specs(T, rb, rev):
    nr = T // rb
    ri = (lambda r: nr - 1 - r) if rev else (lambda r: r)
    return nr, ri, [
        pl.BlockSpec((rb, HGRN_DK), lambda h, r: (ri(r), h)),
        pl.BlockSpec((rb, HGRN_DK), lambda h, r: (ri(r), HGRN_HEADS + h)),
        pl.BlockSpec((rb, HGRN_DK), lambda h, r: (ri(r), 2 * HGRN_HEADS + h)),
        pl.BlockSpec((2, HGRN_DK), lambda h, r: (0, h)),
    ]


def _hgrn_fwd(z, lb_raw, rb=2048, carry=(None, None)):
    T = z.shape[0]
    rb = min(rb, T)
    ncb = rb // CHUNK
    unroll = min(UNROLL, ncb)
    assert ncb % unroll == 0
    nr, ri, in_specs = _hgrn_specs(T, rb, False)

    def body(*refs):
        hh, rr = pl.program_id(0), pl.program_id(1)
        own, finish = _carried(carry, refs, 4, 2, (hh == 0) & (rr == 0), (hh == HGRN_HEADS - 1) & (rr == nr - 1))
        zq_ref, zf_ref, zi_ref, lb_ref, o_ref, st_ref, state = own

        @pl.when(rr == 0)
        def _():
            state[...] = jnp.zeros_like(state)

        lb = _lower_bound(lb_ref)
        row = lax.broadcasted_iota(jnp.int32, (CHUNK, HGRN_DK), 0)
        masks = _level_masks()

        def operands(c):
            rows = pl.ds(pl.multiple_of(c * CHUNK, CHUNK), CHUNK)
            q, _, _, forget, k, lf = _hgrn_gates(zq_ref[rows, :], zf_ref[rows, :], lb)
            v = zi_ref[rows, :]
            b = _scan_rows(lf, row, False)
            pairs = [((q * e).astype(BF16), (k * e).astype(BF16)) for e in _level_scales(b, forget, row)]
            b_last = b[CHUNK - 1:CHUNK, :]
            return dict(c=c, rows=rows, pairs=pairs, vb=v.astype(BF16), diag=jnp.sum(q * k, axis=-1, keepdims=True) * v,
                        kd=(k * jnp.exp(b_last - b)).astype(BF16), qd=(q * jnp.exp(b)).astype(BF16),
                        decay=jnp.exp(b_last))

        def group(i, st):
            parts = [operands(i * unroll + j) for j in range(unroll)]
            for p in parts:
                sc = jnp.zeros((CHUNK, CHUNK), F32)
                for (qs, ks), mask in zip(p["pairs"], masks):
                    sc = sc + jnp.where(mask, _dot_nt(qs, ks), 0.0)
                p["sc"] = sc.astype(BF16)
            for p in parts:
                p["o"] = _dot(p["sc"], p["vb"]) + p["diag"]
                p["gain"] = _dot_tn(p["vb"], p["kd"])
            for p in parts:
                st_ref[p["c"], 0] = st
                o_ref[p["rows"], :] = p["o"] + _dot_nt(p["qd"], st.astype(BF16))
                st = st * p["decay"] + p["gain"]
            return st

        state[...] = lax.fori_loop(0, ncb // unroll, group, state[...])
        finish()

    in_specs, out_specs, out_shape, scratch, extra = _carried_specs(
        carry, in_specs,
        [pl.BlockSpec((rb, HGRN_DK), lambda h, r: (r, h)),
         pl.BlockSpec((ncb, 1, HGRN_DK, HGRN_DK), lambda h, r: (r, h, 0, 0))],
        [jax.ShapeDtypeStruct((T, D_MODEL), F32),
         jax.ShapeDtypeStruct((T // CHUNK, HGRN_HEADS, HGRN_DK, HGRN_DK), F32)],
        [pltpu.VMEM((HGRN_DK, HGRN_DK), F32)])
    return pl.pallas_call(
        body, name="hgrn_fwd", grid=(HGRN_HEADS, nr), in_specs=in_specs, out_specs=out_specs, out_shape=out_shape,
        scratch_shapes=scratch, compiler_params=_params(dimension_semantics=("arbitrary", "arbitrary")),
    )(z, z, z, lb_raw, *extra)


def _hgrn_bwd(z, lb_raw, states, do, rb=2048, carry=(None, None)):
    T = z.shape[0]
    rb = min(rb, T)
    ncb = rb // CHUNK
    unroll = min(UNROLL_BWD, ncb)
    assert ncb % unroll == 0
    nr, ri, in_specs = _hgrn_specs(T, rb, True)
    in_specs += [pl.BlockSpec((ncb, 1, HGRN_DK, HGRN_DK), lambda h, r: (ri(r), h, 0, 0)),
                 pl.BlockSpec((rb, HGRN_DK), lambda h, r: (ri(r), h))]

    def body(*refs):
        hh, rr = pl.program_id(0), pl.program_id(1)
        own, finish = _carried(carry, refs, 6, 4, (hh == 0) & (rr == 0), (hh == HGRN_HEADS - 1) & (rr == nr - 1))
        zq_ref, zf_ref, zi_ref, lb_ref, st_ref, do_ref, dq_ref, df_ref, di_ref, dlb_ref, dstate = own

        @pl.when(rr == 0)
        def _():
            dstate[...] = jnp.zeros_like(dstate)
            dlb_ref[...] = jnp.zeros_like(dlb_ref)

        lb = _lower_bound(lb_ref)
        row = lax.broadcasted_iota(jnp.int32, (CHUNK, HGRN_DK), 0)
        masks = _level_masks()

        def operands(c):
            rows = pl.ds(pl.multiple_of(c * CHUNK, CHUNK), CHUNK)
            zq = zq_ref[rows, :]
            q, sq, sg, forget, k, lf = _hgrn_gates(zq, zf_ref[rows, :], lb)
            v = zi_ref[rows, :]
            dov = do_ref[rows, :]
            b = _scan_rows(lf, row, False)
            b_last = b[CHUNK - 1:CHUNK, :]
            eb, ebb = jnp.exp(b), jnp.exp(b_last - b)
            es = _level_scales(b, forget, row)
            return dict(rows=rows, zq=zq, q=q, sq=sq, sg=sg, forget=forget, k=k, v=v, dov=dov, eb=eb, ebb=ebb,
                        e_last=jnp.exp(b_last), es=es, st=st_ref[c, 0], dob=dov.astype(BF16), vb=v.astype(BF16),
                        pairs=[((q * e).astype(BF16), (k * e).astype(BF16)) for e in es],
                        qd=(q * eb).astype(BF16), kd=(k * ebb).astype(BF16))

        def group(i, dlb):
            parts = [operands(ncb - 1 - (i * unroll + j)) for j in range(unroll)]
            for p in parts:
                p["da"] = _dot_nt(p["dob"], p["vb"])
                sc = jnp.zeros((CHUNK, CHUNK), F32)
                for (qs, ks), mask in zip(p["pairs"], masks):
                    sc = sc + jnp.where(mask, _dot_nt(qs, ks), 0.0)
                p["sc"] = sc.astype(BF16)
                p["dq_state"] = _dot(p["dob"], p["st"].astype(BF16))
                p["gain"] = _dot_tn(p["dob"], p["qd"])
            dst = dstate[...]
            for p in parts:
                p["dst"] = dst
                dst = dst * p["e_last"] + p["gain"]
            dstate[...] = dst
            for p in parts:
                dstb = p["dst"].astype(BF16)
                dk_state = p["ebb"] * _dot(p["vb"], dstb)
                dq = p["eb"] * p["dq_state"]
                dk = dk_state
                dv = _dot_nt(p["kd"], dstb) + _dot_tn(p["sc"], p["dob"])
                for e, (qs, ks), mask in zip(p["es"], p["pairs"], masks):
                    dam = jnp.where(mask, p["da"], 0.0).astype(BF16)
                    dq = dq + e * _dot(dam, ks)
                    dk = dk + e * _dot_tn(dam, qs)
                dad = jnp.sum(p["dov"] * p["v"], axis=-1, keepdims=True)
                p["dq"] = dq + dad * p["k"]
                p["dk"] = dk + dad * p["q"]
                p["dv"] = dv + jnp.sum(p["q"] * p["k"], axis=-1, keepdims=True) * p["dov"]
                p["extra"] = (p["e_last"] * jnp.sum(p["dst"] * p["st"], axis=0, keepdims=True)
                              + jnp.sum(p["k"] * dk_state, axis=0, keepdims=True))
            for p in parts:
                q, k, sq, sg, zq, rows = p["q"], p["k"], p["sq"], p["sg"], p["zq"], p["rows"]
                dlf = _scan_rows(q * p["dq"] - k * p["dk"], row, True) + p["extra"]
                dforget = dlf / p["forget"] - p["dk"]
                dq_ref[rows, :] = (p["dq"] * (sq * (1.0 + zq * (1.0 - sq)))).astype(BF16)
                df_ref[rows, :] = (dforget * (1.0 - lb) * sg * (1.0 - sg)).astype(BF16)
                di_ref[rows, :] = p["dv"].astype(BF16)
                dlb = dlb + jnp.sum(dforget * (1.0 - sg), axis=0, keepdims=True)
            return dlb

        dlb_ref[...] += lax.fori_loop(0, ncb // unroll, group, jnp.zeros((1, HGRN_DK), F32))
        finish()

    blk = pl.BlockSpec((rb, HGRN_DK), lambda h, r: (ri(r), h))
    in_specs, out_specs, out_shape, scratch, extra = _carried_specs(
        carry, in_specs, [blk, blk, blk, pl.BlockSpec((1, HGRN_DK), lambda h, r: (0, h))],
        [jax.ShapeDtypeStruct((T, D_MODEL), BF16)] * 3 + [jax.ShapeDtypeStruct((1, D_MODEL), F32)],
        [pltpu.VMEM((HGRN_DK, HGRN_DK), F32)])
    return pl.pallas_call(
        body, name="hgrn_bwd", grid=(HGRN_HEADS, nr), in_specs=in_specs, out_specs=out_specs, out_shape=out_shape,
        scratch_shapes=scratch, compiler_params=_params(dimension_semantics=("arbitrary", "arbitrary")),
    )(z, z, z, lb_raw, states, do, *extra)


MESH = pl.DeviceIdType.MESH
ANY = pl.BlockSpec(memory_space=pl.ANY)


def _place():
    return lax.axis_index("x"), lax.axis_index("y"), lax.axis_index("c")


def _sems(n):
    return [pltpu.SemaphoreType.DMA((7 * n,)), pltpu.SemaphoreType.DMA((7 * n,)), pltpu.SemaphoreType.DMA((n,))]


class _Gather:
    def __init__(self, x_ref, out_ref, send_sems, recv_sems, local_sems, idx):
        self.x_ref, self.out_ref, self.send_sems, self.recv_sems, self.local_sem, self.base = (
            x_ref, out_ref, send_sems, recv_sems, local_sems.at[idx], 7 * idx)
        x, y, c = _place()
        self.c = c
        self.me, self.sibling = (x, y, c), (x, y, 1 - c)
        self.chips = [(1 - x, y), (x, 1 - y), (1 - x, 1 - y)]

    def rows(self, px, py, pc):
        return self.out_ref.at[4 * px + 2 * py + pc]

    def copy(self, k, block, to, from_input=False):
        return pltpu.make_async_remote_copy(
            src_ref=self.x_ref if from_input else self.rows(*block), dst_ref=self.rows(*block),
            send_sem=self.send_sems.at[self.base + k], recv_sem=self.recv_sems.at[self.base + k], device_id=to,
            device_id_type=MESH)

    def first(self):
        out = [self.copy(0, self.me, self.sibling, from_input=True)]
        return out + [self.copy(1 + j, self.me, (*chip, self.c), from_input=True) for j, chip in enumerate(self.chips)]

    def start(self):
        pltpu.make_async_copy(self.x_ref, self.rows(*self.me), self.local_sem).start()
        for cp in self.first():
            cp.start()

    def finish(self):
        passed = [self.copy(4 + j, (*chip, self.c), self.sibling) for j, chip in enumerate(self.chips)]
        for j, chip in enumerate(self.chips):
            self.copy(1 + j, (*chip, self.c), self.me).wait_recv()
            passed[j].start()
        self.copy(0, self.sibling, self.me).wait_recv()
        for j, chip in enumerate(self.chips):
            self.copy(4 + j, (*chip, 1 - self.c), self.me).wait_recv()
        for cp in self.first() + passed:
            cp.wait_send()
        pltpu.make_async_copy(self.x_ref, self.rows(*self.me), self.local_sem).wait()


class _Many:
    def __init__(self, kind, in_refs, out_refs, send_sems, recv_sems, local_sems):
        self.ops = [kind(x, o, send_sems, recv_sems, local_sems, i) for i, (x, o) in enumerate(zip(in_refs, out_refs))]

    def start(self):
        for op in self.ops:
            op.start()

    def finish(self):
        for op in self.ops:
            op.finish()


def _result_shapes(kind, arrs):
    return [jax.ShapeDtypeStruct(a.shape if kind is _Exchange else (N_DEV,) + a.shape, a.dtype) for a in arrs]


def _all_gather(name, shards):
    n = len(shards)

    def body(*refs):
        g = _Many(_Gather, refs[:n], refs[n:2 * n], *refs[2 * n:])
        g.start()
        g.finish()

    return pl.pallas_call(
        body, name=name, out_shape=_result_shapes(_Gather, shards), in_specs=[ANY] * n, out_specs=[ANY] * n,
        scratch_shapes=_sems(n),
    )(*shards)


def _peers(x, y, c):
    out = []
    for k in range(1, N_DEV):
        px = 1 - x if k & 4 else x
        py = 1 - y if k & 2 else y
        pc = 1 - c if k & 1 else c
        out.append((k, (px, py, pc), 4 * px + 2 * py + pc))
    return out


class _Exchange:
    def __init__(self, g_ref, recv_ref, send_sems, recv_sems, local_sems, idx):
        x, y, c = _place()
        me = 4 * x + 2 * y + c
        self.local = pltpu.make_async_copy(g_ref.at[me], recv_ref.at[me], local_sems.at[idx])
        self.copies = [
            pltpu.make_async_remote_copy(
                src_ref=g_ref.at[pidx], dst_ref=recv_ref.at[me], send_sem=send_sems.at[7 * idx + k - 1],
                recv_sem=recv_sems.at[7 * idx + k - 1], device_id=peer, device_id_type=MESH)
            for k, peer, pidx in _peers(x, y, c)]

    def start(self):
        self.local.start()
        for cp in self.copies:
            cp.start()

    def finish(self):
        for cp in self.copies:
            cp.wait()
        self.local.wait()


def _carried(carry, refs, n_in, n_out, first, last):
    kind, arrs = carry
    if kind is None:
        return refs, lambda: None
    n = len(arrs)
    ins, rest = refs[:n_in], refs[n_in + n:]
    outs, scratch = rest[:n_out], rest[n_out + n:]
    op = _Many(kind, refs[n_in:n_in + n], rest[n_out:n_out + n], *scratch[len(scratch) - 3:])

    @pl.when(first)
    def _():
        op.start()

    def finish():
        @pl.when(last)
        def _():
            op.finish()

    return tuple(ins) + tuple(outs) + tuple(scratch[:len(scratch) - 3]), finish


def _carried_specs(carry, in_specs, out_specs, out_shape, scratch):
    kind, arrs = carry
    if kind is None:
        return in_specs, out_specs, out_shape, scratch, []
    n = len(arrs)
    return (list(in_specs) + [ANY] * n, list(out_specs) + [ANY] * n,
            list(out_shape) + _result_shapes(kind, arrs), list(scratch) + _sems(n), list(arrs))


def _adamw(w, g, m, v):
    m = ADAM_B1 * m + (1.0 - ADAM_B1) * g
    v = ADAM_B2 * v + (1.0 - ADAM_B2) * (g * g)
    m_hat = m / (1.0 - ADAM_B1 ** ADAM_STEP)
    v_hat = v / (1.0 - ADAM_B2 ** ADAM_STEP)
    delta = -ADAM_LR * (m_hat / (jnp.sqrt(v_hat) + ADAM_EPS) + ADAM_WD * w)
    return delta, m, v


def _adamw_sum(name, recvs, w, m, v):
    L, R, C = w.shape
    tm = 128 if R % 128 == 0 else 64
    assert R % tm == 0 and len(recvs) == L

    def body(*refs):
        r_refs, (w_ref, m_ref, v_ref, g_ref, d_ref, nm_ref, nv_ref) = refs[:L], refs[L:]
        for l in range(L):
            g = r_refs[l][0].astype(F32)
            for s in range(1, N_DEV):
                g = g + r_refs[l][s].astype(F32)
            g_ref[l] = g
            d_ref[l], nm_ref[l], nv_ref[l] = _adamw(w_ref[l], g, m_ref[l], v_ref[l])

    blk = pl.BlockSpec((L, tm, C), lambda i: (0, i, 0))
    return pl.pallas_call(
        body, name=name, grid=(R // tm,),
        in_specs=[pl.BlockSpec((N_DEV, tm, C), lambda i: (0, i, 0))] * L + [blk, blk, blk],
        out_specs=[blk] * 4, out_shape=[jax.ShapeDtypeStruct((L, R, C), F32)] * 4,
        compiler_params=_params(dimension_semantics=("arbitrary",)),
    )(*recvs, w, m, v)


def _small_sync(part, w, m, v):
    def body(p_ref, w_ref, m_ref, v_ref, g_ref, d_ref, nm_ref, nv_ref, gath, send_sems, recv_sems):
        x, y, c = _place()
        me = 4 * x + 2 * y + c
        gath[me] = p_ref[...]
        copies = []
        for k, peer, _ in _peers(x, y, c):
            cp = pltpu.make_async_remote_copy(
                src_ref=p_ref, dst_ref=gath.at[me], send_sem=send_sems.at[k - 1], recv_sem=recv_sems.at[k - 1],
                device_id=peer, device_id_type=MESH)
            cp.start()
            copies.append(cp)
        for cp in copies:
            cp.wait()
        g = gath[0]
        for s in range(1, N_DEV):
            g = g + gath[s]
        wv = w_ref[...]
        l0, l1 = w_ref[8:9, :], w_ref[9:10, :]
        mx = jnp.maximum(l0, l1)
        e0, e1 = jnp.exp(l0 - mx), jnp.exp(l1 - mx)
        g9 = g[9:10, :] * (e0 / (e0 + e1)) * (e1 / (e0 + e1))
        row = lax.broadcasted_iota(jnp.int32, g.shape, 0)
        g = jnp.where(row == 9, g9, jnp.where(row == 8, -g9, g))
        g_ref[...] = g
        d_ref[...], nm_ref[...], nv_ref[...] = _adamw(wv, g, m_ref[...], v_ref[...])

    vm = pl.BlockSpec(memory_space=pltpu.VMEM)
    return pl.pallas_call(
        body, name="small_params_sync", in_specs=[vm] * 4, out_specs=[vm] * 4,
        out_shape=[jax.ShapeDtypeStruct(part.shape, F32)] * 4,
        scratch_shapes=[pltpu.VMEM((N_DEV,) + part.shape, F32), pltpu.SemaphoreType.DMA((7,)),
                        pltpu.SemaphoreType.DMA((7,))],
    )(part, w, m, v)


def _shards_bf16(d, pieces):
    return [d[name][layer].astype(BF16) for name, layer in pieces]


def _gathered(arrs, pieces, out):
    for a, (name, layer) in zip(arrs, pieces):
        out[name, layer] = a if name in COL_SHARDED else a.reshape(N_DEV * a.shape[1], a.shape[2])


def _pad_row(a, width=D_MODEL):
    a = a.reshape(1, -1)
    return jnp.pad(a, ((0, 0), (0, width - a.shape[1])))


LOSS_ROW = 11


def _pack_small(d, gn_full, loss=None):
    rows = [d["mix_norm"], d["mlp_norm"], d["final_norm"].reshape(1, D_MODEL),
            _pad_row(d["attn_b_qkv"], 2 * D_MODEL).reshape(2, D_MODEL), _pad_row(d["attn_sinks"]),
            d["hgrn_lower_bounds"], gn_full.reshape(1, D_MODEL)]
    if loss is not None:
        rows.append(_pad_row(loss))
    p = jnp.concatenate(rows, axis=0)
    return jnp.pad(p, ((0, SMALL_ROWS - p.shape[0]), (0, 0)))


def _unpack_small(p, me):
    return dict(
        mix_norm=p[0:2], mlp_norm=p[2:4], final_norm=p[4],
        attn_b_qkv=p[5:7].reshape(1, 2 * D_MODEL)[:, :QKV_DIM], attn_sinks=p[7:8, :N_Q_HEADS],
        hgrn_lower_bounds=p[8:10], hgrn_g_norm=lax.dynamic_slice(p[10:11], (0, me * 128), (1, 128)))


WEIGHT_NAMES = ['mix_norm', 'mlp_norm', 'final_norm', 'attn_w_qkv', 'attn_b_qkv', 'attn_sinks', 'attn_w_o', 'hgrn_w_in',
                'hgrn_g_norm', 'hgrn_w_o', 'hgrn_lower_bounds', 'mlp_w_up', 'mlp_w_down']
SMALL_NAMES = ('mix_norm', 'mlp_norm', 'final_norm', 'attn_b_qkv', 'attn_sinks', 'hgrn_lower_bounds', 'hgrn_g_norm')


def _rotary_tables(positions):
    inv_freq = ROPE_THETA ** (-jnp.arange(0, 2 * ROT_HALF, 2, dtype=F32) / (2 * ROT_HALF))
    ang = positions.astype(F32).reshape(-1, 1) * inv_freq
    cos, sin = jnp.cos(ang), jnp.sin(ang)
    r = jnp.arange(LANES) % HEAD_DIM
    idx = r % ROT_HALF
    c = jnp.where(r < 2 * ROT_HALF, cos[:, idx], 1.0)
    sa = jnp.where((r >= ROT_HALF) & (r < 2 * ROT_HALF), sin[:, idx], 0.0)
    sb = jnp.where(r < ROT_HALF, -sin[:, idx], 0.0)
    return jnp.concatenate([c, sa, sb], axis=1)


def kernel(x, positions, mix_norm, mlp_norm, final_norm, attn_w_qkv, attn_b_qkv, attn_sinks, attn_w_o, hgrn_w_in, hgrn_g_norm, hgrn_w_o, hgrn_lower_bounds, mlp_w_up, mlp_w_down, loss_target, m_mix_norm, m_mlp_norm, m_final_norm, m_attn_w_qkv, m_attn_b_qkv, m_attn_sinks, m_attn_w_o, m_hgrn_w_in, m_hgrn_g_norm, m_hgrn_w_o, m_hgrn_lower_bounds, m_mlp_w_up, m_mlp_w_down, v_mix_norm, v_mlp_norm, v_final_norm, v_attn_w_qkv, v_attn_b_qkv, v_attn_sinks, v_attn_w_o, v_hgrn_w_in, v_hgrn_g_norm, v_hgrn_w_o, v_hgrn_lower_bounds, v_mlp_w_up, v_mlp_w_down):
    w = dict(mix_norm=mix_norm, mlp_norm=mlp_norm, final_norm=final_norm, attn_w_qkv=attn_w_qkv, attn_b_qkv=attn_b_qkv,
             attn_sinks=attn_sinks, attn_w_o=attn_w_o, hgrn_w_in=hgrn_w_in, hgrn_g_norm=hgrn_g_norm, hgrn_w_o=hgrn_w_o,
             hgrn_lower_bounds=hgrn_lower_bounds, mlp_w_up=mlp_w_up, mlp_w_down=mlp_w_down)
    m = dict(mix_norm=m_mix_norm, mlp_norm=m_mlp_norm, final_norm=m_final_norm, attn_w_qkv=m_attn_w_qkv,
             attn_b_qkv=m_attn_b_qkv, attn_sinks=m_attn_sinks, attn_w_o=m_attn_w_o, hgrn_w_in=m_hgrn_w_in,
             hgrn_g_norm=m_hgrn_g_norm, hgrn_w_o=m_hgrn_w_o, hgrn_lower_bounds=m_hgrn_lower_bounds, mlp_w_up=m_mlp_w_up,
             mlp_w_down=m_mlp_w_down)
    v = dict(mix_norm=v_mix_norm, mlp_norm=v_mlp_norm, final_norm=v_final_norm, attn_w_qkv=v_attn_w_qkv,
             attn_b_qkv=v_attn_b_qkv, attn_sinks=v_attn_sinks, attn_w_o=v_attn_w_o, hgrn_w_in=v_hgrn_w_in,
             hgrn_g_norm=v_hgrn_g_norm, hgrn_w_o=v_hgrn_w_o, hgrn_lower_bounds=v_hgrn_lower_bounds, mlp_w_up=v_mlp_w_up,
             mlp_w_down=v_mlp_w_down)
    me = 4 * lax.axis_index("x") + 2 * lax.axis_index("y") + lax.axis_index("c")

    gn = hgrn_g_norm.reshape(1, 128)
    gn_a = gn.astype(BF16)
    gn_b = (gn - gn_a.astype(F32)).astype(BF16)
    gn_c = (gn - gn_a.astype(F32) - gn_b.astype(F32)).astype(BF16)
    gn_rows = jnp.pad(jnp.concatenate([gn_a, gn_b, gn_c], axis=1), ((0, 15), (0, D_MODEL - 3 * 128)))
    full = {}
    got = _all_gather("gather_attn_weights", _shards_bf16(w, GATHER_FIRST) + [gn_rows])
    _gathered(got[:1], GATHER_FIRST, full)
    w_qkv = full["attn_w_qkv", 0].transpose(1, 0, 2).reshape(D_MODEL, QKV_DIM)
    gn_terms = got[1][:, 0, :3 * 128].astype(F32).reshape(N_DEV, 3, 128)
    gn_full = ((gn_terms[:, 0] + gn_terms[:, 1]) + gn_terms[:, 2]).reshape(1, D_MODEL)

    x0 = x[0]
    tgt = loss_target[0]
    rot = _rotary_tables(positions)
    row = lambda a: a.reshape(1, -1)

    qkv, h0 = _norm_mm("qkv_proj", x0, row(mix_norm[0]), w_qkv, attn_b_qkv, rot=rot)
    att, *got = _attn_fwd(qkv, attn_sinks, carry=(_Gather, _shards_bf16(w, GATHER_ATTN)))
    _gathered(got, GATHER_ATTN, full)
    x1 = _mm_res("attn_out_proj", att, full["attn_w_o", 0], x0)
    u0, h1, *got = _norm_mm("mlp0_up", x1, row(mlp_norm[0]), full["mlp_w_up", 0],
                            carry=(_Gather, _shards_bf16(w, GATHER_MLP0)))
    _gathered(got, GATHER_MLP0, full)
    x2, a0 = _mlp_down("mlp0_down", u0, full["mlp_w_down", 0], x1)
    z, h2 = _norm_mm("hgrn_in_proj", x2, row(mix_norm[1]), full["hgrn_w_in", 0])
    o_raw, states, *got = _hgrn_fwd(z, hgrn_lower_bounds, carry=(_Gather, _shards_bf16(w, GATHER_HGRN)))
    _gathered(got, GATHER_HGRN, full)
    x3, o2 = _hgrn_out("hgrn_out_proj", o_raw, z, gn_full, full["hgrn_w_o", 0], x2)
    u1, h3 = _norm_mm("mlp1_up", x3, row(mlp_norm[1]), full["mlp_w_up", 1])
    dx4, a1, loss_part, g_final = _mlp_down("mlp1_down_loss", u1, full["mlp_w_down", 1], x3,
                                            loss_head=(tgt, row(final_norm)))

    gw = {}
    du1, = _mlp_bwd_act("mlp1_bwd_act", dx4, u1, full["mlp_w_down", 1])
    dx3, g_mlp1 = _mm_nt_rmsbwd("mlp1_bwd_in", du1, full["mlp_w_up", 1], x3, row(mlp_norm[1]), dx4)
    gw["mlp_w_down", 1] = _mm_tn("mlp1_dw_down", a1, dx4, "rows")
    gw["mlp_w_up", 1] = _mm_tn("mlp1_dw_up", h3, du1, "cols")

    do_raw, dg, g_gn = _hgrn_out_bwd("hgrn_out_bwd", dx3, o_raw, z, full["hgrn_w_o", 0], gn_full)
    gw["hgrn_w_o", 0] = _mm_tn("hgrn_dw_o", o2, dx3, "rows")
    recvs = {}
    dzq, dzf, dzi, g_lb, *recv = _hgrn_bwd(z, hgrn_lower_bounds, states, do_raw,
                                           carry=(_Exchange, [gw[p] for p in GRADS_HGRN]))
    recvs.update(zip(GRADS_HGRN, recv))
    dz = [dzq, dzf, dzi, dg]
    dx2, g_mix1 = _mm_nt_rmsbwd("hgrn_in_bwd", dz, full["hgrn_w_in", 0], x2, row(mix_norm[1]), dx3)
    gw["hgrn_w_in", 0] = jnp.concatenate(
        [_mm_tn(f"hgrn_dw_in{j}", h2, d, "cols") for j, d in enumerate(dz)], axis=0)

    du0, = _mlp_bwd_act("mlp0_bwd_act", dx2, u0, full["mlp_w_down", 0])
    dx1, g_mlp0 = _mm_nt_rmsbwd("mlp0_bwd_in", du0, full["mlp_w_up", 0], x1, row(mlp_norm[0]), dx2)
    gw["mlp_w_down", 0], recvs["hgrn_w_in", 0] = _mm_tn(
        "mlp0_dw_down", a0, dx2, "rows", carry=(_Exchange, [gw["hgrn_w_in", 0]]))
    gw["mlp_w_up", 0], recvs["mlp_w_down", 0] = _mm_tn(
        "mlp0_dw_up", h1, du0, "cols", carry=(_Exchange, [gw["mlp_w_down", 0]]))

    datt = _mm_nt("attn_out_bwd", dx1, full["attn_w_o", 0], BF16)
    gw["attn_w_o", 0] = _mm_tn("attn_dw_o", att, dx1, "rows")
    dqkv, g_sink, *recv = _attn_bwd(qkv, rot, attn_sinks, datt, carry=(_Exchange, [gw[p] for p in GRADS_ATTN]))
    recvs.update(zip(GRADS_ATTN, recv))
    g_qkv = _mm_tn("attn_dw_qkv", h0, dqkv, bn=512)
    g_qkv = g_qkv.reshape(D_MODEL, N_DEV, QKV_DIM // N_DEV).transpose(1, 0, 2).astype(BF16)
    dx0, g_mix0, g_bqkv, recvs["attn_w_qkv", 0] = _mm_nt_rmsbwd(
        "qkv_bwd", dqkv, w_qkv, x0, row(mix_norm[0]), dx1, with_colsum=True, carry=(_Exchange, [g_qkv]))

    big = {name: _adamw_sum("adamw_" + name, [recvs[name, l] for l in range(w[name].shape[0])], w[name], m[name], v[name])
           for name in BIG_NAMES}

    zero_row = jnp.zeros((1, D_MODEL), F32)
    part = _pack_small(dict(
        mix_norm=jnp.concatenate([g_mix0, g_mix1], axis=0), mlp_norm=jnp.concatenate([g_mlp0, g_mlp1], axis=0),
        final_norm=g_final, attn_b_qkv=g_bqkv, attn_sinks=g_sink[:, :N_Q_HEADS],
        hgrn_lower_bounds=jnp.concatenate([zero_row, g_lb], axis=0)), g_gn, loss=loss_part)

    def spread(a):
        return lax.dynamic_update_slice(zero_row, a.reshape(1, 128), (0, me * 128))

    small_in = [_pack_small({n: d[n] for n in SMALL_NAMES if n != "hgrn_g_norm"}, spread(d["hgrn_g_norm"]))
                for d in (w, m, v)]
    synced = _small_sync(part, *small_in)
    small = [_unpack_small(p, me) for p in synced]

    outs = [synced[0][LOSS_ROW, 0], dx0.reshape(x.shape)]
    for kind, grp_small in enumerate(small):
        for name in WEIGHT_NAMES:
            val = grp_small[name] if name in SMALL_NAMES else big[name][kind]
            outs.append(val.reshape(w[name].shape))
    return tuple(outs)
```

```python
import functools

import jax
import jax.numpy as jnp
from jax import lax
from jax.experimental import pallas as pl
from jax.experimental.pallas import tpu as pltpu

F32 = jnp.float32
BF16 = jnp.bfloat16

D_MODEL = 1024
HEAD_DIM = 64
N_Q_HEADS = 16
Q_DIM = 1024
KV_DIM = 256
QKV_DIM = 1536
ATT_BLOCK = 128
ROT_HALF = 8
ROPE_THETA = 500000.0
NEG_INF = -1e30
HGRN_HEADS = 8
HGRN_DK = 128
CHUNK = 64
D_FF = 4096
NORM_EPS = 1e-5
N_DEV = 8

ADAM_LR = 0.001
ADAM_B1 = 0.9
ADAM_B2 = 0.999
ADAM_EPS = 1e-08
ADAM_WD = 0.01
ADAM_STEP = 10

LANES = 128
VMEM_LIMIT = 56 * 1024 * 1024

GATHER_FIRST = (("attn_w_qkv", 0),)
GATHER_ATTN = (("attn_w_o", 0), ("mlp_w_up", 0), ("mlp_w_down", 0))
GATHER_MLP0 = (("hgrn_w_in", 0), ("hgrn_w_o", 0))
GATHER_HGRN = (("mlp_w_up", 1), ("mlp_w_down", 1))
GRADS_HGRN = (("mlp_w_down", 1), ("mlp_w_up", 1), ("hgrn_w_o", 0))
GRADS_ATTN = (("mlp_w_up", 0), ("attn_w_o", 0))
COL_SHARDED = ("attn_w_qkv", "hgrn_w_in", "mlp_w_up")
BIG_NAMES = ("attn_w_qkv", "attn_w_o", "hgrn_w_in", "hgrn_w_o", "mlp_w_up", "mlp_w_down")
SMALL_ROWS = 16


def _dot(a, b):
    return jnp.dot(a, b, preferred_element_type=F32)


def _dot_nt(a, b):
    return lax.dot_general(a, b, (((1,), (1,)), ((), ())), preferred_element_type=F32)


def _dot_tn(a, b):
    return lax.dot_general(a, b, (((0,), (0,)), ((), ())), preferred_element_type=F32)


def _params(**kw):
    return pltpu.CompilerParams(vmem_limit_bytes=VMEM_LIMIT, **kw)


def _full_spec(a):
    nd = a.ndim
    return pl.BlockSpec(a.shape, lambda *_: (0,) * nd)


def _row_call(name, body, n_rows, tm, row_ins, full_ins, row_outs, acc_outs=(), carry=(None, None)):
    steps = n_rows // tm
    in_specs = [pl.BlockSpec((tm, w), functools.partial(lambda i, cb: (i, cb), cb=cb)) for _, w, cb in row_ins]
    in_specs += [_full_spec(a) for a in full_ins]
    out_shape = [jax.ShapeDtypeStruct((n_rows, w), dt) for w, dt in row_outs]
    out_specs = [pl.BlockSpec((tm, w), lambda i: (i, 0)) for w, _ in row_outs]
    for shp, dt in acc_outs:
        out_shape.append(jax.ShapeDtypeStruct(shp, dt))
        out_specs.append(pl.BlockSpec(shp, functools.partial(lambda i, nd: (0,) * nd, nd=len(shp))))
    n_in, n_out = len(in_specs), len(out_specs)
    in_specs, out_specs, out_shape, scratch, extra = _carried_specs(carry, in_specs, out_specs, out_shape, [])

    def wrapped(*refs):
        i = pl.program_id(0)
        own, finish = _carried(carry, refs, n_in, n_out, i == 0, i == steps - 1)
        body(*own)
        finish()

    return pl.pallas_call(
        wrapped, name=name, grid=(steps,), in_specs=in_specs, out_specs=out_specs, out_shape=out_shape,
        scratch_shapes=scratch, compiler_params=_params(dimension_semantics=("arbitrary",)),
    )(*[a for a, _, _ in row_ins], *full_ins, *extra)


def _rms(x, gain):
    r = lax.rsqrt(jnp.mean(x * x, axis=-1, keepdims=True) + NORM_EPS)
    xhat = x * r
    return xhat * gain, xhat, r


def _rms_bwd(dy, xhat, r, gain):
    dxhat = dy * gain
    dx = r * (dxhat - xhat * jnp.mean(dxhat * xhat, axis=-1, keepdims=True))
    return dx, dy * xhat


def _norm_mm(name, x, gain, w, bias=None, rot=None, tm=512, carry=(None, None)):
    T = x.shape[0]
    tm = min(tm, T)
    nc = 512
    blocked = w.ndim == 3
    n = N_DEV * w.shape[2] if blocked else w.shape[1]
    assert n % nc == 0 and (not blocked or w.shape[2] == nc)

    def body(*refs):
        x_ref, refs = refs[0], refs[1:]
        if rot is not None:
            t_ref, refs = refs[0], refs[1:]
        g_ref, w_ref, refs = refs[0], refs[1], refs[2:]
        if bias is not None:
            b_ref, refs = refs[0], refs[1:]
        y_ref, h_ref = refs
        h, _, _ = _rms(x_ref[...], g_ref[...])
        hb = h.astype(BF16)
        h_ref[...] = hb
        for c in range(n // nc):
            sl = slice(c * nc, (c + 1) * nc)
            y = _dot(hb, w_ref[c] if blocked else w_ref[:, sl])
            if bias is not None:
                y = y + b_ref[:, sl]
            if rot is None:
                y_ref[:, sl] = y
            else:
                n_rot = max(0, min(nc, Q_DIM + KV_DIM - c * nc)) // LANES
                pieces = _rot_fwd(y[:, :n_rot * LANES], t_ref[...]) if n_rot else []
                for j in range(nc // LANES):
                    col = slice(c * nc + j * LANES, c * nc + (j + 1) * LANES)
                    y_ref[:, col] = pieces[j] if j < n_rot else y[:, j * LANES:(j + 1) * LANES]

    rows = [(x, D_MODEL, 0)] + ([(rot, 3 * LANES, 0)] if rot is not None else [])
    full = [gain, w] + ([bias] if bias is not None else [])
    return _row_call(name, body, T, tm, rows, full, [(n, F32), (D_MODEL, BF16)], carry=carry)


def _mm_res(name, a, w, res, tm=512):
    T = a.shape[0]
    tm = min(tm, T)

    def body(a_ref, r_ref, w_ref, o_ref):
        o_ref[...] = r_ref[...] + _dot(a_ref[...], w_ref[...])

    return _row_call(name, body, T, tm, [(a, a.shape[1], 0), (res, D_MODEL, 0)], [w], [(D_MODEL, F32)])[0]


def _mlp_down(name, u, w, res, tm=512, loss_head=None):
    T = u.shape[0]
    tm = min(tm, T)
    kc = 1024
    sub = min(256, tm)

    def body(*refs):
        if loss_head is None:
            u_ref, r_ref, w_ref, o_ref, a_ref = refs
        else:
            u_ref, r_ref, t_ref, w_ref, g_ref, o_ref, a_ref, loss_ref, dg_ref = refs

            @pl.when(pl.program_id(0) == 0)
            def _():
                loss_ref[...] = jnp.zeros_like(loss_ref)
                dg_ref[...] = jnp.zeros_like(dg_ref)

        for r0 in range(0, tm, sub):
            rs = slice(r0, r0 + sub)
            acc = r_ref[rs, :]
            for c in range(D_FF // kc):
                sl = slice(c * kc, (c + 1) * kc)
                a = jnp.maximum(u_ref[rs, sl], 0.0)
                ab = (a * a).astype(BF16)
                a_ref[rs, sl] = ab
                acc = acc + _dot(ab, w_ref[sl, :])
            if loss_head is None:
                o_ref[rs, :] = acc
            else:
                gain_v = g_ref[...]
                y, xhat, r = _rms(acc, gain_v)
                diff = y - t_ref[rs, :]
                per_row = jnp.sum(diff * diff, axis=-1, keepdims=True) * (1.0 / D_MODEL)
                loss_ref[...] += jnp.broadcast_to(0.5 * jnp.sum(per_row, axis=0, keepdims=True), loss_ref.shape)
                dx, dgr = _rms_bwd(diff * (1.0 / D_MODEL), xhat, r, gain_v)
                o_ref[rs, :] = dx
                dg_ref[...] += jnp.sum(dgr, axis=0, keepdims=True)

    rows, full, acc_outs = [(u, D_FF, 0), (res, D_MODEL, 0)], [w], []
    if loss_head is not None:
        rows, full = rows + [(loss_head[0], D_MODEL, 0)], full + [loss_head[1]]
        acc_outs = [((1, LANES), F32), ((1, D_MODEL), F32)]
    return _row_call(name, body, T, tm, rows, full, [(D_MODEL, F32), (D_FF, BF16)], acc_outs)


def _hgrn_out(name, o_raw, z, gn, w, res, tm=512):
    T = o_raw.shape[0]
    tm = min(tm, T)

    def body(o_ref, g_ref, r_ref, gn_ref, w_ref, x_ref, a_ref):
        y, _, _ = _rms(o_ref[...], gn_ref[...])
        g = g_ref[...]
        a = (y * (g * jax.nn.sigmoid(g))).astype(BF16)
        a_ref[...] = a
        x_ref[...] = r_ref[...] + _dot(a, w_ref[...])

    return _row_call(name, body, T, tm, [(o_raw, D_MODEL, 0), (z, D_MODEL, 3), (res, D_MODEL, 0)], [gn, w],
                     [(D_MODEL, F32), (D_MODEL, BF16)])


def _mm_nt_rmsbwd(name, dy, w, x, gain, dres, tm=512, with_colsum=False, carry=(None, None)):
    T = x.shape[0]
    tm = min(tm, T)
    dys = list(dy) if isinstance(dy, (list, tuple)) else [dy]
    width = dys[0].shape[1]
    n = width * len(dys)
    sub = min(256, tm)
    assert not with_colsum or len(dys) == 1

    def body(*refs):
        dy_refs, refs = refs[:len(dys)], refs[len(dys):]
        if with_colsum:
            x_ref, dr_ref, w_ref, g_ref, dx_ref, dg_ref, cs_ref = refs
        else:
            x_ref, dr_ref, w_ref, g_ref, dx_ref, dg_ref = refs

        @pl.when(pl.program_id(0) == 0)
        def _():
            dg_ref[...] = jnp.zeros_like(dg_ref)
            if with_colsum:
                cs_ref[...] = jnp.zeros_like(cs_ref)

        gain_v = g_ref[...]
        for r0 in range(0, tm, sub):
            rs = slice(r0, r0 + sub)
            if w.ndim == 3:
                nb = w.shape[2]
                dh = None
                for p in range(N_DEV):
                    piece, off = divmod(p * nb, width)
                    part = _dot_nt(dy_refs[piece][rs, off:off + nb].astype(BF16), w_ref[p])
                    dh = part if dh is None else dh + part
            else:
                dh = _dot_nt(dy_refs[0][rs, :].astype(BF16), w_ref[...])
            _, xhat, r = _rms(x_ref[rs, :], gain_v)
            dx, dgr = _rms_bwd(dh, xhat, r, gain_v)
            dx_ref[rs, :] = dr_ref[rs, :] + dx
            dg_ref[...] += jnp.sum(dgr, axis=0, keepdims=True)
            if with_colsum:
                cs_ref[...] += jnp.sum(dy_refs[0][rs, :].astype(F32), axis=0, keepdims=True)

    acc = [((1, D_MODEL), F32)] + ([((1, n), F32)] if with_colsum else [])
    rows = [(d, width, 0) for d in dys] + [(x, D_MODEL, 0), (dres, D_MODEL, 0)]
    return _row_call(name, body, T, tm, rows, [w, gain], [(D_MODEL, F32)], acc, carry=carry)


def _mm_nt(name, dy, w, out_dtype, tm=512):
    T = dy.shape[0]
    tm = min(tm, T)
    k = w.shape[0]

    def body(dy_ref, w_ref, o_ref):
        o_ref[...] = _dot_nt(dy_ref[...].astype(BF16), w_ref[...]).astype(out_dtype)

    return _row_call(name, body, T, tm, [(dy, dy.shape[1], 0)], [w], [(k, out_dtype)])[0]


def _mlp_bwd_act(name, dy, u, w_down, tm=512, carry=(None, None)):
    T = u.shape[0]
    tm = min(tm, T)
    kc = 1024

    def body(dy_ref, u_ref, w_ref, du_ref):
        dyb = dy_ref[...].astype(BF16)
        for c in range(D_FF // kc):
            sl = slice(c * kc, (c + 1) * kc)
            da = _dot_nt(dyb, w_ref[sl, :])
            du_ref[:, sl] = (da * (2.0 * jnp.maximum(u_ref[:, sl], 0.0))).astype(BF16)

    return _row_call(name, body, T, tm, [(dy, D_MODEL, 0), (u, D_FF, 0)], [w_down], [(D_FF, BF16)], carry=carry)


def _hgrn_out_bwd(name, dx, o_raw, z, w, gn, tm=512):
    T = dx.shape[0]
    tm = min(tm, T)

    def body(dx_ref, o_ref, g_ref, w_ref, gn_ref, do_ref, dg_ref, dgn_ref):
        @pl.when(pl.program_id(0) == 0)
        def _():
            dgn_ref[...] = jnp.zeros_like(dgn_ref)

        da = _dot_nt(dx_ref[...].astype(BF16), w_ref[...])
        gn_v = gn_ref[...]
        y, xhat, r = _rms(o_ref[...], gn_v)
        g = g_ref[...]
        sg = jax.nn.sigmoid(g)
        dg_ref[...] = (da * y * (sg * (1.0 + g * (1.0 - sg)))).astype(BF16)
        dyn = da * (g * sg)
        do, dgr = _rms_bwd(dyn, xhat, r, gn_v)
        do_ref[...] = do
        dgn_ref[...] += jnp.sum(dgr, axis=0, keepdims=True)

    return _row_call(name, body, T, tm, [(dx, D_MODEL, 0), (o_raw, D_MODEL, 0), (z, D_MODEL, 3)], [w, gn],
                     [(D_MODEL, F32), (D_MODEL, BF16)], [((1, D_MODEL), F32)])


COL_BLOCK = D_FF // N_DEV


def _mm_tn(name, a, b, shard=None, bm=1024, bn=1024, tk=2048, carry=(None, None)):
    T, M = a.shape
    N = b.shape[1]
    bm, bn, tk = min(bm, M), min(bn, N), min(tk, T)
    nk = T // tk
    if shard is None:
        out_shape, out_block = jax.ShapeDtypeStruct((M, N), F32), (bm, bn)
        out_map = lambda i, j, k: (i, j)
    elif shard == "cols":
        assert bn % COL_BLOCK == 0 and N % bn == 0
        out_shape = jax.ShapeDtypeStruct((N // COL_BLOCK, M, COL_BLOCK), BF16)
        out_block = (bn // COL_BLOCK, bm, COL_BLOCK)
        out_map = lambda i, j, k: (j, i, 0)
    else:
        rows = M // N_DEV
        assert bm % rows == 0
        out_shape, out_block = jax.ShapeDtypeStruct((N_DEV, rows, N), BF16), (bm // rows, rows, bn)
        out_map = lambda i, j, k: (i, 0, j)

    grid = (M // bm, N // bn, nk)

    def body(*refs):
        i, j, k = pl.program_id(0), pl.program_id(1), pl.program_id(2)
        own, finish = _carried(carry, refs, 2, 1, (i == 0) & (j == 0) & (k == 0),
                               (i == grid[0] - 1) & (j == grid[1] - 1) & (k == nk - 1))
        a_ref, b_ref, o_ref, acc = own

        @pl.when(k == 0)
        def _():
            acc[...] = jnp.zeros_like(acc)

        acc[...] += _dot_tn(a_ref[...].astype(BF16), b_ref[...].astype(BF16))

        @pl.when(k == nk - 1)
        def _():
            if shard == "cols":
                for c in range(bn // COL_BLOCK):
                    o_ref[c] = acc[:, c * COL_BLOCK:(c + 1) * COL_BLOCK].astype(BF16)
            else:
                o_ref[...] = acc[...].reshape(out_block).astype(o_ref.dtype)

        finish()

    in_specs, out_specs, out_shapes, scratch, extra = _carried_specs(
        carry, [pl.BlockSpec((tk, bm), lambda i, j, k: (k, i)), pl.BlockSpec((tk, bn), lambda i, j, k: (k, j))],
        [pl.BlockSpec(out_block, out_map)], [out_shape], [pltpu.VMEM((bm, bn), F32)])
    res = pl.pallas_call(
        body, name=name, grid=grid, in_specs=in_specs, out_specs=out_specs, out_shape=out_shapes,
        scratch_shapes=scratch, compiler_params=_params(dimension_semantics=("arbitrary", "arbitrary", "arbitrary")),
    )(a, b, *extra)
    return res[0] if carry[0] is None else res


def _rot_fwd(x, tab):
    c, sa, sb = tab[:, :LANES], tab[:, LANES:2 * LANES], tab[:, 2 * LANES:]
    outs = []
    for j in range(x.shape[1] // LANES):
        xs = x[:, j * LANES:(j + 1) * LANES]
        outs.append(xs * c + pltpu.roll(xs, ROT_HALF, 1) * sa + pltpu.roll(xs, LANES - ROT_HALF, 1) * sb)
    return outs


def _rot_bwd(dys, tab):
    c, sa, sb = tab[:, :LANES], tab[:, LANES:2 * LANES], tab[:, 2 * LANES:]
    return [dy * c + pltpu.roll(dy * sa, LANES - ROT_HALF, 1) + pltpu.roll(dy * sb, ROT_HALF, 1) for dy in dys]


ATT_SCALE = HEAD_DIM ** -0.5
HEAD_LAG = 2


def _attn_masks(n):
    kj = lax.broadcasted_iota(jnp.int32, (2 * ATT_BLOCK, ATT_BLOCK), 0)
    qi = lax.broadcasted_iota(jnp.int32, (2 * ATT_BLOCK, ATT_BLOCK), 1)
    delta = qi + ATT_BLOCK - kj
    first_key = jnp.where(n > 0, 0, ATT_BLOCK)
    valid = (delta >= 0) & (delta < ATT_BLOCK) & (kj >= first_key)
    low = lax.broadcasted_iota(jnp.int32, (1, LANES), 1) < HEAD_DIM
    upper = lax.broadcasted_iota(jnp.int32, (LANES, 1), 0) < HEAD_DIM
    return valid, low, upper


def _softmax_sink(s, valid, sink):
    s = jnp.where(valid, s, NEG_INF)
    m = jnp.maximum(jnp.max(s, axis=0, keepdims=True), sink)
    e = jnp.exp(s - m)
    es = jnp.exp(sink - m)
    inv = 1.0 / (jnp.sum(e, axis=0, keepdims=True) + es)
    return e * inv, es * inv


def _attn_specs(nb, tables):
    prev = lambda n: jnp.maximum(jnp.minimum(n, nb - 1) - 1, 0)
    cur = lambda n: jnp.minimum(n, nb - 1)
    specs = [
        pl.BlockSpec((ATT_BLOCK, Q_DIM), lambda n: (cur(n), 0)),
        pl.BlockSpec((ATT_BLOCK, KV_DIM), lambda n: (prev(n), 4)),
        pl.BlockSpec((ATT_BLOCK, KV_DIM), lambda n: (cur(n), 4)),
        pl.BlockSpec((ATT_BLOCK, KV_DIM), lambda n: (prev(n), 5)),
        pl.BlockSpec((ATT_BLOCK, KV_DIM), lambda n: (cur(n), 5)),
    ]
    if tables:
        specs += [pl.BlockSpec((ATT_BLOCK, 3 * LANES), lambda n: (prev(n), 0)),
                  pl.BlockSpec((ATT_BLOCK, 3 * LANES), lambda n: (cur(n), 0))]
    return specs + [pl.BlockSpec(memory_space=pltpu.SMEM)]


def _kv_band(prev_ref, cur_ref):
    out = []
    for j in range(KV_DIM // LANES):
        sl = slice(j * LANES, (j + 1) * LANES)
        band = jnp.concatenate([prev_ref[:, sl], cur_ref[:, sl]], axis=0)
        out.append((band, pltpu.roll(band, HEAD_DIM, 1)))
    return out


def _bf16(bands, transposed=False):
    return [[(a.T if transposed else a).astype(BF16) for a in pair] for pair in bands]


def _attn_fwd(qkv, sinks, carry=(None, None)):
    T = qkv.shape[0]
    nb = T // ATT_BLOCK

    def body(*refs):
        n = pl.program_id(0)
        own, finish = _carried(carry, refs, 6, 1, n == 0, n == nb - 1)
        q_ref, kp_ref, kc_ref, vp_ref, vc_ref, sink_ref, o_ref = own
        valid, low, upper = _attn_masks(n)
        ks = _bf16(_kv_band(kp_ref, kc_ref))
        vts = _bf16(_kv_band(vp_ref, vc_ref), transposed=True)
        heads, outs = {}, {}

        def first(h):
            p, hf = h // 2, h % 2
            kpair, khalf = p // 4, (p // 2) % 2
            qm = jnp.where(low if hf == 0 else ~low, q_ref[:, p * LANES:(p + 1) * LANES] * ATT_SCALE, 0.0)
            sw = 0 if khalf == hf else 1
            heads[h] = (kpair, sw, _dot_nt(ks[kpair][sw], qm.astype(BF16)))

        def second(h):
            kpair, sw, s = heads[h]
            heads[h] = (kpair, sw, _softmax_sink(s, valid, sink_ref[0, h])[0].astype(BF16))

        def third(h):
            kpair, sw, pr = heads.pop(h)
            outs[h] = _dot(vts[kpair][sw], pr)
            if h % 2:
                o_ref[:, (h // 2) * LANES:(h // 2 + 1) * LANES] = jnp.where(upper, outs.pop(h - 1), outs.pop(h)).T.astype(BF16)

        for i in range(N_Q_HEADS + 2 * HEAD_LAG):
            if i < N_Q_HEADS:
                first(i)
            if 0 <= i - HEAD_LAG < N_Q_HEADS:
                second(i - HEAD_LAG)
            if 0 <= i - 2 * HEAD_LAG < N_Q_HEADS:
                third(i - 2 * HEAD_LAG)
        finish()

    in_specs, out_specs, out_shape, scratch, extra = _carried_specs(
        carry, _attn_specs(nb, False), [pl.BlockSpec((ATT_BLOCK, Q_DIM), lambda n: (n, 0))],
        [jax.ShapeDtypeStruct((T, Q_DIM), BF16)], [])
    return pl.pallas_call(
        body, name="attn_fwd", grid=(nb,), in_specs=in_specs, out_specs=out_specs, out_shape=out_shape,
        scratch_shapes=scratch, compiler_params=_params(dimension_semantics=("arbitrary",)),
    )(qkv, qkv, qkv, qkv, qkv, sinks, *extra)


def _attn_bwd(qkv, rot, sinks, dout, carry=(None, None)):
    T = qkv.shape[0]
    nb = T // ATT_BLOCK
    npair = KV_DIM // LANES

    def body(*refs):
        n = pl.program_id(0)
        own, finish = _carried(carry, refs, 9, 2, n == 0, n == nb)
        (q_ref, kp_ref, kc_ref, vp_ref, vc_ref, tp_ref, tc_ref, sink_ref, do_ref, dqkv_ref, dsink_ref,
         dq_c, dk_c, dv_c) = own

        @pl.when(n == 0)
        def _():
            dq_c[...] = jnp.zeros_like(dq_c)
            dk_c[...] = jnp.zeros_like(dk_c)
            dv_c[...] = jnp.zeros_like(dv_c)
            dsink_ref[...] = jnp.zeros_like(dsink_ref)

        def flush(dk_prev, dv_prev, tab_ref):
            dqkv_ref[:, :Q_DIM] = dq_c[...].astype(BF16)
            dk = _rot_bwd([dk_c[:, j * LANES:(j + 1) * LANES] + dk_prev[j] for j in range(npair)], tab_ref[...])
            for j in range(npair):
                dqkv_ref[:, Q_DIM + j * LANES:Q_DIM + (j + 1) * LANES] = dk[j].astype(BF16)
                dqkv_ref[:, Q_DIM + KV_DIM + j * LANES:Q_DIM + KV_DIM + (j + 1) * LANES] = (
                    dv_c[:, j * LANES:(j + 1) * LANES] + dv_prev[j]).astype(BF16)

        @pl.when(n < nb)
        def _():
            valid, low, upper = _attn_masks(n)
            lane = lax.broadcasted_iota(jnp.int32, (1, LANES), 1)
            k_band = _kv_band(kp_ref, kc_ref)
            ks, kts = _bf16(k_band), _bf16(k_band, transposed=True)
            vs = _bf16(_kv_band(vp_ref, vc_ref))
            dk_acc = [[jnp.zeros((2 * ATT_BLOCK, LANES), F32) for _ in range(2)] for _ in range(npair)]
            dv_acc = [[jnp.zeros((2 * ATT_BLOCK, LANES), F32) for _ in range(2)] for _ in range(npair)]
            dsink = jnp.zeros((1, LANES), F32)
            heads, dq_t, dsinks = {}, {}, []

            def first(h):
                p, hf = h // 2, h % 2
                kpair, khalf = p // 4, (p // 2) % 2
                sel = low if hf == 0 else ~low
                qm = jnp.where(sel, q_ref[:, p * LANES:(p + 1) * LANES] * ATT_SCALE, 0.0).astype(BF16)
                dom = jnp.where(sel, do_ref[:, p * LANES:(p + 1) * LANES], 0.0).astype(BF16)
                sw = 0 if khalf == hf else 1
                heads[h] = dict(kpair=kpair, sw=sw, qm=qm, dom=dom, s=_dot_nt(ks[kpair][sw], qm),
                                dp=_dot_nt(vs[kpair][sw], dom))

            def second(h):
                d = heads[h]
                pr, ps = _softmax_sink(d.pop("s"), valid, sink_ref[0, h])
                dp = d.pop("dp")
                dd = jnp.sum(pr * dp, axis=0, keepdims=True)
                dsinks.append(jnp.where(lane == h, -jnp.sum(ps * dd, axis=1, keepdims=True), 0.0))
                d["ds"] = (pr * (dp - dd)).astype(BF16)
                d["pr"] = pr.astype(BF16)

            def third(h):
                d = heads.pop(h)
                kpair, sw = d["kpair"], d["sw"]
                dq_t[h] = _dot(kts[kpair][sw], d["ds"])
                dk_acc[kpair][sw] = dk_acc[kpair][sw] + _dot(d["ds"], d["qm"])
                dv_acc[kpair][sw] = dv_acc[kpair][sw] + _dot(d["pr"], d["dom"])

            for i in range(N_Q_HEADS + 2 * HEAD_LAG):
                if i < N_Q_HEADS:
                    first(i)
                if 0 <= i - HEAD_LAG < N_Q_HEADS:
                    second(i - HEAD_LAG)
                if 0 <= i - 2 * HEAD_LAG < N_Q_HEADS:
                    third(i - 2 * HEAD_LAG)
            dsink = sum(dsinks, dsink)
            dqs = [jnp.where(upper, dq_t[2 * p], dq_t[2 * p + 1]).T * ATT_SCALE for p in range(Q_DIM // LANES)]
            dk_acc = [a[0] + pltpu.roll(a[1], HEAD_DIM, 1) for a in dk_acc]
            dv_acc = [a[0] + pltpu.roll(a[1], HEAD_DIM, 1) for a in dv_acc]
            flush([a[:ATT_BLOCK] for a in dk_acc], [a[:ATT_BLOCK] for a in dv_acc], tp_ref)
            dq = _rot_bwd(dqs, tc_ref[...])
            for p in range(Q_DIM // LANES):
                dq_c[:, p * LANES:(p + 1) * LANES] = dq[p]
            for j in range(npair):
                dk_c[:, j * LANES:(j + 1) * LANES] = dk_acc[j][ATT_BLOCK:]
                dv_c[:, j * LANES:(j + 1) * LANES] = dv_acc[j][ATT_BLOCK:]
            dsink_ref[...] += dsink

        @pl.when(n == nb)
        def _():
            zero = [jnp.zeros((ATT_BLOCK, LANES), F32) for _ in range(npair)]
            flush(zero, zero, tc_ref)

        finish()

    do_spec = pl.BlockSpec((ATT_BLOCK, Q_DIM), lambda n: (jnp.minimum(n, nb - 1), 0))
    in_specs, out_specs, out_shape, scratch, extra = _carried_specs(
        carry, _attn_specs(nb, True) + [do_spec],
        [pl.BlockSpec((ATT_BLOCK, QKV_DIM), lambda n: (jnp.maximum(n - 1, 0), 0)),
         pl.BlockSpec((1, LANES), lambda n: (0, 0))],
        [jax.ShapeDtypeStruct((T, QKV_DIM), BF16), jax.ShapeDtypeStruct((1, LANES), F32)],
        [pltpu.VMEM((ATT_BLOCK, Q_DIM), F32), pltpu.VMEM((ATT_BLOCK, KV_DIM), F32),
         pltpu.VMEM((ATT_BLOCK, KV_DIM), F32)])
    return pl.pallas_call(
        body, name="attn_bwd", grid=(nb + 1,), in_specs=in_specs, out_specs=out_specs, out_shape=out_shape,
        scratch_shapes=scratch, compiler_params=_params(dimension_semantics=("arbitrary",)),
    )(qkv, qkv, qkv, qkv, qkv, rot, rot, sinks, dout, *extra)


LEVELS = (32, 16, 8, 4, 2, 1)
SUBLANES = 8
UNROLL = 16
UNROLL_BWD = 8


def _lower_bound(lb_ref):
    l0, l1 = lb_ref[0:1, :], lb_ref[1:2, :]
    mx = jnp.maximum(l0, l1)
    e0, e1 = jnp.exp(l0 - mx), jnp.exp(l1 - mx)
    return e1 / (e0 + e1)


GROUPS = CHUNK // SUBLANES


def _group_roll(x, k):
    return pltpu.roll(x.reshape(GROUPS, SUBLANES, HGRN_DK), k % SUBLANES, 1).reshape(CHUNK, HGRN_DK)


def _scan_rows(x, row, reverse):
    r8 = row & (SUBLANES - 1)
    for sh in (1, 2, 4):
        ok = (r8 < SUBLANES - sh) if reverse else (r8 >= sh)
        x = x + jnp.where(ok, _group_roll(x, -sh if reverse else sh), 0.0)
    g = x.reshape(GROUPS, SUBLANES, HGRN_DK)
    edge = 0 if reverse else SUBLANES - 1
    tot = jnp.broadcast_to(g[:, edge:edge + 1, :], g.shape)

    def shifted(a, n):
        z = jnp.zeros((n, SUBLANES, HGRN_DK), F32)
        return jnp.concatenate([a[n:], z] if reverse else [z, a[:GROUPS - n]], axis=0)

    acc = shifted(tot, 1)
    for sh in (1, 2, 4):
        acc = acc + shifted(acc, sh)
    return (g + acc).reshape(CHUNK, HGRN_DK)


def _level_masks():
    t = lax.broadcasted_iota(jnp.int32, (CHUNK, CHUNK), 0)
    s = lax.broadcasted_iota(jnp.int32, (CHUNK, CHUNK), 1)
    return [((t & h) != 0) & ((s & h) == 0) & ((t ^ s) < 2 * h) for h in LEVELS]


def _level_scales(b, forget, row):
    out = []
    for h in LEVELS[:3]:
        parts = [jnp.broadcast_to(b[j * 2 * h + h - 1:j * 2 * h + h, :], (2 * h, HGRN_DK))
                 for j in range(CHUNK // (2 * h))]
        mid = parts[0] if len(parts) == 1 else jnp.concatenate(parts, axis=0)
        out.append(jnp.exp(-jnp.abs(b - mid)))
    groups = b.reshape(GROUPS, SUBLANES, HGRN_DK)
    mid = jnp.broadcast_to(groups[:, SUBLANES // 2 - 1:SUBLANES // 2, :], groups.shape)
    e4 = jnp.exp(-jnp.abs(groups - mid)).reshape(CHUNK, HGRN_DK)
    f, r4 = forget, row & 3
    up1, dn1 = _group_roll(f, -1), _group_roll(f, 1)
    e2 = jnp.where(r4 == 0, up1, jnp.where(r4 == 1, 1.0, jnp.where(r4 == 2, f, dn1 * f)))
    e1 = jnp.where((row & 1) == 1, f, 1.0)
    return out + [e4, e2, e1]


def _hgrn_gates(zq, zf, lb):
    sq = jax.nn.sigmoid(zq)
    q = zq * sq
    sg = jax.nn.sigmoid(zf)
    forget = lb + (1.0 - lb) * sg
    return q, sq, sg, forget, 1.0 - forget, jnp.log(forget)


def _hgrn_specs(T, rb, rev):
    nr = T // rb
    ri = (lambda r: nr - 1 - r) if rev else (lambda r: r)
    return nr, ri, [
        pl.BlockSpec((rb, HGRN_DK), lambda h, r: (ri(r), h)),
        pl.BlockSpec((rb, HGRN_DK), lambda h, r: (ri(r), HGRN_HEADS + h)),
        pl.BlockSpec((rb, HGRN_DK), lambda h, r: (ri(r), 2 * HGRN_HEADS + h)),
        pl.BlockSpec((2, HGRN_DK), lambda h, r: (0, h)),
    ]


def _hgrn_fwd(z, lb_raw, rb=2048, carry=(None, None)):
    T = z.shape[0]
    rb = min(rb, T)
    ncb = rb // CHUNK
    unroll = min(UNROLL, ncb)
    assert ncb % unroll == 0
    nr, ri, in_specs = _hgrn_specs(T, rb, False)

    def body(*refs):
        hh, rr = pl.program_id(0), pl.program_id(1)
        own, finish = _carried(carry, refs, 4, 2, (hh == 0) & (rr == 0), (hh == HGRN_HEADS - 1) & (rr == nr - 1))
        zq_ref, zf_ref, zi_ref, lb_ref, o_ref, st_ref, state = own

        @pl.when(rr == 0)
        def _():
            state[...] = jnp.zeros_like(state)

        lb = _lower_bound(lb_ref)
        row = lax.broadcasted_iota(jnp.int32, (CHUNK, HGRN_DK), 0)
        masks = _level_masks()

        def operands(c):
            rows = pl.ds(pl.multiple_of(c * CHUNK, CHUNK), CHUNK)
            q, _, _, forget, k, lf = _hgrn_gates(zq_ref[rows, :], zf_ref[rows, :], lb)
            v = zi_ref[rows, :]
            b = _scan_rows(lf, row, False)
            pairs = [((q * e).astype(BF16), (k * e).astype(BF16)) for e in _level_scales(b, forget, row)]
            b_last = b[CHUNK - 1:CHUNK, :]
            return dict(c=c, rows=rows, pairs=pairs, vb=v.astype(BF16), diag=jnp.sum(q * k, axis=-1, keepdims=True) * v,
                        kd=(k * jnp.exp(b_last - b)).astype(BF16), qd=(q * jnp.exp(b)).astype(BF16),
                        decay=jnp.exp(b_last))

        def group(i, st):
            parts = [operands(i * unroll + j) for j in range(unroll)]
            for p in parts:
                sc = jnp.zeros((CHUNK, CHUNK), F32)
                for (qs, ks), mask in zip(p["pairs"], masks):
                    sc = sc + jnp.where(mask, _dot_nt(qs, ks), 0.0)
                p["sc"] = sc.astype(BF16)
            for p in parts:
                p["o"] = _dot(p["sc"], p["vb"]) + p["diag"]
                p["gain"] = _dot_tn(p["vb"], p["kd"])
            for p in parts:
                st_ref[p["c"], 0] = st
                o_ref[p["rows"], :] = p["o"] + _dot_nt(p["qd"], st.astype(BF16))
                st = st * p["decay"] + p["gain"]
            return st

        state[...] = lax.fori_loop(0, ncb // unroll, group, state[...])
        finish()

    in_specs, out_specs, out_shape, scratch, extra = _carried_specs(
        carry, in_specs,
        [pl.BlockSpec((rb, HGRN_DK), lambda h, r: (r, h)),
         pl.BlockSpec((ncb, 1, HGRN_DK, HGRN_DK), lambda h, r: (r, h, 0, 0))],
        [jax.ShapeDtypeStruct((T, D_MODEL), F32),
         jax.ShapeDtypeStruct((T // CHUNK, HGRN_HEADS, HGRN_DK, HGRN_DK), F32)],
        [pltpu.VMEM((HGRN_DK, HGRN_DK), F32)])
    return pl.pallas_call(
        body, name="hgrn_fwd", grid=(HGRN_HEADS, nr), in_specs=in_specs, out_specs=out_specs, out_shape=out_shape,
        scratch_shapes=scratch, compiler_params=_params(dimension_semantics=("arbitrary", "arbitrary")),
    )(z, z, z, lb_raw, *extra)


def _hgrn_bwd(z, lb_raw, states, do, rb=2048, carry=(None, None)):
    T = z.shape[0]
    rb = min(rb, T)
    ncb = rb // CHUNK
    unroll = min(UNROLL_BWD, ncb)
    assert ncb % unroll == 0
    nr, ri, in_specs = _hgrn_specs(T, rb, True)
    in_specs += [pl.BlockSpec((ncb, 1, HGRN_DK, HGRN_DK), lambda h, r: (ri(r), h, 0, 0)),
                 pl.BlockSpec((rb, HGRN_DK), lambda h, r: (ri(r), h))]

    def body(*refs):
        hh, rr = pl.program_id(0), pl.program_id(1)
        own, finish = _carried(carry, refs, 6, 4, (hh == 0) & (rr == 0), (hh == HGRN_HEADS - 1) & (rr == nr - 1))
        zq_ref, zf_ref, zi_ref, lb_ref, st_ref, do_ref, dq_ref, df_ref, di_ref, dlb_ref, dstate = own

        @pl.when(rr == 0)
        def _():
            dstate[...] = jnp.zeros_like(dstate)
            dlb_ref[...] = jnp.zeros_like(dlb_ref)

        lb = _lower_bound(lb_ref)
        row = lax.broadcasted_iota(jnp.int32, (CHUNK, HGRN_DK), 0)
        masks = _level_masks()

        def operands(c):
            rows = pl.ds(pl.multiple_of(c * CHUNK, CHUNK), CHUNK)
            zq = zq_ref[rows, :]
            q, sq, sg, forget, k, lf = _hgrn_gates(zq, zf_ref[rows, :], lb)
            v = zi_ref[rows, :]
            dov = do_ref[rows, :]
            b = _scan_rows(lf, row, False)
            b_last = b[CHUNK - 1:CHUNK, :]
            eb, ebb = jnp.exp(b), jnp.exp(b_last - b)
            es = _level_scales(b, forget, row)
            return dict(rows=rows, zq=zq, q=q, sq=sq, sg=sg, forget=forget, k=k, v=v, dov=dov, eb=eb, ebb=ebb,
                        e_last=jnp.exp(b_last), es=es, st=st_ref[c, 0], dob=dov.astype(BF16), vb=v.astype(BF16),
                        pairs=[((q * e).astype(BF16), (k * e).astype(BF16)) for e in es],
                        qd=(q * eb).astype(BF16), kd=(k * ebb).astype(BF16))

        def group(i, dlb):
            parts = [operands(ncb - 1 - (i * unroll + j)) for j in range(unroll)]
            for p in parts:
                p["da"] = _dot_nt(p["dob"], p["vb"])
                sc = jnp.zeros((CHUNK, CHUNK), F32)
                for (qs, ks), mask in zip(p["pairs"], masks):
                    sc = sc + jnp.where(mask, _dot_nt(qs, ks), 0.0)
                p["sc"] = sc.astype(BF16)
                p["dq_state"] = _dot(p["dob"], p["st"].astype(BF16))
                p["gain"] = _dot_tn(p["dob"], p["qd"])
            dst = dstate[...]
            for p in parts:
                p["dst"] = dst
                dst = dst * p["e_last"] + p["gain"]
            dstate[...] = dst
            for p in parts:
                dstb = p["dst"].astype(BF16)
                dk_state = p["ebb"] * _dot(p["vb"], dstb)
                dq = p["eb"] * p["dq_state"]
                dk = dk_state
                dv = _dot_nt(p["kd"], dstb) + _dot_tn(p["sc"], p["dob"])
                for e, (qs, ks), mask in zip(p["es"], p["pairs"], masks):
                    dam = jnp.where(mask, p["da"], 0.0).astype(BF16)
                    dq = dq + e * _dot(dam, ks)
                    dk = dk + e * _dot_tn(dam, qs)
                dad = jnp.sum(p["dov"] * p["v"], axis=-1, keepdims=True)
                p["dq"] = dq + dad * p["k"]
                p["dk"] = dk + dad * p["q"]
                p["dv"] = dv + jnp.sum(p["q"] * p["k"], axis=-1, keepdims=True) * p["dov"]
                p["extra"] = (p["e_last"] * jnp.sum(p["dst"] * p["st"], axis=0, keepdims=True)
                              + jnp.sum(p["k"] * dk_state, axis=0, keepdims=True))
            for p in parts:
                q, k, sq, sg, zq, rows = p["q"], p["k"], p["sq"], p["sg"], p["zq"], p["rows"]
                dlf = _scan_rows(q * p["dq"] - k * p["dk"], row, True) + p["extra"]
                dforget = dlf / p["forget"] - p["dk"]
                dq_ref[rows, :] = (p["dq"] * (sq * (1.0 + zq * (1.0 - sq)))).astype(BF16)
                df_ref[rows, :] = (dforget * (1.0 - lb) * sg * (1.0 - sg)).astype(BF16)
                di_ref[rows, :] = p["dv"].astype(BF16)
                dlb = dlb + jnp.sum(dforget * (1.0 - sg), axis=0, keepdims=True)
            return dlb

        dlb_ref[...] += lax.fori_loop(0, ncb // unroll, group, jnp.zeros((1, HGRN_DK), F32))
        finish()

    blk = pl.BlockSpec((rb, HGRN_DK), lambda h, r: (ri(r), h))
    in_specs, out_specs, out_shape, scratch, extra = _carried_specs(
        carry, in_specs, [blk, blk, blk, pl.BlockSpec((1, HGRN_DK), lambda h, r: (0, h))],
        [jax.ShapeDtypeStruct((T, D_MODEL), BF16)] * 3 + [jax.ShapeDtypeStruct((1, D_MODEL), F32)],
        [pltpu.VMEM((HGRN_DK, HGRN_DK), F32)])
    return pl.pallas_call(
        body, name="hgrn_bwd", grid=(HGRN_HEADS, nr), in_specs=in_specs, out_specs=out_specs, out_shape=out_shape,
        scratch_shapes=scratch, compiler_params=_params(dimension_semantics=("arbitrary", "arbitrary")),
    )(z, z, z, lb_raw, states, do, *extra)


MESH = pl.DeviceIdType.MESH
ANY = pl.BlockSpec(memory_space=pl.ANY)


def _place():
    return lax.axis_index("x"), lax.axis_index("y"), lax.axis_index("c")


def _sems(n):
    return [pltpu.SemaphoreType.DMA((7 * n,)), pltpu.SemaphoreType.DMA((7 * n,)), pltpu.SemaphoreType.DMA((n,))]


class _Gather:
    def __init__(self, x_ref, out_ref, send_sems, recv_sems, local_sems, idx):
        self.x_ref, self.out_ref, self.send_sems, self.recv_sems, self.local_sem, self.base = (
            x_ref, out_ref, send_sems, recv_sems, local_sems.at[idx], 7 * idx)
        x, y, c = _place()
        self.c = c
        self.me, self.sibling = (x, y, c), (x, y, 1 - c)
        self.chips = [(1 - x, y), (x, 1 - y), (1 - x, 1 - y)]

    def rows(self, px, py, pc):
        return self.out_ref.at[4 * px + 2 * py + pc]

    def copy(self, k, block, to, from_input=False):
        return pltpu.make_async_remote_copy(
            src_ref=self.x_ref if from_input else self.rows(*block), dst_ref=self.rows(*block),
            send_sem=self.send_sems.at[self.base + k], recv_sem=self.recv_sems.at[self.base + k], device_id=to,
            device_id_type=MESH)

    def first(self):
        out = [self.copy(0, self.me, self.sibling, from_input=True)]
        return out + [self.copy(1 + j, self.me, (*chip, self.c), from_input=True) for j, chip in enumerate(self.chips)]

    def start(self):
        pltpu.make_async_copy(self.x_ref, self.rows(*self.me), self.local_sem).start()
        for cp in self.first():
            cp.start()

    def finish(self):
        passed = [self.copy(4 + j, (*chip, self.c), self.sibling) for j, chip in enumerate(self.chips)]
        for j, chip in enumerate(self.chips):
            self.copy(1 + j, (*chip, self.c), self.me).wait_recv()
            passed[j].start()
        self.copy(0, self.sibling, self.me).wait_recv()
        for j, chip in enumerate(self.chips):
            self.copy(4 + j, (*chip, 1 - self.c), self.me).wait_recv()
        for cp in self.first() + passed:
            cp.wait_send()
        pltpu.make_async_copy(self.x_ref, self.rows(*self.me), self.local_sem).wait()


class _Many:
    def __init__(self, kind, in_refs, out_refs, send_sems, recv_sems, local_sems):
        self.ops = [kind(x, o, send_sems, recv_sems, local_sems, i) for i, (x, o) in enumerate(zip(in_refs, out_refs))]

    def start(self):
        for op in self.ops:
            op.start()

    def finish(self):
        for op in self.ops:
            op.finish()


def _result_shapes(kind, arrs):
    return [jax.ShapeDtypeStruct(a.shape if kind is _Exchange else (N_DEV,) + a.shape, a.dtype) for a in arrs]


def _all_gather(name, shards):
    n = len(shards)

    def body(*refs):
        g = _Many(_Gather, refs[:n], refs[n:2 * n], *refs[2 * n:])
        g.start()
        g.finish()

    return pl.pallas_call(
        body, name=name, out_shape=_result_shapes(_Gather, shards), in_specs=[ANY] * n, out_specs=[ANY] * n,
        scratch_shapes=_sems(n),
    )(*shards)


def _peers(x, y, c):
    out = []
    for k in range(1, N_DEV):
        px = 1 - x if k & 4 else x
        py = 1 - y if k & 2 else y
        pc = 1 - c if k & 1 else c
        out.append((k, (px, py, pc), 4 * px + 2 * py + pc))
    return out


class _Exchange:
    def __init__(self, g_ref, recv_ref, send_sems, recv_sems, local_sems, idx):
        x, y, c = _place()
        me = 4 * x + 2 * y + c
        self.local = pltpu.make_async_copy(g_ref.at[me], recv_ref.at[me], local_sems.at[idx])
        self.copies = [
            pltpu.make_async_remote_copy(
                src_ref=g_ref.at[pidx], dst_ref=recv_ref.at[me], send_sem=send_sems.at[7 * idx + k - 1],
                recv_sem=recv_sems.at[7 * idx + k - 1], device_id=peer, device_id_type=MESH)
            for k, peer, pidx in _peers(x, y, c)]

    def start(self):
        self.local.start()
        for cp in self.copies:
            cp.start()

    def finish(self):
        for cp in self.copies:
            cp.wait()
        self.local.wait()


def _carried(carry, refs, n_in, n_out, first, last):
    kind, arrs = carry
    if kind is None:
        return refs, lambda: None
    n = len(arrs)
    ins, rest = refs[:n_in], refs[n_in + n:]
    outs, scratch = rest[:n_out], rest[n_out + n:]
    op = _Many(kind, refs[n_in:n_in + n], rest[n_out:n_out + n], *scratch[len(scratch) - 3:])

    @pl.when(first)
    def _():
        op.start()

    def finish():
        @pl.when(last)
        def _():
            op.finish()

    return tuple(ins) + tuple(outs) + tuple(scratch[:len(scratch) - 3]), finish


def _carried_specs(carry, in_specs, out_specs, out_shape, scratch):
    kind, arrs = carry
    if kind is None:
        return in_specs, out_specs, out_shape, scratch, []
    n = len(arrs)
    return (list(in_specs) + [ANY] * n, list(out_specs) + [ANY] * n,
            list(out_shape) + _result_shapes(kind, arrs), list(scratch) + _sems(n), list(arrs))


def _adamw(w, g, m, v):
    m = ADAM_B1 * m + (1.0 - ADAM_B1) * g
    v = ADAM_B2 * v + (1.0 - ADAM_B2) * (g * g)
    m_hat = m / (1.0 - ADAM_B1 ** ADAM_STEP)
    v_hat = v / (1.0 - ADAM_B2 ** ADAM_STEP)
    delta = -ADAM_LR * (m_hat / (jnp.sqrt(v_hat) + ADAM_EPS) + ADAM_WD * w)
    return delta, m, v


def _adamw_sum(name, recvs, w, m, v):
    L, R, C = w.shape
    tm = 128 if R % 128 == 0 else 64
    assert R % tm == 0 and len(recvs) == L

    def body(*refs):
        r_refs, (w_ref, m_ref, v_ref, g_ref, d_ref, nm_ref, nv_ref) = refs[:L], refs[L:]
        for l in range(L):
            g = r_refs[l][0].astype(F32)
            for s in range(1, N_DEV):
                g = g + r_refs[l][s].astype(F32)
            g_ref[l] = g
            d_ref[l], nm_ref[l], nv_ref[l] = _adamw(w_ref[l], g, m_ref[l], v_ref[l])

    blk = pl.BlockSpec((L, tm, C), lambda i: (0, i, 0))
    return pl.pallas_call(
        body, name=name, grid=(R // tm,),
        in_specs=[pl.BlockSpec((N_DEV, tm, C), lambda i: (0, i, 0))] * L + [blk, blk, blk],
        out_specs=[blk] * 4, out_shape=[jax.ShapeDtypeStruct((L, R, C), F32)] * 4,
        compiler_params=_params(dimension_semantics=("arbitrary",)),
    )(*recvs, w, m, v)


def _small_sync(part, w, m, v):
    def body(p_ref, w_ref, m_ref, v_ref, g_ref, d_ref, nm_ref, nv_ref, gath, send_sems, recv_sems):
        x, y, c = _place()
        me = 4 * x + 2 * y + c
        gath[me] = p_ref[...]
        copies = []
        for k, peer, _ in _peers(x, y, c):
            cp = pltpu.make_async_remote_copy(
                src_ref=p_ref, dst_ref=gath.at[me], send_sem=send_sems.at[k - 1], recv_sem=recv_sems.at[k - 1],
                device_id=peer, device_id_type=MESH)
            cp.start()
            copies.append(cp)
        for cp in copies:
            cp.wait()
        g = gath[0]
        for s in range(1, N_DEV):
            g = g + gath[s]
        wv = w_ref[...]
        l0, l1 = w_ref[8:9, :], w_ref[9:10, :]
        mx = jnp.maximum(l0, l1)
        e0, e1 = jnp.exp(l0 - mx), jnp.exp(l1 - mx)
        g9 = g[9:10, :] * (e0 / (e0 + e1)) * (e1 / (e0 + e1))
        row = lax.broadcasted_iota(jnp.int32, g.shape, 0)
        g = jnp.where(row == 9, g9, jnp.where(row == 8, -g9, g))
        g_ref[...] = g
        d_ref[...], nm_ref[...], nv_ref[...] = _adamw(wv, g, m_ref[...], v_ref[...])

    vm = pl.BlockSpec(memory_space=pltpu.VMEM)
    return pl.pallas_call(
        body, name="small_params_sync", in_specs=[vm] * 4, out_specs=[vm] * 4,
        out_shape=[jax.ShapeDtypeStruct(part.shape, F32)] * 4,
        scratch_shapes=[pltpu.VMEM((N_DEV,) + part.shape, F32), pltpu.SemaphoreType.DMA((7,)),
                        pltpu.SemaphoreType.DMA((7,))],
    )(part, w, m, v)


def _shards_bf16(d, pieces):
    return [d[name][layer].astype(BF16) for name, layer in pieces]


def _gathered(arrs, pieces, out):
    for a, (name, layer) in zip(arrs, pieces):
        out[name, layer] = a if name in COL_SHARDED else a.reshape(N_DEV * a.shape[1], a.shape[2])


def _pad_row(a, width=D_MODEL):
    a = a.reshape(1, -1)
    return jnp.pad(a, ((0, 0), (0, width - a.shape[1])))


LOSS_ROW = 11


def _pack_small(d, gn_full, loss=None):
    rows = [d["mix_norm"], d["mlp_norm"], d["final_norm"].reshape(1, D_MODEL),
            _pad_row(d["attn_b_qkv"], 2 * D_MODEL).reshape(2, D_MODEL), _pad_row(d["attn_sinks"]),
            d["hgrn_lower_bounds"], gn_full.reshape(1, D_MODEL)]
    if loss is not None:
        rows.append(_pad_row(loss))
    p = jnp.concatenate(rows, axis=0)
    return jnp.pad(p, ((0, SMALL_ROWS - p.shape[0]), (0, 0)))


def _unpack_small(p, me):
    return dict(
        mix_norm=p[0:2], mlp_norm=p[2:4], final_norm=p[4],
        attn_b_qkv=p[5:7].reshape(1, 2 * D_MODEL)[:, :QKV_DIM], attn_sinks=p[7:8, :N_Q_HEADS],
        hgrn_lower_bounds=p[8:10], hgrn_g_norm=lax.dynamic_slice(p[10:11], (0, me * 128), (1, 128)))


WEIGHT_NAMES = ['mix_norm', 'mlp_norm', 'final_norm', 'attn_w_qkv', 'attn_b_qkv', 'attn_sinks', 'attn_w_o', 'hgrn_w_in',
                'hgrn_g_norm', 'hgrn_w_o', 'hgrn_lower_bounds', 'mlp_w_up', 'mlp_w_down']
SMALL_NAMES = ('mix_norm', 'mlp_norm', 'final_norm', 'attn_b_qkv', 'attn_sinks', 'hgrn_lower_bounds', 'hgrn_g_norm')


def _rotary_tables(positions):
    inv_freq = ROPE_THETA ** (-jnp.arange(0, 2 * ROT_HALF, 2, dtype=F32) / (2 * ROT_HALF))
    ang = positions.astype(F32).reshape(-1, 1) * inv_freq
    cos, sin = jnp.cos(ang), jnp.sin(ang)
    r = jnp.arange(LANES) % HEAD_DIM
    idx = r % ROT_HALF
    c = jnp.where(r < 2 * ROT_HALF, cos[:, idx], 1.0)
    sa = jnp.where((r >= ROT_HALF) & (r < 2 * ROT_HALF), sin[:, idx], 0.0)
    sb = jnp.where(r < ROT_HALF, -sin[:, idx], 0.0)
    return jnp.concatenate([c, sa, sb], axis=1)


def kernel(x, positions, mix_norm, mlp_norm, final_norm, attn_w_qkv, attn_b_qkv, attn_sinks, attn_w_o, hgrn_w_in, hgrn_g_norm, hgrn_w_o, hgrn_lower_bounds, mlp_w_up, mlp_w_down, loss_target, m_mix_norm, m_mlp_norm, m_final_norm, m_attn_w_qkv, m_attn_b_qkv, m_attn_sinks, m_attn_w_o, m_hgrn_w_in, m_hgrn_g_norm, m_hgrn_w_o, m_hgrn_lower_bounds, m_mlp_w_up, m_mlp_w_down, v_mix_norm, v_mlp_norm, v_final_norm, v_attn_w_qkv, v_attn_b_qkv, v_attn_sinks, v_attn_w_o, v_hgrn_w_in, v_hgrn_g_norm, v_hgrn_w_o, v_hgrn_lower_bounds, v_mlp_w_up, v_mlp_w_down):
    w = dict(mix_norm=mix_norm, mlp_norm=mlp_norm, final_norm=final_norm, attn_w_qkv=attn_w_qkv, attn_b_qkv=attn_b_qkv,
             attn_sinks=attn_sinks, attn_w_o=attn_w_o, hgrn_w_in=hgrn_w_in, hgrn_g_norm=hgrn_g_norm, hgrn_w_o=hgrn_w_o,
             hgrn_lower_bounds=hgrn_lower_bounds, mlp_w_up=mlp_w_up, mlp_w_down=mlp_w_down)
    m = dict(mix_norm=m_mix_norm, mlp_norm=m_mlp_norm, final_norm=m_final_norm, attn_w_qkv=m_attn_w_qkv,
             attn_b_qkv=m_attn_b_qkv, attn_sinks=m_attn_sinks, attn_w_o=m_attn_w_o, hgrn_w_in=m_hgrn_w_in,
             hgrn_g_norm=m_hgrn_g_norm, hgrn_w_o=m_hgrn_w_o, hgrn_lower_bounds=m_hgrn_lower_bounds, mlp_w_up=m_mlp_w_up,
             mlp_w_down=m_mlp_w_down)
    v = dict(mix_norm=v_mix_norm, mlp_norm=v_mlp_norm, final_norm=v_final_norm, attn_w_qkv=v_attn_w_qkv,
             attn_b_qkv=v_attn_b_qkv, attn_sinks=v_attn_sinks, attn_w_o=v_attn_w_o, hgrn_w_in=v_hgrn_w_in,
             hgrn_g_norm=v_hgrn_g_norm, hgrn_w_o=v_hgrn_w_o, hgrn_lower_bounds=v_hgrn_lower_bounds, mlp_w_up=v_mlp_w_up,
             mlp_w_down=v_mlp_w_down)
    me = 4 * lax.axis_index("x") + 2 * lax.axis_index("y") + lax.axis_index("c")

    gn = hgrn_g_norm.reshape(1, 128)
    gn_a = gn.astype(BF16)
    gn_b = (gn - gn_a.astype(F32)).astype(BF16)
    gn_c = (gn - gn_a.astype(F32) - gn_b.astype(F32)).astype(BF16)
    gn_rows = jnp.pad(jnp.concatenate([gn_a, gn_b, gn_c], axis=1), ((0, 15), (0, D_MODEL - 3 * 128)))
    full = {}
    got = _all_gather("gather_attn_weights", _shards_bf16(w, GATHER_FIRST) + [gn_rows])
    _gathered(got[:1], GATHER_FIRST, full)
    w_qkv = full["attn_w_qkv", 0].transpose(1, 0, 2).reshape(D_MODEL, QKV_DIM)
    gn_terms = got[1][:, 0, :3 * 128].astype(F32).reshape(N_DEV, 3, 128)
    gn_full = ((gn_terms[:, 0] + gn_terms[:, 1]) + gn_terms[:, 2]).reshape(1, D_MODEL)

    x0 = x[0]
    tgt = loss_target[0]
    rot = _rotary_tables(positions)
    row = lambda a: a.reshape(1, -1)

    qkv, h0 = _norm_mm("qkv_proj", x0, row(mix_norm[0]), w_qkv, attn_b_qkv, rot=rot)
    att, *got = _attn_fwd(qkv, attn_sinks, carry=(_Gather, _shards_bf16(w, GATHER_ATTN)))
    _gathered(got, GATHER_ATTN, full)
    x1 = _mm_res("attn_out_proj", att, full["attn_w_o", 0], x0)
    u0, h1, *got = _norm_mm("mlp0_up", x1, row(mlp_norm[0]), full["mlp_w_up", 0],
                            carry=(_Gather, _shards_bf16(w, GATHER_MLP0)))
    _gathered(got, GATHER_MLP0, full)
    x2, a0 = _mlp_down("mlp0_down", u0, full["mlp_w_down", 0], x1)
    z, h2 = _norm_mm("hgrn_in_proj", x2, row(mix_norm[1]), full["hgrn_w_in", 0])
    o_raw, states, *got = _hgrn_fwd(z, hgrn_lower_bounds, carry=(_Gather, _shards_bf16(w, GATHER_HGRN)))
    _gathered(got, GATHER_HGRN, full)
    x3, o2 = _hgrn_out("hgrn_out_proj", o_raw, z, gn_full, full["hgrn_w_o", 0], x2)
    u1, h3 = _norm_mm("mlp1_up", x3, row(mlp_norm[1]), full["mlp_w_up", 1])
    dx4, a1, loss_part, g_final = _mlp_down("mlp1_down_loss", u1, full["mlp_w_down", 1], x3,
                                            loss_head=(tgt, row(final_norm)))

    gw = {}
    du1, = _mlp_bwd_act("mlp1_bwd_act", dx4, u1, full["mlp_w_down", 1])
    dx3, g_mlp1 = _mm_nt_rmsbwd("mlp1_bwd_in", du1, full["mlp_w_up", 1], x3, row(mlp_norm[1]), dx4)
    gw["mlp_w_down", 1] = _mm_tn("mlp1_dw_down", a1, dx4, "rows")
    gw["mlp_w_up", 1] = _mm_tn("mlp1_dw_up", h3, du1, "cols")

    do_raw, dg, g_gn = _hgrn_out_bwd("hgrn_out_bwd", dx3, o_raw, z, full["hgrn_w_o", 0], gn_full)
    gw["hgrn_w_o", 0] = _mm_tn("hgrn_dw_o", o2, dx3, "rows")
    recvs = {}
    dzq, dzf, dzi, g_lb, *recv = _hgrn_bwd(z, hgrn_lower_bounds, states, do_raw,
                                           carry=(_Exchange, [gw[p] for p in GRADS_HGRN]))
    recvs.update(zip(GRADS_HGRN, recv))
    dz = [dzq, dzf, dzi, dg]
    dx2, g_mix1 = _mm_nt_rmsbwd("hgrn_in_bwd", dz, full["hgrn_w_in", 0], x2, row(mix_norm[1]), dx3)
    gw["hgrn_w_in", 0] = jnp.concatenate(
        [_mm_tn(f"hgrn_dw_in{j}", h2, d, "cols") for j, d in enumerate(dz)], axis=0)

    du0, = _mlp_bwd_act("mlp0_bwd_act", dx2, u0, full["mlp_w_down", 0])
    dx1, g_mlp0 = _mm_nt_rmsbwd("mlp0_bwd_in", du0, full["mlp_w_up", 0], x1, row(mlp_norm[0]), dx2)
    gw["mlp_w_down", 0], recvs["hgrn_w_in", 0] = _mm_tn(
        "mlp0_dw_down", a0, dx2, "rows", carry=(_Exchange, [gw["hgrn_w_in", 0]]))
    gw["mlp_w_up", 0], recvs["mlp_w_down", 0] = _mm_tn(
        "mlp0_dw_up", h1, du0, "cols", carry=(_Exchange, [gw["mlp_w_down", 0]]))

    datt = _mm_nt("attn_out_bwd", dx1, full["attn_w_o", 0], BF16)
    gw["attn_w_o", 0] = _mm_tn("attn_dw_o", att, dx1, "rows")
    dqkv, g_sink, *recv = _attn_bwd(qkv, rot, attn_sinks, datt, carry=(_Exchange, [gw[p] for p in GRADS_ATTN]))
    recvs.update(zip(GRADS_ATTN, recv))
    g_qkv = _mm_tn("attn_dw_qkv", h0, dqkv, bn=512)
    g_qkv = g_qkv.reshape(D_MODEL, N_DEV, QKV_DIM // N_DEV).transpose(1, 0, 2).astype(BF16)
    dx0, g_mix0, g_bqkv, recvs["attn_w_qkv", 0] = _mm_nt_rmsbwd(
        "qkv_bwd", dqkv, w_qkv, x0, row(mix_norm[0]), dx1, with_colsum=True, carry=(_Exchange, [g_qkv]))

    big = {name: _adamw_sum("adamw_" + name, [recvs[name, l] for l in range(w[name].shape[0])], w[name], m[name], v[name])
           for name in BIG_NAMES}

    zero_row = jnp.zeros((1, D_MODEL), F32)
    part = _pack_small(dict(
        mix_norm=jnp.concatenate([g_mix0, g_mix1], axis=0), mlp_norm=jnp.concatenate([g_mlp0, g_mlp1], axis=0),
        final_norm=g_final, attn_b_qkv=g_bqkv, attn_sinks=g_sink[:, :N_Q_HEADS],
        hgrn_lower_bounds=jnp.concatenate([zero_row, g_lb], axis=0)), g_gn, loss=loss_part)

    def spread(a):
        return lax.dynamic_update_slice(zero_row, a.reshape(1, 128), (0, me * 128))

    small_in = [_pack_small({n: d[n] for n in SMALL_NAMES if n != "hgrn_g_norm"}, spread(d["hgrn_g_norm"]))
                for d in (w, m, v)]
    synced = _small_sync(part, *small_in)
    small = [_unpack_small(p, me) for p in synced]

    outs = [synced[0][LOSS_ROW, 0], dx0.reshape(x.shape)]
    for kind, grp_small in enumerate(small):
        for name in WEIGHT_NAMES:
            val = grp_small[name] if name in SMALL_NAMES else big[name][kind]
            outs.append(val.reshape(w[name].shape))
    return tuple(outs)
```

```python
import functools

import jax
import jax.numpy as jnp
from jax import lax
from jax.experimental import pallas as pl
from jax.experimental.pallas import tpu as pltpu

F32 = jnp.float32
BF16 = jnp.bfloat16

D_MODEL = 1024
HEAD_DIM = 64
N_Q_HEADS = 16
Q_DIM = 1024
KV_DIM = 256
QKV_DIM = 1536
ATT_BLOCK = 128
ROT_HALF = 8
ROPE_THETA = 500000.0
NEG_INF = -1e30
HGRN_HEADS = 8
HGRN_DK = 128
CHUNK = 64
D_FF = 4096
NORM_EPS = 1e-5
N_DEV = 8

ADAM_LR = 0.001
ADAM_B1 = 0.9
ADAM_B2 = 0.999
ADAM_EPS = 1e-08
ADAM_WD = 0.01
ADAM_STEP = 10

LANES = 128
VMEM_LIMIT = 56 * 1024 * 1024

GATHER_FIRST = (("attn_w_qkv", 0),)
GATHER_ATTN = (("attn_w_o", 0), ("mlp_w_up", 0))
GATHER_MLP0_UP = (("mlp_w_down", 0),)
GATHER_MLP0_DOWN = (("hgrn_w_in", 0), ("hgrn_w_o", 0))
GATHER_HGRN = (("mlp_w_up", 1), ("mlp_w_down", 1))
GRADS_HGRN = (("mlp_w_down", 1), ("mlp_w_up", 1), ("hgrn_w_o", 0))
GRADS_ATTN = (("mlp_w_up", 0), ("attn_w_o", 0))
COL_SHARDED = ("attn_w_qkv", "hgrn_w_in", "mlp_w_up")
BIG_NAMES = ("attn_w_qkv", "attn_w_o", "hgrn_w_in", "hgrn_w_o", "mlp_w_up", "mlp_w_down")
SMALL_ROWS = 16


def _dot(a, b):
    return jnp.dot(a, b, preferred_element_type=F32)


def _dot_nt(a, b):
    return lax.dot_general(a, b, (((1,), (1,)), ((), ())), preferred_element_type=F32)


def _dot_tn(a, b):
    return lax.dot_general(a, b, (((0,), (0,)), ((), ())), preferred_element_type=F32)


def _params(**kw):
    return pltpu.CompilerParams(vmem_limit_bytes=VMEM_LIMIT, **kw)


def _full_spec(a):
    nd = a.ndim
    return pl.BlockSpec(a.shape, lambda *_: (0,) * nd)


def _row_call(name, body, n_rows, tm, row_ins, full_ins, row_outs, acc_outs=(), carry=(None, None)):
    steps = n_rows // tm
    in_specs = [pl.BlockSpec((tm, w), functools.partial(lambda i, cb: (i, cb), cb=cb)) for _, w, cb in row_ins]
    in_specs += [_full_spec(a) for a in full_ins]
    out_shape = [jax.ShapeDtypeStruct((n_rows, w), dt) for w, dt in row_outs]
    out_specs = [pl.BlockSpec((tm, w), lambda i: (i, 0)) for w, _ in row_outs]
    for shp, dt in acc_outs:
        out_shape.append(jax.ShapeDtypeStruct(shp, dt))
        out_specs.append(pl.BlockSpec(shp, functools.partial(lambda i, nd: (0,) * nd, nd=len(shp))))
    n_in, n_out = len(in_specs), len(out_specs)
    in_specs, out_specs, out_shape, scratch, extra = _carried_specs(carry, in_specs, out_specs, out_shape, [])

    def wrapped(*refs):
        i = pl.program_id(0)
        own, finish = _carried(carry, refs, n_in, n_out, i == 0, i == steps - 1)
        body(*own)
        finish()

    return pl.pallas_call(
        wrapped, name=name, grid=(steps,), in_specs=in_specs, out_specs=out_specs, out_shape=out_shape,
        scratch_shapes=scratch, compiler_params=_params(dimension_semantics=("arbitrary",)),
    )(*[a for a, _, _ in row_ins], *full_ins, *extra)


def _rms(x, gain):
    r = lax.rsqrt(jnp.mean(x * x, axis=-1, keepdims=True) + NORM_EPS)
    xhat = x * r
    return xhat * gain, xhat, r


def _rms_bwd(dy, xhat, r, gain):
    dxhat = dy * gain
    dx = r * (dxhat - xhat * jnp.mean(dxhat * xhat, axis=-1, keepdims=True))
    return dx, dy * xhat


def _norm_mm(name, x, gain, w, bias=None, rot=None, tm=512, carry=(None, None)):
    T = x.shape[0]
    tm = min(tm, T)
    nc = 512
    blocked = w.ndim == 3
    n = N_DEV * w.shape[2] if blocked else w.shape[1]
    assert n % nc == 0 and (not blocked or w.shape[2] == nc)

    def body(*refs):
        x_ref, refs = refs[0], refs[1:]
        if rot is not None:
            t_ref, refs = refs[0], refs[1:]
        g_ref, w_ref, refs = refs[0], refs[1], refs[2:]
        if bias is not None:
            b_ref, refs = refs[0], refs[1:]
        y_ref, h_ref = refs
        h, _, _ = _rms(x_ref[...], g_ref[...])
        hb = h.astype(BF16)
        h_ref[...] = hb
        for c in range(n // nc):
            sl = slice(c * nc, (c + 1) * nc)
            y = _dot(hb, w_ref[c] if blocked else w_ref[:, sl])
            if bias is not None:
                y = y + b_ref[:, sl]
            if rot is None:
                y_ref[:, sl] = y
            else:
                n_rot = max(0, min(nc, Q_DIM + KV_DIM - c * nc)) // LANES
                pieces = _rot_fwd(y[:, :n_rot * LANES], t_ref[...]) if n_rot else []
                for j in range(nc // LANES):
                    col = slice(c * nc + j * LANES, c * nc + (j + 1) * LANES)
                    y_ref[:, col] = pieces[j] if j < n_rot else y[:, j * LANES:(j + 1) * LANES]

    rows = [(x, D_MODEL, 0)] + ([(rot, 3 * LANES, 0)] if rot is not None else [])
    full = [gain, w] + ([bias] if bias is not None else [])
    return _row_call(name, body, T, tm, rows, full, [(n, F32), (D_MODEL, BF16)], carry=carry)


def _mm_res(name, a, w, res, tm=512):
    T = a.shape[0]
    tm = min(tm, T)

    def body(a_ref, r_ref, w_ref, o_ref):
        o_ref[...] = r_ref[...] + _dot(a_ref[...], w_ref[...])

    return _row_call(name, body, T, tm, [(a, a.shape[1], 0), (res, D_MODEL, 0)], [w], [(D_MODEL, F32)])[0]


def _mlp_down(name, u, w, res, tm=512, loss_head=None, carry=(None, None)):
    T = u.shape[0]
    tm = min(tm, T)
    kc = 1024
    sub = min(256, tm)

    def body(*refs):
        if loss_head is None:
            u_ref, r_ref, w_ref, o_ref, a_ref = refs
        else:
            u_ref, r_ref, t_ref, w_ref, g_ref, o_ref, a_ref, loss_ref, dg_ref = refs

            @pl.when(pl.program_id(0) == 0)
            def _():
                loss_ref[...] = jnp.zeros_like(loss_ref)
                dg_ref[...] = jnp.zeros_like(dg_ref)

        for r0 in range(0, tm, sub):
            rs = slice(r0, r0 + sub)
            acc = r_ref[rs, :]
            for c in range(D_FF // kc):
                sl = slice(c * kc, (c + 1) * kc)
                a = jnp.maximum(u_ref[rs, sl], 0.0)
                ab = (a * a).astype(BF16)
                a_ref[rs, sl] = ab
                acc = acc + _dot(ab, w_ref[sl, :])
            if loss_head is None:
                o_ref[rs, :] = acc
            else:
                gain_v = g_ref[...]
                y, xhat, r = _rms(acc, gain_v)
                diff = y - t_ref[rs, :]
                per_row = jnp.sum(diff * diff, axis=-1, keepdims=True) * (1.0 / D_MODEL)
                loss_ref[...] += jnp.broadcast_to(0.5 * jnp.sum(per_row, axis=0, keepdims=True), loss_ref.shape)
                dx, dgr = _rms_bwd(diff * (1.0 / D_MODEL), xhat, r, gain_v)
                o_ref[rs, :] = dx
                dg_ref[...] += jnp.sum(dgr, axis=0, keepdims=True)

    rows, full, acc_outs = [(u, D_FF, 0), (res, D_MODEL, 0)], [w], []
    if loss_head is not None:
        rows, full = rows + [(loss_head[0], D_MODEL, 0)], full + [loss_head[1]]
        acc_outs = [((1, LANES), F32), ((1, D_MODEL), F32)]
    return _row_call(name, body, T, tm, rows, full, [(D_MODEL, F32), (D_FF, BF16)], acc_outs, carry=carry)


def _hgrn_out(name, o_raw, z, gn, w, res, tm=512):
    T = o_raw.shape[0]
    tm = min(tm, T)

    def body(o_ref, g_ref, r_ref, gn_ref, w_ref, x_ref, a_ref):
        y, _, _ = _rms(o_ref[...], gn_ref[...])
        g = g_ref[...]
        a = (y * (g * jax.nn.sigmoid(g))).astype(BF16)
        a_ref[...] = a
        x_ref[...] = r_ref[...] + _dot(a, w_ref[...])

    return _row_call(name, body, T, tm, [(o_raw, D_MODEL, 0), (z, D_MODEL, 3), (res, D_MODEL, 0)], [gn, w],
                     [(D_MODEL, F32), (D_MODEL, BF16)])


def _mm_nt_rmsbwd(name, dy, w, x, gain, dres, tm=512, with_colsum=False, carry=(None, None)):
    T = x.shape[0]
    tm = min(tm, T)
    dys = list(dy) if isinstance(dy, (list, tuple)) else [dy]
    width = dys[0].shape[1]
    n = width * len(dys)
    sub = min(256, tm)
    assert not with_colsum or len(dys) == 1

    def body(*refs):
        dy_refs, refs = refs[:len(dys)], refs[len(dys):]
        if with_colsum:
            x_ref, dr_ref, w_ref, g_ref, dx_ref, dg_ref, cs_ref = refs
        else:
            x_ref, dr_ref, w_ref, g_ref, dx_ref, dg_ref = refs

        @pl.when(pl.program_id(0) == 0)
        def _():
            dg_ref[...] = jnp.zeros_like(dg_ref)
            if with_colsum:
                cs_ref[...] = jnp.zeros_like(cs_ref)

        gain_v = g_ref[...]
        for r0 in range(0, tm, sub):
            rs = slice(r0, r0 + sub)
            if w.ndim == 3:
                nb = w.shape[2]
                dh = None
                for p in range(N_DEV):
                    piece, off = divmod(p * nb, width)
                    part = _dot_nt(dy_refs[piece][rs, off:off + nb].astype(BF16), w_ref[p])
                    dh = part if dh is None else dh + part
            else:
                dh = _dot_nt(dy_refs[0][rs, :].astype(BF16), w_ref[...])
            _, xhat, r = _rms(x_ref[rs, :], gain_v)
            dx, dgr = _rms_bwd(dh, xhat, r, gain_v)
            dx_ref[rs, :] = dr_ref[rs, :] + dx
            dg_ref[...] += jnp.sum(dgr, axis=0, keepdims=True)
            if with_colsum:
                cs_ref[...] += jnp.sum(dy_refs[0][rs, :].astype(F32), axis=0, keepdims=True)

    acc = [((1, D_MODEL), F32)] + ([((1, n), F32)] if with_colsum else [])
    rows = [(d, width, 0) for d in dys] + [(x, D_MODEL, 0), (dres, D_MODEL, 0)]
    return _row_call(name, body, T, tm, rows, [w, gain], [(D_MODEL, F32)], acc, carry=carry)


def _mm_nt(name, dy, w, out_dtype, tm=512):
    T = dy.shape[0]
    tm = min(tm, T)
    k = w.shape[0]

    def body(dy_ref, w_ref, o_ref):
        o_ref[...] = _dot_nt(dy_ref[...].astype(BF16), w_ref[...]).astype(out_dtype)

    return _row_call(name, body, T, tm, [(dy, dy.shape[1], 0)], [w], [(k, out_dtype)])[0]


def _mlp_bwd_act(name, dy, u, w_down, tm=512, carry=(None, None)):
    T = u.shape[0]
    tm = min(tm, T)
    kc = 1024

    def body(dy_ref, u_ref, w_ref, du_ref):
        dyb = dy_ref[...].astype(BF16)
        for c in range(D_FF // kc):
            sl = slice(c * kc, (c + 1) * kc)
            da = _dot_nt(dyb, w_ref[sl, :])
            du_ref[:, sl] = (da * (2.0 * jnp.maximum(u_ref[:, sl], 0.0))).astype(BF16)

    return _row_call(name, body, T, tm, [(dy, D_MODEL, 0), (u, D_FF, 0)], [w_down], [(D_FF, BF16)], carry=carry)


def _hgrn_out_bwd(name, dx, o_raw, z, w, gn, tm=512):
    T = dx.shape[0]
    tm = min(tm, T)

    def body(dx_ref, o_ref, g_ref, w_ref, gn_ref, do_ref, dg_ref, dgn_ref):
        @pl.when(pl.program_id(0) == 0)
        def _():
            dgn_ref[...] = jnp.zeros_like(dgn_ref)

        da = _dot_nt(dx_ref[...].astype(BF16), w_ref[...])
        gn_v = gn_ref[...]
        y, xhat, r = _rms(o_ref[...], gn_v)
        g = g_ref[...]
        sg = jax.nn.sigmoid(g)
        dg_ref[...] = (da * y * (sg * (1.0 + g * (1.0 - sg)))).astype(BF16)
        dyn = da * (g * sg)
        do, dgr = _rms_bwd(dyn, xhat, r, gn_v)
        do_ref[...] = do
        dgn_ref[...] += jnp.sum(dgr, axis=0, keepdims=True)

    return _row_call(name, body, T, tm, [(dx, D_MODEL, 0), (o_raw, D_MODEL, 0), (z, D_MODEL, 3)], [w, gn],
                     [(D_MODEL, F32), (D_MODEL, BF16)], [((1, D_MODEL), F32)])


COL_BLOCK = D_FF // N_DEV


def _mm_tn(name, a, b, shard=None, bm=1024, bn=1024, tk=2048, carry=(None, None)):
    T, M = a.shape
    N = b.shape[1]
    bm, bn, tk = min(bm, M), min(bn, N), min(tk, T)
    nk = T // tk
    if shard is None:
        out_shape, out_block = jax.ShapeDtypeStruct((M, N), F32), (bm, bn)
        out_map = lambda i, j, k: (i, j)
    elif shard == "cols":
        assert bn % COL_BLOCK == 0 and N % bn == 0
        out_shape = jax.ShapeDtypeStruct((N // COL_BLOCK, M, COL_BLOCK), BF16)
        out_block = (bn // COL_BLOCK, bm, COL_BLOCK)
        out_map = lambda i, j, k: (j, i, 0)
    else:
        rows = M // N_DEV
        assert bm % rows == 0
        out_shape, out_block = jax.ShapeDtypeStruct((N_DEV, rows, N), BF16), (bm // rows, rows, bn)
        out_map = lambda i, j, k: (i, 0, j)

    grid = (M // bm, N // bn, nk)

    def body(*refs):
        i, j, k = pl.program_id(0), pl.program_id(1), pl.program_id(2)
        own, finish = _carried(carry, refs, 2, 1, (i == 0) & (j == 0) & (k == 0),
                               (i == grid[0] - 1) & (j == grid[1] - 1) & (k == nk - 1))
        a_ref, b_ref, o_ref, acc = own

        @pl.when(k == 0)
        def _():
            acc[...] = jnp.zeros_like(acc)

        acc[...] += _dot_tn(a_ref[...].astype(BF16), b_ref[...].astype(BF16))

        @pl.when(k == nk - 1)
        def _():
            if shard == "cols":
                for c in range(bn // COL_BLOCK):
                    o_ref[c] = acc[:, c * COL_BLOCK:(c + 1) * COL_BLOCK].astype(BF16)
            else:
                o_ref[...] = acc[...].reshape(out_block).astype(o_ref.dtype)

        finish()

    in_specs, out_specs, out_shapes, scratch, extra = _carried_specs(
        carry, [pl.BlockSpec((tk, bm), lambda i, j, k: (k, i)), pl.BlockSpec((tk, bn), lambda i, j, k: (k, j))],
        [pl.BlockSpec(out_block, out_map)], [out_shape], [pltpu.VMEM((bm, bn), F32)])
    res = pl.pallas_call(
        body, name=name, grid=grid, in_specs=in_specs, out_specs=out_specs, out_shape=out_shapes,
        scratch_shapes=scratch, compiler_params=_params(dimension_semantics=("arbitrary", "arbitrary", "arbitrary")),
    )(a, b, *extra)
    return res[0] if carry[0] is None else res


def _rot_fwd(x, tab):
    c, sa, sb = tab[:, :LANES], tab[:, LANES:2 * LANES], tab[:, 2 * LANES:]
    outs = []
    for j in range(x.shape[1] // LANES):
        xs = x[:, j * LANES:(j + 1) * LANES]
        outs.append(xs * c + pltpu.roll(xs, ROT_HALF, 1) * sa + pltpu.roll(xs, LANES - ROT_HALF, 1) * sb)
    return outs


def _rot_bwd(dys, tab):
    c, sa, sb = tab[:, :LANES], tab[:, LANES:2 * LANES], tab[:, 2 * LANES:]
    return [dy * c + pltpu.roll(dy * sa, LANES - ROT_HALF, 1) + pltpu.roll(dy * sb, ROT_HALF, 1) for dy in dys]


ATT_SCALE = HEAD_DIM ** -0.5
HEAD_LAG = 2


def _attn_masks(n):
    kj = lax.broadcasted_iota(jnp.int32, (2 * ATT_BLOCK, ATT_BLOCK), 0)
    qi = lax.broadcasted_iota(jnp.int32, (2 * ATT_BLOCK, ATT_BLOCK), 1)
    delta = qi + ATT_BLOCK - kj
    first_key = jnp.where(n > 0, 0, ATT_BLOCK)
    valid = (delta >= 0) & (delta < ATT_BLOCK) & (kj >= first_key)
    low = lax.broadcasted_iota(jnp.int32, (1, LANES), 1) < HEAD_DIM
    upper = lax.broadcasted_iota(jnp.int32, (LANES, 1), 0) < HEAD_DIM
    return valid, low, upper


def _softmax_sink(s, valid, sink):
    s = jnp.where(valid, s, NEG_INF)
    m = jnp.maximum(jnp.max(s, axis=0, keepdims=True), sink)
    e = jnp.exp(s - m)
    es = jnp.exp(sink - m)
    inv = 1.0 / (jnp.sum(e, axis=0, keepdims=True) + es)
    return e * inv, es * inv


def _attn_specs(nb, tables):
    prev = lambda n: jnp.maximum(jnp.minimum(n, nb - 1) - 1, 0)
    cur = lambda n: jnp.minimum(n, nb - 1)
    specs = [
        pl.BlockSpec((ATT_BLOCK, Q_DIM), lambda n: (cur(n), 0)),
        pl.BlockSpec((ATT_BLOCK, KV_DIM), lambda n: (prev(n), 4)),
        pl.BlockSpec((ATT_BLOCK, KV_DIM), lambda n: (cur(n), 4)),
        pl.BlockSpec((ATT_BLOCK, KV_DIM), lambda n: (prev(n), 5)),
        pl.BlockSpec((ATT_BLOCK, KV_DIM), lambda n: (cur(n), 5)),
    ]
    if tables:
        specs += [pl.BlockSpec((ATT_BLOCK, 3 * LANES), lambda n: (prev(n), 0)),
                  pl.BlockSpec((ATT_BLOCK, 3 * LANES), lambda n: (cur(n), 0))]
    return specs + [pl.BlockSpec(memory_space=pltpu.SMEM)]


def _kv_band(prev_ref, cur_ref):
    out = []
    for j in range(KV_DIM // LANES):
        sl = slice(j * LANES, (j + 1) * LANES)
        band = jnp.concatenate([prev_ref[:, sl], cur_ref[:, sl]], axis=0)
        out.append((band, pltpu.roll(band, HEAD_DIM, 1)))
    return out


def _bf16(bands, transposed=False):
    return [[(a.T if transposed else a).astype(BF16) for a in pair] for pair in bands]


def _attn_fwd(qkv, sinks, carry=(None, None)):
    T = qkv.shape[0]
    nb = T // ATT_BLOCK

    def body(*refs):
        n = pl.program_id(0)
        own, finish = _carried(carry, refs, 6, 1, n == 0, n == nb - 1)
        q_ref, kp_ref, kc_ref, vp_ref, vc_ref, sink_ref, o_ref = own
        valid, low, upper = _attn_masks(n)
        ks = _bf16(_kv_band(kp_ref, kc_ref))
        vts = _bf16(_kv_band(vp_ref, vc_ref), transposed=True)
        heads, outs = {}, {}

        def first(h):
            p, hf = h // 2, h % 2
            kpair, khalf = p // 4, (p // 2) % 2
            qm = jnp.where(low if hf == 0 else ~low, q_ref[:, p * LANES:(p + 1) * LANES] * ATT_SCALE, 0.0)
            sw = 0 if khalf == hf else 1
            heads[h] = (kpair, sw, _dot_nt(ks[kpair][sw], qm.astype(BF16)))

        def second(h):
            kpair, sw, s = heads[h]
            heads[h] = (kpair, sw, _softmax_sink(s, valid, sink_ref[0, h])[0].astype(BF16))

        def third(h):
            kpair, sw, pr = heads.pop(h)
            outs[h] = _dot(vts[kpair][sw], pr)
            if h % 2:
                o_ref[:, (h // 2) * LANES:(h // 2 + 1) * LANES] = jnp.where(upper, outs.pop(h - 1), outs.pop(h)).T.astype(BF16)

        for i in range(N_Q_HEADS + 2 * HEAD_LAG):
            if i < N_Q_HEADS:
                first(i)
            if 0 <= i - HEAD_LAG < N_Q_HEADS:
                second(i - HEAD_LAG)
            if 0 <= i - 2 * HEAD_LAG < N_Q_HEADS:
                third(i - 2 * HEAD_LAG)
        finish()

    in_specs, out_specs, out_shape, scratch, extra = _carried_specs(
        carry, _attn_specs(nb, False), [pl.BlockSpec((ATT_BLOCK, Q_DIM), lambda n: (n, 0))],
        [jax.ShapeDtypeStruct((T, Q_DIM), BF16)], [])
    return pl.pallas_call(
        body, name="attn_fwd", grid=(nb,), in_specs=in_specs, out_specs=out_specs, out_shape=out_shape,
        scratch_shapes=scratch, compiler_params=_params(dimension_semantics=("arbitrary",)),
    )(qkv, qkv, qkv, qkv, qkv, sinks, *extra)


def _attn_bwd(qkv, rot, sinks, dout, carry=(None, None)):
    T = qkv.shape[0]
    nb = T // ATT_BLOCK
    npair = KV_DIM // LANES

    def body(*refs):
        n = pl.program_id(0)
        own, finish = _carried(carry, refs, 9, 2, n == 0, n == nb)
        (q_ref, kp_ref, kc_ref, vp_ref, vc_ref, tp_ref, tc_ref, sink_ref, do_ref, dqkv_ref, dsink_ref,
         dq_c, dk_c, dv_c) = own

        @pl.when(n == 0)
        def _():
            dq_c[...] = jnp.zeros_like(dq_c)
            dk_c[...] = jnp.zeros_like(dk_c)
            dv_c[...] = jnp.zeros_like(dv_c)
            dsink_ref[...] = jnp.zeros_like(dsink_ref)

        def flush(dk_prev, dv_prev, tab_ref):
            dqkv_ref[:, :Q_DIM] = dq_c[...].astype(BF16)
            dk = _rot_bwd([dk_c[:, j * LANES:(j + 1) * LANES] + dk_prev[j] for j in range(npair)], tab_ref[...])
            for j in range(npair):
                dqkv_ref[:, Q_DIM + j * LANES:Q_DIM + (j + 1) * LANES] = dk[j].astype(BF16)
                dqkv_ref[:, Q_DIM + KV_DIM + j * LANES:Q_DIM + KV_DIM + (j + 1) * LANES] = (
                    dv_c[:, j * LANES:(j + 1) * LANES] + dv_prev[j]).astype(BF16)

        @pl.when(n < nb)
        def _():
            valid, low, upper = _attn_masks(n)
            lane = lax.broadcasted_iota(jnp.int32, (1, LANES), 1)
            k_band = _kv_band(kp_ref, kc_ref)
            ks, kts = _bf16(k_band), _bf16(k_band, transposed=True)
            vs = _bf16(_kv_band(vp_ref, vc_ref))
            dk_acc = [[jnp.zeros((2 * ATT_BLOCK, LANES), F32) for _ in range(2)] for _ in range(npair)]
            dv_acc = [[jnp.zeros((2 * ATT_BLOCK, LANES), F32) for _ in range(2)] for _ in range(npair)]
            dsink = jnp.zeros((1, LANES), F32)
            heads, dq_t, dsinks = {}, {}, []

            def first(h):
                p, hf = h // 2, h % 2
                kpair, khalf = p // 4, (p // 2) % 2
                sel = low if hf == 0 else ~low
                qm = jnp.where(sel, q_ref[:, p * LANES:(p + 1) * LANES] * ATT_SCALE, 0.0).astype(BF16)
                dom = jnp.where(sel, do_ref[:, p * LANES:(p + 1) * LANES], 0.0).astype(BF16)
                sw = 0 if khalf == hf else 1
                heads[h] = dict(kpair=kpair, sw=sw, qm=qm, dom=dom, s=_dot_nt(ks[kpair][sw], qm),
                                dp=_dot_nt(vs[kpair][sw], dom))

            def second(h):
                d = heads[h]
                pr, ps = _softmax_sink(d.pop("s"), valid, sink_ref[0, h])
                dp = d.pop("dp")
                dd = jnp.sum(pr * dp, axis=0, keepdims=True)
                dsinks.append(jnp.where(lane == h, -jnp.sum(ps * dd, axis=1, keepdims=True), 0.0))
                d["ds"] = (pr * (dp - dd)).astype(BF16)
                d["pr"] = pr.astype(BF16)

            def third(h):
                d = heads.pop(h)
                kpair, sw = d["kpair"], d["sw"]
                dq_t[h] = _dot(kts[kpair][sw], d["ds"])
                dk_acc[kpair][sw] = dk_acc[kpair][sw] + _dot(d["ds"], d["qm"])
                dv_acc[kpair][sw] = dv_acc[kpair][sw] + _dot(d["pr"], d["dom"])

            for i in range(N_Q_HEADS + 2 * HEAD_LAG):
                if i < N_Q_HEADS:
                    first(i)
                if 0 <= i - HEAD_LAG < N_Q_HEADS:
                    second(i - HEAD_LAG)
                if 0 <= i - 2 * HEAD_LAG < N_Q_HEADS:
                    third(i - 2 * HEAD_LAG)
            dsink = sum(dsinks, dsink)
            dqs = [jnp.where(upper, dq_t[2 * p], dq_t[2 * p + 1]).T * ATT_SCALE for p in range(Q_DIM // LANES)]
            dk_acc = [a[0] + pltpu.roll(a[1], HEAD_DIM, 1) for a in dk_acc]
            dv_acc = [a[0] + pltpu.roll(a[1], HEAD_DIM, 1) for a in dv_acc]
            flush([a[:ATT_BLOCK] for a in dk_acc], [a[:ATT_BLOCK] for a in dv_acc], tp_ref)
            dq = _rot_bwd(dqs, tc_ref[...])
            for p in range(Q_DIM // LANES):
                dq_c[:, p * LANES:(p + 1) * LANES] = dq[p]
            for j in range(npair):
                dk_c[:, j * LANES:(j + 1) * LANES] = dk_acc[j][ATT_BLOCK:]
                dv_c[:, j * LANES:(j + 1) * LANES] = dv_acc[j][ATT_BLOCK:]
            dsink_ref[...] += dsink

        @pl.when(n == nb)
        def _():
            zero = [jnp.zeros((ATT_BLOCK, LANES), F32) for _ in range(npair)]
            flush(zero, zero, tc_ref)

        finish()

    do_spec = pl.BlockSpec((ATT_BLOCK, Q_DIM), lambda n: (jnp.minimum(n, nb - 1), 0))
    in_specs, out_specs, out_shape, scratch, extra = _carried_specs(
        carry, _attn_specs(nb, True) + [do_spec],
        [pl.BlockSpec((ATT_BLOCK, QKV_DIM), lambda n: (jnp.maximum(n - 1, 0), 0)),
         pl.BlockSpec((1, LANES), lambda n: (0, 0))],
        [jax.ShapeDtypeStruct((T, QKV_DIM), BF16), jax.ShapeDtypeStruct((1, LANES), F32)],
        [pltpu.VMEM((ATT_BLOCK, Q_DIM), F32), pltpu.VMEM((ATT_BLOCK, KV_DIM), F32),
         pltpu.VMEM((ATT_BLOCK, KV_DIM), F32)])
    return pl.pallas_call(
        body, name="attn_bwd", grid=(nb + 1,), in_specs=in_specs, out_specs=out_specs, out_shape=out_shape,
        scratch_shapes=scratch, compiler_params=_params(dimension_semantics=("arbitrary",)),
    )(qkv, qkv, qkv, qkv, qkv, rot, rot, sinks, dout, *extra)


LEVELS = (32, 16, 8, 4, 2, 1)
SUBLANES = 8
UNROLL = 16
UNROLL_BWD = 8


def _lower_bound(lb_ref):
    l0, l1 = lb_ref[0:1, :], lb_ref[1:2, :]
    mx = jnp.maximum(l0, l1)
    e0, e1 = jnp.exp(l0 - mx), jnp.exp(l1 - mx)
    return e1 / (e0 + e1)


GROUPS = CHUNK // SUBLANES


def _group_roll(x, k):
    return pltpu.roll(x.reshape(GROUPS, SUBLANES, HGRN_DK), k % SUBLANES, 1).reshape(CHUNK, HGRN_DK)


def _scan_rows(x, row, reverse):
    r8 = row & (SUBLANES - 1)
    for sh in (1, 2, 4):
        ok = (r8 < SUBLANES - sh) if reverse else (r8 >= sh)
        x = x + jnp.where(ok, _group_roll(x, -sh if reverse else sh), 0.0)
    g = x.reshape(GROUPS, SUBLANES, HGRN_DK)
    edge = 0 if reverse else SUBLANES - 1
    tot = jnp.broadcast_to(g[:, edge:edge + 1, :], g.shape)

    def shifted(a, n):
        z = jnp.zeros((n, SUBLANES, HGRN_DK), F32)
        return jnp.concatenate([a[n:], z] if reverse else [z, a[:GROUPS - n]], axis=0)

    acc = shifted(tot, 1)
    for sh in (1, 2, 4):
        acc = acc + shifted(acc, sh)
    return (g + acc).reshape(CHUNK, HGRN_DK)


def _level_masks():
    t = lax.broadcasted_iota(jnp.int32, (CHUNK, CHUNK), 0)
    s = lax.broadcasted_iota(jnp.int32, (CHUNK, CHUNK), 1)
    return [((t & h) != 0) & ((s & h) == 0) & ((t ^ s) < 2 * h) for h in LEVELS]


def _level_scales(b, forget, row):
    out = []
    for h in LEVELS[:3]:
        parts = [jnp.broadcast_to(b[j * 2 * h + h - 1:j * 2 * h + h, :], (2 * h, HGRN_DK))
                 for j in range(CHUNK // (2 * h))]
        mid = parts[0] if len(parts) == 1 else jnp.concatenate(parts, axis=0)
        out.append(jnp.exp(-jnp.abs(b - mid)))
    groups = b.reshape(GROUPS, SUBLANES, HGRN_DK)
    mid = jnp.broadcast_to(groups[:, SUBLANES // 2 - 1:SUBLANES // 2, :], groups.shape)
    e4 = jnp.exp(-jnp.abs(groups - mid)).reshape(CHUNK, HGRN_DK)
    f, r4 = forget, row & 3
    up1, dn1 = _group_roll(f, -1), _group_roll(f, 1)
    e2 = jnp.where(r4 == 0, up1, jnp.where(r4 == 1, 1.0, jnp.where(r4 == 2, f, dn1 * f)))
    e1 = jnp.where((row & 1) == 1, f, 1.0)
    return out + [e4, e2, e1]


def _hgrn_gates(zq, zf, lb):
    sq = jax.nn.sigmoid(zq)
    q = zq * sq
    sg = jax.nn.sigmoid(zf)
    forget = lb + (1.0 - lb) * sg
    return q, sq, sg, forget, 1.0 - forget, jnp.log(forget)


def _hgrn_specs(T, rb, rev):
    nr = T // rb
    ri = (lambda r: nr - 1 - r) if rev else (lambda r: r)
    return nr, ri, [
        pl.BlockSpec((rb, HGRN_DK), lambda h, r: (ri(r), h)),
        pl.BlockSpec((rb, HGRN_DK), lambda h, r: (ri(r), HGRN_HEADS + h)),
        pl.BlockSpec((rb, HGRN_DK), lambda h, r: (ri(r), 2 * HGRN_HEADS + h)),
        pl.BlockSpec((2, HGRN_DK), lambda h, r: (0, h)),
    ]


def _hgrn_fwd(z, lb_raw, rb=2048, carry=(None, None)):
    T = z.shape[0]
    rb = min(rb, T)
    ncb = rb // CHUNK
    unroll = min(UNROLL, ncb)
    assert ncb % unroll == 0
    nr, ri, in_specs = _hgrn_specs(T, rb, False)

    def body(*refs):
        hh, rr = pl.program_id(0), pl.program_id(1)
        own, finish = _carried(carry, refs, 4, 2, (hh == 0) & (rr == 0), (hh == HGRN_HEADS - 1) & (rr == nr - 1))
        zq_ref, zf_ref, zi_ref, lb_ref, o_ref, st_ref, state = own

        @pl.when(rr == 0)
        def _():
            state[...] = jnp.zeros_like(state)

        lb = _lower_bound(lb_ref)
        row = lax.broadcasted_iota(jnp.int32, (CHUNK, HGRN_DK), 0)
        masks = _level_masks()

        def operands(c):
            rows = pl.ds(pl.multiple_of(c * CHUNK, CHUNK), CHUNK)
            q, _, _, forget, k, lf = _hgrn_gates(zq_ref[rows, :], zf_ref[rows, :], lb)
            v = zi_ref[rows, :]
            b = _scan_rows(lf, row, False)
            pairs = [((q * e).astype(BF16), (k * e).astype(BF16)) for e in _level_scales(b, forget, row)]
            b_last = b[CHUNK - 1:CHUNK, :]
            return dict(c=c, rows=rows, pairs=pairs, vb=v.astype(BF16), diag=jnp.sum(q * k, axis=-1, keepdims=True) * v,
                        kd=(k * jnp.exp(b_last - b)).astype(BF16), qd=(q * jnp.exp(b)).astype(BF16),
                        decay=jnp.exp(b_last))

        def group(i, st):
            parts = [operands(i * unroll + j) for j in range(unroll)]
            for p in parts:
                sc = jnp.zeros((CHUNK, CHUNK), F32)
                for (qs, ks), mask in zip(p["pairs"], masks):
                    sc = sc + jnp.where(mask, _dot_nt(qs, ks), 0.0)
                p["sc"] = sc.astype(BF16)
            for p in parts:
                p["o"] = _dot(p["sc"], p["vb"]) + p["diag"]
                p["gain"] = _dot_tn(p["vb"], p["kd"])
            for p in parts:
                st_ref[p["c"], 0] = st
                o_ref[p["rows"], :] = p["o"] + _dot_nt(p["qd"], st.astype(BF16))
                st = st * p["decay"] + p["gain"]
            return st

        state[...] = lax.fori_loop(0, ncb // unroll, group, state[...])
        finish()

    in_specs, out_specs, out_shape, scratch, extra = _carried_specs(
        carry, in_specs,
        [pl.BlockSpec((rb, HGRN_DK), lambda h, r: (r, h)),
         pl.BlockSpec((ncb, 1, HGRN_DK, HGRN_DK), lambda h, r: (r, h, 0, 0))],
        [jax.ShapeDtypeStruct((T, D_MODEL), F32),
         jax.ShapeDtypeStruct((T // CHUNK, HGRN_HEADS, HGRN_DK, HGRN_DK), F32)],
        [pltpu.VMEM((HGRN_DK, HGRN_DK), F32)])
    return pl.pallas_call(
        body, name="hgrn_fwd", grid=(HGRN_HEADS, nr), in_specs=in_specs, out_specs=out_specs, out_shape=out_shape,
        scratch_shapes=scratch, compiler_params=_params(dimension_semantics=("arbitrary", "arbitrary")),
    )(z, z, z, lb_raw, *extra)


def _hgrn_bwd(z, lb_raw, states, do, rb=2048, carry=(None, None)):
    T = z.shape[0]
    rb = min(rb, T)
    ncb = rb // CHUNK
    unroll = min(UNROLL_BWD, ncb)
    assert ncb % unroll == 0
    nr, ri, in_specs = _hgrn_specs(T, rb, True)
    in_specs += [pl.BlockSpec((ncb, 1, HGRN_DK, HGRN_DK), lambda h, r: (ri(r), h, 0, 0)),
                 pl.BlockSpec((rb, HGRN_DK), lambda h, r: (ri(r), h))]

    def body(*refs):
        hh, rr = pl.program_id(0), pl.program_id(1)
        own, finish = _carried(carry, refs, 6, 4, (hh == 0) & (rr == 0), (hh == HGRN_HEADS - 1) & (rr == nr - 1))
        zq_ref, zf_ref, zi_ref, lb_ref, st_ref, do_ref, dq_ref, df_ref, di_ref, dlb_ref, dstate = own

        @pl.when(rr == 0)
        def _():
            dstate[...] = jnp.zeros_like(dstate)
            dlb_ref[...] = jnp.zeros_like(dlb_ref)

        lb = _lower_bound(lb_ref)
        row = lax.broadcasted_iota(jnp.int32, (CHUNK, HGRN_DK), 0)
        masks = _level_masks()

        def operands(c):
            rows = pl.ds(pl.multiple_of(c * CHUNK, CHUNK), CHUNK)
            zq = zq_ref[rows, :]
            q, sq, sg, forget, k, lf = _hgrn_gates(zq, zf_ref[rows, :], lb)
            v = zi_ref[rows, :]
            dov = do_ref[rows, :]
            b = _scan_rows(lf, row, False)
            b_last = b[CHUNK - 1:CHUNK, :]
            eb, ebb = jnp.exp(b), jnp.exp(b_last - b)
            es = _level_scales(b, forget, row)
            return dict(rows=rows, zq=zq, q=q, sq=sq, sg=sg, forget=forget, k=k, v=v, dov=dov, eb=eb, ebb=ebb,
                        e_last=jnp.exp(b_last), es=es, st=st_ref[c, 0], dob=dov.astype(BF16), vb=v.astype(BF16),
                        pairs=[((q * e).astype(BF16), (k * e).astype(BF16)) for e in es],
                        qd=(q * eb).astype(BF16), kd=(k * ebb).astype(BF16))

        def group(i, dlb):
            parts = [operands(ncb - 1 - (i * unroll + j)) for j in range(unroll)]
            for p in parts:
                p["da"] = _dot_nt(p["dob"], p["vb"])
                sc = jnp.zeros((CHUNK, CHUNK), F32)
                for (qs, ks), mask in zip(p["pairs"], masks):
                    sc = sc + jnp.where(mask, _dot_nt(qs, ks), 0.0)
                p["sc"] = sc.astype(BF16)
                p["dq_state"] = _dot(p["dob"], p["st"].astype(BF16))
                p["gain"] = _dot_tn(p["dob"], p["qd"])
            dst = dstate[...]
            for p in parts:
                p["dst"] = dst
                dst = dst * p["e_last"] + p["gain"]
            dstate[...] = dst
            for p in parts:
                dstb = p["dst"].astype(BF16)
                dk_state = p["ebb"] * _dot(p["vb"], dstb)
                dq = p["eb"] * p["dq_state"]
                dk = dk_state
                dv = _dot_nt(p["kd"], dstb) + _dot_tn(p["sc"], p["dob"])
                for e, (qs, ks), mask in zip(p["es"], p["pairs"], masks):
                    dam = jnp.where(mask, p["da"], 0.0).astype(BF16)
                    dq = dq + e * _dot(dam, ks)
                    dk = dk + e * _dot_tn(dam, qs)
                dad = jnp.sum(p["dov"] * p["v"], axis=-1, keepdims=True)
                p["dq"] = dq + dad * p["k"]
                p["dk"] = dk + dad * p["q"]
                p["dv"] = dv + jnp.sum(p["q"] * p["k"], axis=-1, keepdims=True) * p["dov"]
                p["extra"] = (p["e_last"] * jnp.sum(p["dst"] * p["st"], axis=0, keepdims=True)
                              + jnp.sum(p["k"] * dk_state, axis=0, keepdims=True))
            for p in parts:
                q, k, sq, sg, zq, rows = p["q"], p["k"], p["sq"], p["sg"], p["zq"], p["rows"]
                dlf = _scan_rows(q * p["dq"] - k * p["dk"], row, True) + p["extra"]
                dforget = dlf / p["forget"] - p["dk"]
                dq_ref[rows, :] = (p["dq"] * (sq * (1.0 + zq * (1.0 - sq)))).astype(BF16)
                df_ref[rows, :] = (dforget * (1.0 - lb) * sg * (1.0 - sg)).astype(BF16)
                di_ref[rows, :] = p["dv"].astype(BF16)
                dlb = dlb + jnp.sum(dforget * (1.0 - sg), axis=0, keepdims=True)
            return dlb

        dlb_ref[...] += lax.fori_loop(0, ncb // unroll, group, jnp.zeros((1, HGRN_DK), F32))
        finish()

    blk = pl.BlockSpec((rb, HGRN_DK), lambda h, r: (ri(r), h))
    in_specs, out_specs, out_shape, scratch, extra = _carried_specs(
        carry, in_specs, [blk, blk, blk, pl.BlockSpec((1, HGRN_DK), lambda h, r: (0, h))],
        [jax.ShapeDtypeStruct((T, D_MODEL), BF16)] * 3 + [jax.ShapeDtypeStruct((1, D_MODEL), F32)],
        [pltpu.VMEM((HGRN_DK, HGRN_DK), F32)])
    return pl.pallas_call(
        body, name="hgrn_bwd", grid=(HGRN_HEADS, nr), in_specs=in_specs, out_specs=out_specs, out_shape=out_shape,
        scratch_shapes=scratch, compiler_params=_params(dimension_semantics=("arbitrary", "arbitrary")),
    )(z, z, z, lb_raw, states, do, *extra)


MESH = pl.DeviceIdType.MESH
ANY = pl.BlockSpec(memory_space=pl.ANY)


def _place():
    return lax.axis_index("x"), lax.axis_index("y"), lax.axis_index("c")


def _sems(n):
    return [pltpu.SemaphoreType.DMA((7 * n,)), pltpu.SemaphoreType.DMA((7 * n,)), pltpu.SemaphoreType.DMA((n,))]


class _Gather:
    def __init__(self, x_ref, out_ref, send_sems, recv_sems, local_sems, idx):
        self.x_ref, self.out_ref, self.send_sems, self.recv_sems, self.local_sem, self.base = (
            x_ref, out_ref, send_sems, recv_sems, local_sems.at[idx], 7 * idx)
        x, y, c = _place()
        self.c = c
        self.me, self.sibling = (x, y, c), (x, y, 1 - c)
        self.chips = [(1 - x, y), (x, 1 - y), (1 - x, 1 - y)]

    def rows(self, px, py, pc):
        return self.out_ref.at[4 * px + 2 * py + pc]

    def copy(self, k, block, to, from_input=False):
        return pltpu.make_async_remote_copy(
            src_ref=self.x_ref if from_input else self.rows(*block), dst_ref=self.rows(*block),
            send_sem=self.send_sems.at[self.base + k], recv_sem=self.recv_sems.at[self.base + k], device_id=to,
            device_id_type=MESH)

    def first(self):
        out = [self.copy(0, self.me, self.sibling, from_input=True)]
        return out + [self.copy(1 + j, self.me, (*chip, self.c), from_input=True) for j, chip in enumerate(self.chips)]

    def start(self):
        pltpu.make_async_copy(self.x_ref, self.rows(*self.me), self.local_sem).start()
        for cp in self.first():
            cp.start()

    def finish(self):
        passed = [self.copy(4 + j, (*chip, self.c), self.sibling) for j, chip in enumerate(self.chips)]
        for j, chip in enumerate(self.chips):
            self.copy(1 + j, (*chip, self.c), self.me).wait_recv()
            passed[j].start()
        self.copy(0, self.sibling, self.me).wait_recv()
        for j, chip in enumerate(self.chips):
            self.copy(4 + j, (*chip, 1 - self.c), self.me).wait_recv()
        for cp in self.first() + passed:
            cp.wait_send()
        pltpu.make_async_copy(self.x_ref, self.rows(*self.me), self.local_sem).wait()


class _Many:
    def __init__(self, kind, in_refs, out_refs, send_sems, recv_sems, local_sems):
        self.ops = [kind(x, o, send_sems, recv_sems, local_sems, i) for i, (x, o) in enumerate(zip(in_refs, out_refs))]

    def start(self):
        for op in self.ops:
            op.start()

    def finish(self):
        for op in self.ops:
            op.finish()


def _result_shapes(kind, arrs):
    return [jax.ShapeDtypeStruct(a.shape if kind is _Exchange else (N_DEV,) + a.shape, a.dtype) for a in arrs]


def _all_gather(name, shards):
    n = len(shards)

    def body(*refs):
        g = _Many(_Gather, refs[:n], refs[n:2 * n], *refs[2 * n:])
        g.start()
        g.finish()

    return pl.pallas_call(
        body, name=name, out_shape=_result_shapes(_Gather, shards), in_specs=[ANY] * n, out_specs=[ANY] * n,
        scratch_shapes=_sems(n),
    )(*shards)


def _peers(x, y, c):
    out = []
    for k in range(1, N_DEV):
        px = 1 - x if k & 4 else x
        py = 1 - y if k & 2 else y
        pc = 1 - c if k & 1 else c
        out.append((k, (px, py, pc), 4 * px + 2 * py + pc))
    return out


class _Exchange:
    def __init__(self, g_ref, recv_ref, send_sems, recv_sems, local_sems, idx):
        x, y, c = _place()
        me = 4 * x + 2 * y + c
        self.local = pltpu.make_async_copy(g_ref.at[me], recv_ref.at[me], local_sems.at[idx])
        self.copies = [
            pltpu.make_async_remote_copy(
                src_ref=g_ref.at[pidx], dst_ref=recv_ref.at[me], send_sem=send_sems.at[7 * idx + k - 1],
                recv_sem=recv_sems.at[7 * idx + k - 1], device_id=peer, device_id_type=MESH)
            for k, peer, pidx in _peers(x, y, c)]

    def start(self):
        self.local.start()
        for cp in self.copies:
            cp.start()

    def finish(self):
        for cp in self.copies:
            cp.wait()
        self.local.wait()


def _carried(carry, refs, n_in, n_out, first, last):
    kind, arrs = carry
    if kind is None:
        return refs, lambda: None
    n = len(arrs)
    ins, rest = refs[:n_in], refs[n_in + n:]
    outs, scratch = rest[:n_out], rest[n_out + n:]
    op = _Many(kind, refs[n_in:n_in + n], rest[n_out:n_out + n], *scratch[len(scratch) - 3:])

    @pl.when(first)
    def _():
        op.start()

    def finish():
        @pl.when(last)
        def _():
            op.finish()

    return tuple(ins) + tuple(outs) + tuple(scratch[:len(scratch) - 3]), finish


def _carried_specs(carry, in_specs, out_specs, out_shape, scratch):
    kind, arrs = carry
    if kind is None:
        return in_specs, out_specs, out_shape, scratch, []
    n = len(arrs)
    return (list(in_specs) + [ANY] * n, list(out_specs) + [ANY] * n,
            list(out_shape) + _result_shapes(kind, arrs), list(scratch) + _sems(n), list(arrs))


def _adamw(w, g, m, v):
    m = ADAM_B1 * m + (1.0 - ADAM_B1) * g
    v = ADAM_B2 * v + (1.0 - ADAM_B2) * (g * g)
    m_hat = m / (1.0 - ADAM_B1 ** ADAM_STEP)
    v_hat = v / (1.0 - ADAM_B2 ** ADAM_STEP)
    delta = -ADAM_LR * (m_hat / (jnp.sqrt(v_hat) + ADAM_EPS) + ADAM_WD * w)
    return delta, m, v


def _adamw_sum(name, recvs, w, m, v):
    L, R, C = w.shape
    tm = 128 if R % 128 == 0 else 64
    assert R % tm == 0 and len(recvs) == L

    def body(*refs):
        r_refs, (w_ref, m_ref, v_ref, g_ref, d_ref, nm_ref, nv_ref) = refs[:L], refs[L:]
        for l in range(L):
            g = r_refs[l][0].astype(F32)
            for s in range(1, N_DEV):
                g = g + r_refs[l][s].astype(F32)
            g_ref[l] = g
            d_ref[l], nm_ref[l], nv_ref[l] = _adamw(w_ref[l], g, m_ref[l], v_ref[l])

    blk = pl.BlockSpec((L, tm, C), lambda i: (0, i, 0))
    return pl.pallas_call(
        body, name=name, grid=(R // tm,),
        in_specs=[pl.BlockSpec((N_DEV, tm, C), lambda i: (0, i, 0))] * L + [blk, blk, blk],
        out_specs=[blk] * 4, out_shape=[jax.ShapeDtypeStruct((L, R, C), F32)] * 4,
        compiler_params=_params(dimension_semantics=("arbitrary",)),
    )(*recvs, w, m, v)


def _small_sync(part, w, m, v):
    def body(p_ref, w_ref, m_ref, v_ref, g_ref, d_ref, nm_ref, nv_ref, gath, send_sems, recv_sems):
        x, y, c = _place()
        me = 4 * x + 2 * y + c
        gath[me] = p_ref[...]
        copies = []
        for k, peer, _ in _peers(x, y, c):
            cp = pltpu.make_async_remote_copy(
                src_ref=p_ref, dst_ref=gath.at[me], send_sem=send_sems.at[k - 1], recv_sem=recv_sems.at[k - 1],
                device_id=peer, device_id_type=MESH)
            cp.start()
            copies.append(cp)
        for cp in copies:
            cp.wait()
        g = gath[0]
        for s in range(1, N_DEV):
            g = g + gath[s]
        wv = w_ref[...]
        l0, l1 = w_ref[8:9, :], w_ref[9:10, :]
        mx = jnp.maximum(l0, l1)
        e0, e1 = jnp.exp(l0 - mx), jnp.exp(l1 - mx)
        g9 = g[9:10, :] * (e0 / (e0 + e1)) * (e1 / (e0 + e1))
        row = lax.broadcasted_iota(jnp.int32, g.shape, 0)
        g = jnp.where(row == 9, g9, jnp.where(row == 8, -g9, g))
        g_ref[...] = g
        d_ref[...], nm_ref[...], nv_ref[...] = _adamw(wv, g, m_ref[...], v_ref[...])

    vm = pl.BlockSpec(memory_space=pltpu.VMEM)
    return pl.pallas_call(
        body, name="small_params_sync", in_specs=[vm] * 4, out_specs=[vm] * 4,
        out_shape=[jax.ShapeDtypeStruct(part.shape, F32)] * 4,
        scratch_shapes=[pltpu.VMEM((N_DEV,) + part.shape, F32), pltpu.SemaphoreType.DMA((7,)),
                        pltpu.SemaphoreType.DMA((7,))],
    )(part, w, m, v)


def _shards_bf16(d, pieces):
    return [d[name][layer].astype(BF16) for name, layer in pieces]


def _gathered(arrs, pieces, out):
    for a, (name, layer) in zip(arrs, pieces):
        out[name, layer] = a if name in COL_SHARDED else a.reshape(N_DEV * a.shape[1], a.shape[2])


def _pad_row(a, width=D_MODEL):
    a = a.reshape(1, -1)
    return jnp.pad(a, ((0, 0), (0, width - a.shape[1])))


LOSS_ROW = 11


def _pack_small(d, gn_full, loss=None):
    rows = [d["mix_norm"], d["mlp_norm"], d["final_norm"].reshape(1, D_MODEL),
            _pad_row(d["attn_b_qkv"], 2 * D_MODEL).reshape(2, D_MODEL), _pad_row(d["attn_sinks"]),
            d["hgrn_lower_bounds"], gn_full.reshape(1, D_MODEL)]
    if loss is not None:
        rows.append(_pad_row(loss))
    p = jnp.concatenate(rows, axis=0)
    return jnp.pad(p, ((0, SMALL_ROWS - p.shape[0]), (0, 0)))


def _unpack_small(p, me):
    return dict(
        mix_norm=p[0:2], mlp_norm=p[2:4], final_norm=p[4],
        attn_b_qkv=p[5:7].reshape(1, 2 * D_MODEL)[:, :QKV_DIM], attn_sinks=p[7:8, :N_Q_HEADS],
        hgrn_lower_bounds=p[8:10], hgrn_g_norm=lax.dynamic_slice(p[10:11], (0, me * 128), (1, 128)))


WEIGHT_NAMES = ['mix_norm', 'mlp_norm', 'final_norm', 'attn_w_qkv', 'attn_b_qkv', 'attn_sinks', 'attn_w_o', 'hgrn_w_in',
                'hgrn_g_norm', 'hgrn_w_o', 'hgrn_lower_bounds', 'mlp_w_up', 'mlp_w_down']
SMALL_NAMES = ('mix_norm', 'mlp_norm', 'final_norm', 'attn_b_qkv', 'attn_sinks', 'hgrn_lower_bounds', 'hgrn_g_norm')


def _rotary_tables(positions):
    inv_freq = ROPE_THETA ** (-jnp.arange(0, 2 * ROT_HALF, 2, dtype=F32) / (2 * ROT_HALF))
    ang = positions.astype(F32).reshape(-1, 1) * inv_freq
    cos, sin = jnp.cos(ang), jnp.sin(ang)
    r = jnp.arange(LANES) % HEAD_DIM
    idx = r % ROT_HALF
    c = jnp.where(r < 2 * ROT_HALF, cos[:, idx], 1.0)
    sa = jnp.where((r >= ROT_HALF) & (r < 2 * ROT_HALF), sin[:, idx], 0.0)
    sb = jnp.where(r < ROT_HALF, -sin[:, idx], 0.0)
    return jnp.concatenate([c, sa, sb], axis=1)


def kernel(x, positions, mix_norm, mlp_norm, final_norm, attn_w_qkv, attn_b_qkv, attn_sinks, attn_w_o, hgrn_w_in, hgrn_g_norm, hgrn_w_o, hgrn_lower_bounds, mlp_w_up, mlp_w_down, loss_target, m_mix_norm, m_mlp_norm, m_final_norm, m_attn_w_qkv, m_attn_b_qkv, m_attn_sinks, m_attn_w_o, m_hgrn_w_in, m_hgrn_g_norm, m_hgrn_w_o, m_hgrn_lower_bounds, m_mlp_w_up, m_mlp_w_down, v_mix_norm, v_mlp_norm, v_final_norm, v_attn_w_qkv, v_attn_b_qkv, v_attn_sinks, v_attn_w_o, v_hgrn_w_in, v_hgrn_g_norm, v_hgrn_w_o, v_hgrn_lower_bounds, v_mlp_w_up, v_mlp_w_down):
    w = dict(mix_norm=mix_norm, mlp_norm=mlp_norm, final_norm=final_norm, attn_w_qkv=attn_w_qkv, attn_b_qkv=attn_b_qkv,
             attn_sinks=attn_sinks, attn_w_o=attn_w_o, hgrn_w_in=hgrn_w_in, hgrn_g_norm=hgrn_g_norm, hgrn_w_o=hgrn_w_o,
             hgrn_lower_bounds=hgrn_lower_bounds, mlp_w_up=mlp_w_up, mlp_w_down=mlp_w_down)
    m = dict(mix_norm=m_mix_norm, mlp_norm=m_mlp_norm, final_norm=m_final_norm, attn_w_qkv=m_attn_w_qkv,
             attn_b_qkv=m_attn_b_qkv, attn_sinks=m_attn_sinks, attn_w_o=m_attn_w_o, hgrn_w_in=m_hgrn_w_in,
             hgrn_g_norm=m_hgrn_g_norm, hgrn_w_o=m_hgrn_w_o, hgrn_lower_bounds=m_hgrn_lower_bounds, mlp_w_up=m_mlp_w_up,
             mlp_w_down=m_mlp_w_down)
    v = dict(mix_norm=v_mix_norm, mlp_norm=v_mlp_norm, final_norm=v_final_norm, attn_w_qkv=v_attn_w_qkv,
             attn_b_qkv=v_attn_b_qkv, attn_sinks=v_attn_sinks, attn_w_o=v_attn_w_o, hgrn_w_in=v_hgrn_w_in,
             hgrn_g_norm=v_hgrn_g_norm, hgrn_w_o=v_hgrn_w_o, hgrn_lower_bounds=v_hgrn_lower_bounds, mlp_w_up=v_mlp_w_up,
             mlp_w_down=v_mlp_w_down)
    me = 4 * lax.axis_index("x") + 2 * lax.axis_index("y") + lax.axis_index("c")

    gn = hgrn_g_norm.reshape(1, 128)
    gn_a = gn.astype(BF16)
    gn_b = (gn - gn_a.astype(F32)).astype(BF16)
    gn_c = (gn - gn_a.astype(F32) - gn_b.astype(F32)).astype(BF16)
    gn_rows = jnp.pad(jnp.concatenate([gn_a, gn_b, gn_c], axis=1), ((0, 15), (0, D_MODEL - 3 * 128)))
    full = {}
    got = _all_gather("gather_attn_weights", _shards_bf16(w, GATHER_FIRST) + [gn_rows])
    _gathered(got[:1], GATHER_FIRST, full)
    w_qkv = full["attn_w_qkv", 0].transpose(1, 0, 2).reshape(D_MODEL, QKV_DIM)
    gn_terms = got[1][:, 0, :3 * 128].astype(F32).reshape(N_DEV, 3, 128)
    gn_full = ((gn_terms[:, 0] + gn_terms[:, 1]) + gn_terms[:, 2]).reshape(1, D_MODEL)

    x0 = x[0]
    tgt = loss_target[0]
    rot = _rotary_tables(positions)
    row = lambda a: a.reshape(1, -1)

    qkv, h0 = _norm_mm("qkv_proj", x0, row(mix_norm[0]), w_qkv, attn_b_qkv, rot=rot)
    att, *got = _attn_fwd(qkv, attn_sinks, carry=(_Gather, _shards_bf16(w, GATHER_ATTN)))
    _gathered(got, GATHER_ATTN, full)
    x1 = _mm_res("attn_out_proj", att, full["attn_w_o", 0], x0)
    u0, h1, *got = _norm_mm("mlp0_up", x1, row(mlp_norm[0]), full["mlp_w_up", 0],
                            carry=(_Gather, _shards_bf16(w, GATHER_MLP0_UP)))
    _gathered(got, GATHER_MLP0_UP, full)
    x2, a0, *got = _mlp_down("mlp0_down", u0, full["mlp_w_down", 0], x1,
                             carry=(_Gather, _shards_bf16(w, GATHER_MLP0_DOWN)))
    _gathered(got, GATHER_MLP0_DOWN, full)
    z, h2 = _norm_mm("hgrn_in_proj", x2, row(mix_norm[1]), full["hgrn_w_in", 0])
    o_raw, states, *got = _hgrn_fwd(z, hgrn_lower_bounds, carry=(_Gather, _shards_bf16(w, GATHER_HGRN)))
    _gathered(got, GATHER_HGRN, full)
    x3, o2 = _hgrn_out("hgrn_out_proj", o_raw, z, gn_full, full["hgrn_w_o", 0], x2)
    u1, h3 = _norm_mm("mlp1_up", x3, row(mlp_norm[1]), full["mlp_w_up", 1])
    dx4, a1, loss_part, g_final = _mlp_down("mlp1_down_loss", u1, full["mlp_w_down", 1], x3,
                                            loss_head=(tgt, row(final_norm)))

    gw = {}
    du1, = _mlp_bwd_act("mlp1_bwd_act", dx4, u1, full["mlp_w_down", 1])
    dx3, g_mlp1 = _mm_nt_rmsbwd("mlp1_bwd_in", du1, full["mlp_w_up", 1], x3, row(mlp_norm[1]), dx4)
    gw["mlp_w_down", 1] = _mm_tn("mlp1_dw_down", a1, dx4, "rows")
    gw["mlp_w_up", 1] = _mm_tn("mlp1_dw_up", h3, du1, "cols")

    do_raw, dg, g_gn = _hgrn_out_bwd("hgrn_out_bwd", dx3, o_raw, z, full["hgrn_w_o", 0], gn_full)
    gw["hgrn_w_o", 0] = _mm_tn("hgrn_dw_o", o2, dx3, "rows")
    recvs = {}
    dzq, dzf, dzi, g_lb, *recv = _hgrn_bwd(z, hgrn_lower_bounds, states, do_raw,
                                           carry=(_Exchange, [gw[p] for p in GRADS_HGRN]))
    recvs.update(zip(GRADS_HGRN, recv))
    dz = [dzq, dzf, dzi, dg]
    dx2, g_mix1 = _mm_nt_rmsbwd("hgrn_in_bwd", dz, full["hgrn_w_in", 0], x2, row(mix_norm[1]), dx3)
    gw["hgrn_w_in", 0] = jnp.concatenate(
        [_mm_tn(f"hgrn_dw_in{j}", h2, d, "cols") for j, d in enumerate(dz)], axis=0)

    du0, = _mlp_bwd_act("mlp0_bwd_act", dx2, u0, full["mlp_w_down", 0])
    dx1, g_mlp0 = _mm_nt_rmsbwd("mlp0_bwd_in", du0, full["mlp_w_up", 0], x1, row(mlp_norm[0]), dx2)
    gw["mlp_w_down", 0], recvs["hgrn_w_in", 0] = _mm_tn(
        "mlp0_dw_down", a0, dx2, "rows", carry=(_Exchange, [gw["hgrn_w_in", 0]]))
    gw["mlp_w_up", 0], recvs["mlp_w_down", 0] = _mm_tn(
        "mlp0_dw_up", h1, du0, "cols", carry=(_Exchange, [gw["mlp_w_down", 0]]))

    datt = _mm_nt("attn_out_bwd", dx1, full["attn_w_o", 0], BF16)
    gw["attn_w_o", 0] = _mm_tn("attn_dw_o", att, dx1, "rows")
    dqkv, g_sink, *recv = _attn_bwd(qkv, rot, attn_sinks, datt, carry=(_Exchange, [gw[p] for p in GRADS_ATTN]))
    recvs.update(zip(GRADS_ATTN, recv))
    g_qkv = _mm_tn("attn_dw_qkv", h0, dqkv, bn=512)
    g_qkv = g_qkv.reshape(D_MODEL, N_DEV, QKV_DIM // N_DEV).transpose(1, 0, 2).astype(BF16)
    dx0, g_mix0, g_bqkv, recvs["attn_w_qkv", 0] = _mm_nt_rmsbwd(
        "qkv_bwd", dqkv, w_qkv, x0, row(mix_norm[0]), dx1, with_colsum=True, carry=(_Exchange, [g_qkv]))

    big = {name: _adamw_sum("adamw_" + name, [recvs[name, l] for l in range(w[name].shape[0])], w[name], m[name], v[name])
           for name in BIG_NAMES}

    zero_row = jnp.zeros((1, D_MODEL), F32)
    part = _pack_small(dict(
        mix_norm=jnp.concatenate([g_mix0, g_mix1], axis=0), mlp_norm=jnp.concatenate([g_mlp0, g_mlp1], axis=0),
        final_norm=g_final, attn_b_qkv=g_bqkv, attn_sinks=g_sink[:, :N_Q_HEADS],
        hgrn_lower_bounds=jnp.concatenate([zero_row, g_lb], axis=0)), g_gn, loss=loss_part)

    def spread(a):
        return lax.dynamic_update_slice(zero_row, a.reshape(1, 128), (0, me * 128))

    small_in = [_pack_small({n: d[n] for n in SMALL_NAMES if n != "hgrn_g_norm"}, spread(d["hgrn_g_norm"]))
                for d in (w, m, v)]
    synced = _small_sync(part, *small_in)
    small = [_unpack_small(p, me) for p in synced]

    outs = [synced[0][LOSS_ROW, 0], dx0.reshape(x.shape)]
    for kind, grp_small in enumerate(small):
        for name in WEIGHT_NAMES:
            val = grp_small[name] if name in SMALL_NAMES else big[name][kind]
            outs.append(val.reshape(w[name].shape))
    return tuple(outs)
```

```python
import functools

import jax
import jax.numpy as jnp
from jax import lax
from jax.experimental import pallas as pl
from jax.experimental.pallas import tpu as pltpu

F32 = jnp.float32
BF16 = jnp.bfloat16

D_MODEL = 1024
HEAD_DIM = 64
N_Q_HEADS = 16
Q_DIM = 1024
KV_DIM = 256
QKV_DIM = 1536
ATT_BLOCK = 128
ROT_HALF = 8
ROPE_THETA = 500000.0
NEG_INF = -1e30
HGRN_HEADS = 8
HGRN_DK = 128
CHUNK = 64
D_FF = 4096
NORM_EPS = 1e-5
N_DEV = 8

ADAM_LR = 0.001
ADAM_B1 = 0.9
ADAM_B2 = 0.999
ADAM_EPS = 1e-08
ADAM_WD = 0.01
ADAM_STEP = 10

LANES = 128
VMEM_LIMIT = 56 * 1024 * 1024

GATHER_FIRST = (("attn_w_qkv", 0),)
GATHER_ATTN = (("attn_w_o", 0), ("mlp_w_up", 0), ("mlp_w_down", 0))
GATHER_MLP0 = (("hgrn_w_in", 0), ("hgrn_w_o", 0))
GATHER_HGRN = (("mlp_w_up", 1), ("mlp_w_down", 1))
GRADS_HGRN = (("mlp_w_down", 1), ("mlp_w_up", 1), ("hgrn_w_o", 0))
GRADS_ATTN = (("mlp_w_up", 0), ("attn_w_o", 0))
COL_SHARDED = ("attn_w_qkv", "hgrn_w_in", "mlp_w_up")
BIG_NAMES = ("attn_w_qkv", "attn_w_o", "hgrn_w_in", "hgrn_w_o", "mlp_w_up", "mlp_w_down")
SMALL_ROWS = 16


def _dot(a, b):
    return jnp.dot(a, b, preferred_element_type=F32)


def _dot_nt(a, b):
    return lax.dot_general(a, b, (((1,), (1,)), ((), ())), preferred_element_type=F32)


def _dot_tn(a, b):
    return lax.dot_general(a, b, (((0,), (0,)), ((), ())), preferred_element_type=F32)


def _params(**kw):
    return pltpu.CompilerParams(vmem_limit_bytes=VMEM_LIMIT, **kw)


def _full_spec(a):
    nd = a.ndim
    return pl.BlockSpec(a.shape, lambda *_: (0,) * nd)


def _row_call(name, body, n_rows, tm, row_ins, full_ins, row_outs, acc_outs=(), carry=(None, None)):
    steps = n_rows // tm
    in_specs = [pl.BlockSpec((tm, w), functools.partial(lambda i, cb: (i, cb), cb=cb)) for _, w, cb in row_ins]
    in_specs += [_full_spec(a) for a in full_ins]
    out_shape = [jax.ShapeDtypeStruct((n_rows, w), dt) for w, dt in row_outs]
    out_specs = [pl.BlockSpec((tm, w), lambda i: (i, 0)) for w, _ in row_outs]
    for shp, dt in acc_outs:
        out_shape.append(jax.ShapeDtypeStruct(shp, dt))
        out_specs.append(pl.BlockSpec(shp, functools.partial(lambda i, nd: (0,) * nd, nd=len(shp))))
    n_in, n_out = len(in_specs), len(out_specs)
    in_specs, out_specs, out_shape, scratch, extra = _carried_specs(carry, in_specs, out_specs, out_shape, [])

    def wrapped(*refs):
        i = pl.program_id(0)
        own, finish = _carried(carry, refs, n_in, n_out, i == 0, i == steps - 1)
        body(*own)
        finish()

    return pl.pallas_call(
        wrapped, name=name, grid=(steps,), in_specs=in_specs, out_specs=out_specs, out_shape=out_shape,
        scratch_shapes=scratch, compiler_params=_params(dimension_semantics=("arbitrary",)),
    )(*[a for a, _, _ in row_ins], *full_ins, *extra)


def _rms(x, gain):
    r = lax.rsqrt(jnp.mean(x * x, axis=-1, keepdims=True) + NORM_EPS)
    xhat = x * r
    return xhat * gain, xhat, r


def _rms_bwd(dy, xhat, r, gain):
    dxhat = dy * gain
    dx = r * (dxhat - xhat * jnp.mean(dxhat * xhat, axis=-1, keepdims=True))
    return dx, dy * xhat


def _norm_mm(name, x, gain, w, bias=None, rot=None, tm=512, carry=(None, None)):
    T = x.shape[0]
    tm = min(tm, T)
    nc = 512
    blocked = w.ndim == 3
    n = N_DEV * w.shape[2] if blocked else w.shape[1]
    assert n % nc == 0 and (not blocked or w.shape[2] == nc)

    def body(*refs):
        x_ref, refs = refs[0], refs[1:]
        if rot is not None:
            t_ref, refs = refs[0], refs[1:]
        g_ref, w_ref, refs = refs[0], refs[1], refs[2:]
        if bias is not None:
            b_ref, refs = refs[0], refs[1:]
        y_ref, h_ref = refs
        h, _, _ = _rms(x_ref[...], g_ref[...])
        hb = h.astype(BF16)
        h_ref[...] = hb
        for c in range(n // nc):
            sl = slice(c * nc, (c + 1) * nc)
            y = _dot(hb, w_ref[c] if blocked else w_ref[:, sl])
            if bias is not None:
                y = y + b_ref[:, sl]
            if rot is None:
                y_ref[:, sl] = y
            else:
                n_rot = max(0, min(nc, Q_DIM + KV_DIM - c * nc)) // LANES
                pieces = _rot_fwd(y[:, :n_rot * LANES], t_ref[...]) if n_rot else []
                for j in range(nc // LANES):
                    col = slice(c * nc + j * LANES, c * nc + (j + 1) * LANES)
                    y_ref[:, col] = pieces[j] if j < n_rot else y[:, j * LANES:(j + 1) * LANES]

    rows = [(x, D_MODEL, 0)] + ([(rot, 3 * LANES, 0)] if rot is not None else [])
    full = [gain, w] + ([bias] if bias is not None else [])
    return _row_call(name, body, T, tm, rows, full, [(n, F32), (D_MODEL, BF16)], carry=carry)


def _mm_res(name, a, w, res, tm=512):
    T = a.shape[0]
    tm = min(tm, T)

    def body(a_ref, r_ref, w_ref, o_ref):
        o_ref[...] = r_ref[...] + _dot(a_ref[...], w_ref[...])

    return _row_call(name, body, T, tm, [(a, a.shape[1], 0), (res, D_MODEL, 0)], [w], [(D_MODEL, F32)])[0]


def _mlp_down(name, u, w, res, tm=512, loss_head=None):
    T = u.shape[0]
    tm = min(tm, T)
    kc = 1024
    sub = min(256, tm)

    def body(*refs):
        if loss_head is None:
            u_ref, r_ref, w_ref, o_ref, a_ref = refs
        else:
            u_ref, r_ref, t_ref, w_ref, g_ref, o_ref, a_ref, loss_ref, dg_ref = refs

            @pl.when(pl.program_id(0) == 0)
            def _():
                loss_ref[...] = jnp.zeros_like(loss_ref)
                dg_ref[...] = jnp.zeros_like(dg_ref)

        for r0 in range(0, tm, sub):
            rs = slice(r0, r0 + sub)
            acc = r_ref[rs, :]
            for c in range(D_FF // kc):
                sl = slice(c * kc, (c + 1) * kc)
                a = jnp.maximum(u_ref[rs, sl], 0.0)
                ab = (a * a).astype(BF16)
                a_ref[rs, sl] = ab
                acc = acc + _dot(ab, w_ref[sl, :])
            if loss_head is None:
                o_ref[rs, :] = acc
            else:
                gain_v = g_ref[...]
                y, xhat, r = _rms(acc, gain_v)
                diff = y - t_ref[rs, :]
                per_row = jnp.sum(diff * diff, axis=-1, keepdims=True) * (1.0 / D_MODEL)
                loss_ref[...] += jnp.broadcast_to(0.5 * jnp.sum(per_row, axis=0, keepdims=True), loss_ref.shape)
                dx, dgr = _rms_bwd(diff * (1.0 / D_MODEL), xhat, r, gain_v)
                o_ref[rs, :] = dx
                dg_ref[...] += jnp.sum(dgr, axis=0, keepdims=True)

    rows, full, acc_outs = [(u, D_FF, 0), (res, D_MODEL, 0)], [w], []
    if loss_head is not None:
        rows, full = rows + [(loss_head[0], D_MODEL, 0)], full + [loss_head[1]]
        acc_outs = [((1, LANES), F32), ((1, D_MODEL), F32)]
    return _row_call(name, body, T, tm, rows, full, [(D_MODEL, F32), (D_FF, BF16)], acc_outs)


def _hgrn_out(name, o_raw, z, gn, w, res, tm=512):
    T = o_raw.shape[0]
    tm = min(tm, T)

    def body(o_ref, g_ref, r_ref, gn_ref, w_ref, x_ref, a_ref):
        y, _, _ = _rms(o_ref[...], gn_ref[...])
        g = g_ref[...]
        a = (y * (g * jax.nn.sigmoid(g))).astype(BF16)
        a_ref[...] = a
        x_ref[...] = r_ref[...] + _dot(a, w_ref[...])

    return _row_call(name, body, T, tm, [(o_raw, D_MODEL, 0), (z, D_MODEL, 3), (res, D_MODEL, 0)], [gn, w],
                     [(D_MODEL, F32), (D_MODEL, BF16)])


def _mm_nt_rmsbwd(name, dy, w, x, gain, dres, tm=512, with_colsum=False, carry=(None, None)):
    T = x.shape[0]
    tm = min(tm, T)
    dys = list(dy) if isinstance(dy, (list, tuple)) else [dy]
    width = dys[0].shape[1]
    n = width * len(dys)
    sub = min(256, tm)
    assert not with_colsum or len(dys) == 1

    def body(*refs):
        dy_refs, refs = refs[:len(dys)], refs[len(dys):]
        if with_colsum:
            x_ref, dr_ref, w_ref, g_ref, dx_ref, dg_ref, cs_ref = refs
        else:
            x_ref, dr_ref, w_ref, g_ref, dx_ref, dg_ref = refs

        @pl.when(pl.program_id(0) == 0)
        def _():
            dg_ref[...] = jnp.zeros_like(dg_ref)
            if with_colsum:
                cs_ref[...] = jnp.zeros_like(cs_ref)

        gain_v = g_ref[...]
        for r0 in range(0, tm, sub):
            rs = slice(r0, r0 + sub)
            if w.ndim == 3:
                nb = w.shape[2]
                dh = None
                for p in range(N_DEV):
                    piece, off = divmod(p * nb, width)
                    part = _dot_nt(dy_refs[piece][rs, off:off + nb].astype(BF16), w_ref[p])
                    dh = part if dh is None else dh + part
            else:
                dh = _dot_nt(dy_refs[0][rs, :].astype(BF16), w_ref[...])
            _, xhat, r = _rms(x_ref[rs, :], gain_v)
            dx, dgr = _rms_bwd(dh, xhat, r, gain_v)
            dx_ref[rs, :] = dr_ref[rs, :] + dx
            dg_ref[...] += jnp.sum(dgr, axis=0, keepdims=True)
            if with_colsum:
                cs_ref[...] += jnp.sum(dy_refs[0][rs, :].astype(F32), axis=0, keepdims=True)

    acc = [((1, D_MODEL), F32)] + ([((1, n), F32)] if with_colsum else [])
    rows = [(d, width, 0) for d in dys] + [(x, D_MODEL, 0), (dres, D_MODEL, 0)]
    return _row_call(name, body, T, tm, rows, [w, gain], [(D_MODEL, F32)], acc, carry=carry)


def _mm_nt(name, dy, w, out_dtype, tm=512):
    T = dy.shape[0]
    tm = min(tm, T)
    k = w.shape[0]

    def body(dy_ref, w_ref, o_ref):
        o_ref[...] = _dot_nt(dy_ref[...].astype(BF16), w_ref[...]).astype(out_dtype)

    return _row_call(name, body, T, tm, [(dy, dy.shape[1], 0)], [w], [(k, out_dtype)])[0]


def _mlp_bwd_act(name, dy, u, w_down, tm=512, carry=(None, None)):
    T = u.shape[0]
    tm = min(tm, T)
    kc = 1024

    def body(dy_ref, u_ref, w_ref, du_ref):
        dyb = dy_ref[...].astype(BF16)
        for c in range(D_FF // kc):
            sl = slice(c * kc, (c + 1) * kc)
            da = _dot_nt(dyb, w_ref[sl, :])
            du_ref[:, sl] = (da * (2.0 * jnp.maximum(u_ref[:, sl], 0.0))).astype(BF16)

    return _row_call(name, body, T, tm, [(dy, D_MODEL, 0), (u, D_FF, 0)], [w_down], [(D_FF, BF16)], carry=carry)


def _hgrn_out_bwd(name, dx, o_raw, z, w, gn, tm=512):
    T = dx.shape[0]
    tm = min(tm, T)

    def body(dx_ref, o_ref, g_ref, w_ref, gn_ref, do_ref, dg_ref, dgn_ref):
        @pl.when(pl.program_id(0) == 0)
        def _():
            dgn_ref[...] = jnp.zeros_like(dgn_ref)

        da = _dot_nt(dx_ref[...].astype(BF16), w_ref[...])
        gn_v = gn_ref[...]
        y, xhat, r = _rms(o_ref[...], gn_v)
        g = g_ref[...]
        sg = jax.nn.sigmoid(g)
        dg_ref[...] = (da * y * (sg * (1.0 + g * (1.0 - sg)))).astype(BF16)
        dyn = da * (g * sg)
        do, dgr = _rms_bwd(dyn, xhat, r, gn_v)
        do_ref[...] = do
        dgn_ref[...] += jnp.sum(dgr, axis=0, keepdims=True)

    return _row_call(name, body, T, tm, [(dx, D_MODEL, 0), (o_raw, D_MODEL, 0), (z, D_MODEL, 3)], [w, gn],
                     [(D_MODEL, F32), (D_MODEL, BF16)], [((1, D_MODEL), F32)])


COL_BLOCK = D_FF // N_DEV


def _mm_tn(name, a, b, shard=None, bm=1024, bn=1024, tk=2048, carry=(None, None)):
    T, M = a.shape
    N = b.shape[1]
    bm, bn, tk = min(bm, M), min(bn, N), min(tk, T)
    nk = T // tk
    if shard is None:
        out_shape, out_block = jax.ShapeDtypeStruct((M, N), F32), (bm, bn)
        out_map = lambda i, j, k: (i, j)
    elif shard == "cols":
        assert bn % COL_BLOCK == 0 and N % bn == 0
        out_shape = jax.ShapeDtypeStruct((N // COL_BLOCK, M, COL_BLOCK), BF16)
        out_block = (bn // COL_BLOCK, bm, COL_BLOCK)
        out_map = lambda i, j, k: (j, i, 0)
    else:
        rows = M // N_DEV
        assert bm % rows == 0
        out_shape, out_block = jax.ShapeDtypeStruct((N_DEV, rows, N), BF16), (bm // rows, rows, bn)
        out_map = lambda i, j, k: (i, 0, j)

    grid = (M // bm, N // bn, nk)

    def body(*refs):
        i, j, k = pl.program_id(0), pl.program_id(1), pl.program_id(2)
        own, finish = _carried(carry, refs, 2, 1, (i == 0) & (j == 0) & (k == 0),
                               (i == grid[0] - 1) & (j == grid[1] - 1) & (k == nk - 1))
        a_ref, b_ref, o_ref, acc = own

        @pl.when(k == 0)
        def _():
            acc[...] = jnp.zeros_like(acc)

        acc[...] += _dot_tn(a_ref[...].astype(BF16), b_ref[...].astype(BF16))

        @pl.when(k == nk - 1)
        def _():
            if shard == "cols":
                for c in range(bn // COL_BLOCK):
                    o_ref[c] = acc[:, c * COL_BLOCK:(c + 1) * COL_BLOCK].astype(BF16)
            else:
                o_ref[...] = acc[...].reshape(out_block).astype(o_ref.dtype)

        finish()

    in_specs, out_specs, out_shapes, scratch, extra = _carried_specs(
        carry, [pl.BlockSpec((tk, bm), lambda i, j, k: (k, i)), pl.BlockSpec((tk, bn), lambda i, j, k: (k, j))],
        [pl.BlockSpec(out_block, out_map)], [out_shape], [pltpu.VMEM((bm, bn), F32)])
    res = pl.pallas_call(
        body, name=name, grid=grid, in_specs=in_specs, out_specs=out_specs, out_shape=out_shapes,
        scratch_shapes=scratch, compiler_params=_params(dimension_semantics=("arbitrary", "arbitrary", "arbitrary")),
    )(a, b, *extra)
    return res[0] if carry[0] is None else res


def _rot_fwd(x, tab):
    c, sa, sb = tab[:, :LANES], tab[:, LANES:2 * LANES], tab[:, 2 * LANES:]
    outs = []
    for j in range(x.shape[1] // LANES):
        xs = x[:, j * LANES:(j + 1) * LANES]
        outs.append(xs * c + pltpu.roll(xs, ROT_HALF, 1) * sa + pltpu.roll(xs, LANES - ROT_HALF, 1) * sb)
    return outs


def _rot_bwd(dys, tab):
    c, sa, sb = tab[:, :LANES], tab[:, LANES:2 * LANES], tab[:, 2 * LANES:]
    return [dy * c + pltpu.roll(dy * sa, LANES - ROT_HALF, 1) + pltpu.roll(dy * sb, ROT_HALF, 1) for dy in dys]


ATT_SCALE = HEAD_DIM ** -0.5
HEAD_LAG = 2


def _attn_masks(n):
    kj = lax.broadcasted_iota(jnp.int32, (2 * ATT_BLOCK, ATT_BLOCK), 0)
    qi = lax.broadcasted_iota(jnp.int32, (2 * ATT_BLOCK, ATT_BLOCK), 1)
    delta = qi + ATT_BLOCK - kj
    first_key = jnp.where(n > 0, 0, ATT_BLOCK)
    valid = (delta >= 0) & (delta < ATT_BLOCK) & (kj >= first_key)
    low = lax.broadcasted_iota(jnp.int32, (1, LANES), 1) < HEAD_DIM
    upper = lax.broadcasted_iota(jnp.int32, (LANES, 1), 0) < HEAD_DIM
    return valid, low, upper


def _softmax_sink(s, valid, sink):
    s = jnp.where(valid, s, NEG_INF)
    m = jnp.maximum(jnp.max(s, axis=0, keepdims=True), sink)
    e = jnp.exp(s - m)
    es = jnp.exp(sink - m)
    inv = 1.0 / (jnp.sum(e, axis=0, keepdims=True) + es)
    return e * inv, es * inv


def _attn_specs(nb, tables):
    prev = lambda n: jnp.maximum(jnp.minimum(n, nb - 1) - 1, 0)
    cur = lambda n: jnp.minimum(n, nb - 1)
    specs = [
        pl.BlockSpec((ATT_BLOCK, Q_DIM), lambda n: (cur(n), 0)),
        pl.BlockSpec((ATT_BLOCK, KV_DIM), lambda n: (prev(n), 4)),
        pl.BlockSpec((ATT_BLOCK, KV_DIM), lambda n: (cur(n), 4)),
        pl.BlockSpec((ATT_BLOCK, KV_DIM), lambda n: (prev(n), 5)),
        pl.BlockSpec((ATT_BLOCK, KV_DIM), lambda n: (cur(n), 5)),
    ]
    if tables:
        specs += [pl.BlockSpec((ATT_BLOCK, 3 * LANES), lambda n: (prev(n), 0)),
                  pl.BlockSpec((ATT_BLOCK, 3 * LANES), lambda n: (cur(n), 0))]
    return specs + [pl.BlockSpec(memory_space=pltpu.SMEM)]


def _kv_band(prev_ref, cur_ref):
    out = []
    for j in range(KV_DIM // LANES):
        sl = slice(j * LANES, (j + 1) * LANES)
        band = jnp.concatenate([prev_ref[:, sl], cur_ref[:, sl]], axis=0)
        out.append((band, pltpu.roll(band, HEAD_DIM, 1)))
    return out


def _bf16(bands, transposed=False):
    return [[(a.T if transposed else a).astype(BF16) for a in pair] for pair in bands]


def _attn_fwd(qkv, sinks, carry=(None, None)):
    T = qkv.shape[0]
    nb = T // ATT_BLOCK

    def body(*refs):
        n = pl.program_id(0)
        own, finish = _carried(carry, refs, 6, 1, n == 0, n == nb - 1)
        q_ref, kp_ref, kc_ref, vp_ref, vc_ref, sink_ref, o_ref = own
        valid, low, upper = _attn_masks(n)
        ks = _bf16(_kv_band(kp_ref, kc_ref))
        vts = _bf16(_kv_band(vp_ref, vc_ref), transposed=True)
        heads, outs = {}, {}

        def first(h):
            p, hf = h // 2, h % 2
            kpair, khalf = p // 4, (p // 2) % 2
            qm = jnp.where(low if hf == 0 else ~low, q_ref[:, p * LANES:(p + 1) * LANES] * ATT_SCALE, 0.0)
            sw = 0 if khalf == hf else 1
            heads[h] = (kpair, sw, _dot_nt(ks[kpair][sw], qm.astype(BF16)))

        def second(h):
            kpair, sw, s = heads[h]
            heads[h] = (kpair, sw, _softmax_sink(s, valid, sink_ref[0, h])[0].astype(BF16))

        def third(h):
            kpair, sw, pr = heads.pop(h)
            outs[h] = _dot(vts[kpair][sw], pr)
            if h % 2:
                o_ref[:, (h // 2) * LANES:(h // 2 + 1) * LANES] = jnp.where(upper, outs.pop(h - 1), outs.pop(h)).T.astype(BF16)

        for i in range(N_Q_HEADS + 2 * HEAD_LAG):
            if i < N_Q_HEADS:
                first(i)
            if 0 <= i - HEAD_LAG < N_Q_HEADS:
                second(i - HEAD_LAG)
            if 0 <= i - 2 * HEAD_LAG < N_Q_HEADS:
                third(i - 2 * HEAD_LAG)
        finish()

    in_specs, out_specs, out_shape, scratch, extra = _carried_specs(
        carry, _attn_specs(nb, False), [pl.BlockSpec((ATT_BLOCK, Q_DIM), lambda n: (n, 0))],
        [jax.ShapeDtypeStruct((T, Q_DIM), BF16)], [])
    return pl.pallas_call(
        body, name="attn_fwd", grid=(nb,), in_specs=in_specs, out_specs=out_specs, out_shape=out_shape,
        scratch_shapes=scratch, compiler_params=_params(dimension_semantics=("arbitrary",)),
    )(qkv, qkv, qkv, qkv, qkv, sinks, *extra)


def _attn_bwd(qkv, rot, sinks, dout, carry=(None, None)):
    T = qkv.shape[0]
    nb = T // ATT_BLOCK
    npair = KV_DIM // LANES

    def body(*refs):
        n = pl.program_id(0)
        own, finish = _carried(carry, refs, 9, 2, n == 0, n == nb)
        (q_ref, kp_ref, kc_ref, vp_ref, vc_ref, tp_ref, tc_ref, sink_ref, do_ref, dqkv_ref, dsink_ref,
         dq_c, dk_c, dv_c) = own

        @pl.when(n == 0)
        def _():
            dq_c[...] = jnp.zeros_like(dq_c)
            dk_c[...] = jnp.zeros_like(dk_c)
            dv_c[...] = jnp.zeros_like(dv_c)
            dsink_ref[...] = jnp.zeros_like(dsink_ref)

        def flush(dk_prev, dv_prev, tab_ref):
            dqkv_ref[:, :Q_DIM] = dq_c[...].astype(BF16)
            dk = _rot_bwd([dk_c[:, j * LANES:(j + 1) * LANES] + dk_prev[j] for j in range(npair)], tab_ref[...])
            for j in range(npair):
                dqkv_ref[:, Q_DIM + j * LANES:Q_DIM + (j + 1) * LANES] = dk[j].astype(BF16)
                dqkv_ref[:, Q_DIM + KV_DIM + j * LANES:Q_DIM + KV_DIM + (j + 1) * LANES] = (
                    dv_c[:, j * LANES:(j + 1) * LANES] + dv_prev[j]).astype(BF16)

        @pl.when(n < nb)
        def _():
            valid, low, upper = _attn_masks(n)
            lane = lax.broadcasted_iota(jnp.int32, (1, LANES), 1)
            k_band = _kv_band(kp_ref, kc_ref)
            ks, kts = _bf16(k_band), _bf16(k_band, transposed=True)
            vs = _bf16(_kv_band(vp_ref, vc_ref))
            dk_acc = [[jnp.zeros((2 * ATT_BLOCK, LANES), F32) for _ in range(2)] for _ in range(npair)]
            dv_acc = [[jnp.zeros((2 * ATT_BLOCK, LANES), F32) for _ in range(2)] for _ in range(npair)]
            dsink = jnp.zeros((1, LANES), F32)
            heads, dq_t, dsinks = {}, {}, []

            def first(h):
                p, hf = h // 2, h % 2
                kpair, khalf = p // 4, (p // 2) % 2
                sel = low if hf == 0 else ~low
                qm = jnp.where(sel, q_ref[:, p * LANES:(p + 1) * LANES] * ATT_SCALE, 0.0).astype(BF16)
                dom = jnp.where(sel, do_ref[:, p * LANES:(p + 1) * LANES], 0.0).astype(BF16)
                sw = 0 if khalf == hf else 1
                heads[h] = dict(kpair=kpair, sw=sw, qm=qm, dom=dom, s=_dot_nt(ks[kpair][sw], qm),
                                dp=_dot_nt(vs[kpair][sw], dom))

            def second(h):
                d = heads[h]
                pr, ps = _softmax_sink(d.pop("s"), valid, sink_ref[0, h])
                dp = d.pop("dp")
                dd = jnp.sum(pr * dp, axis=0, keepdims=True)
                dsinks.append(jnp.where(lane == h, -jnp.sum(ps * dd, axis=1, keepdims=True), 0.0))
                d["ds"] = (pr * (dp - dd)).astype(BF16)
                d["pr"] = pr.astype(BF16)

            def third(h):
                d = heads.pop(h)
                kpair, sw = d["kpair"], d["sw"]
                dq_t[h] = _dot(kts[kpair][sw], d["ds"])
                dk_acc[kpair][sw] = dk_acc[kpair][sw] + _dot(d["ds"], d["qm"])
                dv_acc[kpair][sw] = dv_acc[kpair][sw] + _dot(d["pr"], d["dom"])

            for i in range(N_Q_HEADS + 2 * HEAD_LAG):
                if i < N_Q_HEADS:
                    first(i)
                if 0 <= i - HEAD_LAG < N_Q_HEADS:
                    second(i - HEAD_LAG)
                if 0 <= i - 2 * HEAD_LAG < N_Q_HEADS:
                    third(i - 2 * HEAD_LAG)
            dsink = sum(dsinks, dsink)
            dqs = [jnp.where(upper, dq_t[2 * p], dq_t[2 * p + 1]).T * ATT_SCALE for p in range(Q_DIM // LANES)]
            dk_acc = [a[0] + pltpu.roll(a[1], HEAD_DIM, 1) for a in dk_acc]
            dv_acc = [a[0] + pltpu.roll(a[1], HEAD_DIM, 1) for a in dv_acc]
            flush([a[:ATT_BLOCK] for a in dk_acc], [a[:ATT_BLOCK] for a in dv_acc], tp_ref)
            dq = _rot_bwd(dqs, tc_ref[...])
            for p in range(Q_DIM // LANES):
                dq_c[:, p * LANES:(p + 1) * LANES] = dq[p]
            for j in range(npair):
                dk_c[:, j * LANES:(j + 1) * LANES] = dk_acc[j][ATT_BLOCK:]
                dv_c[:, j * LANES:(j + 1) * LANES] = dv_acc[j][ATT_BLOCK:]
            dsink_ref[...] += dsink

        @pl.when(n == nb)
        def _():
            zero = [jnp.zeros((ATT_BLOCK, LANES), F32) for _ in range(npair)]
            flush(zero, zero, tc_ref)

        finish()

    do_spec = pl.BlockSpec((ATT_BLOCK, Q_DIM), lambda n: (jnp.minimum(n, nb - 1), 0))
    in_specs, out_specs, out_shape, scratch, extra = _carried_specs(
        carry, _attn_specs(nb, True) + [do_spec],
        [pl.BlockSpec((ATT_BLOCK, QKV_DIM), lambda n: (jnp.maximum(n - 1, 0), 0)),
         pl.BlockSpec((1, LANES), lambda n: (0, 0))],
        [jax.ShapeDtypeStruct((T, QKV_DIM), BF16), jax.ShapeDtypeStruct((1, LANES), F32)],
        [pltpu.VMEM((ATT_BLOCK, Q_DIM), F32), pltpu.VMEM((ATT_BLOCK, KV_DIM), F32),
         pltpu.VMEM((ATT_BLOCK, KV_DIM), F32)])
    return pl.pallas_call(
        body, name="attn_bwd", grid=(nb + 1,), in_specs=in_specs, out_specs=out_specs, out_shape=out_shape,
        scratch_shapes=scratch, compiler_params=_params(dimension_semantics=("arbitrary",)),
    )(qkv, qkv, qkv, qkv, qkv, rot, rot, sinks, dout, *extra)


LEVELS = (32, 16, 8, 4, 2, 1)
SUBLANES = 8
UNROLL = 16
UNROLL_BWD = 8


def _lower_bound(lb_ref):
    l0, l1 = lb_ref[0:1, :], lb_ref[1:2, :]
    mx = jnp.maximum(l0, l1)
    e0, e1 = jnp.exp(l0 - mx), jnp.exp(l1 - mx)
    return e1 / (e0 + e1)


GROUPS = CHUNK // SUBLANES


def _group_roll(x, k):
    return pltpu.roll(x.reshape(GROUPS, SUBLANES, HGRN_DK), k % SUBLANES, 1).reshape(CHUNK, HGRN_DK)


def _scan_rows(x, row, reverse):
    r8 = row & (SUBLANES - 1)
    for sh in (1, 2, 4):
        ok = (r8 < SUBLANES - sh) if reverse else (r8 >= sh)
        x = x + jnp.where(ok, _group_roll(x, -sh if reverse else sh), 0.0)
    g = x.reshape(GROUPS, SUBLANES, HGRN_DK)
    edge = 0 if reverse else SUBLANES - 1
    tot = jnp.broadcast_to(g[:, edge:edge + 1, :], g.shape)

    def shifted(a, n):
        z = jnp.zeros((n, SUBLANES, HGRN_DK), F32)
        return jnp.concatenate([a[n:], z] if reverse else [z, a[:GROUPS - n]], axis=0)

    acc = shifted(tot, 1)
    for sh in (1, 2, 4):
        acc = acc + shifted(acc, sh)
    return (g + acc).reshape(CHUNK, HGRN_DK)


def _level_masks():
    t = lax.broadcasted_iota(jnp.int32, (CHUNK, CHUNK), 0)
    s = lax.broadcasted_iota(jnp.int32, (CHUNK, CHUNK), 1)
    return [((t & h) != 0) & ((s & h) == 0) & ((t ^ s) < 2 * h) for h in LEVELS]


def _level_scales(b, forget, row):
    out = []
    for h in LEVELS[:3]:
        parts = [jnp.broadcast_to(b[j * 2 * h + h - 1:j * 2 * h + h, :], (2 * h, HGRN_DK))
                 for j in range(CHUNK // (2 * h))]
        mid = parts[0] if len(parts) == 1 else jnp.concatenate(parts, axis=0)
        out.append(jnp.exp(-jnp.abs(b - mid)))
    groups = b.reshape(GROUPS, SUBLANES, HGRN_DK)
    mid = jnp.broadcast_to(groups[:, SUBLANES // 2 - 1:SUBLANES // 2, :], groups.shape)
    e4 = jnp.exp(-jnp.abs(groups - mid)).reshape(CHUNK, HGRN_DK)
    f, r4 = forget, row & 3
    up1, dn1 = _group_roll(f, -1), _group_roll(f, 1)
    e2 = jnp.where(r4 == 0, up1, jnp.where(r4 == 1, 1.0, jnp.where(r4 == 2, f, dn1 * f)))
    e1 = jnp.where((row & 1) == 1, f, 1.0)
    return out + [e4, e2, e1]


def _hgrn_gates(zq, zf, lb):
    sq = jax.nn.sigmoid(zq)
    q = zq * sq
    sg = jax.nn.sigmoid(zf)
    forget = lb + (1.0 - lb) * sg
    return q, sq, sg, forget, 1.0 - forget, jnp.log(forget)


def _hgrn_specs(T, rb, rev):
    nr = T // rb
    ri = (lambda r: nr - 1 - r) if rev else (lambda r: r)
    return nr, ri, [
        pl.BlockSpec((rb, HGRN_DK), lambda h, r: (ri(r), h)),
        pl.BlockSpec((rb, HGRN_DK), lambda h, r: (ri(r), HGRN_HEADS + h)),
        pl.BlockSpec((rb, HGRN_DK), lambda h, r: (ri(r), 2 * HGRN_HEADS + h)),
        pl.BlockSpec((2, HGRN_DK), lambda h, r: (0, h)),
    ]


def _hgrn_fwd(z, lb_raw, rb=2048, carry=(None, None)):
    T = z.shape[0]
    rb = min(rb, T)
    ncb = rb // CHUNK
    unroll = min(UNROLL, ncb)
    assert ncb % unroll == 0
    nr, ri, in_specs = _hgrn_specs(T, rb, False)

    def body(*refs):
        hh, rr = pl.program_id(0), pl.program_id(1)
        own, finish = _carried(carry, refs, 4, 2, (hh == 0) & (rr == 0), (hh == HGRN_HEADS - 1) & (rr == nr - 1))
        zq_ref, zf_ref, zi_ref, lb_ref, o_ref, st_ref, state = own

        @pl.when(rr == 0)
        def _():
            state[...] = jnp.zeros_like(state)

        lb = _lower_bound(lb_ref)
        row = lax.broadcasted_iota(jnp.int32, (CHUNK, HGRN_DK), 0)
        masks = _level_masks()

        def operands(c):
            rows = pl.ds(pl.multiple_of(c * CHUNK, CHUNK), CHUNK)
            q, _, _, forget, k, lf = _hgrn_gates(zq_ref[rows, :], zf_ref[rows, :], lb)
            v = zi_ref[rows, :]
            b = _scan_rows(lf, row, False)
            pairs = [((q * e).astype(BF16), (k * e).astype(BF16)) for e in _level_scales(b, forget, row)]
            b_last = b[CHUNK - 1:CHUNK, :]
            return dict(c=c, rows=rows, pairs=pairs, vb=v.astype(BF16), diag=jnp.sum(q * k, axis=-1, keepdims=True) * v,
                        kd=(k * jnp.exp(b_last - b)).astype(BF16), qd=(q * jnp.exp(b)).astype(BF16),
                        decay=jnp.exp(b_last))

        def group(i, st):
            parts = [operands(i * unroll + j) for j in range(unroll)]
            for p in parts:
                sc = jnp.zeros((CHUNK, CHUNK), F32)
                for (qs, ks), mask in zip(p["pairs"], masks):
                    sc = sc + jnp.where(mask, _dot_nt(qs, ks), 0.0)
                p["sc"] = sc.astype(BF16)
            for p in parts:
                p["o"] = _dot(p["sc"], p["vb"]) + p["diag"]
                p["gain"] = _dot_tn(p["vb"], p["kd"])
            for p in parts:
                st_ref[p["c"], 0] = st.astype(BF16)
                o_ref[p["rows"], :] = p["o"] + _dot_nt(p["qd"], st.astype(BF16))
                st = st * p["decay"] + p["gain"]
            return st

        state[...] = lax.fori_loop(0, ncb // unroll, group, state[...])
        finish()

    in_specs, out_specs, out_shape, scratch, extra = _carried_specs(
        carry, in_specs,
        [pl.BlockSpec((rb, HGRN_DK), lambda h, r: (r, h)),
         pl.BlockSpec((ncb, 1, HGRN_DK, HGRN_DK), lambda h, r: (r, h, 0, 0))],
        [jax.ShapeDtypeStruct((T, D_MODEL), F32),
         jax.ShapeDtypeStruct((T // CHUNK, HGRN_HEADS, HGRN_DK, HGRN_DK), BF16)],
        [pltpu.VMEM((HGRN_DK, HGRN_DK), F32)])
    return pl.pallas_call(
        body, name="hgrn_fwd", grid=(HGRN_HEADS, nr), in_specs=in_specs, out_specs=out_specs, out_shape=out_shape,
        scratch_shapes=scratch, compiler_params=_params(dimension_semantics=("arbitrary", "arbitrary")),
    )(z, z, z, lb_raw, *extra)


def _hgrn_bwd(z, lb_raw, states, do, rb=2048, carry=(None, None)):
    T = z.shape[0]
    rb = min(rb, T)
    ncb = rb // CHUNK
    unroll = min(UNROLL_BWD, ncb)
    assert ncb % unroll == 0
    nr, ri, in_specs = _hgrn_specs(T, rb, True)
    in_specs += [pl.BlockSpec((ncb, 1, HGRN_DK, HGRN_DK), lambda h, r: (ri(r), h, 0, 0)),
                 pl.BlockSpec((rb, HGRN_DK), lambda h, r: (ri(r), h))]

    def body(*refs):
        hh, rr = pl.program_id(0), pl.program_id(1)
        own, finish = _carried(carry, refs, 6, 4, (hh == 0) & (rr == 0), (hh == HGRN_HEADS - 1) & (rr == nr - 1))
        zq_ref, zf_ref, zi_ref, lb_ref, st_ref, do_ref, dq_ref, df_ref, di_ref, dlb_ref, dstate = own

        @pl.when(rr == 0)
        def _():
            dstate[...] = jnp.zeros_like(dstate)
            dlb_ref[...] = jnp.zeros_like(dlb_ref)

        lb = _lower_bound(lb_ref)
        row = lax.broadcasted_iota(jnp.int32, (CHUNK, HGRN_DK), 0)
        masks = _level_masks()

        def operands(c):
            rows = pl.ds(pl.multiple_of(c * CHUNK, CHUNK), CHUNK)
            zq = zq_ref[rows, :]
            q, sq, sg, forget, k, lf = _hgrn_gates(zq, zf_ref[rows, :], lb)
            v = zi_ref[rows, :]
            dov = do_ref[rows, :]
            b = _scan_rows(lf, row, False)
            b_last = b[CHUNK - 1:CHUNK, :]
            eb, ebb = jnp.exp(b), jnp.exp(b_last - b)
            es = _level_scales(b, forget, row)
            return dict(rows=rows, zq=zq, q=q, sq=sq, sg=sg, forget=forget, k=k, v=v, dov=dov, eb=eb, ebb=ebb,
                        e_last=jnp.exp(b_last), es=es, st=st_ref[c, 0], dob=dov.astype(BF16), vb=v.astype(BF16),
                        pairs=[((q * e).astype(BF16), (k * e).astype(BF16)) for e in es],
                        qd=(q * eb).astype(BF16), kd=(k * ebb).astype(BF16))

        def group(i, dlb):
            parts = [operands(ncb - 1 - (i * unroll + j)) for j in range(unroll)]
            for p in parts:
                p["da"] = _dot_nt(p["dob"], p["vb"])
                sc = jnp.zeros((CHUNK, CHUNK), F32)
                for (qs, ks), mask in zip(p["pairs"], masks):
                    sc = sc + jnp.where(mask, _dot_nt(qs, ks), 0.0)
                p["sc"] = sc.astype(BF16)
                p["dq_state"] = _dot(p["dob"], p["st"].astype(BF16))
                p["gain"] = _dot_tn(p["dob"], p["qd"])
            dst = dstate[...]
            for p in parts:
                p["dst"] = dst
                dst = dst * p["e_last"] + p["gain"]
            dstate[...] = dst
            for p in parts:
                dstb = p["dst"].astype(BF16)
                dk_state = p["ebb"] * _dot(p["vb"], dstb)
                dq = p["eb"] * p["dq_state"]
                dk = dk_state
                dv = _dot_nt(p["kd"], dstb) + _dot_tn(p["sc"], p["dob"])
                for e, (qs, ks), mask in zip(p["es"], p["pairs"], masks):
                    dam = jnp.where(mask, p["da"], 0.0).astype(BF16)
                    dq = dq + e * _dot(dam, ks)
                    dk = dk + e * _dot_tn(dam, qs)
                dad = jnp.sum(p["dov"] * p["v"], axis=-1, keepdims=True)
                p["dq"] = dq + dad * p["k"]
                p["dk"] = dk + dad * p["q"]
                p["dv"] = dv + jnp.sum(p["q"] * p["k"], axis=-1, keepdims=True) * p["dov"]
                p["extra"] = (p["e_last"] * jnp.sum(p["dst"] * p["st"].astype(F32), axis=0, keepdims=True)
                              + jnp.sum(p["k"] * dk_state, axis=0, keepdims=True))
            for p in parts:
                q, k, sq, sg, zq, rows = p["q"], p["k"], p["sq"], p["sg"], p["zq"], p["rows"]
                dlf = _scan_rows(q * p["dq"] - k * p["dk"], row, True) + p["extra"]
                dforget = dlf / p["forget"] - p["dk"]
                dq_ref[rows, :] = (p["dq"] * (sq * (1.0 + zq * (1.0 - sq)))).astype(BF16)
                df_ref[rows, :] = (dforget * (1.0 - lb) * sg * (1.0 - sg)).astype(BF16)
                di_ref[rows, :] = p["dv"].astype(BF16)
                dlb = dlb + jnp.sum(dforget * (1.0 - sg), axis=0, keepdims=True)
            return dlb

        dlb_ref[...] += lax.fori_loop(0, ncb // unroll, group, jnp.zeros((1, HGRN_DK), F32))
        finish()

    blk = pl.BlockSpec((rb, HGRN_DK), lambda h, r: (ri(r), h))
    in_specs, out_specs, out_shape, scratch, extra = _carried_specs(
        carry, in_specs, [blk, blk, blk, pl.BlockSpec((1, HGRN_DK), lambda h, r: (0, h))],
        [jax.ShapeDtypeStruct((T, D_MODEL), BF16)] * 3 + [jax.ShapeDtypeStruct((1, D_MODEL), F32)],
        [pltpu.VMEM((HGRN_DK, HGRN_DK), F32)])
    return pl.pallas_call(
        body, name="hgrn_bwd", grid=(HGRN_HEADS, nr), in_specs=in_specs, out_specs=out_specs, out_shape=out_shape,
        scratch_shapes=scratch, compiler_params=_params(dimension_semantics=("arbitrary", "arbitrary")),
    )(z, z, z, lb_raw, states, do, *extra)


MESH = pl.DeviceIdType.MESH
ANY = pl.BlockSpec(memory_space=pl.ANY)


def _place():
    return lax.axis_index("x"), lax.axis_index("y"), lax.axis_index("c")


def _sems(n):
    return [pltpu.SemaphoreType.DMA((7 * n,)), pltpu.SemaphoreType.DMA((7 * n,)), pltpu.SemaphoreType.DMA((n,))]


class _Gather:
    def __init__(self, x_ref, out_ref, send_sems, recv_sems, local_sems, idx):
        self.x_ref, self.out_ref, self.send_sems, self.recv_sems, self.local_sem, self.base = (
            x_ref, out_ref, send_sems, recv_sems, local_sems.at[idx], 7 * idx)
        x, y, c = _place()
        self.c = c
        self.me, self.sibling = (x, y, c), (x, y, 1 - c)
        self.chips = [(1 - x, y), (x, 1 - y), (1 - x, 1 - y)]

    def rows(self, px, py, pc):
        return self.out_ref.at[4 * px + 2 * py + pc]

    def copy(self, k, block, to, from_input=False):
        return pltpu.make_async_remote_copy(
            src_ref=self.x_ref if from_input else self.rows(*block), dst_ref=self.rows(*block),
            send_sem=self.send_sems.at[self.base + k], recv_sem=self.recv_sems.at[self.base + k], device_id=to,
            device_id_type=MESH)

    def first(self):
        out = [self.copy(0, self.me, self.sibling, from_input=True)]
        return out + [self.copy(1 + j, self.me, (*chip, self.c), from_input=True) for j, chip in enumerate(self.chips)]

    def start(self):
        pltpu.make_async_copy(self.x_ref, self.rows(*self.me), self.local_sem).start()
        for cp in self.first():
            cp.start()

    def finish(self):
        passed = [self.copy(4 + j, (*chip, self.c), self.sibling) for j, chip in enumerate(self.chips)]
        for j, chip in enumerate(self.chips):
            self.copy(1 + j, (*chip, self.c), self.me).wait_recv()
            passed[j].start()
        self.copy(0, self.sibling, self.me).wait_recv()
        for j, chip in enumerate(self.chips):
            self.copy(4 + j, (*chip, 1 - self.c), self.me).wait_recv()
        for cp in self.first() + passed:
            cp.wait_send()
        pltpu.make_async_copy(self.x_ref, self.rows(*self.me), self.local_sem).wait()


class _Many:
    def __init__(self, kind, in_refs, out_refs, send_sems, recv_sems, local_sems):
        self.ops = [kind(x, o, send_sems, recv_sems, local_sems, i) for i, (x, o) in enumerate(zip(in_refs, out_refs))]

    def start(self):
        for op in self.ops:
            op.start()

    def finish(self):
        for op in self.ops:
            op.finish()


def _result_shapes(kind, arrs):
    return [jax.ShapeDtypeStruct(a.shape if kind is _Exchange else (N_DEV,) + a.shape, a.dtype) for a in arrs]


def _all_gather(name, shards):
    n = len(shards)

    def body(*refs):
        g = _Many(_Gather, refs[:n], refs[n:2 * n], *refs[2 * n:])
        g.start()
        g.finish()

    return pl.pallas_call(
        body, name=name, out_shape=_result_shapes(_Gather, shards), in_specs=[ANY] * n, out_specs=[ANY] * n,
        scratch_shapes=_sems(n),
    )(*shards)


def _peers(x, y, c):
    out = []
    for k in range(1, N_DEV):
        px = 1 - x if k & 4 else x
        py = 1 - y if k & 2 else y
        pc = 1 - c if k & 1 else c
        out.append((k, (px, py, pc), 4 * px + 2 * py + pc))
    return out


class _Exchange:
    def __init__(self, g_ref, recv_ref, send_sems, recv_sems, local_sems, idx):
        x, y, c = _place()
        me = 4 * x + 2 * y + c
        self.local = pltpu.make_async_copy(g_ref.at[me], recv_ref.at[me], local_sems.at[idx])
        self.copies = [
            pltpu.make_async_remote_copy(
                src_ref=g_ref.at[pidx], dst_ref=recv_ref.at[me], send_sem=send_sems.at[7 * idx + k - 1],
                recv_sem=recv_sems.at[7 * idx + k - 1], device_id=peer, device_id_type=MESH)
            for k, peer, pidx in _peers(x, y, c)]

    def start(self):
        self.local.start()
        for cp in self.copies:
            cp.start()

    def finish(self):
        for cp in self.copies:
            cp.wait()
        self.local.wait()


def _carried(carry, refs, n_in, n_out, first, last):
    kind, arrs = carry
    if kind is None:
        return refs, lambda: None
    n = len(arrs)
    ins, rest = refs[:n_in], refs[n_in + n:]
    outs, scratch = rest[:n_out], rest[n_out + n:]
    op = _Many(kind, refs[n_in:n_in + n], rest[n_out:n_out + n], *scratch[len(scratch) - 3:])

    @pl.when(first)
    def _():
        op.start()

    def finish():
        @pl.when(last)
        def _():
            op.finish()

    return tuple(ins) + tuple(outs) + tuple(scratch[:len(scratch) - 3]), finish


def _carried_specs(carry, in_specs, out_specs, out_shape, scratch):
    kind, arrs = carry
    if kind is None:
        return in_specs, out_specs, out_shape, scratch, []
    n = len(arrs)
    return (list(in_specs) + [ANY] * n, list(out_specs) + [ANY] * n,
            list(out_shape) + _result_shapes(kind, arrs), list(scratch) + _sems(n), list(arrs))


def _adamw(w, g, m, v):
    m = ADAM_B1 * m + (1.0 - ADAM_B1) * g
    v = ADAM_B2 * v + (1.0 - ADAM_B2) * (g * g)
    m_hat = m / (1.0 - ADAM_B1 ** ADAM_STEP)
    v_hat = v / (1.0 - ADAM_B2 ** ADAM_STEP)
    delta = -ADAM_LR * (m_hat / (jnp.sqrt(v_hat) + ADAM_EPS) + ADAM_WD * w)
    return delta, m, v


def _adamw_sum(name, recvs, w, m, v):
    L, R, C = w.shape
    tm = 128 if R % 128 == 0 else 64
    assert R % tm == 0 and len(recvs) == L

    def body(*refs):
        r_refs, (w_ref, m_ref, v_ref, g_ref, d_ref, nm_ref, nv_ref) = refs[:L], refs[L:]
        for l in range(L):
            g = r_refs[l][0].astype(F32)
            for s in range(1, N_DEV):
                g = g + r_refs[l][s].astype(F32)
            g_ref[l] = g
            d_ref[l], nm_ref[l], nv_ref[l] = _adamw(w_ref[l], g, m_ref[l], v_ref[l])

    blk = pl.BlockSpec((L, tm, C), lambda i: (0, i, 0))
    return pl.pallas_call(
        body, name=name, grid=(R // tm,),
        in_specs=[pl.BlockSpec((N_DEV, tm, C), lambda i: (0, i, 0))] * L + [blk, blk, blk],
        out_specs=[blk] * 4, out_shape=[jax.ShapeDtypeStruct((L, R, C), F32)] * 4,
        compiler_params=_params(dimension_semantics=("arbitrary",)),
    )(*recvs, w, m, v)


def _small_sync(part, w, m, v):
    def body(p_ref, w_ref, m_ref, v_ref, g_ref, d_ref, nm_ref, nv_ref, gath, send_sems, recv_sems):
        x, y, c = _place()
        me = 4 * x + 2 * y + c
        gath[me] = p_ref[...]
        copies = []
        for k, peer, _ in _peers(x, y, c):
            cp = pltpu.make_async_remote_copy(
                src_ref=p_ref, dst_ref=gath.at[me], send_sem=send_sems.at[k - 1], recv_sem=recv_sems.at[k - 1],
                device_id=peer, device_id_type=MESH)
            cp.start()
            copies.append(cp)
        for cp in copies:
            cp.wait()
        g = gath[0]
        for s in range(1, N_DEV):
            g = g + gath[s]
        wv = w_ref[...]
        l0, l1 = w_ref[8:9, :], w_ref[9:10, :]
        mx = jnp.maximum(l0, l1)
        e0, e1 = jnp.exp(l0 - mx), jnp.exp(l1 - mx)
        g9 = g[9:10, :] * (e0 / (e0 + e1)) * (e1 / (e0 + e1))
        row = lax.broadcasted_iota(jnp.int32, g.shape, 0)
        g = jnp.where(row == 9, g9, jnp.where(row == 8, -g9, g))
        g_ref[...] = g
        d_ref[...], nm_ref[...], nv_ref[...] = _adamw(wv, g, m_ref[...], v_ref[...])

    vm = pl.BlockSpec(memory_space=pltpu.VMEM)
    return pl.pallas_call(
        body, name="small_params_sync", in_specs=[vm] * 4, out_specs=[vm] * 4,
        out_shape=[jax.ShapeDtypeStruct(part.shape, F32)] * 4,
        scratch_shapes=[pltpu.VMEM((N_DEV,) + part.shape, F32), pltpu.SemaphoreType.DMA((7,)),
                        pltpu.SemaphoreType.DMA((7,))],
    )(part, w, m, v)


def _shards_bf16(d, pieces):
    return [d[name][layer].astype(BF16) for name, layer in pieces]


def _gathered(arrs, pieces, out):
    for a, (name, layer) in zip(arrs, pieces):
        out[name, layer] = a if name in COL_SHARDED else a.reshape(N_DEV * a.shape[1], a.shape[2])


def _pad_row(a, width=D_MODEL):
    a = a.reshape(1, -1)
    return jnp.pad(a, ((0, 0), (0, width - a.shape[1])))


LOSS_ROW = 11


def _pack_small(d, gn_full, loss=None):
    rows = [d["mix_norm"], d["mlp_norm"], d["final_norm"].reshape(1, D_MODEL),
            _pad_row(d["attn_b_qkv"], 2 * D_MODEL).reshape(2, D_MODEL), _pad_row(d["attn_sinks"]),
            d["hgrn_lower_bounds"], gn_full.reshape(1, D_MODEL)]
    if loss is not None:
        rows.append(_pad_row(loss))
    p = jnp.concatenate(rows, axis=0)
    return jnp.pad(p, ((0, SMALL_ROWS - p.shape[0]), (0, 0)))


def _unpack_small(p, me):
    return dict(
        mix_norm=p[0:2], mlp_norm=p[2:4], final_norm=p[4],
        attn_b_qkv=p[5:7].reshape(1, 2 * D_MODEL)[:, :QKV_DIM], attn_sinks=p[7:8, :N_Q_HEADS],
        hgrn_lower_bounds=p[8:10], hgrn_g_norm=lax.dynamic_slice(p[10:11], (0, me * 128), (1, 128)))


WEIGHT_NAMES = ['mix_norm', 'mlp_norm', 'final_norm', 'attn_w_qkv', 'attn_b_qkv', 'attn_sinks', 'attn_w_o', 'hgrn_w_in',
                'hgrn_g_norm', 'hgrn_w_o', 'hgrn_lower_bounds', 'mlp_w_up', 'mlp_w_down']
SMALL_NAMES = ('mix_norm', 'mlp_norm', 'final_norm', 'attn_b_qkv', 'attn_sinks', 'hgrn_lower_bounds', 'hgrn_g_norm')


def _rotary_tables(positions):
    inv_freq = ROPE_THETA ** (-jnp.arange(0, 2 * ROT_HALF, 2, dtype=F32) / (2 * ROT_HALF))
    ang = positions.astype(F32).reshape(-1, 1) * inv_freq
    cos, sin = jnp.cos(ang), jnp.sin(ang)
    r = jnp.arange(LANES) % HEAD_DIM
    idx = r % ROT_HALF
    c = jnp.where(r < 2 * ROT_HALF, cos[:, idx], 1.0)
    sa = jnp.where((r >= ROT_HALF) & (r < 2 * ROT_HALF), sin[:, idx], 0.0)
    sb = jnp.where(r < ROT_HALF, -sin[:, idx], 0.0)
    return jnp.concatenate([c, sa, sb], axis=1)


def kernel(x, positions, mix_norm, mlp_norm, final_norm, attn_w_qkv, attn_b_qkv, attn_sinks, attn_w_o, hgrn_w_in, hgrn_g_norm, hgrn_w_o, hgrn_lower_bounds, mlp_w_up, mlp_w_down, loss_target, m_mix_norm, m_mlp_norm, m_final_norm, m_attn_w_qkv, m_attn_b_qkv, m_attn_sinks, m_attn_w_o, m_hgrn_w_in, m_hgrn_g_norm, m_hgrn_w_o, m_hgrn_lower_bounds, m_mlp_w_up, m_mlp_w_down, v_mix_norm, v_mlp_norm, v_final_norm, v_attn_w_qkv, v_attn_b_qkv, v_attn_sinks, v_attn_w_o, v_hgrn_w_in, v_hgrn_g_norm, v_hgrn_w_o, v_hgrn_lower_bounds, v_mlp_w_up, v_mlp_w_down):
    w = dict(mix_norm=mix_norm, mlp_norm=mlp_norm, final_norm=final_norm, attn_w_qkv=attn_w_qkv, attn_b_qkv=attn_b_qkv,
             attn_sinks=attn_sinks, attn_w_o=attn_w_o, hgrn_w_in=hgrn_w_in, hgrn_g_norm=hgrn_g_norm, hgrn_w_o=hgrn_w_o,
             hgrn_lower_bounds=hgrn_lower_bounds, mlp_w_up=mlp_w_up, mlp_w_down=mlp_w_down)
    m = dict(mix_norm=m_mix_norm, mlp_norm=m_mlp_norm, final_norm=m_final_norm, attn_w_qkv=m_attn_w_qkv,
             attn_b_qkv=m_attn_b_qkv, attn_sinks=m_attn_sinks, attn_w_o=m_attn_w_o, hgrn_w_in=m_hgrn_w_in,
             hgrn_g_norm=m_hgrn_g_norm, hgrn_w_o=m_hgrn_w_o, hgrn_lower_bounds=m_hgrn_lower_bounds, mlp_w_up=m_mlp_w_up,
             mlp_w_down=m_mlp_w_down)
    v = dict(mix_norm=v_mix_norm, mlp_norm=v_mlp_norm, final_norm=v_final_norm, attn_w_qkv=v_attn_w_qkv,
             attn_b_qkv=v_attn_b_qkv, attn_sinks=v_attn_sinks, attn_w_o=v_attn_w_o, hgrn_w_in=v_hgrn_w_in,
             hgrn_g_norm=v_hgrn_g_norm, hgrn_w_o=v_hgrn_w_o, hgrn_lower_bounds=v_hgrn_lower_bounds, mlp_w_up=v_mlp_w_up,
             mlp_w_down=v_mlp_w_down)
    me = 4 * lax.axis_index("x") + 2 * lax.axis_index("y") + lax.axis_index("c")

    gn = hgrn_g_norm.reshape(1, 128)
    gn_a = gn.astype(BF16)
    gn_b = (gn - gn_a.astype(F32)).astype(BF16)
    gn_c = (gn - gn_a.astype(F32) - gn_b.astype(F32)).astype(BF16)
    gn_rows = jnp.pad(jnp.concatenate([gn_a, gn_b, gn_c], axis=1), ((0, 15), (0, D_MODEL - 3 * 128)))
    full = {}
    got = _all_gather("gather_attn_weights", _shards_bf16(w, GATHER_FIRST) + [gn_rows])
    _gathered(got[:1], GATHER_FIRST, full)
    w_qkv = full["attn_w_qkv", 0].transpose(1, 0, 2).reshape(D_MODEL, QKV_DIM)
    gn_terms = got[1][:, 0, :3 * 128].astype(F32).reshape(N_DEV, 3, 128)
    gn_full = ((gn_terms[:, 0] + gn_terms[:, 1]) + gn_terms[:, 2]).reshape(1, D_MODEL)

    x0 = x[0]
    tgt = loss_target[0]
    rot = _rotary_tables(positions)
    row = lambda a: a.reshape(1, -1)

    qkv, h0 = _norm_mm("qkv_proj", x0, row(mix_norm[0]), w_qkv, attn_b_qkv, rot=rot)
    att, *got = _attn_fwd(qkv, attn_sinks, carry=(_Gather, _shards_bf16(w, GATHER_ATTN)))
    _gathered(got, GATHER_ATTN, full)
    x1 = _mm_res("attn_out_proj", att, full["attn_w_o", 0], x0)
    u0, h1, *got = _norm_mm("mlp0_up", x1, row(mlp_norm[0]), full["mlp_w_up", 0],
                            carry=(_Gather, _shards_bf16(w, GATHER_MLP0)))
    _gathered(got, GATHER_MLP0, full)
    x2, a0 = _mlp_down("mlp0_down", u0, full["mlp_w_down", 0], x1)
    z, h2 = _norm_mm("hgrn_in_proj", x2, row(mix_norm[1]), full["hgrn_w_in", 0])
    o_raw, states, *got = _hgrn_fwd(z, hgrn_lower_bounds, carry=(_Gather, _shards_bf16(w, GATHER_HGRN)))
    _gathered(got, GATHER_HGRN, full)
    x3, o2 = _hgrn_out("hgrn_out_proj", o_raw, z, gn_full, full["hgrn_w_o", 0], x2)
    u1, h3 = _norm_mm("mlp1_up", x3, row(mlp_norm[1]), full["mlp_w_up", 1])
    dx4, a1, loss_part, g_final = _mlp_down("mlp1_down_loss", u1, full["mlp_w_down", 1], x3,
                                            loss_head=(tgt, row(final_norm)))

    gw = {}
    du1, = _mlp_bwd_act("mlp1_bwd_act", dx4, u1, full["mlp_w_down", 1])
    dx3, g_mlp1 = _mm_nt_rmsbwd("mlp1_bwd_in", du1, full["mlp_w_up", 1], x3, row(mlp_norm[1]), dx4)
    gw["mlp_w_down", 1] = _mm_tn("mlp1_dw_down", a1, dx4, "rows")
    gw["mlp_w_up", 1] = _mm_tn("mlp1_dw_up", h3, du1, "cols")

    do_raw, dg, g_gn = _hgrn_out_bwd("hgrn_out_bwd", dx3, o_raw, z, full["hgrn_w_o", 0], gn_full)
    gw["hgrn_w_o", 0] = _mm_tn("hgrn_dw_o", o2, dx3, "rows")
    recvs = {}
    dzq, dzf, dzi, g_lb, *recv = _hgrn_bwd(z, hgrn_lower_bounds, states, do_raw,
                                           carry=(_Exchange, [gw[p] for p in GRADS_HGRN]))
    recvs.update(zip(GRADS_HGRN, recv))
    dz = [dzq, dzf, dzi, dg]
    dx2, g_mix1 = _mm_nt_rmsbwd("hgrn_in_bwd", dz, full["hgrn_w_in", 0], x2, row(mix_norm[1]), dx3)
    gw["hgrn_w_in", 0] = jnp.concatenate(
        [_mm_tn(f"hgrn_dw_in{j}", h2, d, "cols") for j, d in enumerate(dz)], axis=0)

    du0, = _mlp_bwd_act("mlp0_bwd_act", dx2, u0, full["mlp_w_down", 0])
    dx1, g_mlp0 = _mm_nt_rmsbwd("mlp0_bwd_in", du0, full["mlp_w_up", 0], x1, row(mlp_norm[0]), dx2)
    gw["mlp_w_down", 0], recvs["hgrn_w_in", 0] = _mm_tn(
        "mlp0_dw_down", a0, dx2, "rows", carry=(_Exchange, [gw["hgrn_w_in", 0]]))
    gw["mlp_w_up", 0], recvs["mlp_w_down", 0] = _mm_tn(
        "mlp0_dw_up", h1, du0, "cols", carry=(_Exchange, [gw["mlp_w_down", 0]]))

    datt = _mm_nt("attn_out_bwd", dx1, full["attn_w_o", 0], BF16)
    gw["attn_w_o", 0] = _mm_tn("attn_dw_o", att, dx1, "rows")
    dqkv, g_sink, *recv = _attn_bwd(qkv, rot, attn_sinks, datt, carry=(_Exchange, [gw[p] for p in GRADS_ATTN]))
    recvs.update(zip(GRADS_ATTN, recv))
    g_qkv = _mm_tn("attn_dw_qkv", h0, dqkv, bn=512)
    g_qkv = g_qkv.reshape(D_MODEL, N_DEV, QKV_DIM // N_DEV).transpose(1, 0, 2).astype(BF16)
    dx0, g_mix0, g_bqkv, recvs["attn_w_qkv", 0] = _mm_nt_rmsbwd(
        "qkv_bwd", dqkv, w_qkv, x0, row(mix_norm[0]), dx1, with_colsum=True, carry=(_Exchange, [g_qkv]))

    big = {name: _adamw_sum("adamw_" + name, [recvs[name, l] for l in range(w[name].shape[0])], w[name], m[name], v[name])
           for name in BIG_NAMES}

    zero_row = jnp.zeros((1, D_MODEL), F32)
    part = _pack_small(dict(
        mix_norm=jnp.concatenate([g_mix0, g_mix1], axis=0), mlp_norm=jnp.concatenate([g_mlp0, g_mlp1], axis=0),
        final_norm=g_final, attn_b_qkv=g_bqkv, attn_sinks=g_sink[:, :N_Q_HEADS],
        hgrn_lower_bounds=jnp.concatenate([zero_row, g_lb], axis=0)), g_gn, loss=loss_part)

    def spread(a):
        return lax.dynamic_update_slice(zero_row, a.reshape(1, 128), (0, me * 128))

    small_in = [_pack_small({n: d[n] for n in SMALL_NAMES if n != "hgrn_g_norm"}, spread(d["hgrn_g_norm"]))
                for d in (w, m, v)]
    synced = _small_sync(part, *small_in)
    small = [_unpack_small(p, me) for p in synced]

    outs = [synced[0][LOSS_ROW, 0], dx0.reshape(x.shape)]
    for kind, grp_small in enumerate(small):
        for name in WEIGHT_NAMES:
            val = grp_small[name] if name in SMALL_NAMES else big[name][kind]
            outs.append(val.reshape(w[name].shape))
    return tuple(outs)
```

```python
import functools

import jax
import jax.numpy as jnp
from jax import lax
from jax.experimental import pallas as pl
from jax.experimental.pallas import tpu as pltpu

F32 = jnp.float32
BF16 = jnp.bfloat16

D_MODEL = 1024
HEAD_DIM = 64
N_Q_HEADS = 16
Q_DIM = 1024
KV_DIM = 256
QKV_DIM = 1536
ATT_BLOCK = 128
ROT_HALF = 8
ROPE_THETA = 500000.0
NEG_INF = -1e30
HGRN_HEADS = 8
HGRN_DK = 128
CHUNK = 64
D_FF = 4096
NORM_EPS = 1e-5
N_DEV = 8

ADAM_LR = 0.001
ADAM_B1 = 0.9
ADAM_B2 = 0.999
ADAM_EPS = 1e-08
ADAM_WD = 0.01
ADAM_STEP = 10

LANES = 128
VMEM_LIMIT = 56 * 1024 * 1024

GATHER_FIRST = (("attn_w_qkv", 0),)
GATHER_ATTN = (("attn_w_o", 0), ("mlp_w_up", 0), ("mlp_w_down", 0))
GATHER_MLP0 = (("hgrn_w_in", 0), ("hgrn_w_o", 0))
GATHER_HGRN = (("mlp_w_up", 1), ("mlp_w_down", 1))
GRADS_HGRN = (("mlp_w_down", 1), ("mlp_w_up", 1), ("hgrn_w_o", 0))
GRADS_ATTN = (("mlp_w_up", 0), ("attn_w_o", 0))
COL_SHARDED = ("attn_w_qkv", "hgrn_w_in", "mlp_w_up")
BIG_NAMES = ("attn_w_qkv", "attn_w_o", "hgrn_w_in", "hgrn_w_o", "mlp_w_up", "mlp_w_down")
SMALL_ROWS = 16


def _dot(a, b):
    return jnp.dot(a, b, preferred_element_type=F32)


def _dot_nt(a, b):
    return lax.dot_general(a, b, (((1,), (1,)), ((), ())), preferred_element_type=F32)


def _dot_tn(a, b):
    return lax.dot_general(a, b, (((0,), (0,)), ((), ())), preferred_element_type=F32)


def _params(**kw):
    return pltpu.CompilerParams(vmem_limit_bytes=VMEM_LIMIT, **kw)


def _full_spec(a):
    nd = a.ndim
    return pl.BlockSpec(a.shape, lambda *_: (0,) * nd)


def _row_call(name, body, n_rows, tm, row_ins, full_ins, row_outs, acc_outs=(), carry=(None, None)):
    steps = n_rows // tm
    in_specs = [pl.BlockSpec((tm, w), functools.partial(lambda i, cb: (i, cb), cb=cb)) for _, w, cb in row_ins]
    in_specs += [_full_spec(a) for a in full_ins]
    out_shape = [jax.ShapeDtypeStruct((n_rows, w), dt) for w, dt in row_outs]
    out_specs = [pl.BlockSpec((tm, w), lambda i: (i, 0)) for w, _ in row_outs]
    for shp, dt in acc_outs:
        out_shape.append(jax.ShapeDtypeStruct(shp, dt))
        out_specs.append(pl.BlockSpec(shp, functools.partial(lambda i, nd: (0,) * nd, nd=len(shp))))
    n_in, n_out = len(in_specs), len(out_specs)
    in_specs, out_specs, out_shape, scratch, extra = _carried_specs(carry, in_specs, out_specs, out_shape, [])

    def wrapped(*refs):
        i = pl.program_id(0)
        own, finish = _carried(carry, refs, n_in, n_out, i == 0, i == steps - 1)
        body(*own)
        finish()

    return pl.pallas_call(
        wrapped, name=name, grid=(steps,), in_specs=in_specs, out_specs=out_specs, out_shape=out_shape,
        scratch_shapes=scratch, compiler_params=_params(dimension_semantics=("arbitrary",)),
    )(*[a for a, _, _ in row_ins], *full_ins, *extra)


def _rms(x, gain):
    r = lax.rsqrt(jnp.mean(x * x, axis=-1, keepdims=True) + NORM_EPS)
    xhat = x * r
    return xhat * gain, xhat, r


def _rms_bwd(dy, xhat, r, gain):
    dxhat = dy * gain
    dx = r * (dxhat - xhat * jnp.mean(dxhat * xhat, axis=-1, keepdims=True))
    return dx, dy * xhat


def _norm_mm(name, x, gain, w, bias=None, rot=None, tm=512, carry=(None, None)):
    T = x.shape[0]
    tm = min(tm, T)
    nc = 512
    blocked = w.ndim == 3
    n = N_DEV * w.shape[2] if blocked else w.shape[1]
    assert n % nc == 0 and (not blocked or w.shape[2] == nc)

    def body(*refs):
        x_ref, refs = refs[0], refs[1:]
        if rot is not None:
            t_ref, refs = refs[0], refs[1:]
        g_ref, w_ref, refs = refs[0], refs[1], refs[2:]
        if bias is not None:
            b_ref, refs = refs[0], refs[1:]
        y_ref, h_ref = refs
        h, _, _ = _rms(x_ref[...], g_ref[...])
        hb = h.astype(BF16)
        h_ref[...] = hb
        for c in range(n // nc):
            sl = slice(c * nc, (c + 1) * nc)
            y = _dot(hb, w_ref[c] if blocked else w_ref[:, sl])
            if bias is not None:
                y = y + b_ref[:, sl]
            if rot is None:
                y_ref[:, sl] = y
            else:
                n_rot = max(0, min(nc, Q_DIM + KV_DIM - c * nc)) // LANES
                pieces = _rot_fwd(y[:, :n_rot * LANES], t_ref[...]) if n_rot else []
                for j in range(nc // LANES):
                    col = slice(c * nc + j * LANES, c * nc + (j + 1) * LANES)
                    y_ref[:, col] = pieces[j] if j < n_rot else y[:, j * LANES:(j + 1) * LANES]

    rows = [(x, D_MODEL, 0)] + ([(rot, 3 * LANES, 0)] if rot is not None else [])
    full = [gain, w] + ([bias] if bias is not None else [])
    return _row_call(name, body, T, tm, rows, full, [(n, F32), (D_MODEL, BF16)], carry=carry)


def _mm_res(name, a, w, res, tm=512):
    T = a.shape[0]
    tm = min(tm, T)

    def body(a_ref, r_ref, w_ref, o_ref):
        o_ref[...] = r_ref[...] + _dot(a_ref[...], w_ref[...])

    return _row_call(name, body, T, tm, [(a, a.shape[1], 0), (res, D_MODEL, 0)], [w], [(D_MODEL, F32)])[0]


def _mlp_down(name, u, w, res, tm=512, loss_head=None):
    T = u.shape[0]
    tm = min(tm, T)
    kc = 1024
    sub = min(256, tm)

    def body(*refs):
        if loss_head is None:
            u_ref, r_ref, w_ref, o_ref, a_ref = refs
        else:
            u_ref, r_ref, t_ref, w_ref, g_ref, o_ref, a_ref, loss_ref, dg_ref = refs

            @pl.when(pl.program_id(0) == 0)
            def _():
                loss_ref[...] = jnp.zeros_like(loss_ref)
                dg_ref[...] = jnp.zeros_like(dg_ref)

        for r0 in range(0, tm, sub):
            rs = slice(r0, r0 + sub)
            acc = r_ref[rs, :]
            for c in range(D_FF // kc):
                sl = slice(c * kc, (c + 1) * kc)
                a = jnp.maximum(u_ref[rs, sl], 0.0)
                ab = (a * a).astype(BF16)
                a_ref[rs, sl] = ab
                acc = acc + _dot(ab, w_ref[sl, :])
            if loss_head is None:
                o_ref[rs, :] = acc
            else:
                gain_v = g_ref[...]
                y, xhat, r = _rms(acc, gain_v)
                diff = y - t_ref[rs, :]
                per_row = jnp.sum(diff * diff, axis=-1, keepdims=True) * (1.0 / D_MODEL)
                loss_ref[...] += jnp.broadcast_to(0.5 * jnp.sum(per_row, axis=0, keepdims=True), loss_ref.shape)
                dx, dgr = _rms_bwd(diff * (1.0 / D_MODEL), xhat, r, gain_v)
                o_ref[rs, :] = dx
                dg_ref[...] += jnp.sum(dgr, axis=0, keepdims=True)

    rows, full, acc_outs = [(u, D_FF, 0), (res, D_MODEL, 0)], [w], []
    if loss_head is not None:
        rows, full = rows + [(loss_head[0], D_MODEL, 0)], full + [loss_head[1]]
        acc_outs = [((1, LANES), F32), ((1, D_MODEL), F32)]
    return _row_call(name, body, T, tm, rows, full, [(D_MODEL, F32), (D_FF, BF16)], acc_outs)


def _hgrn_out(name, o_raw, z, gn, w, res, tm=512):
    T = o_raw.shape[0]
    tm = min(tm, T)

    def body(o_ref, g_ref, r_ref, gn_ref, w_ref, x_ref, a_ref):
        y, _, _ = _rms(o_ref[...], gn_ref[...])
        g = g_ref[...]
        a = (y * (g * jax.nn.sigmoid(g))).astype(BF16)
        a_ref[...] = a
        x_ref[...] = r_ref[...] + _dot(a, w_ref[...])

    return _row_call(name, body, T, tm, [(o_raw, D_MODEL, 0), (z, D_MODEL, 3), (res, D_MODEL, 0)], [gn, w],
                     [(D_MODEL, F32), (D_MODEL, BF16)])


def _mm_nt_rmsbwd(name, dy, w, x, gain, dres, tm=512, with_colsum=False, carry=(None, None)):
    T = x.shape[0]
    tm = min(tm, T)
    dys = list(dy) if isinstance(dy, (list, tuple)) else [dy]
    width = dys[0].shape[1]
    n = width * len(dys)
    sub = min(256, tm)
    assert not with_colsum or len(dys) == 1

    def body(*refs):
        dy_refs, refs = refs[:len(dys)], refs[len(dys):]
        if with_colsum:
            x_ref, dr_ref, w_ref, g_ref, dx_ref, dg_ref, cs_ref = refs
        else:
            x_ref, dr_ref, w_ref, g_ref, dx_ref, dg_ref = refs

        @pl.when(pl.program_id(0) == 0)
        def _():
            dg_ref[...] = jnp.zeros_like(dg_ref)
            if with_colsum:
                cs_ref[...] = jnp.zeros_like(cs_ref)

        gain_v = g_ref[...]
        for r0 in range(0, tm, sub):
            rs = slice(r0, r0 + sub)
            if w.ndim == 3:
                nb = w.shape[2]
                dh = None
                for p in range(N_DEV):
                    piece, off = divmod(p * nb, width)
                    part = _dot_nt(dy_refs[piece][rs, off:off + nb].astype(BF16), w_ref[p])
                    dh = part if dh is None else dh + part
            else:
                dh = _dot_nt(dy_refs[0][rs, :].astype(BF16), w_ref[...])
            _, xhat, r = _rms(x_ref[rs, :], gain_v)
            dx, dgr = _rms_bwd(dh, xhat, r, gain_v)
            dx_ref[rs, :] = dr_ref[rs, :] + dx
            dg_ref[...] += jnp.sum(dgr, axis=0, keepdims=True)
            if with_colsum:
                cs_ref[...] += jnp.sum(dy_refs[0][rs, :].astype(F32), axis=0, keepdims=True)

    acc = [((1, D_MODEL), F32)] + ([((1, n), F32)] if with_colsum else [])
    rows = [(d, width, 0) for d in dys] + [(x, D_MODEL, 0), (dres, D_MODEL, 0)]
    return _row_call(name, body, T, tm, rows, [w, gain], [(D_MODEL, F32)], acc, carry=carry)


def _mm_nt(name, dy, w, out_dtype, tm=512):
    T = dy.shape[0]
    tm = min(tm, T)
    k = w.shape[0]

    def body(dy_ref, w_ref, o_ref):
        o_ref[...] = _dot_nt(dy_ref[...].astype(BF16), w_ref[...]).astype(out_dtype)

    return _row_call(name, body, T, tm, [(dy, dy.shape[1], 0)], [w], [(k, out_dtype)])[0]


def _mlp_bwd_act(name, dy, u, w_down, tm=512, carry=(None, None)):
    T = u.shape[0]
    tm = min(tm, T)
    kc = 1024

    def body(dy_ref, u_ref, w_ref, du_ref):
        dyb = dy_ref[...].astype(BF16)
        for c in range(D_FF // kc):
            sl = slice(c * kc, (c + 1) * kc)
            da = _dot_nt(dyb, w_ref[sl, :])
            du_ref[:, sl] = (da * (2.0 * jnp.maximum(u_ref[:, sl], 0.0))).astype(BF16)

    return _row_call(name, body, T, tm, [(dy, D_MODEL, 0), (u, D_FF, 0)], [w_down], [(D_FF, BF16)], carry=carry)


def _hgrn_out_bwd(name, dx, o_raw, z, w, gn, tm=512):
    T = dx.shape[0]
    tm = min(tm, T)

    def body(dx_ref, o_ref, g_ref, w_ref, gn_ref, do_ref, dg_ref, dgn_ref):
        @pl.when(pl.program_id(0) == 0)
        def _():
            dgn_ref[...] = jnp.zeros_like(dgn_ref)

        da = _dot_nt(dx_ref[...].astype(BF16), w_ref[...])
        gn_v = gn_ref[...]
        y, xhat, r = _rms(o_ref[...], gn_v)
        g = g_ref[...]
        sg = jax.nn.sigmoid(g)
        dg_ref[...] = (da * y * (sg * (1.0 + g * (1.0 - sg)))).astype(BF16)
        dyn = da * (g * sg)
        do, dgr = _rms_bwd(dyn, xhat, r, gn_v)
        do_ref[...] = do
        dgn_ref[...] += jnp.sum(dgr, axis=0, keepdims=True)

    return _row_call(name, body, T, tm, [(dx, D_MODEL, 0), (o_raw, D_MODEL, 0), (z, D_MODEL, 3)], [w, gn],
                     [(D_MODEL, F32), (D_MODEL, BF16)], [((1, D_MODEL), F32)])


COL_BLOCK = D_FF // N_DEV


def _mm_tn(name, a, b, shard=None, bm=1024, bn=1024, tk=2048, carry=(None, None)):
    T, M = a.shape
    N = b.shape[1]
    bm, bn, tk = min(bm, M), min(bn, N), min(tk, T)
    nk = T // tk
    if shard is None:
        out_shape, out_block = jax.ShapeDtypeStruct((M, N), F32), (bm, bn)
        out_map = lambda i, j, k: (i, j)
    elif shard == "cols":
        assert bn % COL_BLOCK == 0 and N % bn == 0
        out_shape = jax.ShapeDtypeStruct((N // COL_BLOCK, M, COL_BLOCK), BF16)
        out_block = (bn // COL_BLOCK, bm, COL_BLOCK)
        out_map = lambda i, j, k: (j, i, 0)
    else:
        rows = M // N_DEV
        assert bm % rows == 0
        out_shape, out_block = jax.ShapeDtypeStruct((N_DEV, rows, N), BF16), (bm // rows, rows, bn)
        out_map = lambda i, j, k: (i, 0, j)

    grid = (M // bm, N // bn, nk)

    def body(*refs):
        i, j, k = pl.program_id(0), pl.program_id(1), pl.program_id(2)
        own, finish = _carried(carry, refs, 2, 1, (i == 0) & (j == 0) & (k == 0),
                               (i == grid[0] - 1) & (j == grid[1] - 1) & (k == nk - 1))
        a_ref, b_ref, o_ref, acc = own

        @pl.when(k == 0)
        def _():
            acc[...] = jnp.zeros_like(acc)

        acc[...] += _dot_tn(a_ref[...].astype(BF16), b_ref[...].astype(BF16))

        @pl.when(k == nk - 1)
        def _():
            if shard == "cols":
                for c in range(bn // COL_BLOCK):
                    o_ref[c] = acc[:, c * COL_BLOCK:(c + 1) * COL_BLOCK].astype(BF16)
            else:
                o_ref[...] = acc[...].reshape(out_block).astype(o_ref.dtype)

        finish()

    in_specs, out_specs, out_shapes, scratch, extra = _carried_specs(
        carry, [pl.BlockSpec((tk, bm), lambda i, j, k: (k, i)), pl.BlockSpec((tk, bn), lambda i, j, k: (k, j))],
        [pl.BlockSpec(out_block, out_map)], [out_shape], [pltpu.VMEM((bm, bn), F32)])
    res = pl.pallas_call(
        body, name=name, grid=grid, in_specs=in_specs, out_specs=out_specs, out_shape=out_shapes,
        scratch_shapes=scratch, compiler_params=_params(dimension_semantics=("arbitrary", "arbitrary", "arbitrary")),
    )(a, b, *extra)
    return res[0] if carry[0] is None else res


def _rot_fwd(x, tab):
    c, sa, sb = tab[:, :LANES], tab[:, LANES:2 * LANES], tab[:, 2 * LANES:]
    outs = []
    for j in range(x.shape[1] // LANES):
        xs = x[:, j * LANES:(j + 1) * LANES]
        outs.append(xs * c + pltpu.roll(xs, ROT_HALF, 1) * sa + pltpu.roll(xs, LANES - ROT_HALF, 1) * sb)
    return outs


def _rot_bwd(dys, tab):
    c, sa, sb = tab[:, :LANES], tab[:, LANES:2 * LANES], tab[:, 2 * LANES:]
    return [dy * c + pltpu.roll(dy * sa, LANES - ROT_HALF, 1) + pltpu.roll(dy * sb, ROT_HALF, 1) for dy in dys]


ATT_SCALE = HEAD_DIM ** -0.5
HEAD_LAG = 2


def _attn_masks(n):
    kj = lax.broadcasted_iota(jnp.int32, (2 * ATT_BLOCK, ATT_BLOCK), 0)
    qi = lax.broadcasted_iota(jnp.int32, (2 * ATT_BLOCK, ATT_BLOCK), 1)
    delta = qi + ATT_BLOCK - kj
    first_key = jnp.where(n > 0, 0, ATT_BLOCK)
    valid = (delta >= 0) & (delta < ATT_BLOCK) & (kj >= first_key)
    low = lax.broadcasted_iota(jnp.int32, (1, LANES), 1) < HEAD_DIM
    upper = lax.broadcasted_iota(jnp.int32, (LANES, 1), 0) < HEAD_DIM
    return valid, low, upper


def _softmax_sink(s, valid, sink):
    s = jnp.where(valid, s, NEG_INF)
    m = jnp.maximum(jnp.max(s, axis=0, keepdims=True), sink)
    e = jnp.exp(s - m)
    es = jnp.exp(sink - m)
    inv = 1.0 / (jnp.sum(e, axis=0, keepdims=True) + es)
    return e * inv, es * inv


def _attn_specs(nb, tables):
    prev = lambda n: jnp.maximum(jnp.minimum(n, nb - 1) - 1, 0)
    cur = lambda n: jnp.minimum(n, nb - 1)
    specs = [
        pl.BlockSpec((ATT_BLOCK, Q_DIM), lambda n: (cur(n), 0)),
        pl.BlockSpec((ATT_BLOCK, KV_DIM), lambda n: (prev(n), 4)),
        pl.BlockSpec((ATT_BLOCK, KV_DIM), lambda n: (cur(n), 4)),
        pl.BlockSpec((ATT_BLOCK, KV_DIM), lambda n: (prev(n), 5)),
        pl.BlockSpec((ATT_BLOCK, KV_DIM), lambda n: (cur(n), 5)),
    ]
    if tables:
        specs += [pl.BlockSpec((ATT_BLOCK, 3 * LANES), lambda n: (prev(n), 0)),
                  pl.BlockSpec((ATT_BLOCK, 3 * LANES), lambda n: (cur(n), 0))]
    return specs + [pl.BlockSpec(memory_space=pltpu.SMEM)]


def _kv_band(prev_ref, cur_ref):
    out = []
    for j in range(KV_DIM // LANES):
        sl = slice(j * LANES, (j + 1) * LANES)
        band = jnp.concatenate([prev_ref[:, sl], cur_ref[:, sl]], axis=0)
        out.append((band, pltpu.roll(band, HEAD_DIM, 1)))
    return out


def _bf16(bands, transposed=False):
    return [[(a.T if transposed else a).astype(BF16) for a in pair] for pair in bands]


def _attn_fwd(qkv, sinks, carry=(None, None)):
    T = qkv.shape[0]
    nb = T // ATT_BLOCK

    def body(*refs):
        n = pl.program_id(0)
        own, finish = _carried(carry, refs, 6, 1, n == 0, n == nb - 1)
        q_ref, kp_ref, kc_ref, vp_ref, vc_ref, sink_ref, o_ref = own
        valid, low, upper = _attn_masks(n)
        ks = _bf16(_kv_band(kp_ref, kc_ref))
        vts = _bf16(_kv_band(vp_ref, vc_ref), transposed=True)
        heads, outs = {}, {}

        def first(h):
            p, hf = h // 2, h % 2
            kpair, khalf = p // 4, (p // 2) % 2
            qm = jnp.where(low if hf == 0 else ~low, q_ref[:, p * LANES:(p + 1) * LANES] * ATT_SCALE, 0.0)
            sw = 0 if khalf == hf else 1
            heads[h] = (kpair, sw, _dot_nt(ks[kpair][sw], qm.astype(BF16)))

        def second(h):
            kpair, sw, s = heads[h]
            heads[h] = (kpair, sw, _softmax_sink(s, valid, sink_ref[0, h])[0].astype(BF16))

        def third(h):
            kpair, sw, pr = heads.pop(h)
            outs[h] = _dot(vts[kpair][sw], pr)
            if h % 2:
                o_ref[:, (h // 2) * LANES:(h // 2 + 1) * LANES] = jnp.where(upper, outs.pop(h - 1), outs.pop(h)).T.astype(BF16)

        for i in range(N_Q_HEADS + 2 * HEAD_LAG):
            if i < N_Q_HEADS:
                first(i)
            if 0 <= i - HEAD_LAG < N_Q_HEADS:
                second(i - HEAD_LAG)
            if 0 <= i - 2 * HEAD_LAG < N_Q_HEADS:
                third(i - 2 * HEAD_LAG)
        finish()

    in_specs, out_specs, out_shape, scratch, extra = _carried_specs(
        carry, _attn_specs(nb, False), [pl.BlockSpec((ATT_BLOCK, Q_DIM), lambda n: (n, 0))],
        [jax.ShapeDtypeStruct((T, Q_DIM), BF16)], [])
    return pl.pallas_call(
        body, name="attn_fwd", grid=(nb,), in_specs=in_specs, out_specs=out_specs, out_shape=out_shape,
        scratch_shapes=scratch, compiler_params=_params(dimension_semantics=("arbitrary",)),
    )(qkv, qkv, qkv, qkv, qkv, sinks, *extra)


def _attn_bwd(qkv, rot, sinks, dout, carry=(None, None)):
    T = qkv.shape[0]
    nb = T // ATT_BLOCK
    npair = KV_DIM // LANES

    def body(*refs):
        n = pl.program_id(0)
        own, finish = _carried(carry, refs, 9, 2, n == 0, n == nb)
        (q_ref, kp_ref, kc_ref, vp_ref, vc_ref, tp_ref, tc_ref, sink_ref, do_ref, dqkv_ref, dsink_ref,
         dq_c, dk_c, dv_c) = own

        @pl.when(n == 0)
        def _():
            dq_c[...] = jnp.zeros_like(dq_c)
            dk_c[...] = jnp.zeros_like(dk_c)
            dv_c[...] = jnp.zeros_like(dv_c)
            dsink_ref[...] = jnp.zeros_like(dsink_ref)

        def flush(dk_prev, dv_prev, tab_ref):
            dqkv_ref[:, :Q_DIM] = dq_c[...].astype(BF16)
            dk = _rot_bwd([dk_c[:, j * LANES:(j + 1) * LANES] + dk_prev[j] for j in range(npair)], tab_ref[...])
            for j in range(npair):
                dqkv_ref[:, Q_DIM + j * LANES:Q_DIM + (j + 1) * LANES] = dk[j].astype(BF16)
                dqkv_ref[:, Q_DIM + KV_DIM + j * LANES:Q_DIM + KV_DIM + (j + 1) * LANES] = (
                    dv_c[:, j * LANES:(j + 1) * LANES] + dv_prev[j]).astype(BF16)

        @pl.when(n < nb)
        def _():
            valid, low, upper = _attn_masks(n)
            lane = lax.broadcasted_iota(jnp.int32, (1, LANES), 1)
            k_band = _kv_band(kp_ref, kc_ref)
            ks, kts = _bf16(k_band), _bf16(k_band, transposed=True)
            vs = _bf16(_kv_band(vp_ref, vc_ref))
            dk_acc = [[jnp.zeros((2 * ATT_BLOCK, LANES), F32) for _ in range(2)] for _ in range(npair)]
            dv_acc = [[jnp.zeros((2 * ATT_BLOCK, LANES), F32) for _ in range(2)] for _ in range(npair)]
            dsink = jnp.zeros((1, LANES), F32)
            heads, dq_t, dsinks = {}, {}, []

            def first(h):
                p, hf = h // 2, h % 2
                kpair, khalf = p // 4, (p // 2) % 2
                sel = low if hf == 0 else ~low
                qm = jnp.where(sel, q_ref[:, p * LANES:(p + 1) * LANES] * ATT_SCALE, 0.0).astype(BF16)
                dom = jnp.where(sel, do_ref[:, p * LANES:(p + 1) * LANES], 0.0).astype(BF16)
                sw = 0 if khalf == hf else 1
                heads[h] = dict(kpair=kpair, sw=sw, qm=qm, dom=dom, s=_dot_nt(ks[kpair][sw], qm),
                                dp=_dot_nt(vs[kpair][sw], dom))

            def second(h):
                d = heads[h]
                pr, ps = _softmax_sink(d.pop("s"), valid, sink_ref[0, h])
                dp = d.pop("dp")
                dd = jnp.sum(pr * dp, axis=0, keepdims=True)
                dsinks.append(jnp.where(lane == h, -jnp.sum(ps * dd, axis=1, keepdims=True), 0.0))
                d["ds"] = (pr * (dp - dd)).astype(BF16)
                d["pr"] = pr.astype(BF16)

            def third(h):
                d = heads.pop(h)
                kpair, sw = d["kpair"], d["sw"]
                dq_t[h] = _dot(kts[kpair][sw], d["ds"])
                dk_acc[kpair][sw] = dk_acc[kpair][sw] + _dot(d["ds"], d["qm"])
                dv_acc[kpair][sw] = dv_acc[kpair][sw] + _dot(d["pr"], d["dom"])

            for i in range(N_Q_HEADS + 2 * HEAD_LAG):
                if i < N_Q_HEADS:
                    first(i)
                if 0 <= i - HEAD_LAG < N_Q_HEADS:
                    second(i - HEAD_LAG)
                if 0 <= i - 2 * HEAD_LAG < N_Q_HEADS:
                    third(i - 2 * HEAD_LAG)
            dsink = sum(dsinks, dsink)
            dqs = [jnp.where(upper, dq_t[2 * p], dq_t[2 * p + 1]).T * ATT_SCALE for p in range(Q_DIM // LANES)]
            dk_acc = [a[0] + pltpu.roll(a[1], HEAD_DIM, 1) for a in dk_acc]
            dv_acc = [a[0] + pltpu.roll(a[1], HEAD_DIM, 1) for a in dv_acc]
            flush([a[:ATT_BLOCK] for a in dk_acc], [a[:ATT_BLOCK] for a in dv_acc], tp_ref)
            dq = _rot_bwd(dqs, tc_ref[...])
            for p in range(Q_DIM // LANES):
                dq_c[:, p * LANES:(p + 1) * LANES] = dq[p]
            for j in range(npair):
                dk_c[:, j * LANES:(j + 1) * LANES] = dk_acc[j][ATT_BLOCK:]
                dv_c[:, j * LANES:(j + 1) * LANES] = dv_acc[j][ATT_BLOCK:]
            dsink_ref[...] += dsink

        @pl.when(n == nb)
        def _():
            zero = [jnp.zeros((ATT_BLOCK, LANES), F32) for _ in range(npair)]
            flush(zero, zero, tc_ref)

        finish()

    do_spec = pl.BlockSpec((ATT_BLOCK, Q_DIM), lambda n: (jnp.minimum(n, nb - 1), 0))
    in_specs, out_specs, out_shape, scratch, extra = _carried_specs(
        carry, _attn_specs(nb, True) + [do_spec],
        [pl.BlockSpec((ATT_BLOCK, QKV_DIM), lambda n: (jnp.maximum(n - 1, 0), 0)),
         pl.BlockSpec((1, LANES), lambda n: (0, 0))],
        [jax.ShapeDtypeStruct((T, QKV_DIM), BF16), jax.ShapeDtypeStruct((1, LANES), F32)],
        [pltpu.VMEM((ATT_BLOCK, Q_DIM), F32), pltpu.VMEM((ATT_BLOCK, KV_DIM), F32),
         pltpu.VMEM((ATT_BLOCK, KV_DIM), F32)])
    return pl.pallas_call(
        body, name="attn_bwd", grid=(nb + 1,), in_specs=in_specs, out_specs=out_specs, out_shape=out_shape,
        scratch_shapes=scratch, compiler_params=_params(dimension_semantics=("arbitrary",)),
    )(qkv, qkv, qkv, qkv, qkv, rot, rot, sinks, dout, *extra)


LEVELS = (32, 16, 8, 4, 2, 1)
SUBLANES = 8
UNROLL = 8
UNROLL_BWD = 8


def _lower_bound(lb_ref):
    l0, l1 = lb_ref[0:1, :], lb_ref[1:2, :]
    mx = jnp.maximum(l0, l1)
    e0, e1 = jnp.exp(l0 - mx), jnp.exp(l1 - mx)
    return e1 / (e0 + e1)


GROUPS = CHUNK // SUBLANES


def _group_roll(x, k):
    return pltpu.roll(x.reshape(GROUPS, SUBLANES, HGRN_DK), k % SUBLANES, 1).reshape(CHUNK, HGRN_DK)


def _scan_rows(x, row, reverse):
    r8 = row & (SUBLANES - 1)
    for sh in (1, 2, 4):
        ok = (r8 < SUBLANES - sh) if reverse else (r8 >= sh)
        x = x + jnp.where(ok, _group_roll(x, -sh if reverse else sh), 0.0)
    g = x.reshape(GROUPS, SUBLANES, HGRN_DK)
    edge = 0 if reverse else SUBLANES - 1
    tot = jnp.broadcast_to(g[:, edge:edge + 1, :], g.shape)

    def shifted(a, n):
        z = jnp.zeros((n, SUBLANES, HGRN_DK), F32)
        return jnp.concatenate([a[n:], z] if reverse else [z, a[:GROUPS - n]], axis=0)

    acc = shifted(tot, 1)
    for sh in (1, 2, 4):
        acc = acc + shifted(acc, sh)
    return (g + acc).reshape(CHUNK, HGRN_DK)


def _level_masks():
    t = lax.broadcasted_iota(jnp.int32, (CHUNK, CHUNK), 0)
    s = lax.broadcasted_iota(jnp.int32, (CHUNK, CHUNK), 1)
    return [((t & h) != 0) & ((s & h) == 0) & ((t ^ s) < 2 * h) for h in LEVELS]


def _level_scales(b, forget, row):
    out = []
    for h in LEVELS[:3]:
        parts = [jnp.broadcast_to(b[j * 2 * h + h - 1:j * 2 * h + h, :], (2 * h, HGRN_DK))
                 for j in range(CHUNK // (2 * h))]
        mid = parts[0] if len(parts) == 1 else jnp.concatenate(parts, axis=0)
        out.append(jnp.exp(-jnp.abs(b - mid)))
    groups = b.reshape(GROUPS, SUBLANES, HGRN_DK)
    mid = jnp.broadcast_to(groups[:, SUBLANES // 2 - 1:SUBLANES // 2, :], groups.shape)
    e4 = jnp.exp(-jnp.abs(groups - mid)).reshape(CHUNK, HGRN_DK)
    f, r4 = forget, row & 3
    up1, dn1 = _group_roll(f, -1), _group_roll(f, 1)
    e2 = jnp.where(r4 == 0, up1, jnp.where(r4 == 1, 1.0, jnp.where(r4 == 2, f, dn1 * f)))
    e1 = jnp.where((row & 1) == 1, f, 1.0)
    return out + [e4, e2, e1]


def _hgrn_gates(zq, zf, lb):
    sq = jax.nn.sigmoid(zq)
    q = zq * sq
    sg = jax.nn.sigmoid(zf)
    forget = lb + (1.0 - lb) * sg
    return q, sq, sg, forget, 1.0 - forget, jnp.log(forget)


def _hgrn_specs(T, rb, rev):
    nr = T // rb
    ri = (lambda r: nr - 1 - r) if rev else (lambda r: r)
    return nr, ri, [
        pl.BlockSpec((rb, HGRN_DK), lambda h, r: (ri(r), h)),
        pl.BlockSpec((rb, HGRN_DK), lambda h, r: (ri(r), HGRN_HEADS + h)),
        pl.BlockSpec((rb, HGRN_DK), lambda h, r: (ri(r), 2 * HGRN_HEADS + h)),
        pl.BlockSpec((2, HGRN_DK), lambda h, r: (0, h)),
    ]


def _hgrn_fwd(z, lb_raw, rb=2048, carry=(None, None)):
    T = z.shape[0]
    rb = min(rb, T)
    ncb = rb // CHUNK
    unroll = min(UNROLL, ncb)
    assert ncb % unroll == 0
    nr, ri, in_specs = _hgrn_specs(T, rb, False)

    def body(*refs):
        hh, rr = pl.program_id(0), pl.program_id(1)
        own, finish = _carried(carry, refs, 4, 2, (hh == 0) & (rr == 0), (hh == HGRN_HEADS - 1) & (rr == nr - 1))
        zq_ref, zf_ref, zi_ref, lb_ref, o_ref, st_ref, state = own

        @pl.when(rr == 0)
        def _():
            state[...] = jnp.zeros_like(state)

        lb = _lower_bound(lb_ref)
        row = lax.broadcasted_iota(jnp.int32, (CHUNK, HGRN_DK), 0)
        masks = _level_masks()

        def operands(c):
            rows = pl.ds(pl.multiple_of(c * CHUNK, CHUNK), CHUNK)
            q, _, _, forget, k, lf = _hgrn_gates(zq_ref[rows, :], zf_ref[rows, :], lb)
            v = zi_ref[rows, :]
            b = _scan_rows(lf, row, False)
            pairs = [((q * e).astype(BF16), (k * e).astype(BF16)) for e in _level_scales(b, forget, row)]
            b_last = b[CHUNK - 1:CHUNK, :]
            return dict(c=c, rows=rows, pairs=pairs, vb=v.astype(BF16), diag=jnp.sum(q * k, axis=-1, keepdims=True) * v,
                        kd=(k * jnp.exp(b_last - b)).astype(BF16), qd=(q * jnp.exp(b)).astype(BF16),
                        decay=jnp.exp(b_last))

        def group(i, st):
            parts = [operands(i * unroll + j) for j in range(unroll)]
            for p in parts:
                sc = jnp.zeros((CHUNK, CHUNK), F32)
                for (qs, ks), mask in zip(p["pairs"], masks):
                    sc = sc + jnp.where(mask, _dot_nt(qs, ks), 0.0)
                p["sc"] = sc.astype(BF16)
            for p in parts:
                p["o"] = _dot(p["sc"], p["vb"]) + p["diag"]
                p["gain"] = _dot_tn(p["vb"], p["kd"])
            for p in parts:
                st_ref[p["c"], 0] = st
                o_ref[p["rows"], :] = p["o"] + _dot_nt(p["qd"], st.astype(BF16))
                st = st * p["decay"] + p["gain"]
            return st

        state[...] = lax.fori_loop(0, ncb // unroll, group, state[...])
        finish()

    in_specs, out_specs, out_shape, scratch, extra = _carried_specs(
        carry, in_specs,
        [pl.BlockSpec((rb, HGRN_DK), lambda h, r: (r, h)),
         pl.BlockSpec((ncb, 1, HGRN_DK, HGRN_DK), lambda h, r: (r, h, 0, 0))],
        [jax.ShapeDtypeStruct((T, D_MODEL), F32),
         jax.ShapeDtypeStruct((T // CHUNK, HGRN_HEADS, HGRN_DK, HGRN_DK), F32)],
        [pltpu.VMEM((HGRN_DK, HGRN_DK), F32)])
    return pl.pallas_call(
        body, name="hgrn_fwd", grid=(HGRN_HEADS, nr), in_specs=in_specs, out_specs=out_specs, out_shape=out_shape,
        scratch_shapes=scratch, compiler_params=_params(dimension_semantics=("arbitrary", "arbitrary")),
    )(z, z, z, lb_raw, *extra)


def _hgrn_bwd(z, lb_raw, states, do, rb=2048, carry=(None, None)):
    T = z.shape[0]
    rb = min(rb, T)
    ncb = rb // CHUNK
    unroll = min(UNROLL_BWD, ncb)
    assert ncb % unroll == 0
    nr, ri, in_specs = _hgrn_specs(T, rb, True)
    in_specs += [pl.BlockSpec((ncb, 1, HGRN_DK, HGRN_DK), lambda h, r: (ri(r), h, 0, 0)),
                 pl.BlockSpec((rb, HGRN_DK), lambda h, r: (ri(r), h))]

    def body(*refs):
        hh, rr = pl.program_id(0), pl.program_id(1)
        own, finish = _carried(carry, refs, 6, 4, (hh == 0) & (rr == 0), (hh == HGRN_HEADS - 1) & (rr == nr - 1))
        zq_ref, zf_ref, zi_ref, lb_ref, st_ref, do_ref, dq_ref, df_ref, di_ref, dlb_ref, dstate = own

        @pl.when(rr == 0)
        def _():
            dstate[...] = jnp.zeros_like(dstate)
            dlb_ref[...] = jnp.zeros_like(dlb_ref)

        lb = _lower_bound(lb_ref)
        row = lax.broadcasted_iota(jnp.int32, (CHUNK, HGRN_DK), 0)
        masks = _level_masks()

        def operands(c):
            rows = pl.ds(pl.multiple_of(c * CHUNK, CHUNK), CHUNK)
            zq = zq_ref[rows, :]
            q, sq, sg, forget, k, lf = _hgrn_gates(zq, zf_ref[rows, :], lb)
            v = zi_ref[rows, :]
            dov = do_ref[rows, :]
            b = _scan_rows(lf, row, False)
            b_last = b[CHUNK - 1:CHUNK, :]
            eb, ebb = jnp.exp(b), jnp.exp(b_last - b)
            es = _level_scales(b, forget, row)
            return dict(rows=rows, zq=zq, q=q, sq=sq, sg=sg, forget=forget, k=k, v=v, dov=dov, eb=eb, ebb=ebb,
                        e_last=jnp.exp(b_last), es=es, st=st_ref[c, 0], dob=dov.astype(BF16), vb=v.astype(BF16),
                        pairs=[((q * e).astype(BF16), (k * e).astype(BF16)) for e in es],
                        qd=(q * eb).astype(BF16), kd=(k * ebb).astype(BF16))

        def group(i, dlb):
            parts = [operands(ncb - 1 - (i * unroll + j)) for j in range(unroll)]
            for p in parts:
                p["da"] = _dot_nt(p["dob"], p["vb"])
                sc = jnp.zeros((CHUNK, CHUNK), F32)
                for (qs, ks), mask in zip(p["pairs"], masks):
                    sc = sc + jnp.where(mask, _dot_nt(qs, ks), 0.0)
                p["sc"] = sc.astype(BF16)
                p["dq_state"] = _dot(p["dob"], p["st"].astype(BF16))
                p["gain"] = _dot_tn(p["dob"], p["qd"])
            dst = dstate[...]
            for p in parts:
                p["dst"] = dst
                dst = dst * p["e_last"] + p["gain"]
            dstate[...] = dst
            for p in parts:
                dstb = p["dst"].astype(BF16)
                dk_state = p["ebb"] * _dot(p["vb"], dstb)
                dq = p["eb"] * p["dq_state"]
                dk = dk_state
                dv = _dot_nt(p["kd"], dstb) + _dot_tn(p["sc"], p["dob"])
                for e, (qs, ks), mask in zip(p["es"], p["pairs"], masks):
                    dam = jnp.where(mask, p["da"], 0.0).astype(BF16)
                    dq = dq + e * _dot(dam, ks)
                    dk = dk + e * _dot_tn(dam, qs)
                dad = jnp.sum(p["dov"] * p["v"], axis=-1, keepdims=True)
                p["dq"] = dq + dad * p["k"]
                p["dk"] = dk + dad * p["q"]
                p["dv"] = dv + jnp.sum(p["q"] * p["k"], axis=-1, keepdims=True) * p["dov"]
                p["extra"] = (p["e_last"] * jnp.sum(p["dst"] * p["st"], axis=0, keepdims=True)
                              + jnp.sum(p["k"] * dk_state, axis=0, keepdims=True))
            for p in parts:
                q, k, sq, sg, zq, rows = p["q"], p["k"], p["sq"], p["sg"], p["zq"], p["rows"]
                dlf = _scan_rows(q * p["dq"] - k * p["dk"], row, True) + p["extra"]
                dforget = dlf / p["forget"] - p["dk"]
                dq_ref[rows, :] = (p["dq"] * (sq * (1.0 + zq * (1.0 - sq)))).astype(BF16)
                df_ref[rows, :] = (dforget * (1.0 - lb) * sg * (1.0 - sg)).astype(BF16)
                di_ref[rows, :] = p["dv"].astype(BF16)
                dlb = dlb + jnp.sum(dforget * (1.0 - sg), axis=0, keepdims=True)
            return dlb

        dlb_ref[...] += lax.fori_loop(0, ncb // unroll, group, jnp.zeros((1, HGRN_DK), F32))
        finish()

    blk = pl.BlockSpec((rb, HGRN_DK), lambda h, r: (ri(r), h))
    in_specs, out_specs, out_shape, scratch, extra = _carried_specs(
        carry, in_specs, [blk, blk, blk, pl.BlockSpec((1, HGRN_DK), lambda h, r: (0, h))],
        [jax.ShapeDtypeStruct((T, D_MODEL), BF16)] * 3 + [jax.ShapeDtypeStruct((1, D_MODEL), F32)],
        [pltpu.VMEM((HGRN_DK, HGRN_DK), F32)])
    return pl.pallas_call(
        body, name="hgrn_bwd", grid=(HGRN_HEADS, nr), in_specs=in_specs, out_specs=out_specs, out_shape=out_shape,
        scratch_shapes=scratch, compiler_params=_params(dimension_semantics=("arbitrary", "arbitrary")),
    )(z, z, z, lb_raw, states, do, *extra)


MESH = pl.DeviceIdType.MESH
ANY = pl.BlockSpec(memory_space=pl.ANY)


def _place():
    return lax.axis_index("x"), lax.axis_index("y"), lax.axis_index("c")


def _sems(n):
    return [pltpu.SemaphoreType.DMA((7 * n,)), pltpu.SemaphoreType.DMA((7 * n,)), pltpu.SemaphoreType.DMA((n,))]


class _Gather:
    def __init__(self, x_ref, out_ref, send_sems, recv_sems, local_sems, idx):
        self.x_ref, self.out_ref, self.send_sems, self.recv_sems, self.local_sem, self.base = (
            x_ref, out_ref, send_sems, recv_sems, local_sems.at[idx], 7 * idx)
        x, y, c = _place()
        self.c = c
        self.me, self.sibling = (x, y, c), (x, y, 1 - c)
        self.chips = [(1 - x, y), (x, 1 - y), (1 - x, 1 - y)]

    def rows(self, px, py, pc):
        return self.out_ref.at[4 * px + 2 * py + pc]

    def copy(self, k, block, to, from_input=False):
        return pltpu.make_async_remote_copy(
            src_ref=self.x_ref if from_input else self.rows(*block), dst_ref=self.rows(*block),
            send_sem=self.send_sems.at[self.base + k], recv_sem=self.recv_sems.at[self.base + k], device_id=to,
            device_id_type=MESH)

    def first(self):
        out = [self.copy(0, self.me, self.sibling, from_input=True)]
        return out + [self.copy(1 + j, self.me, (*chip, self.c), from_input=True) for j, chip in enumerate(self.chips)]

    def start(self):
        pltpu.make_async_copy(self.x_ref, self.rows(*self.me), self.local_sem).start()
        for cp in self.first():
            cp.start()

    def finish(self):
        passed = [self.copy(4 + j, (*chip, self.c), self.sibling) for j, chip in enumerate(self.chips)]
        for j, chip in enumerate(self.chips):
            self.copy(1 + j, (*chip, self.c), self.me).wait_recv()
            passed[j].start()
        self.copy(0, self.sibling, self.me).wait_recv()
        for j, chip in enumerate(self.chips):
            self.copy(4 + j, (*chip, 1 - self.c), self.me).wait_recv()
        for cp in self.first() + passed:
            cp.wait_send()
        pltpu.make_async_copy(self.x_ref, self.rows(*self.me), self.local_sem).wait()


class _Many:
    def __init__(self, kind, in_refs, out_refs, send_sems, recv_sems, local_sems):
        self.ops = [kind(x, o, send_sems, recv_sems, local_sems, i) for i, (x, o) in enumerate(zip(in_refs, out_refs))]

    def start(self):
        for op in self.ops:
            op.start()

    def finish(self):
        for op in self.ops:
            op.finish()


def _result_shapes(kind, arrs):
    return [jax.ShapeDtypeStruct(a.shape if kind is _Exchange else (N_DEV,) + a.shape, a.dtype) for a in arrs]


def _all_gather(name, shards):
    n = len(shards)

    def body(*refs):
        g = _Many(_Gather, refs[:n], refs[n:2 * n], *refs[2 * n:])
        g.start()
        g.finish()

    return pl.pallas_call(
        body, name=name, out_shape=_result_shapes(_Gather, shards), in_specs=[ANY] * n, out_specs=[ANY] * n,
        scratch_shapes=_sems(n),
    )(*shards)


def _peers(x, y, c):
    out = []
    for k in range(1, N_DEV):
        px = 1 - x if k & 4 else x
        py = 1 - y if k & 2 else y
        pc = 1 - c if k & 1 else c
        out.append((k, (px, py, pc), 4 * px + 2 * py + pc))
    return out


class _Exchange:
    def __init__(self, g_ref, recv_ref, send_sems, recv_sems, local_sems, idx):
        x, y, c = _place()
        me = 4 * x + 2 * y + c
        self.local = pltpu.make_async_copy(g_ref.at[me], recv_ref.at[me], local_sems.at[idx])
        self.copies = [
            pltpu.make_async_remote_copy(
                src_ref=g_ref.at[pidx], dst_ref=recv_ref.at[me], send_sem=send_sems.at[7 * idx + k - 1],
                recv_sem=recv_sems.at[7 * idx + k - 1], device_id=peer, device_id_type=MESH)
            for k, peer, pidx in _peers(x, y, c)]

    def start(self):
        self.local.start()
        for cp in self.copies:
            cp.start()

    def finish(self):
        for cp in self.copies:
            cp.wait()
        self.local.wait()


def _carried(carry, refs, n_in, n_out, first, last):
    kind, arrs = carry
    if kind is None:
        return refs, lambda: None
    n = len(arrs)
    ins, rest = refs[:n_in], refs[n_in + n:]
    outs, scratch = rest[:n_out], rest[n_out + n:]
    op = _Many(kind, refs[n_in:n_in + n], rest[n_out:n_out + n], *scratch[len(scratch) - 3:])

    @pl.when(first)
    def _():
        op.start()

    def finish():
        @pl.when(last)
        def _():
            op.finish()

    return tuple(ins) + tuple(outs) + tuple(scratch[:len(scratch) - 3]), finish


def _carried_specs(carry, in_specs, out_specs, out_shape, scratch):
    kind, arrs = carry
    if kind is None:
        return in_specs, out_specs, out_shape, scratch, []
    n = len(arrs)
    return (list(in_specs) + [ANY] * n, list(out_specs) + [ANY] * n,
            list(out_shape) + _result_shapes(kind, arrs), list(scratch) + _sems(n), list(arrs))


def _adamw(w, g, m, v):
    m = ADAM_B1 * m + (1.0 - ADAM_B1) * g
    v = ADAM_B2 * v + (1.0 - ADAM_B2) * (g * g)
    m_hat = m / (1.0 - ADAM_B1 ** ADAM_STEP)
    v_hat = v / (1.0 - ADAM_B2 ** ADAM_STEP)
    delta = -ADAM_LR * (m_hat / (jnp.sqrt(v_hat) + ADAM_EPS) + ADAM_WD * w)
    return delta, m, v


def _adamw_sum(name, recvs, w, m, v):
    L, R, C = w.shape
    tm = 128 if R % 128 == 0 else 64
    assert R % tm == 0 and len(recvs) == L

    def body(*refs):
        r_refs, (w_ref, m_ref, v_ref, g_ref, d_ref, nm_ref, nv_ref) = refs[:L], refs[L:]
        for l in range(L):
            g = r_refs[l][0].astype(F32)
            for s in range(1, N_DEV):
                g = g + r_refs[l][s].astype(F32)
            g_ref[l] = g
            d_ref[l], nm_ref[l], nv_ref[l] = _adamw(w_ref[l], g, m_ref[l], v_ref[l])

    blk = pl.BlockSpec((L, tm, C), lambda i: (0, i, 0))
    return pl.pallas_call(
        body, name=name, grid=(R // tm,),
        in_specs=[pl.BlockSpec((N_DEV, tm, C), lambda i: (0, i, 0))] * L + [blk, blk, blk],
        out_specs=[blk] * 4, out_shape=[jax.ShapeDtypeStruct((L, R, C), F32)] * 4,
        compiler_params=_params(dimension_semantics=("arbitrary",)),
    )(*recvs, w, m, v)


def _small_sync(part, w, m, v):
    def body(p_ref, w_ref, m_ref, v_ref, g_ref, d_ref, nm_ref, nv_ref, gath, send_sems, recv_sems):
        x, y, c = _place()
        me = 4 * x + 2 * y + c
        gath[me] = p_ref[...]
        copies = []
        for k, peer, _ in _peers(x, y, c):
            cp = pltpu.make_async_remote_copy(
                src_ref=p_ref, dst_ref=gath.at[me], send_sem=send_sems.at[k - 1], recv_sem=recv_sems.at[k - 1],
                device_id=peer, device_id_type=MESH)
            cp.start()
            copies.append(cp)
        for cp in copies:
            cp.wait()
        g = gath[0]
        for s in range(1, N_DEV):
            g = g + gath[s]
        wv = w_ref[...]
        l0, l1 = w_ref[8:9, :], w_ref[9:10, :]
        mx = jnp.maximum(l0, l1)
        e0, e1 = jnp.exp(l0 - mx), jnp.exp(l1 - mx)
        g9 = g[9:10, :] * (e0 / (e0 + e1)) * (e1 / (e0 + e1))
        row = lax.broadcasted_iota(jnp.int32, g.shape, 0)
        g = jnp.where(row == 9, g9, jnp.where(row == 8, -g9, g))
        g_ref[...] = g
        d_ref[...], nm_ref[...], nv_ref[...] = _adamw(wv, g, m_ref[...], v_ref[...])

    vm = pl.BlockSpec(memory_space=pltpu.VMEM)
    return pl.pallas_call(
        body, name="small_params_sync", in_specs=[vm] * 4, out_specs=[vm] * 4,
        out_shape=[jax.ShapeDtypeStruct(part.shape, F32)] * 4,
        scratch_shapes=[pltpu.VMEM((N_DEV,) + part.shape, F32), pltpu.SemaphoreType.DMA((7,)),
                        pltpu.SemaphoreType.DMA((7,))],
    )(part, w, m, v)


def _shards_bf16(d, pieces):
    return [d[name][layer].astype(BF16) for name, layer in pieces]


def _gathered(arrs, pieces, out):
    for a, (name, layer) in zip(arrs, pieces):
        out[name, layer] = a if name in COL_SHARDED else a.reshape(N_DEV * a.shape[1], a.shape[2])


def _pad_row(a, width=D_MODEL):
    a = a.reshape(1, -1)
    return jnp.pad(a, ((0, 0), (0, width - a.shape[1])))


LOSS_ROW = 11


def _pack_small(d, gn_full, loss=None):
    rows = [d["mix_norm"], d["mlp_norm"], d["final_norm"].reshape(1, D_MODEL),
            _pad_row(d["attn_b_qkv"], 2 * D_MODEL).reshape(2, D_MODEL), _pad_row(d["attn_sinks"]),
            d["hgrn_lower_bounds"], gn_full.reshape(1, D_MODEL)]
    if loss is not None:
        rows.append(_pad_row(loss))
    p = jnp.concatenate(rows, axis=0)
    return jnp.pad(p, ((0, SMALL_ROWS - p.shape[0]), (0, 0)))


def _unpack_small(p, me):
    return dict(
        mix_norm=p[0:2], mlp_norm=p[2:4], final_norm=p[4],
        attn_b_qkv=p[5:7].reshape(1, 2 * D_MODEL)[:, :QKV_DIM], attn_sinks=p[7:8, :N_Q_HEADS],
        hgrn_lower_bounds=p[8:10], hgrn_g_norm=lax.dynamic_slice(p[10:11], (0, me * 128), (1, 128)))


WEIGHT_NAMES = ['mix_norm', 'mlp_norm', 'final_norm', 'attn_w_qkv', 'attn_b_qkv', 'attn_sinks', 'attn_w_o', 'hgrn_w_in',
                'hgrn_g_norm', 'hgrn_w_o', 'hgrn_lower_bounds', 'mlp_w_up', 'mlp_w_down']
SMALL_NAMES = ('mix_norm', 'mlp_norm', 'final_norm', 'attn_b_qkv', 'attn_sinks', 'hgrn_lower_bounds', 'hgrn_g_norm')


def _rotary_tables(positions):
    inv_freq = ROPE_THETA ** (-jnp.arange(0, 2 * ROT_HALF, 2, dtype=F32) / (2 * ROT_HALF))
    ang = positions.astype(F32).reshape(-1, 1) * inv_freq
    cos, sin = jnp.cos(ang), jnp.sin(ang)
    r = jnp.arange(LANES) % HEAD_DIM
    idx = r % ROT_HALF
    c = jnp.where(r < 2 * ROT_HALF, cos[:, idx], 1.0)
    sa = jnp.where((r >= ROT_HALF) & (r < 2 * ROT_HALF), sin[:, idx], 0.0)
    sb = jnp.where(r < ROT_HALF, -sin[:, idx], 0.0)
    return jnp.concatenate([c, sa, sb], axis=1)


def kernel(x, positions, mix_norm, mlp_norm, final_norm, attn_w_qkv, attn_b_qkv, attn_sinks, attn_w_o, hgrn_w_in, hgrn_g_norm, hgrn_w_o, hgrn_lower_bounds, mlp_w_up, mlp_w_down, loss_target, m_mix_norm, m_mlp_norm, m_final_norm, m_attn_w_qkv, m_attn_b_qkv, m_attn_sinks, m_attn_w_o, m_hgrn_w_in, m_hgrn_g_norm, m_hgrn_w_o, m_hgrn_lower_bounds, m_mlp_w_up, m_mlp_w_down, v_mix_norm, v_mlp_norm, v_final_norm, v_attn_w_qkv, v_attn_b_qkv, v_attn_sinks, v_attn_w_o, v_hgrn_w_in, v_hgrn_g_norm, v_hgrn_w_o, v_hgrn_lower_bounds, v_mlp_w_up, v_mlp_w_down):
    w = dict(mix_norm=mix_norm, mlp_norm=mlp_norm, final_norm=final_norm, attn_w_qkv=attn_w_qkv, attn_b_qkv=attn_b_qkv,
             attn_sinks=attn_sinks, attn_w_o=attn_w_o, hgrn_w_in=hgrn_w_in, hgrn_g_norm=hgrn_g_norm, hgrn_w_o=hgrn_w_o,
             hgrn_lower_bounds=hgrn_lower_bounds, mlp_w_up=mlp_w_up, mlp_w_down=mlp_w_down)
    m = dict(mix_norm=m_mix_norm, mlp_norm=m_mlp_norm, final_norm=m_final_norm, attn_w_qkv=m_attn_w_qkv,
             attn_b_qkv=m_attn_b_qkv, attn_sinks=m_attn_sinks, attn_w_o=m_attn_w_o, hgrn_w_in=m_hgrn_w_in,
             hgrn_g_norm=m_hgrn_g_norm, hgrn_w_o=m_hgrn_w_o, hgrn_lower_bounds=m_hgrn_lower_bounds, mlp_w_up=m_mlp_w_up,
             mlp_w_down=m_mlp_w_down)
    v = dict(mix_norm=v_mix_norm, mlp_norm=v_mlp_norm, final_norm=v_final_norm, attn_w_qkv=v_attn_w_qkv,
             attn_b_qkv=v_attn_b_qkv, attn_sinks=v_attn_sinks, attn_w_o=v_attn_w_o, hgrn_w_in=v_hgrn_w_in,
             hgrn_g_norm=v_hgrn_g_norm, hgrn_w_o=v_hgrn_w_o, hgrn_lower_bounds=v_hgrn_lower_bounds, mlp_w_up=v_mlp_w_up,
             mlp_w_down=v_mlp_w_down)
    me = 4 * lax.axis_index("x") + 2 * lax.axis_index("y") + lax.axis_index("c")

    gn = hgrn_g_norm.reshape(1, 128)
    gn_a = gn.astype(BF16)
    gn_b = (gn - gn_a.astype(F32)).astype(BF16)
    gn_c = (gn - gn_a.astype(F32) - gn_b.astype(F32)).astype(BF16)
    gn_rows = jnp.pad(jnp.concatenate([gn_a, gn_b, gn_c], axis=1), ((0, 15), (0, D_MODEL - 3 * 128)))
    full = {}
    got = _all_gather("gather_attn_weights", _shards_bf16(w, GATHER_FIRST) + [gn_rows])
    _gathered(got[:1], GATHER_FIRST, full)
    w_qkv = full["attn_w_qkv", 0].transpose(1, 0, 2).reshape(D_MODEL, QKV_DIM)
    gn_terms = got[1][:, 0, :3 * 128].astype(F32).reshape(N_DEV, 3, 128)
    gn_full = ((gn_terms[:, 0] + gn_terms[:, 1]) + gn_terms[:, 2]).reshape(1, D_MODEL)

    x0 = x[0]
    tgt = loss_target[0]
    rot = _rotary_tables(positions)
    row = lambda a: a.reshape(1, -1)

    qkv, h0 = _norm_mm("qkv_proj", x0, row(mix_norm[0]), w_qkv, attn_b_qkv, rot=rot)
    att, *got = _attn_fwd(qkv, attn_sinks, carry=(_Gather, _shards_bf16(w, GATHER_ATTN)))
    _gathered(got, GATHER_ATTN, full)
    x1 = _mm_res("attn_out_proj", att, full["attn_w_o", 0], x0)
    u0, h1, *got = _norm_mm("mlp0_up", x1, row(mlp_norm[0]), full["mlp_w_up", 0],
                            carry=(_Gather, _shards_bf16(w, GATHER_MLP0)))
    _gathered(got, GATHER_MLP0, full)
    x2, a0 = _mlp_down("mlp0_down", u0, full["mlp_w_down", 0], x1)
    z, h2 = _norm_mm("hgrn_in_proj", x2, row(mix_norm[1]), full["hgrn_w_in", 0])
    o_raw, states, *got = _hgrn_fwd(z, hgrn_lower_bounds, carry=(_Gather, _shards_bf16(w, GATHER_HGRN)))
    _gathered(got, GATHER_HGRN, full)
    x3, o2 = _hgrn_out("hgrn_out_proj", o_raw, z, gn_full, full["hgrn_w_o", 0], x2)
    u1, h3 = _norm_mm("mlp1_up", x3, row(mlp_norm[1]), full["mlp_w_up", 1])
    dx4, a1, loss_part, g_final = _mlp_down("mlp1_down_loss", u1, full["mlp_w_down", 1], x3,
                                            loss_head=(tgt, row(final_norm)))

    gw = {}
    du1, = _mlp_bwd_act("mlp1_bwd_act", dx4, u1, full["mlp_w_down", 1])
    dx3, g_mlp1 = _mm_nt_rmsbwd("mlp1_bwd_in", du1, full["mlp_w_up", 1], x3, row(mlp_norm[1]), dx4)
    gw["mlp_w_down", 1] = _mm_tn("mlp1_dw_down", a1, dx4, "rows")
    gw["mlp_w_up", 1] = _mm_tn("mlp1_dw_up", h3, du1, "cols")

    do_raw, dg, g_gn = _hgrn_out_bwd("hgrn_out_bwd", dx3, o_raw, z, full["hgrn_w_o", 0], gn_full)
    gw["hgrn_w_o", 0] = _mm_tn("hgrn_dw_o", o2, dx3, "rows")
    recvs = {}
    dzq, dzf, dzi, g_lb, *recv = _hgrn_bwd(z, hgrn_lower_bounds, states, do_raw,
                                           carry=(_Exchange, [gw[p] for p in GRADS_HGRN]))
    recvs.update(zip(GRADS_HGRN, recv))
    dz = [dzq, dzf, dzi, dg]
    dx2, g_mix1 = _mm_nt_rmsbwd("hgrn_in_bwd", dz, full["hgrn_w_in", 0], x2, row(mix_norm[1]), dx3)
    gw["hgrn_w_in", 0] = jnp.concatenate(
        [_mm_tn(f"hgrn_dw_in{j}", h2, d, "cols") for j, d in enumerate(dz)], axis=0)

    du0, = _mlp_bwd_act("mlp0_bwd_act", dx2, u0, full["mlp_w_down", 0])
    dx1, g_mlp0 = _mm_nt_rmsbwd("mlp0_bwd_in", du0, full["mlp_w_up", 0], x1, row(mlp_norm[0]), dx2)
    gw["mlp_w_down", 0], recvs["hgrn_w_in", 0] = _mm_tn(
        "mlp0_dw_down", a0, dx2, "rows", carry=(_Exchange, [gw["hgrn_w_in", 0]]))
    gw["mlp_w_up", 0], recvs["mlp_w_down", 0] = _mm_tn(
        "mlp0_dw_up", h1, du0, "cols", carry=(_Exchange, [gw["mlp_w_down", 0]]))

    datt = _mm_nt("attn_out_bwd", dx1, full["attn_w_o", 0], BF16)
    gw["attn_w_o", 0] = _mm_tn("attn_dw_o", att, dx1, "rows")
    dqkv, g_sink, *recv = _attn_bwd(qkv, rot, attn_sinks, datt, carry=(_Exchange, [gw[p] for p in GRADS_ATTN]))
    recvs.update(zip(GRADS_ATTN, recv))
    g_qkv = _mm_tn("attn_dw_qkv", h0, dqkv, bn=512)
    g_qkv = g_qkv.reshape(D_MODEL, N_DEV, QKV_DIM // N_DEV).transpose(1, 0, 2).astype(BF16)
    dx0, g_mix0, g_bqkv, recvs["attn_w_qkv", 0] = _mm_nt_rmsbwd(
        "qkv_bwd", dqkv, w_qkv, x0, row(mix_norm[0]), dx1, with_colsum=True, carry=(_Exchange, [g_qkv]))

    big = {name: _adamw_sum("adamw_" + name, [recvs[name, l] for l in range(w[name].shape[0])], w[name], m[name], v[name])
           for name in BIG_NAMES}

    zero_row = jnp.zeros((1, D_MODEL), F32)
    part = _pack_small(dict(
        mix_norm=jnp.concatenate([g_mix0, g_mix1], axis=0), mlp_norm=jnp.concatenate([g_mlp0, g_mlp1], axis=0),
        final_norm=g_final, attn_b_qkv=g_bqkv, attn_sinks=g_sink[:, :N_Q_HEADS],
        hgrn_lower_bounds=jnp.concatenate([zero_row, g_lb], axis=0)), g_gn, loss=loss_part)

    def spread(a):
        return lax.dynamic_update_slice(zero_row, a.reshape(1, 128), (0, me * 128))

    small_in = [_pack_small({n: d[n] for n in SMALL_NAMES if n != "hgrn_g_norm"}, spread(d["hgrn_g_norm"]))
                for d in (w, m, v)]
    synced = _small_sync(part, *small_in)
    small = [_unpack_small(p, me) for p in synced]

    outs = [synced[0][LOSS_ROW, 0], dx0.reshape(x.shape)]
    for kind, grp_small in enumerate(small):
        for name in WEIGHT_NAMES:
            val = grp_small[name] if name in SMALL_NAMES else big[name][kind]
            outs.append(val.reshape(w[name].shape))
    return tuple(outs)
```

```python
import functools

import jax
import jax.numpy as jnp
from jax import lax
from jax.experimental import pallas as pl
from jax.experimental.pallas import tpu as pltpu

F32 = jnp.float32
BF16 = jnp.bfloat16

D_MODEL = 1024
HEAD_DIM = 64
N_Q_HEADS = 16
Q_DIM = 1024
KV_DIM = 256
QKV_DIM = 1536
ATT_BLOCK = 128
ROT_HALF = 8
ROPE_THETA = 500000.0
NEG_INF = -1e30
HGRN_HEADS = 8
HGRN_DK = 128
CHUNK = 64
D_FF = 4096
NORM_EPS = 1e-5
N_DEV = 8

ADAM_LR = 0.001
ADAM_B1 = 0.9
ADAM_B2 = 0.999
ADAM_EPS = 1e-08
ADAM_WD = 0.01
ADAM_STEP = 10

LANES = 128
VMEM_LIMIT = 56 * 1024 * 1024

GATHER_FIRST = (("attn_w_qkv", 0),)
GATHER_ATTN = (("attn_w_o", 0), ("mlp_w_up", 0), ("mlp_w_down", 0))
GATHER_MLP0 = (("hgrn_w_in", 0), ("hgrn_w_o", 0))
GATHER_HGRN = (("mlp_w_up", 1), ("mlp_w_down", 1))
GRADS_HGRN = (("mlp_w_down", 1), ("mlp_w_up", 1), ("hgrn_w_o", 0))
GRADS_ATTN = (("mlp_w_up", 0), ("attn_w_o", 0))
COL_SHARDED = ("attn_w_qkv", "hgrn_w_in", "mlp_w_up")
BIG_NAMES = ("attn_w_qkv", "attn_w_o", "hgrn_w_in", "hgrn_w_o", "mlp_w_up", "mlp_w_down")
SMALL_ROWS = 16


def _dot(a, b):
    return jnp.dot(a, b, preferred_element_type=F32)


def _dot_nt(a, b):
    return lax.dot_general(a, b, (((1,), (1,)), ((), ())), preferred_element_type=F32)


def _dot_tn(a, b):
    return lax.dot_general(a, b, (((0,), (0,)), ((), ())), preferred_element_type=F32)


def _params(**kw):
    return pltpu.CompilerParams(vmem_limit_bytes=VMEM_LIMIT, **kw)


def _full_spec(a):
    nd = a.ndim
    return pl.BlockSpec(a.shape, lambda *_: (0,) * nd)


def _row_call(name, body, n_rows, tm, row_ins, full_ins, row_outs, acc_outs=(), carry=(None, None)):
    steps = n_rows // tm
    in_specs = [pl.BlockSpec((tm, w), functools.partial(lambda i, cb: (i, cb), cb=cb)) for _, w, cb in row_ins]
    in_specs += [_full_spec(a) for a in full_ins]
    out_shape = [jax.ShapeDtypeStruct((n_rows, w), dt) for w, dt in row_outs]
    out_specs = [pl.BlockSpec((tm, w), lambda i: (i, 0)) for w, _ in row_outs]
    for shp, dt in acc_outs:
        out_shape.append(jax.ShapeDtypeStruct(shp, dt))
        out_specs.append(pl.BlockSpec(shp, functools.partial(lambda i, nd: (0,) * nd, nd=len(shp))))
    n_in, n_out = len(in_specs), len(out_specs)
    in_specs, out_specs, out_shape, scratch, extra = _carried_specs(carry, in_specs, out_specs, out_shape, [])

    def wrapped(*refs):
        i = pl.program_id(0)
        own, finish = _carried(carry, refs, n_in, n_out, i == 0, i == steps - 1)
        body(*own)
        finish()

    return pl.pallas_call(
        wrapped, name=name, grid=(steps,), in_specs=in_specs, out_specs=out_specs, out_shape=out_shape,
        scratch_shapes=scratch, compiler_params=_params(dimension_semantics=("arbitrary",)),
    )(*[a for a, _, _ in row_ins], *full_ins, *extra)


def _rms(x, gain):
    r = lax.rsqrt(jnp.mean(x * x, axis=-1, keepdims=True) + NORM_EPS)
    xhat = x * r
    return xhat * gain, xhat, r


def _rms_bwd(dy, xhat, r, gain):
    dxhat = dy * gain
    dx = r * (dxhat - xhat * jnp.mean(dxhat * xhat, axis=-1, keepdims=True))
    return dx, dy * xhat


def _norm_mm(name, x, gain, w, bias=None, rot=None, tm=512, carry=(None, None)):
    T = x.shape[0]
    tm = min(tm, T)
    nc = 512
    blocked = w.ndim == 3
    n = N_DEV * w.shape[2] if blocked else w.shape[1]
    assert n % nc == 0 and (not blocked or w.shape[2] == nc)

    def body(*refs):
        x_ref, refs = refs[0], refs[1:]
        if rot is not None:
            t_ref, refs = refs[0], refs[1:]
        g_ref, w_ref, refs = refs[0], refs[1], refs[2:]
        if bias is not None:
            b_ref, refs = refs[0], refs[1:]
        y_ref, h_ref = refs
        h, _, _ = _rms(x_ref[...], g_ref[...])
        hb = h.astype(BF16)
        h_ref[...] = hb
        for c in range(n // nc):
            sl = slice(c * nc, (c + 1) * nc)
            y = _dot(hb, w_ref[c] if blocked else w_ref[:, sl])
            if bias is not None:
                y = y + b_ref[:, sl]
            if rot is None:
                y_ref[:, sl] = y
            else:
                n_rot = max(0, min(nc, Q_DIM + KV_DIM - c * nc)) // LANES
                pieces = _rot_fwd(y[:, :n_rot * LANES], t_ref[...]) if n_rot else []
                for j in range(nc // LANES):
                    col = slice(c * nc + j * LANES, c * nc + (j + 1) * LANES)
                    y_ref[:, col] = pieces[j] if j < n_rot else y[:, j * LANES:(j + 1) * LANES]

    rows = [(x, D_MODEL, 0)] + ([(rot, 3 * LANES, 0)] if rot is not None else [])
    full = [gain, w] + ([bias] if bias is not None else [])
    return _row_call(name, body, T, tm, rows, full, [(n, F32), (D_MODEL, BF16)], carry=carry)


def _mm_res(name, a, w, res, tm=512):
    T = a.shape[0]
    tm = min(tm, T)

    def body(a_ref, r_ref, w_ref, o_ref):
        o_ref[...] = r_ref[...] + _dot(a_ref[...], w_ref[...])

    return _row_call(name, body, T, tm, [(a, a.shape[1], 0), (res, D_MODEL, 0)], [w], [(D_MODEL, F32)])[0]


def _mlp_down(name, u, w, res, tm=512, loss_head=None):
    T = u.shape[0]
    tm = min(tm, T)
    kc = 1024
    sub = min(256, tm)

    def body(*refs):
        if loss_head is None:
            u_ref, r_ref, w_ref, o_ref, a_ref = refs
        else:
            u_ref, r_ref, t_ref, w_ref, g_ref, o_ref, a_ref, loss_ref, dg_ref = refs

            @pl.when(pl.program_id(0) == 0)
            def _():
                loss_ref[...] = jnp.zeros_like(loss_ref)
                dg_ref[...] = jnp.zeros_like(dg_ref)

        for r0 in range(0, tm, sub):
            rs = slice(r0, r0 + sub)
            acc = r_ref[rs, :]
            for c in range(D_FF // kc):
                sl = slice(c * kc, (c + 1) * kc)
                a = jnp.maximum(u_ref[rs, sl], 0.0)
                ab = (a * a).astype(BF16)
                a_ref[rs, sl] = ab
                acc = acc + _dot(ab, w_ref[sl, :])
            if loss_head is None:
                o_ref[rs, :] = acc
            else:
                gain_v = g_ref[...]
                y, xhat, r = _rms(acc, gain_v)
                diff = y - t_ref[rs, :]
                per_row = jnp.sum(diff * diff, axis=-1, keepdims=True) * (1.0 / D_MODEL)
                loss_ref[...] += jnp.broadcast_to(0.5 * jnp.sum(per_row, axis=0, keepdims=True), loss_ref.shape)
                dx, dgr = _rms_bwd(diff * (1.0 / D_MODEL), xhat, r, gain_v)
                o_ref[rs, :] = dx
                dg_ref[...] += jnp.sum(dgr, axis=0, keepdims=True)

    rows, full, acc_outs = [(u, D_FF, 0), (res, D_MODEL, 0)], [w], []
    if loss_head is not None:
        rows, full = rows + [(loss_head[0], D_MODEL, 0)], full + [loss_head[1]]
        acc_outs = [((1, LANES), F32), ((1, D_MODEL), F32)]
    return _row_call(name, body, T, tm, rows, full, [(D_MODEL, F32), (D_FF, BF16)], acc_outs)


def _hgrn_out(name, o_raw, z, gn, w, res, tm=512):
    T = o_raw.shape[0]
    tm = min(tm, T)

    def body(o_ref, g_ref, r_ref, gn_ref, w_ref, x_ref, a_ref):
        y, _, _ = _rms(o_ref[...], gn_ref[...])
        g = g_ref[...]
        a = (y * (g * jax.nn.sigmoid(g))).astype(BF16)
        a_ref[...] = a
        x_ref[...] = r_ref[...] + _dot(a, w_ref[...])

    return _row_call(name, body, T, tm, [(o_raw, D_MODEL, 0), (z, D_MODEL, 3), (res, D_MODEL, 0)], [gn, w],
                     [(D_MODEL, F32), (D_MODEL, BF16)])


def _mm_nt_rmsbwd(name, dy, w, x, gain, dres, tm=512, with_colsum=False, carry=(None, None)):
    T = x.shape[0]
    tm = min(tm, T)
    dys = list(dy) if isinstance(dy, (list, tuple)) else [dy]
    width = dys[0].shape[1]
    n = width * len(dys)
    sub = min(256, tm)
    assert not with_colsum or len(dys) == 1

    def body(*refs):
        dy_refs, refs = refs[:len(dys)], refs[len(dys):]
        if with_colsum:
            x_ref, dr_ref, w_ref, g_ref, dx_ref, dg_ref, cs_ref = refs
        else:
            x_ref, dr_ref, w_ref, g_ref, dx_ref, dg_ref = refs

        @pl.when(pl.program_id(0) == 0)
        def _():
            dg_ref[...] = jnp.zeros_like(dg_ref)
            if with_colsum:
                cs_ref[...] = jnp.zeros_like(cs_ref)

        gain_v = g_ref[...]
        for r0 in range(0, tm, sub):
            rs = slice(r0, r0 + sub)
            if w.ndim == 3:
                nb = w.shape[2]
                dh = None
                for p in range(N_DEV):
                    piece, off = divmod(p * nb, width)
                    part = _dot_nt(dy_refs[piece][rs, off:off + nb].astype(BF16), w_ref[p])
                    dh = part if dh is None else dh + part
            else:
                dh = _dot_nt(dy_refs[0][rs, :].astype(BF16), w_ref[...])
            _, xhat, r = _rms(x_ref[rs, :], gain_v)
            dx, dgr = _rms_bwd(dh, xhat, r, gain_v)
            dx_ref[rs, :] = dr_ref[rs, :] + dx
            dg_ref[...] += jnp.sum(dgr, axis=0, keepdims=True)
            if with_colsum:
                cs_ref[...] += jnp.sum(dy_refs[0][rs, :].astype(F32), axis=0, keepdims=True)

    acc = [((1, D_MODEL), F32)] + ([((1, n), F32)] if with_colsum else [])
    rows = [(d, width, 0) for d in dys] + [(x, D_MODEL, 0), (dres, D_MODEL, 0)]
    return _row_call(name, body, T, tm, rows, [w, gain], [(D_MODEL, F32)], acc, carry=carry)


def _mm_nt(name, dy, w, out_dtype, tm=512):
    T = dy.shape[0]
    tm = min(tm, T)
    k = w.shape[0]

    def body(dy_ref, w_ref, o_ref):
        o_ref[...] = _dot_nt(dy_ref[...].astype(BF16), w_ref[...]).astype(out_dtype)

    return _row_call(name, body, T, tm, [(dy, dy.shape[1], 0)], [w], [(k, out_dtype)])[0]


def _mlp_bwd_act(name, dy, u, w_down, tm=512, carry=(None, None)):
    T = u.shape[0]
    tm = min(tm, T)
    kc = 1024

    def body(dy_ref, u_ref, w_ref, du_ref):
        dyb = dy_ref[...].astype(BF16)
        for c in range(D_FF // kc):
            sl = slice(c * kc, (c + 1) * kc)
            da = _dot_nt(dyb, w_ref[sl, :])
            du_ref[:, sl] = (da * (2.0 * jnp.maximum(u_ref[:, sl], 0.0))).astype(BF16)

    return _row_call(name, body, T, tm, [(dy, D_MODEL, 0), (u, D_FF, 0)], [w_down], [(D_FF, BF16)], carry=carry)


def _hgrn_out_bwd(name, dx, o_raw, z, w, gn, tm=512):
    T = dx.shape[0]
    tm = min(tm, T)

    def body(dx_ref, o_ref, g_ref, w_ref, gn_ref, do_ref, dg_ref, dgn_ref):
        @pl.when(pl.program_id(0) == 0)
        def _():
            dgn_ref[...] = jnp.zeros_like(dgn_ref)

        da = _dot_nt(dx_ref[...].astype(BF16), w_ref[...])
        gn_v = gn_ref[...]
        y, xhat, r = _rms(o_ref[...], gn_v)
        g = g_ref[...]
        sg = jax.nn.sigmoid(g)
        dg_ref[...] = (da * y * (sg * (1.0 + g * (1.0 - sg)))).astype(BF16)
        dyn = da * (g * sg)
        do, dgr = _rms_bwd(dyn, xhat, r, gn_v)
        do_ref[...] = do
        dgn_ref[...] += jnp.sum(dgr, axis=0, keepdims=True)

    return _row_call(name, body, T, tm, [(dx, D_MODEL, 0), (o_raw, D_MODEL, 0), (z, D_MODEL, 3)], [w, gn],
                     [(D_MODEL, F32), (D_MODEL, BF16)], [((1, D_MODEL), F32)])


COL_BLOCK = D_FF // N_DEV


def _mm_tn(name, a, b, shard=None, bm=1024, bn=1024, tk=2048, carry=(None, None)):
    T, M = a.shape
    N = b.shape[1]
    bm, bn, tk = min(bm, M), min(bn, N), min(tk, T)
    nk = T // tk
    if shard is None:
        out_shape, out_block = jax.ShapeDtypeStruct((M, N), F32), (bm, bn)
        out_map = lambda i, j, k: (i, j)
    elif shard == "cols":
        assert bn % COL_BLOCK == 0 and N % bn == 0
        out_shape = jax.ShapeDtypeStruct((N // COL_BLOCK, M, COL_BLOCK), BF16)
        out_block = (bn // COL_BLOCK, bm, COL_BLOCK)
        out_map = lambda i, j, k: (j, i, 0)
    else:
        rows = M // N_DEV
        assert bm % rows == 0
        out_shape, out_block = jax.ShapeDtypeStruct((N_DEV, rows, N), BF16), (bm // rows, rows, bn)
        out_map = lambda i, j, k: (i, 0, j)

    grid = (M // bm, N // bn, nk)

    def body(*refs):
        i, j, k = pl.program_id(0), pl.program_id(1), pl.program_id(2)
        own, finish = _carried(carry, refs, 2, 1, (i == 0) & (j == 0) & (k == 0),
                               (i == grid[0] - 1) & (j == grid[1] - 1) & (k == nk - 1))
        a_ref, b_ref, o_ref, acc = own

        @pl.when(k == 0)
        def _():
            acc[...] = jnp.zeros_like(acc)

        acc[...] += _dot_tn(a_ref[...].astype(BF16), b_ref[...].astype(BF16))

        @pl.when(k == nk - 1)
        def _():
            if shard == "cols":
                for c in range(bn // COL_BLOCK):
                    o_ref[c] = acc[:, c * COL_BLOCK:(c + 1) * COL_BLOCK].astype(BF16)
            else:
                o_ref[...] = acc[...].reshape(out_block).astype(o_ref.dtype)

        finish()

    in_specs, out_specs, out_shapes, scratch, extra = _carried_specs(
        carry, [pl.BlockSpec((tk, bm), lambda i, j, k: (k, i)), pl.BlockSpec((tk, bn), lambda i, j, k: (k, j))],
        [pl.BlockSpec(out_block, out_map)], [out_shape], [pltpu.VMEM((bm, bn), F32)])
    res = pl.pallas_call(
        body, name=name, grid=grid, in_specs=in_specs, out_specs=out_specs, out_shape=out_shapes,
        scratch_shapes=scratch, compiler_params=_params(dimension_semantics=("arbitrary", "arbitrary", "arbitrary")),
    )(a, b, *extra)
    return res[0] if carry[0] is None else res


def _rot_fwd(x, tab):
    c, sa, sb = tab[:, :LANES], tab[:, LANES:2 * LANES], tab[:, 2 * LANES:]
    outs = []
    for j in range(x.shape[1] // LANES):
        xs = x[:, j * LANES:(j + 1) * LANES]
        outs.append(xs * c + pltpu.roll(xs, ROT_HALF, 1) * sa + pltpu.roll(xs, LANES - ROT_HALF, 1) * sb)
    return outs


def _rot_bwd(dys, tab):
    c, sa, sb = tab[:, :LANES], tab[:, LANES:2 * LANES], tab[:, 2 * LANES:]
    return [dy * c + pltpu.roll(dy * sa, LANES - ROT_HALF, 1) + pltpu.roll(dy * sb, ROT_HALF, 1) for dy in dys]


ATT_SCALE = HEAD_DIM ** -0.5
HEAD_LAG = 2


def _attn_masks(n):
    kj = lax.broadcasted_iota(jnp.int32, (2 * ATT_BLOCK, ATT_BLOCK), 0)
    qi = lax.broadcasted_iota(jnp.int32, (2 * ATT_BLOCK, ATT_BLOCK), 1)
    delta = qi + ATT_BLOCK - kj
    first_key = jnp.where(n > 0, 0, ATT_BLOCK)
    valid = (delta >= 0) & (delta < ATT_BLOCK) & (kj >= first_key)
    low = lax.broadcasted_iota(jnp.int32, (1, LANES), 1) < HEAD_DIM
    upper = lax.broadcasted_iota(jnp.int32, (LANES, 1), 0) < HEAD_DIM
    return valid, low, upper


def _softmax_sink(s, valid, sink):
    s = jnp.where(valid, s, NEG_INF)
    m = jnp.maximum(jnp.max(s, axis=0, keepdims=True), sink)
    e = jnp.exp(s - m)
    es = jnp.exp(sink - m)
    inv = 1.0 / (jnp.sum(e, axis=0, keepdims=True) + es)
    return e * inv, es * inv


def _attn_specs(nb, tables):
    prev = lambda n: jnp.maximum(jnp.minimum(n, nb - 1) - 1, 0)
    cur = lambda n: jnp.minimum(n, nb - 1)
    specs = [
        pl.BlockSpec((ATT_BLOCK, Q_DIM), lambda n: (cur(n), 0)),
        pl.BlockSpec((ATT_BLOCK, KV_DIM), lambda n: (prev(n), 4)),
        pl.BlockSpec((ATT_BLOCK, KV_DIM), lambda n: (cur(n), 4)),
        pl.BlockSpec((ATT_BLOCK, KV_DIM), lambda n: (prev(n), 5)),
        pl.BlockSpec((ATT_BLOCK, KV_DIM), lambda n: (cur(n), 5)),
    ]
    if tables:
        specs += [pl.BlockSpec((ATT_BLOCK, 3 * LANES), lambda n: (prev(n), 0)),
                  pl.BlockSpec((ATT_BLOCK, 3 * LANES), lambda n: (cur(n), 0))]
    return specs + [pl.BlockSpec(memory_space=pltpu.SMEM)]


def _kv_band(prev_ref, cur_ref):
    out = []
    for j in range(KV_DIM // LANES):
        sl = slice(j * LANES, (j + 1) * LANES)
        band = jnp.concatenate([prev_ref[:, sl], cur_ref[:, sl]], axis=0)
        out.append((band, pltpu.roll(band, HEAD_DIM, 1)))
    return out


def _bf16(bands, transposed=False):
    return [[(a.T if transposed else a).astype(BF16) for a in pair] for pair in bands]


def _attn_fwd(qkv, sinks, carry=(None, None)):
    T = qkv.shape[0]
    nb = T // ATT_BLOCK

    def body(*refs):
        n = pl.program_id(0)
        own, finish = _carried(carry, refs, 6, 1, n == 0, n == nb - 1)
        q_ref, kp_ref, kc_ref, vp_ref, vc_ref, sink_ref, o_ref = own
        valid, low, upper = _attn_masks(n)
        ks = _bf16(_kv_band(kp_ref, kc_ref))
        vts = _bf16(_kv_band(vp_ref, vc_ref), transposed=True)
        heads, outs = {}, {}

        def first(h):
            p, hf = h // 2, h % 2
            kpair, khalf = p // 4, (p // 2) % 2
            qm = jnp.where(low if hf == 0 else ~low, q_ref[:, p * LANES:(p + 1) * LANES] * ATT_SCALE, 0.0)
            sw = 0 if khalf == hf else 1
            heads[h] = (kpair, sw, _dot_nt(ks[kpair][sw], qm.astype(BF16)))

        def second(h):
            kpair, sw, s = heads[h]
            heads[h] = (kpair, sw, _softmax_sink(s, valid, sink_ref[0, h])[0].astype(BF16))

        def third(h):
            kpair, sw, pr = heads.pop(h)
            outs[h] = _dot(vts[kpair][sw], pr)
            if h % 2:
                o_ref[:, (h // 2) * LANES:(h // 2 + 1) * LANES] = jnp.where(upper, outs.pop(h - 1), outs.pop(h)).T.astype(BF16)

        for i in range(N_Q_HEADS + 2 * HEAD_LAG):
            if i < N_Q_HEADS:
                first(i)
            if 0 <= i - HEAD_LAG < N_Q_HEADS:
                second(i - HEAD_LAG)
            if 0 <= i - 2 * HEAD_LAG < N_Q_HEADS:
                third(i - 2 * HEAD_LAG)
        finish()

    in_specs, out_specs, out_shape, scratch, extra = _carried_specs(
        carry, _attn_specs(nb, False), [pl.BlockSpec((ATT_BLOCK, Q_DIM), lambda n: (n, 0))],
        [jax.ShapeDtypeStruct((T, Q_DIM), BF16)], [])
    return pl.pallas_call(
        body, name="attn_fwd", grid=(nb,), in_specs=in_specs, out_specs=out_specs, out_shape=out_shape,
        scratch_shapes=scratch, compiler_params=_params(dimension_semantics=("arbitrary",)),
    )(qkv, qkv, qkv, qkv, qkv, sinks, *extra)


def _attn_bwd(qkv, rot, sinks, dout, carry=(None, None)):
    T = qkv.shape[0]
    nb = T // ATT_BLOCK
    npair = KV_DIM // LANES

    def body(*refs):
        n = pl.program_id(0)
        own, finish = _carried(carry, refs, 9, 2, n == 0, n == nb)
        (q_ref, kp_ref, kc_ref, vp_ref, vc_ref, tp_ref, tc_ref, sink_ref, do_ref, dqkv_ref, dsink_ref,
         dq_c, dk_c, dv_c) = own

        @pl.when(n == 0)
        def _():
            dq_c[...] = jnp.zeros_like(dq_c)
            dk_c[...] = jnp.zeros_like(dk_c)
            dv_c[...] = jnp.zeros_like(dv_c)
            dsink_ref[...] = jnp.zeros_like(dsink_ref)

        def flush(dk_prev, dv_prev, tab_ref):
            dqkv_ref[:, :Q_DIM] = dq_c[...].astype(BF16)
            dk = _rot_bwd([dk_c[:, j * LANES:(j + 1) * LANES] + dk_prev[j] for j in range(npair)], tab_ref[...])
            for j in range(npair):
                dqkv_ref[:, Q_DIM + j * LANES:Q_DIM + (j + 1) * LANES] = dk[j].astype(BF16)
                dqkv_ref[:, Q_DIM + KV_DIM + j * LANES:Q_DIM + KV_DIM + (j + 1) * LANES] = (
                    dv_c[:, j * LANES:(j + 1) * LANES] + dv_prev[j]).astype(BF16)

        @pl.when(n < nb)
        def _():
            valid, low, upper = _attn_masks(n)
            lane = lax.broadcasted_iota(jnp.int32, (1, LANES), 1)
            k_band = _kv_band(kp_ref, kc_ref)
            ks, kts = _bf16(k_band), _bf16(k_band, transposed=True)
            vs = _bf16(_kv_band(vp_ref, vc_ref))
            dk_acc = [[jnp.zeros((2 * ATT_BLOCK, LANES), F32) for _ in range(2)] for _ in range(npair)]
            dv_acc = [[jnp.zeros((2 * ATT_BLOCK, LANES), F32) for _ in range(2)] for _ in range(npair)]
            dsink = jnp.zeros((1, LANES), F32)
            heads, dq_t, dsinks = {}, {}, []

            def first(h):
                p, hf = h // 2, h % 2
                kpair, khalf = p // 4, (p // 2) % 2
                sel = low if hf == 0 else ~low
                qm = jnp.where(sel, q_ref[:, p * LANES:(p + 1) * LANES] * ATT_SCALE, 0.0).astype(BF16)
                dom = jnp.where(sel, do_ref[:, p * LANES:(p + 1) * LANES], 0.0).astype(BF16)
                sw = 0 if khalf == hf else 1
                heads[h] = dict(kpair=kpair, sw=sw, qm=qm, dom=dom, s=_dot_nt(ks[kpair][sw], qm),
                                dp=_dot_nt(vs[kpair][sw], dom))

            def second(h):
                d = heads[h]
                pr, ps = _softmax_sink(d.pop("s"), valid, sink_ref[0, h])
                dp = d.pop("dp")
                dd = jnp.sum(pr * dp, axis=0, keepdims=True)
                dsinks.append(jnp.where(lane == h, -jnp.sum(ps * dd, axis=1, keepdims=True), 0.0))
                d["ds"] = (pr * (dp - dd)).astype(BF16)
                d["pr"] = pr.astype(BF16)

            def third(h):
                d = heads.pop(h)
                kpair, sw = d["kpair"], d["sw"]
                dq_t[h] = _dot(kts[kpair][sw], d["ds"])
                dk_acc[kpair][sw] = dk_acc[kpair][sw] + _dot(d["ds"], d["qm"])
                dv_acc[kpair][sw] = dv_acc[kpair][sw] + _dot(d["pr"], d["dom"])

            for i in range(N_Q_HEADS + 2 * HEAD_LAG):
                if i < N_Q_HEADS:
                    first(i)
                if 0 <= i - HEAD_LAG < N_Q_HEADS:
                    second(i - HEAD_LAG)
                if 0 <= i - 2 * HEAD_LAG < N_Q_HEADS:
                    third(i - 2 * HEAD_LAG)
            dsink = sum(dsinks, dsink)
            dqs = [jnp.where(upper, dq_t[2 * p], dq_t[2 * p + 1]).T * ATT_SCALE for p in range(Q_DIM // LANES)]
            dk_acc = [a[0] + pltpu.roll(a[1], HEAD_DIM, 1) for a in dk_acc]
            dv_acc = [a[0] + pltpu.roll(a[1], HEAD_DIM, 1) for a in dv_acc]
            flush([a[:ATT_BLOCK] for a in dk_acc], [a[:ATT_BLOCK] for a in dv_acc], tp_ref)
            dq = _rot_bwd(dqs, tc_ref[...])
            for p in range(Q_DIM // LANES):
                dq_c[:, p * LANES:(p + 1) * LANES] = dq[p]
            for j in range(npair):
                dk_c[:, j * LANES:(j + 1) * LANES] = dk_acc[j][ATT_BLOCK:]
                dv_c[:, j * LANES:(j + 1) * LANES] = dv_acc[j][ATT_BLOCK:]
            dsink_ref[...] += dsink

        @pl.when(n == nb)
        def _():
            zero = [jnp.zeros((ATT_BLOCK, LANES), F32) for _ in range(npair)]
            flush(zero, zero, tc_ref)

        finish()

    do_spec = pl.BlockSpec((ATT_BLOCK, Q_DIM), lambda n: (jnp.minimum(n, nb - 1), 0))
    in_specs, out_specs, out_shape, scratch, extra = _carried_specs(
        carry, _attn_specs(nb, True) + [do_spec],
        [pl.BlockSpec((ATT_BLOCK, QKV_DIM), lambda n: (jnp.maximum(n - 1, 0), 0)),
         pl.BlockSpec((1, LANES), lambda n: (0, 0))],
        [jax.ShapeDtypeStruct((T, QKV_DIM), BF16), jax.ShapeDtypeStruct((1, LANES), F32)],
        [pltpu.VMEM((ATT_BLOCK, Q_DIM), F32), pltpu.VMEM((ATT_BLOCK, KV_DIM), F32),
         pltpu.VMEM((ATT_BLOCK, KV_DIM), F32)])
    return pl.pallas_call(
        body, name="attn_bwd", grid=(nb + 1,), in_specs=in_specs, out_specs=out_specs, out_shape=out_shape,
        scratch_shapes=scratch, compiler_params=_params(dimension_semantics=("arbitrary",)),
    )(qkv, qkv, qkv, qkv, qkv, rot, rot, sinks, dout, *extra)


LEVELS = (32, 16, 8, 4, 2, 1)
SUBLANES = 8
UNROLL = 16
UNROLL_BWD = 16


def _lower_bound(lb_ref):
    l0, l1 = lb_ref[0:1, :], lb_ref[1:2, :]
    mx = jnp.maximum(l0, l1)
    e0, e1 = jnp.exp(l0 - mx), jnp.exp(l1 - mx)
    return e1 / (e0 + e1)


GROUPS = CHUNK // SUBLANES


def _group_roll(x, k):
    return pltpu.roll(x.reshape(GROUPS, SUBLANES, HGRN_DK), k % SUBLANES, 1).reshape(CHUNK, HGRN_DK)


def _scan_rows(x, row, reverse):
    r8 = row & (SUBLANES - 1)
    for sh in (1, 2, 4):
        ok = (r8 < SUBLANES - sh) if reverse else (r8 >= sh)
        x = x + jnp.where(ok, _group_roll(x, -sh if reverse else sh), 0.0)
    g = x.reshape(GROUPS, SUBLANES, HGRN_DK)
    edge = 0 if reverse else SUBLANES - 1
    tot = jnp.broadcast_to(g[:, edge:edge + 1, :], g.shape)

    def shifted(a, n):
        z = jnp.zeros((n, SUBLANES, HGRN_DK), F32)
        return jnp.concatenate([a[n:], z] if reverse else [z, a[:GROUPS - n]], axis=0)

    acc = shifted(tot, 1)
    for sh in (1, 2, 4):
        acc = acc + shifted(acc, sh)
    return (g + acc).reshape(CHUNK, HGRN_DK)


def _level_masks():
    t = lax.broadcasted_iota(jnp.int32, (CHUNK, CHUNK), 0)
    s = lax.broadcasted_iota(jnp.int32, (CHUNK, CHUNK), 1)
    return [((t & h) != 0) & ((s & h) == 0) & ((t ^ s) < 2 * h) for h in LEVELS]


def _level_scales(b, forget, row):
    out = []
    for h in LEVELS[:3]:
        parts = [jnp.broadcast_to(b[j * 2 * h + h - 1:j * 2 * h + h, :], (2 * h, HGRN_DK))
                 for j in range(CHUNK // (2 * h))]
        mid = parts[0] if len(parts) == 1 else jnp.concatenate(parts, axis=0)
        out.append(jnp.exp(-jnp.abs(b - mid)))
    groups = b.reshape(GROUPS, SUBLANES, HGRN_DK)
    mid = jnp.broadcast_to(groups[:, SUBLANES // 2 - 1:SUBLANES // 2, :], groups.shape)
    e4 = jnp.exp(-jnp.abs(groups - mid)).reshape(CHUNK, HGRN_DK)
    f, r4 = forget, row & 3
    up1, dn1 = _group_roll(f, -1), _group_roll(f, 1)
    e2 = jnp.where(r4 == 0, up1, jnp.where(r4 == 1, 1.0, jnp.where(r4 == 2, f, dn1 * f)))
    e1 = jnp.where((row & 1) == 1, f, 1.0)
    return out + [e4, e2, e1]


def _hgrn_gates(zq, zf, lb):
    sq = jax.nn.sigmoid(zq)
    q = zq * sq
    sg = jax.nn.sigmoid(zf)
    forget = lb + (1.0 - lb) * sg
    return q, sq, sg, forget, 1.0 - forget, jnp.log(forget)


def _hgrn_specs(T, rb, rev):
    nr = T // rb
    ri = (lambda r: nr - 1 - r) if rev else (lambda r: r)
    return nr, ri, [
        pl.BlockSpec((rb, HGRN_DK), lambda h, r: (ri(r), h)),
        pl.BlockSpec((rb, HGRN_DK), lambda h, r: (ri(r), HGRN_HEADS + h)),
        pl.BlockSpec((rb, HGRN_DK), lambda h, r: (ri(r), 2 * HGRN_HEADS + h)),
        pl.BlockSpec((2, HGRN_DK), lambda h, r: (0, h)),
    ]


def _hgrn_fwd(z, lb_raw, rb=2048, carry=(None, None)):
    T = z.shape[0]
    rb = min(rb, T)
    ncb = rb // CHUNK
    unroll = min(UNROLL, ncb)
    assert ncb % unroll == 0
    nr, ri, in_specs = _hgrn_specs(T, rb, False)

    def body(*refs):
        hh, rr = pl.program_id(0), pl.program_id(1)
        own, finish = _carried(carry, refs, 4, 2, (hh == 0) & (rr == 0), (hh == HGRN_HEADS - 1) & (rr == nr - 1))
        zq_ref, zf_ref, zi_ref, lb_ref, o_ref, st_ref, state = own

        @pl.when(rr == 0)
        def _():
            state[...] = jnp.zeros_like(state)

        lb = _lower_bound(lb_ref)
        row = lax.broadcasted_iota(jnp.int32, (CHUNK, HGRN_DK), 0)
        masks = _level_masks()

        def operands(c):
            rows = pl.ds(pl.multiple_of(c * CHUNK, CHUNK), CHUNK)
            q, _, _, forget, k, lf = _hgrn_gates(zq_ref[rows, :], zf_ref[rows, :], lb)
            v = zi_ref[rows, :]
            b = _scan_rows(lf, row, False)
            pairs = [((q * e).astype(BF16), (k * e).astype(BF16)) for e in _level_scales(b, forget, row)]
            b_last = b[CHUNK - 1:CHUNK, :]
            return dict(c=c, rows=rows, pairs=pairs, vb=v.astype(BF16), diag=jnp.sum(q * k, axis=-1, keepdims=True) * v,
                        kd=(k * jnp.exp(b_last - b)).astype(BF16), qd=(q * jnp.exp(b)).astype(BF16),
                        decay=jnp.exp(b_last))

        def group(i, st):
            parts = [operands(i * unroll + j) for j in range(unroll)]
            for p in parts:
                sc = jnp.zeros((CHUNK, CHUNK), F32)
                for (qs, ks), mask in zip(p["pairs"], masks):
                    sc = sc + jnp.where(mask, _dot_nt(qs, ks), 0.0)
                p["sc"] = sc.astype(BF16)
            for p in parts:
                p["o"] = _dot(p["sc"], p["vb"]) + p["diag"]
                p["gain"] = _dot_tn(p["vb"], p["kd"])
            for p in parts:
                st_ref[p["c"], 0] = st
                o_ref[p["rows"], :] = p["o"] + _dot_nt(p["qd"], st.astype(BF16))
                st = st * p["decay"] + p["gain"]
            return st

        state[...] = lax.fori_loop(0, ncb // unroll, group, state[...])
        finish()

    in_specs, out_specs, out_shape, scratch, extra = _carried_specs(
        carry, in_specs,
        [pl.BlockSpec((rb, HGRN_DK), lambda h, r: (r, h)),
         pl.BlockSpec((ncb, 1, HGRN_DK, HGRN_DK), lambda h, r: (r, h, 0, 0))],
        [jax.ShapeDtypeStruct((T, D_MODEL), F32),
         jax.ShapeDtypeStruct((T // CHUNK, HGRN_HEADS, HGRN_DK, HGRN_DK), F32)],
        [pltpu.VMEM((HGRN_DK, HGRN_DK), F32)])
    return pl.pallas_call(
        body, name="hgrn_fwd", grid=(HGRN_HEADS, nr), in_specs=in_specs, out_specs=out_specs, out_shape=out_shape,
        scratch_shapes=scratch, compiler_params=_params(dimension_semantics=("arbitrary", "arbitrary")),
    )(z, z, z, lb_raw, *extra)


def _hgrn_bwd(z, lb_raw, states, do, rb=2048, carry=(None, None)):
    T = z.shape[0]
    rb = min(rb, T)
    ncb = rb // CHUNK
    unroll = min(UNROLL_BWD, ncb)
    assert ncb % unroll == 0
    nr, ri, in_specs = _hgrn_specs(T, rb, True)
    in_specs += [pl.BlockSpec((ncb, 1, HGRN_DK, HGRN_DK), lambda h, r: (ri(r), h, 0, 0)),
                 pl.BlockSpec((rb, HGRN_DK), lambda h, r: (ri(r), h))]

    def body(*refs):
        hh, rr = pl.program_id(0), pl.program_id(1)
        own, finish = _carried(carry, refs, 6, 4, (hh == 0) & (rr == 0), (hh == HGRN_HEADS - 1) & (rr == nr - 1))
        zq_ref, zf_ref, zi_ref, lb_ref, st_ref, do_ref, dq_ref, df_ref, di_ref, dlb_ref, dstate = own

        @pl.when(rr == 0)
        def _():
            dstate[...] = jnp.zeros_like(dstate)
            dlb_ref[...] = jnp.zeros_like(dlb_ref)

        lb = _lower_bound(lb_ref)
        row = lax.broadcasted_iota(jnp.int32, (CHUNK, HGRN_DK), 0)
        masks = _level_masks()

        def operands(c):
            rows = pl.ds(pl.multiple_of(c * CHUNK, CHUNK), CHUNK)
            zq = zq_ref[rows, :]
            q, sq, sg, forget, k, lf = _hgrn_gates(zq, zf_ref[rows, :], lb)
            v = zi_ref[rows, :]
            dov = do_ref[rows, :]
            b = _scan_rows(lf, row, False)
            b_last = b[CHUNK - 1:CHUNK, :]
            eb, ebb = jnp.exp(b), jnp.exp(b_last - b)
            es = _level_scales(b, forget, row)
            return dict(rows=rows, zq=zq, q=q, sq=sq, sg=sg, forget=forget, k=k, v=v, dov=dov, eb=eb, ebb=ebb,
                        e_last=jnp.exp(b_last), es=es, st=st_ref[c, 0], dob=dov.astype(BF16), vb=v.astype(BF16),
                        pairs=[((q * e).astype(BF16), (k * e).astype(BF16)) for e in es],
                        qd=(q * eb).astype(BF16), kd=(k * ebb).astype(BF16))

        def group(i, dlb):
            parts = [operands(ncb - 1 - (i * unroll + j)) for j in range(unroll)]
            for p in parts:
                p["da"] = _dot_nt(p["dob"], p["vb"])
                sc = jnp.zeros((CHUNK, CHUNK), F32)
                for (qs, ks), mask in zip(p["pairs"], masks):
                    sc = sc + jnp.where(mask, _dot_nt(qs, ks), 0.0)
                p["sc"] = sc.astype(BF16)
                p["dq_state"] = _dot(p["dob"], p["st"].astype(BF16))
                p["gain"] = _dot_tn(p["dob"], p["qd"])
            dst = dstate[...]
            for p in parts:
                p["dst"] = dst
                dst = dst * p["e_last"] + p["gain"]
            dstate[...] = dst
            for p in parts:
                dstb = p["dst"].astype(BF16)
                dk_state = p["ebb"] * _dot(p["vb"], dstb)
                dq = p["eb"] * p["dq_state"]
                dk = dk_state
                dv = _dot_nt(p["kd"], dstb) + _dot_tn(p["sc"], p["dob"])
                for e, (qs, ks), mask in zip(p["es"], p["pairs"], masks):
                    dam = jnp.where(mask, p["da"], 0.0).astype(BF16)
                    dq = dq + e * _dot(dam, ks)
                    dk = dk + e * _dot_tn(dam, qs)
                dad = jnp.sum(p["dov"] * p["v"], axis=-1, keepdims=True)
                p["dq"] = dq + dad * p["k"]
                p["dk"] = dk + dad * p["q"]
                p["dv"] = dv + jnp.sum(p["q"] * p["k"], axis=-1, keepdims=True) * p["dov"]
                p["extra"] = (p["e_last"] * jnp.sum(p["dst"] * p["st"], axis=0, keepdims=True)
                              + jnp.sum(p["k"] * dk_state, axis=0, keepdims=True))
            for p in parts:
                q, k, sq, sg, zq, rows = p["q"], p["k"], p["sq"], p["sg"], p["zq"], p["rows"]
                dlf = _scan_rows(q * p["dq"] - k * p["dk"], row, True) + p["extra"]
                dforget = dlf / p["forget"] - p["dk"]
                dq_ref[rows, :] = (p["dq"] * (sq * (1.0 + zq * (1.0 - sq)))).astype(BF16)
                df_ref[rows, :] = (dforget * (1.0 - lb) * sg * (1.0 - sg)).astype(BF16)
                di_ref[rows, :] = p["dv"].astype(BF16)
                dlb = dlb + jnp.sum(dforget * (1.0 - sg), axis=0, keepdims=True)
            return dlb

        dlb_ref[...] += lax.fori_loop(0, ncb // unroll, group, jnp.zeros((1, HGRN_DK), F32))
        finish()

    blk = pl.BlockSpec((rb, HGRN_DK), lambda h, r: (ri(r), h))
    in_specs, out_specs, out_shape, scratch, extra = _carried_specs(
        carry, in_specs, [blk, blk, blk, pl.BlockSpec((1, HGRN_DK), lambda h, r: (0, h))],
        [jax.ShapeDtypeStruct((T, D_MODEL), BF16)] * 3 + [jax.ShapeDtypeStruct((1, D_MODEL), F32)],
        [pltpu.VMEM((HGRN_DK, HGRN_DK), F32)])
    return pl.pallas_call(
        body, name="hgrn_bwd", grid=(HGRN_HEADS, nr), in_specs=in_specs, out_specs=out_specs, out_shape=out_shape,
        scratch_shapes=scratch, compiler_params=_params(dimension_semantics=("arbitrary", "arbitrary")),
    )(z, z, z, lb_raw, states, do, *extra)


MESH = pl.DeviceIdType.MESH
ANY = pl.BlockSpec(memory_space=pl.ANY)


def _place():
    return lax.axis_index("x"), lax.axis_index("y"), lax.axis_index("c")


def _sems(n):
    return [pltpu.SemaphoreType.DMA((7 * n,)), pltpu.SemaphoreType.DMA((7 * n,)), pltpu.SemaphoreType.DMA((n,))]


class _Gather:
    def __init__(self, x_ref, out_ref, send_sems, recv_sems, local_sems, idx):
        self.x_ref, self.out_ref, self.send_sems, self.recv_sems, self.local_sem, self.base = (
            x_ref, out_ref, send_sems, recv_sems, local_sems.at[idx], 7 * idx)
        x, y, c = _place()
        self.c = c
        self.me, self.sibling = (x, y, c), (x, y, 1 - c)
        self.chips = [(1 - x, y), (x, 1 - y), (1 - x, 1 - y)]

    def rows(self, px, py, pc):
        return self.out_ref.at[4 * px + 2 * py + pc]

    def copy(self, k, block, to, from_input=False):
        return pltpu.make_async_remote_copy(
            src_ref=self.x_ref if from_input else self.rows(*block), dst_ref=self.rows(*block),
            send_sem=self.send_sems.at[self.base + k], recv_sem=self.recv_sems.at[self.base + k], device_id=to,
            device_id_type=MESH)

    def first(self):
        out = [self.copy(0, self.me, self.sibling, from_input=True)]
        return out + [self.copy(1 + j, self.me, (*chip, self.c), from_input=True) for j, chip in enumerate(self.chips)]

    def start(self):
        pltpu.make_async_copy(self.x_ref, self.rows(*self.me), self.local_sem).start()
        for cp in self.first():
            cp.start()

    def finish(self):
        passed = [self.copy(4 + j, (*chip, self.c), self.sibling) for j, chip in enumerate(self.chips)]
        for j, chip in enumerate(self.chips):
            self.copy(1 + j, (*chip, self.c), self.me).wait_recv()
            passed[j].start()
        self.copy(0, self.sibling, self.me).wait_recv()
        for j, chip in enumerate(self.chips):
            self.copy(4 + j, (*chip, 1 - self.c), self.me).wait_recv()
        for cp in self.first() + passed:
            cp.wait_send()
        pltpu.make_async_copy(self.x_ref, self.rows(*self.me), self.local_sem).wait()


class _Many:
    def __init__(self, kind, in_refs, out_refs, send_sems, recv_sems, local_sems):
        self.ops = [kind(x, o, send_sems, recv_sems, local_sems, i) for i, (x, o) in enumerate(zip(in_refs, out_refs))]

    def start(self):
        for op in self.ops:
            op.start()

    def finish(self):
        for op in self.ops:
            op.finish()


def _result_shapes(kind, arrs):
    return [jax.ShapeDtypeStruct(a.shape if kind is _Exchange else (N_DEV,) + a.shape, a.dtype) for a in arrs]


def _all_gather(name, shards):
    n = len(shards)

    def body(*refs):
        g = _Many(_Gather, refs[:n], refs[n:2 * n], *refs[2 * n:])
        g.start()
        g.finish()

    return pl.pallas_call(
        body, name=name, out_shape=_result_shapes(_Gather, shards), in_specs=[ANY] * n, out_specs=[ANY] * n,
        scratch_shapes=_sems(n),
    )(*shards)


def _peers(x, y, c):
    out = []
    for k in range(1, N_DEV):
        px = 1 - x if k & 4 else x
        py = 1 - y if k & 2 else y
        pc = 1 - c if k & 1 else c
        out.append((k, (px, py, pc), 4 * px + 2 * py + pc))
    return out


class _Exchange:
    def __init__(self, g_ref, recv_ref, send_sems, recv_sems, local_sems, idx):
        x, y, c = _place()
        me = 4 * x + 2 * y + c
        self.local = pltpu.make_async_copy(g_ref.at[me], recv_ref.at[me], local_sems.at[idx])
        self.copies = [
            pltpu.make_async_remote_copy(
                src_ref=g_ref.at[pidx], dst_ref=recv_ref.at[me], send_sem=send_sems.at[7 * idx + k - 1],
                recv_sem=recv_sems.at[7 * idx + k - 1], device_id=peer, device_id_type=MESH)
            for k, peer, pidx in _peers(x, y, c)]

    def start(self):
        self.local.start()
        for cp in self.copies:
            cp.start()

    def finish(self):
        for cp in self.copies:
            cp.wait()
        self.local.wait()


def _carried(carry, refs, n_in, n_out, first, last):
    kind, arrs = carry
    if kind is None:
        return refs, lambda: None
    n = len(arrs)
    ins, rest = refs[:n_in], refs[n_in + n:]
    outs, scratch = rest[:n_out], rest[n_out + n:]
    op = _Many(kind, refs[n_in:n_in + n], rest[n_out:n_out + n], *scratch[len(scratch) - 3:])

    @pl.when(first)
    def _():
        op.start()

    def finish():
        @pl.when(last)
        def _():
            op.finish()

    return tuple(ins) + tuple(outs) + tuple(scratch[:len(scratch) - 3]), finish


def _carried_specs(carry, in_specs, out_specs, out_shape, scratch):
    kind, arrs = carry
    if kind is None:
        return in_specs, out_specs, out_shape, scratch, []
    n = len(arrs)
    return (list(in_specs) + [ANY] * n, list(out_specs) + [ANY] * n,
            list(out_shape) + _result_shapes(kind, arrs), list(scratch) + _sems(n), list(arrs))


def _adamw(w, g, m, v):
    m = ADAM_B1 * m + (1.0 - ADAM_B1) * g
    v = ADAM_B2 * v + (1.0 - ADAM_B2) * (g * g)
    m_hat = m / (1.0 - ADAM_B1 ** ADAM_STEP)
    v_hat = v / (1.0 - ADAM_B2 ** ADAM_STEP)
    delta = -ADAM_LR * (m_hat / (jnp.sqrt(v_hat) + ADAM_EPS) + ADAM_WD * w)
    return delta, m, v


def _adamw_sum(name, recvs, w, m, v):
    L, R, C = w.shape
    tm = 128 if R % 128 == 0 else 64
    assert R % tm == 0 and len(recvs) == L

    def body(*refs):
        r_refs, (w_ref, m_ref, v_ref, g_ref, d_ref, nm_ref, nv_ref) = refs[:L], refs[L:]
        for l in range(L):
            g = r_refs[l][0].astype(F32)
            for s in range(1, N_DEV):
                g = g + r_refs[l][s].astype(F32)
            g_ref[l] = g
            d_ref[l], nm_ref[l], nv_ref[l] = _adamw(w_ref[l], g, m_ref[l], v_ref[l])

    blk = pl.BlockSpec((L, tm, C), lambda i: (0, i, 0))
    return pl.pallas_call(
        body, name=name, grid=(R // tm,),
        in_specs=[pl.BlockSpec((N_DEV, tm, C), lambda i: (0, i, 0))] * L + [blk, blk, blk],
        out_specs=[blk] * 4, out_shape=[jax.ShapeDtypeStruct((L, R, C), F32)] * 4,
        compiler_params=_params(dimension_semantics=("arbitrary",)),
    )(*recvs, w, m, v)


def _small_sync(part, w, m, v):
    def body(p_ref, w_ref, m_ref, v_ref, g_ref, d_ref, nm_ref, nv_ref, gath, send_sems, recv_sems):
        x, y, c = _place()
        me = 4 * x + 2 * y + c
        gath[me] = p_ref[...]
        copies = []
        for k, peer, _ in _peers(x, y, c):
            cp = pltpu.make_async_remote_copy(
                src_ref=p_ref, dst_ref=gath.at[me], send_sem=send_sems.at[k - 1], recv_sem=recv_sems.at[k - 1],
                device_id=peer, device_id_type=MESH)
            cp.start()
            copies.append(cp)
        for cp in copies:
            cp.wait()
        g = gath[0]
        for s in range(1, N_DEV):
            g = g + gath[s]
        wv = w_ref[...]
        l0, l1 = w_ref[8:9, :], w_ref[9:10, :]
        mx = jnp.maximum(l0, l1)
        e0, e1 = jnp.exp(l0 - mx), jnp.exp(l1 - mx)
        g9 = g[9:10, :] * (e0 / (e0 + e1)) * (e1 / (e0 + e1))
        row = lax.broadcasted_iota(jnp.int32, g.shape, 0)
        g = jnp.where(row == 9, g9, jnp.where(row == 8, -g9, g))
        g_ref[...] = g
        d_ref[...], nm_ref[...], nv_ref[...] = _adamw(wv, g, m_ref[...], v_ref[...])

    vm = pl.BlockSpec(memory_space=pltpu.VMEM)
    return pl.pallas_call(
        body, name="small_params_sync", in_specs=[vm] * 4, out_specs=[vm] * 4,
        out_shape=[jax.ShapeDtypeStruct(part.shape, F32)] * 4,
        scratch_shapes=[pltpu.VMEM((N_DEV,) + part.shape, F32), pltpu.SemaphoreType.DMA((7,)),
                        pltpu.SemaphoreType.DMA((7,))],
    )(part, w, m, v)


def _shards_bf16(d, pieces):
    return [d[name][layer].astype(BF16) for name, layer in pieces]


def _gathered(arrs, pieces, out):
    for a, (name, layer) in zip(arrs, pieces):
        out[name, layer] = a if name in COL_SHARDED else a.reshape(N_DEV * a.shape[1], a.shape[2])


def _pad_row(a, width=D_MODEL):
    a = a.reshape(1, -1)
    return jnp.pad(a, ((0, 0), (0, width - a.shape[1])))


LOSS_ROW = 11


def _pack_small(d, gn_full, loss=None):
    rows = [d["mix_norm"], d["mlp_norm"], d["final_norm"].reshape(1, D_MODEL),
            _pad_row(d["attn_b_qkv"], 2 * D_MODEL).reshape(2, D_MODEL), _pad_row(d["attn_sinks"]),
            d["hgrn_lower_bounds"], gn_full.reshape(1, D_MODEL)]
    if loss is not None:
        rows.append(_pad_row(loss))
    p = jnp.concatenate(rows, axis=0)
    return jnp.pad(p, ((0, SMALL_ROWS - p.shape[0]), (0, 0)))


def _unpack_small(p, me):
    return dict(
        mix_norm=p[0:2], mlp_norm=p[2:4], final_norm=p[4],
        attn_b_qkv=p[5:7].reshape(1, 2 * D_MODEL)[:, :QKV_DIM], attn_sinks=p[7:8, :N_Q_HEADS],
        hgrn_lower_bounds=p[8:10], hgrn_g_norm=lax.dynamic_slice(p[10:11], (0, me * 128), (1, 128)))


WEIGHT_NAMES = ['mix_norm', 'mlp_norm', 'final_norm', 'attn_w_qkv', 'attn_b_qkv', 'attn_sinks', 'attn_w_o', 'hgrn_w_in',
                'hgrn_g_norm', 'hgrn_w_o', 'hgrn_lower_bounds', 'mlp_w_up', 'mlp_w_down']
SMALL_NAMES = ('mix_norm', 'mlp_norm', 'final_norm', 'attn_b_qkv', 'attn_sinks', 'hgrn_lower_bounds', 'hgrn_g_norm')


def _rotary_tables(positions):
    inv_freq = ROPE_THETA ** (-jnp.arange(0, 2 * ROT_HALF, 2, dtype=F32) / (2 * ROT_HALF))
    ang = positions.astype(F32).reshape(-1, 1) * inv_freq
    cos, sin = jnp.cos(ang), jnp.sin(ang)
    r = jnp.arange(LANES) % HEAD_DIM
    idx = r % ROT_HALF
    c = jnp.where(r < 2 * ROT_HALF, cos[:, idx], 1.0)
    sa = jnp.where((r >= ROT_HALF) & (r < 2 * ROT_HALF), sin[:, idx], 0.0)
    sb = jnp.where(r < ROT_HALF, -sin[:, idx], 0.0)
    return jnp.concatenate([c, sa, sb], axis=1)


def kernel(x, positions, mix_norm, mlp_norm, final_norm, attn_w_qkv, attn_b_qkv, attn_sinks, attn_w_o, hgrn_w_in, hgrn_g_norm, hgrn_w_o, hgrn_lower_bounds, mlp_w_up, mlp_w_down, loss_target, m_mix_norm, m_mlp_norm, m_final_norm, m_attn_w_qkv, m_attn_b_qkv, m_attn_sinks, m_attn_w_o, m_hgrn_w_in, m_hgrn_g_norm, m_hgrn_w_o, m_hgrn_lower_bounds, m_mlp_w_up, m_mlp_w_down, v_mix_norm, v_mlp_norm, v_final_norm, v_attn_w_qkv, v_attn_b_qkv, v_attn_sinks, v_attn_w_o, v_hgrn_w_in, v_hgrn_g_norm, v_hgrn_w_o, v_hgrn_lower_bounds, v_mlp_w_up, v_mlp_w_down):
    w = dict(mix_norm=mix_norm, mlp_norm=mlp_norm, final_norm=final_norm, attn_w_qkv=attn_w_qkv, attn_b_qkv=attn_b_qkv,
             attn_sinks=attn_sinks, attn_w_o=attn_w_o, hgrn_w_in=hgrn_w_in, hgrn_g_norm=hgrn_g_norm, hgrn_w_o=hgrn_w_o,
             hgrn_lower_bounds=hgrn_lower_bounds, mlp_w_up=mlp_w_up, mlp_w_down=mlp_w_down)
    m = dict(mix_norm=m_mix_norm, mlp_norm=m_mlp_norm, final_norm=m_final_norm, attn_w_qkv=m_attn_w_qkv,
             attn_b_qkv=m_attn_b_qkv, attn_sinks=m_attn_sinks, attn_w_o=m_attn_w_o, hgrn_w_in=m_hgrn_w_in,
             hgrn_g_norm=m_hgrn_g_norm, hgrn_w_o=m_hgrn_w_o, hgrn_lower_bounds=m_hgrn_lower_bounds, mlp_w_up=m_mlp_w_up,
             mlp_w_down=m_mlp_w_down)
    v = dict(mix_norm=v_mix_norm, mlp_norm=v_mlp_norm, final_norm=v_final_norm, attn_w_qkv=v_attn_w_qkv,
             attn_b_qkv=v_attn_b_qkv, attn_sinks=v_attn_sinks, attn_w_o=v_attn_w_o, hgrn_w_in=v_hgrn_w_in,
             hgrn_g_norm=v_hgrn_g_norm, hgrn_w_o=v_hgrn_w_o, hgrn_lower_bounds=v_hgrn_lower_bounds, mlp_w_up=v_mlp_w_up,
             mlp_w_down=v_mlp_w_down)
    me = 4 * lax.axis_index("x") + 2 * lax.axis_index("y") + lax.axis_index("c")

    gn = hgrn_g_norm.reshape(1, 128)
    gn_a = gn.astype(BF16)
    gn_b = (gn - gn_a.astype(F32)).astype(BF16)
    gn_c = (gn - gn_a.astype(F32) - gn_b.astype(F32)).astype(BF16)
    gn_rows = jnp.pad(jnp.concatenate([gn_a, gn_b, gn_c], axis=1), ((0, 15), (0, D_MODEL - 3 * 128)))
    full = {}
    got = _all_gather("gather_attn_weights", _shards_bf16(w, GATHER_FIRST) + [gn_rows])
    _gathered(got[:1], GATHER_FIRST, full)
    w_qkv = full["attn_w_qkv", 0].transpose(1, 0, 2).reshape(D_MODEL, QKV_DIM)
    gn_terms = got[1][:, 0, :3 * 128].astype(F32).reshape(N_DEV, 3, 128)
    gn_full = ((gn_terms[:, 0] + gn_terms[:, 1]) + gn_terms[:, 2]).reshape(1, D_MODEL)

    x0 = x[0]
    tgt = loss_target[0]
    rot = _rotary_tables(positions)
    row = lambda a: a.reshape(1, -1)

    qkv, h0 = _norm_mm("qkv_proj", x0, row(mix_norm[0]), w_qkv, attn_b_qkv, rot=rot)
    att, *got = _attn_fwd(qkv, attn_sinks, carry=(_Gather, _shards_bf16(w, GATHER_ATTN)))
    _gathered(got, GATHER_ATTN, full)
    x1 = _mm_res("attn_out_proj", att, full["attn_w_o", 0], x0)
    u0, h1, *got = _norm_mm("mlp0_up", x1, row(mlp_norm[0]), full["mlp_w_up", 0],
                            carry=(_Gather, _shards_bf16(w, GATHER_MLP0)))
    _gathered(got, GATHER_MLP0, full)
    x2, a0 = _mlp_down("mlp0_down", u0, full["mlp_w_down", 0], x1)
    z, h2 = _norm_mm("hgrn_in_proj", x2, row(mix_norm[1]), full["hgrn_w_in", 0])
    o_raw, states, *got = _hgrn_fwd(z, hgrn_lower_bounds, carry=(_Gather, _shards_bf16(w, GATHER_HGRN)))
    _gathered(got, GATHER_HGRN, full)
    x3, o2 = _hgrn_out("hgrn_out_proj", o_raw, z, gn_full, full["hgrn_w_o", 0], x2)
    u1, h3 = _norm_mm("mlp1_up", x3, row(mlp_norm[1]), full["mlp_w_up", 1])
    dx4, a1, loss_part, g_final = _mlp_down("mlp1_down_loss", u1, full["mlp_w_down", 1], x3,
                                            loss_head=(tgt, row(final_norm)))

    gw = {}
    du1, = _mlp_bwd_act("mlp1_bwd_act", dx4, u1, full["mlp_w_down", 1])
    dx3, g_mlp1 = _mm_nt_rmsbwd("mlp1_bwd_in", du1, full["mlp_w_up", 1], x3, row(mlp_norm[1]), dx4)
    gw["mlp_w_down", 1] = _mm_tn("mlp1_dw_down", a1, dx4, "rows")
    gw["mlp_w_up", 1] = _mm_tn("mlp1_dw_up", h3, du1, "cols")

    do_raw, dg, g_gn = _hgrn_out_bwd("hgrn_out_bwd", dx3, o_raw, z, full["hgrn_w_o", 0], gn_full)
    gw["hgrn_w_o", 0] = _mm_tn("hgrn_dw_o", o2, dx3, "rows")
    recvs = {}
    dzq, dzf, dzi, g_lb, *recv = _hgrn_bwd(z, hgrn_lower_bounds, states, do_raw,
                                           carry=(_Exchange, [gw[p] for p in GRADS_HGRN]))
    recvs.update(zip(GRADS_HGRN, recv))
    dz = [dzq, dzf, dzi, dg]
    dx2, g_mix1 = _mm_nt_rmsbwd("hgrn_in_bwd", dz, full["hgrn_w_in", 0], x2, row(mix_norm[1]), dx3)
    gw["hgrn_w_in", 0] = jnp.concatenate(
        [_mm_tn(f"hgrn_dw_in{j}", h2, d, "cols") for j, d in enumerate(dz)], axis=0)

    du0, = _mlp_bwd_act("mlp0_bwd_act", dx2, u0, full["mlp_w_down", 0])
    dx1, g_mlp0 = _mm_nt_rmsbwd("mlp0_bwd_in", du0, full["mlp_w_up", 0], x1, row(mlp_norm[0]), dx2)
    gw["mlp_w_down", 0], recvs["hgrn_w_in", 0] = _mm_tn(
        "mlp0_dw_down", a0, dx2, "rows", carry=(_Exchange, [gw["hgrn_w_in", 0]]))
    gw["mlp_w_up", 0], recvs["mlp_w_down", 0] = _mm_tn(
        "mlp0_dw_up", h1, du0, "cols", carry=(_Exchange, [gw["mlp_w_down", 0]]))

    datt = _mm_nt("attn_out_bwd", dx1, full["attn_w_o", 0], BF16)
    gw["attn_w_o", 0] = _mm_tn("attn_dw_o", att, dx1, "rows")
    dqkv, g_sink, *recv = _attn_bwd(qkv, rot, attn_sinks, datt, carry=(_Exchange, [gw[p] for p in GRADS_ATTN]))
    recvs.update(zip(GRADS_ATTN, recv))
    g_qkv = _mm_tn("attn_dw_qkv", h0, dqkv, bn=512)
    g_qkv = g_qkv.reshape(D_MODEL, N_DEV, QKV_DIM // N_DEV).transpose(1, 0, 2).astype(BF16)
    dx0, g_mix0, g_bqkv, recvs["attn_w_qkv", 0] = _mm_nt_rmsbwd(
        "qkv_bwd", dqkv, w_qkv, x0, row(mix_norm[0]), dx1, with_colsum=True, carry=(_Exchange, [g_qkv]))

    big = {name: _adamw_sum("adamw_" + name, [recvs[name, l] for l in range(w[name].shape[0])], w[name], m[name], v[name])
           for name in BIG_NAMES}

    zero_row = jnp.zeros((1, D_MODEL), F32)
    part = _pack_small(dict(
        mix_norm=jnp.concatenate([g_mix0, g_mix1], axis=0), mlp_norm=jnp.concatenate([g_mlp0, g_mlp1], axis=0),
        final_norm=g_final, attn_b_qkv=g_bqkv, attn_sinks=g_sink[:, :N_Q_HEADS],
        hgrn_lower_bounds=jnp.concatenate([zero_row, g_lb], axis=0)), g_gn, loss=loss_part)

    def spread(a):
        return lax.dynamic_update_slice(zero_row, a.reshape(1, 128), (0, me * 128))

    small_in = [_pack_small({n: d[n] for n in SMALL_NAMES if n != "hgrn_g_norm"}, spread(d["hgrn_g_norm"]))
                for d in (w, m, v)]
    synced = _small_sync(part, *small_in)
    small = [_unpack_small(p, me) for p in synced]

    outs = [synced[0][LOSS_ROW, 0], dx0.reshape(x.shape)]
    for kind, grp_small in enumerate(small):
        for name in WEIGHT_NAMES:
            val = grp_small[name] if name in SMALL_NAMES else big[name][kind]
            outs.append(val.reshape(w[name].shape))
    return tuple(outs)
```
